```python
import jax, jax.numpy as jnp
from jax import lax
import numpy as np

D_MODEL = 1024
BATCH = 16
SEQ = 2048
DEPTH = 2

GRID_W = 64
CTX_LEN = 256
Q_BLOCK = 128
ROPE_THETA = 10000.0
EPS = 1e-6
N_EVEN = (DEPTH + 1) // 2
N_ODD = DEPTH // 2
MIX_HALF = D_MODEL // 2

A_HEAD_DIM = 64
A_Q_HEADS = MIX_HALF // A_HEAD_DIM
A_KV_HEADS = 2
A_GROUP = A_Q_HEADS // A_KV_HEADS
B_GROUPS = 8
B_WIDTH = MIX_HALF
B_GROUP_DIM = B_WIDTH // B_GROUPS
B_CHUNK = 128
C_HEADS = 8
C_NOPE = 64
C_ROPE = 32
C_V = MIX_HALF // C_HEADS
C_Q_RANK = D_MODEL // 4
C_KV_RANK = D_MODEL // 8
D_WIDTH = MIX_HALF
D_CONV = 31
FF_DIM = 4 * D_MODEL
N_MOD = 6

EV_Q = A_Q_HEADS * A_HEAD_DIM
EV_KV = A_KV_HEADS * A_HEAD_DIM
EV_IN = EV_Q + 2 * EV_KV + 2 * B_WIDTH
OD_IN = C_Q_RANK + C_KV_RANK + C_ROPE + 2 * D_WIDTH

kernel_name = "hybrid_gqa_gmlp_mla_conformer_prefix_dit"


def rms_norm(x, g):
    xf = x.astype(jnp.float32)
    y = xf * lax.rsqrt(jnp.mean(xf * xf, axis=-1, keepdims=True) + EPS)
    return (y * g.astype(jnp.float32)).astype(x.dtype)


def layer_norm(x, g, b):
    xf = x.astype(jnp.float32)
    mu = jnp.mean(xf, axis=-1, keepdims=True)
    var = jnp.mean(jnp.square(xf - mu), axis=-1, keepdims=True)
    y = (xf - mu) * lax.rsqrt(var + EPS)
    return (y * g.astype(jnp.float32) + b.astype(jnp.float32)).astype(x.dtype)


def modulate(x, g, shift, scale):
    return rms_norm(x, g) * (1 + scale) + shift


def axial_angles(length, d_rot):
    rows = length // GRID_W
    row = jnp.broadcast_to(jnp.arange(rows)[:, None], (rows, GRID_W)).reshape(-1).astype(jnp.float32)
    col = jnp.broadcast_to(jnp.arange(GRID_W)[None, :], (rows, GRID_W)).reshape(-1).astype(jnp.float32)
    d_axis = d_rot // 2
    inv = ROPE_THETA ** (-jnp.arange(0, d_axis, 2, dtype=jnp.float32) / d_axis)
    return jnp.concatenate([row[:, None] * inv, col[:, None] * inv], axis=-1)


def apply_rope(x, ang):
    d = x.shape[-1]
    xf = x.astype(jnp.float32).reshape(x.shape[:-1] + (d // 2, 2))
    cos, sin = jnp.cos(ang), jnp.sin(ang)
    x0, x1 = xf[..., 0], xf[..., 1]
    out = jnp.stack([x0 * cos - x1 * sin, x0 * sin + x1 * cos], axis=-1)
    return out.reshape(x.shape).astype(x.dtype)


def to_heads(t, n_heads, head_dim):
    b, l, _ = t.shape
    return t.reshape(b, l, n_heads, head_dim).transpose(0, 2, 1, 3)


def from_heads(o):
    b, n, l, hd = o.shape
    return o.transpose(0, 2, 1, 3).reshape(b, l, n * hd)


def block_attention(q, k, v):
    b, hk, g, lq, dk = q.shape
    scale = dk ** -0.5
    qb = jnp.moveaxis(q.reshape(b, hk, g, lq // Q_BLOCK, Q_BLOCK, dk), 3, 0)

    def one_block(qi):
        s = jnp.einsum("bhgqd,bhkd->bhgqk", qi, k, preferred_element_type=jnp.float32) * scale
        p = jax.nn.softmax(s, axis=-1)
        return jnp.einsum("bhgqk,bhkd->bhgqd", p.astype(v.dtype), v)

    o = lax.map(one_block, qb)
    return jnp.moveaxis(o, 0, 3).reshape(b, hk, g, lq, v.shape[-1])


def spatial_gating(z, norm_g, w_s, b_s):
    b, l, _ = z.shape
    u, v = jnp.split(jax.nn.gelu(z), 2, axis=-1)
    v = rms_norm(v.reshape(b, l, B_GROUPS, B_GROUP_DIM), norm_g)
    v = v.reshape(b, l // B_CHUNK, B_CHUNK, B_GROUPS, B_GROUP_DIM)
    sv = jnp.einsum("gpq,bnqgc->bnpgc", w_s, v) + b_s.T[None, None, :, :, None]
    return u * sv.reshape(b, l, B_WIDTH)


def even_mixer(h_lat, h_ctx, need_ctx, w_in, q_norm_g, k_norm_g, sgu_norm_g, sgu_w, sgu_b):
    cuts = [EV_Q, EV_Q + EV_KV, EV_Q + 2 * EV_KV]
    w_q, w_k, w_v, w_z = jnp.split(w_in, cuts, axis=1)

    def gqa_q(qp, ang):
        b, l, _ = qp.shape
        q = rms_norm(to_heads(qp, A_Q_HEADS, A_HEAD_DIM), q_norm_g)
        if ang is not None:
            q = apply_rope(q, ang)
        return q.reshape(b, A_KV_HEADS, A_GROUP, l, A_HEAD_DIM)

    def gqa_kv(kp, vp, ang):
        k = rms_norm(to_heads(kp, A_KV_HEADS, A_HEAD_DIM), k_norm_g)
        if ang is not None:
            k = apply_rope(k, ang)
        return k, to_heads(vp, A_KV_HEADS, A_HEAD_DIM)

    def merge(o):
        b, hk, g, l, d = o.shape
        return from_heads(o.reshape(b, hk * g, l, d))

    b, l, _ = h_lat.shape
    ang = axial_angles(l, A_HEAD_DIM)
    qp, kp, vp, zp = jnp.split(h_lat @ w_in, cuts, axis=-1)
    kc, vc = gqa_kv(h_ctx @ w_k, h_ctx @ w_v, None)
    kl, vl = gqa_kv(kp, vp, ang)
    o_att = block_attention(gqa_q(qp, ang), jnp.concatenate([kc, kl], axis=2),
                            jnp.concatenate([vc, vl], axis=2))
    out_lat = jnp.concatenate([merge(o_att), spatial_gating(zp, sgu_norm_g, sgu_w, sgu_b)], axis=-1)
    out_ctx = None
    if need_ctx:
        oc = block_attention(gqa_q(h_ctx @ w_q, None), kc, vc)
        out_ctx = jnp.concatenate([merge(oc), spatial_gating(h_ctx @ w_z, sgu_norm_g, sgu_w, sgu_b)],
                                  axis=-1)
    return out_lat, out_ctx


def odd_mixer(h_lat, h_ctx, need_ctx, w_in, q_norm_g, kv_norm_g, w_uq, w_ukv, conv_w, conv_b,
              ln_g, ln_b):
    cuts = [C_Q_RANK, C_Q_RANK + C_KV_RANK, C_Q_RANK + C_KV_RANK + C_ROPE]
    w_cq, w_ckv, w_kr, w_cv = jnp.split(w_in, cuts, axis=1)

    def mla_q(cq, ang):
        b, l, _ = cq.shape
        q = to_heads(rms_norm(cq, q_norm_g) @ w_uq, C_HEADS, C_NOPE + C_ROPE)
        qn, qr = jnp.split(q, [C_NOPE], axis=-1)
        if ang is not None:
            qr = apply_rope(qr, ang)
        return jnp.concatenate([qn, qr], axis=-1)[:, :, None]

    def mla_kv(ckv, kr, ang):
        b, l, _ = ckv.shape
        kv = to_heads(rms_norm(ckv, kv_norm_g) @ w_ukv, C_HEADS, C_NOPE + C_V)
        kn, v = jnp.split(kv, [C_NOPE], axis=-1)
        kr = kr[:, None]
        if ang is not None:
            kr = apply_rope(kr, ang)
        k = jnp.concatenate([kn, jnp.broadcast_to(kr, (b, C_HEADS, l, C_ROPE))], axis=-1)
        return k, v

    def conformer(z):
        a, gt = jnp.split(z, 2, axis=-1)
        y = a * jax.nn.sigmoid(gt)
        y = lax.conv_general_dilated(y, conv_w[:, None, :], window_strides=(1,),
                                     padding=[(D_CONV // 2, D_CONV // 2)],
                                     dimension_numbers=("NWC", "WIO", "NWC"),
                                     feature_group_count=D_WIDTH) + conv_b
        return jax.nn.silu(layer_norm(y, ln_g, ln_b))

    b, l, _ = h_lat.shape
    ang = axial_angles(l, C_ROPE)
    cq, ckv, kr, zc = jnp.split(h_lat @ w_in, cuts, axis=-1)
    kc, vc = mla_kv(h_ctx @ w_ckv, h_ctx @ w_kr, None)
    kl, vl = mla_kv(ckv, kr, ang)
    o_att = block_attention(mla_q(cq, ang), jnp.concatenate([kc, kl], axis=2),
                            jnp.concatenate([vc, vl], axis=2))[:, :, 0]
    out_lat = jnp.concatenate([from_heads(o_att), conformer(zc)], axis=-1)
    out_ctx = None
    if need_ctx:
        oc = block_attention(mla_q(h_ctx @ w_cq, None), kc, vc)[:, :, 0]
        out_ctx = jnp.concatenate([from_heads(oc), conformer(h_ctx @ w_cv)], axis=-1)
    return out_lat, out_ctx


def sq_relu_mlp(h, w1, w2):
    return jnp.square(jax.nn.relu(h @ w1)) @ w2


def _fwd_setup_inputs(seed: int = 0) -> dict:
    key = jax.random.key(seed)
    ks = jax.random.split(key, 32)

    def nrm(k, shape, scale=1.0):
        return jax.random.normal(k, shape, jnp.float32) * scale

    def gain(k, shape):
        return 1.0 + 0.05 * jax.random.normal(k, shape, jnp.float32)

    D = D_MODEL
    return {
        "x": nrm(ks[0], (BATCH, SEQ, D)),
        "c": nrm(ks[1], (BATCH, D)),
        "ctx": nrm(ks[2], (BATCH, CTX_LEN, D)),
        "c_ctx": nrm(ks[3], (D,)),
        "ada_w": nrm(ks[4], (DEPTH, D, N_MOD * D), 0.5 * D ** -0.5),
        "ada_b": nrm(ks[5], (DEPTH, N_MOD * D), 0.02),
        "norm1_g": gain(ks[6], (DEPTH, D)),
        "norm2_g": gain(ks[7], (DEPTH, D)),
        "w_out": nrm(ks[8], (DEPTH, D, D), D ** -0.5),
        "mlp_w1": nrm(ks[9], (DEPTH, D, FF_DIM), D ** -0.5),
        "mlp_w2": nrm(ks[10], (DEPTH, FF_DIM, D), FF_DIM ** -0.5),
        "ev_w_in": nrm(ks[11], (N_EVEN, D, EV_IN), D ** -0.5),
        "ev_q_norm_g": gain(ks[12], (N_EVEN, A_HEAD_DIM)),
        "ev_k_norm_g": gain(ks[13], (N_EVEN, A_HEAD_DIM)),
        "ev_sgu_norm_g": gain(ks[14], (N_EVEN, B_GROUPS, B_GROUP_DIM)),
        "ev_sgu_w": nrm(ks[15], (N_EVEN, B_GROUPS, B_CHUNK, B_CHUNK), B_CHUNK ** -0.5),
        "ev_sgu_b": gain(ks[16], (N_EVEN, B_GROUPS, B_CHUNK)),
        "od_w_in": nrm(ks[17], (N_ODD, D, OD_IN), D ** -0.5),
        "od_q_norm_g": gain(ks[18], (N_ODD, C_Q_RANK)),
        "od_kv_norm_g": gain(ks[19], (N_ODD, C_KV_RANK)),
        "od_w_uq": nrm(ks[20], (N_ODD, C_Q_RANK, C_HEADS * (C_NOPE + C_ROPE)), C_Q_RANK ** -0.5),
        "od_w_ukv": nrm(ks[21], (N_ODD, C_KV_RANK, C_HEADS * (C_NOPE + C_V)), C_KV_RANK ** -0.5),
        "od_conv_w": nrm(ks[22], (N_ODD, D_CONV, D_WIDTH), D_CONV ** -0.5),
        "od_conv_b": nrm(ks[23], (N_ODD, D_WIDTH), 0.02),
        "od_ln_g": gain(ks[24], (N_ODD, D_WIDTH)),
        "od_ln_b": nrm(ks[25], (N_ODD, D_WIDTH), 0.02),
        "final_g": gain(ks[26], (D,)),
    }


def _fwd_reference(x, c, ctx, c_ctx, ada_w, ada_b, norm1_g, norm2_g, w_out, mlp_w1, mlp_w2,
              ev_w_in, ev_q_norm_g, ev_k_norm_g, ev_sgu_norm_g, ev_sgu_w, ev_sgu_b,
              od_w_in, od_q_norm_g, od_kv_norm_g, od_w_uq, od_w_ukv, od_conv_w, od_conv_b,
              od_ln_g, od_ln_b, final_g):
    x_lat, x_ctx = x, ctx
    silu_c = jax.nn.silu(c)
    silu_cc = jax.nn.silu(c_ctx)
    for i in range(DEPTH):
        last = i == DEPTH - 1
        j = i // 2
        m = jnp.split(silu_c @ ada_w[i] + ada_b[i], N_MOD, axis=-1)
        sh1, sc1, g1, sh2, sc2, g2 = [t[:, None, :] for t in m]
        sh1c, sc1c, g1c, sh2c, sc2c, g2c = jnp.split(silu_cc @ ada_w[i] + ada_b[i], N_MOD, axis=-1)

        h_lat = modulate(x_lat, norm1_g[i], sh1, sc1)
        h_ctx = modulate(x_ctx, norm1_g[i], sh1c, sc1c)
        if i % 2 == 0:
            o_lat, o_ctx = even_mixer(h_lat, h_ctx, not last, ev_w_in[j], ev_q_norm_g[j],
                                      ev_k_norm_g[j], ev_sgu_norm_g[j], ev_sgu_w[j], ev_sgu_b[j])
        else:
            o_lat, o_ctx = odd_mixer(h_lat, h_ctx, not last, od_w_in[j], od_q_norm_g[j],
                                     od_kv_norm_g[j], od_w_uq[j], od_w_ukv[j], od_conv_w[j],
                                     od_conv_b[j], od_ln_g[j], od_ln_b[j])

        x_lat = x_lat + g1 * (o_lat @ w_out[i])
        x_lat = x_lat + g2 * sq_relu_mlp(modulate(x_lat, norm2_g[i], sh2, sc2), mlp_w1[i], mlp_w2[i])
        if not last:
            x_ctx = x_ctx + g1c * (o_ctx @ w_out[i])
            x_ctx = x_ctx + g2c * sq_relu_mlp(modulate(x_ctx, norm2_g[i], sh2c, sc2c),
                                              mlp_w1[i], mlp_w2[i])
    return rms_norm(x_lat, final_g)


import jax as _jax
import jax.numpy as _jnp

TWIN_FORMAT = 'train_step'
FWD_PARAMS = ['x', 'c', 'ctx', 'c_ctx', 'ada_w', 'ada_b', 'norm1_g', 'norm2_g', 'w_out', 'mlp_w1', 'mlp_w2', 'ev_w_in', 'ev_q_norm_g', 'ev_k_norm_g', 'ev_sgu_norm_g', 'ev_sgu_w', 'ev_sgu_b', 'od_w_in', 'od_q_norm_g', 'od_kv_norm_g', 'od_w_uq', 'od_w_ukv', 'od_conv_w', 'od_conv_b', 'od_ln_g', 'od_ln_b', 'final_g']
TWIN_WEIGHTS = ['c_ctx', 'ada_w', 'ada_b', 'norm1_g', 'norm2_g', 'w_out', 'mlp_w1', 'mlp_w2', 'ev_w_in', 'ev_q_norm_g', 'ev_k_norm_g', 'ev_sgu_norm_g', 'ev_sgu_w', 'ev_sgu_b', 'od_w_in', 'od_q_norm_g', 'od_kv_norm_g', 'od_w_uq', 'od_w_ukv', 'od_conv_w', 'od_conv_b', 'od_ln_g', 'od_ln_b', 'final_g']
TWIN_DIFF_INPUT = 'x'
TWIN_INPUTS = ['x', 'c', 'ctx', 'c_ctx', 'ada_w', 'ada_b', 'norm1_g', 'norm2_g', 'w_out', 'mlp_w1', 'mlp_w2', 'ev_w_in', 'ev_q_norm_g', 'ev_k_norm_g', 'ev_sgu_norm_g', 'ev_sgu_w', 'ev_sgu_b', 'od_w_in', 'od_q_norm_g', 'od_kv_norm_g', 'od_w_uq', 'od_w_ukv', 'od_conv_w', 'od_conv_b', 'od_ln_g', 'od_ln_b', 'final_g', 'loss_target', 'm_c_ctx', 'm_ada_w', 'm_ada_b', 'm_norm1_g', 'm_norm2_g', 'm_w_out', 'm_mlp_w1', 'm_mlp_w2', 'm_ev_w_in', 'm_ev_q_norm_g', 'm_ev_k_norm_g', 'm_ev_sgu_norm_g', 'm_ev_sgu_w', 'm_ev_sgu_b', 'm_od_w_in', 'm_od_q_norm_g', 'm_od_kv_norm_g', 'm_od_w_uq', 'm_od_w_ukv', 'm_od_conv_w', 'm_od_conv_b', 'm_od_ln_g', 'm_od_ln_b', 'm_final_g', 'v_c_ctx', 'v_ada_w', 'v_ada_b', 'v_norm1_g', 'v_norm2_g', 'v_w_out', 'v_mlp_w1', 'v_mlp_w2', 'v_ev_w_in', 'v_ev_q_norm_g', 'v_ev_k_norm_g', 'v_ev_sgu_norm_g', 'v_ev_sgu_w', 'v_ev_sgu_b', 'v_od_w_in', 'v_od_q_norm_g', 'v_od_kv_norm_g', 'v_od_w_uq', 'v_od_w_ukv', 'v_od_conv_w', 'v_od_conv_b', 'v_od_ln_g', 'v_od_ln_b', 'v_final_g']
TWIN_OUTPUTS = ['loss', 'grad_x', 'grad_c_ctx', 'grad_ada_w', 'grad_ada_b', 'grad_norm1_g', 'grad_norm2_g', 'grad_w_out', 'grad_mlp_w1', 'grad_mlp_w2', 'grad_ev_w_in', 'grad_ev_q_norm_g', 'grad_ev_k_norm_g', 'grad_ev_sgu_norm_g', 'grad_ev_sgu_w', 'grad_ev_sgu_b', 'grad_od_w_in', 'grad_od_q_norm_g', 'grad_od_kv_norm_g', 'grad_od_w_uq', 'grad_od_w_ukv', 'grad_od_conv_w', 'grad_od_conv_b', 'grad_od_ln_g', 'grad_od_ln_b', 'grad_final_g', 'delta_c_ctx', 'delta_ada_w', 'delta_ada_b', 'delta_norm1_g', 'delta_norm2_g', 'delta_w_out', 'delta_mlp_w1', 'delta_mlp_w2', 'delta_ev_w_in', 'delta_ev_q_norm_g', 'delta_ev_k_norm_g', 'delta_ev_sgu_norm_g', 'delta_ev_sgu_w', 'delta_ev_sgu_b', 'delta_od_w_in', 'delta_od_q_norm_g', 'delta_od_kv_norm_g', 'delta_od_w_uq', 'delta_od_w_ukv', 'delta_od_conv_w', 'delta_od_conv_b', 'delta_od_ln_g', 'delta_od_ln_b', 'delta_final_g', 'new_m_c_ctx', 'new_m_ada_w', 'new_m_ada_b', 'new_m_norm1_g', 'new_m_norm2_g', 'new_m_w_out', 'new_m_mlp_w1', 'new_m_mlp_w2', 'new_m_ev_w_in', 'new_m_ev_q_norm_g', 'new_m_ev_k_norm_g', 'new_m_ev_sgu_norm_g', 'new_m_ev_sgu_w', 'new_m_ev_sgu_b', 'new_m_od_w_in', 'new_m_od_q_norm_g', 'new_m_od_kv_norm_g', 'new_m_od_w_uq', 'new_m_od_w_ukv', 'new_m_od_conv_w', 'new_m_od_conv_b', 'new_m_od_ln_g', 'new_m_od_ln_b', 'new_m_final_g', 'new_v_c_ctx', 'new_v_ada_w', 'new_v_ada_b', 'new_v_norm1_g', 'new_v_norm2_g', 'new_v_w_out', 'new_v_mlp_w1', 'new_v_mlp_w2', 'new_v_ev_w_in', 'new_v_ev_q_norm_g', 'new_v_ev_k_norm_g', 'new_v_ev_sgu_norm_g', 'new_v_ev_sgu_w', 'new_v_ev_sgu_b', 'new_v_od_w_in', 'new_v_od_q_norm_g', 'new_v_od_kv_norm_g', 'new_v_od_w_uq', 'new_v_od_w_ukv', 'new_v_od_conv_w', 'new_v_od_conv_b', 'new_v_od_ln_g', 'new_v_od_ln_b', 'new_v_final_g']
TWIN_LEAF_KINDS = {'loss': 'loss', 'grad_x': 'grad_x', 'grad_c_ctx': 'grad_w', 'grad_ada_w': 'grad_w', 'grad_ada_b': 'grad_w', 'grad_norm1_g': 'grad_w', 'grad_norm2_g': 'grad_w', 'grad_w_out': 'grad_w', 'grad_mlp_w1': 'grad_w', 'grad_mlp_w2': 'grad_w', 'grad_ev_w_in': 'grad_w', 'grad_ev_q_norm_g': 'grad_w', 'grad_ev_k_norm_g': 'grad_w', 'grad_ev_sgu_norm_g': 'grad_w', 'grad_ev_sgu_w': 'grad_w', 'grad_ev_sgu_b': 'grad_w', 'grad_od_w_in': 'grad_w', 'grad_od_q_norm_g': 'grad_w', 'grad_od_kv_norm_g': 'grad_w', 'grad_od_w_uq': 'grad_w', 'grad_od_w_ukv': 'grad_w', 'grad_od_conv_w': 'grad_w', 'grad_od_conv_b': 'grad_w', 'grad_od_ln_g': 'grad_w', 'grad_od_ln_b': 'grad_w', 'grad_final_g': 'grad_w', 'delta_c_ctx': 'delta_w', 'delta_ada_w': 'delta_w', 'delta_ada_b': 'delta_w', 'delta_norm1_g': 'delta_w', 'delta_norm2_g': 'delta_w', 'delta_w_out': 'delta_w', 'delta_mlp_w1': 'delta_w', 'delta_mlp_w2': 'delta_w', 'delta_ev_w_in': 'delta_w', 'delta_ev_q_norm_g': 'delta_w', 'delta_ev_k_norm_g': 'delta_w', 'delta_ev_sgu_norm_g': 'delta_w', 'delta_ev_sgu_w': 'delta_w', 'delta_ev_sgu_b': 'delta_w', 'delta_od_w_in': 'delta_w', 'delta_od_q_norm_g': 'delta_w', 'delta_od_kv_norm_g': 'delta_w', 'delta_od_w_uq': 'delta_w', 'delta_od_w_ukv': 'delta_w', 'delta_od_conv_w': 'delta_w', 'delta_od_conv_b': 'delta_w', 'delta_od_ln_g': 'delta_w', 'delta_od_ln_b': 'delta_w', 'delta_final_g': 'delta_w', 'new_m_c_ctx': 'new_m', 'new_m_ada_w': 'new_m', 'new_m_ada_b': 'new_m', 'new_m_norm1_g': 'new_m', 'new_m_norm2_g': 'new_m', 'new_m_w_out': 'new_m', 'new_m_mlp_w1': 'new_m', 'new_m_mlp_w2': 'new_m', 'new_m_ev_w_in': 'new_m', 'new_m_ev_q_norm_g': 'new_m', 'new_m_ev_k_norm_g': 'new_m', 'new_m_ev_sgu_norm_g': 'new_m', 'new_m_ev_sgu_w': 'new_m', 'new_m_ev_sgu_b': 'new_m', 'new_m_od_w_in': 'new_m', 'new_m_od_q_norm_g': 'new_m', 'new_m_od_kv_norm_g': 'new_m', 'new_m_od_w_uq': 'new_m', 'new_m_od_w_ukv': 'new_m', 'new_m_od_conv_w': 'new_m', 'new_m_od_conv_b': 'new_m', 'new_m_od_ln_g': 'new_m', 'new_m_od_ln_b': 'new_m', 'new_m_final_g': 'new_m', 'new_v_c_ctx': 'new_v', 'new_v_ada_w': 'new_v', 'new_v_ada_b': 'new_v', 'new_v_norm1_g': 'new_v', 'new_v_norm2_g': 'new_v', 'new_v_w_out': 'new_v', 'new_v_mlp_w1': 'new_v', 'new_v_mlp_w2': 'new_v', 'new_v_ev_w_in': 'new_v', 'new_v_ev_q_norm_g': 'new_v', 'new_v_ev_k_norm_g': 'new_v', 'new_v_ev_sgu_norm_g': 'new_v', 'new_v_ev_sgu_w': 'new_v', 'new_v_ev_sgu_b': 'new_v', 'new_v_od_w_in': 'new_v', 'new_v_od_q_norm_g': 'new_v', 'new_v_od_kv_norm_g': 'new_v', 'new_v_od_w_uq': 'new_v', 'new_v_od_w_ukv': 'new_v', 'new_v_od_conv_w': 'new_v', 'new_v_od_conv_b': 'new_v', 'new_v_od_ln_g': 'new_v', 'new_v_od_ln_b': 'new_v', 'new_v_final_g': 'new_v'}


def _forward(args):
    return _fwd_reference(*[args[k] for k in FWD_PARAMS])


def _output_shape():
    out = _jax.eval_shape(lambda: _forward(_fwd_setup_inputs(0)))
    return out.shape, out.dtype

N_MICROBATCH = 1
ADAM_LR = 0.001
ADAM_B1 = 0.9
ADAM_B2 = 0.999
ADAM_EPS = 1e-08
ADAM_WD = 0.01
ADAM_STEP = 10
PER_EXAMPLE_BATCH_AXIS = {'x': 0, 'c': 0, 'ctx': 0, 'loss_target': 0}
SHARED_INPUTS = []
_WEIGHT_DTYPES = {'c_ctx': _jnp.float32, 'ada_w': _jnp.float32, 'ada_b': _jnp.float32, 'norm1_g': _jnp.float32, 'norm2_g': _jnp.float32, 'w_out': _jnp.float32, 'mlp_w1': _jnp.float32, 'mlp_w2': _jnp.float32, 'ev_w_in': _jnp.float32, 'ev_q_norm_g': _jnp.float32, 'ev_k_norm_g': _jnp.float32, 'ev_sgu_norm_g': _jnp.float32, 'ev_sgu_w': _jnp.float32, 'ev_sgu_b': _jnp.float32, 'od_w_in': _jnp.float32, 'od_q_norm_g': _jnp.float32, 'od_kv_norm_g': _jnp.float32, 'od_w_uq': _jnp.float32, 'od_w_ukv': _jnp.float32, 'od_conv_w': _jnp.float32, 'od_conv_b': _jnp.float32, 'od_ln_g': _jnp.float32, 'od_ln_b': _jnp.float32, 'final_g': _jnp.float32}
MOMENT_SCALE = {'c_ctx': 9.476285e-03, 'ada_w': 1.551927e-01, 'ada_b': 2.706643e-01, 'norm1_g': 4.133888e-02, 'norm2_g': 7.866725e-02, 'w_out': 4.107039e-02, 'mlp_w1': 4.362341e-02, 'mlp_w2': 1.004638e-01, 'ev_w_in': 4.252833e-02, 'ev_q_norm_g': 1.407441e-02, 'ev_k_norm_g': 1.340479e-02, 'ev_sgu_norm_g': 3.922925e-02, 'ev_sgu_w': 2.818594e-02, 'ev_sgu_b': 2.760278e-02, 'od_w_in': 2.666306e-02, 'od_q_norm_g': 8.132647e-03, 'od_kv_norm_g': 8.663055e-02, 'od_w_uq': 4.509493e-03, 'od_w_ukv': 2.205977e-02, 'od_conv_w': 3.336374e-02, 'od_conv_b': 9.300647e-02, 'od_ln_g': 5.409900e-02, 'od_ln_b': 5.527040e-02, 'final_g': 3.251604e+01}


def _to_microbatches(a, axis):
    t = _jnp.moveaxis(a, axis, 0)
    t = t.reshape((N_MICROBATCH, t.shape[0] // N_MICROBATCH) + t.shape[1:])
    return _jnp.moveaxis(t, 1, axis + 1)


def setup_inputs(seed: int = 0) -> dict:
    inp = _fwd_setup_inputs(seed)
    key = _jax.random.fold_in(_jax.random.key(seed), 7919)
    shape, _ = _output_shape()
    out = dict(inp)
    out["loss_target"] = _jax.random.normal(_jax.random.fold_in(key, 0), shape, _jnp.float32)
    for i, name in enumerate(TWIN_WEIGHTS):
        w = inp[name].astype(_jnp.float32)
        if MOMENT_SCALE is None:
            s = _jnp.sqrt(_jnp.mean(_jnp.square(w)) + 1e-30)
        else:
            s = MOMENT_SCALE[name]
        km, kv = _jax.random.split(_jax.random.fold_in(key, i + 1))
        out[name] = w
        out["m_" + name] = s * _jax.random.normal(km, w.shape, _jnp.float32)
        out["v_" + name] = (s * s) * _jax.random.uniform(kv, w.shape, _jnp.float32, 0.5, 1.5)
    if N_MICROBATCH > 1:
        for name, axis in PER_EXAMPLE_BATCH_AXIS.items():
            out[name] = _to_microbatches(out[name], axis)
    return {'x': out['x'], 'c': out['c'], 'ctx': out['ctx'], 'c_ctx': out['c_ctx'], 'ada_w': out['ada_w'], 'ada_b': out['ada_b'], 'norm1_g': out['norm1_g'], 'norm2_g': out['norm2_g'], 'w_out': out['w_out'], 'mlp_w1': out['mlp_w1'], 'mlp_w2': out['mlp_w2'], 'ev_w_in': out['ev_w_in'], 'ev_q_norm_g': out['ev_q_norm_g'], 'ev_k_norm_g': out['ev_k_norm_g'], 'ev_sgu_norm_g': out['ev_sgu_norm_g'], 'ev_sgu_w': out['ev_sgu_w'], 'ev_sgu_b': out['ev_sgu_b'], 'od_w_in': out['od_w_in'], 'od_q_norm_g': out['od_q_norm_g'], 'od_kv_norm_g': out['od_kv_norm_g'], 'od_w_uq': out['od_w_uq'], 'od_w_ukv': out['od_w_ukv'], 'od_conv_w': out['od_conv_w'], 'od_conv_b': out['od_conv_b'], 'od_ln_g': out['od_ln_g'], 'od_ln_b': out['od_ln_b'], 'final_g': out['final_g'], 'loss_target': out['loss_target'], 'm_c_ctx': out['m_c_ctx'], 'm_ada_w': out['m_ada_w'], 'm_ada_b': out['m_ada_b'], 'm_norm1_g': out['m_norm1_g'], 'm_norm2_g': out['m_norm2_g'], 'm_w_out': out['m_w_out'], 'm_mlp_w1': out['m_mlp_w1'], 'm_mlp_w2': out['m_mlp_w2'], 'm_ev_w_in': out['m_ev_w_in'], 'm_ev_q_norm_g': out['m_ev_q_norm_g'], 'm_ev_k_norm_g': out['m_ev_k_norm_g'], 'm_ev_sgu_norm_g': out['m_ev_sgu_norm_g'], 'm_ev_sgu_w': out['m_ev_sgu_w'], 'm_ev_sgu_b': out['m_ev_sgu_b'], 'm_od_w_in': out['m_od_w_in'], 'm_od_q_norm_g': out['m_od_q_norm_g'], 'm_od_kv_norm_g': out['m_od_kv_norm_g'], 'm_od_w_uq': out['m_od_w_uq'], 'm_od_w_ukv': out['m_od_w_ukv'], 'm_od_conv_w': out['m_od_conv_w'], 'm_od_conv_b': out['m_od_conv_b'], 'm_od_ln_g': out['m_od_ln_g'], 'm_od_ln_b': out['m_od_ln_b'], 'm_final_g': out['m_final_g'], 'v_c_ctx': out['v_c_ctx'], 'v_ada_w': out['v_ada_w'], 'v_ada_b': out['v_ada_b'], 'v_norm1_g': out['v_norm1_g'], 'v_norm2_g': out['v_norm2_g'], 'v_w_out': out['v_w_out'], 'v_mlp_w1': out['v_mlp_w1'], 'v_mlp_w2': out['v_mlp_w2'], 'v_ev_w_in': out['v_ev_w_in'], 'v_ev_q_norm_g': out['v_ev_q_norm_g'], 'v_ev_k_norm_g': out['v_ev_k_norm_g'], 'v_ev_sgu_norm_g': out['v_ev_sgu_norm_g'], 'v_ev_sgu_w': out['v_ev_sgu_w'], 'v_ev_sgu_b': out['v_ev_sgu_b'], 'v_od_w_in': out['v_od_w_in'], 'v_od_q_norm_g': out['v_od_q_norm_g'], 'v_od_kv_norm_g': out['v_od_kv_norm_g'], 'v_od_w_uq': out['v_od_w_uq'], 'v_od_w_ukv': out['v_od_w_ukv'], 'v_od_conv_w': out['v_od_conv_w'], 'v_od_conv_b': out['v_od_conv_b'], 'v_od_ln_g': out['v_od_ln_g'], 'v_od_ln_b': out['v_od_ln_b'], 'v_final_g': out['v_final_g']}


def _loss(weights, diff, rest, loss_target):
    with _jax.named_scope("forward"):
        args = {**rest, TWIN_DIFF_INPUT: diff, **{k: w.astype(_WEIGHT_DTYPES[k]) for k, w in weights.items()}}
        y = _forward(args)
    with _jax.named_scope("loss_head"):
        err = _jnp.square(y.astype(_jnp.float32) - loss_target)
        return 0.5 * _jnp.sum(_jnp.mean(err, axis=-1)) if err.ndim else 0.5 * err


def _adamw(w, g, m, v):
    m = ADAM_B1 * m + (1.0 - ADAM_B1) * g
    v = ADAM_B2 * v + (1.0 - ADAM_B2) * _jnp.square(g)
    m_hat = m / (1.0 - ADAM_B1 ** ADAM_STEP)
    v_hat = v / (1.0 - ADAM_B2 ** ADAM_STEP)
    delta = -ADAM_LR * (m_hat / (_jnp.sqrt(v_hat) + ADAM_EPS) + ADAM_WD * w)
    return delta, m, v


def reference(x, c, ctx, c_ctx, ada_w, ada_b, norm1_g, norm2_g, w_out, mlp_w1, mlp_w2, ev_w_in, ev_q_norm_g, ev_k_norm_g, ev_sgu_norm_g, ev_sgu_w, ev_sgu_b, od_w_in, od_q_norm_g, od_kv_norm_g, od_w_uq, od_w_ukv, od_conv_w, od_conv_b, od_ln_g, od_ln_b, final_g, loss_target, m_c_ctx, m_ada_w, m_ada_b, m_norm1_g, m_norm2_g, m_w_out, m_mlp_w1, m_mlp_w2, m_ev_w_in, m_ev_q_norm_g, m_ev_k_norm_g, m_ev_sgu_norm_g, m_ev_sgu_w, m_ev_sgu_b, m_od_w_in, m_od_q_norm_g, m_od_kv_norm_g, m_od_w_uq, m_od_w_ukv, m_od_conv_w, m_od_conv_b, m_od_ln_g, m_od_ln_b, m_final_g, v_c_ctx, v_ada_w, v_ada_b, v_norm1_g, v_norm2_g, v_w_out, v_mlp_w1, v_mlp_w2, v_ev_w_in, v_ev_q_norm_g, v_ev_k_norm_g, v_ev_sgu_norm_g, v_ev_sgu_w, v_ev_sgu_b, v_od_w_in, v_od_q_norm_g, v_od_kv_norm_g, v_od_w_uq, v_od_w_ukv, v_od_conv_w, v_od_conv_b, v_od_ln_g, v_od_ln_b, v_final_g):
    given = dict(x=x, c=c, ctx=ctx, c_ctx=c_ctx, ada_w=ada_w, ada_b=ada_b, norm1_g=norm1_g, norm2_g=norm2_g, w_out=w_out, mlp_w1=mlp_w1, mlp_w2=mlp_w2, ev_w_in=ev_w_in, ev_q_norm_g=ev_q_norm_g, ev_k_norm_g=ev_k_norm_g, ev_sgu_norm_g=ev_sgu_norm_g, ev_sgu_w=ev_sgu_w, ev_sgu_b=ev_sgu_b, od_w_in=od_w_in, od_q_norm_g=od_q_norm_g, od_kv_norm_g=od_kv_norm_g, od_w_uq=od_w_uq, od_w_ukv=od_w_ukv, od_conv_w=od_conv_w, od_conv_b=od_conv_b, od_ln_g=od_ln_g, od_ln_b=od_ln_b, final_g=final_g, loss_target=loss_target, m_c_ctx=m_c_ctx, m_ada_w=m_ada_w, m_ada_b=m_ada_b, m_norm1_g=m_norm1_g, m_norm2_g=m_norm2_g, m_w_out=m_w_out, m_mlp_w1=m_mlp_w1, m_mlp_w2=m_mlp_w2, m_ev_w_in=m_ev_w_in, m_ev_q_norm_g=m_ev_q_norm_g, m_ev_k_norm_g=m_ev_k_norm_g, m_ev_sgu_norm_g=m_ev_sgu_norm_g, m_ev_sgu_w=m_ev_sgu_w, m_ev_sgu_b=m_ev_sgu_b, m_od_w_in=m_od_w_in, m_od_q_norm_g=m_od_q_norm_g, m_od_kv_norm_g=m_od_kv_norm_g, m_od_w_uq=m_od_w_uq, m_od_w_ukv=m_od_w_ukv, m_od_conv_w=m_od_conv_w, m_od_conv_b=m_od_conv_b, m_od_ln_g=m_od_ln_g, m_od_ln_b=m_od_ln_b, m_final_g=m_final_g, v_c_ctx=v_c_ctx, v_ada_w=v_ada_w, v_ada_b=v_ada_b, v_norm1_g=v_norm1_g, v_norm2_g=v_norm2_g, v_w_out=v_w_out, v_mlp_w1=v_mlp_w1, v_mlp_w2=v_mlp_w2, v_ev_w_in=v_ev_w_in, v_ev_q_norm_g=v_ev_q_norm_g, v_ev_k_norm_g=v_ev_k_norm_g, v_ev_sgu_norm_g=v_ev_sgu_norm_g, v_ev_sgu_w=v_ev_sgu_w, v_ev_sgu_b=v_ev_sgu_b, v_od_w_in=v_od_w_in, v_od_q_norm_g=v_od_q_norm_g, v_od_kv_norm_g=v_od_kv_norm_g, v_od_w_uq=v_od_w_uq, v_od_w_ukv=v_od_w_ukv, v_od_conv_w=v_od_conv_w, v_od_conv_b=v_od_conv_b, v_od_ln_g=v_od_ln_g, v_od_ln_b=v_od_ln_b, v_final_g=v_final_g)
    weights = {n: given[n] for n in TWIN_WEIGHTS}
    shared = {n: given[n] for n in SHARED_INPUTS}
    per_example = {n: given[n] for n in ['x', 'c', 'ctx']}
    grad_fn = _jax.value_and_grad(_loss, argnums=(0, 1))

    def one_microbatch(ex, loss_target):
        ex = dict(ex)
        diff = ex.pop(TWIN_DIFF_INPUT)
        return grad_fn(weights, diff, {**shared, **ex}, loss_target)

    if N_MICROBATCH == 1:
        loss, (grad_w, grad_x) = one_microbatch(per_example, given["loss_target"])
    else:
        def body(carry, xs):
            loss_sum, grad_sum = carry
            l_k, (gw_k, gx_k) = one_microbatch(xs[0], xs[1])
            with _jax.named_scope("update"):
                return (loss_sum + l_k, _jax.tree.map(_jnp.add, grad_sum, gw_k)), gx_k

        init = (_jnp.zeros((), _jnp.float32), _jax.tree.map(_jnp.zeros_like, weights))
        (loss, grad_w), grad_x = _jax.lax.scan(body, init, (per_example, given["loss_target"]))
    with _jax.named_scope("update"):
        delta_w, new_m, new_v = {}, {}, {}
        for n in TWIN_WEIGHTS:
            delta_w[n], new_m[n], new_v[n] = _adamw(weights[n], grad_w[n], given["m_" + n], given["v_" + n])
    return (loss, grad_x, *[grad_w[n] for n in TWIN_WEIGHTS], *[delta_w[n] for n in TWIN_WEIGHTS],
            *[new_m[n] for n in TWIN_WEIGHTS], *[new_v[n] for n in TWIN_WEIGHTS])
```

```python
import functools
import math

import jax
import jax.numpy as jnp
from jax import lax
from jax.experimental import pallas as pl
from jax.experimental.pallas import tpu as pltpu

F32, BF16 = jnp.float32, jnp.bfloat16

EPS = 1e-6
GRID_W = 64
ROPE_THETA = 10000.0
A_HEAD_DIM, A_Q_HEADS, A_KV_HEADS = 64, 8, 2
B_GROUPS, B_GROUP_DIM, B_CHUNK = 8, 64, 128
C_HEADS, C_NOPE, C_ROPE, C_V, C_Q_RANK, C_KV_RANK = 8, 64, 32, 64, 256, 128
D_CONV = 31
CONV_PAD = D_CONV // 2
N_MOD = 6
N_DEV = 8
MESH_AXES = ("x", "y", "c")

ADAM_LR, ADAM_B1, ADAM_B2, ADAM_EPS, ADAM_WD, ADAM_STEP = 0.001, 0.9, 0.999, 1e-08, 0.01, 10

VMEM_LIMIT = 56 * 1024 * 1024
PACK_COLS = 1024
ROW_ALIGN = 16


def _pcall(body, **kw):
    return pl.pallas_call(body, **kw)


def _params(sem=None):
    return pltpu.CompilerParams(dimension_semantics=sem, vmem_limit_bytes=VMEM_LIMIT)


def _pick(n, cands):
    for c in cands:
        if n % c == 0:
            return c
    return n


def _mm(a, b, mode, out_dtypes, name, epi=None, extras=()):
    if mode == "tn":
        kk, m = a.shape
    else:
        m, kk = a.shape
    n = b.shape[0] if mode == "nt" else b.shape[1]
    tm = _pick(m, (512, 256, 128))
    tn = _pick(n, (512, 256, 128))
    tk = kk if kk <= 1024 else _pick(kk, (1024, 512, 256, 128))
    nk = kk // tk
    ne, no = len(extras), len(out_dtypes)
    a_spec = pl.BlockSpec((tk, tm), lambda i, j, k: (k, i)) if mode == "tn" else pl.BlockSpec((tm, tk), lambda i, j, k: (i, k))
    b_spec = pl.BlockSpec((tn, tk), lambda i, j, k: (j, k)) if mode == "nt" else pl.BlockSpec((tk, tn), lambda i, j, k: (k, j))
    t_spec = pl.BlockSpec((tm, tn), lambda i, j, k: (i, j))
    dn = {"nn": ((1,), (0,)), "nt": ((1,), (1,)), "tn": ((0,), (0,))}[mode]

    def body(a_ref, b_ref, *rest):
        extra_refs, out_refs = rest[:ne], rest[ne:ne + no]

        def finish(acc):
            outs = (acc,) if epi is None else epi(acc, *[r[...] for r in extra_refs])
            for r, o in zip(out_refs, outs):
                r[...] = o.astype(r.dtype)

        part = lax.dot_general(a_ref[...].astype(BF16), b_ref[...].astype(BF16), (dn, ((), ())), preferred_element_type=F32)
        if nk == 1:
            finish(part)
        else:
            acc_ref = rest[-1]
            k = pl.program_id(2)

            @pl.when(k == 0)
            def _():
                acc_ref[...] = part

            @pl.when(k > 0)
            def _():
                acc_ref[...] += part

            @pl.when(k == nk - 1)
            def _():
                finish(acc_ref[...])

    outs = _pcall(
        body, name=name, grid=(m // tm, n // tn, nk),
        in_specs=[a_spec, b_spec] + [t_spec] * ne,
        out_specs=[t_spec] * no,
        out_shape=[jax.ShapeDtypeStruct((m, n), d) for d in out_dtypes],
        scratch_shapes=[pltpu.VMEM((tm, tn), F32)] if nk > 1 else [],
        compiler_params=_params(("parallel", "parallel", "arbitrary")),
    )(a, b, *extras)
    return outs


def _linear(name, out_dtype=F32):
    @jax.custom_vjp
    def op(x, w):
        return _mm(x, w, "nn", (out_dtype,), name + "_fwd")[0]

    def fwd(x, w):
        return op(x, w), (x, w)

    def bwd(res, dy):
        x, w = res
        dx = _mm(dy, w, "nt", (x.dtype,), name + "_dx")[0]
        dw = _mm(x, dy, "tn", (w.dtype,), name + "_dw")[0]
        return dx, dw

    op.defvjp(fwd, bwd)
    return op


def _relu2_epi(acc):
    return jnp.square(jnp.maximum(acc, 0.0)), acc


def _relu2_bwd_epi(acc, a):
    return (acc * (2.0 * jnp.maximum(a.astype(F32), 0.0)),)


def _mlp(name):
    @jax.custom_vjp
    def op(h, w1, w2):
        s, _ = _mm(h, w1, "nn", (BF16, BF16), name + "_up", epi=_relu2_epi)
        return _mm(s, w2, "nn", (F32,), name + "_down")[0]

    def fwd(h, w1, w2):
        s, a = _mm(h, w1, "nn", (BF16, BF16), name + "_up", epi=_relu2_epi)
        return _mm(s, w2, "nn", (F32,), name + "_down")[0], (h, w1, w2, s, a)

    def bwd(res, dy):
        h, w1, w2, s, a = res
        da = _mm(dy, w2, "nt", (BF16,), name + "_ds", epi=_relu2_bwd_epi, extras=(a,))[0]
        dw2 = _mm(s, dy, "tn", (w2.dtype,), name + "_dw2")[0]
        dw1 = _mm(h, da, "tn", (w1.dtype,), name + "_dw1")[0]
        dh = _mm(da, w1, "nt", (h.dtype,), name + "_dh")[0]
        return dh, dw1, dw2

    op.defvjp(fwd, bwd)
    return op


def _dot_exact(x, c):
    return jnp.dot(x, c, precision=lax.Precision.HIGHEST, preferred_element_type=F32)


@jax.custom_vjp
def _bdot(a, b):
    return jnp.dot(a.astype(BF16), b.astype(BF16), preferred_element_type=F32)


def _bdot_fwd(a, b):
    return _bdot(a, b), (a, b)


def _bdot_bwd(res, g):
    a, b = res
    gb = g.astype(BF16)
    da = lax.dot_general(gb, b.astype(BF16), (((1,), (1,)), ((), ())), preferred_element_type=F32)
    db = lax.dot_general(a.astype(BF16), gb, (((0,), (0,)), ((), ())), preferred_element_type=F32)
    return da, db


_bdot.defvjp(_bdot_fwd, _bdot_bwd)


def _rowwise(name, f, out_specs, tl, ctx_blocks=0):
    def seg(l, s):
        return jnp.where(l >= ctx_blocks, s - 1, 0) if s > 1 else 0

    def specs(rows, tabs, pers, glbs, consts):
        row_specs = [pl.BlockSpec((1, tl, r.shape[2]), lambda b, l: (b, l, 0)) for r in rows]
        tab_specs = [pl.BlockSpec((tl, t.shape[1]), lambda b, l: (l, 0)) for t in tabs]
        per_specs = [pl.BlockSpec((1, 1, 1, p.shape[3]), functools.partial(lambda b, l, s: (b, seg(l, s), 0, 0), s=p.shape[1])) for p in pers]
        glb_specs = [pl.BlockSpec(g.shape, functools.partial(lambda b, l, nd: (0,) * nd, nd=g.ndim)) for g in glbs]
        const_specs = [pl.BlockSpec(c.shape, functools.partial(lambda b, l, nd: (0,) * nd, nd=c.ndim)) for c in consts]
        return row_specs, tab_specs, per_specs, glb_specs, const_specs

    def load(refs_rows, refs_tabs, refs_pers, refs_glbs, refs_consts):
        return (tuple(r[0].astype(F32) for r in refs_rows), tuple(t[...] for t in refs_tabs),
                tuple(p[0, 0].astype(F32) for p in refs_pers), tuple(g[...].astype(F32) for g in refs_glbs),
                tuple(c[...] for c in refs_consts))

    def call_fwd(rows, tabs, pers, glbs, consts):
        bsz, length = rows[0].shape[:2]
        nr, nt, npp, ng, nc = len(rows), len(tabs), len(pers), len(glbs), len(consts)
        rs, ts, ps, gs, cs = specs(rows, tabs, pers, glbs, consts)

        def body(*refs):
            ins, outs = refs[:nr + nt + npp + ng + nc], refs[nr + nt + npp + ng + nc:]
            r, t, p, g, c = load(ins[:nr], ins[nr:nr + nt], ins[nr + nt:nr + nt + npp], ins[nr + nt + npp:nr + nt + npp + ng], ins[nr + nt + npp + ng:])
            for o_ref, o in zip(outs, f(r, t, p, g, c)):
                o_ref[0] = o.astype(o_ref.dtype)

        return _pcall(
            body, name=name + "_fwd", grid=(bsz, length // tl),
            in_specs=rs + ts + ps + gs + cs,
            out_specs=[pl.BlockSpec((1, tl, w), lambda b, l: (b, l, 0)) for w, _ in out_specs],
            out_shape=[jax.ShapeDtypeStruct((bsz, length, w), d) for w, d in out_specs],
            compiler_params=_params(("parallel", "parallel")),
        )(*rows, *tabs, *pers, *glbs, *consts)

    def call_bwd(rows, tabs, pers, glbs, consts, cts):
        bsz, length = rows[0].shape[:2]
        nr, nt, npp, ng, nc, no = len(rows), len(tabs), len(pers), len(glbs), len(consts), len(cts)
        rs, ts, ps, gs, cs = specs(rows, tabs, pers, glbs, consts)
        n_in = nr + nt + npp + ng + nc

        def body(*refs):
            ins, ct_refs, outs = refs[:n_in], refs[n_in:n_in + no], refs[n_in + no:]
            r, t, p, g, c = load(ins[:nr], ins[nr:nr + nt], ins[nr + nt:nr + nt + npp], ins[nr + nt + npp:nr + nt + npp + ng], ins[nr + nt + npp + ng:])
            _, vjp = jax.vjp(lambda r_, p_, g_: tuple(f(r_, t, p_, g_, c)), r, p, g)
            dr, dp, dg = vjp(tuple(ct[0].astype(F32) for ct in ct_refs))
            dr_refs, dp_refs, dg_refs = outs[:nr], outs[nr:nr + npp], outs[nr + npp:]
            for ref, d in zip(dr_refs, dr):
                ref[0] = d.astype(ref.dtype)
            b, l = pl.program_id(0), pl.program_id(1)
            first_of_segment = (l == 0) | (l == ctx_blocks)
            for ref, d in zip(dp_refs, dp):
                @pl.when(first_of_segment)
                def _(ref=ref, d=d):
                    ref[0, 0] = d

                @pl.when(jnp.logical_not(first_of_segment))
                def _(ref=ref, d=d):
                    ref[0, 0] += d
            first = (b == 0) & (l == 0)
            for ref, d in zip(dg_refs, dg):
                @pl.when(first)
                def _(ref=ref, d=d):
                    ref[...] = d

                @pl.when(jnp.logical_not(first))
                def _(ref=ref, d=d):
                    ref[...] += d

        ct_specs = [pl.BlockSpec((1, tl, w), lambda b, l: (b, l, 0)) for w, _ in out_specs]
        outs = _pcall(
            body, name=name + "_bwd", grid=(bsz, length // tl),
            in_specs=rs + ts + ps + gs + cs + ct_specs,
            out_specs=rs + ps + gs,
            out_shape=[jax.ShapeDtypeStruct(r.shape, r.dtype) for r in rows]
            + [jax.ShapeDtypeStruct(p.shape, F32) for p in pers] + [jax.ShapeDtypeStruct(g.shape, F32) for g in glbs],
            compiler_params=_params(("arbitrary", "arbitrary")),
        )(*rows, *tabs, *pers, *glbs, *consts, *cts)
        return tuple(outs[:nr]), tuple(outs[nr:nr + npp]), tuple(outs[nr + npp:])

    @jax.custom_vjp
    def op(rows, tabs, pers, glbs, consts):
        return tuple(call_fwd(rows, tabs, pers, glbs, consts))

    def fwd(rows, tabs, pers, glbs, consts):
        return op(rows, tabs, pers, glbs, consts), (rows, tabs, pers, glbs, consts)

    def bwd(res, cts):
        rows, tabs, pers, glbs, consts = res
        dr, dp, dg = call_bwd(rows, tabs, pers, glbs, consts, tuple(cts))
        dp = tuple(d.astype(p.dtype) for d, p in zip(dp, pers))
        dg = tuple(d.astype(g.dtype) for d, g in zip(dg, glbs))
        return dr, tuple(jnp.zeros_like(t) for t in tabs), dp, dg, tuple(jnp.zeros_like(c) for c in consts)

    op.defvjp(fwd, bwd)
    return op


def _rms(x, g):
    return x * lax.rsqrt(jnp.mean(x * x, axis=-1, keepdims=True) + EPS) * g


def _f_silu(r, t, p, g, c):
    return (jax.nn.silu(r[0]),)


def _f_modulate(r, t, p, g, c):
    shift, scale = p
    return (_rms(r[0], g[0]) * (1.0 + scale) + shift,)


def _f_res_modulate(r, t, p, g, c):
    x, y = r
    gate, shift, scale = p
    xn = x + gate * y
    return xn, _rms(xn, g[0]) * (1.0 + scale) + shift


def _f_res(r, t, p, g, c):
    return (r[0] + p[0] * r[1],)


def _f_headnorm_rope(r, t, p, g, c):
    x = r[0]
    cos, sin = t
    group_mean, swap = c
    xn = x * lax.rsqrt(_dot_exact(x * x, group_mean) + EPS) * g[0]
    return (xn * cos + _dot_exact(xn, swap) * sin,)


def _f_rope(r, t, p, g, c):
    x = r[0]
    cos, sin = t
    return (x * cos + _dot_exact(x, c[0]) * sin,)


def _f_rms(r, t, p, g, c):
    return (_rms(r[0], g[0]),)


def _f_sgu_pre(r, t, p, g, c):
    u = jax.nn.gelu(r[0])
    v = jax.nn.gelu(r[1])
    vn = v * lax.rsqrt(_dot_exact(v * v, c[0]) + EPS) * g[0]
    return u, vn


def _f_sgu_mix(r, t, p, g, c):
    u, vn = r
    bias = g[B_GROUPS]
    group = lax.broadcasted_iota(jnp.int32, vn.shape, 1) // B_GROUP_DIM
    sv = bias
    for k in range(B_GROUPS):
        sv = sv + jnp.where(group == k, _bdot(g[k], vn), 0.0)
    return (u * sv,)


def _f_glu(r, t, p, g, c):
    return (r[0] * jax.nn.sigmoid(r[1]),)


def _f_ln_silu(r, t, p, g, c):
    x = r[0]
    mu = jnp.mean(x, axis=-1, keepdims=True)
    var = jnp.mean(jnp.square(x - mu), axis=-1, keepdims=True)
    return (jax.nn.silu((x - mu) * lax.rsqrt(var + EPS) * g[0] + g[1]),)


def _attention(name, ctx_len):
    def tiles(q, v):
        lq = q.shape[2]
        tq = _pick(lq, (256, 128))
        if ctx_len:
            tq = math.gcd(tq, ctx_len)
        return tq, ctx_len // tq

    def probs(qq, kk, scale):
        s = lax.dot_general(qq, kk, (((1,), (1,)), ((), ())), preferred_element_type=F32) * scale
        p = jnp.exp(s - jnp.max(s, axis=-1, keepdims=True))
        return p, jnp.sum(p, axis=-1, keepdims=True)

    def by_segment(qi, cb, lk, run):
        if cb > 0:
            @pl.when(qi < cb)
            def _():
                run(ctx_len)

            @pl.when(qi >= cb)
            def _():
                run(lk)
        else:
            run(lk)

    def call_fwd(q, k, v):
        bsz, hq, lq, dk = q.shape
        _, hk, lk, dv = v.shape
        grp = hq // hk
        tq, cb = tiles(q, v)
        scale = dk ** -0.5

        def body(q_ref, k_ref, v_ref, o_ref):
            def run(nk):
                p, l = probs(q_ref[0, 0], k_ref[0, 0, :nk], scale)
                o = jnp.dot(p.astype(BF16), v_ref[0, 0, :nk], preferred_element_type=F32) / l
                o_ref[0, 0] = o.astype(o_ref.dtype)

            by_segment(pl.program_id(2), cb, lk, run)

        return _pcall(
            body, name=name + "_fwd", grid=(bsz, hq, lq // tq),
            in_specs=[pl.BlockSpec((1, 1, tq, dk), lambda b, h, i: (b, h, i, 0)),
                      pl.BlockSpec((1, 1, lk, dk), lambda b, h, i: (b, h // grp, 0, 0)),
                      pl.BlockSpec((1, 1, lk, dv), lambda b, h, i: (b, h // grp, 0, 0))],
            out_specs=pl.BlockSpec((1, 1, tq, dv), lambda b, h, i: (b, h, i, 0)),
            out_shape=jax.ShapeDtypeStruct((bsz, hq, lq, dv), BF16),
            compiler_params=_params(("parallel", "parallel", "parallel")),
        )(q, k, v)

    def call_bwd(q, k, v, do):
        bsz, hq, lq, dk = q.shape
        _, hk, lk, dv = v.shape
        grp = hq // hk
        tq, cb = tiles(q, v)
        scale = dk ** -0.5

        def body(q_ref, k_ref, v_ref, do_ref, dq_ref, dk_ref, dv_ref):
            g, qi = pl.program_id(2), pl.program_id(3)

            @pl.when((g == 0) & (qi == 0))
            def _():
                dk_ref[...] = jnp.zeros_like(dk_ref)
                dv_ref[...] = jnp.zeros_like(dv_ref)

            def run(nk):
                qq, kk, vv, dd = q_ref[0, 0], k_ref[0, 0, :nk], v_ref[0, 0, :nk], do_ref[0, 0]
                p, l = probs(qq, kk, scale)
                pn = p / l
                dp = lax.dot_general(dd, vv, (((1,), (1,)), ((), ())), preferred_element_type=F32)
                ds = (pn * (dp - jnp.sum(pn * dp, axis=-1, keepdims=True)) * scale).astype(BF16)
                dq_ref[0, 0] = jnp.dot(ds, kk, preferred_element_type=F32).astype(dq_ref.dtype)
                dk_ref[0, 0, :nk] += lax.dot_general(ds, qq, (((0,), (0,)), ((), ())), preferred_element_type=F32)
                dv_ref[0, 0, :nk] += lax.dot_general(pn.astype(BF16), dd, (((0,), (0,)), ((), ())), preferred_element_type=F32)

            by_segment(qi, cb, lk, run)

        q_spec = pl.BlockSpec((1, 1, tq, dk), lambda b, h, g, i: (b, h * grp + g, i, 0))
        k_spec = pl.BlockSpec((1, 1, lk, dk), lambda b, h, g, i: (b, h, 0, 0))
        v_spec = pl.BlockSpec((1, 1, lk, dv), lambda b, h, g, i: (b, h, 0, 0))
        do_spec = pl.BlockSpec((1, 1, tq, dv), lambda b, h, g, i: (b, h * grp + g, i, 0))
        return _pcall(
            body, name=name + "_bwd", grid=(bsz, hk, grp, lq // tq),
            in_specs=[q_spec, k_spec, v_spec, do_spec],
            out_specs=[q_spec, k_spec, v_spec],
            out_shape=[jax.ShapeDtypeStruct(q.shape, BF16), jax.ShapeDtypeStruct(k.shape, F32), jax.ShapeDtypeStruct(v.shape, F32)],
            compiler_params=_params(("parallel", "parallel", "arbitrary", "arbitrary")),
        )(q, k, v, do)

    @jax.custom_vjp
    def op(q, k, v):
        return call_fwd(q, k, v)

    def fwd(q, k, v):
        return op(q, k, v), (q, k, v)

    def bwd(res, do):
        q, k, v = res
        dq, dk, dv = call_bwd(q, k, v, do)
        return dq, dk.astype(k.dtype), dv.astype(v.dtype)

    op.defvjp(fwd, bwd)
    return op


def _conv_call(ypad, taps, name):
    bsz, lp, ch = ypad.shape
    length = lp - 2 * ROW_ALIGN
    tl = _pick(length, (256, 128))

    def body(y_ref, w_ref, o_ref):
        base = pl.multiple_of(pl.program_id(1) * tl, tl)
        win = y_ref[0, pl.ds(base, tl + 2 * ROW_ALIGN), :]
        acc = jnp.broadcast_to(w_ref[pl.ds(D_CONV, 1), :], (tl, ch))
        for k in range(D_CONV):
            acc = acc + win[k:k + tl, :] * w_ref[pl.ds(k, 1), :]
        o_ref[0] = acc

    return _pcall(
        body, name=name, grid=(bsz, length // tl),
        in_specs=[pl.BlockSpec((1, lp, ch), lambda b, l: (b, 0, 0)), pl.BlockSpec((D_CONV + 1, ch), lambda b, l: (0, 0))],
        out_specs=pl.BlockSpec((1, tl, ch), lambda b, l: (b, l, 0)),
        out_shape=jax.ShapeDtypeStruct((bsz, length, ch), F32),
        compiler_params=_params(("parallel", "parallel")),
    )(ypad, taps)


def _conv_dw_call(ypad, dout, name):
    bsz, lp, ch = ypad.shape
    length = lp - 2 * ROW_ALIGN
    tl = _pick(length, (256, 128))

    def body(y_ref, d_ref, o_ref):
        b, l = pl.program_id(0), pl.program_id(1)

        @pl.when((b == 0) & (l == 0))
        def _():
            o_ref[...] = jnp.zeros_like(o_ref)

        base = pl.multiple_of(l * tl, tl)
        win = y_ref[0, pl.ds(base, tl + 2 * ROW_ALIGN), :]
        dd = d_ref[0]
        for k in range(D_CONV):
            o_ref[pl.ds(k, 1), :] += jnp.sum(win[k:k + tl, :] * dd, axis=0, keepdims=True)
        o_ref[pl.ds(D_CONV, 1), :] += jnp.sum(dd, axis=0, keepdims=True)

    return _pcall(
        body, name=name, grid=(bsz, length // tl),
        in_specs=[pl.BlockSpec((1, lp, ch), lambda b, l: (b, 0, 0)), pl.BlockSpec((1, tl, ch), lambda b, l: (b, l, 0))],
        out_specs=pl.BlockSpec((D_CONV + 1, ch), lambda b, l: (0, 0)),
        out_shape=jax.ShapeDtypeStruct((D_CONV + 1, ch), F32),
        compiler_params=_params(("arbitrary", "arbitrary")),
    )(ypad, dout)


def _pad_rows(y):
    return jnp.pad(y, ((0, 0), (CONV_PAD, 2 * ROW_ALIGN - CONV_PAD), (0, 0)))


@jax.custom_vjp
def _dwconv(y, taps):
    return _conv_call(_pad_rows(y), taps, "conv_fwd")


def _dwconv_fwd(y, taps):
    return _dwconv(y, taps), (y, taps)


def _dwconv_bwd(res, dout):
    y, taps = res
    flipped = jnp.concatenate([taps[:D_CONV][::-1], jnp.zeros_like(taps[D_CONV:])], axis=0)
    dy = _conv_call(_pad_rows(dout), flipped, "conv_dy")
    dtaps = _conv_dw_call(_pad_rows(y), dout, "conv_dw")
    return dy, dtaps


_dwconv.defvjp(_dwconv_fwd, _dwconv_bwd)


def _loss_head(x, target, g):
    bsz, length, d = x.shape
    tl = _pick(length, (256, 128))

    def f(xb, tb, gb):
        err = _rms(xb, gb) - tb
        return 0.5 * jnp.sum(jnp.sum(err * err, axis=-1, keepdims=True), axis=0, keepdims=True) / d

    def body(x_ref, t_ref, g_ref, loss_ref, dx_ref, dg_ref):
        val, vjp = jax.vjp(lambda xb, gb: f(xb, t_ref[0], gb), x_ref[0], g_ref[...])
        dx, dg = vjp(jnp.ones((1, 1), F32))
        dx_ref[0] = dx
        first = (pl.program_id(0) == 0) & (pl.program_id(1) == 0)

        @pl.when(first)
        def _():
            loss_ref[...] = val
            dg_ref[...] = dg

        @pl.when(jnp.logical_not(first))
        def _():
            loss_ref[...] += val
            dg_ref[...] += dg

    row = pl.BlockSpec((1, tl, d), lambda b, l: (b, l, 0))
    return _pcall(
        body, name="loss_head", grid=(bsz, length // tl),
        in_specs=[row, row, pl.BlockSpec((1, d), lambda b, l: (0, 0))],
        out_specs=[pl.BlockSpec((1, 1), lambda b, l: (0, 0)), row, pl.BlockSpec((1, d), lambda b, l: (0, 0))],
        out_shape=[jax.ShapeDtypeStruct((1, 1), F32), jax.ShapeDtypeStruct(x.shape, F32), jax.ShapeDtypeStruct((1, d), F32)],
        compiler_params=_params(("arbitrary", "arbitrary")),
    )(x, target, g)


def _adamw(w, g, m, v, name):
    shape = w.shape
    cols = shape[-1]
    rows = w.size // cols
    tr = _pick(rows, (512, 256, 128))
    w2, g2, m2, v2 = (t.reshape(rows, cols) for t in (w, g, m, v))

    def body(w_ref, g_ref, m_ref, v_ref, d_ref, nm_ref, nv_ref):
        gg = g_ref[...]
        nm = ADAM_B1 * m_ref[...] + (1.0 - ADAM_B1) * gg
        nv = ADAM_B2 * v_ref[...] + (1.0 - ADAM_B2) * jnp.square(gg)
        m_hat = nm / (1.0 - ADAM_B1 ** ADAM_STEP)
        v_hat = nv / (1.0 - ADAM_B2 ** ADAM_STEP)
        d_ref[...] = -ADAM_LR * (m_hat / (jnp.sqrt(v_hat) + ADAM_EPS) + ADAM_WD * w_ref[...])
        nm_ref[...] = nm
        nv_ref[...] = nv

    spec = pl.BlockSpec((tr, cols), lambda i: (i, 0))
    outs = _pcall(
        body, name=name, grid=(rows // tr,), in_specs=[spec] * 4, out_specs=[spec] * 3,
        out_shape=[jax.ShapeDtypeStruct((rows, cols), F32)] * 3,
        compiler_params=_params(("parallel",)),
    )(w2, g2, m2, v2)
    return tuple(o.reshape(shape) for o in outs)


def _mesh_pos():
    return lax.axis_index("x"), lax.axis_index("y"), lax.axis_index("c")


_RELATIONS = [(dx, dy, dc) for dx in (0, 1) for dy in (0, 1) for dc in (0, 1)][1:]


def _peer(pos, rel):
    return tuple(jnp.where(r == 1, 1 - p, p) if r else p for p, r in zip(pos, rel))


def _block_index(pos):
    return 4 * pos[0] + 2 * pos[1] + pos[2]


_HBM = pl.BlockSpec(memory_space=pltpu.HBM)


def _all_gather(x, name):
    rows, cols = x.shape

    def body(x_ref, out_ref, send_sems, recv_sems, local_sem):
        x_, y_, c_ = _mesh_pos()
        me, sibling = (x_, y_, c_), (x_, y_, 1 - c_)
        chips = [(1 - x_, y_), (x_, 1 - y_), (1 - x_, 1 - y_)]

        def slot(pos):
            return out_ref.at[_block_index(pos)]

        def copy(k, block, to, src=None):
            return pltpu.make_async_remote_copy(
                src_ref=slot(block) if src is None else src, dst_ref=slot(block),
                send_sem=send_sems.at[k], recv_sem=recv_sems.at[k], device_id=to, device_id_type=pl.DeviceIdType.MESH)

        mine = pltpu.make_async_copy(x_ref, slot(me), local_sem)
        mine.start()
        first = [copy(0, me, sibling, src=x_ref)] + [copy(1 + j, me, (*chip, c_), src=x_ref) for j, chip in enumerate(chips)]
        for cp in first:
            cp.start()
        passed = [copy(4 + j, (*chip, c_), sibling) for j, chip in enumerate(chips)]
        for j, chip in enumerate(chips):
            copy(1 + j, (*chip, c_), me).wait_recv()
            passed[j].start()
        copy(0, sibling, me).wait_recv()
        for j, chip in enumerate(chips):
            copy(4 + j, (*chip, 1 - c_), me).wait_recv()
        for cp in first + passed:
            cp.wait_send()
        mine.wait()

    return _pcall(
        body, name=name, in_specs=[_HBM], out_specs=_HBM,
        out_shape=jax.ShapeDtypeStruct((N_DEV, rows, cols), x.dtype),
        scratch_shapes=[pltpu.SemaphoreType.DMA((7,)), pltpu.SemaphoreType.DMA((7,)), pltpu.SemaphoreType.DMA(())],
    )(x)


def _exchange_blocks(p, name):
    _, rows, cols = p.shape

    def body(p_ref, out_ref, send_sems, recv_sems, local_sem):
        me = _mesh_pos()
        mine = pltpu.make_async_copy(p_ref.at[_block_index(me)], out_ref.at[_block_index(me)], local_sem)
        mine.start()
        sends = []
        for k, rel in enumerate(_RELATIONS):
            peer = _peer(me, rel)
            sends.append(pltpu.make_async_remote_copy(
                src_ref=p_ref.at[_block_index(peer)], dst_ref=out_ref.at[_block_index(me)],
                send_sem=send_sems.at[k], recv_sem=recv_sems.at[k], device_id=peer, device_id_type=pl.DeviceIdType.MESH))
        for cp in sends:
            cp.start()
        for k, rel in enumerate(_RELATIONS):
            peer = _peer(me, rel)
            pltpu.make_async_remote_copy(
                src_ref=p_ref.at[_block_index(peer)], dst_ref=out_ref.at[_block_index(peer)],
                send_sem=send_sems.at[k], recv_sem=recv_sems.at[k], device_id=peer, device_id_type=pl.DeviceIdType.MESH).wait_recv()
        for cp in sends:
            cp.wait_send()
        mine.wait()

    return _pcall(
        body, name=name, in_specs=[_HBM], out_specs=_HBM,
        out_shape=jax.ShapeDtypeStruct(p.shape, p.dtype),
        scratch_shapes=[pltpu.SemaphoreType.DMA((7,)), pltpu.SemaphoreType.DMA((7,)), pltpu.SemaphoreType.DMA(())],
    )(p)


def _sum_blocks(p, name):
    n, rows, cols = p.shape
    tr = _pick(rows, (256, 128, 64, 32, 16, 8))

    def body(p_ref, o_ref):
        acc = p_ref[0].astype(F32)
        for s in range(1, n):
            acc = acc + p_ref[s].astype(F32)
        o_ref[...] = acc

    return _pcall(
        body, name=name, grid=(rows // tr,),
        in_specs=[pl.BlockSpec((n, tr, cols), lambda i: (0, i, 0))], out_specs=pl.BlockSpec((tr, cols), lambda i: (i, 0)),
        out_shape=jax.ShapeDtypeStruct((rows, cols), F32), compiler_params=_params(("parallel",)),
    )(p)


def _to_rows(flat, align):
    per = align * PACK_COLS
    padded = -(-flat.size // per) * per
    return jnp.pad(flat, (0, padded - flat.size)).reshape(-1, PACK_COLS)


class _Packing:
    def __init__(self, sizes, align):
        self.offsets, self.sizes, self.align = {}, dict(sizes), align
        row = 0
        for name, size in sizes:
            self.offsets[name] = row
            row += -(-size // (align * PACK_COLS)) * align
        self.rows = row

    def pack(self, pieces):
        return jnp.concatenate([_to_rows(pieces[n].reshape(-1), self.align) for n in self.sizes], axis=0)

    def pack_blocks(self, pieces):
        out = []
        for n, size in self.sizes.items():
            padded = -(-size // (self.align * PACK_COLS)) * self.align * PACK_COLS
            out.append(jnp.pad(pieces[n], ((0, 0), (0, padded - size))).reshape(pieces[n].shape[0], -1, PACK_COLS))
        return jnp.concatenate(out, axis=1)

    def piece(self, packed, name, lead=()):
        start, size = self.offsets[name], self.sizes[name]
        nrow = -(-size // (self.align * PACK_COLS)) * self.align
        sl = packed[..., start:start + nrow, :]
        return sl.reshape(lead + (nrow * PACK_COLS,))[..., :size]


_BIG = (("ada_w", 2), ("w_out", 1), ("mlp_w1", 2), ("mlp_w2", 1), ("ev_w_in", 2), ("od_w_in", 2), ("od_w_uq", 2), ("od_w_ukv", 2))
_SMALL_SHARDED = (("od_q_norm_g", 1), ("od_conv_w", 2), ("od_conv_b", 1), ("od_ln_g", 1), ("od_ln_b", 1))
_REPLICATED = ("c_ctx", "ada_b", "norm1_g", "norm2_g", "ev_q_norm_g", "ev_k_norm_g", "ev_sgu_norm_g", "ev_sgu_w", "ev_sgu_b",
               "od_kv_norm_g", "final_g")


def _unshard(blocks, axis):
    moved = jnp.moveaxis(blocks, 0, axis)
    shape = moved.shape
    return moved.reshape(shape[:axis] + (shape[axis] * shape[axis + 1],) + shape[axis + 2:])


def _shard_blocks(full, axis):
    shape = full.shape
    split = full.reshape(shape[:axis] + (N_DEV, shape[axis] // N_DEV) + shape[axis + 1:])
    return jnp.moveaxis(split, axis, 0)


def _group_mean_matrix(width, group):
    idx = jnp.arange(width) // group
    return (idx[:, None] == idx[None, :]).astype(F32) / group


def _swap_matrix(width):
    idx = jnp.arange(width)
    return ((idx[:, None] ^ 1) == idx[None, :]).astype(F32)


def _angles(length, d_rot):
    rows = length // GRID_W
    row = jnp.broadcast_to(jnp.arange(rows)[:, None], (rows, GRID_W)).reshape(-1).astype(F32)
    col = jnp.broadcast_to(jnp.arange(GRID_W)[None, :], (rows, GRID_W)).reshape(-1).astype(F32)
    d_axis = d_rot // 2
    inv = ROPE_THETA ** (-jnp.arange(0, d_axis, 2, dtype=F32) / d_axis)
    return jnp.concatenate([row[:, None] * inv, col[:, None] * inv], axis=-1)


def _rope_tables(length, d_rot, head_dim, heads, ctx_len):
    ang = _angles(length, d_rot)
    cos = jnp.repeat(jnp.cos(ang), 2, axis=1)
    sin = jnp.repeat(jnp.sin(ang), 2, axis=1) * jnp.tile(jnp.array([-1.0, 1.0], F32), d_rot // 2)
    keep = head_dim - d_rot
    cos = jnp.concatenate([jnp.ones((length, keep), F32), cos], axis=1)
    sin = jnp.concatenate([jnp.zeros((length, keep), F32), sin], axis=1)
    cos, sin = jnp.tile(cos, (1, heads)), jnp.tile(sin, (1, heads))
    cos = jnp.concatenate([jnp.ones((ctx_len, cos.shape[1]), F32), cos], axis=0)
    sin = jnp.concatenate([jnp.zeros((ctx_len, sin.shape[1]), F32), sin], axis=0)
    return cos, sin


def _to_heads(t, heads):
    b, l, w = t.shape
    return t.reshape(b, l, heads, w // heads).transpose(0, 2, 1, 3)


def _from_heads(t):
    b, h, l, d = t.shape
    return t.transpose(0, 2, 1, 3).reshape(b, l, h * d)


def _segment_params(mod, bsz):
    parts = jnp.split(mod, N_MOD, axis=-1)
    out = []
    for part in parts:
        lat = part[:bsz]
        ctx = jnp.broadcast_to(part[bsz:bsz + 1], lat.shape)
        out.append(jnp.stack([ctx, lat], axis=1)[:, :, None, :])
    return out


def _forward(xall, crows, w, bsz, length, ctx_len):
    d = xall.shape[-1]
    total = ctx_len + length
    tl = math.gcd(256, ctx_len)
    cb = ctx_len // tl

    def rowwise(name, f, out_specs, rows_per_block=tl, ctx_blocks=cb):
        return _rowwise(name, f, out_specs, rows_per_block, ctx_blocks)

    silu = _rowwise("silu", _f_silu, [(d, F32)], crows.shape[0])((crows[None],), (), (), (), ())[0][0]
    mods = []
    for i in range(2):
        mod = _linear(f"ada{i}")(silu.astype(BF16), w["ada_w"][i]) + w["ada_b"][i]
        mods.append(_segment_params(mod, bsz))

    def flat(t):
        return t.reshape(-1, t.shape[-1])

    sh1, sc1, g1, sh2, sc2, g2 = mods[0]
    (h,) = rowwise("mod0", _f_modulate, [(d, BF16)])((xall,), (), (sh1, sc1), (w["norm1_g"][0][None],), ())
    proj = _linear("ev_in")(flat(h), w["ev_w_in"][0]).reshape(bsz, total, -1)
    ev_q, ev_kv = A_Q_HEADS * A_HEAD_DIM, A_KV_HEADS * A_HEAD_DIM
    half = B_GROUPS * B_GROUP_DIM
    qp, kp, vp, zu, zv = jnp.split(proj, [ev_q, ev_q + ev_kv, ev_q + 2 * ev_kv, ev_q + 2 * ev_kv + half], axis=-1)
    cos_q, sin_q = _rope_tables(length, A_HEAD_DIM, A_HEAD_DIM, A_Q_HEADS, ctx_len)
    cos_k, sin_k = cos_q[:, :ev_kv], sin_q[:, :ev_kv]
    (q,) = rowwise("ev_q", _f_headnorm_rope, [(ev_q, BF16)])(
        (qp,), (cos_q, sin_q), (), (jnp.tile(w["ev_q_norm_g"][0], A_Q_HEADS)[None],), (_group_mean_matrix(ev_q, A_HEAD_DIM), _swap_matrix(ev_q)))
    (k,) = rowwise("ev_k", _f_headnorm_rope, [(ev_kv, BF16)])(
        (kp,), (cos_k, sin_k), (), (jnp.tile(w["ev_k_norm_g"][0], A_KV_HEADS)[None],), (_group_mean_matrix(ev_kv, A_HEAD_DIM), _swap_matrix(ev_kv)))
    o_att = _attention("gqa", ctx_len)(_to_heads(q, A_Q_HEADS), _to_heads(k, A_KV_HEADS), _to_heads(vp.astype(BF16), A_KV_HEADS))
    u, vn = rowwise("sgu_pre", _f_sgu_pre, [(half, F32), (half, BF16)])(
        (zu, zv), (), (), (w["ev_sgu_norm_g"][0].reshape(1, half),), (_group_mean_matrix(half, B_GROUP_DIM),))
    bias = jnp.repeat(w["ev_sgu_b"][0].T, B_GROUP_DIM, axis=1)
    (o_sgu,) = rowwise("sgu_mix", _f_sgu_mix, [(half, BF16)], rows_per_block=B_CHUNK, ctx_blocks=0)(
        (u, vn), (), (), tuple(w["ev_sgu_w"][0][g] for g in range(B_GROUPS)) + (bias,), ())
    o = jnp.concatenate([_from_heads(o_att), o_sgu], axis=-1)
    y = _linear("out0")(flat(o), w["w_out"][0]).reshape(bsz, total, d)
    x1, h = rowwise("res_mod0a", _f_res_modulate, [(d, F32), (d, BF16)])((xall, y), (), (g1, sh2, sc2), (w["norm2_g"][0][None],), ())
    y = _mlp("mlp0")(flat(h), w["mlp_w1"][0], w["mlp_w2"][0]).reshape(bsz, total, d)
    sh1, sc1, g1n, sh2n, sc2n, g2n = mods[1]
    x2, h = rowwise("res_mod0b", _f_res_modulate, [(d, F32), (d, BF16)])((x1, y), (), (g2, sh1, sc1), (w["norm1_g"][1][None],), ())

    proj = _linear("od_in")(flat(h), w["od_w_in"][0]).reshape(bsz, total, -1)
    cq, ckv, kr, za, zg = jnp.split(proj, [C_Q_RANK, C_Q_RANK + C_KV_RANK, C_Q_RANK + C_KV_RANK + C_ROPE,
                                          C_Q_RANK + C_KV_RANK + C_ROPE + half], axis=-1)
    lat = slice(ctx_len, total)
    lat_tl = math.gcd(256, length)
    (cqn,) = _rowwise("od_qn", _f_rms, [(C_Q_RANK, BF16)], lat_tl)((cq[:, lat],), (), (), (w["od_q_norm_g"],), ())
    c_qk = C_NOPE + C_ROPE
    qf = _linear("od_uq")(flat(cqn), w["od_w_uq"][0]).reshape(bsz, length, C_HEADS * c_qk)
    cos_q, sin_q = _rope_tables(length, C_ROPE, c_qk, C_HEADS, 0)
    (q,) = _rowwise("od_qrope", _f_rope, [(C_HEADS * c_qk, BF16)], lat_tl)((qf,), (cos_q, sin_q), (), (), (_swap_matrix(C_HEADS * c_qk),))
    (ckvn,) = rowwise("od_kvn", _f_rms, [(C_KV_RANK, BF16)])((ckv,), (), (), (w["od_kv_norm_g"],), ())
    kv = _linear("od_ukv")(flat(ckvn), w["od_w_ukv"][0]).reshape(bsz, total, C_HEADS, C_NOPE + C_V)
    cos_r, sin_r = _rope_tables(length, C_ROPE, C_ROPE, 1, ctx_len)
    (krr,) = rowwise("od_krope", _f_rope, [(C_ROPE, BF16)])((kr,), (cos_r, sin_r), (), (), (_swap_matrix(C_ROPE),))
    kn = kv[..., :C_NOPE].astype(BF16).transpose(0, 2, 1, 3)
    vv = kv[..., C_NOPE:].astype(BF16).transpose(0, 2, 1, 3)
    kfull = jnp.concatenate([kn, jnp.broadcast_to(krr[:, None], (bsz, C_HEADS, total, C_ROPE))], axis=-1)
    o_att = _attention("mla", 0)(_to_heads(q, C_HEADS), kfull, vv)
    (glu,) = _rowwise("glu", _f_glu, [(half, F32)], lat_tl)((za[:, lat], zg[:, lat]), (), (), (), ())
    taps = jnp.concatenate([w["od_conv_w"][0], w["od_conv_b"]], axis=0)
    conv = _dwconv(glu, taps)
    (o_conv,) = _rowwise("ln_silu", _f_ln_silu, [(half, BF16)], lat_tl)((conv,), (), (), (w["od_ln_g"], w["od_ln_b"]), ())
    o = jnp.concatenate([_from_heads(o_att), o_conv], axis=-1)
    y = _linear("out1")(flat(o), w["w_out"][1]).reshape(bsz, length, d)
    lat_param = lambda p: p[:, 1:]
    x3, h = _rowwise("res_mod1a", _f_res_modulate, [(d, F32), (d, BF16)], lat_tl)(
        (x2[:, lat], y), (), (lat_param(g1n), lat_param(sh2n), lat_param(sc2n)), (w["norm2_g"][1][None],), ())
    y = _mlp("mlp1")(flat(h), w["mlp_w1"][1], w["mlp_w2"][1]).reshape(bsz, length, d)
    (x4,) = _rowwise("res1b", _f_res, [(d, F32)], lat_tl)((x3, y), (), (lat_param(g2n),), (), ())
    return x4


def kernel(x, c, ctx, c_ctx, ada_w, ada_b, norm1_g, norm2_g, w_out, mlp_w1, mlp_w2, ev_w_in, ev_q_norm_g, ev_k_norm_g, ev_sgu_norm_g, ev_sgu_w, ev_sgu_b, od_w_in, od_q_norm_g, od_kv_norm_g, od_w_uq, od_w_ukv, od_conv_w, od_conv_b, od_ln_g, od_ln_b, final_g, loss_target, m_c_ctx, m_ada_w, m_ada_b, m_norm1_g, m_norm2_g, m_w_out, m_mlp_w1, m_mlp_w2, m_ev_w_in, m_ev_q_norm_g, m_ev_k_norm_g, m_ev_sgu_norm_g, m_ev_sgu_w, m_ev_sgu_b, m_od_w_in, m_od_q_norm_g, m_od_kv_norm_g, m_od_w_uq, m_od_w_ukv, m_od_conv_w, m_od_conv_b, m_od_ln_g, m_od_ln_b, m_final_g, v_c_ctx, v_ada_w, v_ada_b, v_norm1_g, v_norm2_g, v_w_out, v_mlp_w1, v_mlp_w2, v_ev_w_in, v_ev_q_norm_g, v_ev_k_norm_g, v_ev_sgu_norm_g, v_ev_sgu_w, v_ev_sgu_b, v_od_w_in, v_od_q_norm_g, v_od_kv_norm_g, v_od_w_uq, v_od_w_ukv, v_od_conv_w, v_od_conv_b, v_od_ln_g, v_od_ln_b, v_final_g):
    names = ["c_ctx", "ada_w", "ada_b", "norm1_g", "norm2_g", "w_out", "mlp_w1", "mlp_w2", "ev_w_in", "ev_q_norm_g", "ev_k_norm_g",
             "ev_sgu_norm_g", "ev_sgu_w", "ev_sgu_b", "od_w_in", "od_q_norm_g", "od_kv_norm_g", "od_w_uq", "od_w_ukv", "od_conv_w",
             "od_conv_b", "od_ln_g", "od_ln_b", "final_g"]
    local = dict(zip(names, [c_ctx, ada_w, ada_b, norm1_g, norm2_g, w_out, mlp_w1, mlp_w2, ev_w_in, ev_q_norm_g, ev_k_norm_g, ev_sgu_norm_g, ev_sgu_w, ev_sgu_b, od_w_in, od_q_norm_g, od_kv_norm_g, od_w_uq, od_w_ukv, od_conv_w, od_conv_b, od_ln_g, od_ln_b, final_g]))
    mom1 = dict(zip(names, [m_c_ctx, m_ada_w, m_ada_b, m_norm1_g, m_norm2_g, m_w_out, m_mlp_w1, m_mlp_w2, m_ev_w_in, m_ev_q_norm_g, m_ev_k_norm_g, m_ev_sgu_norm_g, m_ev_sgu_w, m_ev_sgu_b, m_od_w_in, m_od_q_norm_g, m_od_kv_norm_g, m_od_w_uq, m_od_w_ukv, m_od_conv_w, m_od_conv_b, m_od_ln_g, m_od_ln_b, m_final_g]))
    mom2 = dict(zip(names, [v_c_ctx, v_ada_w, v_ada_b, v_norm1_g, v_norm2_g, v_w_out, v_mlp_w1, v_mlp_w2, v_ev_w_in, v_ev_q_norm_g, v_ev_k_norm_g, v_ev_sgu_norm_g, v_ev_sgu_w, v_ev_sgu_b, v_od_w_in, v_od_q_norm_g, v_od_kv_norm_g, v_od_w_uq, v_od_w_ukv, v_od_conv_w, v_od_conv_b, v_od_ln_g, v_od_ln_b, v_final_g]))
    bsz, length, d = x.shape
    ctx_len = ctx.shape[1]
    me = _block_index(_mesh_pos())

    big_pack = _Packing([(n, local[n].size) for n, _ in _BIG], ROW_ALIGN)
    gathered = _all_gather(big_pack.pack({n: local[n].astype(BF16) for n, _ in _BIG}), "gather_weights")
    tiny_pack = _Packing([(n, local[n].size) for n, _ in _SMALL_SHARDED], 8)
    gathered_tiny = _all_gather(tiny_pack.pack({n: local[n] for n, _ in _SMALL_SHARDED}), "gather_small_weights")
    full = {n: local[n] for n in _REPLICATED}
    for n, axis in _BIG:
        full[n] = _unshard(big_pack.piece(gathered, n, (N_DEV,)).reshape((N_DEV,) + local[n].shape), axis)
    for n, axis in _SMALL_SHARDED:
        full[n] = _unshard(tiny_pack.piece(gathered_tiny, n, (N_DEV,)).reshape((N_DEV,) + local[n].shape), axis)

    xall = jnp.concatenate([ctx, x], axis=1)
    diff = {n: full[n] for n in names if n not in ("c_ctx", "final_g")}

    def run(xall_, c_ctx_, weights):
        crows_ = jnp.concatenate([c, c_ctx_[None], jnp.zeros((ROW_ALIGN - bsz - 1, d), F32)], axis=0)
        return _forward(xall_, crows_, weights, bsz, length, ctx_len)

    x4, pullback = jax.vjp(run, xall, c_ctx, diff)
    loss_part, dx4, dfinal = _loss_head(x4, loss_target, final_g[None])
    dxall, dc_ctx, grads = pullback(dx4)
    grads = dict(grads)
    grads["c_ctx"] = dc_ctx
    grads["final_g"] = dfinal[0]
    loss = lax.psum(loss_part[0, 0], MESH_AXES)
    grad_x = dxall[:, ctx_len:]

    packed = big_pack.pack_blocks({n: _shard_blocks(grads[n], axis).reshape(N_DEV, -1) for n, axis in _BIG})
    summed = _sum_blocks(_exchange_blocks(packed, "exchange_grads"), "sum_grads")
    reduced = {n: big_pack.piece(summed, n).reshape(local[n].shape) for n, _ in _BIG}

    small_names = list(_REPLICATED) + [n for n, _ in _SMALL_SHARDED]
    small_pack = _Packing([(n, full[n].size) for n in small_names], 8)
    small_sum = _sum_blocks(_all_gather(small_pack.pack({n: grads[n].astype(F32) for n in small_names}), "gather_small_grads"), "sum_small_grads")
    for n in _REPLICATED:
        reduced[n] = small_pack.piece(small_sum, n).reshape(local[n].shape)
    for n, axis in _SMALL_SHARDED:
        whole = small_pack.piece(small_sum, n).reshape(full[n].shape)
        reduced[n] = lax.dynamic_slice_in_dim(whole, me * local[n].shape[axis], local[n].shape[axis], axis=axis)

    delta, new_m, new_v = {}, {}, {}
    for n in names:
        delta[n], new_m[n], new_v[n] = _adamw(local[n], reduced[n], mom1[n], mom2[n], "adamw_" + n)
    return (loss, grad_x, *[reduced[n] for n in names], *[delta[n] for n in names], *[new_m[n] for n in names], *[new_v[n] for n in names])
```

```python
import functools
import math

import jax
import jax.numpy as jnp
from jax import lax
from jax.experimental import pallas as pl
from jax.experimental.pallas import tpu as pltpu

F32, BF16 = jnp.float32, jnp.bfloat16

EPS = 1e-6
GRID_W = 64
ROPE_THETA = 10000.0
A_HEAD_DIM, A_Q_HEADS, A_KV_HEADS = 64, 8, 2
B_GROUPS, B_GROUP_DIM, B_CHUNK = 8, 64, 128
C_HEADS, C_NOPE, C_ROPE, C_V, C_Q_RANK, C_KV_RANK = 8, 64, 32, 64, 256, 128
D_CONV = 31
CONV_PAD = D_CONV // 2
N_MOD = 6
N_DEV = 8
MESH_AXES = ("x", "y", "c")

ADAM_LR, ADAM_B1, ADAM_B2, ADAM_EPS, ADAM_WD, ADAM_STEP = 0.001, 0.9, 0.999, 1e-08, 0.01, 10

VMEM_LIMIT = 56 * 1024 * 1024
PACK_COLS = 1024
ROW_ALIGN = 16
SUM_ROWS = 256
COND_ROWS = 8


def _pcall(body, **kw):
    return pl.pallas_call(body, **kw)


def _params(sem=None):
    return pltpu.CompilerParams(dimension_semantics=sem, vmem_limit_bytes=VMEM_LIMIT)


def _pick(n, cands):
    for c in cands:
        if n % c == 0:
            return c
    return n


def _mm(a, b, mode, out_dtypes, name, epi=None, extras=()):
    if mode == "tn":
        kk, m = a.shape
    else:
        m, kk = a.shape
    n = b.shape[0] if mode == "nt" else b.shape[1]
    tm = _pick(m, (1152, 1024, 768, 512, 256, 128))
    tn = _pick(n, (1024, 896, 768, 512, 256, 128))
    tk = kk if kk <= 1024 else _pick(kk, (1024, 896, 768, 512, 256, 128))
    nk = kk // tk
    ne, no = len(extras), len(out_dtypes)
    a_spec = pl.BlockSpec((tk, tm), lambda i, j, k: (k, i)) if mode == "tn" else pl.BlockSpec((tm, tk), lambda i, j, k: (i, k))
    b_spec = pl.BlockSpec((tn, tk), lambda i, j, k: (j, k)) if mode == "nt" else pl.BlockSpec((tk, tn), lambda i, j, k: (k, j))
    t_spec = pl.BlockSpec((tm, tn), lambda i, j, k: (i, j))
    dn = {"nn": ((1,), (0,)), "nt": ((1,), (1,)), "tn": ((0,), (0,))}[mode]

    def body(a_ref, b_ref, *rest):
        extra_refs, out_refs = rest[:ne], rest[ne:ne + no]

        def finish(acc):
            outs = (acc,) if epi is None else epi(acc, *[r[...] for r in extra_refs])
            for r, o in zip(out_refs, outs):
                r[...] = o.astype(r.dtype)

        part = lax.dot_general(a_ref[...].astype(BF16), b_ref[...].astype(BF16), (dn, ((), ())), preferred_element_type=F32)
        if nk == 1:
            finish(part)
        else:
            acc_ref = rest[-1]
            k = pl.program_id(2)

            @pl.when(k == 0)
            def _():
                acc_ref[...] = part

            @pl.when(k > 0)
            def _():
                acc_ref[...] += part

            @pl.when(k == nk - 1)
            def _():
                finish(acc_ref[...])

    outs = _pcall(
        body, name=name, grid=(m // tm, n // tn, nk),
        in_specs=[a_spec, b_spec] + [t_spec] * ne,
        out_specs=[t_spec] * no,
        out_shape=[jax.ShapeDtypeStruct((m, n), d) for d in out_dtypes],
        scratch_shapes=[pltpu.VMEM((tm, tn), F32)] if nk > 1 else [],
        compiler_params=_params(("parallel", "parallel", "arbitrary")),
    )(a, b, *extras)
    return outs


def _linear(name, out_dtype=F32):
    @jax.custom_vjp
    def op(x, w):
        return _mm(x, w, "nn", (out_dtype,), name + "_fwd")[0]

    def fwd(x, w):
        return op(x, w), (x, w)

    def bwd(res, dy):
        x, w = res
        dx = _mm(dy, w, "nt", (x.dtype,), name + "_dx")[0]
        dw = _mm(x, dy, "tn", (w.dtype,), name + "_dw")[0]
        return dx, dw

    op.defvjp(fwd, bwd)
    return op


def _relu2_epi(acc):
    return jnp.square(jnp.maximum(acc, 0.0)), acc


def _relu2_bwd_epi(acc, a):
    return (acc * (2.0 * jnp.maximum(a.astype(F32), 0.0)),)


def _mlp(name):
    @jax.custom_vjp
    def op(h, w1, w2):
        s, _ = _mm(h, w1, "nn", (BF16, BF16), name + "_up", epi=_relu2_epi)
        return _mm(s, w2, "nn", (F32,), name + "_down")[0]

    def fwd(h, w1, w2):
        s, a = _mm(h, w1, "nn", (BF16, BF16), name + "_up", epi=_relu2_epi)
        return _mm(s, w2, "nn", (F32,), name + "_down")[0], (h, w1, w2, s, a)

    def bwd(res, dy):
        h, w1, w2, s, a = res
        da = _mm(dy, w2, "nt", (BF16,), name + "_ds", epi=_relu2_bwd_epi, extras=(a,))[0]
        dw2 = _mm(s, dy, "tn", (w2.dtype,), name + "_dw2")[0]
        dw1 = _mm(h, da, "tn", (w1.dtype,), name + "_dw1")[0]
        dh = _mm(da, w1, "nt", (h.dtype,), name + "_dh")[0]
        return dh, dw1, dw2

    op.defvjp(fwd, bwd)
    return op


def _dot_exact(x, c):
    return jnp.dot(x, c, precision=lax.Precision.HIGHEST, preferred_element_type=F32)


@jax.custom_vjp
def _bdot(a, b):
    return jnp.dot(a.astype(BF16), b.astype(BF16), preferred_element_type=F32)


def _bdot_fwd(a, b):
    return _bdot(a, b), (a, b)


def _bdot_bwd(res, g):
    a, b = res
    gb = g.astype(BF16)
    da = lax.dot_general(gb, b.astype(BF16), (((1,), (1,)), ((), ())), preferred_element_type=F32)
    db = lax.dot_general(a.astype(BF16), gb, (((0,), (0,)), ((), ())), preferred_element_type=F32)
    return da, db


_bdot.defvjp(_bdot_fwd, _bdot_bwd)


def _rowwise(name, f, out_specs, tl, ctx_blocks=0):
    def seg(l, s):
        return jnp.where(l >= ctx_blocks, s - 1, 0) if s > 1 else 0

    def specs(rows, tabs, pers, glbs, consts):
        row_specs = [pl.BlockSpec((1, tl, r.shape[2]), lambda b, l: (b, l, 0)) for r in rows]
        tab_specs = [pl.BlockSpec((tl, t.shape[1]), lambda b, l: (l, 0)) for t in tabs]
        per_specs = [pl.BlockSpec((1, 1, 1, p.shape[3]), functools.partial(lambda b, l, s: (b, seg(l, s), 0, 0), s=p.shape[1])) for p in pers]
        glb_specs = [pl.BlockSpec(g.shape, functools.partial(lambda b, l, nd: (0,) * nd, nd=g.ndim)) for g in glbs]
        const_specs = [pl.BlockSpec(c.shape, functools.partial(lambda b, l, nd: (0,) * nd, nd=c.ndim)) for c in consts]
        return row_specs, tab_specs, per_specs, glb_specs, const_specs

    def load(refs_rows, refs_tabs, refs_pers, refs_glbs, refs_consts):
        return (tuple(r[0].astype(F32) for r in refs_rows), tuple(t[...] for t in refs_tabs),
                tuple(p[0, 0].astype(F32) for p in refs_pers), tuple(g[...].astype(F32) for g in refs_glbs),
                tuple(c[...] for c in refs_consts))

    def call_fwd(rows, tabs, pers, glbs, consts):
        bsz, length = rows[0].shape[:2]
        nr, nt, npp, ng, nc = len(rows), len(tabs), len(pers), len(glbs), len(consts)
        rs, ts, ps, gs, cs = specs(rows, tabs, pers, glbs, consts)

        def body(*refs):
            ins, outs = refs[:nr + nt + npp + ng + nc], refs[nr + nt + npp + ng + nc:]
            r, t, p, g, c = load(ins[:nr], ins[nr:nr + nt], ins[nr + nt:nr + nt + npp], ins[nr + nt + npp:nr + nt + npp + ng], ins[nr + nt + npp + ng:])
            for o_ref, o in zip(outs, f(r, t, p, g, c)):
                o_ref[0] = o.astype(o_ref.dtype)

        return _pcall(
            body, name=name + "_fwd", grid=(bsz, length // tl),
            in_specs=rs + ts + ps + gs + cs,
            out_specs=[pl.BlockSpec((1, tl, w), lambda b, l: (b, l, 0)) for w, _ in out_specs],
            out_shape=[jax.ShapeDtypeStruct((bsz, length, w), d) for w, d in out_specs],
            compiler_params=_params(("parallel", "parallel")),
        )(*rows, *tabs, *pers, *glbs, *consts)

    def call_bwd(rows, tabs, pers, glbs, consts, cts):
        bsz, length = rows[0].shape[:2]
        nr, nt, npp, ng, nc, no = len(rows), len(tabs), len(pers), len(glbs), len(consts), len(cts)
        rs, ts, ps, gs, cs = specs(rows, tabs, pers, glbs, consts)
        n_in = nr + nt + npp + ng + nc

        def body(*refs):
            ins, ct_refs, outs = refs[:n_in], refs[n_in:n_in + no], refs[n_in + no:]
            r, t, p, g, c = load(ins[:nr], ins[nr:nr + nt], ins[nr + nt:nr + nt + npp], ins[nr + nt + npp:nr + nt + npp + ng], ins[nr + nt + npp + ng:])
            _, vjp = jax.vjp(lambda r_, p_, g_: tuple(f(r_, t, p_, g_, c)), r, p, g)
            dr, dp, dg = vjp(tuple(ct[0].astype(F32) for ct in ct_refs))
            dr_refs, dp_refs, dg_refs = outs[:nr], outs[nr:nr + npp], outs[nr + npp:]
            for ref, d in zip(dr_refs, dr):
                ref[0] = d.astype(ref.dtype)
            b, l = pl.program_id(0), pl.program_id(1)
            first_of_segment = (l == 0) | (l == ctx_blocks)
            for ref, d in zip(dp_refs, dp):
                @pl.when(first_of_segment)
                def _(ref=ref, d=d):
                    ref[0, 0] = d

                @pl.when(jnp.logical_not(first_of_segment))
                def _(ref=ref, d=d):
                    ref[0, 0] += d
            first = (b == 0) & (l == 0)
            for ref, d in zip(dg_refs, dg):
                @pl.when(first)
                def _(ref=ref, d=d):
                    ref[...] = d

                @pl.when(jnp.logical_not(first))
                def _(ref=ref, d=d):
                    ref[...] += d

        ct_specs = [pl.BlockSpec((1, tl, w), lambda b, l: (b, l, 0)) for w, _ in out_specs]
        outs = _pcall(
            body, name=name + "_bwd", grid=(bsz, length // tl),
            in_specs=rs + ts + ps + gs + cs + ct_specs,
            out_specs=rs + ps + gs,
            out_shape=[jax.ShapeDtypeStruct(r.shape, r.dtype) for r in rows]
            + [jax.ShapeDtypeStruct(p.shape, F32) for p in pers] + [jax.ShapeDtypeStruct(g.shape, F32) for g in glbs],
            compiler_params=_params(("arbitrary", "arbitrary")),
        )(*rows, *tabs, *pers, *glbs, *consts, *cts)
        return tuple(outs[:nr]), tuple(outs[nr:nr + npp]), tuple(outs[nr + npp:])

    @jax.custom_vjp
    def op(rows, tabs, pers, glbs, consts):
        return tuple(call_fwd(rows, tabs, pers, glbs, consts))

    def fwd(rows, tabs, pers, glbs, consts):
        return op(rows, tabs, pers, glbs, consts), (rows, tabs, pers, glbs, consts)

    def bwd(res, cts):
        rows, tabs, pers, glbs, consts = res
        dr, dp, dg = call_bwd(rows, tabs, pers, glbs, consts, tuple(cts))
        dp = tuple(d.astype(p.dtype) for d, p in zip(dp, pers))
        dg = tuple(d.astype(g.dtype) for d, g in zip(dg, glbs))
        return dr, tuple(jnp.zeros_like(t) for t in tabs), dp, dg, tuple(jnp.zeros_like(c) for c in consts)

    op.defvjp(fwd, bwd)
    return op


def _rms(x, g):
    return x * lax.rsqrt(jnp.mean(x * x, axis=-1, keepdims=True) + EPS) * g


def _f_silu(r, t, p, g, c):
    return (jax.nn.silu(r[0]),)


def _f_modulate(r, t, p, g, c):
    shift, scale = p
    return (_rms(r[0], g[0]) * (1.0 + scale) + shift,)


def _f_res_modulate(r, t, p, g, c):
    x, y = r
    gate, shift, scale = p
    xn = x + gate * y
    return xn, _rms(xn, g[0]) * (1.0 + scale) + shift


def _f_res(r, t, p, g, c):
    return (r[0] + p[0] * r[1],)


def _f_headnorm_rope(r, t, p, g, c):
    x = r[0]
    cos, sin = t
    group_mean, swap = c
    xn = x * lax.rsqrt(_dot_exact(x * x, group_mean) + EPS) * g[0]
    return (xn * cos + _dot_exact(xn, swap) * sin,)


def _f_rope(r, t, p, g, c):
    x = r[0]
    cos, sin = t
    return (x * cos + _dot_exact(x, c[0]) * sin,)


def _f_rms(r, t, p, g, c):
    return (_rms(r[0], g[0]),)


def _f_sgu_pre(r, t, p, g, c):
    u = jax.nn.gelu(r[0])
    v = jax.nn.gelu(r[1])
    vn = v * lax.rsqrt(_dot_exact(v * v, c[0]) + EPS) * g[0]
    return u, vn


def _f_sgu_mix(r, t, p, g, c):
    u, vn = r
    bias = g[B_GROUPS]
    group = lax.broadcasted_iota(jnp.int32, vn.shape, 1) // B_GROUP_DIM
    sv = bias
    for k in range(B_GROUPS):
        sv = sv + jnp.where(group == k, _bdot(g[k], vn), 0.0)
    return (u * sv,)


def _f_glu(r, t, p, g, c):
    return (r[0] * jax.nn.sigmoid(r[1]),)


def _f_ln_silu(r, t, p, g, c):
    x = r[0]
    mu = jnp.mean(x, axis=-1, keepdims=True)
    var = jnp.mean(jnp.square(x - mu), axis=-1, keepdims=True)
    return (jax.nn.silu((x - mu) * lax.rsqrt(var + EPS) * g[0] + g[1]),)


def _attention(name, ctx_len):
    def tiles(q, v):
        lq = q.shape[2]
        tq = _pick(lq, (256, 128))
        if ctx_len:
            tq = math.gcd(tq, ctx_len)
        return tq, ctx_len // tq

    def probs(qq, kk, scale):
        s = lax.dot_general(qq, kk, (((1,), (1,)), ((), ())), preferred_element_type=F32) * scale
        p = jnp.exp(s - jnp.max(s, axis=-1, keepdims=True))
        return p, jnp.sum(p, axis=-1, keepdims=True)

    def by_segment(qi, cb, lk, run):
        if cb > 0:
            @pl.when(qi < cb)
            def _():
                run(ctx_len)

            @pl.when(qi >= cb)
            def _():
                run(lk)
        else:
            run(lk)

    def call_fwd(q, k, v):
        bsz, hq, lq, dk = q.shape
        _, hk, lk, dv = v.shape
        grp = hq // hk
        tq, cb = tiles(q, v)
        scale = dk ** -0.5

        def body(q_ref, k_ref, v_ref, o_ref):
            def run(nk):
                p, l = probs(q_ref[0, 0], k_ref[0, 0, :nk], scale)
                o = jnp.dot(p.astype(BF16), v_ref[0, 0, :nk], preferred_element_type=F32) / l
                o_ref[0, 0] = o.astype(o_ref.dtype)

            by_segment(pl.program_id(2), cb, lk, run)

        return _pcall(
            body, name=name + "_fwd", grid=(bsz, hq, lq // tq),
            in_specs=[pl.BlockSpec((1, 1, tq, dk), lambda b, h, i: (b, h, i, 0)),
                      pl.BlockSpec((1, 1, lk, dk), lambda b, h, i: (b, h // grp, 0, 0)),
                      pl.BlockSpec((1, 1, lk, dv), lambda b, h, i: (b, h // grp, 0, 0))],
            out_specs=pl.BlockSpec((1, 1, tq, dv), lambda b, h, i: (b, h, i, 0)),
            out_shape=jax.ShapeDtypeStruct((bsz, hq, lq, dv), BF16),
            compiler_params=_params(("parallel", "parallel", "parallel")),
        )(q, k, v)

    def call_bwd(q, k, v, do):
        bsz, hq, lq, dk = q.shape
        _, hk, lk, dv = v.shape
        grp = hq // hk
        tq, cb = tiles(q, v)
        scale = dk ** -0.5

        def body(q_ref, k_ref, v_ref, do_ref, dq_ref, dk_ref, dv_ref):
            g, qi = pl.program_id(2), pl.program_id(3)

            @pl.when((g == 0) & (qi == 0))
            def _():
                dk_ref[...] = jnp.zeros_like(dk_ref)
                dv_ref[...] = jnp.zeros_like(dv_ref)

            def run(nk):
                qq, kk, vv, dd = q_ref[0, 0], k_ref[0, 0, :nk], v_ref[0, 0, :nk], do_ref[0, 0]
                p, l = probs(qq, kk, scale)
                pn = p / l
                dp = lax.dot_general(dd, vv, (((1,), (1,)), ((), ())), preferred_element_type=F32)
                ds = (pn * (dp - jnp.sum(pn * dp, axis=-1, keepdims=True)) * scale).astype(BF16)
                dq_ref[0, 0] = jnp.dot(ds, kk, preferred_element_type=F32).astype(dq_ref.dtype)
                dk_ref[0, 0, :nk] += lax.dot_general(ds, qq, (((0,), (0,)), ((), ())), preferred_element_type=F32)
                dv_ref[0, 0, :nk] += lax.dot_general(pn.astype(BF16), dd, (((0,), (0,)), ((), ())), preferred_element_type=F32)

            by_segment(qi, cb, lk, run)

        q_spec = pl.BlockSpec((1, 1, tq, dk), lambda b, h, g, i: (b, h * grp + g, i, 0))
        k_spec = pl.BlockSpec((1, 1, lk, dk), lambda b, h, g, i: (b, h, 0, 0))
        v_spec = pl.BlockSpec((1, 1, lk, dv), lambda b, h, g, i: (b, h, 0, 0))
        do_spec = pl.BlockSpec((1, 1, tq, dv), lambda b, h, g, i: (b, h * grp + g, i, 0))
        return _pcall(
            body, name=name + "_bwd", grid=(bsz, hk, grp, lq // tq),
            in_specs=[q_spec, k_spec, v_spec, do_spec],
            out_specs=[q_spec, k_spec, v_spec],
            out_shape=[jax.ShapeDtypeStruct(q.shape, BF16), jax.ShapeDtypeStruct(k.shape, F32), jax.ShapeDtypeStruct(v.shape, F32)],
            compiler_params=_params(("parallel", "parallel", "arbitrary", "arbitrary")),
        )(q, k, v, do)

    @jax.custom_vjp
    def op(q, k, v):
        return call_fwd(q, k, v)

    def fwd(q, k, v):
        return op(q, k, v), (q, k, v)

    def bwd(res, do):
        q, k, v = res
        dq, dk, dv = call_bwd(q, k, v, do)
        return dq, dk.astype(k.dtype), dv.astype(v.dtype)

    op.defvjp(fwd, bwd)
    return op


def _conv_call(ypad, taps, name):
    bsz, lp, ch = ypad.shape
    length = lp - 2 * ROW_ALIGN
    tl = _pick(length, (256, 128))

    def body(y_ref, w_ref, o_ref):
        base = pl.multiple_of(pl.program_id(1) * tl, tl)
        win = y_ref[0, pl.ds(base, tl + 2 * ROW_ALIGN), :]
        acc = jnp.broadcast_to(w_ref[pl.ds(D_CONV, 1), :], (tl, ch))
        for k in range(D_CONV):
            acc = acc + win[k:k + tl, :] * w_ref[pl.ds(k, 1), :]
        o_ref[0] = acc

    return _pcall(
        body, name=name, grid=(bsz, length // tl),
        in_specs=[pl.BlockSpec((1, lp, ch), lambda b, l: (b, 0, 0)), pl.BlockSpec((D_CONV + 1, ch), lambda b, l: (0, 0))],
        out_specs=pl.BlockSpec((1, tl, ch), lambda b, l: (b, l, 0)),
        out_shape=jax.ShapeDtypeStruct((bsz, length, ch), F32),
        compiler_params=_params(("parallel", "parallel")),
    )(ypad, taps)


def _conv_dw_call(ypad, dout, name):
    bsz, lp, ch = ypad.shape
    length = lp - 2 * ROW_ALIGN
    tl = _pick(length, (256, 128))

    def body(y_ref, d_ref, o_ref):
        b, l = pl.program_id(0), pl.program_id(1)

        @pl.when((b == 0) & (l == 0))
        def _():
            o_ref[...] = jnp.zeros_like(o_ref)

        base = pl.multiple_of(l * tl, tl)
        win = y_ref[0, pl.ds(base, tl + 2 * ROW_ALIGN), :]
        dd = d_ref[0]
        for k in range(D_CONV):
            o_ref[pl.ds(k, 1), :] += jnp.sum(win[k:k + tl, :] * dd, axis=0, keepdims=True)
        o_ref[pl.ds(D_CONV, 1), :] += jnp.sum(dd, axis=0, keepdims=True)

    return _pcall(
        body, name=name, grid=(bsz, length // tl),
        in_specs=[pl.BlockSpec((1, lp, ch), lambda b, l: (b, 0, 0)), pl.BlockSpec((1, tl, ch), lambda b, l: (b, l, 0))],
        out_specs=pl.BlockSpec((D_CONV + 1, ch), lambda b, l: (0, 0)),
        out_shape=jax.ShapeDtypeStruct((D_CONV + 1, ch), F32),
        compiler_params=_params(("arbitrary", "arbitrary")),
    )(ypad, dout)


def _pad_rows(y):
    return jnp.pad(y, ((0, 0), (CONV_PAD, 2 * ROW_ALIGN - CONV_PAD), (0, 0)))


@jax.custom_vjp
def _dwconv(y, taps):
    return _conv_call(_pad_rows(y), taps, "conv_fwd")


def _dwconv_fwd(y, taps):
    return _dwconv(y, taps), (y, taps)


def _dwconv_bwd(res, dout):
    y, taps = res
    flipped = jnp.concatenate([taps[:D_CONV][::-1], jnp.zeros_like(taps[D_CONV:])], axis=0)
    dy = _conv_call(_pad_rows(dout), flipped, "conv_dy")
    dtaps = _conv_dw_call(_pad_rows(y), dout, "conv_dw")
    return dy, dtaps


_dwconv.defvjp(_dwconv_fwd, _dwconv_bwd)


def _loss_head(x, target, g):
    bsz, length, d = x.shape
    tl = _pick(length, (256, 128))

    def f(xb, tb, gb):
        err = _rms(xb, gb) - tb
        return 0.5 * jnp.sum(jnp.sum(err * err, axis=-1, keepdims=True), axis=0, keepdims=True) / d

    def body(x_ref, t_ref, g_ref, loss_ref, dx_ref, dg_ref):
        val, vjp = jax.vjp(lambda xb, gb: f(xb, t_ref[0], gb), x_ref[0], g_ref[...])
        dx, dg = vjp(jnp.ones((1, 1), F32))
        dx_ref[0] = dx
        first = (pl.program_id(0) == 0) & (pl.program_id(1) == 0)

        @pl.when(first)
        def _():
            loss_ref[...] = val
            dg_ref[...] = dg

        @pl.when(jnp.logical_not(first))
        def _():
            loss_ref[...] += val
            dg_ref[...] += dg

    row = pl.BlockSpec((1, tl, d), lambda b, l: (b, l, 0))
    return _pcall(
        body, name="loss_head", grid=(bsz, length // tl),
        in_specs=[row, row, pl.BlockSpec((1, d), lambda b, l: (0, 0))],
        out_specs=[pl.BlockSpec((1, 1), lambda b, l: (0, 0)), row, pl.BlockSpec((1, d), lambda b, l: (0, 0))],
        out_shape=[jax.ShapeDtypeStruct((1, 1), F32), jax.ShapeDtypeStruct(x.shape, F32), jax.ShapeDtypeStruct((1, d), F32)],
        compiler_params=_params(("arbitrary", "arbitrary")),
    )(x, target, g)


def _adamw(w, g, m, v, name):
    shape = w.shape
    cols = shape[-1]
    rows = w.size // cols
    tr = _pick(rows, (512, 256, 128))
    w2, g2, m2, v2 = (t.reshape(rows, cols) for t in (w, g, m, v))

    def body(w_ref, g_ref, m_ref, v_ref, d_ref, nm_ref, nv_ref):
        gg = g_ref[...]
        nm = ADAM_B1 * m_ref[...] + (1.0 - ADAM_B1) * gg
        nv = ADAM_B2 * v_ref[...] + (1.0 - ADAM_B2) * jnp.square(gg)
        m_hat = nm / (1.0 - ADAM_B1 ** ADAM_STEP)
        v_hat = nv / (1.0 - ADAM_B2 ** ADAM_STEP)
        d_ref[...] = -ADAM_LR * (m_hat / (jnp.sqrt(v_hat) + ADAM_EPS) + ADAM_WD * w_ref[...])
        nm_ref[...] = nm
        nv_ref[...] = nv

    spec = pl.BlockSpec((tr, cols), lambda i: (i, 0))
    outs = _pcall(
        body, name=name, grid=(rows // tr,), in_specs=[spec] * 4, out_specs=[spec] * 3,
        out_shape=[jax.ShapeDtypeStruct((rows, cols), F32)] * 3,
        compiler_params=_params(("parallel",)),
    )(w2, g2, m2, v2)
    return tuple(o.reshape(shape) for o in outs)


def _mesh_pos():
    return lax.axis_index("x"), lax.axis_index("y"), lax.axis_index("c")


_RELATIONS = [(dx, dy, dc) for dx in (0, 1) for dy in (0, 1) for dc in (0, 1)][1:]


def _peer(pos, rel):
    return tuple(jnp.where(r == 1, 1 - p, p) if r else p for p, r in zip(pos, rel))


def _block_index(pos):
    return 4 * pos[0] + 2 * pos[1] + pos[2]


_HBM = pl.BlockSpec(memory_space=pltpu.HBM)


def _all_gather(x, name):
    rows, cols = x.shape

    def body(x_ref, out_ref, send_sems, recv_sems, local_sem):
        x_, y_, c_ = _mesh_pos()
        me, sibling = (x_, y_, c_), (x_, y_, 1 - c_)
        chips = [(1 - x_, y_), (x_, 1 - y_), (1 - x_, 1 - y_)]

        def slot(pos):
            return out_ref.at[_block_index(pos)]

        def copy(k, block, to, src=None):
            return pltpu.make_async_remote_copy(
                src_ref=slot(block) if src is None else src, dst_ref=slot(block),
                send_sem=send_sems.at[k], recv_sem=recv_sems.at[k], device_id=to, device_id_type=pl.DeviceIdType.MESH)

        mine = pltpu.make_async_copy(x_ref, slot(me), local_sem)
        mine.start()
        first = [copy(0, me, sibling, src=x_ref)] + [copy(1 + j, me, (*chip, c_), src=x_ref) for j, chip in enumerate(chips)]
        for cp in first:
            cp.start()
        passed = [copy(4 + j, (*chip, c_), sibling) for j, chip in enumerate(chips)]
        for j, chip in enumerate(chips):
            copy(1 + j, (*chip, c_), me).wait_recv()
            passed[j].start()
        copy(0, sibling, me).wait_recv()
        for j, chip in enumerate(chips):
            copy(4 + j, (*chip, 1 - c_), me).wait_recv()
        for cp in first + passed:
            cp.wait_send()
        mine.wait()

    return _pcall(
        body, name=name, in_specs=[_HBM], out_specs=_HBM,
        out_shape=jax.ShapeDtypeStruct((N_DEV, rows, cols), x.dtype),
        scratch_shapes=[pltpu.SemaphoreType.DMA((7,)), pltpu.SemaphoreType.DMA((7,)), pltpu.SemaphoreType.DMA(())],
    )(x)


def _exchange_blocks(p, name):
    _, rows, cols = p.shape

    def body(p_ref, out_ref, send_sems, recv_sems, local_sem):
        me = _mesh_pos()
        mine = pltpu.make_async_copy(p_ref.at[_block_index(me)], out_ref.at[_block_index(me)], local_sem)
        mine.start()
        sends = []
        for k, rel in enumerate(_RELATIONS):
            peer = _peer(me, rel)
            sends.append(pltpu.make_async_remote_copy(
                src_ref=p_ref.at[_block_index(peer)], dst_ref=out_ref.at[_block_index(me)],
                send_sem=send_sems.at[k], recv_sem=recv_sems.at[k], device_id=peer, device_id_type=pl.DeviceIdType.MESH))
        for cp in sends:
            cp.start()
        for k, rel in enumerate(_RELATIONS):
            peer = _peer(me, rel)
            pltpu.make_async_remote_copy(
                src_ref=p_ref.at[_block_index(peer)], dst_ref=out_ref.at[_block_index(peer)],
                send_sem=send_sems.at[k], recv_sem=recv_sems.at[k], device_id=peer, device_id_type=pl.DeviceIdType.MESH).wait_recv()
        for cp in sends:
            cp.wait_send()
        mine.wait()

    return _pcall(
        body, name=name, in_specs=[_HBM], out_specs=_HBM,
        out_shape=jax.ShapeDtypeStruct(p.shape, p.dtype),
        scratch_shapes=[pltpu.SemaphoreType.DMA((7,)), pltpu.SemaphoreType.DMA((7,)), pltpu.SemaphoreType.DMA(())],
    )(p)


def _sum_blocks(p, name):
    n, rows, cols = p.shape
    tr = _pick(rows, (SUM_ROWS, 128, 64, 32, 16, 8))

    def body(p_ref, o_ref):
        acc = p_ref[0].astype(F32)
        for s in range(1, n):
            acc = acc + p_ref[s].astype(F32)
        o_ref[...] = acc

    return _pcall(
        body, name=name, grid=(rows // tr,),
        in_specs=[pl.BlockSpec((n, tr, cols), lambda i: (0, i, 0))], out_specs=pl.BlockSpec((tr, cols), lambda i: (i, 0)),
        out_shape=jax.ShapeDtypeStruct((rows, cols), F32), compiler_params=_params(("parallel",)),
    )(p)


def _sum_rows(t, name):
    def body(t_ref, o_ref):
        o_ref[...] = jnp.sum(t_ref[...], axis=0, keepdims=True)

    return _pcall(body, name=name, out_shape=jax.ShapeDtypeStruct((1, t.shape[1]), F32))(t)


class _Packing:
    def __init__(self, sizes, align, total_align=None):
        self.offsets, self.sizes, self.align = {}, dict(sizes), align
        row = 0
        for name, size in sizes:
            self.offsets[name] = row
            row += -(-size // (align * PACK_COLS)) * align
        total_align = total_align or align
        self.rows = -(-row // total_align) * total_align
        self.tail = self.rows - row

    def pack(self, pieces):
        return self.pack_blocks({n: pieces[n].reshape(1, -1) for n in self.sizes})[0]

    def pack_blocks(self, pieces):
        out = []
        for n, size in self.sizes.items():
            padded = -(-size // (self.align * PACK_COLS)) * self.align * PACK_COLS
            out.append(jnp.pad(pieces[n], ((0, 0), (0, padded - size))).reshape(pieces[n].shape[0], -1, PACK_COLS))
        if self.tail:
            out.append(jnp.zeros((out[0].shape[0], self.tail, PACK_COLS), out[0].dtype))
        return jnp.concatenate(out, axis=1)

    def piece(self, packed, name, lead=()):
        start, size = self.offsets[name], self.sizes[name]
        nrow = -(-size // (self.align * PACK_COLS)) * self.align
        sl = packed[..., start:start + nrow, :]
        return sl.reshape(lead + (nrow * PACK_COLS,))[..., :size]


_BIG = (("w_out", 1), ("mlp_w1", 2), ("mlp_w2", 1), ("ev_w_in", 2), ("od_w_in", 2), ("od_w_uq", 2), ("od_w_ukv", 2))
_SMALL_SHARDED = (("od_q_norm_g", 1), ("od_conv_w", 2), ("od_conv_b", 1), ("od_ln_g", 1), ("od_ln_b", 1))
_REPLICATED = ("c_ctx", "norm1_g", "norm2_g", "ev_q_norm_g", "ev_k_norm_g", "ev_sgu_norm_g", "ev_sgu_w", "ev_sgu_b",
               "od_kv_norm_g", "final_g")


def _unshard(blocks, axis):
    moved = jnp.moveaxis(blocks, 0, axis)
    shape = moved.shape
    return moved.reshape(shape[:axis] + (shape[axis] * shape[axis + 1],) + shape[axis + 2:])


def _shard_blocks(full, axis):
    shape = full.shape
    split = full.reshape(shape[:axis] + (N_DEV, shape[axis] // N_DEV) + shape[axis + 1:])
    return jnp.moveaxis(split, axis, 0)


def _group_mean_matrix(width, group):
    idx = jnp.arange(width) // group
    return (idx[:, None] == idx[None, :]).astype(F32) / group


def _swap_matrix(width):
    idx = jnp.arange(width)
    return ((idx[:, None] ^ 1) == idx[None, :]).astype(F32)


def _angles(length, d_rot):
    rows = length // GRID_W
    row = jnp.broadcast_to(jnp.arange(rows)[:, None], (rows, GRID_W)).reshape(-1).astype(F32)
    col = jnp.broadcast_to(jnp.arange(GRID_W)[None, :], (rows, GRID_W)).reshape(-1).astype(F32)
    d_axis = d_rot // 2
    inv = ROPE_THETA ** (-jnp.arange(0, d_axis, 2, dtype=F32) / d_axis)
    return jnp.concatenate([row[:, None] * inv, col[:, None] * inv], axis=-1)


def _rope_tables(length, d_rot, head_dim, heads, ctx_len):
    ang = _angles(length, d_rot)
    cos = jnp.repeat(jnp.cos(ang), 2, axis=1)
    sin = jnp.repeat(jnp.sin(ang), 2, axis=1) * jnp.tile(jnp.array([-1.0, 1.0], F32), d_rot // 2)
    keep = head_dim - d_rot
    cos = jnp.concatenate([jnp.ones((length, keep), F32), cos], axis=1)
    sin = jnp.concatenate([jnp.zeros((length, keep), F32), sin], axis=1)
    cos, sin = jnp.tile(cos, (1, heads)), jnp.tile(sin, (1, heads))
    cos = jnp.concatenate([jnp.ones((ctx_len, cos.shape[1]), F32), cos], axis=0)
    sin = jnp.concatenate([jnp.zeros((ctx_len, sin.shape[1]), F32), sin], axis=0)
    return cos, sin


def _to_heads(t, heads):
    b, l, w = t.shape
    return t.reshape(b, l, heads, w // heads).transpose(0, 2, 1, 3)


def _from_heads(t):
    b, h, l, d = t.shape
    return t.transpose(0, 2, 1, 3).reshape(b, l, h * d)


def _segment_params(mod, bsz):
    parts = jnp.split(mod, N_MOD, axis=-1)
    out = []
    for part in parts:
        lat = part[:bsz]
        ctx = jnp.broadcast_to(part[bsz:bsz + 1], lat.shape)
        out.append(jnp.stack([ctx, lat], axis=1)[:, :, None, :])
    return out


def _forward(xall, modrows, w, bsz, length, ctx_len):
    d = xall.shape[-1]
    total = ctx_len + length
    tl = math.gcd(256, ctx_len)
    cb = ctx_len // tl

    def rowwise(name, f, out_specs, rows_per_block=tl, ctx_blocks=cb):
        return _rowwise(name, f, out_specs, rows_per_block, ctx_blocks)

    mods = [_segment_params(m, bsz) for m in modrows]

    def flat(t):
        return t.reshape(-1, t.shape[-1])

    sh1, sc1, g1, sh2, sc2, g2 = mods[0]
    (h,) = rowwise("mod0", _f_modulate, [(d, BF16)])((xall,), (), (sh1, sc1), (w["norm1_g"][0][None],), ())
    proj = _linear("ev_in")(flat(h), w["ev_w_in"][0]).reshape(bsz, total, -1)
    ev_q, ev_kv = A_Q_HEADS * A_HEAD_DIM, A_KV_HEADS * A_HEAD_DIM
    half = B_GROUPS * B_GROUP_DIM
    qp, kp, vp, zu, zv = jnp.split(proj, [ev_q, ev_q + ev_kv, ev_q + 2 * ev_kv, ev_q + 2 * ev_kv + half], axis=-1)
    cos_q, sin_q = _rope_tables(length, A_HEAD_DIM, A_HEAD_DIM, A_Q_HEADS, ctx_len)
    cos_k, sin_k = cos_q[:, :ev_kv], sin_q[:, :ev_kv]
    (q,) = rowwise("ev_q", _f_headnorm_rope, [(ev_q, BF16)])(
        (qp,), (cos_q, sin_q), (), (jnp.tile(w["ev_q_norm_g"][0], A_Q_HEADS)[None],), (_group_mean_matrix(ev_q, A_HEAD_DIM), _swap_matrix(ev_q)))
    (k,) = rowwise("ev_k", _f_headnorm_rope, [(ev_kv, BF16)])(
        (kp,), (cos_k, sin_k), (), (jnp.tile(w["ev_k_norm_g"][0], A_KV_HEADS)[None],), (_group_mean_matrix(ev_kv, A_HEAD_DIM), _swap_matrix(ev_kv)))
    o_att = _attention("gqa", ctx_len)(_to_heads(q, A_Q_HEADS), _to_heads(k, A_KV_HEADS), _to_heads(vp.astype(BF16), A_KV_HEADS))
    u, vn = rowwise("sgu_pre", _f_sgu_pre, [(half, F32), (half, BF16)])(
        (zu, zv), (), (), (w["ev_sgu_norm_g"][0].reshape(1, half),), (_group_mean_matrix(half, B_GROUP_DIM),))
    bias = jnp.repeat(w["ev_sgu_b"][0].T, B_GROUP_DIM, axis=1)
    (o_sgu,) = rowwise("sgu_mix", _f_sgu_mix, [(half, BF16)], rows_per_block=B_CHUNK, ctx_blocks=0)(
        (u, vn), (), (), tuple(w["ev_sgu_w"][0][g] for g in range(B_GROUPS)) + (bias,), ())
    o = jnp.concatenate([_from_heads(o_att), o_sgu], axis=-1)
    y = _linear("out0")(flat(o), w["w_out"][0]).reshape(bsz, total, d)
    x1, h = rowwise("res_mod0a", _f_res_modulate, [(d, F32), (d, BF16)])((xall, y), (), (g1, sh2, sc2), (w["norm2_g"][0][None],), ())
    y = _mlp("mlp0")(flat(h), w["mlp_w1"][0], w["mlp_w2"][0]).reshape(bsz, total, d)
    sh1, sc1, g1n, sh2n, sc2n, g2n = mods[1]
    x2, h = rowwise("res_mod0b", _f_res_modulate, [(d, F32), (d, BF16)])((x1, y), (), (g2, sh1, sc1), (w["norm1_g"][1][None],), ())

    proj = _linear("od_in")(flat(h), w["od_w_in"][0]).reshape(bsz, total, -1)
    cq, ckv, kr, za, zg = jnp.split(proj, [C_Q_RANK, C_Q_RANK + C_KV_RANK, C_Q_RANK + C_KV_RANK + C_ROPE,
                                          C_Q_RANK + C_KV_RANK + C_ROPE + half], axis=-1)
    lat = slice(ctx_len, total)
    lat_tl = math.gcd(256, length)
    (cqn,) = _rowwise("od_qn", _f_rms, [(C_Q_RANK, BF16)], lat_tl)((cq[:, lat],), (), (), (w["od_q_norm_g"],), ())
    c_qk = C_NOPE + C_ROPE
    qf = _linear("od_uq")(flat(cqn), w["od_w_uq"][0]).reshape(bsz, length, C_HEADS * c_qk)
    cos_q, sin_q = _rope_tables(length, C_ROPE, c_qk, C_HEADS, 0)
    (q,) = _rowwise("od_qrope", _f_rope, [(C_HEADS * c_qk, BF16)], lat_tl)((qf,), (cos_q, sin_q), (), (), (_swap_matrix(C_HEADS * c_qk),))
    (ckvn,) = rowwise("od_kvn", _f_rms, [(C_KV_RANK, BF16)])((ckv,), (), (), (w["od_kv_norm_g"],), ())
    kv = _linear("od_ukv")(flat(ckvn), w["od_w_ukv"][0]).reshape(bsz, total, C_HEADS, C_NOPE + C_V)
    cos_r, sin_r = _rope_tables(length, C_ROPE, C_ROPE, 1, ctx_len)
    (krr,) = rowwise("od_krope", _f_rope, [(C_ROPE, BF16)])((kr,), (cos_r, sin_r), (), (), (_swap_matrix(C_ROPE),))
    kn = kv[..., :C_NOPE].astype(BF16).transpose(0, 2, 1, 3)
    vv = kv[..., C_NOPE:].astype(BF16).transpose(0, 2, 1, 3)
    kfull = jnp.concatenate([kn, jnp.broadcast_to(krr[:, None], (bsz, C_HEADS, total, C_ROPE))], axis=-1)
    o_att = _attention("mla", 0)(_to_heads(q, C_HEADS), kfull, vv)
    (glu,) = _rowwise("glu", _f_glu, [(half, F32)], lat_tl)((za[:, lat], zg[:, lat]), (), (), (), ())
    taps = jnp.concatenate([w["od_conv_w"][0], w["od_conv_b"]], axis=0)
    conv = _dwconv(glu, taps)
    (o_conv,) = _rowwise("ln_silu", _f_ln_silu, [(half, BF16)], lat_tl)((conv,), (), (), (w["od_ln_g"], w["od_ln_b"]), ())
    o = jnp.concatenate([_from_heads(o_att), o_conv], axis=-1)
    y = _linear("out1")(flat(o), w["w_out"][1]).reshape(bsz, length, d)
    lat_param = lambda p: p[:, 1:]
    x3, h = _rowwise("res_mod1a", _f_res_modulate, [(d, F32), (d, BF16)], lat_tl)(
        (x2[:, lat], y), (), (lat_param(g1n), lat_param(sh2n), lat_param(sc2n)), (w["norm2_g"][1][None],), ())
    y = _mlp("mlp1")(flat(h), w["mlp_w1"][1], w["mlp_w2"][1]).reshape(bsz, length, d)
    (x4,) = _rowwise("res1b", _f_res, [(d, F32)], lat_tl)((x3, y), (), (lat_param(g2n),), (), ())
    return x4


def kernel(x, c, ctx, c_ctx, ada_w, ada_b, norm1_g, norm2_g, w_out, mlp_w1, mlp_w2, ev_w_in, ev_q_norm_g, ev_k_norm_g, ev_sgu_norm_g, ev_sgu_w, ev_sgu_b, od_w_in, od_q_norm_g, od_kv_norm_g, od_w_uq, od_w_ukv, od_conv_w, od_conv_b, od_ln_g, od_ln_b, final_g, loss_target, m_c_ctx, m_ada_w, m_ada_b, m_norm1_g, m_norm2_g, m_w_out, m_mlp_w1, m_mlp_w2, m_ev_w_in, m_ev_q_norm_g, m_ev_k_norm_g, m_ev_sgu_norm_g, m_ev_sgu_w, m_ev_sgu_b, m_od_w_in, m_od_q_norm_g, m_od_kv_norm_g, m_od_w_uq, m_od_w_ukv, m_od_conv_w, m_od_conv_b, m_od_ln_g, m_od_ln_b, m_final_g, v_c_ctx, v_ada_w, v_ada_b, v_norm1_g, v_norm2_g, v_w_out, v_mlp_w1, v_mlp_w2, v_ev_w_in, v_ev_q_norm_g, v_ev_k_norm_g, v_ev_sgu_norm_g, v_ev_sgu_w, v_ev_sgu_b, v_od_w_in, v_od_q_norm_g, v_od_kv_norm_g, v_od_w_uq, v_od_w_ukv, v_od_conv_w, v_od_conv_b, v_od_ln_g, v_od_ln_b, v_final_g):
    names = ["c_ctx", "ada_w", "ada_b", "norm1_g", "norm2_g", "w_out", "mlp_w1", "mlp_w2", "ev_w_in", "ev_q_norm_g", "ev_k_norm_g",
             "ev_sgu_norm_g", "ev_sgu_w", "ev_sgu_b", "od_w_in", "od_q_norm_g", "od_kv_norm_g", "od_w_uq", "od_w_ukv", "od_conv_w",
             "od_conv_b", "od_ln_g", "od_ln_b", "final_g"]
    local = dict(zip(names, [c_ctx, ada_w, ada_b, norm1_g, norm2_g, w_out, mlp_w1, mlp_w2, ev_w_in, ev_q_norm_g, ev_k_norm_g, ev_sgu_norm_g, ev_sgu_w, ev_sgu_b, od_w_in, od_q_norm_g, od_kv_norm_g, od_w_uq, od_w_ukv, od_conv_w, od_conv_b, od_ln_g, od_ln_b, final_g]))
    mom1 = dict(zip(names, [m_c_ctx, m_ada_w, m_ada_b, m_norm1_g, m_norm2_g, m_w_out, m_mlp_w1, m_mlp_w2, m_ev_w_in, m_ev_q_norm_g, m_ev_k_norm_g, m_ev_sgu_norm_g, m_ev_sgu_w, m_ev_sgu_b, m_od_w_in, m_od_q_norm_g, m_od_kv_norm_g, m_od_w_uq, m_od_w_ukv, m_od_conv_w, m_od_conv_b, m_od_ln_g, m_od_ln_b, m_final_g]))
    mom2 = dict(zip(names, [v_c_ctx, v_ada_w, v_ada_b, v_norm1_g, v_norm2_g, v_w_out, v_mlp_w1, v_mlp_w2, v_ev_w_in, v_ev_q_norm_g, v_ev_k_norm_g, v_ev_sgu_norm_g, v_ev_sgu_w, v_ev_sgu_b, v_od_w_in, v_od_q_norm_g, v_od_kv_norm_g, v_od_w_uq, v_od_w_ukv, v_od_conv_w, v_od_conv_b, v_od_ln_g, v_od_ln_b, v_final_g]))
    bsz, length, d = x.shape
    ctx_len = ctx.shape[1]
    me = _block_index(_mesh_pos())

    cond_local = jnp.concatenate([c, c_ctx[None], jnp.zeros((COND_ROWS - bsz - 1, d), F32)], axis=0)
    cond = _all_gather(cond_local, "gather_cond").reshape(N_DEV * COND_ROWS, d)
    silu_op = _rowwise("silu", _f_silu, [(d, F32)], N_DEV * COND_ROWS)
    silu_rows, silu_pullback = jax.vjp(lambda r: silu_op((r[None],), (), (), (), ())[0][0], cond)
    mod_cols = ada_w.shape[2]
    mod_part = jnp.concatenate([_mm(silu_rows, ada_w[i], "nn", (F32,), f"ada{i}_fwd")[0] for i in range(2)], axis=0)
    mod_all = _all_gather(mod_part, "gather_mod").reshape(N_DEV, 2, N_DEV * COND_ROWS, mod_cols)
    modrows = []
    for i in range(2):
        whole = mod_all[:, i].transpose(1, 0, 2).reshape(N_DEV * COND_ROWS, N_DEV * mod_cols) + ada_b[i]
        modrows.append(lax.dynamic_slice_in_dim(whole, me * COND_ROWS, COND_ROWS, axis=0)[:bsz + 1])

    big_pack = _Packing([(n, local[n].size) for n, _ in _BIG], ROW_ALIGN, SUM_ROWS)
    gathered = _all_gather(big_pack.pack({n: local[n].astype(BF16) for n, _ in _BIG}), "gather_weights")
    tiny_pack = _Packing([(n, local[n].size) for n, _ in _SMALL_SHARDED], 8)
    gathered_tiny = _all_gather(tiny_pack.pack({n: local[n] for n, _ in _SMALL_SHARDED}), "gather_small_weights")
    full = {n: local[n] for n in _REPLICATED}
    for n, axis in _BIG:
        full[n] = _unshard(big_pack.piece(gathered, n, (N_DEV,)).reshape((N_DEV,) + local[n].shape), axis)
    for n, axis in _SMALL_SHARDED:
        full[n] = _unshard(tiny_pack.piece(gathered_tiny, n, (N_DEV,)).reshape((N_DEV,) + local[n].shape), axis)

    xall = jnp.concatenate([ctx, x], axis=1)
    diff = {n: full[n] for n in names if n not in ("c_ctx", "final_g", "ada_w", "ada_b")}
    x4, pullback = jax.vjp(lambda xall_, modrows_, weights: _forward(xall_, modrows_, weights, bsz, length, ctx_len), xall, modrows, diff)
    loss_part, dx4, dfinal = _loss_head(x4, loss_target, final_g[None])
    dxall, dmodrows, grads = pullback(dx4)
    grads = dict(grads)
    grads["final_g"] = dfinal[0]
    loss = lax.psum(loss_part[0, 0], MESH_AXES)
    grad_x = dxall[:, ctx_len:]

    dmod_local = jnp.concatenate([jnp.pad(dm, ((0, COND_ROWS - bsz - 1), (0, 0))) for dm in dmodrows], axis=0)
    dmod_all = _all_gather(dmod_local, "gather_dmod").reshape(N_DEV, 2, COND_ROWS, N_DEV * mod_cols)
    reduced = {}
    grad_ada_w, grad_ada_b, dmod_mine = [], [], []
    for i in range(2):
        dmod = dmod_all[:, i].reshape(N_DEV * COND_ROWS, N_DEV * mod_cols)
        grad_ada_b.append(_sum_rows(dmod, f"ada{i}_db")[0])
        dmod_mine.append(lax.dynamic_slice_in_dim(dmod, me * mod_cols, mod_cols, axis=1))
        grad_ada_w.append(_mm(silu_rows, dmod_mine[i], "tn", (F32,), f"ada{i}_dw")[0])
    reduced["ada_w"], reduced["ada_b"] = jnp.stack(grad_ada_w), jnp.stack(grad_ada_b)
    dsilu = _mm(jnp.concatenate(dmod_mine, axis=1), jnp.concatenate([ada_w[0], ada_w[1]], axis=1), "nt", (F32,), "ada_dx")[0]
    (dcond,) = silu_pullback(dsilu)
    grads["c_ctx"] = _sum_rows(dcond.reshape(N_DEV, COND_ROWS, d)[:, bsz], "c_ctx_rows")[0]

    packed = big_pack.pack_blocks({n: _shard_blocks(grads[n], axis).reshape(N_DEV, -1) for n, axis in _BIG})
    summed = _sum_blocks(_exchange_blocks(packed, "exchange_grads"), "sum_grads")
    for n, _ in _BIG:
        reduced[n] = big_pack.piece(summed, n).reshape(local[n].shape)

    small_names = list(_REPLICATED) + [n for n, _ in _SMALL_SHARDED]
    small_pack = _Packing([(n, full[n].size) for n in small_names], 8)
    small_sum = _sum_blocks(_all_gather(small_pack.pack({n: grads[n].astype(F32) for n in small_names}), "gather_small_grads"), "sum_small_grads")
    for n in _REPLICATED:
        reduced[n] = small_pack.piece(small_sum, n).reshape(local[n].shape)
    for n, axis in _SMALL_SHARDED:
        whole = small_pack.piece(small_sum, n).reshape(full[n].shape)
        reduced[n] = lax.dynamic_slice_in_dim(whole, me * local[n].shape[axis], local[n].shape[axis], axis=axis)

    delta, new_m, new_v = {}, {}, {}
    for n in names:
        delta[n], new_m[n], new_v[n] = _adamw(local[n], reduced[n], mom1[n], mom2[n], "adamw_" + n)
    return (loss, grad_x, *[reduced[n] for n in names], *[delta[n] for n in names], *[new_m[n] for n in names], *[new_v[n] for n in names])
```

```python
import functools
import math

import jax
import jax.numpy as jnp
from jax import lax
from jax.experimental import pallas as pl
from jax.experimental.pallas import tpu as pltpu

F32, BF16 = jnp.float32, jnp.bfloat16

EPS = 1e-6
GRID_W = 64
ROPE_THETA = 10000.0
A_HEAD_DIM, A_Q_HEADS, A_KV_HEADS = 64, 8, 2
B_GROUPS, B_GROUP_DIM, B_CHUNK = 8, 64, 128
C_HEADS, C_NOPE, C_ROPE, C_V, C_Q_RANK, C_KV_RANK = 8, 64, 32, 64, 256, 128
D_CONV = 31
CONV_PAD = D_CONV // 2
N_MOD = 6
N_DEV = 8
MESH_AXES = ("x", "y", "c")

ADAM_LR, ADAM_B1, ADAM_B2, ADAM_EPS, ADAM_WD, ADAM_STEP = 0.001, 0.9, 0.999, 1e-08, 0.01, 10

VMEM_LIMIT = 56 * 1024 * 1024
PACK_COLS = 1024
ROW_ALIGN = 16
PACK_ROWS = 32
COND_ROWS = 8


def _pcall(body, **kw):
    return pl.pallas_call(body, **kw)


def _params(sem=None):
    return pltpu.CompilerParams(dimension_semantics=sem, vmem_limit_bytes=VMEM_LIMIT)


def _pick(n, cands):
    for c in cands:
        if n % c == 0:
            return c
    return n


def _mm(a, b, mode, out_dtypes, name, epi=None, extras=()):
    if mode == "tn":
        kk, m = a.shape
    else:
        m, kk = a.shape
    n = b.shape[0] if mode == "nt" else b.shape[1]
    tm = _pick(m, (1152, 1024, 768, 512, 256, 128))
    tn = _pick(n, (1024, 896, 768, 512, 256, 128))
    tk = kk if kk <= 1024 else _pick(kk, (1024, 896, 768, 512, 256, 128))
    nk = kk // tk
    ne, no = len(extras), len(out_dtypes)
    a_spec = pl.BlockSpec((tk, tm), lambda i, j, k: (k, i)) if mode == "tn" else pl.BlockSpec((tm, tk), lambda i, j, k: (i, k))
    b_spec = pl.BlockSpec((tn, tk), lambda i, j, k: (j, k)) if mode == "nt" else pl.BlockSpec((tk, tn), lambda i, j, k: (k, j))
    t_spec = pl.BlockSpec((tm, tn), lambda i, j, k: (i, j))
    dn = {"nn": ((1,), (0,)), "nt": ((1,), (1,)), "tn": ((0,), (0,))}[mode]

    def body(a_ref, b_ref, *rest):
        extra_refs, out_refs = rest[:ne], rest[ne:ne + no]

        def finish(acc):
            outs = (acc,) if epi is None else epi(acc, *[r[...] for r in extra_refs])
            for r, o in zip(out_refs, outs):
                r[...] = o.astype(r.dtype)

        part = lax.dot_general(a_ref[...].astype(BF16), b_ref[...].astype(BF16), (dn, ((), ())), preferred_element_type=F32)
        if nk == 1:
            finish(part)
        else:
            acc_ref = rest[-1]
            k = pl.program_id(2)

            @pl.when(k == 0)
            def _():
                acc_ref[...] = part

            @pl.when(k > 0)
            def _():
                acc_ref[...] += part

            @pl.when(k == nk - 1)
            def _():
                finish(acc_ref[...])

    outs = _pcall(
        body, name=name, grid=(m // tm, n // tn, nk),
        in_specs=[a_spec, b_spec] + [t_spec] * ne,
        out_specs=[t_spec] * no,
        out_shape=[jax.ShapeDtypeStruct((m, n), d) for d in out_dtypes],
        scratch_shapes=[pltpu.VMEM((tm, tn), F32)] if nk > 1 else [],
        compiler_params=_params(("parallel", "parallel", "arbitrary")),
    )(a, b, *extras)
    return outs


def _linear(name, out_dtype=F32):
    @jax.custom_vjp
    def op(x, w):
        return _mm(x, w, "nn", (out_dtype,), name + "_fwd")[0]

    def fwd(x, w):
        return op(x, w), (x, w)

    def bwd(res, dy):
        x, w = res
        dx = _mm(dy, w, "nt", (x.dtype,), name + "_dx")[0]
        dw = _mm(x, dy, "tn", (w.dtype,), name + "_dw")[0]
        return dx, dw

    op.defvjp(fwd, bwd)
    return op


def _relu2_epi(acc):
    return jnp.square(jnp.maximum(acc, 0.0)), acc


def _relu2_bwd_epi(acc, a):
    return (acc * (2.0 * jnp.maximum(a.astype(F32), 0.0)),)


def _mlp(name):
    @jax.custom_vjp
    def op(h, w1, w2):
        s, _ = _mm(h, w1, "nn", (BF16, BF16), name + "_up", epi=_relu2_epi)
        return _mm(s, w2, "nn", (F32,), name + "_down")[0]

    def fwd(h, w1, w2):
        s, a = _mm(h, w1, "nn", (BF16, BF16), name + "_up", epi=_relu2_epi)
        return _mm(s, w2, "nn", (F32,), name + "_down")[0], (h, w1, w2, s, a)

    def bwd(res, dy):
        h, w1, w2, s, a = res
        da = _mm(dy, w2, "nt", (BF16,), name + "_ds", epi=_relu2_bwd_epi, extras=(a,))[0]
        dw2 = _mm(s, dy, "tn", (w2.dtype,), name + "_dw2")[0]
        dw1 = _mm(h, da, "tn", (w1.dtype,), name + "_dw1")[0]
        dh = _mm(da, w1, "nt", (h.dtype,), name + "_dh")[0]
        return dh, dw1, dw2

    op.defvjp(fwd, bwd)
    return op


def _dot_exact(x, c):
    return jnp.dot(x, c, precision=lax.Precision.HIGHEST, preferred_element_type=F32)


@jax.custom_vjp
def _bdot(a, b):
    return jnp.dot(a.astype(BF16), b.astype(BF16), preferred_element_type=F32)


def _bdot_fwd(a, b):
    return _bdot(a, b), (a, b)


def _bdot_bwd(res, g):
    a, b = res
    gb = g.astype(BF16)
    da = lax.dot_general(gb, b.astype(BF16), (((1,), (1,)), ((), ())), preferred_element_type=F32)
    db = lax.dot_general(a.astype(BF16), gb, (((0,), (0,)), ((), ())), preferred_element_type=F32)
    return da, db


_bdot.defvjp(_bdot_fwd, _bdot_bwd)


def _rowwise(name, f, out_specs, tl, ctx_blocks=0):
    def seg(l, s):
        return jnp.where(l >= ctx_blocks, s - 1, 0) if s > 1 else 0

    def specs(rows, tabs, pers, glbs, consts):
        row_specs = [pl.BlockSpec((1, tl, r.shape[2]), lambda b, l: (b, l, 0)) for r in rows]
        tab_specs = [pl.BlockSpec((tl, t.shape[1]), lambda b, l: (l, 0)) for t in tabs]
        per_specs = [pl.BlockSpec((1, 1, 1, p.shape[3]), functools.partial(lambda b, l, s: (b, seg(l, s), 0, 0), s=p.shape[1])) for p in pers]
        glb_specs = [pl.BlockSpec(g.shape, functools.partial(lambda b, l, nd: (0,) * nd, nd=g.ndim)) for g in glbs]
        const_specs = [pl.BlockSpec(c.shape, functools.partial(lambda b, l, nd: (0,) * nd, nd=c.ndim)) for c in consts]
        return row_specs, tab_specs, per_specs, glb_specs, const_specs

    def load(refs_rows, refs_tabs, refs_pers, refs_glbs, refs_consts):
        return (tuple(r[0].astype(F32) for r in refs_rows), tuple(t[...] for t in refs_tabs),
                tuple(p[0, 0].astype(F32) for p in refs_pers), tuple(g[...].astype(F32) for g in refs_glbs),
                tuple(c[...] for c in refs_consts))

    def call_fwd(rows, tabs, pers, glbs, consts):
        bsz, length = rows[0].shape[:2]
        nr, nt, npp, ng, nc = len(rows), len(tabs), len(pers), len(glbs), len(consts)
        rs, ts, ps, gs, cs = specs(rows, tabs, pers, glbs, consts)

        def body(*refs):
            ins, outs = refs[:nr + nt + npp + ng + nc], refs[nr + nt + npp + ng + nc:]
            r, t, p, g, c = load(ins[:nr], ins[nr:nr + nt], ins[nr + nt:nr + nt + npp], ins[nr + nt + npp:nr + nt + npp + ng], ins[nr + nt + npp + ng:])
            for o_ref, o in zip(outs, f(r, t, p, g, c)):
                o_ref[0] = o.astype(o_ref.dtype)

        return _pcall(
            body, name=name + "_fwd", grid=(bsz, length // tl),
            in_specs=rs + ts + ps + gs + cs,
            out_specs=[pl.BlockSpec((1, tl, w), lambda b, l: (b, l, 0)) for w, _ in out_specs],
            out_shape=[jax.ShapeDtypeStruct((bsz, length, w), d) for w, d in out_specs],
            compiler_params=_params(("parallel", "parallel")),
        )(*rows, *tabs, *pers, *glbs, *consts)

    def call_bwd(rows, tabs, pers, glbs, consts, cts):
        bsz, length = rows[0].shape[:2]
        nr, nt, npp, ng, nc, no = len(rows), len(tabs), len(pers), len(glbs), len(consts), len(cts)
        rs, ts, ps, gs, cs = specs(rows, tabs, pers, glbs, consts)
        n_in = nr + nt + npp + ng + nc

        def body(*refs):
            ins, ct_refs, outs = refs[:n_in], refs[n_in:n_in + no], refs[n_in + no:]
            r, t, p, g, c = load(ins[:nr], ins[nr:nr + nt], ins[nr + nt:nr + nt + npp], ins[nr + nt + npp:nr + nt + npp + ng], ins[nr + nt + npp + ng:])
            _, vjp = jax.vjp(lambda r_, p_, g_: tuple(f(r_, t, p_, g_, c)), r, p, g)
            dr, dp, dg = vjp(tuple(ct[0].astype(F32) for ct in ct_refs))
            dr_refs, dp_refs, dg_refs = outs[:nr], outs[nr:nr + npp], outs[nr + npp:]
            for ref, d in zip(dr_refs, dr):
                ref[0] = d.astype(ref.dtype)
            b, l = pl.program_id(0), pl.program_id(1)
            first_of_segment = (l == 0) | (l == ctx_blocks)
            for ref, d in zip(dp_refs, dp):
                @pl.when(first_of_segment)
                def _(ref=ref, d=d):
                    ref[0, 0] = d

                @pl.when(jnp.logical_not(first_of_segment))
                def _(ref=ref, d=d):
                    ref[0, 0] += d
            first = (b == 0) & (l == 0)
            for ref, d in zip(dg_refs, dg):
                @pl.when(first)
                def _(ref=ref, d=d):
                    ref[...] = d

                @pl.when(jnp.logical_not(first))
                def _(ref=ref, d=d):
                    ref[...] += d

        ct_specs = [pl.BlockSpec((1, tl, w), lambda b, l: (b, l, 0)) for w, _ in out_specs]
        outs = _pcall(
            body, name=name + "_bwd", grid=(bsz, length // tl),
            in_specs=rs + ts + ps + gs + cs + ct_specs,
            out_specs=rs + ps + gs,
            out_shape=[jax.ShapeDtypeStruct(r.shape, r.dtype) for r in rows]
            + [jax.ShapeDtypeStruct(p.shape, F32) for p in pers] + [jax.ShapeDtypeStruct(g.shape, F32) for g in glbs],
            compiler_params=_params(("arbitrary", "arbitrary")),
        )(*rows, *tabs, *pers, *glbs, *consts, *cts)
        return tuple(outs[:nr]), tuple(outs[nr:nr + npp]), tuple(outs[nr + npp:])

    @jax.custom_vjp
    def op(rows, tabs, pers, glbs, consts):
        return tuple(call_fwd(rows, tabs, pers, glbs, consts))

    def fwd(rows, tabs, pers, glbs, consts):
        return op(rows, tabs, pers, glbs, consts), (rows, tabs, pers, glbs, consts)

    def bwd(res, cts):
        rows, tabs, pers, glbs, consts = res
        dr, dp, dg = call_bwd(rows, tabs, pers, glbs, consts, tuple(cts))
        dp = tuple(d.astype(p.dtype) for d, p in zip(dp, pers))
        dg = tuple(d.astype(g.dtype) for d, g in zip(dg, glbs))
        return dr, tuple(jnp.zeros_like(t) for t in tabs), dp, dg, tuple(jnp.zeros_like(c) for c in consts)

    op.defvjp(fwd, bwd)
    return op


def _rms(x, g):
    return x * lax.rsqrt(jnp.mean(x * x, axis=-1, keepdims=True) + EPS) * g


def _f_silu(r, t, p, g, c):
    return (jax.nn.silu(r[0]),)


def _f_modulate(r, t, p, g, c):
    shift, scale = p
    return (_rms(r[0], g[0]) * (1.0 + scale) + shift,)


def _f_res_modulate(r, t, p, g, c):
    x, y = r
    gate, shift, scale = p
    xn = x + gate * y
    return xn, _rms(xn, g[0]) * (1.0 + scale) + shift


def _f_res(r, t, p, g, c):
    return (r[0] + p[0] * r[1],)


def _f_headnorm_rope(r, t, p, g, c):
    x = r[0]
    cos, sin = t
    group_mean, swap = c
    xn = x * lax.rsqrt(_dot_exact(x * x, group_mean) + EPS) * g[0]
    return (xn * cos + _dot_exact(xn, swap) * sin,)


def _f_rope(r, t, p, g, c):
    x = r[0]
    cos, sin = t
    return (x * cos + _dot_exact(x, c[0]) * sin,)


def _f_rms(r, t, p, g, c):
    return (_rms(r[0], g[0]),)


def _f_sgu_pre(r, t, p, g, c):
    u = jax.nn.gelu(r[0])
    v = jax.nn.gelu(r[1])
    vn = v * lax.rsqrt(_dot_exact(v * v, c[0]) + EPS) * g[0]
    return u, vn


def _f_sgu_mix(r, t, p, g, c):
    u, vn = r
    bias = g[B_GROUPS]
    group = lax.broadcasted_iota(jnp.int32, vn.shape, 1) // B_GROUP_DIM
    sv = bias
    for k in range(B_GROUPS):
        sv = sv + jnp.where(group == k, _bdot(g[k], vn), 0.0)
    return (u * sv,)


def _f_glu(r, t, p, g, c):
    return (r[0] * jax.nn.sigmoid(r[1]),)


def _f_ln_silu(r, t, p, g, c):
    x = r[0]
    mu = jnp.mean(x, axis=-1, keepdims=True)
    var = jnp.mean(jnp.square(x - mu), axis=-1, keepdims=True)
    return (jax.nn.silu((x - mu) * lax.rsqrt(var + EPS) * g[0] + g[1]),)


def _attention(name, ctx_len):
    def tiles(q, v):
        lq = q.shape[2]
        tq = _pick(lq, (256, 128))
        if ctx_len:
            tq = math.gcd(tq, ctx_len)
        return tq, ctx_len // tq

    def probs(qq, kk, scale):
        s = lax.dot_general(qq, kk, (((1,), (1,)), ((), ())), preferred_element_type=F32) * scale
        p = jnp.exp(s - jnp.max(s, axis=-1, keepdims=True))
        return p, jnp.sum(p, axis=-1, keepdims=True)

    def by_segment(qi, cb, lk, run):
        if cb > 0:
            @pl.when(qi < cb)
            def _():
                run(ctx_len)

            @pl.when(qi >= cb)
            def _():
                run(lk)
        else:
            run(lk)

    def call_fwd(q, k, v):
        bsz, hq, lq, dk = q.shape
        _, hk, lk, dv = v.shape
        grp = hq // hk
        tq, cb = tiles(q, v)
        scale = dk ** -0.5

        def body(q_ref, k_ref, v_ref, o_ref):
            def run(nk):
                p, l = probs(q_ref[0, 0], k_ref[0, 0, :nk], scale)
                o = jnp.dot(p.astype(BF16), v_ref[0, 0, :nk], preferred_element_type=F32) / l
                o_ref[0, 0] = o.astype(o_ref.dtype)

            by_segment(pl.program_id(2), cb, lk, run)

        return _pcall(
            body, name=name + "_fwd", grid=(bsz, hq, lq // tq),
            in_specs=[pl.BlockSpec((1, 1, tq, dk), lambda b, h, i: (b, h, i, 0)),
                      pl.BlockSpec((1, 1, lk, dk), lambda b, h, i: (b, h // grp, 0, 0)),
                      pl.BlockSpec((1, 1, lk, dv), lambda b, h, i: (b, h // grp, 0, 0))],
            out_specs=pl.BlockSpec((1, 1, tq, dv), lambda b, h, i: (b, h, i, 0)),
            out_shape=jax.ShapeDtypeStruct((bsz, hq, lq, dv), BF16),
            compiler_params=_params(("parallel", "parallel", "parallel")),
        )(q, k, v)

    def call_bwd(q, k, v, do):
        bsz, hq, lq, dk = q.shape
        _, hk, lk, dv = v.shape
        grp = hq // hk
        tq, cb = tiles(q, v)
        scale = dk ** -0.5

        def body(q_ref, k_ref, v_ref, do_ref, dq_ref, dk_ref, dv_ref):
            g, qi = pl.program_id(2), pl.program_id(3)

            @pl.when((g == 0) & (qi == 0))
            def _():
                dk_ref[...] = jnp.zeros_like(dk_ref)
                dv_ref[...] = jnp.zeros_like(dv_ref)

            def run(nk):
                qq, kk, vv, dd = q_ref[0, 0], k_ref[0, 0, :nk], v_ref[0, 0, :nk], do_ref[0, 0]
                p, l = probs(qq, kk, scale)
                pn = p / l
                dp = lax.dot_general(dd, vv, (((1,), (1,)), ((), ())), preferred_element_type=F32)
                ds = (pn * (dp - jnp.sum(pn * dp, axis=-1, keepdims=True)) * scale).astype(BF16)
                dq_ref[0, 0] = jnp.dot(ds, kk, preferred_element_type=F32).astype(dq_ref.dtype)
                dk_ref[0, 0, :nk] += lax.dot_general(ds, qq, (((0,), (0,)), ((), ())), preferred_element_type=F32)
                dv_ref[0, 0, :nk] += lax.dot_general(pn.astype(BF16), dd, (((0,), (0,)), ((), ())), preferred_element_type=F32)

            by_segment(qi, cb, lk, run)

        q_spec = pl.BlockSpec((1, 1, tq, dk), lambda b, h, g, i: (b, h * grp + g, i, 0))
        k_spec = pl.BlockSpec((1, 1, lk, dk), lambda b, h, g, i: (b, h, 0, 0))
        v_spec = pl.BlockSpec((1, 1, lk, dv), lambda b, h, g, i: (b, h, 0, 0))
        do_spec = pl.BlockSpec((1, 1, tq, dv), lambda b, h, g, i: (b, h * grp + g, i, 0))
        return _pcall(
            body, name=name + "_bwd", grid=(bsz, hk, grp, lq // tq),
            in_specs=[q_spec, k_spec, v_spec, do_spec],
            out_specs=[q_spec, k_spec, v_spec],
            out_shape=[jax.ShapeDtypeStruct(q.shape, BF16), jax.ShapeDtypeStruct(k.shape, F32), jax.ShapeDtypeStruct(v.shape, F32)],
            compiler_params=_params(("parallel", "parallel", "arbitrary", "arbitrary")),
        )(q, k, v, do)

    @jax.custom_vjp
    def op(q, k, v):
        return call_fwd(q, k, v)

    def fwd(q, k, v):
        return op(q, k, v), (q, k, v)

    def bwd(res, do):
        q, k, v = res
        dq, dk, dv = call_bwd(q, k, v, do)
        return dq, dk.astype(k.dtype), dv.astype(v.dtype)

    op.defvjp(fwd, bwd)
    return op


def _conv_call(ypad, taps, name):
    bsz, lp, ch = ypad.shape
    length = lp - 2 * ROW_ALIGN
    tl = _pick(length, (256, 128))

    def body(y_ref, w_ref, o_ref):
        base = pl.multiple_of(pl.program_id(1) * tl, tl)
        win = y_ref[0, pl.ds(base, tl + 2 * ROW_ALIGN), :]
        acc = jnp.broadcast_to(w_ref[pl.ds(D_CONV, 1), :], (tl, ch))
        for k in range(D_CONV):
            acc = acc + win[k:k + tl, :] * w_ref[pl.ds(k, 1), :]
        o_ref[0] = acc

    return _pcall(
        body, name=name, grid=(bsz, length // tl),
        in_specs=[pl.BlockSpec((1, lp, ch), lambda b, l: (b, 0, 0)), pl.BlockSpec((D_CONV + 1, ch), lambda b, l: (0, 0))],
        out_specs=pl.BlockSpec((1, tl, ch), lambda b, l: (b, l, 0)),
        out_shape=jax.ShapeDtypeStruct((bsz, length, ch), F32),
        compiler_params=_params(("parallel", "parallel")),
    )(ypad, taps)


def _conv_dw_call(ypad, dout, name):
    bsz, lp, ch = ypad.shape
    length = lp - 2 * ROW_ALIGN
    tl = _pick(length, (256, 128))

    def body(y_ref, d_ref, o_ref):
        b, l = pl.program_id(0), pl.program_id(1)

        @pl.when((b == 0) & (l == 0))
        def _():
            o_ref[...] = jnp.zeros_like(o_ref)

        base = pl.multiple_of(l * tl, tl)
        win = y_ref[0, pl.ds(base, tl + 2 * ROW_ALIGN), :]
        dd = d_ref[0]
        for k in range(D_CONV):
            o_ref[pl.ds(k, 1), :] += jnp.sum(win[k:k + tl, :] * dd, axis=0, keepdims=True)
        o_ref[pl.ds(D_CONV, 1), :] += jnp.sum(dd, axis=0, keepdims=True)

    return _pcall(
        body, name=name, grid=(bsz, length // tl),
        in_specs=[pl.BlockSpec((1, lp, ch), lambda b, l: (b, 0, 0)), pl.BlockSpec((1, tl, ch), lambda b, l: (b, l, 0))],
        out_specs=pl.BlockSpec((D_CONV + 1, ch), lambda b, l: (0, 0)),
        out_shape=jax.ShapeDtypeStruct((D_CONV + 1, ch), F32),
        compiler_params=_params(("arbitrary", "arbitrary")),
    )(ypad, dout)


def _pad_rows(y):
    return jnp.pad(y, ((0, 0), (CONV_PAD, 2 * ROW_ALIGN - CONV_PAD), (0, 0)))


@jax.custom_vjp
def _dwconv(y, taps):
    return _conv_call(_pad_rows(y), taps, "conv_fwd")


def _dwconv_fwd(y, taps):
    return _dwconv(y, taps), (y, taps)


def _dwconv_bwd(res, dout):
    y, taps = res
    flipped = jnp.concatenate([taps[:D_CONV][::-1], jnp.zeros_like(taps[D_CONV:])], axis=0)
    dy = _conv_call(_pad_rows(dout), flipped, "conv_dy")
    dtaps = _conv_dw_call(_pad_rows(y), dout, "conv_dw")
    return dy, dtaps


_dwconv.defvjp(_dwconv_fwd, _dwconv_bwd)


def _loss_head(x, target, g):
    bsz, length, d = x.shape
    tl = _pick(length, (256, 128))

    def f(xb, tb, gb):
        err = _rms(xb, gb) - tb
        return 0.5 * jnp.sum(jnp.sum(err * err, axis=-1, keepdims=True), axis=0, keepdims=True) / d

    def body(x_ref, t_ref, g_ref, loss_ref, dx_ref, dg_ref):
        val, vjp = jax.vjp(lambda xb, gb: f(xb, t_ref[0], gb), x_ref[0], g_ref[...])
        dx, dg = vjp(jnp.ones((1, 1), F32))
        dx_ref[0] = dx
        first = (pl.program_id(0) == 0) & (pl.program_id(1) == 0)

        @pl.when(first)
        def _():
            loss_ref[...] = val
            dg_ref[...] = dg

        @pl.when(jnp.logical_not(first))
        def _():
            loss_ref[...] += val
            dg_ref[...] += dg

    row = pl.BlockSpec((1, tl, d), lambda b, l: (b, l, 0))
    return _pcall(
        body, name="loss_head", grid=(bsz, length // tl),
        in_specs=[row, row, pl.BlockSpec((1, d), lambda b, l: (0, 0))],
        out_specs=[pl.BlockSpec((1, 1), lambda b, l: (0, 0)), row, pl.BlockSpec((1, d), lambda b, l: (0, 0))],
        out_shape=[jax.ShapeDtypeStruct((1, 1), F32), jax.ShapeDtypeStruct(x.shape, F32), jax.ShapeDtypeStruct((1, d), F32)],
        compiler_params=_params(("arbitrary", "arbitrary")),
    )(x, target, g)


def _adamw(w, g, m, v, name):
    shape = w.shape
    cols = shape[-1]
    rows = w.size // cols
    tr = _pick(rows, (512, 256, 128))
    w2, g2, m2, v2 = (t.reshape(rows, cols) for t in (w, g, m, v))

    def body(w_ref, g_ref, m_ref, v_ref, d_ref, nm_ref, nv_ref):
        gg = g_ref[...]
        nm = ADAM_B1 * m_ref[...] + (1.0 - ADAM_B1) * gg
        nv = ADAM_B2 * v_ref[...] + (1.0 - ADAM_B2) * jnp.square(gg)
        m_hat = nm / (1.0 - ADAM_B1 ** ADAM_STEP)
        v_hat = nv / (1.0 - ADAM_B2 ** ADAM_STEP)
        d_ref[...] = -ADAM_LR * (m_hat / (jnp.sqrt(v_hat) + ADAM_EPS) + ADAM_WD * w_ref[...])
        nm_ref[...] = nm
        nv_ref[...] = nv

    spec = pl.BlockSpec((tr, cols), lambda i: (i, 0))
    outs = _pcall(
        body, name=name, grid=(rows // tr,), in_specs=[spec] * 4, out_specs=[spec] * 3,
        out_shape=[jax.ShapeDtypeStruct((rows, cols), F32)] * 3,
        compiler_params=_params(("parallel",)),
    )(w2, g2, m2, v2)
    return tuple(o.reshape(shape) for o in outs)


def _mesh_pos():
    return lax.axis_index("x"), lax.axis_index("y"), lax.axis_index("c")


_RELATIONS = [(dx, dy, dc) for dx in (0, 1) for dy in (0, 1) for dc in (0, 1)][1:]


def _peer(pos, rel):
    return tuple(jnp.where(r == 1, 1 - p, p) if r else p for p, r in zip(pos, rel))


def _block_index(pos):
    return 4 * pos[0] + 2 * pos[1] + pos[2]


_HBM = pl.BlockSpec(memory_space=pltpu.HBM)


def _all_gather(x, name):
    rows, cols = x.shape

    def body(x_ref, out_ref, send_sems, recv_sems, local_sem):
        x_, y_, c_ = _mesh_pos()
        me, sibling = (x_, y_, c_), (x_, y_, 1 - c_)
        chips = [(1 - x_, y_), (x_, 1 - y_), (1 - x_, 1 - y_)]

        def slot(pos):
            return out_ref.at[_block_index(pos)]

        def copy(k, block, to, src=None):
            return pltpu.make_async_remote_copy(
                src_ref=slot(block) if src is None else src, dst_ref=slot(block),
                send_sem=send_sems.at[k], recv_sem=recv_sems.at[k], device_id=to, device_id_type=pl.DeviceIdType.MESH)

        mine = pltpu.make_async_copy(x_ref, slot(me), local_sem)
        mine.start()
        first = [copy(0, me, sibling, src=x_ref)] + [copy(1 + j, me, (*chip, c_), src=x_ref) for j, chip in enumerate(chips)]
        for cp in first:
            cp.start()
        passed = [copy(4 + j, (*chip, c_), sibling) for j, chip in enumerate(chips)]
        for j, chip in enumerate(chips):
            copy(1 + j, (*chip, c_), me).wait_recv()
            passed[j].start()
        copy(0, sibling, me).wait_recv()
        for j, chip in enumerate(chips):
            copy(4 + j, (*chip, 1 - c_), me).wait_recv()
        for cp in first + passed:
            cp.wait_send()
        mine.wait()

    return _pcall(
        body, name=name, in_specs=[_HBM], out_specs=_HBM,
        out_shape=jax.ShapeDtypeStruct((N_DEV, rows, cols), x.dtype),
        scratch_shapes=[pltpu.SemaphoreType.DMA((7,)), pltpu.SemaphoreType.DMA((7,)), pltpu.SemaphoreType.DMA(())],
    )(x)


def _exchange_blocks(p, name):
    _, rows, cols = p.shape

    def body(p_ref, out_ref, send_sems, recv_sems, local_sem):
        me = _mesh_pos()
        mine = pltpu.make_async_copy(p_ref.at[_block_index(me)], out_ref.at[_block_index(me)], local_sem)
        mine.start()
        sends = []
        for k, rel in enumerate(_RELATIONS):
            peer = _peer(me, rel)
            sends.append(pltpu.make_async_remote_copy(
                src_ref=p_ref.at[_block_index(peer)], dst_ref=out_ref.at[_block_index(me)],
                send_sem=send_sems.at[k], recv_sem=recv_sems.at[k], device_id=peer, device_id_type=pl.DeviceIdType.MESH))
        for cp in sends:
            cp.start()
        for k, rel in enumerate(_RELATIONS):
            peer = _peer(me, rel)
            pltpu.make_async_remote_copy(
                src_ref=p_ref.at[_block_index(peer)], dst_ref=out_ref.at[_block_index(peer)],
                send_sem=send_sems.at[k], recv_sem=recv_sems.at[k], device_id=peer, device_id_type=pl.DeviceIdType.MESH).wait_recv()
        for cp in sends:
            cp.wait_send()
        mine.wait()

    return _pcall(
        body, name=name, in_specs=[_HBM], out_specs=_HBM,
        out_shape=jax.ShapeDtypeStruct(p.shape, p.dtype),
        scratch_shapes=[pltpu.SemaphoreType.DMA((7,)), pltpu.SemaphoreType.DMA((7,)), pltpu.SemaphoreType.DMA(())],
    )(p)


_SEM = pl.BlockSpec(memory_space=pltpu.SEMAPHORE)
_EFFECT = pltpu.SideEffectType.DATAFLOW_SIDE_EFFECTING


def _push_start(src, name):
    per_peer = src.ndim == 3
    rows, cols = src.shape[-2:]
    land_shape = (N_DEV, rows, cols)

    def body(src_ref, land_ref, send_sem, recv_sem, src_thru, land_thru, token):
        me = _mesh_pos()
        for rel in _RELATIONS:
            peer = _peer(me, rel)
            pltpu.make_async_remote_copy(
                src_ref=src_ref.at[_block_index(peer)] if per_peer else src_ref, dst_ref=land_ref.at[_block_index(me)],
                send_sem=send_sem, recv_sem=recv_sem, device_id=peer, device_id_type=pl.DeviceIdType.MESH).start()
        token[...] = jnp.zeros_like(token)

    send_sem, recv_sem, src_thru, land_thru, token = _pcall(
        body, name=name,
        out_shape=(pltpu.SemaphoreType.DMA(()), pltpu.SemaphoreType.DMA(()), pltpu.HBM(src.shape, src.dtype),
                   pltpu.HBM(land_shape, src.dtype), jax.ShapeDtypeStruct((8, 128), F32)),
        in_specs=(_HBM, _HBM), out_specs=(_SEM, _SEM, _HBM, _HBM, pl.BlockSpec(memory_space=pltpu.VMEM)),
        input_output_aliases={0: 2, 1: 3}, compiler_params=pltpu.CompilerParams(has_side_effects=_EFFECT),
    )(pltpu.with_memory_space_constraint(src, pltpu.HBM), pltpu.with_memory_space_constraint(lax.empty(land_shape, src.dtype), pltpu.HBM))
    return (send_sem, recv_sem, src_thru, land_thru), token[0, 0]


def _push_wait(handle, after, own, me, name):
    send_sem, recv_sem, src_thru, land_thru = handle

    def body(src_ref, land_ref, send_sem, recv_sem, after_ref, src_dead, land_out):
        sent = land_ref.at[pl.ds(0, N_DEV - 1)]
        all_seven = pltpu.make_async_remote_copy(src_ref=sent, dst_ref=sent, send_sem=send_sem, recv_sem=recv_sem,
                                                 device_id=_mesh_pos(), device_id_type=pl.DeviceIdType.MESH)
        all_seven.wait_send()
        all_seven.wait_recv()

    land = _pcall(
        body, name=name,
        out_shape=(pltpu.HBM(src_thru.shape, src_thru.dtype), pltpu.HBM(land_thru.shape, land_thru.dtype)),
        in_specs=(_HBM, _HBM, _SEM, _SEM, pl.BlockSpec(memory_space=pl.ANY)), out_specs=(_HBM, _HBM),
        input_output_aliases={0: 0, 1: 1}, compiler_params=pltpu.CompilerParams(has_side_effects=_EFFECT),
    )(src_thru, land_thru, send_sem, recv_sem, after)[1]
    return lax.dynamic_update_slice(land, own[None], (me, 0, 0))


def _sum_blocks(p, name):
    n, rows, cols = p.shape
    tr = _pick(rows, (256, 128, 64, PACK_ROWS, 16, 8))

    def body(p_ref, o_ref):
        acc = p_ref[0].astype(F32)
        for s in range(1, n):
            acc = acc + p_ref[s].astype(F32)
        o_ref[...] = acc

    return _pcall(
        body, name=name, grid=(rows // tr,),
        in_specs=[pl.BlockSpec((n, tr, cols), lambda i: (0, i, 0))], out_specs=pl.BlockSpec((tr, cols), lambda i: (i, 0)),
        out_shape=jax.ShapeDtypeStruct((rows, cols), F32), compiler_params=_params(("parallel",)),
    )(p)


def _sum_rows(t, name):
    def body(t_ref, o_ref):
        o_ref[...] = jnp.sum(t_ref[...], axis=0, keepdims=True)

    return _pcall(body, name=name, out_shape=jax.ShapeDtypeStruct((1, t.shape[1]), F32))(t)


class _Packing:
    def __init__(self, sizes, align, total_align=None):
        self.offsets, self.sizes, self.align = {}, dict(sizes), align
        row = 0
        for name, size in sizes:
            self.offsets[name] = row
            row += -(-size // (align * PACK_COLS)) * align
        total_align = total_align or align
        self.rows = -(-row // total_align) * total_align
        self.tail = self.rows - row

    def pack(self, pieces):
        return self.pack_blocks({n: pieces[n].reshape(1, -1) for n in self.sizes})[0]

    def pack_blocks(self, pieces):
        out = []
        for n, size in self.sizes.items():
            padded = -(-size // (self.align * PACK_COLS)) * self.align * PACK_COLS
            out.append(jnp.pad(pieces[n], ((0, 0), (0, padded - size))).reshape(pieces[n].shape[0], -1, PACK_COLS))
        if self.tail:
            out.append(jnp.zeros((out[0].shape[0], self.tail, PACK_COLS), out[0].dtype))
        return jnp.concatenate(out, axis=1)

    def piece(self, packed, name, lead=()):
        start, size = self.offsets[name], self.sizes[name]
        nrow = -(-size // (self.align * PACK_COLS)) * self.align
        sl = packed[..., start:start + nrow, :]
        return sl.reshape(lead + (nrow * PACK_COLS,))[..., :size]


_PIECES = (("ev_w_in", "ev_w_in", 0, 1, "a"), ("w_out0", "w_out", 0, 0, "a"),
           ("mlp_w1_0", "mlp_w1", 0, 1, "b"), ("mlp_w2_0", "mlp_w2", 0, 0, "b"),
           ("od_w_in", "od_w_in", 0, 1, "c"), ("od_w_uq", "od_w_uq", 0, 1, "c"), ("od_w_ukv", "od_w_ukv", 0, 1, "c"),
           ("w_out1", "w_out", 1, 0, "c"), ("mlp_w1_1", "mlp_w1", 1, 1, "c"), ("mlp_w2_1", "mlp_w2", 1, 0, "c"))
_SMALL_SHARDED = (("od_q_norm_g", 1), ("od_conv_w", 2), ("od_conv_b", 1), ("od_ln_g", 1), ("od_ln_b", 1))
_REPLICATED = ("c_ctx", "norm1_g", "norm2_g", "ev_q_norm_g", "ev_k_norm_g", "ev_sgu_norm_g", "ev_sgu_w", "ev_sgu_b",
               "od_kv_norm_g", "final_g")


def _unshard(blocks, axis):
    moved = jnp.moveaxis(blocks, 0, axis)
    shape = moved.shape
    return moved.reshape(shape[:axis] + (shape[axis] * shape[axis + 1],) + shape[axis + 2:])


def _shard_blocks(full, axis):
    shape = full.shape
    split = full.reshape(shape[:axis] + (N_DEV, shape[axis] // N_DEV) + shape[axis + 1:])
    return jnp.moveaxis(split, axis, 0)


def _group_mean_matrix(width, group):
    idx = jnp.arange(width) // group
    return (idx[:, None] == idx[None, :]).astype(F32) / group


def _swap_matrix(width):
    idx = jnp.arange(width)
    return ((idx[:, None] ^ 1) == idx[None, :]).astype(F32)


def _angles(length, d_rot):
    rows = length // GRID_W
    row = jnp.broadcast_to(jnp.arange(rows)[:, None], (rows, GRID_W)).reshape(-1).astype(F32)
    col = jnp.broadcast_to(jnp.arange(GRID_W)[None, :], (rows, GRID_W)).reshape(-1).astype(F32)
    d_axis = d_rot // 2
    inv = ROPE_THETA ** (-jnp.arange(0, d_axis, 2, dtype=F32) / d_axis)
    return jnp.concatenate([row[:, None] * inv, col[:, None] * inv], axis=-1)


def _rope_tables(length, d_rot, head_dim, heads, ctx_len):
    ang = _angles(length, d_rot)
    cos = jnp.repeat(jnp.cos(ang), 2, axis=1)
    sin = jnp.repeat(jnp.sin(ang), 2, axis=1) * jnp.tile(jnp.array([-1.0, 1.0], F32), d_rot // 2)
    keep = head_dim - d_rot
    cos = jnp.concatenate([jnp.ones((length, keep), F32), cos], axis=1)
    sin = jnp.concatenate([jnp.zeros((length, keep), F32), sin], axis=1)
    cos, sin = jnp.tile(cos, (1, heads)), jnp.tile(sin, (1, heads))
    cos = jnp.concatenate([jnp.ones((ctx_len, cos.shape[1]), F32), cos], axis=0)
    sin = jnp.concatenate([jnp.zeros((ctx_len, sin.shape[1]), F32), sin], axis=0)
    return cos, sin


def _to_heads(t, heads):
    b, l, w = t.shape
    return t.reshape(b, l, heads, w // heads).transpose(0, 2, 1, 3)


def _from_heads(t):
    b, h, l, d = t.shape
    return t.transpose(0, 2, 1, 3).reshape(b, l, h * d)


def _segment_params(mod, bsz):
    parts = jnp.split(mod, N_MOD, axis=-1)
    out = []
    for part in parts:
        lat = part[:bsz]
        ctx = jnp.broadcast_to(part[bsz:bsz + 1], lat.shape)
        out.append(jnp.stack([ctx, lat], axis=1)[:, :, None, :])
    return out


def _flat(t):
    return t.reshape(-1, t.shape[-1])


def _sequence_rowwise(ctx_len):
    tl = math.gcd(256, ctx_len)

    def make(name, f, out_specs, rows_per_block=tl, ctx_blocks=ctx_len // tl):
        return _rowwise(name, f, out_specs, rows_per_block, ctx_blocks)

    return make


def _mixer0(xall, modrows0, w, bsz, length, ctx_len):
    d = xall.shape[-1]
    total = ctx_len + length
    rowwise, flat = _sequence_rowwise(ctx_len), _flat
    sh1, sc1, g1, sh2, sc2, _ = _segment_params(modrows0, bsz)
    (h,) = rowwise("mod0", _f_modulate, [(d, BF16)])((xall,), (), (sh1, sc1), (w["norm1_g0"],), ())
    proj = _linear("ev_in")(flat(h), w["ev_w_in"]).reshape(bsz, total, -1)
    ev_q, ev_kv = A_Q_HEADS * A_HEAD_DIM, A_KV_HEADS * A_HEAD_DIM
    half = B_GROUPS * B_GROUP_DIM
    qp, kp, vp, zu, zv = jnp.split(proj, [ev_q, ev_q + ev_kv, ev_q + 2 * ev_kv, ev_q + 2 * ev_kv + half], axis=-1)
    cos_q, sin_q = _rope_tables(length, A_HEAD_DIM, A_HEAD_DIM, A_Q_HEADS, ctx_len)
    cos_k, sin_k = cos_q[:, :ev_kv], sin_q[:, :ev_kv]
    (q,) = rowwise("ev_q", _f_headnorm_rope, [(ev_q, BF16)])(
        (qp,), (cos_q, sin_q), (), (jnp.tile(w["ev_q_norm_g"][0], A_Q_HEADS)[None],), (_group_mean_matrix(ev_q, A_HEAD_DIM), _swap_matrix(ev_q)))
    (k,) = rowwise("ev_k", _f_headnorm_rope, [(ev_kv, BF16)])(
        (kp,), (cos_k, sin_k), (), (jnp.tile(w["ev_k_norm_g"][0], A_KV_HEADS)[None],), (_group_mean_matrix(ev_kv, A_HEAD_DIM), _swap_matrix(ev_kv)))
    o_att = _attention("gqa", ctx_len)(_to_heads(q, A_Q_HEADS), _to_heads(k, A_KV_HEADS), _to_heads(vp.astype(BF16), A_KV_HEADS))
    u, vn = rowwise("sgu_pre", _f_sgu_pre, [(half, F32), (half, BF16)])(
        (zu, zv), (), (), (w["ev_sgu_norm_g"][0].reshape(1, half),), (_group_mean_matrix(half, B_GROUP_DIM),))
    bias = jnp.repeat(w["ev_sgu_b"][0].T, B_GROUP_DIM, axis=1)
    (o_sgu,) = rowwise("sgu_mix", _f_sgu_mix, [(half, BF16)], rows_per_block=B_CHUNK, ctx_blocks=0)(
        (u, vn), (), (), tuple(w["ev_sgu_w"][0][g] for g in range(B_GROUPS)) + (bias,), ())
    o = jnp.concatenate([_from_heads(o_att), o_sgu], axis=-1)
    y = _linear("out0")(flat(o), w["w_out0"]).reshape(bsz, total, d)
    x1, h = rowwise("res_mod0a", _f_res_modulate, [(d, F32), (d, BF16)])((xall, y), (), (g1, sh2, sc2), (w["norm2_g0"],), ())
    return x1, h


def _mlp0(x1, h, modrows0, modrows1, w, bsz, length, ctx_len):
    d = x1.shape[-1]
    total = ctx_len + length
    g2 = _segment_params(modrows0, bsz)[5]
    sh1, sc1 = _segment_params(modrows1, bsz)[:2]
    y = _mlp("mlp0")(_flat(h), w["mlp_w1_0"], w["mlp_w2_0"]).reshape(bsz, total, d)
    return _sequence_rowwise(ctx_len)("res_mod0b", _f_res_modulate, [(d, F32), (d, BF16)])((x1, y), (), (g2, sh1, sc1), (w["norm1_g1"],), ())


def _layer1(x2, h, modrows1, w, bsz, length, ctx_len):
    d = x2.shape[-1]
    total = ctx_len + length
    half = B_GROUPS * B_GROUP_DIM
    rowwise, flat = _sequence_rowwise(ctx_len), _flat
    _, _, g1n, sh2n, sc2n, g2n = _segment_params(modrows1, bsz)
    proj = _linear("od_in")(flat(h), w["od_w_in"]).reshape(bsz, total, -1)
    cq, ckv, kr, za, zg = jnp.split(proj, [C_Q_RANK, C_Q_RANK + C_KV_RANK, C_Q_RANK + C_KV_RANK + C_ROPE,
                                          C_Q_RANK + C_KV_RANK + C_ROPE + half], axis=-1)
    lat = slice(ctx_len, total)
    lat_tl = math.gcd(256, length)
    (cqn,) = _rowwise("od_qn", _f_rms, [(C_Q_RANK, BF16)], lat_tl)((cq[:, lat],), (), (), (w["od_q_norm_g"],), ())
    c_qk = C_NOPE + C_ROPE
    qf = _linear("od_uq")(flat(cqn), w["od_w_uq"]).reshape(bsz, length, C_HEADS * c_qk)
    cos_q, sin_q = _rope_tables(length, C_ROPE, c_qk, C_HEADS, 0)
    (q,) = _rowwise("od_qrope", _f_rope, [(C_HEADS * c_qk, BF16)], lat_tl)((qf,), (cos_q, sin_q), (), (), (_swap_matrix(C_HEADS * c_qk),))
    (ckvn,) = rowwise("od_kvn", _f_rms, [(C_KV_RANK, BF16)])((ckv,), (), (), (w["od_kv_norm_g"],), ())
    kv = _linear("od_ukv")(flat(ckvn), w["od_w_ukv"]).reshape(bsz, total, C_HEADS, C_NOPE + C_V)
    cos_r, sin_r = _rope_tables(length, C_ROPE, C_ROPE, 1, ctx_len)
    (krr,) = rowwise("od_krope", _f_rope, [(C_ROPE, BF16)])((kr,), (cos_r, sin_r), (), (), (_swap_matrix(C_ROPE),))
    kn = kv[..., :C_NOPE].astype(BF16).transpose(0, 2, 1, 3)
    vv = kv[..., C_NOPE:].astype(BF16).transpose(0, 2, 1, 3)
    kfull = jnp.concatenate([kn, jnp.broadcast_to(krr[:, None], (bsz, C_HEADS, total, C_ROPE))], axis=-1)
    o_att = _attention("mla", 0)(_to_heads(q, C_HEADS), kfull, vv)
    (glu,) = _rowwise("glu", _f_glu, [(half, F32)], lat_tl)((za[:, lat], zg[:, lat]), (), (), (), ())
    taps = jnp.concatenate([w["od_conv_w"][0], w["od_conv_b"]], axis=0)
    conv = _dwconv(glu, taps)
    (o_conv,) = _rowwise("ln_silu", _f_ln_silu, [(half, BF16)], lat_tl)((conv,), (), (), (w["od_ln_g"], w["od_ln_b"]), ())
    o = jnp.concatenate([_from_heads(o_att), o_conv], axis=-1)
    y = _linear("out1")(flat(o), w["w_out1"]).reshape(bsz, length, d)
    lat_param = lambda p: p[:, 1:]
    x3, h = _rowwise("res_mod1a", _f_res_modulate, [(d, F32), (d, BF16)], lat_tl)(
        (x2[:, lat], y), (), (lat_param(g1n), lat_param(sh2n), lat_param(sc2n)), (w["norm2_g1"],), ())
    y = _mlp("mlp1")(flat(h), w["mlp_w1_1"], w["mlp_w2_1"]).reshape(bsz, length, d)
    (x4,) = _rowwise("res1b", _f_res, [(d, F32)], lat_tl)((x3, y), (), (lat_param(g2n),), (), ())
    return x4


def kernel(x, c, ctx, c_ctx, ada_w, ada_b, norm1_g, norm2_g, w_out, mlp_w1, mlp_w2, ev_w_in, ev_q_norm_g, ev_k_norm_g, ev_sgu_norm_g, ev_sgu_w, ev_sgu_b, od_w_in, od_q_norm_g, od_kv_norm_g, od_w_uq, od_w_ukv, od_conv_w, od_conv_b, od_ln_g, od_ln_b, final_g, loss_target, m_c_ctx, m_ada_w, m_ada_b, m_norm1_g, m_norm2_g, m_w_out, m_mlp_w1, m_mlp_w2, m_ev_w_in, m_ev_q_norm_g, m_ev_k_norm_g, m_ev_sgu_norm_g, m_ev_sgu_w, m_ev_sgu_b, m_od_w_in, m_od_q_norm_g, m_od_kv_norm_g, m_od_w_uq, m_od_w_ukv, m_od_conv_w, m_od_conv_b, m_od_ln_g, m_od_ln_b, m_final_g, v_c_ctx, v_ada_w, v_ada_b, v_norm1_g, v_norm2_g, v_w_out, v_mlp_w1, v_mlp_w2, v_ev_w_in, v_ev_q_norm_g, v_ev_k_norm_g, v_ev_sgu_norm_g, v_ev_sgu_w, v_ev_sgu_b, v_od_w_in, v_od_q_norm_g, v_od_kv_norm_g, v_od_w_uq, v_od_w_ukv, v_od_conv_w, v_od_conv_b, v_od_ln_g, v_od_ln_b, v_final_g):
    names = ["c_ctx", "ada_w", "ada_b", "norm1_g", "norm2_g", "w_out", "mlp_w1", "mlp_w2", "ev_w_in", "ev_q_norm_g", "ev_k_norm_g",
             "ev_sgu_norm_g", "ev_sgu_w", "ev_sgu_b", "od_w_in", "od_q_norm_g", "od_kv_norm_g", "od_w_uq", "od_w_ukv", "od_conv_w",
             "od_conv_b", "od_ln_g", "od_ln_b", "final_g"]
    local = dict(zip(names, [c_ctx, ada_w, ada_b, norm1_g, norm2_g, w_out, mlp_w1, mlp_w2, ev_w_in, ev_q_norm_g, ev_k_norm_g, ev_sgu_norm_g, ev_sgu_w, ev_sgu_b, od_w_in, od_q_norm_g, od_kv_norm_g, od_w_uq, od_w_ukv, od_conv_w, od_conv_b, od_ln_g, od_ln_b, final_g]))
    mom1 = dict(zip(names, [m_c_ctx, m_ada_w, m_ada_b, m_norm1_g, m_norm2_g, m_w_out, m_mlp_w1, m_mlp_w2, m_ev_w_in, m_ev_q_norm_g, m_ev_k_norm_g, m_ev_sgu_norm_g, m_ev_sgu_w, m_ev_sgu_b, m_od_w_in, m_od_q_norm_g, m_od_kv_norm_g, m_od_w_uq, m_od_w_ukv, m_od_conv_w, m_od_conv_b, m_od_ln_g, m_od_ln_b, m_final_g]))
    mom2 = dict(zip(names, [v_c_ctx, v_ada_w, v_ada_b, v_norm1_g, v_norm2_g, v_w_out, v_mlp_w1, v_mlp_w2, v_ev_w_in, v_ev_q_norm_g, v_ev_k_norm_g, v_ev_sgu_norm_g, v_ev_sgu_w, v_ev_sgu_b, v_od_w_in, v_od_q_norm_g, v_od_kv_norm_g, v_od_w_uq, v_od_w_ukv, v_od_conv_w, v_od_conv_b, v_od_ln_g, v_od_ln_b, v_final_g]))
    bsz, length, d = x.shape
    ctx_len = ctx.shape[1]
    me = _block_index(_mesh_pos())

    cond_local = jnp.concatenate([c, c_ctx[None], jnp.zeros((COND_ROWS - bsz - 1, d), F32)], axis=0)
    cond = _all_gather(cond_local, "gather_cond").reshape(N_DEV * COND_ROWS, d)
    silu_op = _rowwise("silu", _f_silu, [(d, F32)], N_DEV * COND_ROWS)
    silu_rows, silu_pullback = jax.vjp(lambda r: silu_op((r[None],), (), (), (), ())[0][0], cond)
    mod_cols = ada_w.shape[2]
    mod_part = jnp.concatenate([_mm(silu_rows, ada_w[i], "nn", (F32,), f"ada{i}_fwd")[0] for i in range(2)], axis=0)
    mod_all = _all_gather(mod_part, "gather_mod").reshape(N_DEV, 2, N_DEV * COND_ROWS, mod_cols)
    modrows = []
    for i in range(2):
        whole = mod_all[:, i].transpose(1, 0, 2).reshape(N_DEV * COND_ROWS, N_DEV * mod_cols) + ada_b[i]
        modrows.append(lax.dynamic_slice_in_dim(whole, me * COND_ROWS, COND_ROWS, axis=0)[:bsz + 1])

    shard = {p: local[wn][layer] for p, wn, layer, _, _ in _PIECES}
    stages = {s: [(p, axis) for p, _, _, axis, st in _PIECES if st == s] for s in "abc"}
    packs = {s: _Packing([(p, shard[p].size) for p, _ in stages[s]], ROW_ALIGN, PACK_ROWS) for s in "abc"}
    mine = {s: packs[s].pack({p: shard[p].astype(BF16) for p, _ in stages[s]}) for s in "abc"}

    def unpack_weights(s, gathered):
        return {p: _unshard(packs[s].piece(gathered, p, (N_DEV,)).reshape((N_DEV,) + shard[p].shape), axis) for p, axis in stages[s]}

    def pack_grads(s, g):
        return packs[s].pack_blocks({p: _shard_blocks(g[p], axis).reshape(N_DEV, -1) for p, axis in stages[s]})

    weights_a = unpack_weights("a", _all_gather(mine["a"], "gather_weights_a"))
    gather_b, token_b = _push_start(mine["b"], "gather_weights_b_start")
    gather_c, token_c = _push_start(mine["c"], "gather_weights_c_start")
    tiny_pack = _Packing([(n, local[n].size) for n, _ in _SMALL_SHARDED], 8)
    gathered_tiny = _all_gather(tiny_pack.pack({n: local[n] for n, _ in _SMALL_SHARDED}), "gather_small_weights")
    full = {n: local[n] for n in _REPLICATED}
    for n, axis in _SMALL_SHARDED:
        full[n] = _unshard(tiny_pack.piece(gathered_tiny, n, (N_DEV,)).reshape((N_DEV,) + local[n].shape), axis)

    xall = jnp.concatenate([ctx, x], axis=1)
    modrows0, modrows1 = modrows[0] + (token_b + token_c), modrows[1]
    w_a = dict(weights_a, norm1_g0=norm1_g[0][None], norm2_g0=norm2_g[0][None],
               **{n: full[n] for n in ("ev_q_norm_g", "ev_k_norm_g", "ev_sgu_norm_g", "ev_sgu_w", "ev_sgu_b")})
    (x1, h0), pull_a = jax.vjp(lambda x_, m0, w: _mixer0(x_, m0, w, bsz, length, ctx_len), xall, modrows0, w_a)
    w_b = dict(unpack_weights("b", _push_wait(gather_b, x1, mine["b"], me, "gather_weights_b_wait")), norm1_g1=norm1_g[1][None])
    (x2, h1), pull_b = jax.vjp(lambda x_, h_, m0, m1, w: _mlp0(x_, h_, m0, m1, w, bsz, length, ctx_len), x1, h0, modrows0, modrows1, w_b)
    w_c = dict(unpack_weights("c", _push_wait(gather_c, x2, mine["c"], me, "gather_weights_c_wait")), norm2_g1=norm2_g[1][None],
               **{n: full[n] for n in ("od_q_norm_g", "od_kv_norm_g", "od_conv_w", "od_conv_b", "od_ln_g", "od_ln_b")})
    x4, pull_c = jax.vjp(lambda x_, h_, m1, w: _layer1(x_, h_, m1, w, bsz, length, ctx_len), x2, h1, modrows1, w_c)
    loss_part, dx4, dfinal = _loss_head(x4, loss_target, final_g[None])
    loss = lax.psum(loss_part[0, 0], MESH_AXES)

    dx2, dh1, dmod1_c, g_c = pull_c(dx4)
    grads_c = pack_grads("c", g_c)
    exchange_c, token = _push_start(grads_c, "exchange_grads_c_start")
    dx1, dh0, dmod0_b, dmod1_b, g_b = pull_b((dx2, dh1 + token.astype(dh1.dtype)))
    grads_b = pack_grads("b", g_b)
    exchange_b, token = _push_start(grads_b, "exchange_grads_b_start")
    dxall, dmod0_a, g_a = pull_a((dx1, dh0 + token.astype(dh0.dtype)))
    grad_x = dxall[:, ctx_len:]
    dmodrows = [dmod0_a + dmod0_b, dmod1_b + dmod1_c]
    grads = {n: g[n] for g in (g_a, g_c) for n in g if n in full}
    grads["norm1_g"] = jnp.concatenate([g_a["norm1_g0"], g_b["norm1_g1"]], axis=0)
    grads["norm2_g"] = jnp.concatenate([g_a["norm2_g0"], g_c["norm2_g1"]], axis=0)
    grads["final_g"] = dfinal[0]

    dmod_local = jnp.concatenate([jnp.pad(dm, ((0, COND_ROWS - bsz - 1), (0, 0))) for dm in dmodrows], axis=0)
    dmod_all = _all_gather(dmod_local, "gather_dmod").reshape(N_DEV, 2, COND_ROWS, N_DEV * mod_cols)
    reduced = {}
    grad_ada_w, grad_ada_b, dmod_mine = [], [], []
    for i in range(2):
        dmod = dmod_all[:, i].reshape(N_DEV * COND_ROWS, N_DEV * mod_cols)
        grad_ada_b.append(_sum_rows(dmod, f"ada{i}_db")[0])
        dmod_mine.append(lax.dynamic_slice_in_dim(dmod, me * mod_cols, mod_cols, axis=1))
        grad_ada_w.append(_mm(silu_rows, dmod_mine[i], "tn", (F32,), f"ada{i}_dw")[0])
    reduced["ada_w"], reduced["ada_b"] = jnp.stack(grad_ada_w), jnp.stack(grad_ada_b)
    dsilu = _mm(jnp.concatenate(dmod_mine, axis=1), jnp.concatenate([ada_w[0], ada_w[1]], axis=1), "nt", (F32,), "ada_dx")[0]
    (dcond,) = silu_pullback(dsilu)
    grads["c_ctx"] = _sum_rows(dcond.reshape(N_DEV, COND_ROWS, d)[:, bsz], "c_ctx_rows")[0]

    def own_block(packed):
        return lax.dynamic_index_in_dim(packed, me, 0, keepdims=False)

    received = {"a": _exchange_blocks(pack_grads("a", g_a), "exchange_grads_a"),
                "b": _push_wait(exchange_b, dmod_all, own_block(grads_b), me, "exchange_grads_b_wait"),
                "c": _push_wait(exchange_c, dmod_all, own_block(grads_c), me, "exchange_grads_c_wait")}
    piece_grad = {}
    for s in "abc":
        summed = _sum_blocks(received[s], "sum_grads_" + s)
        for p, _ in stages[s]:
            piece_grad[p] = packs[s].piece(summed, p).reshape(shard[p].shape)
    for n in ("w_out", "mlp_w1", "mlp_w2"):
        reduced[n] = jnp.stack([piece_grad[p] for p, wn, _, _, _ in _PIECES if wn == n])
    for n in ("ev_w_in", "od_w_in", "od_w_uq", "od_w_ukv"):
        reduced[n] = piece_grad[n][None]

    small_names = list(_REPLICATED) + [n for n, _ in _SMALL_SHARDED]
    small_pack = _Packing([(n, full[n].size) for n in small_names], 8)
    small_sum = _sum_blocks(_all_gather(small_pack.pack({n: grads[n].astype(F32) for n in small_names}), "gather_small_grads"), "sum_small_grads")
    for n in _REPLICATED:
        reduced[n] = small_pack.piece(small_sum, n).reshape(local[n].shape)
    for n, axis in _SMALL_SHARDED:
        whole = small_pack.piece(small_sum, n).reshape(full[n].shape)
        reduced[n] = lax.dynamic_slice_in_dim(whole, me * local[n].shape[axis], local[n].shape[axis], axis=axis)

    delta, new_m, new_v = {}, {}, {}
    for n in names:
        delta[n], new_m[n], new_v[n] = _adamw(local[n], reduced[n], mom1[n], mom2[n], "adamw_" + n)
    return (loss, grad_x, *[reduced[n] for n in names], *[delta[n] for n in names], *[new_m[n] for n in names], *[new_v[n] for n in names])
```

```python
import functools
import math

import jax
import jax.numpy as jnp
from jax import lax
from jax.experimental import pallas as pl
from jax.experimental.pallas import tpu as pltpu

F32, BF16 = jnp.float32, jnp.bfloat16

EPS = 1e-6
GRID_W = 64
ROPE_THETA = 10000.0
A_HEAD_DIM, A_Q_HEADS, A_KV_HEADS = 64, 8, 2
B_GROUPS, B_GROUP_DIM, B_CHUNK = 8, 64, 128
C_HEADS, C_NOPE, C_ROPE, C_V, C_Q_RANK, C_KV_RANK = 8, 64, 32, 64, 256, 128
D_CONV = 31
CONV_PAD = D_CONV // 2
N_MOD = 6
N_DEV = 8
MESH_AXES = ("x", "y", "c")

ADAM_LR, ADAM_B1, ADAM_B2, ADAM_EPS, ADAM_WD, ADAM_STEP = 0.001, 0.9, 0.999, 1e-08, 0.01, 10

VMEM_LIMIT = 56 * 1024 * 1024
PACK_COLS = 1024
ROW_ALIGN = 16
PACK_ROWS = 32
COND_ROWS = 8


def _pcall(body, **kw):
    return pl.pallas_call(body, **kw)


def _params(sem=None):
    return pltpu.CompilerParams(dimension_semantics=sem, vmem_limit_bytes=VMEM_LIMIT)


def _pick(n, cands):
    for c in cands:
        if n % c == 0:
            return c
    return n


def _mm(a, b, mode, out_dtypes, name, epi=None, extras=()):
    if mode == "tn":
        kk, m = a.shape
    else:
        m, kk = a.shape
    n = b.shape[0] if mode == "nt" else b.shape[1]
    tm = _pick(m, (1152, 1024, 768, 512, 256, 128))
    tn = _pick(n, (1024, 896, 768, 512, 256, 128))
    tk = kk if kk <= 1024 else _pick(kk, (1024, 896, 768, 512, 256, 128))
    nk = kk // tk
    ne, no = len(extras), len(out_dtypes)
    a_spec = pl.BlockSpec((tk, tm), lambda i, j, k: (k, i)) if mode == "tn" else pl.BlockSpec((tm, tk), lambda i, j, k: (i, k))
    b_spec = pl.BlockSpec((tn, tk), lambda i, j, k: (j, k)) if mode == "nt" else pl.BlockSpec((tk, tn), lambda i, j, k: (k, j))
    t_spec = pl.BlockSpec((tm, tn), lambda i, j, k: (i, j))
    dn = {"nn": ((1,), (0,)), "nt": ((1,), (1,)), "tn": ((0,), (0,))}[mode]

    def body(a_ref, b_ref, *rest):
        extra_refs, out_refs = rest[:ne], rest[ne:ne + no]

        def finish(acc):
            outs = (acc,) if epi is None else epi(acc, *[r[...] for r in extra_refs])
            for r, o in zip(out_refs, outs):
                r[...] = o.astype(r.dtype)

        part = lax.dot_general(a_ref[...].astype(BF16), b_ref[...].astype(BF16), (dn, ((), ())), preferred_element_type=F32)
        if nk == 1:
            finish(part)
        else:
            acc_ref = rest[-1]
            k = pl.program_id(2)

            @pl.when(k == 0)
            def _():
                acc_ref[...] = part

            @pl.when(k > 0)
            def _():
                acc_ref[...] += part

            @pl.when(k == nk - 1)
            def _():
                finish(acc_ref[...])

    outs = _pcall(
        body, name=name, grid=(m // tm, n // tn, nk),
        in_specs=[a_spec, b_spec] + [t_spec] * ne,
        out_specs=[t_spec] * no,
        out_shape=[jax.ShapeDtypeStruct((m, n), d) for d in out_dtypes],
        scratch_shapes=[pltpu.VMEM((tm, tn), F32)] if nk > 1 else [],
        compiler_params=_params(("parallel", "parallel", "arbitrary")),
    )(a, b, *extras)
    return outs


def _linear(name, out_dtype=F32):
    @jax.custom_vjp
    def op(x, w):
        return _mm(x, w, "nn", (out_dtype,), name + "_fwd")[0]

    def fwd(x, w):
        return op(x, w), (x, w)

    def bwd(res, dy):
        x, w = res
        dx = _mm(dy, w, "nt", (x.dtype,), name + "_dx")[0]
        dw = _mm(x, dy, "tn", (w.dtype,), name + "_dw")[0]
        return dx, dw

    op.defvjp(fwd, bwd)
    return op


def _relu2_epi(acc):
    return jnp.square(jnp.maximum(acc, 0.0)), acc


def _relu2_bwd_epi(acc, a):
    return (acc * (2.0 * jnp.maximum(a.astype(F32), 0.0)),)


def _mlp(name):
    @jax.custom_vjp
    def op(h, w1, w2):
        s, _ = _mm(h, w1, "nn", (BF16, BF16), name + "_up", epi=_relu2_epi)
        return _mm(s, w2, "nn", (F32,), name + "_down")[0]

    def fwd(h, w1, w2):
        s, a = _mm(h, w1, "nn", (BF16, BF16), name + "_up", epi=_relu2_epi)
        return _mm(s, w2, "nn", (F32,), name + "_down")[0], (h, w1, w2, s, a)

    def bwd(res, dy):
        h, w1, w2, s, a = res
        da = _mm(dy, w2, "nt", (BF16,), name + "_ds", epi=_relu2_bwd_epi, extras=(a,))[0]
        dw2 = _mm(s, dy, "tn", (w2.dtype,), name + "_dw2")[0]
        dw1 = _mm(h, da, "tn", (w1.dtype,), name + "_dw1")[0]
        dh = _mm(da, w1, "nt", (h.dtype,), name + "_dh")[0]
        return dh, dw1, dw2

    op.defvjp(fwd, bwd)
    return op


def _dot_exact(x, c):
    return jnp.dot(x, c, precision=lax.Precision.HIGHEST, preferred_element_type=F32)


@jax.custom_vjp
def _bdot(a, b):
    return jnp.dot(a.astype(BF16), b.astype(BF16), preferred_element_type=F32)


def _bdot_fwd(a, b):
    return _bdot(a, b), (a, b)


def _bdot_bwd(res, g):
    a, b = res
    gb = g.astype(BF16)
    da = lax.dot_general(gb, b.astype(BF16), (((1,), (1,)), ((), ())), preferred_element_type=F32)
    db = lax.dot_general(a.astype(BF16), gb, (((0,), (0,)), ((), ())), preferred_element_type=F32)
    return da, db


_bdot.defvjp(_bdot_fwd, _bdot_bwd)


def _rowwise(name, f, out_specs, tl, ctx_blocks=0):
    def seg(l, s):
        return jnp.where(l >= ctx_blocks, s - 1, 0) if s > 1 else 0

    def specs(rows, tabs, pers, glbs, consts):
        row_specs = [pl.BlockSpec((1, tl, r.shape[2]), lambda b, l: (b, l, 0)) for r in rows]
        tab_specs = [pl.BlockSpec((tl, t.shape[1]), lambda b, l: (l, 0)) for t in tabs]
        per_specs = [pl.BlockSpec((1, 1, 1, p.shape[3]), functools.partial(lambda b, l, s: (b, seg(l, s), 0, 0), s=p.shape[1])) for p in pers]
        glb_specs = [pl.BlockSpec(g.shape, functools.partial(lambda b, l, nd: (0,) * nd, nd=g.ndim)) for g in glbs]
        const_specs = [pl.BlockSpec(c.shape, functools.partial(lambda b, l, nd: (0,) * nd, nd=c.ndim)) for c in consts]
        return row_specs, tab_specs, per_specs, glb_specs, const_specs

    def load(refs_rows, refs_tabs, refs_pers, refs_glbs, refs_consts):
        return (tuple(r[0].astype(F32) for r in refs_rows), tuple(t[...] for t in refs_tabs),
                tuple(p[0, 0].astype(F32) for p in refs_pers), tuple(g[...].astype(F32) for g in refs_glbs),
                tuple(c[...] for c in refs_consts))

    def call_fwd(rows, tabs, pers, glbs, consts):
        bsz, length = rows[0].shape[:2]
        nr, nt, npp, ng, nc = len(rows), len(tabs), len(pers), len(glbs), len(consts)
        rs, ts, ps, gs, cs = specs(rows, tabs, pers, glbs, consts)

        def body(*refs):
            ins, outs = refs[:nr + nt + npp + ng + nc], refs[nr + nt + npp + ng + nc:]
            r, t, p, g, c = load(ins[:nr], ins[nr:nr + nt], ins[nr + nt:nr + nt + npp], ins[nr + nt + npp:nr + nt + npp + ng], ins[nr + nt + npp + ng:])
            for o_ref, o in zip(outs, f(r, t, p, g, c)):
                o_ref[0] = o.astype(o_ref.dtype)

        return _pcall(
            body, name=name + "_fwd", grid=(bsz, length // tl),
            in_specs=rs + ts + ps + gs + cs,
            out_specs=[pl.BlockSpec((1, tl, w), lambda b, l: (b, l, 0)) for w, _ in out_specs],
            out_shape=[jax.ShapeDtypeStruct((bsz, length, w), d) for w, d in out_specs],
            compiler_params=_params(("parallel", "parallel")),
        )(*rows, *tabs, *pers, *glbs, *consts)

    def call_bwd(rows, tabs, pers, glbs, consts, cts):
        bsz, length = rows[0].shape[:2]
        nr, nt, npp, ng, nc, no = len(rows), len(tabs), len(pers), len(glbs), len(consts), len(cts)
        rs, ts, ps, gs, cs = specs(rows, tabs, pers, glbs, consts)
        n_in = nr + nt + npp + ng + nc

        def body(*refs):
            ins, ct_refs, outs = refs[:n_in], refs[n_in:n_in + no], refs[n_in + no:]
            r, t, p, g, c = load(ins[:nr], ins[nr:nr + nt], ins[nr + nt:nr + nt + npp], ins[nr + nt + npp:nr + nt + npp + ng], ins[nr + nt + npp + ng:])
            _, vjp = jax.vjp(lambda r_, p_, g_: tuple(f(r_, t, p_, g_, c)), r, p, g)
            dr, dp, dg = vjp(tuple(ct[0].astype(F32) for ct in ct_refs))
            dr_refs, dp_refs, dg_refs = outs[:nr], outs[nr:nr + npp], outs[nr + npp:]
            for ref, d in zip(dr_refs, dr):
                ref[0] = d.astype(ref.dtype)
            b, l = pl.program_id(0), pl.program_id(1)
            first_of_segment = (l == 0) | (l == ctx_blocks)
            for ref, d in zip(dp_refs, dp):
                @pl.when(first_of_segment)
                def _(ref=ref, d=d):
                    ref[0, 0] = d

                @pl.when(jnp.logical_not(first_of_segment))
                def _(ref=ref, d=d):
                    ref[0, 0] += d
            first = (b == 0) & (l == 0)
            for ref, d in zip(dg_refs, dg):
                @pl.when(first)
                def _(ref=ref, d=d):
                    ref[...] = d

                @pl.when(jnp.logical_not(first))
                def _(ref=ref, d=d):
                    ref[...] += d

        ct_specs = [pl.BlockSpec((1, tl, w), lambda b, l: (b, l, 0)) for w, _ in out_specs]
        outs = _pcall(
            body, name=name + "_bwd", grid=(bsz, length // tl),
            in_specs=rs + ts + ps + gs + cs + ct_specs,
            out_specs=rs + ps + gs,
            out_shape=[jax.ShapeDtypeStruct(r.shape, r.dtype) for r in rows]
            + [jax.ShapeDtypeStruct(p.shape, F32) for p in pers] + [jax.ShapeDtypeStruct(g.shape, F32) for g in glbs],
            compiler_params=_params(("arbitrary", "arbitrary")),
        )(*rows, *tabs, *pers, *glbs, *consts, *cts)
        return tuple(outs[:nr]), tuple(outs[nr:nr + npp]), tuple(outs[nr + npp:])

    @jax.custom_vjp
    def op(rows, tabs, pers, glbs, consts):
        return tuple(call_fwd(rows, tabs, pers, glbs, consts))

    def fwd(rows, tabs, pers, glbs, consts):
        return op(rows, tabs, pers, glbs, consts), (rows, tabs, pers, glbs, consts)

    def bwd(res, cts):
        rows, tabs, pers, glbs, consts = res
        dr, dp, dg = call_bwd(rows, tabs, pers, glbs, consts, tuple(cts))
        dp = tuple(d.astype(p.dtype) for d, p in zip(dp, pers))
        dg = tuple(d.astype(g.dtype) for d, g in zip(dg, glbs))
        return dr, tuple(jnp.zeros_like(t) for t in tabs), dp, dg, tuple(jnp.zeros_like(c) for c in consts)

    op.defvjp(fwd, bwd)
    return op


def _rms(x, g):
    return x * lax.rsqrt(jnp.mean(x * x, axis=-1, keepdims=True) + EPS) * g


def _f_silu(r, t, p, g, c):
    return (jax.nn.silu(r[0]),)


def _f_modulate(r, t, p, g, c):
    shift, scale = p
    return (_rms(r[0], g[0]) * (1.0 + scale) + shift,)


def _f_res_modulate(r, t, p, g, c):
    x, y = r
    gate, shift, scale = p
    xn = x + gate * y
    return xn, _rms(xn, g[0]) * (1.0 + scale) + shift


def _f_res(r, t, p, g, c):
    return (r[0] + p[0] * r[1],)


def _f_headnorm_rope(r, t, p, g, c):
    x = r[0]
    cos, sin = t
    group_mean, swap = c
    xn = x * lax.rsqrt(_dot_exact(x * x, group_mean) + EPS) * g[0]
    return (xn * cos + _dot_exact(xn, swap) * sin,)


def _f_rope(r, t, p, g, c):
    x = r[0]
    cos, sin = t
    return (x * cos + _dot_exact(x, c[0]) * sin,)


def _f_rms(r, t, p, g, c):
    return (_rms(r[0], g[0]),)


def _f_sgu_pre(r, t, p, g, c):
    u = jax.nn.gelu(r[0])
    v = jax.nn.gelu(r[1])
    vn = v * lax.rsqrt(_dot_exact(v * v, c[0]) + EPS) * g[0]
    return u, vn


def _f_sgu_mix(r, t, p, g, c):
    u, vn = r
    bias = g[B_GROUPS]
    group = lax.broadcasted_iota(jnp.int32, vn.shape, 1) // B_GROUP_DIM
    sv = bias
    for k in range(B_GROUPS):
        sv = sv + jnp.where(group == k, _bdot(g[k], vn), 0.0)
    return (u * sv,)


def _f_glu(r, t, p, g, c):
    return (r[0] * jax.nn.sigmoid(r[1]),)


def _f_ln_silu(r, t, p, g, c):
    x = r[0]
    mu = jnp.mean(x, axis=-1, keepdims=True)
    var = jnp.mean(jnp.square(x - mu), axis=-1, keepdims=True)
    return (jax.nn.silu((x - mu) * lax.rsqrt(var + EPS) * g[0] + g[1]),)


def _attention(name, ctx_len):
    def tiles(q, v):
        lq = q.shape[2]
        tq = _pick(lq, (256, 128))
        if ctx_len:
            tq = math.gcd(tq, ctx_len)
        return tq, ctx_len // tq

    def probs(qq, kk, scale):
        s = lax.dot_general(qq, kk, (((1,), (1,)), ((), ())), preferred_element_type=F32) * scale
        p = jnp.exp(s - jnp.max(s, axis=-1, keepdims=True))
        return p, jnp.sum(p, axis=-1, keepdims=True)

    def by_segment(qi, cb, lk, run):
        if cb > 0:
            @pl.when(qi < cb)
            def _():
                run(ctx_len)

            @pl.when(qi >= cb)
            def _():
                run(lk)
        else:
            run(lk)

    def call_fwd(q, k, v):
        bsz, hq, lq, dk = q.shape
        _, hk, lk, dv = v.shape
        grp = hq // hk
        tq, cb = tiles(q, v)
        scale = dk ** -0.5

        def body(q_ref, k_ref, v_ref, o_ref):
            def run(nk):
                p, l = probs(q_ref[0, 0], k_ref[0, 0, :nk], scale)
                o = jnp.dot(p.astype(BF16), v_ref[0, 0, :nk], preferred_element_type=F32) / l
                o_ref[0, 0] = o.astype(o_ref.dtype)

            by_segment(pl.program_id(2), cb, lk, run)

        return _pcall(
            body, name=name + "_fwd", grid=(bsz, hq, lq // tq),
            in_specs=[pl.BlockSpec((1, 1, tq, dk), lambda b, h, i: (b, h, i, 0)),
                      pl.BlockSpec((1, 1, lk, dk), lambda b, h, i: (b, h // grp, 0, 0)),
                      pl.BlockSpec((1, 1, lk, dv), lambda b, h, i: (b, h // grp, 0, 0))],
            out_specs=pl.BlockSpec((1, 1, tq, dv), lambda b, h, i: (b, h, i, 0)),
            out_shape=jax.ShapeDtypeStruct((bsz, hq, lq, dv), BF16),
            compiler_params=_params(("parallel", "parallel", "parallel")),
        )(q, k, v)

    def call_bwd(q, k, v, do):
        bsz, hq, lq, dk = q.shape
        _, hk, lk, dv = v.shape
        grp = hq // hk
        tq, cb = tiles(q, v)
        scale = dk ** -0.5

        def body(q_ref, k_ref, v_ref, do_ref, dq_ref, dk_ref, dv_ref):
            g, qi = pl.program_id(2), pl.program_id(3)

            @pl.when((g == 0) & (qi == 0))
            def _():
                dk_ref[...] = jnp.zeros_like(dk_ref)
                dv_ref[...] = jnp.zeros_like(dv_ref)

            def run(nk):
                qq, kk, vv, dd = q_ref[0, 0], k_ref[0, 0, :nk], v_ref[0, 0, :nk], do_ref[0, 0]
                p, l = probs(qq, kk, scale)
                pn = p / l
                dp = lax.dot_general(dd, vv, (((1,), (1,)), ((), ())), preferred_element_type=F32)
                ds = (pn * (dp - jnp.sum(pn * dp, axis=-1, keepdims=True)) * scale).astype(BF16)
                dq_ref[0, 0] = jnp.dot(ds, kk, preferred_element_type=F32).astype(dq_ref.dtype)
                dk_ref[0, 0, :nk] += lax.dot_general(ds, qq, (((0,), (0,)), ((), ())), preferred_element_type=F32)
                dv_ref[0, 0, :nk] += lax.dot_general(pn.astype(BF16), dd, (((0,), (0,)), ((), ())), preferred_element_type=F32)

            by_segment(qi, cb, lk, run)

        q_spec = pl.BlockSpec((1, 1, tq, dk), lambda b, h, g, i: (b, h * grp + g, i, 0))
        k_spec = pl.BlockSpec((1, 1, lk, dk), lambda b, h, g, i: (b, h, 0, 0))
        v_spec = pl.BlockSpec((1, 1, lk, dv), lambda b, h, g, i: (b, h, 0, 0))
        do_spec = pl.BlockSpec((1, 1, tq, dv), lambda b, h, g, i: (b, h * grp + g, i, 0))
        return _pcall(
            body, name=name + "_bwd", grid=(bsz, hk, grp, lq // tq),
            in_specs=[q_spec, k_spec, v_spec, do_spec],
            out_specs=[q_spec, k_spec, v_spec],
            out_shape=[jax.ShapeDtypeStruct(q.shape, BF16), jax.ShapeDtypeStruct(k.shape, F32), jax.ShapeDtypeStruct(v.shape, F32)],
            compiler_params=_params(("parallel", "parallel", "arbitrary", "arbitrary")),
        )(q, k, v, do)

    @jax.custom_vjp
    def op(q, k, v):
        return call_fwd(q, k, v)

    def fwd(q, k, v):
        return op(q, k, v), (q, k, v)

    def bwd(res, do):
        q, k, v = res
        dq, dk, dv = call_bwd(q, k, v, do)
        return dq, dk.astype(k.dtype), dv.astype(v.dtype)

    op.defvjp(fwd, bwd)
    return op


def _conv_call(ypad, taps, name):
    bsz, lp, ch = ypad.shape
    length = lp - 2 * ROW_ALIGN
    tl = _pick(length, (256, 128))

    def body(y_ref, w_ref, o_ref):
        base = pl.multiple_of(pl.program_id(1) * tl, tl)
        win = y_ref[0, pl.ds(base, tl + 2 * ROW_ALIGN), :]
        acc = jnp.broadcast_to(w_ref[pl.ds(D_CONV, 1), :], (tl, ch))
        for k in range(D_CONV):
            acc = acc + win[k:k + tl, :] * w_ref[pl.ds(k, 1), :]
        o_ref[0] = acc

    return _pcall(
        body, name=name, grid=(bsz, length // tl),
        in_specs=[pl.BlockSpec((1, lp, ch), lambda b, l: (b, 0, 0)), pl.BlockSpec((D_CONV + 1, ch), lambda b, l: (0, 0))],
        out_specs=pl.BlockSpec((1, tl, ch), lambda b, l: (b, l, 0)),
        out_shape=jax.ShapeDtypeStruct((bsz, length, ch), F32),
        compiler_params=_params(("parallel", "parallel")),
    )(ypad, taps)


def _conv_dw_call(ypad, dout, name):
    bsz, lp, ch = ypad.shape
    length = lp - 2 * ROW_ALIGN
    tl = _pick(length, (256, 128))

    def body(y_ref, d_ref, o_ref):
        b, l = pl.program_id(0), pl.program_id(1)

        @pl.when((b == 0) & (l == 0))
        def _():
            o_ref[...] = jnp.zeros_like(o_ref)

        base = pl.multiple_of(l * tl, tl)
        win = y_ref[0, pl.ds(base, tl + 2 * ROW_ALIGN), :]
        dd = d_ref[0]
        for k in range(D_CONV):
            o_ref[pl.ds(k, 1), :] += jnp.sum(win[k:k + tl, :] * dd, axis=0, keepdims=True)
        o_ref[pl.ds(D_CONV, 1), :] += jnp.sum(dd, axis=0, keepdims=True)

    return _pcall(
        body, name=name, grid=(bsz, length // tl),
        in_specs=[pl.BlockSpec((1, lp, ch), lambda b, l: (b, 0, 0)), pl.BlockSpec((1, tl, ch), lambda b, l: (b, l, 0))],
        out_specs=pl.BlockSpec((D_CONV + 1, ch), lambda b, l: (0, 0)),
        out_shape=jax.ShapeDtypeStruct((D_CONV + 1, ch), F32),
        compiler_params=_params(("arbitrary", "arbitrary")),
    )(ypad, dout)


def _pad_rows(y):
    return jnp.pad(y, ((0, 0), (CONV_PAD, 2 * ROW_ALIGN - CONV_PAD), (0, 0)))


@jax.custom_vjp
def _dwconv(y, taps):
    return _conv_call(_pad_rows(y), taps, "conv_fwd")


def _dwconv_fwd(y, taps):
    return _dwconv(y, taps), (y, taps)


def _dwconv_bwd(res, dout):
    y, taps = res
    flipped = jnp.concatenate([taps[:D_CONV][::-1], jnp.zeros_like(taps[D_CONV:])], axis=0)
    dy = _conv_call(_pad_rows(dout), flipped, "conv_dy")
    dtaps = _conv_dw_call(_pad_rows(y), dout, "conv_dw")
    return dy, dtaps


_dwconv.defvjp(_dwconv_fwd, _dwconv_bwd)


def _loss_head(x, target, g):
    bsz, length, d = x.shape
    tl = _pick(length, (256, 128))

    def f(xb, tb, gb):
        err = _rms(xb, gb) - tb
        return 0.5 * jnp.sum(jnp.sum(err * err, axis=-1, keepdims=True), axis=0, keepdims=True) / d

    def body(x_ref, t_ref, g_ref, loss_ref, dx_ref, dg_ref):
        val, vjp = jax.vjp(lambda xb, gb: f(xb, t_ref[0], gb), x_ref[0], g_ref[...])
        dx, dg = vjp(jnp.ones((1, 1), F32))
        dx_ref[0] = dx
        first = (pl.program_id(0) == 0) & (pl.program_id(1) == 0)

        @pl.when(first)
        def _():
            loss_ref[...] = val
            dg_ref[...] = dg

        @pl.when(jnp.logical_not(first))
        def _():
            loss_ref[...] += val
            dg_ref[...] += dg

    row = pl.BlockSpec((1, tl, d), lambda b, l: (b, l, 0))
    return _pcall(
        body, name="loss_head", grid=(bsz, length // tl),
        in_specs=[row, row, pl.BlockSpec((1, d), lambda b, l: (0, 0))],
        out_specs=[pl.BlockSpec((1, 1), lambda b, l: (0, 0)), row, pl.BlockSpec((1, d), lambda b, l: (0, 0))],
        out_shape=[jax.ShapeDtypeStruct((1, 1), F32), jax.ShapeDtypeStruct(x.shape, F32), jax.ShapeDtypeStruct((1, d), F32)],
        compiler_params=_params(("arbitrary", "arbitrary")),
    )(x, target, g)


def _adamw(w, g, m, v, name):
    shape = w.shape
    cols = shape[-1]
    rows = w.size // cols
    tr = _pick(rows, (512, 256, 128))
    w2, g2, m2, v2 = (t.reshape(rows, cols) for t in (w, g, m, v))

    def body(w_ref, g_ref, m_ref, v_ref, d_ref, nm_ref, nv_ref):
        gg = g_ref[...]
        nm = ADAM_B1 * m_ref[...] + (1.0 - ADAM_B1) * gg
        nv = ADAM_B2 * v_ref[...] + (1.0 - ADAM_B2) * jnp.square(gg)
        m_hat = nm / (1.0 - ADAM_B1 ** ADAM_STEP)
        v_hat = nv / (1.0 - ADAM_B2 ** ADAM_STEP)
        d_ref[...] = -ADAM_LR * (m_hat / (jnp.sqrt(v_hat) + ADAM_EPS) + ADAM_WD * w_ref[...])
        nm_ref[...] = nm
        nv_ref[...] = nv

    spec = pl.BlockSpec((tr, cols), lambda i: (i, 0))
    outs = _pcall(
        body, name=name, grid=(rows // tr,), in_specs=[spec] * 4, out_specs=[spec] * 3,
        out_shape=[jax.ShapeDtypeStruct((rows, cols), F32)] * 3,
        compiler_params=_params(("parallel",)),
    )(w2, g2, m2, v2)
    return tuple(o.reshape(shape) for o in outs)


def _mesh_pos():
    return lax.axis_index("x"), lax.axis_index("y"), lax.axis_index("c")


_RELATIONS = [(dx, dy, dc) for dx in (0, 1) for dy in (0, 1) for dc in (0, 1)][1:]


def _peer(pos, rel):
    return tuple(jnp.where(r == 1, 1 - p, p) if r else p for p, r in zip(pos, rel))


def _block_index(pos):
    return 4 * pos[0] + 2 * pos[1] + pos[2]


_HBM = pl.BlockSpec(memory_space=pltpu.HBM)


def _all_gather(xs, name):
    n = len(xs)

    def body(*refs):
        x_refs, out_refs, (send_sems, recv_sems, local_sems) = refs[:n], refs[n:2 * n], refs[2 * n:]
        x_, y_, c_ = _mesh_pos()
        me, sibling = (x_, y_, c_), (x_, y_, 1 - c_)
        chips = [(1 - x_, y_), (x_, 1 - y_), (1 - x_, 1 - y_)]

        def copy(t, k, block, to, own=False):
            slot = out_refs[t].at[_block_index(block)]
            return pltpu.make_async_remote_copy(
                src_ref=x_refs[t] if own else slot, dst_ref=slot, send_sem=send_sems.at[7 * t + k], recv_sem=recv_sems.at[7 * t + k],
                device_id=to, device_id_type=pl.DeviceIdType.MESH)

        mine = [pltpu.make_async_copy(x_refs[t], out_refs[t].at[_block_index(me)], local_sems.at[t]) for t in range(n)]
        first = [[copy(t, 0, me, sibling, own=True)] + [copy(t, 1 + j, me, (*chip, c_), own=True) for j, chip in enumerate(chips)]
                 for t in range(n)]
        passed = [[copy(t, 4 + j, (*chip, c_), sibling) for j, chip in enumerate(chips)] for t in range(n)]
        for t in range(n):
            mine[t].start()
            for cp in first[t]:
                cp.start()
        for t in range(n):
            for j, chip in enumerate(chips):
                copy(t, 1 + j, (*chip, c_), me).wait_recv()
                passed[t][j].start()
        for t in range(n):
            copy(t, 0, sibling, me).wait_recv()
            for j, chip in enumerate(chips):
                copy(t, 4 + j, (*chip, 1 - c_), me).wait_recv()
            for cp in first[t] + passed[t]:
                cp.wait_send()
            mine[t].wait()

    return _pcall(
        body, name=name, in_specs=[_HBM] * n, out_specs=[_HBM] * n,
        out_shape=[jax.ShapeDtypeStruct((N_DEV,) + x.shape, x.dtype) for x in xs],
        scratch_shapes=[pltpu.SemaphoreType.DMA((7 * n,)), pltpu.SemaphoreType.DMA((7 * n,)), pltpu.SemaphoreType.DMA((n,))],
    )(*xs)


def _exchange_blocks(p, name):
    _, rows, cols = p.shape

    def body(p_ref, out_ref, send_sems, recv_sems, local_sem):
        me = _mesh_pos()
        mine = pltpu.make_async_copy(p_ref.at[_block_index(me)], out_ref.at[_block_index(me)], local_sem)
        mine.start()
        sends = []
        for k, rel in enumerate(_RELATIONS):
            peer = _peer(me, rel)
            sends.append(pltpu.make_async_remote_copy(
                src_ref=p_ref.at[_block_index(peer)], dst_ref=out_ref.at[_block_index(me)],
                send_sem=send_sems.at[k], recv_sem=recv_sems.at[k], device_id=peer, device_id_type=pl.DeviceIdType.MESH))
        for cp in sends:
            cp.start()
        for k, rel in enumerate(_RELATIONS):
            peer = _peer(me, rel)
            pltpu.make_async_remote_copy(
                src_ref=p_ref.at[_block_index(peer)], dst_ref=out_ref.at[_block_index(peer)],
                send_sem=send_sems.at[k], recv_sem=recv_sems.at[k], device_id=peer, device_id_type=pl.DeviceIdType.MESH).wait_recv()
        for cp in sends:
            cp.wait_send()
        mine.wait()

    return _pcall(
        body, name=name, in_specs=[_HBM], out_specs=_HBM,
        out_shape=jax.ShapeDtypeStruct(p.shape, p.dtype),
        scratch_shapes=[pltpu.SemaphoreType.DMA((7,)), pltpu.SemaphoreType.DMA((7,)), pltpu.SemaphoreType.DMA(())],
    )(p)


_SEM = pl.BlockSpec(memory_space=pltpu.SEMAPHORE)
_EFFECT = pltpu.SideEffectType.DATAFLOW_SIDE_EFFECTING


def _push_start(src, after, name):
    per_peer = src.ndim == 3
    rows, cols = src.shape[-2:]
    land_shape = (N_DEV, rows, cols)

    def body(src_ref, land_ref, after_ref, send_sem, recv_sem, src_thru, land_thru, token):
        me = _mesh_pos()
        for rel in _RELATIONS:
            peer = _peer(me, rel)
            pltpu.make_async_remote_copy(
                src_ref=src_ref.at[_block_index(peer)] if per_peer else src_ref, dst_ref=land_ref.at[_block_index(me)],
                send_sem=send_sem, recv_sem=recv_sem, device_id=peer, device_id_type=pl.DeviceIdType.MESH).start()
        token[...] = jnp.zeros_like(token)

    send_sem, recv_sem, src_thru, land_thru, token = _pcall(
        body, name=name,
        out_shape=(pltpu.SemaphoreType.DMA(()), pltpu.SemaphoreType.DMA(()), pltpu.HBM(src.shape, src.dtype),
                   pltpu.HBM(land_shape, src.dtype), jax.ShapeDtypeStruct((8, 128), F32)),
        in_specs=(_HBM, _HBM, pl.BlockSpec(memory_space=pl.ANY)), out_specs=(_SEM, _SEM, _HBM, _HBM, pl.BlockSpec(memory_space=pltpu.VMEM)),
        input_output_aliases={0: 2, 1: 3}, compiler_params=pltpu.CompilerParams(has_side_effects=_EFFECT),
    )(pltpu.with_memory_space_constraint(src, pltpu.HBM), pltpu.with_memory_space_constraint(lax.empty(land_shape, src.dtype), pltpu.HBM), after)
    return (send_sem, recv_sem, src_thru, land_thru), token


def _push_wait(handle, after, own, me, name):
    send_sem, recv_sem, src_thru, land_thru = handle

    def body(src_ref, land_ref, send_sem, recv_sem, after_ref, src_dead, land_out):
        sent = land_ref.at[pl.ds(0, N_DEV - 1)]
        all_seven = pltpu.make_async_remote_copy(src_ref=sent, dst_ref=sent, send_sem=send_sem, recv_sem=recv_sem,
                                                 device_id=_mesh_pos(), device_id_type=pl.DeviceIdType.MESH)
        all_seven.wait_send()
        all_seven.wait_recv()

    land = _pcall(
        body, name=name,
        out_shape=(pltpu.HBM(src_thru.shape, src_thru.dtype), pltpu.HBM(land_thru.shape, land_thru.dtype)),
        in_specs=(_HBM, _HBM, _SEM, _SEM, pl.BlockSpec(memory_space=pl.ANY)), out_specs=(_HBM, _HBM),
        input_output_aliases={0: 0, 1: 1}, compiler_params=pltpu.CompilerParams(has_side_effects=_EFFECT),
    )(src_thru, land_thru, send_sem, recv_sem, after)[1]
    return lax.dynamic_update_slice(land, own[None], (me, 0, 0))


def _sum_blocks(p, name):
    n, rows, cols = p.shape
    tr = _pick(rows, (256, 128, 64, PACK_ROWS, 16, 8))

    def body(p_ref, o_ref):
        acc = p_ref[0].astype(F32)
        for s in range(1, n):
            acc = acc + p_ref[s].astype(F32)
        o_ref[...] = acc

    return _pcall(
        body, name=name, grid=(rows // tr,),
        in_specs=[pl.BlockSpec((n, tr, cols), lambda i: (0, i, 0))], out_specs=pl.BlockSpec((tr, cols), lambda i: (i, 0)),
        out_shape=jax.ShapeDtypeStruct((rows, cols), F32), compiler_params=_params(("parallel",)),
    )(p)


def _sum_rows(t, name):
    def body(t_ref, o_ref):
        o_ref[...] = jnp.sum(t_ref[...], axis=0, keepdims=True)

    return _pcall(body, name=name, out_shape=jax.ShapeDtypeStruct((1, t.shape[1]), F32))(t)


class _Packing:
    def __init__(self, sizes, align, total_align=None):
        self.offsets, self.sizes, self.align = {}, dict(sizes), align
        row = 0
        for name, size in sizes:
            self.offsets[name] = row
            row += -(-size // (align * PACK_COLS)) * align
        total_align = total_align or align
        self.rows = -(-row // total_align) * total_align
        self.tail = self.rows - row

    def pack(self, pieces):
        return self.pack_blocks({n: pieces[n].reshape(1, -1) for n in self.sizes})[0]

    def pack_blocks(self, pieces):
        out = []
        for n, size in self.sizes.items():
            padded = -(-size // (self.align * PACK_COLS)) * self.align * PACK_COLS
            out.append(jnp.pad(pieces[n], ((0, 0), (0, padded - size))).reshape(pieces[n].shape[0], -1, PACK_COLS))
        if self.tail:
            out.append(jnp.zeros((out[0].shape[0], self.tail, PACK_COLS), out[0].dtype))
        return jnp.concatenate(out, axis=1)

    def piece(self, packed, name, lead=()):
        start, size = self.offsets[name], self.sizes[name]
        nrow = -(-size // (self.align * PACK_COLS)) * self.align
        sl = packed[..., start:start + nrow, :]
        return sl.reshape(lead + (nrow * PACK_COLS,))[..., :size]


_PIECES = (("ev_w_in", "ev_w_in", 0, 1, "a"), ("w_out0", "w_out", 0, 0, "a"),
           ("mlp_w1_0", "mlp_w1", 0, 1, "b"), ("mlp_w2_0", "mlp_w2", 0, 0, "b"),
           ("od_w_in", "od_w_in", 0, 1, "c"), ("od_w_uq", "od_w_uq", 0, 1, "c"), ("od_w_ukv", "od_w_ukv", 0, 1, "c"),
           ("w_out1", "w_out", 1, 0, "c"), ("mlp_w1_1", "mlp_w1", 1, 1, "c"), ("mlp_w2_1", "mlp_w2", 1, 0, "c"))
_SMALL_SHARDED = (("od_q_norm_g", 1), ("od_conv_w", 2), ("od_conv_b", 1), ("od_ln_g", 1), ("od_ln_b", 1))
_REPLICATED = ("c_ctx", "norm1_g", "norm2_g", "ev_q_norm_g", "ev_k_norm_g", "ev_sgu_norm_g", "ev_sgu_w", "ev_sgu_b",
               "od_kv_norm_g", "final_g")


def _unshard(blocks, axis):
    moved = jnp.moveaxis(blocks, 0, axis)
    shape = moved.shape
    return moved.reshape(shape[:axis] + (shape[axis] * shape[axis + 1],) + shape[axis + 2:])


def _shard_blocks(full, axis):
    shape = full.shape
    split = full.reshape(shape[:axis] + (N_DEV, shape[axis] // N_DEV) + shape[axis + 1:])
    return jnp.moveaxis(split, axis, 0)


def _group_mean_matrix(width, group):
    idx = jnp.arange(width) // group
    return (idx[:, None] == idx[None, :]).astype(F32) / group


def _swap_matrix(width):
    idx = jnp.arange(width)
    return ((idx[:, None] ^ 1) == idx[None, :]).astype(F32)


def _angles(length, d_rot):
    rows = length // GRID_W
    row = jnp.broadcast_to(jnp.arange(rows)[:, None], (rows, GRID_W)).reshape(-1).astype(F32)
    col = jnp.broadcast_to(jnp.arange(GRID_W)[None, :], (rows, GRID_W)).reshape(-1).astype(F32)
    d_axis = d_rot // 2
    inv = ROPE_THETA ** (-jnp.arange(0, d_axis, 2, dtype=F32) / d_axis)
    return jnp.concatenate([row[:, None] * inv, col[:, None] * inv], axis=-1)


def _rope_tables(length, d_rot, head_dim, heads, ctx_len):
    ang = _angles(length, d_rot)
    cos = jnp.repeat(jnp.cos(ang), 2, axis=1)
    sin = jnp.repeat(jnp.sin(ang), 2, axis=1) * jnp.tile(jnp.array([-1.0, 1.0], F32), d_rot // 2)
    keep = head_dim - d_rot
    cos = jnp.concatenate([jnp.ones((length, keep), F32), cos], axis=1)
    sin = jnp.concatenate([jnp.zeros((length, keep), F32), sin], axis=1)
    cos, sin = jnp.tile(cos, (1, heads)), jnp.tile(sin, (1, heads))
    cos = jnp.concatenate([jnp.ones((ctx_len, cos.shape[1]), F32), cos], axis=0)
    sin = jnp.concatenate([jnp.zeros((ctx_len, sin.shape[1]), F32), sin], axis=0)
    return cos, sin


def _to_heads(t, heads):
    b, l, w = t.shape
    return t.reshape(b, l, heads, w // heads).transpose(0, 2, 1, 3)


def _from_heads(t):
    b, h, l, d = t.shape
    return t.transpose(0, 2, 1, 3).reshape(b, l, h * d)


def _segment_params(mod, bsz):
    parts = jnp.split(mod, N_MOD, axis=-1)
    out = []
    for part in parts:
        lat = part[:bsz]
        ctx = jnp.broadcast_to(part[bsz:bsz + 1], lat.shape)
        out.append(jnp.stack([ctx, lat], axis=1)[:, :, None, :])
    return out


def _flat(t):
    return t.reshape(-1, t.shape[-1])


def _sequence_rowwise(ctx_len):
    tl = math.gcd(256, ctx_len)

    def make(name, f, out_specs, rows_per_block=tl, ctx_blocks=ctx_len // tl):
        return _rowwise(name, f, out_specs, rows_per_block, ctx_blocks)

    return make


def _mixer0(xall, modrows0, w, bsz, length, ctx_len):
    d = xall.shape[-1]
    total = ctx_len + length
    rowwise, flat = _sequence_rowwise(ctx_len), _flat
    sh1, sc1, g1, sh2, sc2, _ = _segment_params(modrows0, bsz)
    (h,) = rowwise("mod0", _f_modulate, [(d, BF16)])((xall,), (), (sh1, sc1), (w["norm1_g0"],), ())
    proj = _linear("ev_in")(flat(h), w["ev_w_in"]).reshape(bsz, total, -1)
    ev_q, ev_kv = A_Q_HEADS * A_HEAD_DIM, A_KV_HEADS * A_HEAD_DIM
    half = B_GROUPS * B_GROUP_DIM
    qp, kp, vp, zu, zv = jnp.split(proj, [ev_q, ev_q + ev_kv, ev_q + 2 * ev_kv, ev_q + 2 * ev_kv + half], axis=-1)
    cos_q, sin_q = _rope_tables(length, A_HEAD_DIM, A_HEAD_DIM, A_Q_HEADS, ctx_len)
    cos_k, sin_k = cos_q[:, :ev_kv], sin_q[:, :ev_kv]
    (q,) = rowwise("ev_q", _f_headnorm_rope, [(ev_q, BF16)])(
        (qp,), (cos_q, sin_q), (), (jnp.tile(w["ev_q_norm_g"][0], A_Q_HEADS)[None],), (_group_mean_matrix(ev_q, A_HEAD_DIM), _swap_matrix(ev_q)))
    (k,) = rowwise("ev_k", _f_headnorm_rope, [(ev_kv, BF16)])(
        (kp,), (cos_k, sin_k), (), (jnp.tile(w["ev_k_norm_g"][0], A_KV_HEADS)[None],), (_group_mean_matrix(ev_kv, A_HEAD_DIM), _swap_matrix(ev_kv)))
    o_att = _attention("gqa", ctx_len)(_to_heads(q, A_Q_HEADS), _to_heads(k, A_KV_HEADS), _to_heads(vp.astype(BF16), A_KV_HEADS))
    u, vn = rowwise("sgu_pre", _f_sgu_pre, [(half, F32), (half, BF16)])(
        (zu, zv), (), (), (w["ev_sgu_norm_g"][0].reshape(1, half),), (_group_mean_matrix(half, B_GROUP_DIM),))
    bias = jnp.repeat(w["ev_sgu_b"][0].T, B_GROUP_DIM, axis=1)
    (o_sgu,) = rowwise("sgu_mix", _f_sgu_mix, [(half, BF16)], rows_per_block=B_CHUNK, ctx_blocks=0)(
        (u, vn), (), (), tuple(w["ev_sgu_w"][0][g] for g in range(B_GROUPS)) + (bias,), ())
    o = jnp.concatenate([_from_heads(o_att), o_sgu], axis=-1)
    y = _linear("out0")(flat(o), w["w_out0"]).reshape(bsz, total, d)
    x1, h = rowwise("res_mod0a", _f_res_modulate, [(d, F32), (d, BF16)])((xall, y), (), (g1, sh2, sc2), (w["norm2_g0"],), ())
    return x1, h


def _mlp0(x1, h, modrows0, modrows1, w, bsz, length, ctx_len):
    d = x1.shape[-1]
    total = ctx_len + length
    g2 = _segment_params(modrows0, bsz)[5]
    sh1, sc1 = _segment_params(modrows1, bsz)[:2]
    y = _mlp("mlp0")(_flat(h), w["mlp_w1_0"], w["mlp_w2_0"]).reshape(bsz, total, d)
    return _sequence_rowwise(ctx_len)("res_mod0b", _f_res_modulate, [(d, F32), (d, BF16)])((x1, y), (), (g2, sh1, sc1), (w["norm1_g1"],), ())


def _layer1(x2, h, modrows1, w, bsz, length, ctx_len):
    d = x2.shape[-1]
    total = ctx_len + length
    half = B_GROUPS * B_GROUP_DIM
    rowwise, flat = _sequence_rowwise(ctx_len), _flat
    _, _, g1n, sh2n, sc2n, g2n = _segment_params(modrows1, bsz)
    proj = _linear("od_in")(flat(h), w["od_w_in"]).reshape(bsz, total, -1)
    cq, ckv, kr, za, zg = jnp.split(proj, [C_Q_RANK, C_Q_RANK + C_KV_RANK, C_Q_RANK + C_KV_RANK + C_ROPE,
                                          C_Q_RANK + C_KV_RANK + C_ROPE + half], axis=-1)
    lat = slice(ctx_len, total)
    lat_tl = math.gcd(256, length)
    (cqn,) = _rowwise("od_qn", _f_rms, [(C_Q_RANK, BF16)], lat_tl)((cq[:, lat],), (), (), (w["od_q_norm_g"],), ())
    c_qk = C_NOPE + C_ROPE
    qf = _linear("od_uq")(flat(cqn), w["od_w_uq"]).reshape(bsz, length, C_HEADS * c_qk)
    cos_q, sin_q = _rope_tables(length, C_ROPE, c_qk, C_HEADS, 0)
    (q,) = _rowwise("od_qrope", _f_rope, [(C_HEADS * c_qk, BF16)], lat_tl)((qf,), (cos_q, sin_q), (), (), (_swap_matrix(C_HEADS * c_qk),))
    (ckvn,) = rowwise("od_kvn", _f_rms, [(C_KV_RANK, BF16)])((ckv,), (), (), (w["od_kv_norm_g"],), ())
    kv = _linear("od_ukv")(flat(ckvn), w["od_w_ukv"]).reshape(bsz, total, C_HEADS, C_NOPE + C_V)
    cos_r, sin_r = _rope_tables(length, C_ROPE, C_ROPE, 1, ctx_len)
    (krr,) = rowwise("od_krope", _f_rope, [(C_ROPE, BF16)])((kr,), (cos_r, sin_r), (), (), (_swap_matrix(C_ROPE),))
    kn = kv[..., :C_NOPE].astype(BF16).transpose(0, 2, 1, 3)
    vv = kv[..., C_NOPE:].astype(BF16).transpose(0, 2, 1, 3)
    kfull = jnp.concatenate([kn, jnp.broadcast_to(krr[:, None], (bsz, C_HEADS, total, C_ROPE))], axis=-1)
    o_att = _attention("mla", 0)(_to_heads(q, C_HEADS), kfull, vv)
    (glu,) = _rowwise("glu", _f_glu, [(half, F32)], lat_tl)((za[:, lat], zg[:, lat]), (), (), (), ())
    taps = jnp.concatenate([w["od_conv_w"][0], w["od_conv_b"]], axis=0)
    conv = _dwconv(glu, taps)
    (o_conv,) = _rowwise("ln_silu", _f_ln_silu, [(half, BF16)], lat_tl)((conv,), (), (), (w["od_ln_g"], w["od_ln_b"]), ())
    o = jnp.concatenate([_from_heads(o_att), o_conv], axis=-1)
    y = _linear("out1")(flat(o), w["w_out1"]).reshape(bsz, length, d)
    lat_param = lambda p: p[:, 1:]
    x3, h = _rowwise("res_mod1a", _f_res_modulate, [(d, F32), (d, BF16)], lat_tl)(
        (x2[:, lat], y), (), (lat_param(g1n), lat_param(sh2n), lat_param(sc2n)), (w["norm2_g1"],), ())
    y = _mlp("mlp1")(flat(h), w["mlp_w1_1"], w["mlp_w2_1"]).reshape(bsz, length, d)
    (x4,) = _rowwise("res1b", _f_res, [(d, F32)], lat_tl)((x3, y), (), (lat_param(g2n),), (), ())
    return x4


def kernel(x, c, ctx, c_ctx, ada_w, ada_b, norm1_g, norm2_g, w_out, mlp_w1, mlp_w2, ev_w_in, ev_q_norm_g, ev_k_norm_g, ev_sgu_norm_g, ev_sgu_w, ev_sgu_b, od_w_in, od_q_norm_g, od_kv_norm_g, od_w_uq, od_w_ukv, od_conv_w, od_conv_b, od_ln_g, od_ln_b, final_g, loss_target, m_c_ctx, m_ada_w, m_ada_b, m_norm1_g, m_norm2_g, m_w_out, m_mlp_w1, m_mlp_w2, m_ev_w_in, m_ev_q_norm_g, m_ev_k_norm_g, m_ev_sgu_norm_g, m_ev_sgu_w, m_ev_sgu_b, m_od_w_in, m_od_q_norm_g, m_od_kv_norm_g, m_od_w_uq, m_od_w_ukv, m_od_conv_w, m_od_conv_b, m_od_ln_g, m_od_ln_b, m_final_g, v_c_ctx, v_ada_w, v_ada_b, v_norm1_g, v_norm2_g, v_w_out, v_mlp_w1, v_mlp_w2, v_ev_w_in, v_ev_q_norm_g, v_ev_k_norm_g, v_ev_sgu_norm_g, v_ev_sgu_w, v_ev_sgu_b, v_od_w_in, v_od_q_norm_g, v_od_kv_norm_g, v_od_w_uq, v_od_w_ukv, v_od_conv_w, v_od_conv_b, v_od_ln_g, v_od_ln_b, v_final_g):
    names = ["c_ctx", "ada_w", "ada_b", "norm1_g", "norm2_g", "w_out", "mlp_w1", "mlp_w2", "ev_w_in", "ev_q_norm_g", "ev_k_norm_g",
             "ev_sgu_norm_g", "ev_sgu_w", "ev_sgu_b", "od_w_in", "od_q_norm_g", "od_kv_norm_g", "od_w_uq", "od_w_ukv", "od_conv_w",
             "od_conv_b", "od_ln_g", "od_ln_b", "final_g"]
    local = dict(zip(names, [c_ctx, ada_w, ada_b, norm1_g, norm2_g, w_out, mlp_w1, mlp_w2, ev_w_in, ev_q_norm_g, ev_k_norm_g, ev_sgu_norm_g, ev_sgu_w, ev_sgu_b, od_w_in, od_q_norm_g, od_kv_norm_g, od_w_uq, od_w_ukv, od_conv_w, od_conv_b, od_ln_g, od_ln_b, final_g]))
    mom1 = dict(zip(names, [m_c_ctx, m_ada_w, m_ada_b, m_norm1_g, m_norm2_g, m_w_out, m_mlp_w1, m_mlp_w2, m_ev_w_in, m_ev_q_norm_g, m_ev_k_norm_g, m_ev_sgu_norm_g, m_ev_sgu_w, m_ev_sgu_b, m_od_w_in, m_od_q_norm_g, m_od_kv_norm_g, m_od_w_uq, m_od_w_ukv, m_od_conv_w, m_od_conv_b, m_od_ln_g, m_od_ln_b, m_final_g]))
    mom2 = dict(zip(names, [v_c_ctx, v_ada_w, v_ada_b, v_norm1_g, v_norm2_g, v_w_out, v_mlp_w1, v_mlp_w2, v_ev_w_in, v_ev_q_norm_g, v_ev_k_norm_g, v_ev_sgu_norm_g, v_ev_sgu_w, v_ev_sgu_b, v_od_w_in, v_od_q_norm_g, v_od_kv_norm_g, v_od_w_uq, v_od_w_ukv, v_od_conv_w, v_od_conv_b, v_od_ln_g, v_od_ln_b, v_final_g]))
    bsz, length, d = x.shape
    ctx_len = ctx.shape[1]
    me = _block_index(_mesh_pos())

    shard = {p: local[wn][layer] for p, wn, layer, _, _ in _PIECES}
    stages = {s: [(p, axis) for p, _, _, axis, st in _PIECES if st == s] for s in "abc"}
    packs = {s: _Packing([(p, shard[p].size) for p, _ in stages[s]], ROW_ALIGN, PACK_ROWS) for s in "abc"}
    mine = {s: packs[s].pack({p: shard[p].astype(BF16) for p, _ in stages[s]}) for s in "abc"}
    tiny_pack = _Packing([(n, local[n].size) for n, _ in _SMALL_SHARDED], 8)

    def unpack_weights(s, gathered):
        return {p: _unshard(packs[s].piece(gathered, p, (N_DEV,)).reshape((N_DEV,) + shard[p].shape), axis) for p, axis in stages[s]}

    def pack_grads(s, g):
        return packs[s].pack_blocks({p: _shard_blocks(g[p], axis).reshape(N_DEV, -1) for p, axis in stages[s]})

    cond_local = jnp.concatenate([c, c_ctx[None], jnp.zeros((COND_ROWS - bsz - 1, d), F32)], axis=0)
    cond, gathered_a, gathered_tiny = _all_gather(
        [cond_local, mine["a"], tiny_pack.pack({n: local[n] for n, _ in _SMALL_SHARDED})], "gather_inputs")
    cond = cond.reshape(N_DEV * COND_ROWS, d)
    silu_op = _rowwise("silu", _f_silu, [(d, F32)], N_DEV * COND_ROWS)
    silu_rows, silu_pullback = jax.vjp(lambda r: silu_op((r[None],), (), (), (), ())[0][0], cond)
    mod_cols = ada_w.shape[2]
    mod_part = jnp.concatenate([_mm(silu_rows, ada_w[i], "nn", (F32,), f"ada{i}_fwd")[0] for i in range(2)], axis=0)
    (mod_all,) = _all_gather([mod_part], "gather_mod")
    mod_all = mod_all.reshape(N_DEV, 2, N_DEV * COND_ROWS, mod_cols)
    modrows = []
    for i in range(2):
        whole = mod_all[:, i].transpose(1, 0, 2).reshape(N_DEV * COND_ROWS, N_DEV * mod_cols) + ada_b[i]
        modrows.append(lax.dynamic_slice_in_dim(whole, me * COND_ROWS, COND_ROWS, axis=0)[:bsz + 1])

    weights_a = unpack_weights("a", gathered_a)
    gather_b, token_b = _push_start(mine["b"], mod_all, "gather_weights_b_start")
    gather_c, token_c = _push_start(mine["c"], token_b, "gather_weights_c_start")
    full = {n: local[n] for n in _REPLICATED}
    for n, axis in _SMALL_SHARDED:
        full[n] = _unshard(tiny_pack.piece(gathered_tiny, n, (N_DEV,)).reshape((N_DEV,) + local[n].shape), axis)

    xall = jnp.concatenate([ctx, x], axis=1)
    modrows0, modrows1 = modrows[0] + token_c[0, 0], modrows[1]
    w_a = dict(weights_a, norm1_g0=norm1_g[0][None], norm2_g0=norm2_g[0][None],
               **{n: full[n] for n in ("ev_q_norm_g", "ev_k_norm_g", "ev_sgu_norm_g", "ev_sgu_w", "ev_sgu_b")})
    (x1, h0), pull_a = jax.vjp(lambda x_, m0, w: _mixer0(x_, m0, w, bsz, length, ctx_len), xall, modrows0, w_a)
    w_b = dict(unpack_weights("b", _push_wait(gather_b, x1, mine["b"], me, "gather_weights_b_wait")), norm1_g1=norm1_g[1][None])
    (x2, h1), pull_b = jax.vjp(lambda x_, h_, m0, m1, w: _mlp0(x_, h_, m0, m1, w, bsz, length, ctx_len), x1, h0, modrows0, modrows1, w_b)
    w_c = dict(unpack_weights("c", _push_wait(gather_c, x2, mine["c"], me, "gather_weights_c_wait")), norm2_g1=norm2_g[1][None],
               **{n: full[n] for n in ("od_q_norm_g", "od_kv_norm_g", "od_conv_w", "od_conv_b", "od_ln_g", "od_ln_b")})
    x4, pull_c = jax.vjp(lambda x_, h_, m1, w: _layer1(x_, h_, m1, w, bsz, length, ctx_len), x2, h1, modrows1, w_c)
    loss_part, dx4, dfinal = _loss_head(x4, loss_target, final_g[None])
    loss = lax.psum(loss_part[0, 0], MESH_AXES)

    dx2, dh1, dmod1_c, g_c = pull_c(dx4)
    grads_c = pack_grads("c", g_c)
    exchange_c, token = _push_start(grads_c, dx2, "exchange_grads_c_start")
    dx1, dh0, dmod0_b, dmod1_b, g_b = pull_b((dx2, dh1 + token[0, 0].astype(dh1.dtype)))
    grads_b = pack_grads("b", g_b)
    exchange_b, token = _push_start(grads_b, dx1, "exchange_grads_b_start")
    dxall, dmod0_a, g_a = pull_a((dx1, dh0 + token[0, 0].astype(dh0.dtype)))
    grad_x = dxall[:, ctx_len:]
    dmodrows = [dmod0_a + dmod0_b, dmod1_b + dmod1_c]
    grads = {n: g[n] for g in (g_a, g_c) for n in g if n in full}
    grads["norm1_g"] = jnp.concatenate([g_a["norm1_g0"], g_b["norm1_g1"]], axis=0)
    grads["norm2_g"] = jnp.concatenate([g_a["norm2_g0"], g_c["norm2_g1"]], axis=0)
    grads["final_g"] = dfinal[0]

    dmod_local = jnp.concatenate([jnp.pad(dm, ((0, COND_ROWS - bsz - 1), (0, 0))) for dm in dmodrows], axis=0)
    (dmod_all,) = _all_gather([dmod_local], "gather_dmod")
    dmod_all = dmod_all.reshape(N_DEV, 2, COND_ROWS, N_DEV * mod_cols)
    reduced = {}
    grad_ada_w, grad_ada_b, dmod_mine = [], [], []
    for i in range(2):
        dmod = dmod_all[:, i].reshape(N_DEV * COND_ROWS, N_DEV * mod_cols)
        grad_ada_b.append(_sum_rows(dmod, f"ada{i}_db")[0])
        dmod_mine.append(lax.dynamic_slice_in_dim(dmod, me * mod_cols, mod_cols, axis=1))
        grad_ada_w.append(_mm(silu_rows, dmod_mine[i], "tn", (F32,), f"ada{i}_dw")[0])
    reduced["ada_w"], reduced["ada_b"] = jnp.stack(grad_ada_w), jnp.stack(grad_ada_b)
    dsilu = _mm(jnp.concatenate(dmod_mine, axis=1), jnp.concatenate([ada_w[0], ada_w[1]], axis=1), "nt", (F32,), "ada_dx")[0]
    (dcond,) = silu_pullback(dsilu)
    grads["c_ctx"] = _sum_rows(dcond.reshape(N_DEV, COND_ROWS, d)[:, bsz], "c_ctx_rows")[0]

    def own_block(packed):
        return lax.dynamic_index_in_dim(packed, me, 0, keepdims=False)

    received = {"a": _exchange_blocks(pack_grads("a", g_a), "exchange_grads_a"),
                "b": _push_wait(exchange_b, dmod_all, own_block(grads_b), me, "exchange_grads_b_wait"),
                "c": _push_wait(exchange_c, dmod_all, own_block(grads_c), me, "exchange_grads_c_wait")}
    piece_grad = {}
    for s in "abc":
        summed = _sum_blocks(received[s], "sum_grads_" + s)
        for p, _ in stages[s]:
            piece_grad[p] = packs[s].piece(summed, p).reshape(shard[p].shape)
    for n in ("w_out", "mlp_w1", "mlp_w2"):
        reduced[n] = jnp.stack([piece_grad[p] for p, wn, _, _, _ in _PIECES if wn == n])
    for n in ("ev_w_in", "od_w_in", "od_w_uq", "od_w_ukv"):
        reduced[n] = piece_grad[n][None]

    small_names = list(_REPLICATED) + [n for n, _ in _SMALL_SHARDED]
    small_pack = _Packing([(n, full[n].size) for n in small_names], 8)
    (small_all,) = _all_gather([small_pack.pack({n: grads[n].astype(F32) for n in small_names})], "gather_small_grads")
    small_sum = _sum_blocks(small_all, "sum_small_grads")
    for n in _REPLICATED:
        reduced[n] = small_pack.piece(small_sum, n).reshape(local[n].shape)
    for n, axis in _SMALL_SHARDED:
        whole = small_pack.piece(small_sum, n).reshape(full[n].shape)
        reduced[n] = lax.dynamic_slice_in_dim(whole, me * local[n].shape[axis], local[n].shape[axis], axis=axis)

    delta, new_m, new_v = {}, {}, {}
    for n in names:
        delta[n], new_m[n], new_v[n] = _adamw(local[n], reduced[n], mom1[n], mom2[n], "adamw_" + n)
    return (loss, grad_x, *[reduced[n] for n in names], *[delta[n] for n in names], *[new_m[n] for n in names], *[new_v[n] for n in names])
```

```python
import functools
import math

import jax
import jax.numpy as jnp
from jax import lax
from jax.experimental import pallas as pl
from jax.experimental.pallas import tpu as pltpu

F32, BF16 = jnp.float32, jnp.bfloat16

EPS = 1e-6
GRID_W = 64
ROPE_THETA = 10000.0
A_HEAD_DIM, A_Q_HEADS, A_KV_HEADS = 64, 8, 2
B_GROUPS, B_GROUP_DIM, B_CHUNK = 8, 64, 128
C_HEADS, C_NOPE, C_ROPE, C_V, C_Q_RANK, C_KV_RANK = 8, 64, 32, 64, 256, 128
D_CONV = 31
CONV_PAD = D_CONV // 2
N_MOD = 6
N_DEV = 8
MESH_AXES = ("x", "y", "c")

ADAM_LR, ADAM_B1, ADAM_B2, ADAM_EPS, ADAM_WD, ADAM_STEP = 0.001, 0.9, 0.999, 1e-08, 0.01, 10

VMEM_LIMIT = 56 * 1024 * 1024
PACK_COLS = 1024
ROW_ALIGN = 16
PACK_ROWS = 32
COND_ROWS = 8


def _pcall(body, **kw):
    return pl.pallas_call(body, **kw)


def _params(sem=None):
    return pltpu.CompilerParams(dimension_semantics=sem, vmem_limit_bytes=VMEM_LIMIT)


def _pick(n, cands):
    for c in cands:
        if n % c == 0:
            return c
    return n


def _mm(a, b, mode, out_dtypes, name, epi=None, extras=()):
    if mode == "tn":
        kk, m = a.shape
    else:
        m, kk = a.shape
    n = b.shape[0] if mode == "nt" else b.shape[1]
    tm = _pick(m, (1152, 1024, 768, 512, 256, 128))
    tn = _pick(n, (1024, 896, 768, 512, 256, 128))
    tk = kk if kk <= 1024 else _pick(kk, (1024, 896, 768, 512, 256, 128))
    nk = kk // tk
    ne, no = len(extras), len(out_dtypes)
    a_spec = pl.BlockSpec((tk, tm), lambda i, j, k: (k, i)) if mode == "tn" else pl.BlockSpec((tm, tk), lambda i, j, k: (i, k))
    b_spec = pl.BlockSpec((tn, tk), lambda i, j, k: (j, k)) if mode == "nt" else pl.BlockSpec((tk, tn), lambda i, j, k: (k, j))
    t_spec = pl.BlockSpec((tm, tn), lambda i, j, k: (i, j))
    dn = {"nn": ((1,), (0,)), "nt": ((1,), (1,)), "tn": ((0,), (0,))}[mode]

    def body(a_ref, b_ref, *rest):
        extra_refs, out_refs = rest[:ne], rest[ne:ne + no]

        def finish(acc):
            outs = (acc,) if epi is None else epi(acc, *[r[...] for r in extra_refs])
            for r, o in zip(out_refs, outs):
                r[...] = o.astype(r.dtype)

        part = lax.dot_general(a_ref[...].astype(BF16), b_ref[...].astype(BF16), (dn, ((), ())), preferred_element_type=F32)
        if nk == 1:
            finish(part)
        else:
            acc_ref = rest[-1]
            k = pl.program_id(2)

            @pl.when(k == 0)
            def _():
                acc_ref[...] = part

            @pl.when(k > 0)
            def _():
                acc_ref[...] += part

            @pl.when(k == nk - 1)
            def _():
                finish(acc_ref[...])

    outs = _pcall(
        body, name=name, grid=(m // tm, n // tn, nk),
        in_specs=[a_spec, b_spec] + [t_spec] * ne,
        out_specs=[t_spec] * no,
        out_shape=[jax.ShapeDtypeStruct((m, n), d) for d in out_dtypes],
        scratch_shapes=[pltpu.VMEM((tm, tn), F32)] if nk > 1 else [],
        compiler_params=_params(("parallel", "parallel", "arbitrary")),
    )(a, b, *extras)
    return outs


def _linear(name, out_dtype=F32):
    @jax.custom_vjp
    def op(x, w):
        return _mm(x, w, "nn", (out_dtype,), name + "_fwd")[0]

    def fwd(x, w):
        return op(x, w), (x, w)

    def bwd(res, dy):
        x, w = res
        dx = _mm(dy, w, "nt", (x.dtype,), name + "_dx")[0]
        dw = _mm(x, dy, "tn", (w.dtype,), name + "_dw")[0]
        return dx, dw

    op.defvjp(fwd, bwd)
    return op


def _relu2_epi(acc):
    return jnp.square(jnp.maximum(acc, 0.0)), acc


def _relu2_bwd_epi(acc, a):
    return (acc * (2.0 * jnp.maximum(a.astype(F32), 0.0)),)


def _mlp(name):
    @jax.custom_vjp
    def op(h, w1, w2):
        s, _ = _mm(h, w1, "nn", (BF16, BF16), name + "_up", epi=_relu2_epi)
        return _mm(s, w2, "nn", (F32,), name + "_down")[0]

    def fwd(h, w1, w2):
        s, a = _mm(h, w1, "nn", (BF16, BF16), name + "_up", epi=_relu2_epi)
        return _mm(s, w2, "nn", (F32,), name + "_down")[0], (h, w1, w2, s, a)

    def bwd(res, dy):
        h, w1, w2, s, a = res
        da = _mm(dy, w2, "nt", (BF16,), name + "_ds", epi=_relu2_bwd_epi, extras=(a,))[0]
        dw2 = _mm(s, dy, "tn", (w2.dtype,), name + "_dw2")[0]
        dw1 = _mm(h, da, "tn", (w1.dtype,), name + "_dw1")[0]
        dh = _mm(da, w1, "nt", (h.dtype,), name + "_dh")[0]
        return dh, dw1, dw2

    op.defvjp(fwd, bwd)
    return op


def _two_pass_dot(x, m):
    hi = x.astype(BF16)
    lo = (x - hi.astype(F32)).astype(BF16)
    mb = m.astype(BF16)
    return jnp.dot(hi, mb, preferred_element_type=F32) + jnp.dot(lo, mb, preferred_element_type=F32)


@jax.custom_vjp
def _sym_dot(x, m):
    return _two_pass_dot(x, m)


def _sym_dot_fwd(x, m):
    return _two_pass_dot(x, m), m


def _sym_dot_bwd(m, g):
    return _two_pass_dot(g, m), jnp.zeros_like(m)


_sym_dot.defvjp(_sym_dot_fwd, _sym_dot_bwd)


def _neighbour(x):
    lane = lax.broadcasted_iota(jnp.int32, x.shape, 1)
    return jnp.where(lane % 2 == 0, pltpu.roll(x, x.shape[1] - 1, 1), pltpu.roll(x, 1, 1))


@jax.custom_vjp
def _swap_pairs(x):
    return _neighbour(x)


_swap_pairs.defvjp(lambda x: (_neighbour(x), None), lambda _, g: (_neighbour(g),))


@jax.custom_vjp
def _bdot(a, b):
    return jnp.dot(a.astype(BF16), b.astype(BF16), preferred_element_type=F32)


def _bdot_fwd(a, b):
    return _bdot(a, b), (a, b)


def _bdot_bwd(res, g):
    a, b = res
    gb = g.astype(BF16)
    da = lax.dot_general(gb, b.astype(BF16), (((1,), (1,)), ((), ())), preferred_element_type=F32)
    db = lax.dot_general(a.astype(BF16), gb, (((0,), (0,)), ((), ())), preferred_element_type=F32)
    return da, db


_bdot.defvjp(_bdot_fwd, _bdot_bwd)


def _rowwise(name, f, out_specs, tl, ctx_blocks=0):
    def seg(l, s):
        return jnp.where(l >= ctx_blocks, s - 1, 0) if s > 1 else 0

    def specs(rows, tabs, pers, glbs, consts):
        row_specs = [pl.BlockSpec((1, tl, r.shape[2]), lambda b, l: (b, l, 0)) for r in rows]
        tab_specs = [pl.BlockSpec((tl, t.shape[1]), lambda b, l: (l, 0)) for t in tabs]
        per_specs = [pl.BlockSpec((1, 1, 1, p.shape[3]), functools.partial(lambda b, l, s: (b, seg(l, s), 0, 0), s=p.shape[1])) for p in pers]
        glb_specs = [pl.BlockSpec(g.shape, functools.partial(lambda b, l, nd: (0,) * nd, nd=g.ndim)) for g in glbs]
        const_specs = [pl.BlockSpec(c.shape, functools.partial(lambda b, l, nd: (0,) * nd, nd=c.ndim)) for c in consts]
        return row_specs, tab_specs, per_specs, glb_specs, const_specs

    def load(refs_rows, refs_tabs, refs_pers, refs_glbs, refs_consts):
        return (tuple(r[0].astype(F32) for r in refs_rows), tuple(t[...] for t in refs_tabs),
                tuple(p[0, 0].astype(F32) for p in refs_pers), tuple(g[...].astype(F32) for g in refs_glbs),
                tuple(c[...] for c in refs_consts))

    def call_fwd(rows, tabs, pers, glbs, consts):
        bsz, length = rows[0].shape[:2]
        nr, nt, npp, ng, nc = len(rows), len(tabs), len(pers), len(glbs), len(consts)
        rs, ts, ps, gs, cs = specs(rows, tabs, pers, glbs, consts)

        def body(*refs):
            ins, outs = refs[:nr + nt + npp + ng + nc], refs[nr + nt + npp + ng + nc:]
            r, t, p, g, c = load(ins[:nr], ins[nr:nr + nt], ins[nr + nt:nr + nt + npp], ins[nr + nt + npp:nr + nt + npp + ng], ins[nr + nt + npp + ng:])
            for o_ref, o in zip(outs, f(r, t, p, g, c)):
                o_ref[0] = o.astype(o_ref.dtype)

        return _pcall(
            body, name=name + "_fwd", grid=(bsz, length // tl),
            in_specs=rs + ts + ps + gs + cs,
            out_specs=[pl.BlockSpec((1, tl, w), lambda b, l: (b, l, 0)) for w, _ in out_specs],
            out_shape=[jax.ShapeDtypeStruct((bsz, length, w), d) for w, d in out_specs],
            compiler_params=_params(("parallel", "parallel")),
        )(*rows, *tabs, *pers, *glbs, *consts)

    def call_bwd(rows, tabs, pers, glbs, consts, cts):
        bsz, length = rows[0].shape[:2]
        nr, nt, npp, ng, nc, no = len(rows), len(tabs), len(pers), len(glbs), len(consts), len(cts)
        rs, ts, ps, gs, cs = specs(rows, tabs, pers, glbs, consts)
        n_in = nr + nt + npp + ng + nc

        def body(*refs):
            ins, ct_refs, outs = refs[:n_in], refs[n_in:n_in + no], refs[n_in + no:]
            r, t, p, g, c = load(ins[:nr], ins[nr:nr + nt], ins[nr + nt:nr + nt + npp], ins[nr + nt + npp:nr + nt + npp + ng], ins[nr + nt + npp + ng:])
            _, vjp = jax.vjp(lambda r_, p_, g_: tuple(f(r_, t, p_, g_, c)), r, p, g)
            dr, dp, dg = vjp(tuple(ct[0].astype(F32) for ct in ct_refs))
            dr_refs, dp_refs, dg_refs = outs[:nr], outs[nr:nr + npp], outs[nr + npp:]
            for ref, d in zip(dr_refs, dr):
                ref[0] = d.astype(ref.dtype)
            b, l = pl.program_id(0), pl.program_id(1)
            first_of_segment = (l == 0) | (l == ctx_blocks)
            for ref, d in zip(dp_refs, dp):
                @pl.when(first_of_segment)
                def _(ref=ref, d=d):
                    ref[0, 0] = d

                @pl.when(jnp.logical_not(first_of_segment))
                def _(ref=ref, d=d):
                    ref[0, 0] += d
            first = (b == 0) & (l == 0)
            for ref, d in zip(dg_refs, dg):
                @pl.when(first)
                def _(ref=ref, d=d):
                    ref[...] = d

                @pl.when(jnp.logical_not(first))
                def _(ref=ref, d=d):
                    ref[...] += d

        ct_specs = [pl.BlockSpec((1, tl, w), lambda b, l: (b, l, 0)) for w, _ in out_specs]
        outs = _pcall(
            body, name=name + "_bwd", grid=(bsz, length // tl),
            in_specs=rs + ts + ps + gs + cs + ct_specs,
            out_specs=rs + ps + gs,
            out_shape=[jax.ShapeDtypeStruct(r.shape, r.dtype) for r in rows]
            + [jax.ShapeDtypeStruct(p.shape, F32) for p in pers] + [jax.ShapeDtypeStruct(g.shape, F32) for g in glbs],
            compiler_params=_params(("arbitrary", "arbitrary")),
        )(*rows, *tabs, *pers, *glbs, *consts, *cts)
        return tuple(outs[:nr]), tuple(outs[nr:nr + npp]), tuple(outs[nr + npp:])

    @jax.custom_vjp
    def op(rows, tabs, pers, glbs, consts):
        return tuple(call_fwd(rows, tabs, pers, glbs, consts))

    def fwd(rows, tabs, pers, glbs, consts):
        return op(rows, tabs, pers, glbs, consts), (rows, tabs, pers, glbs, consts)

    def bwd(res, cts):
        rows, tabs, pers, glbs, consts = res
        dr, dp, dg = call_bwd(rows, tabs, pers, glbs, consts, tuple(cts))
        dp = tuple(d.astype(p.dtype) for d, p in zip(dp, pers))
        dg = tuple(d.astype(g.dtype) for d, g in zip(dg, glbs))
        return dr, tuple(jnp.zeros_like(t) for t in tabs), dp, dg, tuple(jnp.zeros_like(c) for c in consts)

    op.defvjp(fwd, bwd)
    return op


def _rms(x, g):
    return x * lax.rsqrt(jnp.mean(x * x, axis=-1, keepdims=True) + EPS) * g


def _f_silu(r, t, p, g, c):
    return (jax.nn.silu(r[0]),)


def _f_modulate(r, t, p, g, c):
    shift, scale = p
    return (_rms(r[0], g[0]) * (1.0 + scale) + shift,)


def _f_res_modulate(r, t, p, g, c):
    x, y = r
    gate, shift, scale = p
    xn = x + gate * y
    return xn, _rms(xn, g[0]) * (1.0 + scale) + shift


def _f_res(r, t, p, g, c):
    return (r[0] + p[0] * r[1],)


def _f_headnorm_rope(r, t, p, g, c):
    x = r[0]
    cos, sin = t
    xn = x * lax.rsqrt(_sym_dot(x * x, c[0]) + EPS) * g[0]
    return (xn * cos + _swap_pairs(xn) * sin,)


def _f_rope(r, t, p, g, c):
    x = r[0]
    cos, sin = t
    return (x * cos + _swap_pairs(x) * sin,)


def _f_rope_narrow(r, t, p, g, c):
    x = r[0]
    cos, sin = t
    return (x * cos + _sym_dot(x, c[0]) * sin,)


def _f_rms(r, t, p, g, c):
    return (_rms(r[0], g[0]),)


def _f_sgu_pre(r, t, p, g, c):
    u = jax.nn.gelu(r[0])
    v = jax.nn.gelu(r[1])
    vn = v * lax.rsqrt(_sym_dot(v * v, c[0]) + EPS) * g[0]
    return u, vn


def _f_sgu_mix(r, t, p, g, c):
    u, vn = r
    bias = g[B_GROUPS]
    group = lax.broadcasted_iota(jnp.int32, vn.shape, 1) // B_GROUP_DIM
    sv = bias
    for k in range(B_GROUPS):
        sv = sv + jnp.where(group == k, _bdot(g[k], vn), 0.0)
    return (u * sv,)


def _f_glu(r, t, p, g, c):
    return (r[0] * jax.nn.sigmoid(r[1]),)


def _f_ln_silu(r, t, p, g, c):
    x = r[0]
    mu = jnp.mean(x, axis=-1, keepdims=True)
    var = jnp.mean(jnp.square(x - mu), axis=-1, keepdims=True)
    return (jax.nn.silu((x - mu) * lax.rsqrt(var + EPS) * g[0] + g[1]),)


LOG2_E = 1.4426950408889634
LN_2 = 0.6931471805599453


def _attention(name, ctx_len):
    def tiles(q):
        tq = _pick(q.shape[2], (256, 128))
        if ctx_len:
            tq = math.gcd(tq, ctx_len)
        return tq, ctx_len // tq

    def scores(q_ref, kk, dk):
        qs = (q_ref[0, 0].astype(F32) * (LOG2_E * dk ** -0.5)).astype(BF16)
        return qs, lax.dot_general(qs, kk, (((1,), (1,)), ((), ())), preferred_element_type=F32)

    def by_segment(qi, cb, lk, run):
        if cb > 0:
            @pl.when(qi < cb)
            def _():
                run(ctx_len)

            @pl.when(qi >= cb)
            def _():
                run(lk)
        else:
            run(lk)

    def call_fwd(q, k, v):
        bsz, hq, lq, dk = q.shape
        _, hk, lk, dv = v.shape
        grp = hq // hk
        tq, cb = tiles(q)
        v_ones = jnp.concatenate([v, jnp.ones_like(v)], axis=-1)

        def body(q_ref, k_ref, v_ref, o_ref, lse_ref):
            def run(nk):
                _, s = scores(q_ref, k_ref[0, 0, :nk], dk)
                m = jnp.max(s, axis=-1, keepdims=True)
                acc = jnp.dot(jnp.exp2(s - m).astype(BF16), v_ref[0, 0, :nk], preferred_element_type=F32)
                l = acc[:, dv:dv + 1]
                o_ref[0, 0] = (acc[:, :dv] / l).astype(o_ref.dtype)
                lse_ref[0, 0] = m + jnp.log2(l)

            by_segment(pl.program_id(2), cb, lk, run)

        return _pcall(
            body, name=name + "_fwd", grid=(bsz, hq, lq // tq),
            in_specs=[pl.BlockSpec((1, 1, tq, dk), lambda b, h, i: (b, h, i, 0)),
                      pl.BlockSpec((1, 1, lk, dk), lambda b, h, i: (b, h // grp, 0, 0)),
                      pl.BlockSpec((1, 1, lk, 2 * dv), lambda b, h, i: (b, h // grp, 0, 0))],
            out_specs=[pl.BlockSpec((1, 1, tq, dv), lambda b, h, i: (b, h, i, 0)), pl.BlockSpec((1, 1, tq, 1), lambda b, h, i: (b, h, i, 0))],
            out_shape=[jax.ShapeDtypeStruct((bsz, hq, lq, dv), BF16), jax.ShapeDtypeStruct((bsz, hq, lq, 1), F32)],
            compiler_params=_params(("parallel", "parallel", "parallel")),
        )(q, k, v_ones)

    def call_bwd(q, k, v, o, lse, do):
        bsz, hq, lq, dk = q.shape
        _, hk, lk, dv = v.shape
        grp = hq // hk
        tq, cb = tiles(q)
        nq = lq // tq

        def body(q_ref, k_ref, v_ref, o_ref, lse_ref, do_ref, dq_ref, dk_ref, dv_ref):
            g, qi = pl.program_id(2), pl.program_id(3)

            @pl.when((g == 0) & (qi == 0))
            def _():
                dk_ref[...] = jnp.zeros_like(dk_ref)
                dv_ref[...] = jnp.zeros_like(dv_ref)

            def run(nk):
                kk, vv, dd = k_ref[0, 0, :nk], v_ref[0, 0, :nk], do_ref[0, 0]
                qs, s = scores(q_ref, kk, dk)
                p = jnp.exp2(s - lse_ref[0, 0])
                delta = jnp.sum(dd.astype(F32) * o_ref[0, 0].astype(F32), axis=-1, keepdims=True)
                dp = lax.dot_general(dd, vv, (((1,), (1,)), ((), ())), preferred_element_type=F32)
                t = (p * (dp - delta)).astype(BF16)
                dq_ref[0, 0] = (jnp.dot(t, kk, preferred_element_type=F32) * dk ** -0.5).astype(dq_ref.dtype)
                dk_ref[0, 0, :nk] += lax.dot_general(t, qs, (((0,), (0,)), ((), ())), preferred_element_type=F32)
                dv_ref[0, 0, :nk] += lax.dot_general(p.astype(BF16), dd, (((0,), (0,)), ((), ())), preferred_element_type=F32)

            by_segment(qi, cb, lk, run)

            @pl.when((g == grp - 1) & (qi == nq - 1))
            def _():
                dk_ref[...] = dk_ref[...] * LN_2

        q_spec = pl.BlockSpec((1, 1, tq, dk), lambda b, h, g, i: (b, h * grp + g, i, 0))
        k_spec = pl.BlockSpec((1, 1, lk, dk), lambda b, h, g, i: (b, h, 0, 0))
        v_spec = pl.BlockSpec((1, 1, lk, dv), lambda b, h, g, i: (b, h, 0, 0))
        o_spec = pl.BlockSpec((1, 1, tq, dv), lambda b, h, g, i: (b, h * grp + g, i, 0))
        lse_spec = pl.BlockSpec((1, 1, tq, 1), lambda b, h, g, i: (b, h * grp + g, i, 0))
        return _pcall(
            body, name=name + "_bwd", grid=(bsz, hk, grp, nq),
            in_specs=[q_spec, k_spec, v_spec, o_spec, lse_spec, o_spec],
            out_specs=[q_spec, k_spec, v_spec],
            out_shape=[jax.ShapeDtypeStruct(q.shape, BF16), jax.ShapeDtypeStruct(k.shape, F32), jax.ShapeDtypeStruct(v.shape, F32)],
            compiler_params=_params(("parallel", "parallel", "arbitrary", "arbitrary")),
        )(q, k, v, o, lse, do)

    @jax.custom_vjp
    def op(q, k, v):
        return call_fwd(q, k, v)[0]

    def fwd(q, k, v):
        o, lse = call_fwd(q, k, v)
        return o, (q, k, v, o, lse)

    def bwd(res, do):
        q, k, v, o, lse = res
        dq, dk, dv = call_bwd(q, k, v, o, lse, do)
        return dq, dk.astype(k.dtype), dv.astype(v.dtype)

    op.defvjp(fwd, bwd)
    return op


def _conv_call(ypad, taps, name):
    bsz, lp, ch = ypad.shape
    length = lp - 2 * ROW_ALIGN
    tl = _pick(length, (256, 128))

    def body(y_ref, w_ref, o_ref):
        base = pl.multiple_of(pl.program_id(1) * tl, tl)
        win = y_ref[0, pl.ds(base, tl + 2 * ROW_ALIGN), :]
        acc = jnp.broadcast_to(w_ref[pl.ds(D_CONV, 1), :], (tl, ch))
        for k in range(D_CONV):
            acc = acc + win[k:k + tl, :] * w_ref[pl.ds(k, 1), :]
        o_ref[0] = acc

    return _pcall(
        body, name=name, grid=(bsz, length // tl),
        in_specs=[pl.BlockSpec((1, lp, ch), lambda b, l: (b, 0, 0)), pl.BlockSpec((D_CONV + 1, ch), lambda b, l: (0, 0))],
        out_specs=pl.BlockSpec((1, tl, ch), lambda b, l: (b, l, 0)),
        out_shape=jax.ShapeDtypeStruct((bsz, length, ch), F32),
        compiler_params=_params(("parallel", "parallel")),
    )(ypad, taps)


def _conv_dw_call(ypad, dout, name):
    bsz, lp, ch = ypad.shape
    length = lp - 2 * ROW_ALIGN
    tl = _pick(length, (256, 128))

    def body(y_ref, d_ref, o_ref):
        b, l = pl.program_id(0), pl.program_id(1)

        @pl.when((b == 0) & (l == 0))
        def _():
            o_ref[...] = jnp.zeros_like(o_ref)

        base = pl.multiple_of(l * tl, tl)
        win = y_ref[0, pl.ds(base, tl + 2 * ROW_ALIGN), :]
        dd = d_ref[0]
        for k in range(D_CONV):
            o_ref[pl.ds(k, 1), :] += jnp.sum(win[k:k + tl, :] * dd, axis=0, keepdims=True)
        o_ref[pl.ds(D_CONV, 1), :] += jnp.sum(dd, axis=0, keepdims=True)

    return _pcall(
        body, name=name, grid=(bsz, length // tl),
        in_specs=[pl.BlockSpec((1, lp, ch), lambda b, l: (b, 0, 0)), pl.BlockSpec((1, tl, ch), lambda b, l: (b, l, 0))],
        out_specs=pl.BlockSpec((D_CONV + 1, ch), lambda b, l: (0, 0)),
        out_shape=jax.ShapeDtypeStruct((D_CONV + 1, ch), F32),
        compiler_params=_params(("arbitrary", "arbitrary")),
    )(ypad, dout)


def _pad_rows(y):
    return jnp.pad(y, ((0, 0), (CONV_PAD, 2 * ROW_ALIGN - CONV_PAD), (0, 0)))


@jax.custom_vjp
def _dwconv(y, taps):
    return _conv_call(_pad_rows(y), taps, "conv_fwd")


def _dwconv_fwd(y, taps):
    return _dwconv(y, taps), (y, taps)


def _dwconv_bwd(res, dout):
    y, taps = res
    flipped = jnp.concatenate([taps[:D_CONV][::-1], jnp.zeros_like(taps[D_CONV:])], axis=0)
    dy = _conv_call(_pad_rows(dout), flipped, "conv_dy")
    dtaps = _conv_dw_call(_pad_rows(y), dout, "conv_dw")
    return dy, dtaps


_dwconv.defvjp(_dwconv_fwd, _dwconv_bwd)


def _loss_head(x, target, g):
    bsz, length, d = x.shape
    tl = _pick(length, (256, 128))

    def f(xb, tb, gb):
        err = _rms(xb, gb) - tb
        return 0.5 * jnp.sum(jnp.sum(err * err, axis=-1, keepdims=True), axis=0, keepdims=True) / d

    def body(x_ref, t_ref, g_ref, loss_ref, dx_ref, dg_ref):
        val, vjp = jax.vjp(lambda xb, gb: f(xb, t_ref[0], gb), x_ref[0], g_ref[...])
        dx, dg = vjp(jnp.ones((1, 1), F32))
        dx_ref[0] = dx
        first = (pl.program_id(0) == 0) & (pl.program_id(1) == 0)

        @pl.when(first)
        def _():
            loss_ref[...] = val
            dg_ref[...] = dg

        @pl.when(jnp.logical_not(first))
        def _():
            loss_ref[...] += val
            dg_ref[...] += dg

    row = pl.BlockSpec((1, tl, d), lambda b, l: (b, l, 0))
    return _pcall(
        body, name="loss_head", grid=(bsz, length // tl),
        in_specs=[row, row, pl.BlockSpec((1, d), lambda b, l: (0, 0))],
        out_specs=[pl.BlockSpec((1, 1), lambda b, l: (0, 0)), row, pl.BlockSpec((1, d), lambda b, l: (0, 0))],
        out_shape=[jax.ShapeDtypeStruct((1, 1), F32), jax.ShapeDtypeStruct(x.shape, F32), jax.ShapeDtypeStruct((1, d), F32)],
        compiler_params=_params(("arbitrary", "arbitrary")),
    )(x, target, g)


def _adamw(w, g, m, v, name):
    shape = w.shape
    cols = shape[-1]
    rows = w.size // cols
    tr = _pick(rows, (512, 256, 128))
    w2, g2, m2, v2 = (t.reshape(rows, cols) for t in (w, g, m, v))

    def body(w_ref, g_ref, m_ref, v_ref, d_ref, nm_ref, nv_ref):
        gg = g_ref[...]
        nm = ADAM_B1 * m_ref[...] + (1.0 - ADAM_B1) * gg
        nv = ADAM_B2 * v_ref[...] + (1.0 - ADAM_B2) * jnp.square(gg)
        m_hat = nm / (1.0 - ADAM_B1 ** ADAM_STEP)
        v_hat = nv / (1.0 - ADAM_B2 ** ADAM_STEP)
        d_ref[...] = -ADAM_LR * (m_hat / (jnp.sqrt(v_hat) + ADAM_EPS) + ADAM_WD * w_ref[...])
        nm_ref[...] = nm
        nv_ref[...] = nv

    spec = pl.BlockSpec((tr, cols), lambda i: (i, 0))
    outs = _pcall(
        body, name=name, grid=(rows // tr,), in_specs=[spec] * 4, out_specs=[spec] * 3,
        out_shape=[jax.ShapeDtypeStruct((rows, cols), F32)] * 3,
        compiler_params=_params(("parallel",)),
    )(w2, g2, m2, v2)
    return tuple(o.reshape(shape) for o in outs)


def _mesh_pos():
    return lax.axis_index("x"), lax.axis_index("y"), lax.axis_index("c")


_RELATIONS = [(dx, dy, dc) for dx in (0, 1) for dy in (0, 1) for dc in (0, 1)][1:]


def _peer(pos, rel):
    return tuple(jnp.where(r == 1, 1 - p, p) if r else p for p, r in zip(pos, rel))


def _block_index(pos):
    return 4 * pos[0] + 2 * pos[1] + pos[2]


_HBM = pl.BlockSpec(memory_space=pltpu.HBM)


def _all_gather(xs, name):
    n = len(xs)

    def body(*refs):
        x_refs, out_refs, (send_sems, recv_sems, local_sems) = refs[:n], refs[n:2 * n], refs[2 * n:]
        x_, y_, c_ = _mesh_pos()
        me, sibling = (x_, y_, c_), (x_, y_, 1 - c_)
        chips = [(1 - x_, y_), (x_, 1 - y_), (1 - x_, 1 - y_)]

        def copy(t, k, block, to, own=False):
            slot = out_refs[t].at[_block_index(block)]
            return pltpu.make_async_remote_copy(
                src_ref=x_refs[t] if own else slot, dst_ref=slot, send_sem=send_sems.at[7 * t + k], recv_sem=recv_sems.at[7 * t + k],
                device_id=to, device_id_type=pl.DeviceIdType.MESH)

        mine = [pltpu.make_async_copy(x_refs[t], out_refs[t].at[_block_index(me)], local_sems.at[t]) for t in range(n)]
        first = [[copy(t, 0, me, sibling, own=True)] + [copy(t, 1 + j, me, (*chip, c_), own=True) for j, chip in enumerate(chips)]
                 for t in range(n)]
        passed = [[copy(t, 4 + j, (*chip, c_), sibling) for j, chip in enumerate(chips)] for t in range(n)]
        for t in range(n):
            mine[t].start()
            for cp in first[t]:
                cp.start()
        for t in range(n):
            for j, chip in enumerate(chips):
                copy(t, 1 + j, (*chip, c_), me).wait_recv()
                passed[t][j].start()
        for t in range(n):
            copy(t, 0, sibling, me).wait_recv()
            for j, chip in enumerate(chips):
                copy(t, 4 + j, (*chip, 1 - c_), me).wait_recv()
            for cp in first[t] + passed[t]:
                cp.wait_send()
            mine[t].wait()

    return _pcall(
        body, name=name, in_specs=[_HBM] * n, out_specs=[_HBM] * n,
        out_shape=[jax.ShapeDtypeStruct((N_DEV,) + x.shape, x.dtype) for x in xs],
        scratch_shapes=[pltpu.SemaphoreType.DMA((7 * n,)), pltpu.SemaphoreType.DMA((7 * n,)), pltpu.SemaphoreType.DMA((n,))],
    )(*xs)


def _exchange_blocks(p, name):
    _, rows, cols = p.shape

    def body(p_ref, out_ref, send_sems, recv_sems, local_sem):
        me = _mesh_pos()
        mine = pltpu.make_async_copy(p_ref.at[_block_index(me)], out_ref.at[_block_index(me)], local_sem)
        mine.start()
        sends = []
        for k, rel in enumerate(_RELATIONS):
            peer = _peer(me, rel)
            sends.append(pltpu.make_async_remote_copy(
                src_ref=p_ref.at[_block_index(peer)], dst_ref=out_ref.at[_block_index(me)],
                send_sem=send_sems.at[k], recv_sem=recv_sems.at[k], device_id=peer, device_id_type=pl.DeviceIdType.MESH))
        for cp in sends:
            cp.start()
        for k, rel in enumerate(_RELATIONS):
            peer = _peer(me, rel)
            pltpu.make_async_remote_copy(
                src_ref=p_ref.at[_block_index(peer)], dst_ref=out_ref.at[_block_index(peer)],
                send_sem=send_sems.at[k], recv_sem=recv_sems.at[k], device_id=peer, device_id_type=pl.DeviceIdType.MESH).wait_recv()
        for cp in sends:
            cp.wait_send()
        mine.wait()

    return _pcall(
        body, name=name, in_specs=[_HBM], out_specs=_HBM,
        out_shape=jax.ShapeDtypeStruct(p.shape, p.dtype),
        scratch_shapes=[pltpu.SemaphoreType.DMA((7,)), pltpu.SemaphoreType.DMA((7,)), pltpu.SemaphoreType.DMA(())],
    )(p)


_SEM = pl.BlockSpec(memory_space=pltpu.SEMAPHORE)
_EFFECT = pltpu.SideEffectType.DATAFLOW_SIDE_EFFECTING


def _push_start(src, after, name):
    per_peer = src.ndim == 3
    rows, cols = src.shape[-2:]
    land_shape = (N_DEV, rows, cols)

    def body(src_ref, land_ref, after_ref, send_sem, recv_sem, src_thru, land_thru, token):
        me = _mesh_pos()
        for rel in _RELATIONS:
            peer = _peer(me, rel)
            pltpu.make_async_remote_copy(
                src_ref=src_ref.at[_block_index(peer)] if per_peer else src_ref, dst_ref=land_ref.at[_block_index(me)],
                send_sem=send_sem, recv_sem=recv_sem, device_id=peer, device_id_type=pl.DeviceIdType.MESH).start()
        token[...] = jnp.zeros_like(token)

    send_sem, recv_sem, src_thru, land_thru, token = _pcall(
        body, name=name,
        out_shape=(pltpu.SemaphoreType.DMA(()), pltpu.SemaphoreType.DMA(()), pltpu.HBM(src.shape, src.dtype),
                   pltpu.HBM(land_shape, src.dtype), jax.ShapeDtypeStruct((8, 128), F32)),
        in_specs=(_HBM, _HBM, pl.BlockSpec(memory_space=pl.ANY)), out_specs=(_SEM, _SEM, _HBM, _HBM, pl.BlockSpec(memory_space=pltpu.VMEM)),
        input_output_aliases={0: 2, 1: 3}, compiler_params=pltpu.CompilerParams(has_side_effects=_EFFECT),
    )(pltpu.with_memory_space_constraint(src, pltpu.HBM), pltpu.with_memory_space_constraint(lax.empty(land_shape, src.dtype), pltpu.HBM), after)
    return (send_sem, recv_sem, src_thru, land_thru), token


def _push_wait(handle, after, own, me, name):
    send_sem, recv_sem, src_thru, land_thru = handle

    def body(src_ref, land_ref, send_sem, recv_sem, after_ref, src_dead, land_out):
        sent = land_ref.at[pl.ds(0, N_DEV - 1)]
        all_seven = pltpu.make_async_remote_copy(src_ref=sent, dst_ref=sent, send_sem=send_sem, recv_sem=recv_sem,
                                                 device_id=_mesh_pos(), device_id_type=pl.DeviceIdType.MESH)
        all_seven.wait_send()
        all_seven.wait_recv()

    land = _pcall(
        body, name=name,
        out_shape=(pltpu.HBM(src_thru.shape, src_thru.dtype), pltpu.HBM(land_thru.shape, land_thru.dtype)),
        in_specs=(_HBM, _HBM, _SEM, _SEM, pl.BlockSpec(memory_space=pl.ANY)), out_specs=(_HBM, _HBM),
        input_output_aliases={0: 0, 1: 1}, compiler_params=pltpu.CompilerParams(has_side_effects=_EFFECT),
    )(src_thru, land_thru, send_sem, recv_sem, after)[1]
    return lax.dynamic_update_slice(land, own[None], (me, 0, 0))


def _sum_blocks(p, name):
    n, rows, cols = p.shape
    tr = _pick(rows, (256, 128, 64, PACK_ROWS, 16, 8))

    def body(p_ref, o_ref):
        acc = p_ref[0].astype(F32)
        for s in range(1, n):
            acc = acc + p_ref[s].astype(F32)
        o_ref[...] = acc

    return _pcall(
        body, name=name, grid=(rows // tr,),
        in_specs=[pl.BlockSpec((n, tr, cols), lambda i: (0, i, 0))], out_specs=pl.BlockSpec((tr, cols), lambda i: (i, 0)),
        out_shape=jax.ShapeDtypeStruct((rows, cols), F32), compiler_params=_params(("parallel",)),
    )(p)


def _sum_rows(t, name):
    def body(t_ref, o_ref):
        o_ref[...] = jnp.sum(t_ref[...], axis=0, keepdims=True)

    return _pcall(body, name=name, out_shape=jax.ShapeDtypeStruct((1, t.shape[1]), F32))(t)


class _Packing:
    def __init__(self, sizes, align, total_align=None):
        self.offsets, self.sizes, self.align = {}, dict(sizes), align
        row = 0
        for name, size in sizes:
            self.offsets[name] = row
            row += -(-size // (align * PACK_COLS)) * align
        total_align = total_align or align
        self.rows = -(-row // total_align) * total_align
        self.tail = self.rows - row

    def pack(self, pieces):
        return self.pack_blocks({n: pieces[n].reshape(1, -1) for n in self.sizes})[0]

    def pack_blocks(self, pieces):
        out = []
        for n, size in self.sizes.items():
            padded = -(-size // (self.align * PACK_COLS)) * self.align * PACK_COLS
            out.append(jnp.pad(pieces[n], ((0, 0), (0, padded - size))).reshape(pieces[n].shape[0], -1, PACK_COLS))
        if self.tail:
            out.append(jnp.zeros((out[0].shape[0], self.tail, PACK_COLS), out[0].dtype))
        return jnp.concatenate(out, axis=1)

    def piece(self, packed, name, lead=()):
        start, size = self.offsets[name], self.sizes[name]
        nrow = -(-size // (self.align * PACK_COLS)) * self.align
        sl = packed[..., start:start + nrow, :]
        return sl.reshape(lead + (nrow * PACK_COLS,))[..., :size]


_PIECES = (("ev_w_in", "ev_w_in", 0, 1, "a"), ("w_out0", "w_out", 0, 0, "a"),
           ("mlp_w1_0", "mlp_w1", 0, 1, "b"), ("mlp_w2_0", "mlp_w2", 0, 0, "b"),
           ("od_w_in", "od_w_in", 0, 1, "c"), ("od_w_uq", "od_w_uq", 0, 1, "c"), ("od_w_ukv", "od_w_ukv", 0, 1, "c"),
           ("w_out1", "w_out", 1, 0, "c"), ("mlp_w1_1", "mlp_w1", 1, 1, "c"), ("mlp_w2_1", "mlp_w2", 1, 0, "c"))
_SMALL_SHARDED = (("od_q_norm_g", 1), ("od_conv_w", 2), ("od_conv_b", 1), ("od_ln_g", 1), ("od_ln_b", 1))
_REPLICATED = ("c_ctx", "norm1_g", "norm2_g", "ev_q_norm_g", "ev_k_norm_g", "ev_sgu_norm_g", "ev_sgu_w", "ev_sgu_b",
               "od_kv_norm_g", "final_g")


def _unshard(blocks, axis):
    moved = jnp.moveaxis(blocks, 0, axis)
    shape = moved.shape
    return moved.reshape(shape[:axis] + (shape[axis] * shape[axis + 1],) + shape[axis + 2:])


def _shard_blocks(full, axis):
    shape = full.shape
    split = full.reshape(shape[:axis] + (N_DEV, shape[axis] // N_DEV) + shape[axis + 1:])
    return jnp.moveaxis(split, axis, 0)


def _group_mean_matrix(width, group):
    idx = jnp.arange(width) // group
    return (idx[:, None] == idx[None, :]).astype(F32) / group


def _swap_matrix(width):
    idx = jnp.arange(width)
    return ((idx[:, None] ^ 1) == idx[None, :]).astype(F32)


def _angles(length, d_rot):
    rows = length // GRID_W
    row = jnp.broadcast_to(jnp.arange(rows)[:, None], (rows, GRID_W)).reshape(-1).astype(F32)
    col = jnp.broadcast_to(jnp.arange(GRID_W)[None, :], (rows, GRID_W)).reshape(-1).astype(F32)
    d_axis = d_rot // 2
    inv = ROPE_THETA ** (-jnp.arange(0, d_axis, 2, dtype=F32) / d_axis)
    return jnp.concatenate([row[:, None] * inv, col[:, None] * inv], axis=-1)


def _rope_tables(length, d_rot, head_dim, heads, ctx_len):
    ang = _angles(length, d_rot)
    cos = jnp.repeat(jnp.cos(ang), 2, axis=1)
    sin = jnp.repeat(jnp.sin(ang), 2, axis=1) * jnp.tile(jnp.array([-1.0, 1.0], F32), d_rot // 2)
    keep = head_dim - d_rot
    cos = jnp.concatenate([jnp.ones((length, keep), F32), cos], axis=1)
    sin = jnp.concatenate([jnp.zeros((length, keep), F32), sin], axis=1)
    cos, sin = jnp.tile(cos, (1, heads)), jnp.tile(sin, (1, heads))
    cos = jnp.concatenate([jnp.ones((ctx_len, cos.shape[1]), F32), cos], axis=0)
    sin = jnp.concatenate([jnp.zeros((ctx_len, sin.shape[1]), F32), sin], axis=0)
    return cos, sin


def _to_heads(t, heads):
    b, l, w = t.shape
    return t.reshape(b, l, heads, w // heads).transpose(0, 2, 1, 3)


def _from_heads(t):
    b, h, l, d = t.shape
    return t.transpose(0, 2, 1, 3).reshape(b, l, h * d)


def _segment_params(mod, bsz):
    parts = jnp.split(mod, N_MOD, axis=-1)
    out = []
    for part in parts:
        lat = part[:bsz]
        ctx = jnp.broadcast_to(part[bsz:bsz + 1], lat.shape)
        out.append(jnp.stack([ctx, lat], axis=1)[:, :, None, :])
    return out


def _flat(t):
    return t.reshape(-1, t.shape[-1])


def _sequence_rowwise(ctx_len):
    tl = math.gcd(256, ctx_len)

    def make(name, f, out_specs, rows_per_block=tl, ctx_blocks=ctx_len // tl):
        return _rowwise(name, f, out_specs, rows_per_block, ctx_blocks)

    return make


def _mixer0(xall, modrows0, w, bsz, length, ctx_len):
    d = xall.shape[-1]
    total = ctx_len + length
    rowwise, flat = _sequence_rowwise(ctx_len), _flat
    sh1, sc1, g1, sh2, sc2, _ = _segment_params(modrows0, bsz)
    (h,) = rowwise("mod0", _f_modulate, [(d, BF16)])((xall,), (), (sh1, sc1), (w["norm1_g0"],), ())
    proj = _linear("ev_in")(flat(h), w["ev_w_in"]).reshape(bsz, total, -1)
    ev_q, ev_kv = A_Q_HEADS * A_HEAD_DIM, A_KV_HEADS * A_HEAD_DIM
    half = B_GROUPS * B_GROUP_DIM
    qp, kp, vp, zu, zv = jnp.split(proj, [ev_q, ev_q + ev_kv, ev_q + 2 * ev_kv, ev_q + 2 * ev_kv + half], axis=-1)
    cos_q, sin_q = _rope_tables(length, A_HEAD_DIM, A_HEAD_DIM, A_Q_HEADS, ctx_len)
    cos_k, sin_k = cos_q[:, :ev_kv], sin_q[:, :ev_kv]
    (q,) = rowwise("ev_q", _f_headnorm_rope, [(ev_q, BF16)])(
        (qp,), (cos_q, sin_q), (), (jnp.tile(w["ev_q_norm_g"][0], A_Q_HEADS)[None],), (_group_mean_matrix(ev_q, A_HEAD_DIM),))
    (k,) = rowwise("ev_k", _f_headnorm_rope, [(ev_kv, BF16)])(
        (kp,), (cos_k, sin_k), (), (jnp.tile(w["ev_k_norm_g"][0], A_KV_HEADS)[None],), (_group_mean_matrix(ev_kv, A_HEAD_DIM),))
    o_att = _attention("gqa", ctx_len)(_to_heads(q, A_Q_HEADS), _to_heads(k, A_KV_HEADS), _to_heads(vp.astype(BF16), A_KV_HEADS))
    u, vn = rowwise("sgu_pre", _f_sgu_pre, [(half, F32), (half, BF16)])(
        (zu, zv), (), (), (w["ev_sgu_norm_g"][0].reshape(1, half),), (_group_mean_matrix(half, B_GROUP_DIM),))
    bias = jnp.repeat(w["ev_sgu_b"][0].T, B_GROUP_DIM, axis=1)
    (o_sgu,) = rowwise("sgu_mix", _f_sgu_mix, [(half, BF16)], rows_per_block=B_CHUNK, ctx_blocks=0)(
        (u, vn), (), (), tuple(w["ev_sgu_w"][0][g] for g in range(B_GROUPS)) + (bias,), ())
    o = jnp.concatenate([_from_heads(o_att), o_sgu], axis=-1)
    y = _linear("out0")(flat(o), w["w_out0"]).reshape(bsz, total, d)
    x1, h = rowwise("res_mod0a", _f_res_modulate, [(d, F32), (d, BF16)])((xall, y), (), (g1, sh2, sc2), (w["norm2_g0"],), ())
    return x1, h


def _mlp0(x1, h, modrows0, modrows1, w, bsz, length, ctx_len):
    d = x1.shape[-1]
    total = ctx_len + length
    g2 = _segment_params(modrows0, bsz)[5]
    sh1, sc1 = _segment_params(modrows1, bsz)[:2]
    y = _mlp("mlp0")(_flat(h), w["mlp_w1_0"], w["mlp_w2_0"]).reshape(bsz, total, d)
    return _sequence_rowwise(ctx_len)("res_mod0b", _f_res_modulate, [(d, F32), (d, BF16)])((x1, y), (), (g2, sh1, sc1), (w["norm1_g1"],), ())


def _layer1(x2, h, modrows1, w, bsz, length, ctx_len):
    d = x2.shape[-1]
    total = ctx_len + length
    half = B_GROUPS * B_GROUP_DIM
    rowwise, flat = _sequence_rowwise(ctx_len), _flat
    _, _, g1n, sh2n, sc2n, g2n = _segment_params(modrows1, bsz)
    proj = _linear("od_in")(flat(h), w["od_w_in"]).reshape(bsz, total, -1)
    cq, ckv, kr, za, zg = jnp.split(proj, [C_Q_RANK, C_Q_RANK + C_KV_RANK, C_Q_RANK + C_KV_RANK + C_ROPE,
                                          C_Q_RANK + C_KV_RANK + C_ROPE + half], axis=-1)
    lat = slice(ctx_len, total)
    lat_tl = math.gcd(256, length)
    (cqn,) = _rowwise("od_qn", _f_rms, [(C_Q_RANK, BF16)], lat_tl)((cq[:, lat],), (), (), (w["od_q_norm_g"],), ())
    c_qk = C_NOPE + C_ROPE
    qf = _linear("od_uq")(flat(cqn), w["od_w_uq"]).reshape(bsz, length, C_HEADS * c_qk)
    cos_q, sin_q = _rope_tables(length, C_ROPE, c_qk, C_HEADS, 0)
    (q,) = _rowwise("od_qrope", _f_rope, [(C_HEADS * c_qk, BF16)], lat_tl)((qf,), (cos_q, sin_q), (), (), ())
    (ckvn,) = rowwise("od_kvn", _f_rms, [(C_KV_RANK, BF16)])((ckv,), (), (), (w["od_kv_norm_g"],), ())
    kv = _linear("od_ukv")(flat(ckvn), w["od_w_ukv"]).reshape(bsz, total, C_HEADS, C_NOPE + C_V)
    cos_r, sin_r = _rope_tables(length, C_ROPE, C_ROPE, 1, ctx_len)
    (krr,) = rowwise("od_krope", _f_rope_narrow, [(C_ROPE, BF16)])((kr,), (cos_r, sin_r), (), (), (_swap_matrix(C_ROPE),))
    kn = kv[..., :C_NOPE].astype(BF16).transpose(0, 2, 1, 3)
    vv = kv[..., C_NOPE:].astype(BF16).transpose(0, 2, 1, 3)
    kfull = jnp.concatenate([kn, jnp.broadcast_to(krr[:, None], (bsz, C_HEADS, total, C_ROPE))], axis=-1)
    o_att = _attention("mla", 0)(_to_heads(q, C_HEADS), kfull, vv)
    (glu,) = _rowwise("glu", _f_glu, [(half, F32)], lat_tl)((za[:, lat], zg[:, lat]), (), (), (), ())
    taps = jnp.concatenate([w["od_conv_w"][0], w["od_conv_b"]], axis=0)
    conv = _dwconv(glu, taps)
    (o_conv,) = _rowwise("ln_silu", _f_ln_silu, [(half, BF16)], lat_tl)((conv,), (), (), (w["od_ln_g"], w["od_ln_b"]), ())
    o = jnp.concatenate([_from_heads(o_att), o_conv], axis=-1)
    y = _linear("out1")(flat(o), w["w_out1"]).reshape(bsz, length, d)
    lat_param = lambda p: p[:, 1:]
    x3, h = _rowwise("res_mod1a", _f_res_modulate, [(d, F32), (d, BF16)], lat_tl)(
        (x2[:, lat], y), (), (lat_param(g1n), lat_param(sh2n), lat_param(sc2n)), (w["norm2_g1"],), ())
    y = _mlp("mlp1")(flat(h), w["mlp_w1_1"], w["mlp_w2_1"]).reshape(bsz, length, d)
    (x4,) = _rowwise("res1b", _f_res, [(d, F32)], lat_tl)((x3, y), (), (lat_param(g2n),), (), ())
    return x4


def kernel(x, c, ctx, c_ctx, ada_w, ada_b, norm1_g, norm2_g, w_out, mlp_w1, mlp_w2, ev_w_in, ev_q_norm_g, ev_k_norm_g, ev_sgu_norm_g, ev_sgu_w, ev_sgu_b, od_w_in, od_q_norm_g, od_kv_norm_g, od_w_uq, od_w_ukv, od_conv_w, od_conv_b, od_ln_g, od_ln_b, final_g, loss_target, m_c_ctx, m_ada_w, m_ada_b, m_norm1_g, m_norm2_g, m_w_out, m_mlp_w1, m_mlp_w2, m_ev_w_in, m_ev_q_norm_g, m_ev_k_norm_g, m_ev_sgu_norm_g, m_ev_sgu_w, m_ev_sgu_b, m_od_w_in, m_od_q_norm_g, m_od_kv_norm_g, m_od_w_uq, m_od_w_ukv, m_od_conv_w, m_od_conv_b, m_od_ln_g, m_od_ln_b, m_final_g, v_c_ctx, v_ada_w, v_ada_b, v_norm1_g, v_norm2_g, v_w_out, v_mlp_w1, v_mlp_w2, v_ev_w_in, v_ev_q_norm_g, v_ev_k_norm_g, v_ev_sgu_norm_g, v_ev_sgu_w, v_ev_sgu_b, v_od_w_in, v_od_q_norm_g, v_od_kv_norm_g, v_od_w_uq, v_od_w_ukv, v_od_conv_w, v_od_conv_b, v_od_ln_g, v_od_ln_b, v_final_g):
    names = ["c_ctx", "ada_w", "ada_b", "norm1_g", "norm2_g", "w_out", "mlp_w1", "mlp_w2", "ev_w_in", "ev_q_norm_g", "ev_k_norm_g",
             "ev_sgu_norm_g", "ev_sgu_w", "ev_sgu_b", "od_w_in", "od_q_norm_g", "od_kv_norm_g", "od_w_uq", "od_w_ukv", "od_conv_w",
             "od_conv_b", "od_ln_g", "od_ln_b", "final_g"]
    local = dict(zip(names, [c_ctx, ada_w, ada_b, norm1_g, norm2_g, w_out, mlp_w1, mlp_w2, ev_w_in, ev_q_norm_g, ev_k_norm_g, ev_sgu_norm_g, ev_sgu_w, ev_sgu_b, od_w_in, od_q_norm_g, od_kv_norm_g, od_w_uq, od_w_ukv, od_conv_w, od_conv_b, od_ln_g, od_ln_b, final_g]))
    mom1 = dict(zip(names, [m_c_ctx, m_ada_w, m_ada_b, m_norm1_g, m_norm2_g, m_w_out, m_mlp_w1, m_mlp_w2, m_ev_w_in, m_ev_q_norm_g, m_ev_k_norm_g, m_ev_sgu_norm_g, m_ev_sgu_w, m_ev_sgu_b, m_od_w_in, m_od_q_norm_g, m_od_kv_norm_g, m_od_w_uq, m_od_w_ukv, m_od_conv_w, m_od_conv_b, m_od_ln_g, m_od_ln_b, m_final_g]))
    mom2 = dict(zip(names, [v_c_ctx, v_ada_w, v_ada_b, v_norm1_g, v_norm2_g, v_w_out, v_mlp_w1, v_mlp_w2, v_ev_w_in, v_ev_q_norm_g, v_ev_k_norm_g, v_ev_sgu_norm_g, v_ev_sgu_w, v_ev_sgu_b, v_od_w_in, v_od_q_norm_g, v_od_kv_norm_g, v_od_w_uq, v_od_w_ukv, v_od_conv_w, v_od_conv_b, v_od_ln_g, v_od_ln_b, v_final_g]))
    bsz, length, d = x.shape
    ctx_len = ctx.shape[1]
    me = _block_index(_mesh_pos())

    shard = {p: local[wn][layer] for p, wn, layer, _, _ in _PIECES}
    stages = {s: [(p, axis) for p, _, _, axis, st in _PIECES if st == s] for s in "abc"}
    packs = {s: _Packing([(p, shard[p].size) for p, _ in stages[s]], ROW_ALIGN, PACK_ROWS) for s in "abc"}
    mine = {s: packs[s].pack({p: shard[p].astype(BF16) for p, _ in stages[s]}) for s in "abc"}
    tiny_pack = _Packing([(n, local[n].size) for n, _ in _SMALL_SHARDED], 8)

    def unpack_weights(s, gathered):
        return {p: _unshard(packs[s].piece(gathered, p, (N_DEV,)).reshape((N_DEV,) + shard[p].shape), axis) for p, axis in stages[s]}

    def pack_grads(s, g):
        return packs[s].pack_blocks({p: _shard_blocks(g[p], axis).reshape(N_DEV, -1) for p, axis in stages[s]})

    cond_local = jnp.concatenate([c, c_ctx[None], jnp.zeros((COND_ROWS - bsz - 1, d), F32)], axis=0)
    cond, gathered_a, gathered_tiny = _all_gather(
        [cond_local, mine["a"], tiny_pack.pack({n: local[n] for n, _ in _SMALL_SHARDED})], "gather_inputs")
    cond = cond.reshape(N_DEV * COND_ROWS, d)
    silu_op = _rowwise("silu", _f_silu, [(d, F32)], N_DEV * COND_ROWS)
    silu_rows, silu_pullback = jax.vjp(lambda r: silu_op((r[None],), (), (), (), ())[0][0], cond)
    mod_cols = ada_w.shape[2]
    mod_part = jnp.concatenate([_mm(silu_rows, ada_w[i], "nn", (F32,), f"ada{i}_fwd")[0] for i in range(2)], axis=0)
    (mod_all,) = _all_gather([mod_part], "gather_mod")
    mod_all = mod_all.reshape(N_DEV, 2, N_DEV * COND_ROWS, mod_cols)
    modrows = []
    for i in range(2):
        whole = mod_all[:, i].transpose(1, 0, 2).reshape(N_DEV * COND_ROWS, N_DEV * mod_cols) + ada_b[i]
        modrows.append(lax.dynamic_slice_in_dim(whole, me * COND_ROWS, COND_ROWS, axis=0)[:bsz + 1])

    weights_a = unpack_weights("a", gathered_a)
    gather_b, token_b = _push_start(mine["b"], mod_all, "gather_weights_b_start")
    gather_c, token_c = _push_start(mine["c"], token_b, "gather_weights_c_start")
    full = {n: local[n] for n in _REPLICATED}
    for n, axis in _SMALL_SHARDED:
        full[n] = _unshard(tiny_pack.piece(gathered_tiny, n, (N_DEV,)).reshape((N_DEV,) + local[n].shape), axis)

    xall = jnp.concatenate([ctx, x], axis=1)
    modrows0, modrows1 = modrows[0] + token_c[0, 0], modrows[1]
    w_a = dict(weights_a, norm1_g0=norm1_g[0][None], norm2_g0=norm2_g[0][None],
               **{n: full[n] for n in ("ev_q_norm_g", "ev_k_norm_g", "ev_sgu_norm_g", "ev_sgu_w", "ev_sgu_b")})
    (x1, h0), pull_a = jax.vjp(lambda x_, m0, w: _mixer0(x_, m0, w, bsz, length, ctx_len), xall, modrows0, w_a)
    w_b = dict(unpack_weights("b", _push_wait(gather_b, x1, mine["b"], me, "gather_weights_b_wait")), norm1_g1=norm1_g[1][None])
    (x2, h1), pull_b = jax.vjp(lambda x_, h_, m0, m1, w: _mlp0(x_, h_, m0, m1, w, bsz, length, ctx_len), x1, h0, modrows0, modrows1, w_b)
    w_c = dict(unpack_weights("c", _push_wait(gather_c, x2, mine["c"], me, "gather_weights_c_wait")), norm2_g1=norm2_g[1][None],
               **{n: full[n] for n in ("od_q_norm_g", "od_kv_norm_g", "od_conv_w", "od_conv_b", "od_ln_g", "od_ln_b")})
    x4, pull_c = jax.vjp(lambda x_, h_, m1, w: _layer1(x_, h_, m1, w, bsz, length, ctx_len), x2, h1, modrows1, w_c)
    loss_part, dx4, dfinal = _loss_head(x4, loss_target, final_g[None])
    loss = lax.psum(loss_part[0, 0], MESH_AXES)

    dx2, dh1, dmod1_c, g_c = pull_c(dx4)
    grads_c = pack_grads("c", g_c)
    exchange_c, token = _push_start(grads_c, dx2, "exchange_grads_c_start")
    dx1, dh0, dmod0_b, dmod1_b, g_b = pull_b((dx2, dh1 + token[0, 0].astype(dh1.dtype)))
    grads_b = pack_grads("b", g_b)
    exchange_b, token = _push_start(grads_b, dx1, "exchange_grads_b_start")
    dxall, dmod0_a, g_a = pull_a((dx1, dh0 + token[0, 0].astype(dh0.dtype)))
    grad_x = dxall[:, ctx_len:]
    dmodrows = [dmod0_a + dmod0_b, dmod1_b + dmod1_c]
    grads = {n: g[n] for g in (g_a, g_c) for n in g if n in full}
    grads["norm1_g"] = jnp.concatenate([g_a["norm1_g0"], g_b["norm1_g1"]], axis=0)
    grads["norm2_g"] = jnp.concatenate([g_a["norm2_g0"], g_c["norm2_g1"]], axis=0)
    grads["final_g"] = dfinal[0]

    dmod_local = jnp.concatenate([jnp.pad(dm, ((0, COND_ROWS - bsz - 1), (0, 0))) for dm in dmodrows], axis=0)
    (dmod_all,) = _all_gather([dmod_local], "gather_dmod")
    dmod_all = dmod_all.reshape(N_DEV, 2, COND_ROWS, N_DEV * mod_cols)
    reduced = {}
    grad_ada_w, grad_ada_b, dmod_mine = [], [], []
    for i in range(2):
        dmod = dmod_all[:, i].reshape(N_DEV * COND_ROWS, N_DEV * mod_cols)
        grad_ada_b.append(_sum_rows(dmod, f"ada{i}_db")[0])
        dmod_mine.append(lax.dynamic_slice_in_dim(dmod, me * mod_cols, mod_cols, axis=1))
        grad_ada_w.append(_mm(silu_rows, dmod_mine[i], "tn", (F32,), f"ada{i}_dw")[0])
    reduced["ada_w"], reduced["ada_b"] = jnp.stack(grad_ada_w), jnp.stack(grad_ada_b)
    dsilu = _mm(jnp.concatenate(dmod_mine, axis=1), jnp.concatenate([ada_w[0], ada_w[1]], axis=1), "nt", (F32,), "ada_dx")[0]
    (dcond,) = silu_pullback(dsilu)
    grads["c_ctx"] = _sum_rows(dcond.reshape(N_DEV, COND_ROWS, d)[:, bsz], "c_ctx_rows")[0]

    def own_block(packed):
        return lax.dynamic_index_in_dim(packed, me, 0, keepdims=False)

    received = {"a": _exchange_blocks(pack_grads("a", g_a), "exchange_grads_a"),
                "b": _push_wait(exchange_b, dmod_all, own_block(grads_b), me, "exchange_grads_b_wait"),
                "c": _push_wait(exchange_c, dmod_all, own_block(grads_c), me, "exchange_grads_c_wait")}
    piece_grad = {}
    for s in "abc":
        summed = _sum_blocks(received[s], "sum_grads_" + s)
        for p, _ in stages[s]:
            piece_grad[p] = packs[s].piece(summed, p).reshape(shard[p].shape)
    for n in ("w_out", "mlp_w1", "mlp_w2"):
        reduced[n] = jnp.stack([piece_grad[p] for p, wn, _, _, _ in _PIECES if wn == n])
    for n in ("ev_w_in", "od_w_in", "od_w_uq", "od_w_ukv"):
        reduced[n] = piece_grad[n][None]

    small_names = list(_REPLICATED) + [n for n, _ in _SMALL_SHARDED]
    small_pack = _Packing([(n, full[n].size) for n in small_names], 8)
    (small_all,) = _all_gather([small_pack.pack({n: grads[n].astype(F32) for n in small_names})], "gather_small_grads")
    small_sum = _sum_blocks(small_all, "sum_small_grads")
    for n in _REPLICATED:
        reduced[n] = small_pack.piece(small_sum, n).reshape(local[n].shape)
    for n, axis in _SMALL_SHARDED:
        whole = small_pack.piece(small_sum, n).reshape(full[n].shape)
        reduced[n] = lax.dynamic_slice_in_dim(whole, me * local[n].shape[axis], local[n].shape[axis], axis=axis)

    delta, new_m, new_v = {}, {}, {}
    for n in names:
        delta[n], new_m[n], new_v[n] = _adamw(local[n], reduced[n], mom1[n], mom2[n], "adamw_" + n)
    return (loss, grad_x, *[reduced[n] for n in names], *[delta[n] for n in names], *[new_m[n] for n in names], *[new_v[n] for n in names])
```

```python
import functools
import math

import jax
import jax.numpy as jnp
from jax import lax
from jax.experimental import pallas as pl
from jax.experimental.pallas import tpu as pltpu

F32, BF16 = jnp.float32, jnp.bfloat16

EPS = 1e-6
GRID_W = 64
ROPE_THETA = 10000.0
A_HEAD_DIM, A_Q_HEADS, A_KV_HEADS = 64, 8, 2
B_GROUPS, B_GROUP_DIM, B_CHUNK = 8, 64, 128
C_HEADS, C_NOPE, C_ROPE, C_V, C_Q_RANK, C_KV_RANK = 8, 64, 32, 64, 256, 128
D_CONV = 31
CONV_PAD = D_CONV // 2
N_MOD = 6
N_DEV = 8
MESH_AXES = ("x", "y", "c")

ADAM_LR, ADAM_B1, ADAM_B2, ADAM_EPS, ADAM_WD, ADAM_STEP = 0.001, 0.9, 0.999, 1e-08, 0.01, 10

VMEM_LIMIT = 56 * 1024 * 1024
PACK_COLS = 1024
ROW_ALIGN = 16
PACK_ROWS = 32
COND_ROWS = 8


def _pcall(body, **kw):
    return pl.pallas_call(body, **kw)


def _params(sem=None):
    return pltpu.CompilerParams(dimension_semantics=sem, vmem_limit_bytes=VMEM_LIMIT)


def _pick(n, cands):
    for c in cands:
        if n % c == 0:
            return c
    return n


def _mm(a, b, mode, out_dtypes, name, epi=None, extras=()):
    if mode == "tn":
        kk, m = a.shape
    else:
        m, kk = a.shape
    n = b.shape[0] if mode == "nt" else b.shape[1]
    tm = _pick(m, (1152, 1024, 896, 768, 512, 256, 128))
    tn = _pick(n, (1024, 896, 768, 512, 256, 128))
    tk = kk if kk <= 1024 else _pick(kk, (1024, 896, 768, 512, 256, 128))
    nk = kk // tk
    ne, no = len(extras), len(out_dtypes)
    a_spec = pl.BlockSpec((tk, tm), lambda i, j, k: (k, i)) if mode == "tn" else pl.BlockSpec((tm, tk), lambda i, j, k: (i, k))
    b_spec = pl.BlockSpec((tn, tk), lambda i, j, k: (j, k)) if mode == "nt" else pl.BlockSpec((tk, tn), lambda i, j, k: (k, j))
    t_spec = pl.BlockSpec((tm, tn), lambda i, j, k: (i, j))
    dn = {"nn": ((1,), (0,)), "nt": ((1,), (1,)), "tn": ((0,), (0,))}[mode]

    def body(a_ref, b_ref, *rest):
        extra_refs, out_refs = rest[:ne], rest[ne:ne + no]

        def finish(acc):
            outs = (acc,) if epi is None else epi(acc, *[r[...] for r in extra_refs])
            for r, o in zip(out_refs, outs):
                r[...] = o.astype(r.dtype)

        part = lax.dot_general(a_ref[...].astype(BF16), b_ref[...].astype(BF16), (dn, ((), ())), preferred_element_type=F32)
        if nk == 1:
            finish(part)
        else:
            acc_ref = rest[-1]
            k = pl.program_id(2)

            @pl.when(k == 0)
            def _():
                acc_ref[...] = part

            @pl.when(k > 0)
            def _():
                acc_ref[...] += part

            @pl.when(k == nk - 1)
            def _():
                finish(acc_ref[...])

    outs = _pcall(
        body, name=name, grid=(m // tm, n // tn, nk),
        in_specs=[a_spec, b_spec] + [t_spec] * ne,
        out_specs=[t_spec] * no,
        out_shape=[jax.ShapeDtypeStruct((m, n), d) for d in out_dtypes],
        scratch_shapes=[pltpu.VMEM((tm, tn), F32)] if nk > 1 else [],
        compiler_params=_params(("parallel", "parallel", "arbitrary")),
    )(a, b, *extras)
    return outs


def _linear(name, transposed=False, out_dtype=F32):
    fwd_mode, dx_mode = ("nt", "nn") if transposed else ("nn", "nt")

    @jax.custom_vjp
    def op(x, w):
        return _mm(x, w, fwd_mode, (out_dtype,), name + "_fwd")[0]

    def fwd(x, w):
        return op(x, w), (x, w)

    def bwd(res, dy):
        x, w = res
        dx = _mm(dy, w, dx_mode, (x.dtype,), name + "_dx")[0]
        dw = _mm(dy, x, "tn", (w.dtype,), name + "_dw")[0] if transposed else _mm(x, dy, "tn", (w.dtype,), name + "_dw")[0]
        return dx, dw

    op.defvjp(fwd, bwd)
    return op


def _relu2_epi(acc):
    return jnp.square(jnp.maximum(acc, 0.0)), acc


def _relu2_bwd_epi(acc, a):
    return (acc * (2.0 * jnp.maximum(a.astype(F32), 0.0)),)


def _mlp(name):
    @jax.custom_vjp
    def op(h, w1t, w2):
        s, _ = _mm(h, w1t, "nt", (BF16, BF16), name + "_up", epi=_relu2_epi)
        return _mm(s, w2, "nn", (F32,), name + "_down")[0]

    def fwd(h, w1t, w2):
        s, a = _mm(h, w1t, "nt", (BF16, BF16), name + "_up", epi=_relu2_epi)
        return _mm(s, w2, "nn", (F32,), name + "_down")[0], (h, w1t, w2, s, a)

    def bwd(res, dy):
        h, w1t, w2, s, a = res
        da = _mm(dy, w2, "nt", (BF16,), name + "_ds", epi=_relu2_bwd_epi, extras=(a,))[0]
        dw2 = _mm(s, dy, "tn", (w2.dtype,), name + "_dw2")[0]
        dw1t = _mm(da, h, "tn", (w1t.dtype,), name + "_dw1")[0]
        dh = _mm(da, w1t, "nn", (h.dtype,), name + "_dh")[0]
        return dh, dw1t, dw2

    op.defvjp(fwd, bwd)
    return op


def _two_pass_dot(x, m):
    hi = x.astype(BF16)
    lo = (x - hi.astype(F32)).astype(BF16)
    mb = m.astype(BF16)
    return jnp.dot(hi, mb, preferred_element_type=F32) + jnp.dot(lo, mb, preferred_element_type=F32)


@jax.custom_vjp
def _sym_dot(x, m):
    return _two_pass_dot(x, m)


def _sym_dot_fwd(x, m):
    return _two_pass_dot(x, m), m


def _sym_dot_bwd(m, g):
    return _two_pass_dot(g, m), jnp.zeros_like(m)


_sym_dot.defvjp(_sym_dot_fwd, _sym_dot_bwd)


def _neighbour(x):
    lane = lax.broadcasted_iota(jnp.int32, x.shape, 1)
    return jnp.where(lane % 2 == 0, pltpu.roll(x, x.shape[1] - 1, 1), pltpu.roll(x, 1, 1))


@jax.custom_vjp
def _swap_pairs(x):
    return _neighbour(x)


_swap_pairs.defvjp(lambda x: (_neighbour(x), None), lambda _, g: (_neighbour(g),))


@jax.custom_vjp
def _bdot(a, b):
    return jnp.dot(a.astype(BF16), b.astype(BF16), preferred_element_type=F32)


def _bdot_fwd(a, b):
    return _bdot(a, b), (a, b)


def _bdot_bwd(res, g):
    a, b = res
    gb = g.astype(BF16)
    da = lax.dot_general(gb, b.astype(BF16), (((1,), (1,)), ((), ())), preferred_element_type=F32)
    db = lax.dot_general(a.astype(BF16), gb, (((0,), (0,)), ((), ())), preferred_element_type=F32)
    return da, db


_bdot.defvjp(_bdot_fwd, _bdot_bwd)


def _rowwise(name, f, out_specs, tl, ctx_blocks=0):
    def seg(l, s):
        return jnp.where(l >= ctx_blocks, s - 1, 0) if s > 1 else 0

    def specs(rows, tabs, pers, glbs, consts):
        row_specs = [pl.BlockSpec((1, tl, r.shape[2]), lambda b, l: (b, l, 0)) for r in rows]
        tab_specs = [pl.BlockSpec((tl, t.shape[1]), lambda b, l: (l, 0)) for t in tabs]
        per_specs = [pl.BlockSpec((1, 1, 1, p.shape[3]), functools.partial(lambda b, l, s: (b, seg(l, s), 0, 0), s=p.shape[1])) for p in pers]
        glb_specs = [pl.BlockSpec(g.shape, functools.partial(lambda b, l, nd: (0,) * nd, nd=g.ndim)) for g in glbs]
        const_specs = [pl.BlockSpec(c.shape, functools.partial(lambda b, l, nd: (0,) * nd, nd=c.ndim)) for c in consts]
        return row_specs, tab_specs, per_specs, glb_specs, const_specs

    def load(refs_rows, refs_tabs, refs_pers, refs_glbs, refs_consts):
        return (tuple(r[0].astype(F32) for r in refs_rows), tuple(t[...] for t in refs_tabs),
                tuple(p[0, 0].astype(F32) for p in refs_pers), tuple(g[...].astype(F32) for g in refs_glbs),
                tuple(c[...] for c in refs_consts))

    def call_fwd(rows, tabs, pers, glbs, consts):
        bsz, length = rows[0].shape[:2]
        nr, nt, npp, ng, nc = len(rows), len(tabs), len(pers), len(glbs), len(consts)
        rs, ts, ps, gs, cs = specs(rows, tabs, pers, glbs, consts)

        def body(*refs):
            ins, outs = refs[:nr + nt + npp + ng + nc], refs[nr + nt + npp + ng + nc:]
            r, t, p, g, c = load(ins[:nr], ins[nr:nr + nt], ins[nr + nt:nr + nt + npp], ins[nr + nt + npp:nr + nt + npp + ng], ins[nr + nt + npp + ng:])
            for o_ref, o in zip(outs, f(r, t, p, g, c)):
                o_ref[0] = o.astype(o_ref.dtype)

        return _pcall(
            body, name=name + "_fwd", grid=(bsz, length // tl),
            in_specs=rs + ts + ps + gs + cs,
            out_specs=[pl.BlockSpec((1, tl, w), lambda b, l: (b, l, 0)) for w, _ in out_specs],
            out_shape=[jax.ShapeDtypeStruct((bsz, length, w), d) for w, d in out_specs],
            compiler_params=_params(("parallel", "parallel")),
        )(*rows, *tabs, *pers, *glbs, *consts)

    def call_bwd(rows, tabs, pers, glbs, consts, cts):
        bsz, length = rows[0].shape[:2]
        nr, nt, npp, ng, nc, no = len(rows), len(tabs), len(pers), len(glbs), len(consts), len(cts)
        rs, ts, ps, gs, cs = specs(rows, tabs, pers, glbs, consts)
        n_in = nr + nt + npp + ng + nc

        def body(*refs):
            ins, ct_refs, outs = refs[:n_in], refs[n_in:n_in + no], refs[n_in + no:]
            r, t, p, g, c = load(ins[:nr], ins[nr:nr + nt], ins[nr + nt:nr + nt + npp], ins[nr + nt + npp:nr + nt + npp + ng], ins[nr + nt + npp + ng:])
            _, vjp = jax.vjp(lambda r_, p_, g_: tuple(f(r_, t, p_, g_, c)), r, p, g)
            dr, dp, dg = vjp(tuple(ct[0].astype(F32) for ct in ct_refs))
            dr_refs, dp_refs, dg_refs = outs[:nr], outs[nr:nr + npp], outs[nr + npp:]
            for ref, d in zip(dr_refs, dr):
                ref[0] = d.astype(ref.dtype)
            b, l = pl.program_id(0), pl.program_id(1)
            first_of_segment = (l == 0) | (l == ctx_blocks)
            for ref, d in zip(dp_refs, dp):
                @pl.when(first_of_segment)
                def _(ref=ref, d=d):
                    ref[0, 0] = d

                @pl.when(jnp.logical_not(first_of_segment))
                def _(ref=ref, d=d):
                    ref[0, 0] += d
            first = (b == 0) & (l == 0)
            for ref, d in zip(dg_refs, dg):
                @pl.when(first)
                def _(ref=ref, d=d):
                    ref[...] = d

                @pl.when(jnp.logical_not(first))
                def _(ref=ref, d=d):
                    ref[...] += d

        ct_specs = [pl.BlockSpec((1, tl, w), lambda b, l: (b, l, 0)) for w, _ in out_specs]
        outs = _pcall(
            body, name=name + "_bwd", grid=(bsz, length // tl),
            in_specs=rs + ts + ps + gs + cs + ct_specs,
            out_specs=rs + ps + gs,
            out_shape=[jax.ShapeDtypeStruct(r.shape, r.dtype) for r in rows]
            + [jax.ShapeDtypeStruct(p.shape, F32) for p in pers] + [jax.ShapeDtypeStruct(g.shape, F32) for g in glbs],
            compiler_params=_params(("arbitrary", "arbitrary")),
        )(*rows, *tabs, *pers, *glbs, *consts, *cts)
        return tuple(outs[:nr]), tuple(outs[nr:nr + npp]), tuple(outs[nr + npp:])

    @jax.custom_vjp
    def op(rows, tabs, pers, glbs, consts):
        return tuple(call_fwd(rows, tabs, pers, glbs, consts))

    def fwd(rows, tabs, pers, glbs, consts):
        return op(rows, tabs, pers, glbs, consts), (rows, tabs, pers, glbs, consts)

    def bwd(res, cts):
        rows, tabs, pers, glbs, consts = res
        dr, dp, dg = call_bwd(rows, tabs, pers, glbs, consts, tuple(cts))
        dp = tuple(d.astype(p.dtype) for d, p in zip(dp, pers))
        dg = tuple(d.astype(g.dtype) for d, g in zip(dg, glbs))
        return dr, tuple(jnp.zeros_like(t) for t in tabs), dp, dg, tuple(jnp.zeros_like(c) for c in consts)

    op.defvjp(fwd, bwd)
    return op


def _rms(x, g):
    return x * lax.rsqrt(jnp.mean(x * x, axis=-1, keepdims=True) + EPS) * g


def _f_silu(r, t, p, g, c):
    return (jax.nn.silu(r[0]),)


def _f_modulate(r, t, p, g, c):
    shift, scale = p
    return (_rms(r[0], g[0]) * (1.0 + scale) + shift,)


def _f_res_modulate(r, t, p, g, c):
    x, y = r
    gate, shift, scale = p
    xn = x + gate * y
    return xn, _rms(xn, g[0]) * (1.0 + scale) + shift


def _f_res(r, t, p, g, c):
    return (r[0] + p[0] * r[1],)


def _f_headnorm_rope(r, t, p, g, c):
    x = r[0]
    cos, sin = t
    xn = x * lax.rsqrt(_sym_dot(x * x, c[0]) + EPS) * g[0]
    return (xn * cos + _swap_pairs(xn) * sin,)


def _f_rope(r, t, p, g, c):
    x = r[0]
    cos, sin = t
    return (x * cos + _swap_pairs(x) * sin,)


def _f_rope_narrow(r, t, p, g, c):
    x = r[0]
    cos, sin = t
    return (x * cos + _sym_dot(x, c[0]) * sin,)


def _f_rms(r, t, p, g, c):
    return (_rms(r[0], g[0]),)


def _f_sgu_pre(r, t, p, g, c):
    u = jax.nn.gelu(r[0])
    v = jax.nn.gelu(r[1])
    vn = v * lax.rsqrt(_sym_dot(v * v, c[0]) + EPS) * g[0]
    return u, vn


def _f_sgu_mix(r, t, p, g, c):
    u, vn = r
    bias = g[B_GROUPS]
    group = lax.broadcasted_iota(jnp.int32, vn.shape, 1) // B_GROUP_DIM
    sv = bias
    for k in range(B_GROUPS):
        sv = sv + jnp.where(group == k, _bdot(g[k], vn), 0.0)
    return (u * sv,)


def _f_glu(r, t, p, g, c):
    return (r[0] * jax.nn.sigmoid(r[1]),)


def _f_ln_silu(r, t, p, g, c):
    x = r[0]
    mu = jnp.mean(x, axis=-1, keepdims=True)
    var = jnp.mean(jnp.square(x - mu), axis=-1, keepdims=True)
    return (jax.nn.silu((x - mu) * lax.rsqrt(var + EPS) * g[0] + g[1]),)


LOG2_E = 1.4426950408889634
LN_2 = 0.6931471805599453


def _attention(name, ctx_len):
    def tiles(q):
        tq = _pick(q.shape[2], (512, 256, 128))
        if ctx_len:
            tq = math.gcd(tq, ctx_len)
        return tq, ctx_len // tq

    def scores(q_ref, kk, dk):
        qs = (q_ref[0, 0].astype(F32) * (LOG2_E * dk ** -0.5)).astype(BF16)
        return qs, lax.dot_general(qs, kk, (((1,), (1,)), ((), ())), preferred_element_type=F32)

    def by_segment(qi, cb, lk, run):
        if cb > 0:
            @pl.when(qi < cb)
            def _():
                run(ctx_len)

            @pl.when(qi >= cb)
            def _():
                run(lk)
        else:
            run(lk)

    def call_fwd(q, k, v):
        bsz, hq, lq, dk = q.shape
        _, hk, lk, dv = v.shape
        grp = hq // hk
        tq, cb = tiles(q)
        v_ones = jnp.concatenate([v, jnp.ones_like(v)], axis=-1)

        def body(q_ref, k_ref, v_ref, o_ref, lse_ref):
            def run(nk):
                _, s = scores(q_ref, k_ref[0, 0, :nk], dk)
                m = jnp.max(s, axis=-1, keepdims=True)
                acc = jnp.dot(jnp.exp2(s - m).astype(BF16), v_ref[0, 0, :nk], preferred_element_type=F32)
                l = acc[:, dv:dv + 1]
                o_ref[0, 0] = (acc[:, :dv] / l).astype(o_ref.dtype)
                lse_ref[0, 0] = m + jnp.log2(l)

            by_segment(pl.program_id(2), cb, lk, run)

        return _pcall(
            body, name=name + "_fwd", grid=(bsz, hq, lq // tq),
            in_specs=[pl.BlockSpec((1, 1, tq, dk), lambda b, h, i: (b, h, i, 0)),
                      pl.BlockSpec((1, 1, lk, dk), lambda b, h, i: (b, h // grp, 0, 0)),
                      pl.BlockSpec((1, 1, lk, 2 * dv), lambda b, h, i: (b, h // grp, 0, 0))],
            out_specs=[pl.BlockSpec((1, 1, tq, dv), lambda b, h, i: (b, h, i, 0)), pl.BlockSpec((1, 1, tq, 1), lambda b, h, i: (b, h, i, 0))],
            out_shape=[jax.ShapeDtypeStruct((bsz, hq, lq, dv), BF16), jax.ShapeDtypeStruct((bsz, hq, lq, 1), F32)],
            compiler_params=_params(("parallel", "parallel", "parallel")),
        )(q, k, v_ones)

    def call_bwd(q, k, v, o, lse, do):
        bsz, hq, lq, dk = q.shape
        _, hk, lk, dv = v.shape
        grp = hq // hk
        tq, cb = tiles(q)
        nq = lq // tq

        def body(q_ref, k_ref, v_ref, o_ref, lse_ref, do_ref, dq_ref, dk_ref, dv_ref):
            g, qi = pl.program_id(2), pl.program_id(3)

            @pl.when((g == 0) & (qi == 0))
            def _():
                dk_ref[...] = jnp.zeros_like(dk_ref)
                dv_ref[...] = jnp.zeros_like(dv_ref)

            def run(nk):
                kk, vv, dd = k_ref[0, 0, :nk], v_ref[0, 0, :nk], do_ref[0, 0]
                qs, s = scores(q_ref, kk, dk)
                p = jnp.exp2(s - lse_ref[0, 0])
                delta = jnp.sum(dd.astype(F32) * o_ref[0, 0].astype(F32), axis=-1, keepdims=True)
                dp = lax.dot_general(dd, vv, (((1,), (1,)), ((), ())), preferred_element_type=F32)
                t = (p * (dp - delta)).astype(BF16)
                dq_ref[0, 0] = (jnp.dot(t, kk, preferred_element_type=F32) * dk ** -0.5).astype(dq_ref.dtype)
                dk_ref[0, 0, :nk] += lax.dot_general(t, qs, (((0,), (0,)), ((), ())), preferred_element_type=F32)
                dv_ref[0, 0, :nk] += lax.dot_general(p.astype(BF16), dd, (((0,), (0,)), ((), ())), preferred_element_type=F32)

            by_segment(qi, cb, lk, run)

            @pl.when((g == grp - 1) & (qi == nq - 1))
            def _():
                dk_ref[...] = dk_ref[...] * LN_2

        q_spec = pl.BlockSpec((1, 1, tq, dk), lambda b, h, g, i: (b, h * grp + g, i, 0))
        k_spec = pl.BlockSpec((1, 1, lk, dk), lambda b, h, g, i: (b, h, 0, 0))
        v_spec = pl.BlockSpec((1, 1, lk, dv), lambda b, h, g, i: (b, h, 0, 0))
        o_spec = pl.BlockSpec((1, 1, tq, dv), lambda b, h, g, i: (b, h * grp + g, i, 0))
        lse_spec = pl.BlockSpec((1, 1, tq, 1), lambda b, h, g, i: (b, h * grp + g, i, 0))
        return _pcall(
            body, name=name + "_bwd", grid=(bsz, hk, grp, nq),
            in_specs=[q_spec, k_spec, v_spec, o_spec, lse_spec, o_spec],
            out_specs=[q_spec, k_spec, v_spec],
            out_shape=[jax.ShapeDtypeStruct(q.shape, BF16), jax.ShapeDtypeStruct(k.shape, F32), jax.ShapeDtypeStruct(v.shape, F32)],
            compiler_params=_params(("parallel", "parallel", "arbitrary", "arbitrary")),
        )(q, k, v, o, lse, do)

    @jax.custom_vjp
    def op(q, k, v):
        return call_fwd(q, k, v)[0]

    def fwd(q, k, v):
        o, lse = call_fwd(q, k, v)
        return o, (q, k, v, o, lse)

    def bwd(res, do):
        q, k, v, o, lse = res
        dq, dk, dv = call_bwd(q, k, v, o, lse, do)
        return dq, dk.astype(k.dtype), dv.astype(v.dtype)

    op.defvjp(fwd, bwd)
    return op


def _conv_call(ypad, taps, name):
    bsz, lp, ch = ypad.shape
    length = lp - 2 * ROW_ALIGN
    tl = _pick(length, (256, 128))

    def body(y_ref, w_ref, o_ref):
        base = pl.multiple_of(pl.program_id(1) * tl, tl)
        win = y_ref[0, pl.ds(base, tl + 2 * ROW_ALIGN), :]
        acc = jnp.broadcast_to(w_ref[pl.ds(D_CONV, 1), :], (tl, ch))
        for k in range(D_CONV):
            acc = acc + win[k:k + tl, :] * w_ref[pl.ds(k, 1), :]
        o_ref[0] = acc

    return _pcall(
        body, name=name, grid=(bsz, length // tl),
        in_specs=[pl.BlockSpec((1, lp, ch), lambda b, l: (b, 0, 0)), pl.BlockSpec((D_CONV + 1, ch), lambda b, l: (0, 0))],
        out_specs=pl.BlockSpec((1, tl, ch), lambda b, l: (b, l, 0)),
        out_shape=jax.ShapeDtypeStruct((bsz, length, ch), F32),
        compiler_params=_params(("parallel", "parallel")),
    )(ypad, taps)


def _conv_dw_call(ypad, dout, name):
    bsz, lp, ch = ypad.shape
    length = lp - 2 * ROW_ALIGN
    tl = _pick(length, (256, 128))

    def body(y_ref, d_ref, o_ref):
        b, l = pl.program_id(0), pl.program_id(1)

        @pl.when((b == 0) & (l == 0))
        def _():
            o_ref[...] = jnp.zeros_like(o_ref)

        base = pl.multiple_of(l * tl, tl)
        win = y_ref[0, pl.ds(base, tl + 2 * ROW_ALIGN), :]
        dd = d_ref[0]
        for k in range(D_CONV):
            o_ref[pl.ds(k, 1), :] += jnp.sum(win[k:k + tl, :] * dd, axis=0, keepdims=True)
        o_ref[pl.ds(D_CONV, 1), :] += jnp.sum(dd, axis=0, keepdims=True)

    return _pcall(
        body, name=name, grid=(bsz, length // tl),
        in_specs=[pl.BlockSpec((1, lp, ch), lambda b, l: (b, 0, 0)), pl.BlockSpec((1, tl, ch), lambda b, l: (b, l, 0))],
        out_specs=pl.BlockSpec((D_CONV + 1, ch), lambda b, l: (0, 0)),
        out_shape=jax.ShapeDtypeStruct((D_CONV + 1, ch), F32),
        compiler_params=_params(("arbitrary", "arbitrary")),
    )(ypad, dout)


def _pad_rows(y):
    return jnp.pad(y, ((0, 0), (CONV_PAD, 2 * ROW_ALIGN - CONV_PAD), (0, 0)))


@jax.custom_vjp
def _dwconv(y, taps):
    return _conv_call(_pad_rows(y), taps, "conv_fwd")


def _dwconv_fwd(y, taps):
    return _dwconv(y, taps), (y, taps)


def _dwconv_bwd(res, dout):
    y, taps = res
    flipped = jnp.concatenate([taps[:D_CONV][::-1], jnp.zeros_like(taps[D_CONV:])], axis=0)
    dy = _conv_call(_pad_rows(dout), flipped, "conv_dy")
    dtaps = _conv_dw_call(_pad_rows(y), dout, "conv_dw")
    return dy, dtaps


_dwconv.defvjp(_dwconv_fwd, _dwconv_bwd)


def _loss_head(x, target, g):
    bsz, length, d = x.shape
    tl = _pick(length, (256, 128))

    def f(xb, tb, gb):
        err = _rms(xb, gb) - tb
        return 0.5 * jnp.sum(jnp.sum(err * err, axis=-1, keepdims=True), axis=0, keepdims=True) / d

    def body(x_ref, t_ref, g_ref, loss_ref, dx_ref, dg_ref):
        val, vjp = jax.vjp(lambda xb, gb: f(xb, t_ref[0], gb), x_ref[0], g_ref[...])
        dx, dg = vjp(jnp.ones((1, 1), F32))
        dx_ref[0] = dx
        first = (pl.program_id(0) == 0) & (pl.program_id(1) == 0)

        @pl.when(first)
        def _():
            loss_ref[...] = val
            dg_ref[...] = dg

        @pl.when(jnp.logical_not(first))
        def _():
            loss_ref[...] += val
            dg_ref[...] += dg

    row = pl.BlockSpec((1, tl, d), lambda b, l: (b, l, 0))
    return _pcall(
        body, name="loss_head", grid=(bsz, length // tl),
        in_specs=[row, row, pl.BlockSpec((1, d), lambda b, l: (0, 0))],
        out_specs=[pl.BlockSpec((1, 1), lambda b, l: (0, 0)), row, pl.BlockSpec((1, d), lambda b, l: (0, 0))],
        out_shape=[jax.ShapeDtypeStruct((1, 1), F32), jax.ShapeDtypeStruct(x.shape, F32), jax.ShapeDtypeStruct((1, d), F32)],
        compiler_params=_params(("arbitrary", "arbitrary")),
    )(x, target, g)


def _adamw(w, g, m, v, name):
    shape = w.shape
    cols = shape[-1]
    rows = w.size // cols
    tr = _pick(rows, (512, 256, 128))
    w2, g2, m2, v2 = (t.reshape(rows, cols) for t in (w, g, m, v))

    def body(w_ref, g_ref, m_ref, v_ref, d_ref, nm_ref, nv_ref):
        gg = g_ref[...]
        nm = ADAM_B1 * m_ref[...] + (1.0 - ADAM_B1) * gg
        nv = ADAM_B2 * v_ref[...] + (1.0 - ADAM_B2) * jnp.square(gg)
        m_hat = nm / (1.0 - ADAM_B1 ** ADAM_STEP)
        v_hat = nv / (1.0 - ADAM_B2 ** ADAM_STEP)
        d_ref[...] = -ADAM_LR * (m_hat / (jnp.sqrt(v_hat) + ADAM_EPS) + ADAM_WD * w_ref[...])
        nm_ref[...] = nm
        nv_ref[...] = nv

    spec = pl.BlockSpec((tr, cols), lambda i: (i, 0))
    outs = _pcall(
        body, name=name, grid=(rows // tr,), in_specs=[spec] * 4, out_specs=[spec] * 3,
        out_shape=[jax.ShapeDtypeStruct((rows, cols), F32)] * 3,
        compiler_params=_params(("parallel",)),
    )(w2, g2, m2, v2)
    return tuple(o.reshape(shape) for o in outs)


def _mesh_pos():
    return lax.axis_index("x"), lax.axis_index("y"), lax.axis_index("c")


_RELATIONS = [(dx, dy, dc) for dx in (0, 1) for dy in (0, 1) for dc in (0, 1)][1:]


def _peer(pos, rel):
    return tuple(jnp.where(r == 1, 1 - p, p) if r else p for p, r in zip(pos, rel))


def _block_index(pos):
    return 4 * pos[0] + 2 * pos[1] + pos[2]


_HBM = pl.BlockSpec(memory_space=pltpu.HBM)


def _all_gather(xs, name):
    n = len(xs)

    def body(*refs):
        x_refs, out_refs, (send_sems, recv_sems, local_sems) = refs[:n], refs[n:2 * n], refs[2 * n:]
        x_, y_, c_ = _mesh_pos()
        me, sibling = (x_, y_, c_), (x_, y_, 1 - c_)
        chips = [(1 - x_, y_), (x_, 1 - y_), (1 - x_, 1 - y_)]

        def copy(t, k, block, to, own=False):
            slot = out_refs[t].at[_block_index(block)]
            return pltpu.make_async_remote_copy(
                src_ref=x_refs[t] if own else slot, dst_ref=slot, send_sem=send_sems.at[7 * t + k], recv_sem=recv_sems.at[7 * t + k],
                device_id=to, device_id_type=pl.DeviceIdType.MESH)

        mine = [pltpu.make_async_copy(x_refs[t], out_refs[t].at[_block_index(me)], local_sems.at[t]) for t in range(n)]
        first = [[copy(t, 0, me, sibling, own=True)] + [copy(t, 1 + j, me, (*chip, c_), own=True) for j, chip in enumerate(chips)]
                 for t in range(n)]
        passed = [[copy(t, 4 + j, (*chip, c_), sibling) for j, chip in enumerate(chips)] for t in range(n)]
        for t in range(n):
            mine[t].start()
            for cp in first[t]:
                cp.start()
        for t in range(n):
            for j, chip in enumerate(chips):
                copy(t, 1 + j, (*chip, c_), me).wait_recv()
                passed[t][j].start()
        for t in range(n):
            copy(t, 0, sibling, me).wait_recv()
            for j, chip in enumerate(chips):
                copy(t, 4 + j, (*chip, 1 - c_), me).wait_recv()
            for cp in first[t] + passed[t]:
                cp.wait_send()
            mine[t].wait()

    return _pcall(
        body, name=name, in_specs=[_HBM] * n, out_specs=[_HBM] * n,
        out_shape=[jax.ShapeDtypeStruct((N_DEV,) + x.shape, x.dtype) for x in xs],
        scratch_shapes=[pltpu.SemaphoreType.DMA((7 * n,)), pltpu.SemaphoreType.DMA((7 * n,)), pltpu.SemaphoreType.DMA((n,))],
    )(*xs)


def _exchange_blocks(ps, name):
    n = len(ps)

    def body(*refs):
        p_refs, out_refs, (send_sems, recv_sems, local_sems) = refs[:n], refs[n:2 * n], refs[2 * n:]
        me = _mesh_pos()

        def copy(t, k, rel, arriving=False):
            peer = _peer(me, rel)
            return pltpu.make_async_remote_copy(
                src_ref=p_refs[t].at[_block_index(peer)], dst_ref=out_refs[t].at[_block_index(peer if arriving else me)],
                send_sem=send_sems.at[7 * t + k], recv_sem=recv_sems.at[7 * t + k], device_id=peer, device_id_type=pl.DeviceIdType.MESH)

        mine = [pltpu.make_async_copy(p_refs[t].at[_block_index(me)], out_refs[t].at[_block_index(me)], local_sems.at[t]) for t in range(n)]
        sends = [copy(t, k, rel) for t in range(n) for k, rel in enumerate(_RELATIONS)]
        for cp in mine + sends:
            cp.start()
        for t in range(n):
            for k, rel in enumerate(_RELATIONS):
                copy(t, k, rel, arriving=True).wait_recv()
        for cp in sends:
            cp.wait_send()
        for cp in mine:
            cp.wait()

    return _pcall(
        body, name=name, in_specs=[_HBM] * n, out_specs=[_HBM] * n,
        out_shape=[jax.ShapeDtypeStruct(p.shape, p.dtype) for p in ps],
        scratch_shapes=[pltpu.SemaphoreType.DMA((7 * n,)), pltpu.SemaphoreType.DMA((7 * n,)), pltpu.SemaphoreType.DMA((n,))],
    )(*ps)


_SEM = pl.BlockSpec(memory_space=pltpu.SEMAPHORE)
_EFFECT = pltpu.SideEffectType.DATAFLOW_SIDE_EFFECTING


def _push_start(srcs, after, name):
    n = len(srcs)
    lands = [lax.empty((N_DEV,) + s.shape[-2:], s.dtype) for s in srcs]

    def body(*refs):
        src_refs, land_refs = refs[:n], refs[n:2 * n]
        send_sems, recv_sems, token = refs[2 * n + 1:3 * n + 1], refs[3 * n + 1:4 * n + 1], refs[-1]
        me = _mesh_pos()
        for t in range(n):
            for rel in _RELATIONS:
                peer = _peer(me, rel)
                pltpu.make_async_remote_copy(
                    src_ref=src_refs[t].at[_block_index(peer)] if srcs[t].ndim == 3 else src_refs[t], dst_ref=land_refs[t].at[_block_index(me)],
                    send_sem=send_sems[t], recv_sem=recv_sems[t], device_id=peer, device_id_type=pl.DeviceIdType.MESH).start()
        token[...] = jnp.zeros_like(token)

    outs = _pcall(
        body, name=name,
        out_shape=[pltpu.SemaphoreType.DMA(())] * (2 * n) + [pltpu.HBM(s.shape, s.dtype) for s in srcs]
        + [pltpu.HBM(l.shape, l.dtype) for l in lands] + [jax.ShapeDtypeStruct((8, 128), F32)],
        in_specs=[_HBM] * (2 * n) + [pl.BlockSpec(memory_space=pl.ANY)],
        out_specs=[_SEM] * (2 * n) + [_HBM] * (2 * n) + [pl.BlockSpec(memory_space=pltpu.VMEM)],
        input_output_aliases={i: 2 * n + i for i in range(2 * n)}, compiler_params=pltpu.CompilerParams(has_side_effects=_EFFECT),
    )(*[pltpu.with_memory_space_constraint(t, pltpu.HBM) for t in list(srcs) + lands], after)
    return (outs[:n], outs[n:2 * n], outs[2 * n:3 * n], outs[3 * n:4 * n]), outs[-1]


def _push_wait(handle, after, owns, me, name):
    send_sems, recv_sems, src_thrus, land_thrus = handle
    n = len(land_thrus)

    def body(*refs):
        land_refs, sends, recvs = refs[n:2 * n], refs[2 * n:3 * n], refs[3 * n:4 * n]
        for t in range(n):
            seven = land_refs[t].at[pl.ds(0, N_DEV - 1)]
            all_seven = pltpu.make_async_remote_copy(src_ref=seven, dst_ref=seven, send_sem=sends[t], recv_sem=recvs[t],
                                                     device_id=_mesh_pos(), device_id_type=pl.DeviceIdType.MESH)
            all_seven.wait_send()
            all_seven.wait_recv()

    outs = _pcall(
        body, name=name,
        out_shape=[pltpu.HBM(t.shape, t.dtype) for t in list(src_thrus) + list(land_thrus)],
        in_specs=[_HBM] * (2 * n) + [_SEM] * (2 * n) + [pl.BlockSpec(memory_space=pl.ANY)], out_specs=[_HBM] * (2 * n),
        input_output_aliases={i: i for i in range(2 * n)}, compiler_params=pltpu.CompilerParams(has_side_effects=_EFFECT),
    )(*src_thrus, *land_thrus, *send_sems, *recv_sems, after)
    return [lax.dynamic_update_slice(land, own[None], (me, 0, 0)) for land, own in zip(outs[n:], owns)]


def _sum_blocks(p, name):
    n, rows, cols = p.shape
    tr = _pick(rows, (256, 128, 64, PACK_ROWS, 16, 8))

    def body(p_ref, o_ref):
        acc = p_ref[0].astype(F32)
        for s in range(1, n):
            acc = acc + p_ref[s].astype(F32)
        o_ref[...] = acc

    return _pcall(
        body, name=name, grid=(rows // tr,),
        in_specs=[pl.BlockSpec((n, tr, cols), lambda i: (0, i, 0))], out_specs=pl.BlockSpec((tr, cols), lambda i: (i, 0)),
        out_shape=jax.ShapeDtypeStruct((rows, cols), F32), compiler_params=_params(("parallel",)),
    )(p)


def _sum_rows(t, name):
    def body(t_ref, o_ref):
        o_ref[...] = jnp.sum(t_ref[...], axis=0, keepdims=True)

    return _pcall(body, name=name, out_shape=jax.ShapeDtypeStruct((1, t.shape[1]), F32))(t)


class _Packing:
    def __init__(self, sizes, align, total_align=None):
        self.offsets, self.sizes, self.align = {}, dict(sizes), align
        row = 0
        for name, size in sizes:
            self.offsets[name] = row
            row += -(-size // (align * PACK_COLS)) * align
        total_align = total_align or align
        self.rows = -(-row // total_align) * total_align
        self.tail = self.rows - row

    def pack(self, pieces):
        return self.pack_blocks({n: pieces[n].reshape(1, -1) for n in self.sizes})[0]

    def pack_blocks(self, pieces):
        out = []
        for n, size in self.sizes.items():
            padded = -(-size // (self.align * PACK_COLS)) * self.align * PACK_COLS
            out.append(jnp.pad(pieces[n], ((0, 0), (0, padded - size))).reshape(pieces[n].shape[0], -1, PACK_COLS))
        if self.tail:
            out.append(jnp.zeros((out[0].shape[0], self.tail, PACK_COLS), out[0].dtype))
        return jnp.concatenate(out, axis=1)

    def piece(self, packed, name, lead=()):
        start, size = self.offsets[name], self.sizes[name]
        nrow = -(-size // (self.align * PACK_COLS)) * self.align
        sl = packed[..., start:start + nrow, :]
        return sl.reshape(lead + (nrow * PACK_COLS,))[..., :size]


_PIECES = (("ev_w_in", "ev_w_in", 0, 1, "a"), ("w_out0", "w_out", 0, 0, "a"),
           ("mlp_w1_0", "mlp_w1", 0, 1, "b"), ("mlp_w2_0", "mlp_w2", 0, 0, "b"),
           ("od_w_in", "od_w_in", 0, 1, "c"), ("od_w_uq", "od_w_uq", 0, 1, "c"), ("od_w_ukv", "od_w_ukv", 0, 1, "c"),
           ("w_out1", "w_out", 1, 0, "c"), ("mlp_w1_1", "mlp_w1", 1, 1, "c"), ("mlp_w2_1", "mlp_w2", 1, 0, "c"))
_SMALL_SHARDED = (("od_q_norm_g", 1), ("od_conv_w", 2), ("od_conv_b", 1), ("od_ln_g", 1), ("od_ln_b", 1))
_REPLICATED = ("c_ctx", "norm1_g", "norm2_g", "ev_q_norm_g", "ev_k_norm_g", "ev_sgu_norm_g", "ev_sgu_w", "ev_sgu_b",
               "od_kv_norm_g", "final_g")


def _unshard(blocks, axis):
    moved = jnp.moveaxis(blocks, 0, axis)
    shape = moved.shape
    return moved.reshape(shape[:axis] + (shape[axis] * shape[axis + 1],) + shape[axis + 2:])


def _shard_blocks(full, axis):
    shape = full.shape
    split = full.reshape(shape[:axis] + (N_DEV, shape[axis] // N_DEV) + shape[axis + 1:])
    return jnp.moveaxis(split, axis, 0)


def _group_mean_matrix(width, group):
    idx = jnp.arange(width) // group
    return (idx[:, None] == idx[None, :]).astype(F32) / group


def _swap_matrix(width):
    idx = jnp.arange(width)
    return ((idx[:, None] ^ 1) == idx[None, :]).astype(F32)


def _angles(length, d_rot):
    rows = length // GRID_W
    row = jnp.broadcast_to(jnp.arange(rows)[:, None], (rows, GRID_W)).reshape(-1).astype(F32)
    col = jnp.broadcast_to(jnp.arange(GRID_W)[None, :], (rows, GRID_W)).reshape(-1).astype(F32)
    d_axis = d_rot // 2
    inv = ROPE_THETA ** (-jnp.arange(0, d_axis, 2, dtype=F32) / d_axis)
    return jnp.concatenate([row[:, None] * inv, col[:, None] * inv], axis=-1)


def _rope_tables(length, d_rot, head_dim, heads, ctx_len):
    ang = _angles(length, d_rot)
    cos = jnp.repeat(jnp.cos(ang), 2, axis=1)
    sin = jnp.repeat(jnp.sin(ang), 2, axis=1) * jnp.tile(jnp.array([-1.0, 1.0], F32), d_rot // 2)
    keep = head_dim - d_rot
    cos = jnp.concatenate([jnp.ones((length, keep), F32), cos], axis=1)
    sin = jnp.concatenate([jnp.zeros((length, keep), F32), sin], axis=1)
    cos, sin = jnp.tile(cos, (1, heads)), jnp.tile(sin, (1, heads))
    cos = jnp.concatenate([jnp.ones((ctx_len, cos.shape[1]), F32), cos], axis=0)
    sin = jnp.concatenate([jnp.zeros((ctx_len, sin.shape[1]), F32), sin], axis=0)
    return cos, sin


def _to_heads(t, heads):
    b, l, w = t.shape
    return t.reshape(b, l, heads, w // heads).transpose(0, 2, 1, 3)


def _from_heads(t):
    b, h, l, d = t.shape
    return t.transpose(0, 2, 1, 3).reshape(b, l, h * d)


def _segment_params(mod, bsz):
    parts = jnp.split(mod, N_MOD, axis=-1)
    out = []
    for part in parts:
        lat = part[:bsz]
        ctx = jnp.broadcast_to(part[bsz:bsz + 1], lat.shape)
        out.append(jnp.stack([ctx, lat], axis=1)[:, :, None, :])
    return out


def _flat(t):
    return t.reshape(-1, t.shape[-1])


def _sequence_rowwise(ctx_len):
    tl = math.gcd(256, ctx_len)

    def make(name, f, out_specs, rows_per_block=tl, ctx_blocks=ctx_len // tl):
        return _rowwise(name, f, out_specs, rows_per_block, ctx_blocks)

    return make


def _mixer0(xall, modrows0, w, bsz, length, ctx_len):
    d = xall.shape[-1]
    total = ctx_len + length
    rowwise, flat = _sequence_rowwise(ctx_len), _flat
    sh1, sc1, g1, sh2, sc2, _ = _segment_params(modrows0, bsz)
    (h,) = rowwise("mod0", _f_modulate, [(d, BF16)])((xall,), (), (sh1, sc1), (w["norm1_g0"],), ())
    proj = _linear("ev_in", transposed=True)(flat(h), w["ev_w_in"]).reshape(bsz, total, -1)
    ev_q, ev_kv = A_Q_HEADS * A_HEAD_DIM, A_KV_HEADS * A_HEAD_DIM
    half = B_GROUPS * B_GROUP_DIM
    qp, kp, vp, zu, zv = jnp.split(proj, [ev_q, ev_q + ev_kv, ev_q + 2 * ev_kv, ev_q + 2 * ev_kv + half], axis=-1)
    cos_q, sin_q = _rope_tables(length, A_HEAD_DIM, A_HEAD_DIM, A_Q_HEADS, ctx_len)
    cos_k, sin_k = cos_q[:, :ev_kv], sin_q[:, :ev_kv]
    (q,) = rowwise("ev_q", _f_headnorm_rope, [(ev_q, BF16)])(
        (qp,), (cos_q, sin_q), (), (jnp.tile(w["ev_q_norm_g"][0], A_Q_HEADS)[None],), (_group_mean_matrix(ev_q, A_HEAD_DIM),))
    (k,) = rowwise("ev_k", _f_headnorm_rope, [(ev_kv, BF16)])(
        (kp,), (cos_k, sin_k), (), (jnp.tile(w["ev_k_norm_g"][0], A_KV_HEADS)[None],), (_group_mean_matrix(ev_kv, A_HEAD_DIM),))
    o_att = _attention("gqa", ctx_len)(_to_heads(q, A_Q_HEADS), _to_heads(k, A_KV_HEADS), _to_heads(vp.astype(BF16), A_KV_HEADS))
    u, vn = rowwise("sgu_pre", _f_sgu_pre, [(half, F32), (half, BF16)])(
        (zu, zv), (), (), (w["ev_sgu_norm_g"][0].reshape(1, half),), (_group_mean_matrix(half, B_GROUP_DIM),))
    bias = jnp.repeat(w["ev_sgu_b"][0].T, B_GROUP_DIM, axis=1)
    (o_sgu,) = rowwise("sgu_mix", _f_sgu_mix, [(half, BF16)], rows_per_block=B_CHUNK, ctx_blocks=0)(
        (u, vn), (), (), tuple(w["ev_sgu_w"][0][g] for g in range(B_GROUPS)) + (bias,), ())
    o = jnp.concatenate([_from_heads(o_att), o_sgu], axis=-1)
    y = _linear("out0")(flat(o), w["w_out0"]).reshape(bsz, total, d)
    x1, h = rowwise("res_mod0a", _f_res_modulate, [(d, F32), (d, BF16)])((xall, y), (), (g1, sh2, sc2), (w["norm2_g0"],), ())
    return x1, h


def _mlp0(x1, h, modrows0, modrows1, w, bsz, length, ctx_len):
    d = x1.shape[-1]
    total = ctx_len + length
    g2 = _segment_params(modrows0, bsz)[5]
    sh1, sc1 = _segment_params(modrows1, bsz)[:2]
    y = _mlp("mlp0")(_flat(h), w["mlp_w1_0"], w["mlp_w2_0"]).reshape(bsz, total, d)
    return _sequence_rowwise(ctx_len)("res_mod0b", _f_res_modulate, [(d, F32), (d, BF16)])((x1, y), (), (g2, sh1, sc1), (w["norm1_g1"],), ())


def _layer1(x2, h, modrows1, w, bsz, length, ctx_len):
    d = x2.shape[-1]
    total = ctx_len + length
    half = B_GROUPS * B_GROUP_DIM
    rowwise, flat = _sequence_rowwise(ctx_len), _flat
    _, _, g1n, sh2n, sc2n, g2n = _segment_params(modrows1, bsz)
    proj = _linear("od_in", transposed=True)(flat(h), w["od_w_in"]).reshape(bsz, total, -1)
    cq, ckv, kr, za, zg = jnp.split(proj, [C_Q_RANK, C_Q_RANK + C_KV_RANK, C_Q_RANK + C_KV_RANK + C_ROPE,
                                          C_Q_RANK + C_KV_RANK + C_ROPE + half], axis=-1)
    lat = slice(ctx_len, total)
    lat_tl = math.gcd(256, length)
    (cqn,) = _rowwise("od_qn", _f_rms, [(C_Q_RANK, BF16)], lat_tl)((cq[:, lat],), (), (), (w["od_q_norm_g"],), ())
    c_qk = C_NOPE + C_ROPE
    qf = _linear("od_uq", transposed=True)(flat(cqn), w["od_w_uq"]).reshape(bsz, length, C_HEADS * c_qk)
    cos_q, sin_q = _rope_tables(length, C_ROPE, c_qk, C_HEADS, 0)
    (q,) = _rowwise("od_qrope", _f_rope, [(C_HEADS * c_qk, BF16)], lat_tl)((qf,), (cos_q, sin_q), (), (), ())
    (ckvn,) = rowwise("od_kvn", _f_rms, [(C_KV_RANK, BF16)])((ckv,), (), (), (w["od_kv_norm_g"],), ())
    kv = _linear("od_ukv", transposed=True)(flat(ckvn), w["od_w_ukv"]).reshape(bsz, total, C_HEADS, C_NOPE + C_V)
    cos_r, sin_r = _rope_tables(length, C_ROPE, C_ROPE, 1, ctx_len)
    (krr,) = rowwise("od_krope", _f_rope_narrow, [(C_ROPE, BF16)])((kr,), (cos_r, sin_r), (), (), (_swap_matrix(C_ROPE),))
    kn = kv[..., :C_NOPE].astype(BF16).transpose(0, 2, 1, 3)
    vv = kv[..., C_NOPE:].astype(BF16).transpose(0, 2, 1, 3)
    kfull = jnp.concatenate([kn, jnp.broadcast_to(krr[:, None], (bsz, C_HEADS, total, C_ROPE))], axis=-1)
    o_att = _attention("mla", 0)(_to_heads(q, C_HEADS), kfull, vv)
    (glu,) = _rowwise("glu", _f_glu, [(half, F32)], lat_tl)((za[:, lat], zg[:, lat]), (), (), (), ())
    taps = jnp.concatenate([w["od_conv_w"][0], w["od_conv_b"]], axis=0)
    conv = _dwconv(glu, taps)
    (o_conv,) = _rowwise("ln_silu", _f_ln_silu, [(half, BF16)], lat_tl)((conv,), (), (), (w["od_ln_g"], w["od_ln_b"]), ())
    o = jnp.concatenate([_from_heads(o_att), o_conv], axis=-1)
    y = _linear("out1")(flat(o), w["w_out1"]).reshape(bsz, length, d)
    lat_param = lambda p: p[:, 1:]
    x3, h = _rowwise("res_mod1a", _f_res_modulate, [(d, F32), (d, BF16)], lat_tl)(
        (x2[:, lat], y), (), (lat_param(g1n), lat_param(sh2n), lat_param(sc2n)), (w["norm2_g1"],), ())
    y = _mlp("mlp1")(flat(h), w["mlp_w1_1"], w["mlp_w2_1"]).reshape(bsz, length, d)
    (x4,) = _rowwise("res1b", _f_res, [(d, F32)], lat_tl)((x3, y), (), (lat_param(g2n),), (), ())
    return x4


def kernel(x, c, ctx, c_ctx, ada_w, ada_b, norm1_g, norm2_g, w_out, mlp_w1, mlp_w2, ev_w_in, ev_q_norm_g, ev_k_norm_g, ev_sgu_norm_g, ev_sgu_w, ev_sgu_b, od_w_in, od_q_norm_g, od_kv_norm_g, od_w_uq, od_w_ukv, od_conv_w, od_conv_b, od_ln_g, od_ln_b, final_g, loss_target, m_c_ctx, m_ada_w, m_ada_b, m_norm1_g, m_norm2_g, m_w_out, m_mlp_w1, m_mlp_w2, m_ev_w_in, m_ev_q_norm_g, m_ev_k_norm_g, m_ev_sgu_norm_g, m_ev_sgu_w, m_ev_sgu_b, m_od_w_in, m_od_q_norm_g, m_od_kv_norm_g, m_od_w_uq, m_od_w_ukv, m_od_conv_w, m_od_conv_b, m_od_ln_g, m_od_ln_b, m_final_g, v_c_ctx, v_ada_w, v_ada_b, v_norm1_g, v_norm2_g, v_w_out, v_mlp_w1, v_mlp_w2, v_ev_w_in, v_ev_q_norm_g, v_ev_k_norm_g, v_ev_sgu_norm_g, v_ev_sgu_w, v_ev_sgu_b, v_od_w_in, v_od_q_norm_g, v_od_kv_norm_g, v_od_w_uq, v_od_w_ukv, v_od_conv_w, v_od_conv_b, v_od_ln_g, v_od_ln_b, v_final_g):
    names = ["c_ctx", "ada_w", "ada_b", "norm1_g", "norm2_g", "w_out", "mlp_w1", "mlp_w2", "ev_w_in", "ev_q_norm_g", "ev_k_norm_g",
             "ev_sgu_norm_g", "ev_sgu_w", "ev_sgu_b", "od_w_in", "od_q_norm_g", "od_kv_norm_g", "od_w_uq", "od_w_ukv", "od_conv_w",
             "od_conv_b", "od_ln_g", "od_ln_b", "final_g"]
    local = dict(zip(names, [c_ctx, ada_w, ada_b, norm1_g, norm2_g, w_out, mlp_w1, mlp_w2, ev_w_in, ev_q_norm_g, ev_k_norm_g, ev_sgu_norm_g, ev_sgu_w, ev_sgu_b, od_w_in, od_q_norm_g, od_kv_norm_g, od_w_uq, od_w_ukv, od_conv_w, od_conv_b, od_ln_g, od_ln_b, final_g]))
    mom1 = dict(zip(names, [m_c_ctx, m_ada_w, m_ada_b, m_norm1_g, m_norm2_g, m_w_out, m_mlp_w1, m_mlp_w2, m_ev_w_in, m_ev_q_norm_g, m_ev_k_norm_g, m_ev_sgu_norm_g, m_ev_sgu_w, m_ev_sgu_b, m_od_w_in, m_od_q_norm_g, m_od_kv_norm_g, m_od_w_uq, m_od_w_ukv, m_od_conv_w, m_od_conv_b, m_od_ln_g, m_od_ln_b, m_final_g]))
    mom2 = dict(zip(names, [v_c_ctx, v_ada_w, v_ada_b, v_norm1_g, v_norm2_g, v_w_out, v_mlp_w1, v_mlp_w2, v_ev_w_in, v_ev_q_norm_g, v_ev_k_norm_g, v_ev_sgu_norm_g, v_ev_sgu_w, v_ev_sgu_b, v_od_w_in, v_od_q_norm_g, v_od_kv_norm_g, v_od_w_uq, v_od_w_ukv, v_od_conv_w, v_od_conv_b, v_od_ln_g, v_od_ln_b, v_final_g]))
    bsz, length, d = x.shape
    ctx_len = ctx.shape[1]
    me = _block_index(_mesh_pos())

    shard = {p: local[wn][layer] for p, wn, layer, _, _ in _PIECES}
    by_columns = {p: axis == 1 for p, _, _, axis, _ in _PIECES}
    stages = {s: [p for p, _, _, _, st in _PIECES if st == s] for s in "abc"}
    block_rows = {p: shard[p].shape[1] if by_columns[p] else shard[p].shape[0] for p in shard}

    def pad_block_rows(t, p):
        extra = -block_rows[p] % ROW_ALIGN
        return jnp.pad(t, [(0, 0)] * (t.ndim - 2) + [(0, extra), (0, 0)]) if extra else t

    def travelling(p):
        t = shard[p].astype(BF16)
        return pad_block_rows(t.T if by_columns[p] else t, p)

    mine = {s: [travelling(p) for p in stages[s]] for s in "abc"}
    tiny_pack = _Packing([(n, local[n].size) for n, _ in _SMALL_SHARDED], 8)

    def unpack_weights(s, gathered):
        return {p: g[:, :block_rows[p]].reshape(N_DEV * block_rows[p], g.shape[2]) for p, g in zip(stages[s], gathered)}

    def pack_grads(s, g):
        return [pad_block_rows(g[p].reshape(N_DEV, block_rows[p], g[p].shape[1]), p) for p in stages[s]]

    cond_local = jnp.concatenate([c, c_ctx[None], jnp.zeros((COND_ROWS - bsz - 1, d), F32)], axis=0)
    cond, gathered_tiny, *gathered_a = _all_gather(
        [cond_local, tiny_pack.pack({n: local[n] for n, _ in _SMALL_SHARDED})] + mine["a"], "gather_inputs")
    cond = cond.reshape(N_DEV * COND_ROWS, d)
    silu_op = _rowwise("silu", _f_silu, [(d, F32)], N_DEV * COND_ROWS)
    silu_rows, silu_pullback = jax.vjp(lambda r: silu_op((r[None],), (), (), (), ())[0][0], cond)
    mod_cols = ada_w.shape[2]
    mod_part = jnp.concatenate([_mm(silu_rows, ada_w[i], "nn", (F32,), f"ada{i}_fwd")[0] for i in range(2)], axis=0)
    (mod_all,) = _all_gather([mod_part], "gather_mod")
    mod_all = mod_all.reshape(N_DEV, 2, N_DEV * COND_ROWS, mod_cols)
    modrows = []
    for i in range(2):
        whole = mod_all[:, i].transpose(1, 0, 2).reshape(N_DEV * COND_ROWS, N_DEV * mod_cols) + ada_b[i]
        modrows.append(lax.dynamic_slice_in_dim(whole, me * COND_ROWS, COND_ROWS, axis=0)[:bsz + 1])

    weights_a = unpack_weights("a", gathered_a)
    gather_b, token_b = _push_start(mine["b"], mod_all, "gather_weights_b_start")
    gather_c, token_c = _push_start(mine["c"], token_b, "gather_weights_c_start")
    full = {n: local[n] for n in _REPLICATED}
    for n, axis in _SMALL_SHARDED:
        full[n] = _unshard(tiny_pack.piece(gathered_tiny, n, (N_DEV,)).reshape((N_DEV,) + local[n].shape), axis)

    xall = jnp.concatenate([ctx, x], axis=1)
    modrows0, modrows1 = modrows[0] + token_c[0, 0], modrows[1]
    w_a = dict(weights_a, norm1_g0=norm1_g[0][None], norm2_g0=norm2_g[0][None],
               **{n: full[n] for n in ("ev_q_norm_g", "ev_k_norm_g", "ev_sgu_norm_g", "ev_sgu_w", "ev_sgu_b")})
    (x1, h0), pull_a = jax.vjp(lambda x_, m0, w: _mixer0(x_, m0, w, bsz, length, ctx_len), xall, modrows0, w_a)
    w_b = dict(unpack_weights("b", _push_wait(gather_b, x1, mine["b"], me, "gather_weights_b_wait")), norm1_g1=norm1_g[1][None])
    (x2, h1), pull_b = jax.vjp(lambda x_, h_, m0, m1, w: _mlp0(x_, h_, m0, m1, w, bsz, length, ctx_len), x1, h0, modrows0, modrows1, w_b)
    w_c = dict(unpack_weights("c", _push_wait(gather_c, x2, mine["c"], me, "gather_weights_c_wait")), norm2_g1=norm2_g[1][None],
               **{n: full[n] for n in ("od_q_norm_g", "od_kv_norm_g", "od_conv_w", "od_conv_b", "od_ln_g", "od_ln_b")})
    x4, pull_c = jax.vjp(lambda x_, h_, m1, w: _layer1(x_, h_, m1, w, bsz, length, ctx_len), x2, h1, modrows1, w_c)
    loss_part, dx4, dfinal = _loss_head(x4, loss_target, final_g[None])
    loss = lax.psum(loss_part[0, 0], MESH_AXES)

    dx2, dh1, dmod1_c, g_c = pull_c(dx4)
    grads_c = pack_grads("c", g_c)
    exchange_c, token = _push_start(grads_c, dx2, "exchange_grads_c_start")
    dx1, dh0, dmod0_b, dmod1_b, g_b = pull_b((dx2, dh1 + token[0, 0].astype(dh1.dtype)))
    grads_b = pack_grads("b", g_b)
    exchange_b, token = _push_start(grads_b, dx1, "exchange_grads_b_start")
    dxall, dmod0_a, g_a = pull_a((dx1, dh0 + token[0, 0].astype(dh0.dtype)))
    grad_x = dxall[:, ctx_len:]
    dmodrows = [dmod0_a + dmod0_b, dmod1_b + dmod1_c]
    grads = {n: g[n] for g in (g_a, g_c) for n in g if n in full}
    grads["norm1_g"] = jnp.concatenate([g_a["norm1_g0"], g_b["norm1_g1"]], axis=0)
    grads["norm2_g"] = jnp.concatenate([g_a["norm2_g0"], g_c["norm2_g1"]], axis=0)
    grads["final_g"] = dfinal[0]

    dmod_local = jnp.concatenate([jnp.pad(dm, ((0, COND_ROWS - bsz - 1), (0, 0))) for dm in dmodrows], axis=0)
    (dmod_all,) = _all_gather([dmod_local], "gather_dmod")
    dmod_all = dmod_all.reshape(N_DEV, 2, COND_ROWS, N_DEV * mod_cols)
    reduced = {}
    grad_ada_w, grad_ada_b, dmod_mine = [], [], []
    for i in range(2):
        dmod = dmod_all[:, i].reshape(N_DEV * COND_ROWS, N_DEV * mod_cols)
        grad_ada_b.append(_sum_rows(dmod, f"ada{i}_db")[0])
        dmod_mine.append(lax.dynamic_slice_in_dim(dmod, me * mod_cols, mod_cols, axis=1))
        grad_ada_w.append(_mm(silu_rows, dmod_mine[i], "tn", (F32,), f"ada{i}_dw")[0])
    reduced["ada_w"], reduced["ada_b"] = jnp.stack(grad_ada_w), jnp.stack(grad_ada_b)
    dsilu = _mm(jnp.concatenate(dmod_mine, axis=1), jnp.concatenate([ada_w[0], ada_w[1]], axis=1), "nt", (F32,), "ada_dx")[0]
    (dcond,) = silu_pullback(dsilu)
    grads["c_ctx"] = _sum_rows(dcond.reshape(N_DEV, COND_ROWS, d)[:, bsz], "c_ctx_rows")[0]

    def own_blocks(blocks):
        return [lax.dynamic_index_in_dim(t, me, 0, keepdims=False) for t in blocks]

    received = {"a": _exchange_blocks(pack_grads("a", g_a), "exchange_grads_a"),
                "b": _push_wait(exchange_b, dmod_all, own_blocks(grads_b), me, "exchange_grads_b_wait"),
                "c": _push_wait(exchange_c, dmod_all, own_blocks(grads_c), me, "exchange_grads_c_wait")}
    piece_grad = {}
    for s in "abc":
        for p, blocks in zip(stages[s], received[s]):
            summed = _sum_blocks(blocks, "sum_grads_" + p)[:block_rows[p]]
            piece_grad[p] = summed.T if by_columns[p] else summed
    for n in ("w_out", "mlp_w1", "mlp_w2"):
        reduced[n] = jnp.stack([piece_grad[p] for p, wn, _, _, _ in _PIECES if wn == n])
    for n in ("ev_w_in", "od_w_in", "od_w_uq", "od_w_ukv"):
        reduced[n] = piece_grad[n][None]

    small_names = list(_REPLICATED) + [n for n, _ in _SMALL_SHARDED]
    small_pack = _Packing([(n, full[n].size) for n in small_names], 8)
    (small_all,) = _all_gather([small_pack.pack({n: grads[n].astype(F32) for n in small_names})], "gather_small_grads")
    small_sum = _sum_blocks(small_all, "sum_small_grads")
    for n in _REPLICATED:
        reduced[n] = small_pack.piece(small_sum, n).reshape(local[n].shape)
    for n, axis in _SMALL_SHARDED:
        whole = small_pack.piece(small_sum, n).reshape(full[n].shape)
        reduced[n] = lax.dynamic_slice_in_dim(whole, me * local[n].shape[axis], local[n].shape[axis], axis=axis)

    delta, new_m, new_v = {}, {}, {}
    for n in names:
        delta[n], new_m[n], new_v[n] = _adamw(local[n], reduced[n], mom1[n], mom2[n], "adamw_" + n)
    return (loss, grad_x, *[reduced[n] for n in names], *[delta[n] for n in names], *[new_m[n] for n in names], *[new_v[n] for n in names])
```

```python
import functools
import math

import jax
import jax.numpy as jnp
from jax import lax
from jax.experimental import pallas as pl
from jax.experimental.pallas import tpu as pltpu

F32, BF16 = jnp.float32, jnp.bfloat16

EPS = 1e-6
GRID_W = 64
ROPE_THETA = 10000.0
A_HEAD_DIM, A_Q_HEADS, A_KV_HEADS = 64, 8, 2
B_GROUPS, B_GROUP_DIM, B_CHUNK = 8, 64, 128
C_HEADS, C_NOPE, C_ROPE, C_V, C_Q_RANK, C_KV_RANK = 8, 64, 32, 64, 256, 128
D_CONV = 31
CONV_PAD = D_CONV // 2
N_MOD = 6
N_DEV = 8
MESH_AXES = ("x", "y", "c")

ADAM_LR, ADAM_B1, ADAM_B2, ADAM_EPS, ADAM_WD, ADAM_STEP = 0.001, 0.9, 0.999, 1e-08, 0.01, 10

VMEM_LIMIT = 56 * 1024 * 1024
PACK_COLS = 1024
ROW_ALIGN = 16
PACK_ROWS = 32
COND_ROWS = 8


def _pcall(body, **kw):
    return pl.pallas_call(body, **kw)


def _params(sem=None):
    return pltpu.CompilerParams(dimension_semantics=sem, vmem_limit_bytes=VMEM_LIMIT)


def _pick(n, cands):
    for c in cands:
        if n % c == 0:
            return c
    return n


def _mm(a, b, mode, out_dtypes, name, epi=None, extras=()):
    if mode == "tn":
        kk, m = a.shape
    else:
        m, kk = a.shape
    n = b.shape[0] if mode == "nt" else b.shape[1]
    tm = _pick(m, (1152, 1024, 896, 768, 512, 256, 128))
    tn = _pick(n, (1024, 896, 768, 512, 256, 128))
    tk = kk if kk <= 1024 else _pick(kk, (1024, 896, 768, 512, 256, 128))
    nk = kk // tk
    ne, no = len(extras), len(out_dtypes)
    a_spec = pl.BlockSpec((tk, tm), lambda i, j, k: (k, i)) if mode == "tn" else pl.BlockSpec((tm, tk), lambda i, j, k: (i, k))
    b_spec = pl.BlockSpec((tn, tk), lambda i, j, k: (j, k)) if mode == "nt" else pl.BlockSpec((tk, tn), lambda i, j, k: (k, j))
    t_spec = pl.BlockSpec((tm, tn), lambda i, j, k: (i, j))
    dn = {"nn": ((1,), (0,)), "nt": ((1,), (1,)), "tn": ((0,), (0,))}[mode]

    def body(a_ref, b_ref, *rest):
        extra_refs, out_refs = rest[:ne], rest[ne:ne + no]

        def finish(acc):
            outs = (acc,) if epi is None else epi(acc, *[r[...] for r in extra_refs])
            for r, o in zip(out_refs, outs):
                r[...] = o.astype(r.dtype)

        part = lax.dot_general(a_ref[...].astype(BF16), b_ref[...].astype(BF16), (dn, ((), ())), preferred_element_type=F32)
        if nk == 1:
            finish(part)
        else:
            acc_ref = rest[-1]
            k = pl.program_id(2)

            @pl.when(k == 0)
            def _():
                acc_ref[...] = part

            @pl.when(k > 0)
            def _():
                acc_ref[...] += part

            @pl.when(k == nk - 1)
            def _():
                finish(acc_ref[...])

    outs = _pcall(
        body, name=name, grid=(m // tm, n // tn, nk),
        in_specs=[a_spec, b_spec] + [t_spec] * ne,
        out_specs=[t_spec] * no,
        out_shape=[jax.ShapeDtypeStruct((m, n), d) for d in out_dtypes],
        scratch_shapes=[pltpu.VMEM((tm, tn), F32)] if nk > 1 else [],
        compiler_params=_params(("parallel", "parallel", "arbitrary")),
    )(a, b, *extras)
    return outs


def _linear(name, transposed=False, out_dtype=F32):
    fwd_mode, dx_mode = ("nt", "nn") if transposed else ("nn", "nt")

    @jax.custom_vjp
    def op(x, w):
        return _mm(x, w, fwd_mode, (out_dtype,), name + "_fwd")[0]

    def fwd(x, w):
        return op(x, w), (x, w)

    def bwd(res, dy):
        x, w = res
        dx = _mm(dy, w, dx_mode, (x.dtype,), name + "_dx")[0]
        dw = _mm(dy, x, "tn", (w.dtype,), name + "_dw")[0] if transposed else _mm(x, dy, "tn", (w.dtype,), name + "_dw")[0]
        return dx, dw

    op.defvjp(fwd, bwd)
    return op


def _relu2_epi(acc):
    return jnp.square(jnp.maximum(acc, 0.0)), acc


def _relu2_bwd_epi(acc, a):
    return (acc * (2.0 * jnp.maximum(a.astype(F32), 0.0)),)


def _mlp(name):
    @jax.custom_vjp
    def op(h, w1t, w2):
        s, _ = _mm(h, w1t, "nt", (BF16, BF16), name + "_up", epi=_relu2_epi)
        return _mm(s, w2, "nn", (F32,), name + "_down")[0]

    def fwd(h, w1t, w2):
        s, a = _mm(h, w1t, "nt", (BF16, BF16), name + "_up", epi=_relu2_epi)
        return _mm(s, w2, "nn", (F32,), name + "_down")[0], (h, w1t, w2, s, a)

    def bwd(res, dy):
        h, w1t, w2, s, a = res
        da = _mm(dy, w2, "nt", (BF16,), name + "_ds", epi=_relu2_bwd_epi, extras=(a,))[0]
        dw2 = _mm(s, dy, "tn", (w2.dtype,), name + "_dw2")[0]
        dw1t = _mm(da, h, "tn", (w1t.dtype,), name + "_dw1")[0]
        dh = _mm(da, w1t, "nn", (h.dtype,), name + "_dh")[0]
        return dh, dw1t, dw2

    op.defvjp(fwd, bwd)
    return op


def _two_pass_dot(x, m):
    hi = x.astype(BF16)
    lo = (x - hi.astype(F32)).astype(BF16)
    mb = m.astype(BF16)
    return jnp.dot(hi, mb, preferred_element_type=F32) + jnp.dot(lo, mb, preferred_element_type=F32)


@jax.custom_vjp
def _sym_dot(x, m):
    return _two_pass_dot(x, m)


def _sym_dot_fwd(x, m):
    return _two_pass_dot(x, m), m


def _sym_dot_bwd(m, g):
    return _two_pass_dot(g, m), jnp.zeros_like(m)


_sym_dot.defvjp(_sym_dot_fwd, _sym_dot_bwd)


def _neighbour(x):
    lane = lax.broadcasted_iota(jnp.int32, x.shape, 1)
    return jnp.where(lane % 2 == 0, pltpu.roll(x, x.shape[1] - 1, 1), pltpu.roll(x, 1, 1))


@jax.custom_vjp
def _swap_pairs(x):
    return _neighbour(x)


_swap_pairs.defvjp(lambda x: (_neighbour(x), None), lambda _, g: (_neighbour(g),))


@jax.custom_vjp
def _bdot(a, b):
    return jnp.dot(a.astype(BF16), b.astype(BF16), preferred_element_type=F32)


def _bdot_fwd(a, b):
    return _bdot(a, b), (a, b)


def _bdot_bwd(res, g):
    a, b = res
    gb = g.astype(BF16)
    da = lax.dot_general(gb, b.astype(BF16), (((1,), (1,)), ((), ())), preferred_element_type=F32)
    db = lax.dot_general(a.astype(BF16), gb, (((0,), (0,)), ((), ())), preferred_element_type=F32)
    return da, db


_bdot.defvjp(_bdot_fwd, _bdot_bwd)


def _rowwise(name, f, out_specs, tl, ctx_blocks=0):
    def seg(l, s):
        return jnp.where(l >= ctx_blocks, s - 1, 0) if s > 1 else 0

    def specs(rows, tabs, pers, glbs, consts):
        row_specs = [pl.BlockSpec((1, tl, r.shape[2]), lambda b, l: (b, l, 0)) for r in rows]
        tab_specs = [pl.BlockSpec((tl, t.shape[1]), lambda b, l: (l, 0)) for t in tabs]
        per_specs = [pl.BlockSpec((1, 1, 1, p.shape[3]), functools.partial(lambda b, l, s: (b, seg(l, s), 0, 0), s=p.shape[1])) for p in pers]
        glb_specs = [pl.BlockSpec(g.shape, functools.partial(lambda b, l, nd: (0,) * nd, nd=g.ndim)) for g in glbs]
        const_specs = [pl.BlockSpec(c.shape, functools.partial(lambda b, l, nd: (0,) * nd, nd=c.ndim)) for c in consts]
        return row_specs, tab_specs, per_specs, glb_specs, const_specs

    def load(refs_rows, refs_tabs, refs_pers, refs_glbs, refs_consts):
        return (tuple(r[0].astype(F32) for r in refs_rows), tuple(t[...] for t in refs_tabs),
                tuple(p[0, 0].astype(F32) for p in refs_pers), tuple(g[...].astype(F32) for g in refs_glbs),
                tuple(c[...] for c in refs_consts))

    def call_fwd(rows, tabs, pers, glbs, consts):
        bsz, length = rows[0].shape[:2]
        nr, nt, npp, ng, nc = len(rows), len(tabs), len(pers), len(glbs), len(consts)
        rs, ts, ps, gs, cs = specs(rows, tabs, pers, glbs, consts)

        def body(*refs):
            ins, outs = refs[:nr + nt + npp + ng + nc], refs[nr + nt + npp + ng + nc:]
            r, t, p, g, c = load(ins[:nr], ins[nr:nr + nt], ins[nr + nt:nr + nt + npp], ins[nr + nt + npp:nr + nt + npp + ng], ins[nr + nt + npp + ng:])
            for o_ref, o in zip(outs, f(r, t, p, g, c)):
                o_ref[0] = o.astype(o_ref.dtype)

        return _pcall(
            body, name=name + "_fwd", grid=(bsz, length // tl),
            in_specs=rs + ts + ps + gs + cs,
            out_specs=[pl.BlockSpec((1, tl, w), lambda b, l: (b, l, 0)) for w, _ in out_specs],
            out_shape=[jax.ShapeDtypeStruct((bsz, length, w), d) for w, d in out_specs],
            compiler_params=_params(("parallel", "parallel")),
        )(*rows, *tabs, *pers, *glbs, *consts)

    def call_bwd(rows, tabs, pers, glbs, consts, cts):
        bsz, length = rows[0].shape[:2]
        nr, nt, npp, ng, nc, no = len(rows), len(tabs), len(pers), len(glbs), len(consts), len(cts)
        rs, ts, ps, gs, cs = specs(rows, tabs, pers, glbs, consts)
        n_in = nr + nt + npp + ng + nc

        def body(*refs):
            ins, ct_refs, outs = refs[:n_in], refs[n_in:n_in + no], refs[n_in + no:]
            r, t, p, g, c = load(ins[:nr], ins[nr:nr + nt], ins[nr + nt:nr + nt + npp], ins[nr + nt + npp:nr + nt + npp + ng], ins[nr + nt + npp + ng:])
            _, vjp = jax.vjp(lambda r_, p_, g_: tuple(f(r_, t, p_, g_, c)), r, p, g)
            dr, dp, dg = vjp(tuple(ct[0].astype(F32) for ct in ct_refs))
            dr_refs, dp_refs, dg_refs = outs[:nr], outs[nr:nr + npp], outs[nr + npp:]
            for ref, d in zip(dr_refs, dr):
                ref[0] = d.astype(ref.dtype)
            b, l = pl.program_id(0), pl.program_id(1)
            first_of_segment = (l == 0) | (l == ctx_blocks)
            for ref, d in zip(dp_refs, dp):
                @pl.when(first_of_segment)
                def _(ref=ref, d=d):
                    ref[0, 0] = d

                @pl.when(jnp.logical_not(first_of_segment))
                def _(ref=ref, d=d):
                    ref[0, 0] += d
            first = (b == 0) & (l == 0)
            for ref, d in zip(dg_refs, dg):
                @pl.when(first)
                def _(ref=ref, d=d):
                    ref[...] = d

                @pl.when(jnp.logical_not(first))
                def _(ref=ref, d=d):
                    ref[...] += d

        ct_specs = [pl.BlockSpec((1, tl, w), lambda b, l: (b, l, 0)) for w, _ in out_specs]
        outs = _pcall(
            body, name=name + "_bwd", grid=(bsz, length // tl),
            in_specs=rs + ts + ps + gs + cs + ct_specs,
            out_specs=rs + ps + gs,
            out_shape=[jax.ShapeDtypeStruct(r.shape, r.dtype) for r in rows]
            + [jax.ShapeDtypeStruct(p.shape, F32) for p in pers] + [jax.ShapeDtypeStruct(g.shape, F32) for g in glbs],
            compiler_params=_params(("arbitrary", "arbitrary")),
        )(*rows, *tabs, *pers, *glbs, *consts, *cts)
        return tuple(outs[:nr]), tuple(outs[nr:nr + npp]), tuple(outs[nr + npp:])

    @jax.custom_vjp
    def op(rows, tabs, pers, glbs, consts):
        return tuple(call_fwd(rows, tabs, pers, glbs, consts))

    def fwd(rows, tabs, pers, glbs, consts):
        return op(rows, tabs, pers, glbs, consts), (rows, tabs, pers, glbs, consts)

    def bwd(res, cts):
        rows, tabs, pers, glbs, consts = res
        dr, dp, dg = call_bwd(rows, tabs, pers, glbs, consts, tuple(cts))
        dp = tuple(d.astype(p.dtype) for d, p in zip(dp, pers))
        dg = tuple(d.astype(g.dtype) for d, g in zip(dg, glbs))
        return dr, tuple(jnp.zeros_like(t) for t in tabs), dp, dg, tuple(jnp.zeros_like(c) for c in consts)

    op.defvjp(fwd, bwd)
    return op


def _rms(x, g):
    return x * lax.rsqrt(jnp.mean(x * x, axis=-1, keepdims=True) + EPS) * g


def _f_silu(r, t, p, g, c):
    return (jax.nn.silu(r[0]),)


def _f_modulate(r, t, p, g, c):
    shift, scale = p
    return (_rms(r[0], g[0]) * (1.0 + scale) + shift,)


def _f_res_modulate(r, t, p, g, c):
    x, y = r
    gate, shift, scale = p
    xn = x + gate * y
    return xn, _rms(xn, g[0]) * (1.0 + scale) + shift


def _f_res(r, t, p, g, c):
    return (r[0] + p[0] * r[1],)


def _f_headnorm_rope(r, t, p, g, c):
    x = r[0]
    cos, sin = t
    xn = x * lax.rsqrt(_sym_dot(x * x, c[0]) + EPS) * g[0]
    return (xn * cos + _swap_pairs(xn) * sin,)


def _f_rope(r, t, p, g, c):
    x = r[0]
    cos, sin = t
    return (x * cos + _swap_pairs(x) * sin,)


def _f_rope_narrow(r, t, p, g, c):
    x = r[0]
    cos, sin = t
    return (x * cos + _sym_dot(x, c[0]) * sin,)


def _f_rms(r, t, p, g, c):
    return (_rms(r[0], g[0]),)


def _f_sgu_pre(r, t, p, g, c):
    u = jax.nn.gelu(r[0])
    v = jax.nn.gelu(r[1])
    vn = v * lax.rsqrt(_sym_dot(v * v, c[0]) + EPS) * g[0]
    return u, vn


def _f_sgu_mix(r, t, p, g, c):
    u, vn = r
    bias = g[B_GROUPS]
    group = lax.broadcasted_iota(jnp.int32, vn.shape, 1) // B_GROUP_DIM
    sv = bias
    for k in range(B_GROUPS):
        sv = sv + jnp.where(group == k, _bdot(g[k], vn), 0.0)
    return (u * sv,)


def _f_glu(r, t, p, g, c):
    return (r[0] * jax.nn.sigmoid(r[1]),)


def _f_ln_silu(r, t, p, g, c):
    x = r[0]
    mu = jnp.mean(x, axis=-1, keepdims=True)
    var = jnp.mean(jnp.square(x - mu), axis=-1, keepdims=True)
    return (jax.nn.silu((x - mu) * lax.rsqrt(var + EPS) * g[0] + g[1]),)


LOG2_E = 1.4426950408889634
LN_2 = 0.6931471805599453


def _attention(name, ctx_len, kv_per_step):
    def tiles(q):
        tq = _pick(q.shape[2], (256, 128))
        if ctx_len:
            tq = math.gcd(tq, ctx_len)
        return tq, ctx_len // tq

    def scores(q_tile, kk, dk):
        qs = (q_tile.astype(F32) * (LOG2_E * dk ** -0.5)).astype(BF16)
        return qs, lax.dot_general(qs, kk, (((1,), (1,)), ((), ())), preferred_element_type=F32)

    def by_segment(qi, cb, lk, run):
        if cb > 0:
            @pl.when(qi < cb)
            def _():
                run(ctx_len)

            @pl.when(qi >= cb)
            def _():
                run(lk)
        else:
            run(lk)

    def call_fwd(q, k, v):
        bsz, hq, lq, dk = q.shape
        _, hk, lk, dv = v.shape
        grp = hq // hk
        tq, cb = tiles(q)
        v_ones = jnp.concatenate([v, jnp.ones_like(v)], axis=-1)

        def body(q_ref, k_ref, v_ref, o_ref, lse_ref):
            def run(nk):
                for j in range(kv_per_step):
                    kk, vv = k_ref[0, j, :nk], v_ref[0, j, :nk]
                    for h in range(j * grp, (j + 1) * grp):
                        _, s = scores(q_ref[0, h], kk, dk)
                        m = jnp.max(s, axis=-1, keepdims=True)
                        acc = jnp.dot(jnp.exp2(s - m).astype(BF16), vv, preferred_element_type=F32)
                        l = acc[:, dv:dv + 1]
                        o_ref[0, h] = (acc[:, :dv] / l).astype(o_ref.dtype)
                        lse_ref[0, h] = m + jnp.log2(l)

            by_segment(pl.program_id(2), cb, lk, run)

        qh = grp * kv_per_step
        return _pcall(
            body, name=name + "_fwd", grid=(bsz, hk // kv_per_step, lq // tq),
            in_specs=[pl.BlockSpec((1, qh, tq, dk), lambda b, h, i: (b, h, i, 0)),
                      pl.BlockSpec((1, kv_per_step, lk, dk), lambda b, h, i: (b, h, 0, 0)),
                      pl.BlockSpec((1, kv_per_step, lk, 2 * dv), lambda b, h, i: (b, h, 0, 0))],
            out_specs=[pl.BlockSpec((1, qh, tq, dv), lambda b, h, i: (b, h, i, 0)), pl.BlockSpec((1, qh, tq, 1), lambda b, h, i: (b, h, i, 0))],
            out_shape=[jax.ShapeDtypeStruct((bsz, hq, lq, dv), BF16), jax.ShapeDtypeStruct((bsz, hq, lq, 1), F32)],
            compiler_params=_params(("parallel", "parallel", "parallel")),
        )(q, k, v_ones)

    def call_bwd(q, k, v, o, lse, do):
        bsz, hq, lq, dk = q.shape
        _, hk, lk, dv = v.shape
        grp = hq // hk
        tq, cb = tiles(q)
        nq = lq // tq

        def body(q_ref, k_ref, v_ref, o_ref, lse_ref, do_ref, dq_ref, dk_ref, dv_ref):
            qi = pl.program_id(2)

            @pl.when(qi == 0)
            def _():
                dk_ref[...] = jnp.zeros_like(dk_ref)
                dv_ref[...] = jnp.zeros_like(dv_ref)

            def run(nk):
                for j in range(kv_per_step):
                    kk, vv = k_ref[0, j, :nk], v_ref[0, j, :nk]
                    dk_sum = dv_sum = None
                    for h in range(j * grp, (j + 1) * grp):
                        dd = do_ref[0, h]
                        qs, s = scores(q_ref[0, h], kk, dk)
                        p = jnp.exp2(s - lse_ref[0, h])
                        delta = jnp.sum(dd.astype(F32) * o_ref[0, h].astype(F32), axis=-1, keepdims=True)
                        dp = lax.dot_general(dd, vv, (((1,), (1,)), ((), ())), preferred_element_type=F32)
                        t = (p * (dp - delta)).astype(BF16)
                        dq_ref[0, h] = (jnp.dot(t, kk, preferred_element_type=F32) * dk ** -0.5).astype(dq_ref.dtype)
                        dk_h = lax.dot_general(t, qs, (((0,), (0,)), ((), ())), preferred_element_type=F32)
                        dv_h = lax.dot_general(p.astype(BF16), dd, (((0,), (0,)), ((), ())), preferred_element_type=F32)
                        dk_sum = dk_h if dk_sum is None else dk_sum + dk_h
                        dv_sum = dv_h if dv_sum is None else dv_sum + dv_h
                    dk_ref[0, j, :nk] += dk_sum
                    dv_ref[0, j, :nk] += dv_sum

            by_segment(qi, cb, lk, run)

            @pl.when(qi == nq - 1)
            def _():
                dk_ref[...] = dk_ref[...] * LN_2

        qh = grp * kv_per_step
        q_spec = pl.BlockSpec((1, qh, tq, dk), lambda b, h, i: (b, h, i, 0))
        k_spec = pl.BlockSpec((1, kv_per_step, lk, dk), lambda b, h, i: (b, h, 0, 0))
        v_spec = pl.BlockSpec((1, kv_per_step, lk, dv), lambda b, h, i: (b, h, 0, 0))
        o_spec = pl.BlockSpec((1, qh, tq, dv), lambda b, h, i: (b, h, i, 0))
        lse_spec = pl.BlockSpec((1, qh, tq, 1), lambda b, h, i: (b, h, i, 0))
        return _pcall(
            body, name=name + "_bwd", grid=(bsz, hk // kv_per_step, nq),
            in_specs=[q_spec, k_spec, v_spec, o_spec, lse_spec, o_spec],
            out_specs=[q_spec, k_spec, v_spec],
            out_shape=[jax.ShapeDtypeStruct(q.shape, BF16), jax.ShapeDtypeStruct(k.shape, F32), jax.ShapeDtypeStruct(v.shape, F32)],
            compiler_params=_params(("parallel", "parallel", "arbitrary")),
        )(q, k, v, o, lse, do)

    @jax.custom_vjp
    def op(q, k, v):
        return call_fwd(q, k, v)[0]

    def fwd(q, k, v):
        o, lse = call_fwd(q, k, v)
        return o, (q, k, v, o, lse)

    def bwd(res, do):
        q, k, v, o, lse = res
        dq, dk, dv = call_bwd(q, k, v, o, lse, do)
        return dq, dk.astype(k.dtype), dv.astype(v.dtype)

    op.defvjp(fwd, bwd)
    return op


def _conv_call(ypad, taps, name):
    bsz, lp, ch = ypad.shape
    length = lp - 2 * ROW_ALIGN
    tl = _pick(length, (256, 128))

    def body(y_ref, w_ref, o_ref):
        base = pl.multiple_of(pl.program_id(1) * tl, tl)
        win = y_ref[0, pl.ds(base, tl + 2 * ROW_ALIGN), :]
        acc = jnp.broadcast_to(w_ref[pl.ds(D_CONV, 1), :], (tl, ch))
        for k in range(D_CONV):
            acc = acc + win[k:k + tl, :] * w_ref[pl.ds(k, 1), :]
        o_ref[0] = acc

    return _pcall(
        body, name=name, grid=(bsz, length // tl),
        in_specs=[pl.BlockSpec((1, lp, ch), lambda b, l: (b, 0, 0)), pl.BlockSpec((D_CONV + 1, ch), lambda b, l: (0, 0))],
        out_specs=pl.BlockSpec((1, tl, ch), lambda b, l: (b, l, 0)),
        out_shape=jax.ShapeDtypeStruct((bsz, length, ch), F32),
        compiler_params=_params(("parallel", "parallel")),
    )(ypad, taps)


def _conv_dw_call(ypad, dout, name):
    bsz, lp, ch = ypad.shape
    length = lp - 2 * ROW_ALIGN
    tl = _pick(length, (256, 128))

    def body(y_ref, d_ref, o_ref):
        b, l = pl.program_id(0), pl.program_id(1)

        @pl.when((b == 0) & (l == 0))
        def _():
            o_ref[...] = jnp.zeros_like(o_ref)

        base = pl.multiple_of(l * tl, tl)
        win = y_ref[0, pl.ds(base, tl + 2 * ROW_ALIGN), :]
        dd = d_ref[0]
        for k in range(D_CONV):
            o_ref[pl.ds(k, 1), :] += jnp.sum(win[k:k + tl, :] * dd, axis=0, keepdims=True)
        o_ref[pl.ds(D_CONV, 1), :] += jnp.sum(dd, axis=0, keepdims=True)

    return _pcall(
        body, name=name, grid=(bsz, length // tl),
        in_specs=[pl.BlockSpec((1, lp, ch), lambda b, l: (b, 0, 0)), pl.BlockSpec((1, tl, ch), lambda b, l: (b, l, 0))],
        out_specs=pl.BlockSpec((D_CONV + 1, ch), lambda b, l: (0, 0)),
        out_shape=jax.ShapeDtypeStruct((D_CONV + 1, ch), F32),
        compiler_params=_params(("arbitrary", "arbitrary")),
    )(ypad, dout)


def _pad_rows(y):
    return jnp.pad(y, ((0, 0), (CONV_PAD, 2 * ROW_ALIGN - CONV_PAD), (0, 0)))


@jax.custom_vjp
def _dwconv(y, taps):
    return _conv_call(_pad_rows(y), taps, "conv_fwd")


def _dwconv_fwd(y, taps):
    return _dwconv(y, taps), (y, taps)


def _dwconv_bwd(res, dout):
    y, taps = res
    flipped = jnp.concatenate([taps[:D_CONV][::-1], jnp.zeros_like(taps[D_CONV:])], axis=0)
    dy = _conv_call(_pad_rows(dout), flipped, "conv_dy")
    dtaps = _conv_dw_call(_pad_rows(y), dout, "conv_dw")
    return dy, dtaps


_dwconv.defvjp(_dwconv_fwd, _dwconv_bwd)


def _loss_head(x, target, g):
    bsz, length, d = x.shape
    tl = _pick(length, (256, 128))

    def f(xb, tb, gb):
        err = _rms(xb, gb) - tb
        return 0.5 * jnp.sum(jnp.sum(err * err, axis=-1, keepdims=True), axis=0, keepdims=True) / d

    def body(x_ref, t_ref, g_ref, loss_ref, dx_ref, dg_ref):
        val, vjp = jax.vjp(lambda xb, gb: f(xb, t_ref[0], gb), x_ref[0], g_ref[...])
        dx, dg = vjp(jnp.ones((1, 1), F32))
        dx_ref[0] = dx
        first = (pl.program_id(0) == 0) & (pl.program_id(1) == 0)

        @pl.when(first)
        def _():
            loss_ref[...] = val
            dg_ref[...] = dg

        @pl.when(jnp.logical_not(first))
        def _():
            loss_ref[...] += val
            dg_ref[...] += dg

    row = pl.BlockSpec((1, tl, d), lambda b, l: (b, l, 0))
    return _pcall(
        body, name="loss_head", grid=(bsz, length // tl),
        in_specs=[row, row, pl.BlockSpec((1, d), lambda b, l: (0, 0))],
        out_specs=[pl.BlockSpec((1, 1), lambda b, l: (0, 0)), row, pl.BlockSpec((1, d), lambda b, l: (0, 0))],
        out_shape=[jax.ShapeDtypeStruct((1, 1), F32), jax.ShapeDtypeStruct(x.shape, F32), jax.ShapeDtypeStruct((1, d), F32)],
        compiler_params=_params(("arbitrary", "arbitrary")),
    )(x, target, g)


def _adamw(w, g, m, v, name):
    shape = w.shape
    cols = shape[-1]
    rows = w.size // cols
    tr = _pick(rows, (512, 256, 128))
    w2, g2, m2, v2 = (t.reshape(rows, cols) for t in (w, g, m, v))

    def body(w_ref, g_ref, m_ref, v_ref, d_ref, nm_ref, nv_ref):
        gg = g_ref[...]
        nm = ADAM_B1 * m_ref[...] + (1.0 - ADAM_B1) * gg
        nv = ADAM_B2 * v_ref[...] + (1.0 - ADAM_B2) * jnp.square(gg)
        m_hat = nm / (1.0 - ADAM_B1 ** ADAM_STEP)
        v_hat = nv / (1.0 - ADAM_B2 ** ADAM_STEP)
        d_ref[...] = -ADAM_LR * (m_hat / (jnp.sqrt(v_hat) + ADAM_EPS) + ADAM_WD * w_ref[...])
        nm_ref[...] = nm
        nv_ref[...] = nv

    spec = pl.BlockSpec((tr, cols), lambda i: (i, 0))
    outs = _pcall(
        body, name=name, grid=(rows // tr,), in_specs=[spec] * 4, out_specs=[spec] * 3,
        out_shape=[jax.ShapeDtypeStruct((rows, cols), F32)] * 3,
        compiler_params=_params(("parallel",)),
    )(w2, g2, m2, v2)
    return tuple(o.reshape(shape) for o in outs)


def _mesh_pos():
    return lax.axis_index("x"), lax.axis_index("y"), lax.axis_index("c")


_RELATIONS = [(dx, dy, dc) for dx in (0, 1) for dy in (0, 1) for dc in (0, 1)][1:]


def _peer(pos, rel):
    return tuple(jnp.where(r == 1, 1 - p, p) if r else p for p, r in zip(pos, rel))


def _block_index(pos):
    return 4 * pos[0] + 2 * pos[1] + pos[2]


_HBM = pl.BlockSpec(memory_space=pltpu.HBM)


def _all_gather(xs, name):
    n = len(xs)

    def body(*refs):
        x_refs, out_refs, (send_sems, recv_sems, local_sems) = refs[:n], refs[n:2 * n], refs[2 * n:]
        x_, y_, c_ = _mesh_pos()
        me, sibling = (x_, y_, c_), (x_, y_, 1 - c_)
        chips = [(1 - x_, y_), (x_, 1 - y_), (1 - x_, 1 - y_)]

        def copy(t, k, block, to, own=False):
            slot = out_refs[t].at[_block_index(block)]
            return pltpu.make_async_remote_copy(
                src_ref=x_refs[t] if own else slot, dst_ref=slot, send_sem=send_sems.at[7 * t + k], recv_sem=recv_sems.at[7 * t + k],
                device_id=to, device_id_type=pl.DeviceIdType.MESH)

        mine = [pltpu.make_async_copy(x_refs[t], out_refs[t].at[_block_index(me)], local_sems.at[t]) for t in range(n)]
        first = [[copy(t, 0, me, sibling, own=True)] + [copy(t, 1 + j, me, (*chip, c_), own=True) for j, chip in enumerate(chips)]
                 for t in range(n)]
        passed = [[copy(t, 4 + j, (*chip, c_), sibling) for j, chip in enumerate(chips)] for t in range(n)]
        for t in range(n):
            mine[t].start()
            for cp in first[t]:
                cp.start()
        for t in range(n):
            for j, chip in enumerate(chips):
                copy(t, 1 + j, (*chip, c_), me).wait_recv()
                passed[t][j].start()
        for t in range(n):
            copy(t, 0, sibling, me).wait_recv()
            for j, chip in enumerate(chips):
                copy(t, 4 + j, (*chip, 1 - c_), me).wait_recv()
            for cp in first[t] + passed[t]:
                cp.wait_send()
            mine[t].wait()

    return _pcall(
        body, name=name, in_specs=[_HBM] * n, out_specs=[_HBM] * n,
        out_shape=[jax.ShapeDtypeStruct((N_DEV,) + x.shape, x.dtype) for x in xs],
        scratch_shapes=[pltpu.SemaphoreType.DMA((7 * n,)), pltpu.SemaphoreType.DMA((7 * n,)), pltpu.SemaphoreType.DMA((n,))],
    )(*xs)


def _exchange_blocks(ps, name):
    n = len(ps)

    def body(*refs):
        p_refs, out_refs, (send_sems, recv_sems, local_sems) = refs[:n], refs[n:2 * n], refs[2 * n:]
        me = _mesh_pos()

        def copy(t, k, rel, arriving=False):
            peer = _peer(me, rel)
            return pltpu.make_async_remote_copy(
                src_ref=p_refs[t].at[_block_index(peer)], dst_ref=out_refs[t].at[_block_index(peer if arriving else me)],
                send_sem=send_sems.at[7 * t + k], recv_sem=recv_sems.at[7 * t + k], device_id=peer, device_id_type=pl.DeviceIdType.MESH)

        mine = [pltpu.make_async_copy(p_refs[t].at[_block_index(me)], out_refs[t].at[_block_index(me)], local_sems.at[t]) for t in range(n)]
        sends = [copy(t, k, rel) for t in range(n) for k, rel in enumerate(_RELATIONS)]
        for cp in mine + sends:
            cp.start()
        for t in range(n):
            for k, rel in enumerate(_RELATIONS):
                copy(t, k, rel, arriving=True).wait_recv()
        for cp in sends:
            cp.wait_send()
        for cp in mine:
            cp.wait()

    return _pcall(
        body, name=name, in_specs=[_HBM] * n, out_specs=[_HBM] * n,
        out_shape=[jax.ShapeDtypeStruct(p.shape, p.dtype) for p in ps],
        scratch_shapes=[pltpu.SemaphoreType.DMA((7 * n,)), pltpu.SemaphoreType.DMA((7 * n,)), pltpu.SemaphoreType.DMA((n,))],
    )(*ps)


_SEM = pl.BlockSpec(memory_space=pltpu.SEMAPHORE)
_EFFECT = pltpu.SideEffectType.DATAFLOW_SIDE_EFFECTING


def _push_start(srcs, after, name):
    n = len(srcs)
    lands = [lax.empty((N_DEV,) + s.shape[-2:], s.dtype) for s in srcs]

    def body(*refs):
        src_refs, land_refs = refs[:n], refs[n:2 * n]
        send_sems, recv_sems, token = refs[2 * n + 1:3 * n + 1], refs[3 * n + 1:4 * n + 1], refs[-1]
        me = _mesh_pos()
        for t in range(n):
            for rel in _RELATIONS:
                peer = _peer(me, rel)
                pltpu.make_async_remote_copy(
                    src_ref=src_refs[t].at[_block_index(peer)] if srcs[t].ndim == 3 else src_refs[t], dst_ref=land_refs[t].at[_block_index(me)],
                    send_sem=send_sems[t], recv_sem=recv_sems[t], device_id=peer, device_id_type=pl.DeviceIdType.MESH).start()
        token[...] = jnp.zeros_like(token)

    outs = _pcall(
        body, name=name,
        out_shape=[pltpu.SemaphoreType.DMA(())] * (2 * n) + [pltpu.HBM(s.shape, s.dtype) for s in srcs]
        + [pltpu.HBM(l.shape, l.dtype) for l in lands] + [jax.ShapeDtypeStruct((8, 128), F32)],
        in_specs=[_HBM] * (2 * n) + [pl.BlockSpec(memory_space=pl.ANY)],
        out_specs=[_SEM] * (2 * n) + [_HBM] * (2 * n) + [pl.BlockSpec(memory_space=pltpu.VMEM)],
        input_output_aliases={i: 2 * n + i for i in range(2 * n)}, compiler_params=pltpu.CompilerParams(has_side_effects=_EFFECT),
    )(*[pltpu.with_memory_space_constraint(t, pltpu.HBM) for t in list(srcs) + lands], after)
    return (outs[:n], outs[n:2 * n], outs[2 * n:3 * n], outs[3 * n:4 * n]), outs[-1]


def _push_wait(handle, after, owns, me, name):
    send_sems, recv_sems, src_thrus, land_thrus = handle
    n = len(land_thrus)

    def body(*refs):
        land_refs, sends, recvs = refs[n:2 * n], refs[2 * n:3 * n], refs[3 * n:4 * n]
        for t in range(n):
            seven = land_refs[t].at[pl.ds(0, N_DEV - 1)]
            all_seven = pltpu.make_async_remote_copy(src_ref=seven, dst_ref=seven, send_sem=sends[t], recv_sem=recvs[t],
                                                     device_id=_mesh_pos(), device_id_type=pl.DeviceIdType.MESH)
            all_seven.wait_send()
            all_seven.wait_recv()

    outs = _pcall(
        body, name=name,
        out_shape=[pltpu.HBM(t.shape, t.dtype) for t in list(src_thrus) + list(land_thrus)],
        in_specs=[_HBM] * (2 * n) + [_SEM] * (2 * n) + [pl.BlockSpec(memory_space=pl.ANY)], out_specs=[_HBM] * (2 * n),
        input_output_aliases={i: i for i in range(2 * n)}, compiler_params=pltpu.CompilerParams(has_side_effects=_EFFECT),
    )(*src_thrus, *land_thrus, *send_sems, *recv_sems, after)
    return [lax.dynamic_update_slice(land, own[None], (me, 0, 0)) for land, own in zip(outs[n:], owns)]


def _sum_blocks(p, name):
    n, rows, cols = p.shape
    tr = _pick(rows, (256, 128, 64, PACK_ROWS, 16, 8))

    def body(p_ref, o_ref):
        acc = p_ref[0].astype(F32)
        for s in range(1, n):
            acc = acc + p_ref[s].astype(F32)
        o_ref[...] = acc

    return _pcall(
        body, name=name, grid=(rows // tr,),
        in_specs=[pl.BlockSpec((n, tr, cols), lambda i: (0, i, 0))], out_specs=pl.BlockSpec((tr, cols), lambda i: (i, 0)),
        out_shape=jax.ShapeDtypeStruct((rows, cols), F32), compiler_params=_params(("parallel",)),
    )(p)


def _sum_rows(t, name):
    def body(t_ref, o_ref):
        o_ref[...] = jnp.sum(t_ref[...], axis=0, keepdims=True)

    return _pcall(body, name=name, out_shape=jax.ShapeDtypeStruct((1, t.shape[1]), F32))(t)


class _Packing:
    def __init__(self, sizes, align, total_align=None):
        self.offsets, self.sizes, self.align = {}, dict(sizes), align
        row = 0
        for name, size in sizes:
            self.offsets[name] = row
            row += -(-size // (align * PACK_COLS)) * align
        total_align = total_align or align
        self.rows = -(-row // total_align) * total_align
        self.tail = self.rows - row

    def pack(self, pieces):
        return self.pack_blocks({n: pieces[n].reshape(1, -1) for n in self.sizes})[0]

    def pack_blocks(self, pieces):
        out = []
        for n, size in self.sizes.items():
            padded = -(-size // (self.align * PACK_COLS)) * self.align * PACK_COLS
            out.append(jnp.pad(pieces[n], ((0, 0), (0, padded - size))).reshape(pieces[n].shape[0], -1, PACK_COLS))
        if self.tail:
            out.append(jnp.zeros((out[0].shape[0], self.tail, PACK_COLS), out[0].dtype))
        return jnp.concatenate(out, axis=1)

    def piece(self, packed, name, lead=()):
        start, size = self.offsets[name], self.sizes[name]
        nrow = -(-size // (self.align * PACK_COLS)) * self.align
        sl = packed[..., start:start + nrow, :]
        return sl.reshape(lead + (nrow * PACK_COLS,))[..., :size]


_PIECES = (("ev_w_in", "ev_w_in", 0, 1, "a"), ("w_out0", "w_out", 0, 0, "a"),
           ("mlp_w1_0", "mlp_w1", 0, 1, "b"), ("mlp_w2_0", "mlp_w2", 0, 0, "b"),
           ("od_w_in", "od_w_in", 0, 1, "c"), ("od_w_uq", "od_w_uq", 0, 1, "c"), ("od_w_ukv", "od_w_ukv", 0, 1, "c"),
           ("w_out1", "w_out", 1, 0, "c"), ("mlp_w1_1", "mlp_w1", 1, 1, "c"), ("mlp_w2_1", "mlp_w2", 1, 0, "c"))
_SMALL_SHARDED = (("od_q_norm_g", 1), ("od_conv_w", 2), ("od_conv_b", 1), ("od_ln_g", 1), ("od_ln_b", 1))
_REPLICATED = ("c_ctx", "norm1_g", "norm2_g", "ev_q_norm_g", "ev_k_norm_g", "ev_sgu_norm_g", "ev_sgu_w", "ev_sgu_b",
               "od_kv_norm_g", "final_g")


def _unshard(blocks, axis):
    moved = jnp.moveaxis(blocks, 0, axis)
    shape = moved.shape
    return moved.reshape(shape[:axis] + (shape[axis] * shape[axis + 1],) + shape[axis + 2:])


def _shard_blocks(full, axis):
    shape = full.shape
    split = full.reshape(shape[:axis] + (N_DEV, shape[axis] // N_DEV) + shape[axis + 1:])
    return jnp.moveaxis(split, axis, 0)


def _group_mean_matrix(width, group):
    idx = jnp.arange(width) // group
    return (idx[:, None] == idx[None, :]).astype(F32) / group


def _swap_matrix(width):
    idx = jnp.arange(width)
    return ((idx[:, None] ^ 1) == idx[None, :]).astype(F32)


def _angles(length, d_rot):
    rows = length // GRID_W
    row = jnp.broadcast_to(jnp.arange(rows)[:, None], (rows, GRID_W)).reshape(-1).astype(F32)
    col = jnp.broadcast_to(jnp.arange(GRID_W)[None, :], (rows, GRID_W)).reshape(-1).astype(F32)
    d_axis = d_rot // 2
    inv = ROPE_THETA ** (-jnp.arange(0, d_axis, 2, dtype=F32) / d_axis)
    return jnp.concatenate([row[:, None] * inv, col[:, None] * inv], axis=-1)


def _rope_tables(length, d_rot, head_dim, heads, ctx_len):
    ang = _angles(length, d_rot)
    cos = jnp.repeat(jnp.cos(ang), 2, axis=1)
    sin = jnp.repeat(jnp.sin(ang), 2, axis=1) * jnp.tile(jnp.array([-1.0, 1.0], F32), d_rot // 2)
    keep = head_dim - d_rot
    cos = jnp.concatenate([jnp.ones((length, keep), F32), cos], axis=1)
    sin = jnp.concatenate([jnp.zeros((length, keep), F32), sin], axis=1)
    cos, sin = jnp.tile(cos, (1, heads)), jnp.tile(sin, (1, heads))
    cos = jnp.concatenate([jnp.ones((ctx_len, cos.shape[1]), F32), cos], axis=0)
    sin = jnp.concatenate([jnp.zeros((ctx_len, sin.shape[1]), F32), sin], axis=0)
    return cos, sin


def _to_heads(t, heads):
    b, l, w = t.shape
    return t.reshape(b, l, heads, w // heads).transpose(0, 2, 1, 3)


def _from_heads(t):
    b, h, l, d = t.shape
    return t.transpose(0, 2, 1, 3).reshape(b, l, h * d)


def _segment_params(mod, bsz):
    parts = jnp.split(mod, N_MOD, axis=-1)
    out = []
    for part in parts:
        lat = part[:bsz]
        ctx = jnp.broadcast_to(part[bsz:bsz + 1], lat.shape)
        out.append(jnp.stack([ctx, lat], axis=1)[:, :, None, :])
    return out


def _flat(t):
    return t.reshape(-1, t.shape[-1])


def _sequence_rowwise(ctx_len):
    tl = math.gcd(256, ctx_len)

    def make(name, f, out_specs, rows_per_block=tl, ctx_blocks=ctx_len // tl):
        return _rowwise(name, f, out_specs, rows_per_block, ctx_blocks)

    return make


def _mixer0(xall, modrows0, w, bsz, length, ctx_len):
    d = xall.shape[-1]
    total = ctx_len + length
    rowwise, flat = _sequence_rowwise(ctx_len), _flat
    sh1, sc1, g1, sh2, sc2, _ = _segment_params(modrows0, bsz)
    (h,) = rowwise("mod0", _f_modulate, [(d, BF16)])((xall,), (), (sh1, sc1), (w["norm1_g0"],), ())
    proj = _linear("ev_in", transposed=True)(flat(h), w["ev_w_in"]).reshape(bsz, total, -1)
    ev_q, ev_kv = A_Q_HEADS * A_HEAD_DIM, A_KV_HEADS * A_HEAD_DIM
    half = B_GROUPS * B_GROUP_DIM
    qp, kp, vp, zu, zv = jnp.split(proj, [ev_q, ev_q + ev_kv, ev_q + 2 * ev_kv, ev_q + 2 * ev_kv + half], axis=-1)
    cos_q, sin_q = _rope_tables(length, A_HEAD_DIM, A_HEAD_DIM, A_Q_HEADS, ctx_len)
    cos_k, sin_k = cos_q[:, :ev_kv], sin_q[:, :ev_kv]
    (q,) = rowwise("ev_q", _f_headnorm_rope, [(ev_q, BF16)])(
        (qp,), (cos_q, sin_q), (), (jnp.tile(w["ev_q_norm_g"][0], A_Q_HEADS)[None],), (_group_mean_matrix(ev_q, A_HEAD_DIM),))
    (k,) = rowwise("ev_k", _f_headnorm_rope, [(ev_kv, BF16)])(
        (kp,), (cos_k, sin_k), (), (jnp.tile(w["ev_k_norm_g"][0], A_KV_HEADS)[None],), (_group_mean_matrix(ev_kv, A_HEAD_DIM),))
    o_att = _attention("gqa", ctx_len, 1)(_to_heads(q, A_Q_HEADS), _to_heads(k, A_KV_HEADS), _to_heads(vp.astype(BF16), A_KV_HEADS))
    u, vn = rowwise("sgu_pre", _f_sgu_pre, [(half, F32), (half, BF16)])(
        (zu, zv), (), (), (w["ev_sgu_norm_g"][0].reshape(1, half),), (_group_mean_matrix(half, B_GROUP_DIM),))
    bias = jnp.repeat(w["ev_sgu_b"][0].T, B_GROUP_DIM, axis=1)
    (o_sgu,) = rowwise("sgu_mix", _f_sgu_mix, [(half, BF16)], rows_per_block=B_CHUNK, ctx_blocks=0)(
        (u, vn), (), (), tuple(w["ev_sgu_w"][0][g] for g in range(B_GROUPS)) + (bias,), ())
    o = jnp.concatenate([_from_heads(o_att), o_sgu], axis=-1)
    y = _linear("out0")(flat(o), w["w_out0"]).reshape(bsz, total, d)
    x1, h = rowwise("res_mod0a", _f_res_modulate, [(d, F32), (d, BF16)])((xall, y), (), (g1, sh2, sc2), (w["norm2_g0"],), ())
    return x1, h


def _mlp0(x1, h, modrows0, modrows1, w, bsz, length, ctx_len):
    d = x1.shape[-1]
    total = ctx_len + length
    g2 = _segment_params(modrows0, bsz)[5]
    sh1, sc1 = _segment_params(modrows1, bsz)[:2]
    y = _mlp("mlp0")(_flat(h), w["mlp_w1_0"], w["mlp_w2_0"]).reshape(bsz, total, d)
    return _sequence_rowwise(ctx_len)("res_mod0b", _f_res_modulate, [(d, F32), (d, BF16)])((x1, y), (), (g2, sh1, sc1), (w["norm1_g1"],), ())


def _layer1(x2, h, modrows1, w, bsz, length, ctx_len):
    d = x2.shape[-1]
    total = ctx_len + length
    half = B_GROUPS * B_GROUP_DIM
    rowwise, flat = _sequence_rowwise(ctx_len), _flat
    _, _, g1n, sh2n, sc2n, g2n = _segment_params(modrows1, bsz)
    proj = _linear("od_in", transposed=True)(flat(h), w["od_w_in"]).reshape(bsz, total, -1)
    cq, ckv, kr, za, zg = jnp.split(proj, [C_Q_RANK, C_Q_RANK + C_KV_RANK, C_Q_RANK + C_KV_RANK + C_ROPE,
                                          C_Q_RANK + C_KV_RANK + C_ROPE + half], axis=-1)
    lat = slice(ctx_len, total)
    lat_tl = math.gcd(256, length)
    (cqn,) = _rowwise("od_qn", _f_rms, [(C_Q_RANK, BF16)], lat_tl)((cq[:, lat],), (), (), (w["od_q_norm_g"],), ())
    c_qk = C_NOPE + C_ROPE
    qf = _linear("od_uq", transposed=True)(flat(cqn), w["od_w_uq"]).reshape(bsz, length, C_HEADS * c_qk)
    cos_q, sin_q = _rope_tables(length, C_ROPE, c_qk, C_HEADS, 0)
    (q,) = _rowwise("od_qrope", _f_rope, [(C_HEADS * c_qk, BF16)], lat_tl)((qf,), (cos_q, sin_q), (), (), ())
    (ckvn,) = rowwise("od_kvn", _f_rms, [(C_KV_RANK, BF16)])((ckv,), (), (), (w["od_kv_norm_g"],), ())
    kv = _linear("od_ukv", transposed=True)(flat(ckvn), w["od_w_ukv"]).reshape(bsz, total, C_HEADS, C_NOPE + C_V)
    cos_r, sin_r = _rope_tables(length, C_ROPE, C_ROPE, 1, ctx_len)
    (krr,) = rowwise("od_krope", _f_rope_narrow, [(C_ROPE, BF16)])((kr,), (cos_r, sin_r), (), (), (_swap_matrix(C_ROPE),))
    kn = kv[..., :C_NOPE].astype(BF16).transpose(0, 2, 1, 3)
    vv = kv[..., C_NOPE:].astype(BF16).transpose(0, 2, 1, 3)
    kfull = jnp.concatenate([kn, jnp.broadcast_to(krr[:, None], (bsz, C_HEADS, total, C_ROPE))], axis=-1)
    o_att = _attention("mla", 0, 2)(_to_heads(q, C_HEADS), kfull, vv)
    (glu,) = _rowwise("glu", _f_glu, [(half, F32)], lat_tl)((za[:, lat], zg[:, lat]), (), (), (), ())
    taps = jnp.concatenate([w["od_conv_w"][0], w["od_conv_b"]], axis=0)
    conv = _dwconv(glu, taps)
    (o_conv,) = _rowwise("ln_silu", _f_ln_silu, [(half, BF16)], lat_tl)((conv,), (), (), (w["od_ln_g"], w["od_ln_b"]), ())
    o = jnp.concatenate([_from_heads(o_att), o_conv], axis=-1)
    y = _linear("out1")(flat(o), w["w_out1"]).reshape(bsz, length, d)
    lat_param = lambda p: p[:, 1:]
    x3, h = _rowwise("res_mod1a", _f_res_modulate, [(d, F32), (d, BF16)], lat_tl)(
        (x2[:, lat], y), (), (lat_param(g1n), lat_param(sh2n), lat_param(sc2n)), (w["norm2_g1"],), ())
    y = _mlp("mlp1")(flat(h), w["mlp_w1_1"], w["mlp_w2_1"]).reshape(bsz, length, d)
    (x4,) = _rowwise("res1b", _f_res, [(d, F32)], lat_tl)((x3, y), (), (lat_param(g2n),), (), ())
    return x4


def kernel(x, c, ctx, c_ctx, ada_w, ada_b, norm1_g, norm2_g, w_out, mlp_w1, mlp_w2, ev_w_in, ev_q_norm_g, ev_k_norm_g, ev_sgu_norm_g, ev_sgu_w, ev_sgu_b, od_w_in, od_q_norm_g, od_kv_norm_g, od_w_uq, od_w_ukv, od_conv_w, od_conv_b, od_ln_g, od_ln_b, final_g, loss_target, m_c_ctx, m_ada_w, m_ada_b, m_norm1_g, m_norm2_g, m_w_out, m_mlp_w1, m_mlp_w2, m_ev_w_in, m_ev_q_norm_g, m_ev_k_norm_g, m_ev_sgu_norm_g, m_ev_sgu_w, m_ev_sgu_b, m_od_w_in, m_od_q_norm_g, m_od_kv_norm_g, m_od_w_uq, m_od_w_ukv, m_od_conv_w, m_od_conv_b, m_od_ln_g, m_od_ln_b, m_final_g, v_c_ctx, v_ada_w, v_ada_b, v_norm1_g, v_norm2_g, v_w_out, v_mlp_w1, v_mlp_w2, v_ev_w_in, v_ev_q_norm_g, v_ev_k_norm_g, v_ev_sgu_norm_g, v_ev_sgu_w, v_ev_sgu_b, v_od_w_in, v_od_q_norm_g, v_od_kv_norm_g, v_od_w_uq, v_od_w_ukv, v_od_conv_w, v_od_conv_b, v_od_ln_g, v_od_ln_b, v_final_g):
    names = ["c_ctx", "ada_w", "ada_b", "norm1_g", "norm2_g", "w_out", "mlp_w1", "mlp_w2", "ev_w_in", "ev_q_norm_g", "ev_k_norm_g",
             "ev_sgu_norm_g", "ev_sgu_w", "ev_sgu_b", "od_w_in", "od_q_norm_g", "od_kv_norm_g", "od_w_uq", "od_w_ukv", "od_conv_w",
             "od_conv_b", "od_ln_g", "od_ln_b", "final_g"]
    local = dict(zip(names, [c_ctx, ada_w, ada_b, norm1_g, norm2_g, w_out, mlp_w1, mlp_w2, ev_w_in, ev_q_norm_g, ev_k_norm_g, ev_sgu_norm_g, ev_sgu_w, ev_sgu_b, od_w_in, od_q_norm_g, od_kv_norm_g, od_w_uq, od_w_ukv, od_conv_w, od_conv_b, od_ln_g, od_ln_b, final_g]))
    mom1 = dict(zip(names, [m_c_ctx, m_ada_w, m_ada_b, m_norm1_g, m_norm2_g, m_w_out, m_mlp_w1, m_mlp_w2, m_ev_w_in, m_ev_q_norm_g, m_ev_k_norm_g, m_ev_sgu_norm_g, m_ev_sgu_w, m_ev_sgu_b, m_od_w_in, m_od_q_norm_g, m_od_kv_norm_g, m_od_w_uq, m_od_w_ukv, m_od_conv_w, m_od_conv_b, m_od_ln_g, m_od_ln_b, m_final_g]))
    mom2 = dict(zip(names, [v_c_ctx, v_ada_w, v_ada_b, v_norm1_g, v_norm2_g, v_w_out, v_mlp_w1, v_mlp_w2, v_ev_w_in, v_ev_q_norm_g, v_ev_k_norm_g, v_ev_sgu_norm_g, v_ev_sgu_w, v_ev_sgu_b, v_od_w_in, v_od_q_norm_g, v_od_kv_norm_g, v_od_w_uq, v_od_w_ukv, v_od_conv_w, v_od_conv_b, v_od_ln_g, v_od_ln_b, v_final_g]))
    bsz, length, d = x.shape
    ctx_len = ctx.shape[1]
    me = _block_index(_mesh_pos())

    shard = {p: local[wn][layer] for p, wn, layer, _, _ in _PIECES}
    by_columns = {p: axis == 1 for p, _, _, axis, _ in _PIECES}
    stages = {s: [p for p, _, _, _, st in _PIECES if st == s] for s in "abc"}
    block_rows = {p: shard[p].shape[1] if by_columns[p] else shard[p].shape[0] for p in shard}

    def pad_block_rows(t, p):
        extra = -block_rows[p] % ROW_ALIGN
        return jnp.pad(t, [(0, 0)] * (t.ndim - 2) + [(0, extra), (0, 0)]) if extra else t

    def travelling(p):
        t = shard[p].astype(BF16)
        return pad_block_rows(t.T if by_columns[p] else t, p)

    mine = {s: [travelling(p) for p in stages[s]] for s in "abc"}
    tiny_pack = _Packing([(n, local[n].size) for n, _ in _SMALL_SHARDED], 8)

    def unpack_weights(s, gathered):
        return {p: g[:, :block_rows[p]].reshape(N_DEV * block_rows[p], g.shape[2]) for p, g in zip(stages[s], gathered)}

    def pack_grads(s, g):
        return [pad_block_rows(g[p].reshape(N_DEV, block_rows[p], g[p].shape[1]), p) for p in stages[s]]

    cond_local = jnp.concatenate([c, c_ctx[None], jnp.zeros((COND_ROWS - bsz - 1, d), F32)], axis=0)
    cond, gathered_tiny, *gathered_a = _all_gather(
        [cond_local, tiny_pack.pack({n: local[n] for n, _ in _SMALL_SHARDED})] + mine["a"], "gather_inputs")
    cond = cond.reshape(N_DEV * COND_ROWS, d)
    silu_op = _rowwise("silu", _f_silu, [(d, F32)], N_DEV * COND_ROWS)
    silu_rows, silu_pullback = jax.vjp(lambda r: silu_op((r[None],), (), (), (), ())[0][0], cond)
    mod_cols = ada_w.shape[2]
    mod_part = jnp.concatenate([_mm(silu_rows, ada_w[i], "nn", (F32,), f"ada{i}_fwd")[0] for i in range(2)], axis=0)
    (mod_all,) = _all_gather([mod_part], "gather_mod")
    mod_all = mod_all.reshape(N_DEV, 2, N_DEV * COND_ROWS, mod_cols)
    modrows = []
    for i in range(2):
        whole = mod_all[:, i].transpose(1, 0, 2).reshape(N_DEV * COND_ROWS, N_DEV * mod_cols) + ada_b[i]
        modrows.append(lax.dynamic_slice_in_dim(whole, me * COND_ROWS, COND_ROWS, axis=0)[:bsz + 1])

    weights_a = unpack_weights("a", gathered_a)
    gather_b, token_b = _push_start(mine["b"], mod_all, "gather_weights_b_start")
    gather_c, token_c = _push_start(mine["c"], token_b, "gather_weights_c_start")
    full = {n: local[n] for n in _REPLICATED}
    for n, axis in _SMALL_SHARDED:
        full[n] = _unshard(tiny_pack.piece(gathered_tiny, n, (N_DEV,)).reshape((N_DEV,) + local[n].shape), axis)

    xall = jnp.concatenate([ctx, x], axis=1)
    modrows0, modrows1 = modrows[0] + token_c[0, 0], modrows[1]
    w_a = dict(weights_a, norm1_g0=norm1_g[0][None], norm2_g0=norm2_g[0][None],
               **{n: full[n] for n in ("ev_q_norm_g", "ev_k_norm_g", "ev_sgu_norm_g", "ev_sgu_w", "ev_sgu_b")})
    (x1, h0), pull_a = jax.vjp(lambda x_, m0, w: _mixer0(x_, m0, w, bsz, length, ctx_len), xall, modrows0, w_a)
    w_b = dict(unpack_weights("b", _push_wait(gather_b, x1, mine["b"], me, "gather_weights_b_wait")), norm1_g1=norm1_g[1][None])
    (x2, h1), pull_b = jax.vjp(lambda x_, h_, m0, m1, w: _mlp0(x_, h_, m0, m1, w, bsz, length, ctx_len), x1, h0, modrows0, modrows1, w_b)
    w_c = dict(unpack_weights("c", _push_wait(gather_c, x2, mine["c"], me, "gather_weights_c_wait")), norm2_g1=norm2_g[1][None],
               **{n: full[n] for n in ("od_q_norm_g", "od_kv_norm_g", "od_conv_w", "od_conv_b", "od_ln_g", "od_ln_b")})
    x4, pull_c = jax.vjp(lambda x_, h_, m1, w: _layer1(x_, h_, m1, w, bsz, length, ctx_len), x2, h1, modrows1, w_c)
    loss_part, dx4, dfinal = _loss_head(x4, loss_target, final_g[None])
    loss = lax.psum(loss_part[0, 0], MESH_AXES)

    dx2, dh1, dmod1_c, g_c = pull_c(dx4)
    grads_c = pack_grads("c", g_c)
    exchange_c, token = _push_start(grads_c, dx2, "exchange_grads_c_start")
    dx1, dh0, dmod0_b, dmod1_b, g_b = pull_b((dx2, dh1 + token[0, 0].astype(dh1.dtype)))
    grads_b = pack_grads("b", g_b)
    exchange_b, token = _push_start(grads_b, dx1, "exchange_grads_b_start")
    dxall, dmod0_a, g_a = pull_a((dx1, dh0 + token[0, 0].astype(dh0.dtype)))
    grad_x = dxall[:, ctx_len:]
    dmodrows = [dmod0_a + dmod0_b, dmod1_b + dmod1_c]
    grads = {n: g[n] for g in (g_a, g_c) for n in g if n in full}
    grads["norm1_g"] = jnp.concatenate([g_a["norm1_g0"], g_b["norm1_g1"]], axis=0)
    grads["norm2_g"] = jnp.concatenate([g_a["norm2_g0"], g_c["norm2_g1"]], axis=0)
    grads["final_g"] = dfinal[0]

    dmod_local = jnp.concatenate([jnp.pad(dm, ((0, COND_ROWS - bsz - 1), (0, 0))) for dm in dmodrows], axis=0)
    (dmod_all,) = _all_gather([dmod_local], "gather_dmod")
    dmod_all = dmod_all.reshape(N_DEV, 2, COND_ROWS, N_DEV * mod_cols)
    reduced = {}
    grad_ada_w, grad_ada_b, dmod_mine = [], [], []
    for i in range(2):
        dmod = dmod_all[:, i].reshape(N_DEV * COND_ROWS, N_DEV * mod_cols)
        grad_ada_b.append(_sum_rows(dmod, f"ada{i}_db")[0])
        dmod_mine.append(lax.dynamic_slice_in_dim(dmod, me * mod_cols, mod_cols, axis=1))
        grad_ada_w.append(_mm(silu_rows, dmod_mine[i], "tn", (F32,), f"ada{i}_dw")[0])
    reduced["ada_w"], reduced["ada_b"] = jnp.stack(grad_ada_w), jnp.stack(grad_ada_b)
    dsilu = _mm(jnp.concatenate(dmod_mine, axis=1), jnp.concatenate([ada_w[0], ada_w[1]], axis=1), "nt", (F32,), "ada_dx")[0]
    (dcond,) = silu_pullback(dsilu)
    grads["c_ctx"] = _sum_rows(dcond.reshape(N_DEV, COND_ROWS, d)[:, bsz], "c_ctx_rows")[0]

    def own_blocks(blocks):
        return [lax.dynamic_index_in_dim(t, me, 0, keepdims=False) for t in blocks]

    received = {"a": _exchange_blocks(pack_grads("a", g_a), "exchange_grads_a"),
                "b": _push_wait(exchange_b, dmod_all, own_blocks(grads_b), me, "exchange_grads_b_wait"),
                "c": _push_wait(exchange_c, dmod_all, own_blocks(grads_c), me, "exchange_grads_c_wait")}
    piece_grad = {}
    for s in "abc":
        for p, blocks in zip(stages[s], received[s]):
            summed = _sum_blocks(blocks, "sum_grads_" + p)[:block_rows[p]]
            piece_grad[p] = summed.T if by_columns[p] else summed
    for n in ("w_out", "mlp_w1", "mlp_w2"):
        reduced[n] = jnp.stack([piece_grad[p] for p, wn, _, _, _ in _PIECES if wn == n])
    for n in ("ev_w_in", "od_w_in", "od_w_uq", "od_w_ukv"):
        reduced[n] = piece_grad[n][None]

    small_names = list(_REPLICATED) + [n for n, _ in _SMALL_SHARDED]
    small_pack = _Packing([(n, full[n].size) for n in small_names], 8)
    (small_all,) = _all_gather([small_pack.pack({n: grads[n].astype(F32) for n in small_names})], "gather_small_grads")
    small_sum = _sum_blocks(small_all, "sum_small_grads")
    for n in _REPLICATED:
        reduced[n] = small_pack.piece(small_sum, n).reshape(local[n].shape)
    for n, axis in _SMALL_SHARDED:
        whole = small_pack.piece(small_sum, n).reshape(full[n].shape)
        reduced[n] = lax.dynamic_slice_in_dim(whole, me * local[n].shape[axis], local[n].shape[axis], axis=axis)

    delta, new_m, new_v = {}, {}, {}
    for n in names:
        delta[n], new_m[n], new_v[n] = _adamw(local[n], reduced[n], mom1[n], mom2[n], "adamw_" + n)
    return (loss, grad_x, *[reduced[n] for n in names], *[delta[n] for n in names], *[new_m[n] for n in names], *[new_v[n] for n in names])
```

```python
import functools
import math

import jax
import jax.numpy as jnp
from jax import lax
from jax.experimental import pallas as pl
from jax.experimental.pallas import tpu as pltpu

F32, BF16 = jnp.float32, jnp.bfloat16

EPS = 1e-6
GRID_W = 64
ROPE_THETA = 10000.0
A_HEAD_DIM, A_Q_HEADS, A_KV_HEADS = 64, 8, 2
B_GROUPS, B_GROUP_DIM, B_CHUNK = 8, 64, 128
C_HEADS, C_NOPE, C_ROPE, C_V, C_Q_RANK, C_KV_RANK = 8, 64, 32, 64, 256, 128
D_CONV = 31
CONV_PAD = D_CONV // 2
N_MOD = 6
N_DEV = 8
MESH_AXES = ("x", "y", "c")

ADAM_LR, ADAM_B1, ADAM_B2, ADAM_EPS, ADAM_WD, ADAM_STEP = 0.001, 0.9, 0.999, 1e-08, 0.01, 10

VMEM_LIMIT = 56 * 1024 * 1024
PACK_COLS = 1024
ROW_ALIGN = 16
PACK_ROWS = 32
COND_ROWS = 8


def _pcall(body, **kw):
    return pl.pallas_call(body, **kw)


def _params(sem=None):
    return pltpu.CompilerParams(dimension_semantics=sem, vmem_limit_bytes=VMEM_LIMIT)


def _pick(n, cands):
    for c in cands:
        if n % c == 0:
            return c
    return n


def _mm(a, b, mode, out_dtypes, name, epi=None, extras=()):
    if mode == "tn":
        kk, m = a.shape
    else:
        m, kk = a.shape
    n = b.shape[0] if mode == "nt" else b.shape[1]
    tm = _pick(m, (1152, 1024, 896, 768, 512, 256, 128))
    tn = _pick(n, (1024, 896, 768, 512, 256, 128))
    tk = kk if kk <= 1024 else _pick(kk, (1024, 896, 768, 512, 256, 128))
    nk = kk // tk
    ne, no = len(extras), len(out_dtypes)
    a_spec = pl.BlockSpec((tk, tm), lambda i, j, k: (k, i)) if mode == "tn" else pl.BlockSpec((tm, tk), lambda i, j, k: (i, k))
    b_spec = pl.BlockSpec((tn, tk), lambda i, j, k: (j, k)) if mode == "nt" else pl.BlockSpec((tk, tn), lambda i, j, k: (k, j))
    t_spec = pl.BlockSpec((tm, tn), lambda i, j, k: (i, j))
    dn = {"nn": ((1,), (0,)), "nt": ((1,), (1,)), "tn": ((0,), (0,))}[mode]

    def body(a_ref, b_ref, *rest):
        extra_refs, out_refs = rest[:ne], rest[ne:ne + no]

        def finish(acc):
            outs = (acc,) if epi is None else epi(acc, *[r[...] for r in extra_refs])
            for r, o in zip(out_refs, outs):
                r[...] = o.astype(r.dtype)

        part = lax.dot_general(a_ref[...].astype(BF16), b_ref[...].astype(BF16), (dn, ((), ())), preferred_element_type=F32)
        if nk == 1:
            finish(part)
        else:
            acc_ref = rest[-1]
            k = pl.program_id(2)

            @pl.when(k == 0)
            def _():
                acc_ref[...] = part

            @pl.when(k > 0)
            def _():
                acc_ref[...] += part

            @pl.when(k == nk - 1)
            def _():
                finish(acc_ref[...])

    outs = _pcall(
        body, name=name, grid=(m // tm, n // tn, nk),
        in_specs=[a_spec, b_spec] + [t_spec] * ne,
        out_specs=[t_spec] * no,
        out_shape=[jax.ShapeDtypeStruct((m, n), d) for d in out_dtypes],
        scratch_shapes=[pltpu.VMEM((tm, tn), F32)] if nk > 1 else [],
        compiler_params=_params(("parallel", "parallel", "arbitrary")),
    )(a, b, *extras)
    return outs


def _linear(name, transposed=False, out_dtype=F32):
    fwd_mode, dx_mode = ("nt", "nn") if transposed else ("nn", "nt")

    @jax.custom_vjp
    def op(x, w):
        return _mm(x, w, fwd_mode, (out_dtype,), name + "_fwd")[0]

    def fwd(x, w):
        return op(x, w), (x, w)

    def bwd(res, dy):
        x, w = res
        dx = _mm(dy, w, dx_mode, (x.dtype,), name + "_dx")[0]
        dw = _mm(dy, x, "tn", (w.dtype,), name + "_dw")[0] if transposed else _mm(x, dy, "tn", (w.dtype,), name + "_dw")[0]
        return dx, dw

    op.defvjp(fwd, bwd)
    return op


_NT, _NN, _TN = (((1,), (1,)), ((), ())), (((1,), (0,)), ((), ())), (((0,), (0,)), ((), ()))
_ROW_TILES = (1152, 1024, 768, 512, 256, 128)


def _whole(w):
    return pl.BlockSpec(w.shape, lambda i: (0, 0))


def _groups_apply(x, ws, out_dtypes, name):
    n, (m, kk) = len(ws), x.shape
    tm = _pick(m, _ROW_TILES)

    def body(x_ref, *refs):
        a = x_ref[...].astype(BF16)
        for w_ref, o_ref in zip(refs[:n], refs[n:]):
            o_ref[...] = lax.dot_general(a, w_ref[...], _NT, preferred_element_type=F32).astype(o_ref.dtype)

    return _pcall(
        body, name=name, grid=(m // tm,),
        in_specs=[pl.BlockSpec((tm, kk), lambda i: (i, 0))] + [_whole(w) for w in ws],
        out_specs=[pl.BlockSpec((tm, w.shape[0]), lambda i: (i, 0)) for w in ws],
        out_shape=[jax.ShapeDtypeStruct((m, w.shape[0]), dt) for w, dt in zip(ws, out_dtypes)],
        compiler_params=_params(("parallel",)),
    )(x, *ws)


def _groups_sum(xs, ws, out_dtype, name):
    n, m, kk = len(ws), xs[0].shape[0], ws[0].shape[1]
    tm = _pick(m, _ROW_TILES)

    def body(*refs):
        acc = None
        for x_ref, w_ref in zip(refs[:n], refs[n:2 * n]):
            part = lax.dot_general(x_ref[...].astype(BF16), w_ref[...], _NN, preferred_element_type=F32)
            acc = part if acc is None else acc + part
        refs[2 * n][...] = acc.astype(out_dtype)

    return _pcall(
        body, name=name, grid=(m // tm,),
        in_specs=[pl.BlockSpec((tm, w.shape[0]), lambda i: (i, 0)) for w in ws] + [_whole(w) for w in ws],
        out_specs=pl.BlockSpec((tm, kk), lambda i: (i, 0)), out_shape=jax.ShapeDtypeStruct((m, kk), out_dtype),
        compiler_params=_params(("parallel",)),
    )(*xs, *ws)


def _groups_outer(xs, y, ws, name):
    n, (m, kk) = len(ws), y.shape
    tk = _pick(m, (768, 512, 256, 128))
    steps = m // tk

    def body(y_ref, *refs):
        x_refs, o_refs, acc_refs = refs[:n], refs[n:2 * n], refs[2 * n:]
        k = pl.program_id(0)
        b = y_ref[...].astype(BF16)
        for x_ref, o_ref, acc_ref in zip(x_refs, o_refs, acc_refs):
            part = lax.dot_general(x_ref[...].astype(BF16), b, _TN, preferred_element_type=F32)

            @pl.when(k == 0)
            def _(acc_ref=acc_ref, part=part):
                acc_ref[...] = part

            @pl.when(k > 0)
            def _(acc_ref=acc_ref, part=part):
                acc_ref[...] += part

            @pl.when(k == steps - 1)
            def _(acc_ref=acc_ref, o_ref=o_ref):
                o_ref[...] = acc_ref[...].astype(o_ref.dtype)

    return _pcall(
        body, name=name, grid=(steps,),
        in_specs=[pl.BlockSpec((tk, kk), lambda k: (k, 0))] + [pl.BlockSpec((tk, w.shape[0]), lambda k: (k, 0)) for w in ws],
        out_specs=[_whole(w) for w in ws], out_shape=[jax.ShapeDtypeStruct(w.shape, w.dtype) for w in ws],
        scratch_shapes=[pltpu.VMEM(w.shape, F32) for w in ws], compiler_params=_params(("arbitrary",)),
    )(y, *xs)


def _linear_multi(name, out_dtypes):
    @jax.custom_vjp
    def op(x, wts):
        return tuple(_groups_apply(x, wts, out_dtypes, name + "_fwd"))

    def fwd(x, wts):
        return op(x, wts), (x, wts)

    def bwd(res, dys):
        x, wts = res
        return _groups_sum(dys, wts, x.dtype, name + "_dx"), tuple(_groups_outer(dys, x, wts, name + "_dw"))

    op.defvjp(fwd, bwd)
    return op


def _linear_sum(name):
    @jax.custom_vjp
    def op(xs, ws):
        return _groups_sum(xs, ws, F32, name + "_fwd")

    def fwd(xs, ws):
        return op(xs, ws), (xs, ws)

    def bwd(res, dy):
        xs, ws = res
        return tuple(_groups_apply(dy, ws, [x.dtype for x in xs], name + "_dx")), tuple(_groups_outer(xs, dy, ws, name + "_dw"))

    op.defvjp(fwd, bwd)
    return op


def _relu2_epi(acc):
    return jnp.square(jnp.maximum(acc, 0.0)), acc


def _relu2_bwd_epi(acc, a):
    return (acc * (2.0 * jnp.maximum(a.astype(F32), 0.0)),)


def _mlp(name):
    @jax.custom_vjp
    def op(h, w1t, w2):
        s, _ = _mm(h, w1t, "nt", (BF16, BF16), name + "_up", epi=_relu2_epi)
        return _mm(s, w2, "nn", (F32,), name + "_down")[0]

    def fwd(h, w1t, w2):
        s, a = _mm(h, w1t, "nt", (BF16, BF16), name + "_up", epi=_relu2_epi)
        return _mm(s, w2, "nn", (F32,), name + "_down")[0], (h, w1t, w2, s, a)

    def bwd(res, dy):
        h, w1t, w2, s, a = res
        da = _mm(dy, w2, "nt", (BF16,), name + "_ds", epi=_relu2_bwd_epi, extras=(a,))[0]
        dw2 = _mm(s, dy, "tn", (w2.dtype,), name + "_dw2")[0]
        dw1t = _mm(da, h, "tn", (w1t.dtype,), name + "_dw1")[0]
        dh = _mm(da, w1t, "nn", (h.dtype,), name + "_dh")[0]
        return dh, dw1t, dw2

    op.defvjp(fwd, bwd)
    return op


def _two_pass_dot(x, m):
    hi = x.astype(BF16)
    lo = (x - hi.astype(F32)).astype(BF16)
    mb = m.astype(BF16)
    return jnp.dot(hi, mb, preferred_element_type=F32) + jnp.dot(lo, mb, preferred_element_type=F32)


@jax.custom_vjp
def _sym_dot(x, m):
    return _two_pass_dot(x, m)


def _sym_dot_fwd(x, m):
    return _two_pass_dot(x, m), m


def _sym_dot_bwd(m, g):
    return _two_pass_dot(g, m), jnp.zeros_like(m)


_sym_dot.defvjp(_sym_dot_fwd, _sym_dot_bwd)


def _neighbour(x):
    lane = lax.broadcasted_iota(jnp.int32, x.shape, 1)
    return jnp.where(lane % 2 == 0, pltpu.roll(x, x.shape[1] - 1, 1), pltpu.roll(x, 1, 1))


@jax.custom_vjp
def _swap_pairs(x):
    return _neighbour(x)


_swap_pairs.defvjp(lambda x: (_neighbour(x), None), lambda _, g: (_neighbour(g),))


@jax.custom_vjp
def _bdot(a, b):
    return jnp.dot(a.astype(BF16), b.astype(BF16), preferred_element_type=F32)


def _bdot_fwd(a, b):
    return _bdot(a, b), (a, b)


def _bdot_bwd(res, g):
    a, b = res
    gb = g.astype(BF16)
    da = lax.dot_general(gb, b.astype(BF16), (((1,), (1,)), ((), ())), preferred_element_type=F32)
    db = lax.dot_general(a.astype(BF16), gb, (((0,), (0,)), ((), ())), preferred_element_type=F32)
    return da, db


_bdot.defvjp(_bdot_fwd, _bdot_bwd)


def _rowwise(name, f, out_specs, tl, ctx_blocks=0):
    def seg(l, s):
        return jnp.where(l >= ctx_blocks, s - 1, 0) if s > 1 else 0

    def specs(rows, tabs, pers, glbs, consts):
        row_specs = [pl.BlockSpec((1, tl, r.shape[2]), lambda b, l: (b, l, 0)) for r in rows]
        tab_specs = [pl.BlockSpec((tl, t.shape[1]), lambda b, l: (l, 0)) for t in tabs]
        per_specs = [pl.BlockSpec((1, 1, 1, p.shape[3]), functools.partial(lambda b, l, s: (b, seg(l, s), 0, 0), s=p.shape[1])) for p in pers]
        glb_specs = [pl.BlockSpec(g.shape, functools.partial(lambda b, l, nd: (0,) * nd, nd=g.ndim)) for g in glbs]
        const_specs = [pl.BlockSpec(c.shape, functools.partial(lambda b, l, nd: (0,) * nd, nd=c.ndim)) for c in consts]
        return row_specs, tab_specs, per_specs, glb_specs, const_specs

    def load(refs_rows, refs_tabs, refs_pers, refs_glbs, refs_consts):
        return (tuple(r[0].astype(F32) for r in refs_rows), tuple(t[...] for t in refs_tabs),
                tuple(p[0, 0].astype(F32) for p in refs_pers), tuple(g[...].astype(F32) for g in refs_glbs),
                tuple(c[...] for c in refs_consts))

    def call_fwd(rows, tabs, pers, glbs, consts):
        bsz, length = rows[0].shape[:2]
        nr, nt, npp, ng, nc = len(rows), len(tabs), len(pers), len(glbs), len(consts)
        rs, ts, ps, gs, cs = specs(rows, tabs, pers, glbs, consts)

        def body(*refs):
            ins, outs = refs[:nr + nt + npp + ng + nc], refs[nr + nt + npp + ng + nc:]
            r, t, p, g, c = load(ins[:nr], ins[nr:nr + nt], ins[nr + nt:nr + nt + npp], ins[nr + nt + npp:nr + nt + npp + ng], ins[nr + nt + npp + ng:])
            for o_ref, o in zip(outs, f(r, t, p, g, c)):
                o_ref[0] = o.astype(o_ref.dtype)

        return _pcall(
            body, name=name + "_fwd", grid=(bsz, length // tl),
            in_specs=rs + ts + ps + gs + cs,
            out_specs=[pl.BlockSpec((1, tl, w), lambda b, l: (b, l, 0)) for w, _ in out_specs],
            out_shape=[jax.ShapeDtypeStruct((bsz, length, w), d) for w, d in out_specs],
            compiler_params=_params(("parallel", "parallel")),
        )(*rows, *tabs, *pers, *glbs, *consts)

    def call_bwd(rows, tabs, pers, glbs, consts, cts):
        bsz, length = rows[0].shape[:2]
        nr, nt, npp, ng, nc, no = len(rows), len(tabs), len(pers), len(glbs), len(consts), len(cts)
        rs, ts, ps, gs, cs = specs(rows, tabs, pers, glbs, consts)
        n_in = nr + nt + npp + ng + nc

        def body(*refs):
            ins, ct_refs, outs = refs[:n_in], refs[n_in:n_in + no], refs[n_in + no:]
            r, t, p, g, c = load(ins[:nr], ins[nr:nr + nt], ins[nr + nt:nr + nt + npp], ins[nr + nt + npp:nr + nt + npp + ng], ins[nr + nt + npp + ng:])
            _, vjp = jax.vjp(lambda r_, p_, g_: tuple(f(r_, t, p_, g_, c)), r, p, g)
            dr, dp, dg = vjp(tuple(ct[0].astype(F32) for ct in ct_refs))
            dr_refs, dp_refs, dg_refs = outs[:nr], outs[nr:nr + npp], outs[nr + npp:]
            for ref, d in zip(dr_refs, dr):
                ref[0] = d.astype(ref.dtype)
            b, l = pl.program_id(0), pl.program_id(1)
            first_of_segment = (l == 0) | (l == ctx_blocks)
            for ref, d in zip(dp_refs, dp):
                @pl.when(first_of_segment)
                def _(ref=ref, d=d):
                    ref[0, 0] = d

                @pl.when(jnp.logical_not(first_of_segment))
                def _(ref=ref, d=d):
                    ref[0, 0] += d
            first = (b == 0) & (l == 0)
            for ref, d in zip(dg_refs, dg):
                @pl.when(first)
                def _(ref=ref, d=d):
                    ref[...] = d

                @pl.when(jnp.logical_not(first))
                def _(ref=ref, d=d):
                    ref[...] += d

        ct_specs = [pl.BlockSpec((1, tl, w), lambda b, l: (b, l, 0)) for w, _ in out_specs]
        outs = _pcall(
            body, name=name + "_bwd", grid=(bsz, length // tl),
            in_specs=rs + ts + ps + gs + cs + ct_specs,
            out_specs=rs + ps + gs,
            out_shape=[jax.ShapeDtypeStruct(r.shape, r.dtype) for r in rows]
            + [jax.ShapeDtypeStruct(p.shape, F32) for p in pers] + [jax.ShapeDtypeStruct(g.shape, F32) for g in glbs],
            compiler_params=_params(("arbitrary", "arbitrary")),
        )(*rows, *tabs, *pers, *glbs, *consts, *cts)
        return tuple(outs[:nr]), tuple(outs[nr:nr + npp]), tuple(outs[nr + npp:])

    @jax.custom_vjp
    def op(rows, tabs, pers, glbs, consts):
        return tuple(call_fwd(rows, tabs, pers, glbs, consts))

    def fwd(rows, tabs, pers, glbs, consts):
        return op(rows, tabs, pers, glbs, consts), (rows, tabs, pers, glbs, consts)

    def bwd(res, cts):
        rows, tabs, pers, glbs, consts = res
        dr, dp, dg = call_bwd(rows, tabs, pers, glbs, consts, tuple(cts))
        dp = tuple(d.astype(p.dtype) for d, p in zip(dp, pers))
        dg = tuple(d.astype(g.dtype) for d, g in zip(dg, glbs))
        return dr, tuple(jnp.zeros_like(t) for t in tabs), dp, dg, tuple(jnp.zeros_like(c) for c in consts)

    op.defvjp(fwd, bwd)
    return op


def _rms(x, g):
    return x * lax.rsqrt(jnp.mean(x * x, axis=-1, keepdims=True) + EPS) * g


def _f_silu(r, t, p, g, c):
    return (jax.nn.silu(r[0]),)


def _f_modulate(r, t, p, g, c):
    shift, scale = p
    return (_rms(r[0], g[0]) * (1.0 + scale) + shift,)


def _f_res_modulate(r, t, p, g, c):
    x, y = r
    gate, shift, scale = p
    xn = x + gate * y
    return xn, _rms(xn, g[0]) * (1.0 + scale) + shift


def _f_res(r, t, p, g, c):
    return (r[0] + p[0] * r[1],)


def _f_headnorm_rope(r, t, p, g, c):
    x = r[0]
    cos, sin = t
    xn = x * lax.rsqrt(_sym_dot(x * x, c[0]) + EPS) * g[0]
    return (xn * cos + _swap_pairs(xn) * sin,)


def _f_rope(r, t, p, g, c):
    x = r[0]
    cos, sin = t
    return (x * cos + _swap_pairs(x) * sin,)


def _f_rope_narrow(r, t, p, g, c):
    x = r[0]
    cos, sin = t
    return (x * cos + _sym_dot(x, c[0]) * sin,)


def _f_rms(r, t, p, g, c):
    return (_rms(r[0], g[0]),)


def _f_sgu_pre(r, t, p, g, c):
    u = jax.nn.gelu(r[0])
    v = jax.nn.gelu(r[1])
    vn = v * lax.rsqrt(_sym_dot(v * v, c[0]) + EPS) * g[0]
    return u, vn


def _f_sgu_mix(r, t, p, g, c):
    u, vn = r
    bias = g[B_GROUPS]
    group = lax.broadcasted_iota(jnp.int32, vn.shape, 1) // B_GROUP_DIM
    sv = bias
    for k in range(B_GROUPS):
        sv = sv + jnp.where(group == k, _bdot(g[k], vn), 0.0)
    return (u * sv,)


def _f_glu(r, t, p, g, c):
    return (r[0] * jax.nn.sigmoid(r[1]),)


def _f_ln_silu(r, t, p, g, c):
    x = r[0]
    mu = jnp.mean(x, axis=-1, keepdims=True)
    var = jnp.mean(jnp.square(x - mu), axis=-1, keepdims=True)
    return (jax.nn.silu((x - mu) * lax.rsqrt(var + EPS) * g[0] + g[1]),)


LOG2_E = 1.4426950408889634
LN_2 = 0.6931471805599453


def _attention(name, ctx_len, kv_per_step):
    def tiles(q):
        tq = _pick(q.shape[2], (256, 128))
        if ctx_len:
            tq = math.gcd(tq, ctx_len)
        return tq, ctx_len // tq

    def scores(q_tile, kk, dk):
        qs = (q_tile.astype(F32) * (LOG2_E * dk ** -0.5)).astype(BF16)
        return qs, lax.dot_general(qs, kk, (((1,), (1,)), ((), ())), preferred_element_type=F32)

    def by_segment(qi, cb, lk, run):
        if cb > 0:
            @pl.when(qi < cb)
            def _():
                run(ctx_len)

            @pl.when(qi >= cb)
            def _():
                run(lk)
        else:
            run(lk)

    def call_fwd(q, k, v):
        bsz, hq, lq, dk = q.shape
        _, hk, lk, dv = v.shape
        grp = hq // hk
        tq, cb = tiles(q)
        v_ones = jnp.concatenate([v, jnp.ones_like(v)], axis=-1)

        def body(q_ref, k_ref, v_ref, o_ref, lse_ref):
            def run(nk):
                for j in range(kv_per_step):
                    kk, vv = k_ref[0, j, :nk], v_ref[0, j, :nk]
                    for h in range(j * grp, (j + 1) * grp):
                        _, s = scores(q_ref[0, h], kk, dk)
                        m = jnp.max(s, axis=-1, keepdims=True)
                        acc = jnp.dot(jnp.exp2(s - m).astype(BF16), vv, preferred_element_type=F32)
                        l = acc[:, dv:dv + 1]
                        o_ref[0, h] = (acc[:, :dv] / l).astype(o_ref.dtype)
                        lse_ref[0, h] = m + jnp.log2(l)

            by_segment(pl.program_id(2), cb, lk, run)

        qh = grp * kv_per_step
        return _pcall(
            body, name=name + "_fwd", grid=(bsz, hk // kv_per_step, lq // tq),
            in_specs=[pl.BlockSpec((1, qh, tq, dk), lambda b, h, i: (b, h, i, 0)),
                      pl.BlockSpec((1, kv_per_step, lk, dk), lambda b, h, i: (b, h, 0, 0)),
                      pl.BlockSpec((1, kv_per_step, lk, 2 * dv), lambda b, h, i: (b, h, 0, 0))],
            out_specs=[pl.BlockSpec((1, qh, tq, dv), lambda b, h, i: (b, h, i, 0)), pl.BlockSpec((1, qh, tq, 1), lambda b, h, i: (b, h, i, 0))],
            out_shape=[jax.ShapeDtypeStruct((bsz, hq, lq, dv), BF16), jax.ShapeDtypeStruct((bsz, hq, lq, 1), F32)],
            compiler_params=_params(("parallel", "parallel", "parallel")),
        )(q, k, v_ones)

    def call_bwd(q, k, v, o, lse, do):
        bsz, hq, lq, dk = q.shape
        _, hk, lk, dv = v.shape
        grp = hq // hk
        tq, cb = tiles(q)
        nq = lq // tq

        def body(q_ref, k_ref, v_ref, o_ref, lse_ref, do_ref, dq_ref, dk_ref, dv_ref):
            qi = pl.program_id(2)

            @pl.when(qi == 0)
            def _():
                dk_ref[...] = jnp.zeros_like(dk_ref)
                dv_ref[...] = jnp.zeros_like(dv_ref)

            def run(nk):
                for j in range(kv_per_step):
                    kk, vv = k_ref[0, j, :nk], v_ref[0, j, :nk]
                    dk_sum = dv_sum = None
                    for h in range(j * grp, (j + 1) * grp):
                        dd = do_ref[0, h]
                        qs, s = scores(q_ref[0, h], kk, dk)
                        p = jnp.exp2(s - lse_ref[0, h])
                        delta = jnp.sum(dd.astype(F32) * o_ref[0, h].astype(F32), axis=-1, keepdims=True)
                        dp = lax.dot_general(dd, vv, (((1,), (1,)), ((), ())), preferred_element_type=F32)
                        t = (p * (dp - delta)).astype(BF16)
                        dq_ref[0, h] = (jnp.dot(t, kk, preferred_element_type=F32) * dk ** -0.5).astype(dq_ref.dtype)
                        dk_h = lax.dot_general(t, qs, (((0,), (0,)), ((), ())), preferred_element_type=F32)
                        dv_h = lax.dot_general(p.astype(BF16), dd, (((0,), (0,)), ((), ())), preferred_element_type=F32)
                        dk_sum = dk_h if dk_sum is None else dk_sum + dk_h
                        dv_sum = dv_h if dv_sum is None else dv_sum + dv_h
                    dk_ref[0, j, :nk] += dk_sum
                    dv_ref[0, j, :nk] += dv_sum

            by_segment(qi, cb, lk, run)

            @pl.when(qi == nq - 1)
            def _():
                dk_ref[...] = dk_ref[...] * LN_2

        qh = grp * kv_per_step
        q_spec = pl.BlockSpec((1, qh, tq, dk), lambda b, h, i: (b, h, i, 0))
        k_spec = pl.BlockSpec((1, kv_per_step, lk, dk), lambda b, h, i: (b, h, 0, 0))
        v_spec = pl.BlockSpec((1, kv_per_step, lk, dv), lambda b, h, i: (b, h, 0, 0))
        o_spec = pl.BlockSpec((1, qh, tq, dv), lambda b, h, i: (b, h, i, 0))
        lse_spec = pl.BlockSpec((1, qh, tq, 1), lambda b, h, i: (b, h, i, 0))
        return _pcall(
            body, name=name + "_bwd", grid=(bsz, hk // kv_per_step, nq),
            in_specs=[q_spec, k_spec, v_spec, o_spec, lse_spec, o_spec],
            out_specs=[q_spec, k_spec, v_spec],
            out_shape=[jax.ShapeDtypeStruct(q.shape, BF16), jax.ShapeDtypeStruct(k.shape, F32), jax.ShapeDtypeStruct(v.shape, F32)],
            compiler_params=_params(("parallel", "parallel", "arbitrary")),
        )(q, k, v, o, lse, do)

    @jax.custom_vjp
    def op(q, k, v):
        return call_fwd(q, k, v)[0]

    def fwd(q, k, v):
        o, lse = call_fwd(q, k, v)
        return o, (q, k, v, o, lse)

    def bwd(res, do):
        q, k, v, o, lse = res
        dq, dk, dv = call_bwd(q, k, v, o, lse, do)
        return dq, dk.astype(k.dtype), dv.astype(v.dtype)

    op.defvjp(fwd, bwd)
    return op


def _conv_call(ypad, taps, name):
    bsz, lp, ch = ypad.shape
    length = lp - 2 * ROW_ALIGN
    tl = _pick(length, (256, 128))

    def body(y_ref, w_ref, o_ref):
        base = pl.multiple_of(pl.program_id(1) * tl, tl)
        win = y_ref[0, pl.ds(base, tl + 2 * ROW_ALIGN), :]
        acc = jnp.broadcast_to(w_ref[pl.ds(D_CONV, 1), :], (tl, ch))
        for k in range(D_CONV):
            acc = acc + win[k:k + tl, :] * w_ref[pl.ds(k, 1), :]
        o_ref[0] = acc

    return _pcall(
        body, name=name, grid=(bsz, length // tl),
        in_specs=[pl.BlockSpec((1, lp, ch), lambda b, l: (b, 0, 0)), pl.BlockSpec((D_CONV + 1, ch), lambda b, l: (0, 0))],
        out_specs=pl.BlockSpec((1, tl, ch), lambda b, l: (b, l, 0)),
        out_shape=jax.ShapeDtypeStruct((bsz, length, ch), F32),
        compiler_params=_params(("parallel", "parallel")),
    )(ypad, taps)


def _conv_dw_call(ypad, dout, name):
    bsz, lp, ch = ypad.shape
    length = lp - 2 * ROW_ALIGN
    tl = _pick(length, (256, 128))

    def body(y_ref, d_ref, o_ref):
        b, l = pl.program_id(0), pl.program_id(1)

        @pl.when((b == 0) & (l == 0))
        def _():
            o_ref[...] = jnp.zeros_like(o_ref)

        base = pl.multiple_of(l * tl, tl)
        win = y_ref[0, pl.ds(base, tl + 2 * ROW_ALIGN), :]
        dd = d_ref[0]
        for k in range(D_CONV):
            o_ref[pl.ds(k, 1), :] += jnp.sum(win[k:k + tl, :] * dd, axis=0, keepdims=True)
        o_ref[pl.ds(D_CONV, 1), :] += jnp.sum(dd, axis=0, keepdims=True)

    return _pcall(
        body, name=name, grid=(bsz, length // tl),
        in_specs=[pl.BlockSpec((1, lp, ch), lambda b, l: (b, 0, 0)), pl.BlockSpec((1, tl, ch), lambda b, l: (b, l, 0))],
        out_specs=pl.BlockSpec((D_CONV + 1, ch), lambda b, l: (0, 0)),
        out_shape=jax.ShapeDtypeStruct((D_CONV + 1, ch), F32),
        compiler_params=_params(("arbitrary", "arbitrary")),
    )(ypad, dout)


def _pad_rows(y):
    return jnp.pad(y, ((0, 0), (CONV_PAD, 2 * ROW_ALIGN - CONV_PAD), (0, 0)))


@jax.custom_vjp
def _dwconv(y, taps):
    return _conv_call(_pad_rows(y), taps, "conv_fwd")


def _dwconv_fwd(y, taps):
    return _dwconv(y, taps), (y, taps)


def _dwconv_bwd(res, dout):
    y, taps = res
    flipped = jnp.concatenate([taps[:D_CONV][::-1], jnp.zeros_like(taps[D_CONV:])], axis=0)
    dy = _conv_call(_pad_rows(dout), flipped, "conv_dy")
    dtaps = _conv_dw_call(_pad_rows(y), dout, "conv_dw")
    return dy, dtaps


_dwconv.defvjp(_dwconv_fwd, _dwconv_bwd)


def _loss_head(x, target, g):
    bsz, length, d = x.shape
    tl = _pick(length, (256, 128))

    def f(xb, tb, gb):
        err = _rms(xb, gb) - tb
        return 0.5 * jnp.sum(jnp.sum(err * err, axis=-1, keepdims=True), axis=0, keepdims=True) / d

    def body(x_ref, t_ref, g_ref, loss_ref, dx_ref, dg_ref):
        val, vjp = jax.vjp(lambda xb, gb: f(xb, t_ref[0], gb), x_ref[0], g_ref[...])
        dx, dg = vjp(jnp.ones((1, 1), F32))
        dx_ref[0] = dx
        first = (pl.program_id(0) == 0) & (pl.program_id(1) == 0)

        @pl.when(first)
        def _():
            loss_ref[...] = val
            dg_ref[...] = dg

        @pl.when(jnp.logical_not(first))
        def _():
            loss_ref[...] += val
            dg_ref[...] += dg

    row = pl.BlockSpec((1, tl, d), lambda b, l: (b, l, 0))
    return _pcall(
        body, name="loss_head", grid=(bsz, length // tl),
        in_specs=[row, row, pl.BlockSpec((1, d), lambda b, l: (0, 0))],
        out_specs=[pl.BlockSpec((1, 1), lambda b, l: (0, 0)), row, pl.BlockSpec((1, d), lambda b, l: (0, 0))],
        out_shape=[jax.ShapeDtypeStruct((1, 1), F32), jax.ShapeDtypeStruct(x.shape, F32), jax.ShapeDtypeStruct((1, d), F32)],
        compiler_params=_params(("arbitrary", "arbitrary")),
    )(x, target, g)


def _adamw(w, g, m, v, name):
    shape = w.shape
    cols = shape[-1]
    rows = w.size // cols
    tr = _pick(rows, (512, 256, 128))
    w2, g2, m2, v2 = (t.reshape(rows, cols) for t in (w, g, m, v))

    def body(w_ref, g_ref, m_ref, v_ref, d_ref, nm_ref, nv_ref):
        gg = g_ref[...]
        nm = ADAM_B1 * m_ref[...] + (1.0 - ADAM_B1) * gg
        nv = ADAM_B2 * v_ref[...] + (1.0 - ADAM_B2) * jnp.square(gg)
        m_hat = nm / (1.0 - ADAM_B1 ** ADAM_STEP)
        v_hat = nv / (1.0 - ADAM_B2 ** ADAM_STEP)
        d_ref[...] = -ADAM_LR * (m_hat / (jnp.sqrt(v_hat) + ADAM_EPS) + ADAM_WD * w_ref[...])
        nm_ref[...] = nm
        nv_ref[...] = nv

    spec = pl.BlockSpec((tr, cols), lambda i: (i, 0))
    outs = _pcall(
        body, name=name, grid=(rows // tr,), in_specs=[spec] * 4, out_specs=[spec] * 3,
        out_shape=[jax.ShapeDtypeStruct((rows, cols), F32)] * 3,
        compiler_params=_params(("parallel",)),
    )(w2, g2, m2, v2)
    return tuple(o.reshape(shape) for o in outs)


def _mesh_pos():
    return lax.axis_index("x"), lax.axis_index("y"), lax.axis_index("c")


_RELATIONS = [(dx, dy, dc) for dx in (0, 1) for dy in (0, 1) for dc in (0, 1)][1:]


def _peer(pos, rel):
    return tuple(jnp.where(r == 1, 1 - p, p) if r else p for p, r in zip(pos, rel))


def _block_index(pos):
    return 4 * pos[0] + 2 * pos[1] + pos[2]


_HBM = pl.BlockSpec(memory_space=pltpu.HBM)


def _all_gather(xs, name):
    n = len(xs)

    def body(*refs):
        x_refs, out_refs, (send_sems, recv_sems, local_sems) = refs[:n], refs[n:2 * n], refs[2 * n:]
        x_, y_, c_ = _mesh_pos()
        me, sibling = (x_, y_, c_), (x_, y_, 1 - c_)
        chips = [(1 - x_, y_), (x_, 1 - y_), (1 - x_, 1 - y_)]

        def copy(t, k, block, to, own=False):
            slot = out_refs[t].at[_block_index(block)]
            return pltpu.make_async_remote_copy(
                src_ref=x_refs[t] if own else slot, dst_ref=slot, send_sem=send_sems.at[7 * t + k], recv_sem=recv_sems.at[7 * t + k],
                device_id=to, device_id_type=pl.DeviceIdType.MESH)

        mine = [pltpu.make_async_copy(x_refs[t], out_refs[t].at[_block_index(me)], local_sems.at[t]) for t in range(n)]
        first = [[copy(t, 0, me, sibling, own=True)] + [copy(t, 1 + j, me, (*chip, c_), own=True) for j, chip in enumerate(chips)]
                 for t in range(n)]
        passed = [[copy(t, 4 + j, (*chip, c_), sibling) for j, chip in enumerate(chips)] for t in range(n)]
        for t in range(n):
            mine[t].start()
            for cp in first[t]:
                cp.start()
        for t in range(n):
            for j, chip in enumerate(chips):
                copy(t, 1 + j, (*chip, c_), me).wait_recv()
                passed[t][j].start()
        for t in range(n):
            copy(t, 0, sibling, me).wait_recv()
            for j, chip in enumerate(chips):
                copy(t, 4 + j, (*chip, 1 - c_), me).wait_recv()
            for cp in first[t] + passed[t]:
                cp.wait_send()
            mine[t].wait()

    return _pcall(
        body, name=name, in_specs=[_HBM] * n, out_specs=[_HBM] * n,
        out_shape=[jax.ShapeDtypeStruct((N_DEV,) + x.shape, x.dtype) for x in xs],
        scratch_shapes=[pltpu.SemaphoreType.DMA((7 * n,)), pltpu.SemaphoreType.DMA((7 * n,)), pltpu.SemaphoreType.DMA((n,))],
    )(*xs)


def _exchange_blocks(ps, name):
    n = len(ps)

    def body(*refs):
        p_refs, out_refs, (send_sems, recv_sems, local_sems) = refs[:n], refs[n:2 * n], refs[2 * n:]
        me = _mesh_pos()

        def copy(t, k, rel, arriving=False):
            peer = _peer(me, rel)
            return pltpu.make_async_remote_copy(
                src_ref=p_refs[t].at[_block_index(peer)], dst_ref=out_refs[t].at[_block_index(peer if arriving else me)],
                send_sem=send_sems.at[7 * t + k], recv_sem=recv_sems.at[7 * t + k], device_id=peer, device_id_type=pl.DeviceIdType.MESH)

        mine = [pltpu.make_async_copy(p_refs[t].at[_block_index(me)], out_refs[t].at[_block_index(me)], local_sems.at[t]) for t in range(n)]
        sends = [copy(t, k, rel) for t in range(n) for k, rel in enumerate(_RELATIONS)]
        for cp in mine + sends:
            cp.start()
        for t in range(n):
            for k, rel in enumerate(_RELATIONS):
                copy(t, k, rel, arriving=True).wait_recv()
        for cp in sends:
            cp.wait_send()
        for cp in mine:
            cp.wait()

    return _pcall(
        body, name=name, in_specs=[_HBM] * n, out_specs=[_HBM] * n,
        out_shape=[jax.ShapeDtypeStruct(p.shape, p.dtype) for p in ps],
        scratch_shapes=[pltpu.SemaphoreType.DMA((7 * n,)), pltpu.SemaphoreType.DMA((7 * n,)), pltpu.SemaphoreType.DMA((n,))],
    )(*ps)


_SEM = pl.BlockSpec(memory_space=pltpu.SEMAPHORE)
_EFFECT = pltpu.SideEffectType.DATAFLOW_SIDE_EFFECTING


def _push_start(srcs, after, name):
    n = len(srcs)
    lands = [lax.empty((N_DEV,) + s.shape[-2:], s.dtype) for s in srcs]

    def body(*refs):
        src_refs, land_refs = refs[:n], refs[n:2 * n]
        send_sems, recv_sems, token = refs[2 * n + 1:3 * n + 1], refs[3 * n + 1:4 * n + 1], refs[-1]
        me = _mesh_pos()
        for t in range(n):
            for rel in _RELATIONS:
                peer = _peer(me, rel)
                pltpu.make_async_remote_copy(
                    src_ref=src_refs[t].at[_block_index(peer)] if srcs[t].ndim == 3 else src_refs[t], dst_ref=land_refs[t].at[_block_index(me)],
                    send_sem=send_sems[t], recv_sem=recv_sems[t], device_id=peer, device_id_type=pl.DeviceIdType.MESH).start()
        token[...] = jnp.zeros_like(token)

    outs = _pcall(
        body, name=name,
        out_shape=[pltpu.SemaphoreType.DMA(())] * (2 * n) + [pltpu.HBM(s.shape, s.dtype) for s in srcs]
        + [pltpu.HBM(l.shape, l.dtype) for l in lands] + [jax.ShapeDtypeStruct((8, 128), F32)],
        in_specs=[_HBM] * (2 * n) + [pl.BlockSpec(memory_space=pl.ANY)],
        out_specs=[_SEM] * (2 * n) + [_HBM] * (2 * n) + [pl.BlockSpec(memory_space=pltpu.VMEM)],
        input_output_aliases={i: 2 * n + i for i in range(2 * n)}, compiler_params=pltpu.CompilerParams(has_side_effects=_EFFECT),
    )(*[pltpu.with_memory_space_constraint(t, pltpu.HBM) for t in list(srcs) + lands], after)
    return (outs[:n], outs[n:2 * n], outs[2 * n:3 * n], outs[3 * n:4 * n]), outs[-1]


def _push_wait(handle, after, owns, me, name):
    send_sems, recv_sems, src_thrus, land_thrus = handle
    n = len(land_thrus)

    def body(*refs):
        land_refs, sends, recvs = refs[n:2 * n], refs[2 * n:3 * n], refs[3 * n:4 * n]
        for t in range(n):
            seven = land_refs[t].at[pl.ds(0, N_DEV - 1)]
            all_seven = pltpu.make_async_remote_copy(src_ref=seven, dst_ref=seven, send_sem=sends[t], recv_sem=recvs[t],
                                                     device_id=_mesh_pos(), device_id_type=pl.DeviceIdType.MESH)
            all_seven.wait_send()
            all_seven.wait_recv()

    outs = _pcall(
        body, name=name,
        out_shape=[pltpu.HBM(t.shape, t.dtype) for t in list(src_thrus) + list(land_thrus)],
        in_specs=[_HBM] * (2 * n) + [_SEM] * (2 * n) + [pl.BlockSpec(memory_space=pl.ANY)], out_specs=[_HBM] * (2 * n),
        input_output_aliases={i: i for i in range(2 * n)}, compiler_params=pltpu.CompilerParams(has_side_effects=_EFFECT),
    )(*src_thrus, *land_thrus, *send_sems, *recv_sems, after)
    return [lax.dynamic_update_slice(land, own[None], (me, 0, 0)) for land, own in zip(outs[n:], owns)]


def _sum_blocks(p, name):
    n, rows, cols = p.shape
    tr = _pick(rows, (256, 128, 64, PACK_ROWS, 16, 8))

    def body(p_ref, o_ref):
        acc = p_ref[0].astype(F32)
        for s in range(1, n):
            acc = acc + p_ref[s].astype(F32)
        o_ref[...] = acc

    return _pcall(
        body, name=name, grid=(rows // tr,),
        in_specs=[pl.BlockSpec((n, tr, cols), lambda i: (0, i, 0))], out_specs=pl.BlockSpec((tr, cols), lambda i: (i, 0)),
        out_shape=jax.ShapeDtypeStruct((rows, cols), F32), compiler_params=_params(("parallel",)),
    )(p)


def _sum_rows(t, name):
    def body(t_ref, o_ref):
        o_ref[...] = jnp.sum(t_ref[...], axis=0, keepdims=True)

    return _pcall(body, name=name, out_shape=jax.ShapeDtypeStruct((1, t.shape[1]), F32))(t)


class _Packing:
    def __init__(self, sizes, align, total_align=None):
        self.offsets, self.sizes, self.align = {}, dict(sizes), align
        row = 0
        for name, size in sizes:
            self.offsets[name] = row
            row += -(-size // (align * PACK_COLS)) * align
        total_align = total_align or align
        self.rows = -(-row // total_align) * total_align
        self.tail = self.rows - row

    def pack(self, pieces):
        return self.pack_blocks({n: pieces[n].reshape(1, -1) for n in self.sizes})[0]

    def pack_blocks(self, pieces):
        out = []
        for n, size in self.sizes.items():
            padded = -(-size // (self.align * PACK_COLS)) * self.align * PACK_COLS
            out.append(jnp.pad(pieces[n], ((0, 0), (0, padded - size))).reshape(pieces[n].shape[0], -1, PACK_COLS))
        if self.tail:
            out.append(jnp.zeros((out[0].shape[0], self.tail, PACK_COLS), out[0].dtype))
        return jnp.concatenate(out, axis=1)

    def piece(self, packed, name, lead=()):
        start, size = self.offsets[name], self.sizes[name]
        nrow = -(-size // (self.align * PACK_COLS)) * self.align
        sl = packed[..., start:start + nrow, :]
        return sl.reshape(lead + (nrow * PACK_COLS,))[..., :size]


_PIECES = (("ev_w_in", "ev_w_in", 0, 1, "a"), ("w_out0", "w_out", 0, 0, "a"),
           ("mlp_w1_0", "mlp_w1", 0, 1, "b"), ("mlp_w2_0", "mlp_w2", 0, 0, "b"),
           ("od_w_in", "od_w_in", 0, 1, "c"), ("od_w_uq", "od_w_uq", 0, 1, "c"), ("od_w_ukv", "od_w_ukv", 0, 1, "c"),
           ("w_out1", "w_out", 1, 0, "c"), ("mlp_w1_1", "mlp_w1", 1, 1, "c"), ("mlp_w2_1", "mlp_w2", 1, 0, "c"))
_SMALL_SHARDED = (("od_q_norm_g", 1), ("od_conv_w", 2), ("od_conv_b", 1), ("od_ln_g", 1), ("od_ln_b", 1))
_REPLICATED = ("c_ctx", "norm1_g", "norm2_g", "ev_q_norm_g", "ev_k_norm_g", "ev_sgu_norm_g", "ev_sgu_w", "ev_sgu_b",
               "od_kv_norm_g", "final_g")


def _unshard(blocks, axis):
    moved = jnp.moveaxis(blocks, 0, axis)
    shape = moved.shape
    return moved.reshape(shape[:axis] + (shape[axis] * shape[axis + 1],) + shape[axis + 2:])


def _shard_blocks(full, axis):
    shape = full.shape
    split = full.reshape(shape[:axis] + (N_DEV, shape[axis] // N_DEV) + shape[axis + 1:])
    return jnp.moveaxis(split, axis, 0)


def _group_mean_matrix(width, group):
    idx = jnp.arange(width) // group
    return (idx[:, None] == idx[None, :]).astype(F32) / group


def _swap_matrix(width):
    idx = jnp.arange(width)
    return ((idx[:, None] ^ 1) == idx[None, :]).astype(F32)


def _angles(length, d_rot):
    rows = length // GRID_W
    row = jnp.broadcast_to(jnp.arange(rows)[:, None], (rows, GRID_W)).reshape(-1).astype(F32)
    col = jnp.broadcast_to(jnp.arange(GRID_W)[None, :], (rows, GRID_W)).reshape(-1).astype(F32)
    d_axis = d_rot // 2
    inv = ROPE_THETA ** (-jnp.arange(0, d_axis, 2, dtype=F32) / d_axis)
    return jnp.concatenate([row[:, None] * inv, col[:, None] * inv], axis=-1)


def _rope_tables(length, d_rot, head_dim, heads, ctx_len):
    ang = _angles(length, d_rot)
    cos = jnp.repeat(jnp.cos(ang), 2, axis=1)
    sin = jnp.repeat(jnp.sin(ang), 2, axis=1) * jnp.tile(jnp.array([-1.0, 1.0], F32), d_rot // 2)
    keep = head_dim - d_rot
    cos = jnp.concatenate([jnp.ones((length, keep), F32), cos], axis=1)
    sin = jnp.concatenate([jnp.zeros((length, keep), F32), sin], axis=1)
    cos, sin = jnp.tile(cos, (1, heads)), jnp.tile(sin, (1, heads))
    cos = jnp.concatenate([jnp.ones((ctx_len, cos.shape[1]), F32), cos], axis=0)
    sin = jnp.concatenate([jnp.zeros((ctx_len, sin.shape[1]), F32), sin], axis=0)
    return cos, sin


def _to_heads(t, heads):
    b, l, w = t.shape
    return t.reshape(b, l, heads, w // heads).transpose(0, 2, 1, 3)


def _from_heads(t):
    b, h, l, d = t.shape
    return t.transpose(0, 2, 1, 3).reshape(b, l, h * d)


def _segment_params(mod, bsz):
    parts = jnp.split(mod, N_MOD, axis=-1)
    out = []
    for part in parts:
        lat = part[:bsz]
        ctx = jnp.broadcast_to(part[bsz:bsz + 1], lat.shape)
        out.append(jnp.stack([ctx, lat], axis=1)[:, :, None, :])
    return out


def _flat(t):
    return t.reshape(-1, t.shape[-1])


def _sequence_rowwise(ctx_len):
    tl = math.gcd(256, ctx_len)

    def make(name, f, out_specs, rows_per_block=tl, ctx_blocks=ctx_len // tl):
        return _rowwise(name, f, out_specs, rows_per_block, ctx_blocks)

    return make


def _mixer0(xall, modrows0, w, bsz, length, ctx_len):
    d = xall.shape[-1]
    total = ctx_len + length
    rowwise, flat = _sequence_rowwise(ctx_len), _flat
    sh1, sc1, g1, sh2, sc2, _ = _segment_params(modrows0, bsz)
    (h,) = rowwise("mod0", _f_modulate, [(d, BF16)])((xall,), (), (sh1, sc1), (w["norm1_g0"],), ())
    ev_q, ev_kv = A_Q_HEADS * A_HEAD_DIM, A_KV_HEADS * A_HEAD_DIM
    half = B_GROUPS * B_GROUP_DIM
    groups = tuple(jnp.split(w["ev_w_in"], [ev_q, ev_q + ev_kv, ev_q + 2 * ev_kv, ev_q + 2 * ev_kv + half], axis=0))
    qp, kp, vp, zu, zv = [t.reshape(bsz, total, -1) for t in _linear_multi("ev_in", (F32, F32, BF16, F32, F32))(flat(h), groups)]
    cos_q, sin_q = _rope_tables(length, A_HEAD_DIM, A_HEAD_DIM, A_Q_HEADS, ctx_len)
    cos_k, sin_k = cos_q[:, :ev_kv], sin_q[:, :ev_kv]
    (q,) = rowwise("ev_q", _f_headnorm_rope, [(ev_q, BF16)])(
        (qp,), (cos_q, sin_q), (), (jnp.tile(w["ev_q_norm_g"][0], A_Q_HEADS)[None],), (_group_mean_matrix(ev_q, A_HEAD_DIM),))
    (k,) = rowwise("ev_k", _f_headnorm_rope, [(ev_kv, BF16)])(
        (kp,), (cos_k, sin_k), (), (jnp.tile(w["ev_k_norm_g"][0], A_KV_HEADS)[None],), (_group_mean_matrix(ev_kv, A_HEAD_DIM),))
    o_att = _attention("gqa", ctx_len, 1)(_to_heads(q, A_Q_HEADS), _to_heads(k, A_KV_HEADS), _to_heads(vp, A_KV_HEADS))
    u, vn = rowwise("sgu_pre", _f_sgu_pre, [(half, F32), (half, BF16)])(
        (zu, zv), (), (), (w["ev_sgu_norm_g"][0].reshape(1, half),), (_group_mean_matrix(half, B_GROUP_DIM),))
    bias = jnp.repeat(w["ev_sgu_b"][0].T, B_GROUP_DIM, axis=1)
    (o_sgu,) = rowwise("sgu_mix", _f_sgu_mix, [(half, BF16)], rows_per_block=B_CHUNK, ctx_blocks=0)(
        (u, vn), (), (), tuple(w["ev_sgu_w"][0][g] for g in range(B_GROUPS)) + (bias,), ())
    y = _linear_sum("out0")((flat(_from_heads(o_att)), flat(o_sgu)), tuple(jnp.split(w["w_out0"], 2, axis=0))).reshape(bsz, total, d)
    x1, h = rowwise("res_mod0a", _f_res_modulate, [(d, F32), (d, BF16)])((xall, y), (), (g1, sh2, sc2), (w["norm2_g0"],), ())
    return x1, h


def _mlp0(x1, h, modrows0, modrows1, w, bsz, length, ctx_len):
    d = x1.shape[-1]
    total = ctx_len + length
    g2 = _segment_params(modrows0, bsz)[5]
    sh1, sc1 = _segment_params(modrows1, bsz)[:2]
    y = _mlp("mlp0")(_flat(h), w["mlp_w1_0"], w["mlp_w2_0"]).reshape(bsz, total, d)
    return _sequence_rowwise(ctx_len)("res_mod0b", _f_res_modulate, [(d, F32), (d, BF16)])((x1, y), (), (g2, sh1, sc1), (w["norm1_g1"],), ())


def _layer1(x2, h, modrows1, w, bsz, length, ctx_len):
    d = x2.shape[-1]
    total = ctx_len + length
    half = B_GROUPS * B_GROUP_DIM
    rowwise, flat = _sequence_rowwise(ctx_len), _flat
    _, _, g1n, sh2n, sc2n, g2n = _segment_params(modrows1, bsz)
    groups = tuple(jnp.split(w["od_w_in"], [C_Q_RANK, C_Q_RANK + C_KV_RANK, C_Q_RANK + C_KV_RANK + C_ROPE,
                                            C_Q_RANK + C_KV_RANK + C_ROPE + half], axis=0))
    cq, ckv, kr, za, zg = [t.reshape(bsz, total, -1) for t in _linear_multi("od_in", (F32,) * 5)(flat(h), groups)]
    lat = slice(ctx_len, total)
    lat_tl = math.gcd(256, length)
    (cqn,) = _rowwise("od_qn", _f_rms, [(C_Q_RANK, BF16)], lat_tl)((cq[:, lat],), (), (), (w["od_q_norm_g"],), ())
    c_qk = C_NOPE + C_ROPE
    qf = _linear("od_uq", transposed=True)(flat(cqn), w["od_w_uq"]).reshape(bsz, length, C_HEADS * c_qk)
    cos_q, sin_q = _rope_tables(length, C_ROPE, c_qk, C_HEADS, 0)
    (q,) = _rowwise("od_qrope", _f_rope, [(C_HEADS * c_qk, BF16)], lat_tl)((qf,), (cos_q, sin_q), (), (), ())
    (ckvn,) = rowwise("od_kvn", _f_rms, [(C_KV_RANK, BF16)])((ckv,), (), (), (w["od_kv_norm_g"],), ())
    per_head = w["od_w_ukv"].reshape(C_HEADS, C_NOPE + C_V, C_KV_RANK)
    kn, vv = _linear_multi("od_ukv", (BF16, BF16))(
        flat(ckvn), (per_head[:, :C_NOPE].reshape(-1, C_KV_RANK), per_head[:, C_NOPE:].reshape(-1, C_KV_RANK)))
    kn, vv = _to_heads(kn.reshape(bsz, total, -1), C_HEADS), _to_heads(vv.reshape(bsz, total, -1), C_HEADS)
    cos_r, sin_r = _rope_tables(length, C_ROPE, C_ROPE, 1, ctx_len)
    (krr,) = rowwise("od_krope", _f_rope_narrow, [(C_ROPE, BF16)])((kr,), (cos_r, sin_r), (), (), (_swap_matrix(C_ROPE),))
    kfull = jnp.concatenate([kn, jnp.broadcast_to(krr[:, None], (bsz, C_HEADS, total, C_ROPE))], axis=-1)
    o_att = _attention("mla", 0, 2)(_to_heads(q, C_HEADS), kfull, vv)
    (glu,) = _rowwise("glu", _f_glu, [(half, F32)], lat_tl)((za[:, lat], zg[:, lat]), (), (), (), ())
    taps = jnp.concatenate([w["od_conv_w"][0], w["od_conv_b"]], axis=0)
    conv = _dwconv(glu, taps)
    (o_conv,) = _rowwise("ln_silu", _f_ln_silu, [(half, BF16)], lat_tl)((conv,), (), (), (w["od_ln_g"], w["od_ln_b"]), ())
    y = _linear_sum("out1")((flat(_from_heads(o_att)), flat(o_conv)), tuple(jnp.split(w["w_out1"], 2, axis=0))).reshape(bsz, length, d)
    lat_param = lambda p: p[:, 1:]
    x3, h = _rowwise("res_mod1a", _f_res_modulate, [(d, F32), (d, BF16)], lat_tl)(
        (x2[:, lat], y), (), (lat_param(g1n), lat_param(sh2n), lat_param(sc2n)), (w["norm2_g1"],), ())
    y = _mlp("mlp1")(flat(h), w["mlp_w1_1"], w["mlp_w2_1"]).reshape(bsz, length, d)
    (x4,) = _rowwise("res1b", _f_res, [(d, F32)], lat_tl)((x3, y), (), (lat_param(g2n),), (), ())
    return x4


def kernel(x, c, ctx, c_ctx, ada_w, ada_b, norm1_g, norm2_g, w_out, mlp_w1, mlp_w2, ev_w_in, ev_q_norm_g, ev_k_norm_g, ev_sgu_norm_g, ev_sgu_w, ev_sgu_b, od_w_in, od_q_norm_g, od_kv_norm_g, od_w_uq, od_w_ukv, od_conv_w, od_conv_b, od_ln_g, od_ln_b, final_g, loss_target, m_c_ctx, m_ada_w, m_ada_b, m_norm1_g, m_norm2_g, m_w_out, m_mlp_w1, m_mlp_w2, m_ev_w_in, m_ev_q_norm_g, m_ev_k_norm_g, m_ev_sgu_norm_g, m_ev_sgu_w, m_ev_sgu_b, m_od_w_in, m_od_q_norm_g, m_od_kv_norm_g, m_od_w_uq, m_od_w_ukv, m_od_conv_w, m_od_conv_b, m_od_ln_g, m_od_ln_b, m_final_g, v_c_ctx, v_ada_w, v_ada_b, v_norm1_g, v_norm2_g, v_w_out, v_mlp_w1, v_mlp_w2, v_ev_w_in, v_ev_q_norm_g, v_ev_k_norm_g, v_ev_sgu_norm_g, v_ev_sgu_w, v_ev_sgu_b, v_od_w_in, v_od_q_norm_g, v_od_kv_norm_g, v_od_w_uq, v_od_w_ukv, v_od_conv_w, v_od_conv_b, v_od_ln_g, v_od_ln_b, v_final_g):
    names = ["c_ctx", "ada_w", "ada_b", "norm1_g", "norm2_g", "w_out", "mlp_w1", "mlp_w2", "ev_w_in", "ev_q_norm_g", "ev_k_norm_g",
             "ev_sgu_norm_g", "ev_sgu_w", "ev_sgu_b", "od_w_in", "od_q_norm_g", "od_kv_norm_g", "od_w_uq", "od_w_ukv", "od_conv_w",
             "od_conv_b", "od_ln_g", "od_ln_b", "final_g"]
    local = dict(zip(names, [c_ctx, ada_w, ada_b, norm1_g, norm2_g, w_out, mlp_w1, mlp_w2, ev_w_in, ev_q_norm_g, ev_k_norm_g, ev_sgu_norm_g, ev_sgu_w, ev_sgu_b, od_w_in, od_q_norm_g, od_kv_norm_g, od_w_uq, od_w_ukv, od_conv_w, od_conv_b, od_ln_g, od_ln_b, final_g]))
    mom1 = dict(zip(names, [m_c_ctx, m_ada_w, m_ada_b, m_norm1_g, m_norm2_g, m_w_out, m_mlp_w1, m_mlp_w2, m_ev_w_in, m_ev_q_norm_g, m_ev_k_norm_g, m_ev_sgu_norm_g, m_ev_sgu_w, m_ev_sgu_b, m_od_w_in, m_od_q_norm_g, m_od_kv_norm_g, m_od_w_uq, m_od_w_ukv, m_od_conv_w, m_od_conv_b, m_od_ln_g, m_od_ln_b, m_final_g]))
    mom2 = dict(zip(names, [v_c_ctx, v_ada_w, v_ada_b, v_norm1_g, v_norm2_g, v_w_out, v_mlp_w1, v_mlp_w2, v_ev_w_in, v_ev_q_norm_g, v_ev_k_norm_g, v_ev_sgu_norm_g, v_ev_sgu_w, v_ev_sgu_b, v_od_w_in, v_od_q_norm_g, v_od_kv_norm_g, v_od_w_uq, v_od_w_ukv, v_od_conv_w, v_od_conv_b, v_od_ln_g, v_od_ln_b, v_final_g]))
    bsz, length, d = x.shape
    ctx_len = ctx.shape[1]
    me = _block_index(_mesh_pos())

    shard = {p: local[wn][layer] for p, wn, layer, _, _ in _PIECES}
    by_columns = {p: axis == 1 for p, _, _, axis, _ in _PIECES}
    stages = {s: [p for p, _, _, _, st in _PIECES if st == s] for s in "abc"}
    block_rows = {p: shard[p].shape[1] if by_columns[p] else shard[p].shape[0] for p in shard}

    def pad_block_rows(t, p):
        extra = -block_rows[p] % ROW_ALIGN
        return jnp.pad(t, [(0, 0)] * (t.ndim - 2) + [(0, extra), (0, 0)]) if extra else t

    def travelling(p):
        t = shard[p].astype(BF16)
        return pad_block_rows(t.T if by_columns[p] else t, p)

    mine = {s: [travelling(p) for p in stages[s]] for s in "abc"}
    tiny_pack = _Packing([(n, local[n].size) for n, _ in _SMALL_SHARDED], 8)

    def unpack_weights(s, gathered):
        return {p: g[:, :block_rows[p]].reshape(N_DEV * block_rows[p], g.shape[2]) for p, g in zip(stages[s], gathered)}

    def pack_grads(s, g):
        return [pad_block_rows(g[p].reshape(N_DEV, block_rows[p], g[p].shape[1]), p) for p in stages[s]]

    cond_local = jnp.concatenate([c, c_ctx[None], jnp.zeros((COND_ROWS - bsz - 1, d), F32)], axis=0)
    cond, gathered_tiny, *gathered_a = _all_gather(
        [cond_local, tiny_pack.pack({n: local[n] for n, _ in _SMALL_SHARDED})] + mine["a"], "gather_inputs")
    cond = cond.reshape(N_DEV * COND_ROWS, d)
    silu_op = _rowwise("silu", _f_silu, [(d, F32)], N_DEV * COND_ROWS)
    silu_rows, silu_pullback = jax.vjp(lambda r: silu_op((r[None],), (), (), (), ())[0][0], cond)
    mod_cols = ada_w.shape[2]
    mod_part = jnp.concatenate([_mm(silu_rows, ada_w[i], "nn", (F32,), f"ada{i}_fwd")[0] for i in range(2)], axis=0)
    (mod_all,) = _all_gather([mod_part], "gather_mod")
    mod_all = mod_all.reshape(N_DEV, 2, N_DEV * COND_ROWS, mod_cols)
    modrows = []
    for i in range(2):
        whole = mod_all[:, i].transpose(1, 0, 2).reshape(N_DEV * COND_ROWS, N_DEV * mod_cols) + ada_b[i]
        modrows.append(lax.dynamic_slice_in_dim(whole, me * COND_ROWS, COND_ROWS, axis=0)[:bsz + 1])

    weights_a = unpack_weights("a", gathered_a)
    gather_b, token_b = _push_start(mine["b"], mod_all, "gather_weights_b_start")
    gather_c, token_c = _push_start(mine["c"], token_b, "gather_weights_c_start")
    full = {n: local[n] for n in _REPLICATED}
    for n, axis in _SMALL_SHARDED:
        full[n] = _unshard(tiny_pack.piece(gathered_tiny, n, (N_DEV,)).reshape((N_DEV,) + local[n].shape), axis)

    xall = jnp.concatenate([ctx, x], axis=1)
    modrows0, modrows1 = modrows[0] + token_c[0, 0], modrows[1]
    w_a = dict(weights_a, norm1_g0=norm1_g[0][None], norm2_g0=norm2_g[0][None],
               **{n: full[n] for n in ("ev_q_norm_g", "ev_k_norm_g", "ev_sgu_norm_g", "ev_sgu_w", "ev_sgu_b")})
    (x1, h0), pull_a = jax.vjp(lambda x_, m0, w: _mixer0(x_, m0, w, bsz, length, ctx_len), xall, modrows0, w_a)
    w_b = dict(unpack_weights("b", _push_wait(gather_b, x1, mine["b"], me, "gather_weights_b_wait")), norm1_g1=norm1_g[1][None])
    (x2, h1), pull_b = jax.vjp(lambda x_, h_, m0, m1, w: _mlp0(x_, h_, m0, m1, w, bsz, length, ctx_len), x1, h0, modrows0, modrows1, w_b)
    w_c = dict(unpack_weights("c", _push_wait(gather_c, x2, mine["c"], me, "gather_weights_c_wait")), norm2_g1=norm2_g[1][None],
               **{n: full[n] for n in ("od_q_norm_g", "od_kv_norm_g", "od_conv_w", "od_conv_b", "od_ln_g", "od_ln_b")})
    x4, pull_c = jax.vjp(lambda x_, h_, m1, w: _layer1(x_, h_, m1, w, bsz, length, ctx_len), x2, h1, modrows1, w_c)
    loss_part, dx4, dfinal = _loss_head(x4, loss_target, final_g[None])
    loss = lax.psum(loss_part[0, 0], MESH_AXES)

    dx2, dh1, dmod1_c, g_c = pull_c(dx4)
    grads_c = pack_grads("c", g_c)
    exchange_c, token = _push_start(grads_c, dx2, "exchange_grads_c_start")
    dx1, dh0, dmod0_b, dmod1_b, g_b = pull_b((dx2, dh1 + token[0, 0].astype(dh1.dtype)))
    grads_b = pack_grads("b", g_b)
    exchange_b, token = _push_start(grads_b, dx1, "exchange_grads_b_start")
    dxall, dmod0_a, g_a = pull_a((dx1, dh0 + token[0, 0].astype(dh0.dtype)))
    grad_x = dxall[:, ctx_len:]
    dmodrows = [dmod0_a + dmod0_b, dmod1_b + dmod1_c]
    grads = {n: g[n] for g in (g_a, g_c) for n in g if n in full}
    grads["norm1_g"] = jnp.concatenate([g_a["norm1_g0"], g_b["norm1_g1"]], axis=0)
    grads["norm2_g"] = jnp.concatenate([g_a["norm2_g0"], g_c["norm2_g1"]], axis=0)
    grads["final_g"] = dfinal[0]

    dmod_local = jnp.concatenate([jnp.pad(dm, ((0, COND_ROWS - bsz - 1), (0, 0))) for dm in dmodrows], axis=0)
    (dmod_all,) = _all_gather([dmod_local], "gather_dmod")
    dmod_all = dmod_all.reshape(N_DEV, 2, COND_ROWS, N_DEV * mod_cols)
    reduced = {}
    grad_ada_w, grad_ada_b, dmod_mine = [], [], []
    for i in range(2):
        dmod = dmod_all[:, i].reshape(N_DEV * COND_ROWS, N_DEV * mod_cols)
        grad_ada_b.append(_sum_rows(dmod, f"ada{i}_db")[0])
        dmod_mine.append(lax.dynamic_slice_in_dim(dmod, me * mod_cols, mod_cols, axis=1))
        grad_ada_w.append(_mm(silu_rows, dmod_mine[i], "tn", (F32,), f"ada{i}_dw")[0])
    reduced["ada_w"], reduced["ada_b"] = jnp.stack(grad_ada_w), jnp.stack(grad_ada_b)
    dsilu = _mm(jnp.concatenate(dmod_mine, axis=1), jnp.concatenate([ada_w[0], ada_w[1]], axis=1), "nt", (F32,), "ada_dx")[0]
    (dcond,) = silu_pullback(dsilu)
    grads["c_ctx"] = _sum_rows(dcond.reshape(N_DEV, COND_ROWS, d)[:, bsz], "c_ctx_rows")[0]

    def own_blocks(blocks):
        return [lax.dynamic_index_in_dim(t, me, 0, keepdims=False) for t in blocks]

    received = {"a": _exchange_blocks(pack_grads("a", g_a), "exchange_grads_a"),
                "b": _push_wait(exchange_b, dmod_all, own_blocks(grads_b), me, "exchange_grads_b_wait"),
                "c": _push_wait(exchange_c, dmod_all, own_blocks(grads_c), me, "exchange_grads_c_wait")}
    piece_grad = {}
    for s in "abc":
        for p, blocks in zip(stages[s], received[s]):
            summed = _sum_blocks(blocks, "sum_grads_" + p)[:block_rows[p]]
            piece_grad[p] = summed.T if by_columns[p] else summed
    for n in ("w_out", "mlp_w1", "mlp_w2"):
        reduced[n] = jnp.stack([piece_grad[p] for p, wn, _, _, _ in _PIECES if wn == n])
    for n in ("ev_w_in", "od_w_in", "od_w_uq", "od_w_ukv"):
        reduced[n] = piece_grad[n][None]

    small_names = list(_REPLICATED) + [n for n, _ in _SMALL_SHARDED]
    small_pack = _Packing([(n, full[n].size) for n in small_names], 8)
    (small_all,) = _all_gather([small_pack.pack({n: grads[n].astype(F32) for n in small_names})], "gather_small_grads")
    small_sum = _sum_blocks(small_all, "sum_small_grads")
    for n in _REPLICATED:
        reduced[n] = small_pack.piece(small_sum, n).reshape(local[n].shape)
    for n, axis in _SMALL_SHARDED:
        whole = small_pack.piece(small_sum, n).reshape(full[n].shape)
        reduced[n] = lax.dynamic_slice_in_dim(whole, me * local[n].shape[axis], local[n].shape[axis], axis=axis)

    delta, new_m, new_v = {}, {}, {}
    for n in names:
        delta[n], new_m[n], new_v[n] = _adamw(local[n], reduced[n], mom1[n], mom2[n], "adamw_" + n)
    return (loss, grad_x, *[reduced[n] for n in names], *[delta[n] for n in names], *[new_m[n] for n in names], *[new_v[n] for n in names])
```

```python
import functools
import math

import jax
import jax.numpy as jnp
from jax import lax
from jax.experimental import pallas as pl
from jax.experimental.pallas import tpu as pltpu

F32, BF16 = jnp.float32, jnp.bfloat16

EPS = 1e-6
GRID_W = 64
ROPE_THETA = 10000.0
A_HEAD_DIM, A_Q_HEADS, A_KV_HEADS = 64, 8, 2
B_GROUPS, B_GROUP_DIM, B_CHUNK = 8, 64, 128
C_HEADS, C_NOPE, C_ROPE, C_V, C_Q_RANK, C_KV_RANK = 8, 64, 32, 64, 256, 128
D_CONV = 31
CONV_PAD = D_CONV // 2
N_MOD = 6
N_DEV = 8
MESH_AXES = ("x", "y", "c")

ADAM_LR, ADAM_B1, ADAM_B2, ADAM_EPS, ADAM_WD, ADAM_STEP = 0.001, 0.9, 0.999, 1e-08, 0.01, 10

VMEM_LIMIT = 56 * 1024 * 1024
PACK_COLS = 1024
ROW_ALIGN = 16
PACK_ROWS = 32
COND_ROWS = 8


def _pcall(body, **kw):
    return pl.pallas_call(body, **kw)


def _params(sem=None):
    return pltpu.CompilerParams(dimension_semantics=sem, vmem_limit_bytes=VMEM_LIMIT)


def _pick(n, cands):
    for c in cands:
        if n % c == 0:
            return c
    return n


def _mm(a, b, mode, out_dtypes, name, epi=None, extras=()):
    if mode == "tn":
        kk, m = a.shape
    else:
        m, kk = a.shape
    n = b.shape[0] if mode == "nt" else b.shape[1]
    tm = _pick(m, (1152, 1024, 896, 768, 512, 256, 128))
    tn = _pick(n, (1024, 896, 768, 512, 256, 128))
    tk = kk if kk <= 1024 else _pick(kk, (1024, 896, 768, 512, 256, 128))
    nk = kk // tk
    ne, no = len(extras), len(out_dtypes)
    a_spec = pl.BlockSpec((tk, tm), lambda i, j, k: (k, i)) if mode == "tn" else pl.BlockSpec((tm, tk), lambda i, j, k: (i, k))
    b_spec = pl.BlockSpec((tn, tk), lambda i, j, k: (j, k)) if mode == "nt" else pl.BlockSpec((tk, tn), lambda i, j, k: (k, j))
    t_spec = pl.BlockSpec((tm, tn), lambda i, j, k: (i, j))
    dn = {"nn": ((1,), (0,)), "nt": ((1,), (1,)), "tn": ((0,), (0,))}[mode]

    def body(a_ref, b_ref, *rest):
        extra_refs, out_refs = rest[:ne], rest[ne:ne + no]

        def finish(acc):
            outs = (acc,) if epi is None else epi(acc, *[r[...] for r in extra_refs])
            for r, o in zip(out_refs, outs):
                r[...] = o.astype(r.dtype)

        part = lax.dot_general(a_ref[...].astype(BF16), b_ref[...].astype(BF16), (dn, ((), ())), preferred_element_type=F32)
        if nk == 1:
            finish(part)
        else:
            acc_ref = rest[-1]
            k = pl.program_id(2)

            @pl.when(k == 0)
            def _():
                acc_ref[...] = part

            @pl.when(k > 0)
            def _():
                acc_ref[...] += part

            @pl.when(k == nk - 1)
            def _():
                finish(acc_ref[...])

    outs = _pcall(
        body, name=name, grid=(m // tm, n // tn, nk),
        in_specs=[a_spec, b_spec] + [t_spec] * ne,
        out_specs=[t_spec] * no,
        out_shape=[jax.ShapeDtypeStruct((m, n), d) for d in out_dtypes],
        scratch_shapes=[pltpu.VMEM((tm, tn), F32)] if nk > 1 else [],
        compiler_params=_params(("parallel", "parallel", "arbitrary")),
    )(a, b, *extras)
    return outs


def _linear(name, transposed=False, out_dtype=F32):
    fwd_mode, dx_mode = ("nt", "nn") if transposed else ("nn", "nt")

    @jax.custom_vjp
    def op(x, w):
        return _mm(x, w, fwd_mode, (out_dtype,), name + "_fwd")[0]

    def fwd(x, w):
        return op(x, w), (x, w)

    def bwd(res, dy):
        x, w = res
        dx = _mm(dy, w, dx_mode, (x.dtype,), name + "_dx")[0]
        dw = _mm(dy, x, "tn", (w.dtype,), name + "_dw")[0] if transposed else _mm(x, dy, "tn", (w.dtype,), name + "_dw")[0]
        return dx, dw

    op.defvjp(fwd, bwd)
    return op


_NT, _NN, _TN = (((1,), (1,)), ((), ())), (((1,), (0,)), ((), ())), (((0,), (0,)), ((), ()))
_ROW_TILES = (1152, 1024, 768, 512, 256, 128)


def _whole(w):
    return pl.BlockSpec(w.shape, lambda i: (0, 0))


def _groups_apply(x, ws, out_dtypes, name):
    n, (m, kk) = len(ws), x.shape
    tm = _pick(m, _ROW_TILES)

    def body(x_ref, *refs):
        a = x_ref[...].astype(BF16)
        for w_ref, o_ref in zip(refs[:n], refs[n:]):
            o_ref[...] = lax.dot_general(a, w_ref[...], _NT, preferred_element_type=F32).astype(o_ref.dtype)

    return _pcall(
        body, name=name, grid=(m // tm,),
        in_specs=[pl.BlockSpec((tm, kk), lambda i: (i, 0))] + [_whole(w) for w in ws],
        out_specs=[pl.BlockSpec((tm, w.shape[0]), lambda i: (i, 0)) for w in ws],
        out_shape=[jax.ShapeDtypeStruct((m, w.shape[0]), dt) for w, dt in zip(ws, out_dtypes)],
        compiler_params=_params(("parallel",)),
    )(x, *ws)


def _groups_sum(xs, ws, out_dtype, name):
    n, m, kk = len(ws), xs[0].shape[0], ws[0].shape[1]
    tm = _pick(m, _ROW_TILES)

    def body(*refs):
        acc = None
        for x_ref, w_ref in zip(refs[:n], refs[n:2 * n]):
            part = lax.dot_general(x_ref[...].astype(BF16), w_ref[...], _NN, preferred_element_type=F32)
            acc = part if acc is None else acc + part
        refs[2 * n][...] = acc.astype(out_dtype)

    return _pcall(
        body, name=name, grid=(m // tm,),
        in_specs=[pl.BlockSpec((tm, w.shape[0]), lambda i: (i, 0)) for w in ws] + [_whole(w) for w in ws],
        out_specs=pl.BlockSpec((tm, kk), lambda i: (i, 0)), out_shape=jax.ShapeDtypeStruct((m, kk), out_dtype),
        compiler_params=_params(("parallel",)),
    )(*xs, *ws)


def _groups_outer(xs, y, ws, name):
    n, (m, kk) = len(ws), y.shape
    tk = _pick(m, (768, 512, 256, 128))
    steps = m // tk

    def body(y_ref, *refs):
        x_refs, o_refs, acc_refs = refs[:n], refs[n:2 * n], refs[2 * n:]
        k = pl.program_id(0)
        b = y_ref[...].astype(BF16)
        for x_ref, o_ref, acc_ref in zip(x_refs, o_refs, acc_refs):
            part = lax.dot_general(x_ref[...].astype(BF16), b, _TN, preferred_element_type=F32)

            @pl.when(k == 0)
            def _(acc_ref=acc_ref, part=part):
                acc_ref[...] = part

            @pl.when(k > 0)
            def _(acc_ref=acc_ref, part=part):
                acc_ref[...] += part

            @pl.when(k == steps - 1)
            def _(acc_ref=acc_ref, o_ref=o_ref):
                o_ref[...] = acc_ref[...].astype(o_ref.dtype)

    return _pcall(
        body, name=name, grid=(steps,),
        in_specs=[pl.BlockSpec((tk, kk), lambda k: (k, 0))] + [pl.BlockSpec((tk, w.shape[0]), lambda k: (k, 0)) for w in ws],
        out_specs=[_whole(w) for w in ws], out_shape=[jax.ShapeDtypeStruct(w.shape, w.dtype) for w in ws],
        scratch_shapes=[pltpu.VMEM(w.shape, F32) for w in ws], compiler_params=_params(("arbitrary",)),
    )(y, *xs)


def _linear_multi(name, out_dtypes):
    @jax.custom_vjp
    def op(x, wts):
        return tuple(_groups_apply(x, wts, out_dtypes, name + "_fwd"))

    def fwd(x, wts):
        return op(x, wts), (x, wts)

    def bwd(res, dys):
        x, wts = res
        return _groups_sum(dys, wts, x.dtype, name + "_dx"), tuple(_groups_outer(dys, x, wts, name + "_dw"))

    op.defvjp(fwd, bwd)
    return op


def _linear_sum(name):
    @jax.custom_vjp
    def op(xs, ws):
        return _groups_sum(xs, ws, F32, name + "_fwd")

    def fwd(xs, ws):
        return op(xs, ws), (xs, ws)

    def bwd(res, dy):
        xs, ws = res
        return tuple(_groups_apply(dy, ws, [x.dtype for x in xs], name + "_dx")), tuple(_groups_outer(xs, dy, ws, name + "_dw"))

    op.defvjp(fwd, bwd)
    return op


def _relu2_epi(acc):
    return jnp.square(jnp.maximum(acc, 0.0)), acc


def _relu2_bwd_epi(acc, a):
    return (acc * (2.0 * jnp.maximum(a.astype(F32), 0.0)),)


def _mlp(name):
    @jax.custom_vjp
    def op(h, w1t, w2):
        s, _ = _mm(h, w1t, "nt", (BF16, BF16), name + "_up", epi=_relu2_epi)
        return _mm(s, w2, "nn", (F32,), name + "_down")[0]

    def fwd(h, w1t, w2):
        s, a = _mm(h, w1t, "nt", (BF16, BF16), name + "_up", epi=_relu2_epi)
        return _mm(s, w2, "nn", (F32,), name + "_down")[0], (h, w1t, w2, s, a)

    def bwd(res, dy):
        h, w1t, w2, s, a = res
        da = _mm(dy, w2, "nt", (BF16,), name + "_ds", epi=_relu2_bwd_epi, extras=(a,))[0]
        dw2 = _mm(s, dy, "tn", (w2.dtype,), name + "_dw2")[0]
        dw1t = _mm(da, h, "tn", (w1t.dtype,), name + "_dw1")[0]
        dh = _mm(da, w1t, "nn", (h.dtype,), name + "_dh")[0]
        return dh, dw1t, dw2

    op.defvjp(fwd, bwd)
    return op


def _two_pass_dot(x, m):
    hi = x.astype(BF16)
    lo = (x - hi.astype(F32)).astype(BF16)
    mb = m.astype(BF16)
    return jnp.dot(hi, mb, preferred_element_type=F32) + jnp.dot(lo, mb, preferred_element_type=F32)


@jax.custom_vjp
def _sym_dot(x, m):
    return _two_pass_dot(x, m)


def _sym_dot_fwd(x, m):
    return _two_pass_dot(x, m), m


def _sym_dot_bwd(m, g):
    return _two_pass_dot(g, m), jnp.zeros_like(m)


_sym_dot.defvjp(_sym_dot_fwd, _sym_dot_bwd)


def _neighbour(x):
    lane = lax.broadcasted_iota(jnp.int32, x.shape, 1)
    return jnp.where(lane % 2 == 0, pltpu.roll(x, x.shape[1] - 1, 1), pltpu.roll(x, 1, 1))


@jax.custom_vjp
def _swap_pairs(x):
    return _neighbour(x)


_swap_pairs.defvjp(lambda x: (_neighbour(x), None), lambda _, g: (_neighbour(g),))


@jax.custom_vjp
def _bdot(a, b):
    return jnp.dot(a.astype(BF16), b.astype(BF16), preferred_element_type=F32)


def _bdot_fwd(a, b):
    return _bdot(a, b), (a, b)


def _bdot_bwd(res, g):
    a, b = res
    gb = g.astype(BF16)
    da = lax.dot_general(gb, b.astype(BF16), (((1,), (1,)), ((), ())), preferred_element_type=F32)
    db = lax.dot_general(a.astype(BF16), gb, (((0,), (0,)), ((), ())), preferred_element_type=F32)
    return da, db


_bdot.defvjp(_bdot_fwd, _bdot_bwd)


def _rowwise(name, f, out_specs, tl, ctx_blocks=0):
    def seg(l, s):
        return jnp.where(l >= ctx_blocks, s - 1, 0) if s > 1 else 0

    def specs(rows, tabs, pers, glbs, consts):
        row_specs = [pl.BlockSpec((1, tl, r.shape[2]), lambda b, l: (b, l, 0)) for r in rows]
        tab_specs = [pl.BlockSpec((tl, t.shape[1]), lambda b, l: (l, 0)) for t in tabs]
        per_specs = [pl.BlockSpec((1, 1, 1, p.shape[3]), functools.partial(lambda b, l, s: (b, seg(l, s), 0, 0), s=p.shape[1])) for p in pers]
        glb_specs = [pl.BlockSpec(g.shape, functools.partial(lambda b, l, nd: (0,) * nd, nd=g.ndim)) for g in glbs]
        const_specs = [pl.BlockSpec(c.shape, functools.partial(lambda b, l, nd: (0,) * nd, nd=c.ndim)) for c in consts]
        return row_specs, tab_specs, per_specs, glb_specs, const_specs

    def load(refs_rows, refs_tabs, refs_pers, refs_glbs, refs_consts):
        return (tuple(r[0].astype(F32) for r in refs_rows), tuple(t[...] for t in refs_tabs),
                tuple(p[0, 0].astype(F32) for p in refs_pers), tuple(g[...].astype(F32) for g in refs_glbs),
                tuple(c[...] for c in refs_consts))

    def call_fwd(rows, tabs, pers, glbs, consts):
        bsz, length = rows[0].shape[:2]
        nr, nt, npp, ng, nc = len(rows), len(tabs), len(pers), len(glbs), len(consts)
        rs, ts, ps, gs, cs = specs(rows, tabs, pers, glbs, consts)

        def body(*refs):
            ins, outs = refs[:nr + nt + npp + ng + nc], refs[nr + nt + npp + ng + nc:]
            r, t, p, g, c = load(ins[:nr], ins[nr:nr + nt], ins[nr + nt:nr + nt + npp], ins[nr + nt + npp:nr + nt + npp + ng], ins[nr + nt + npp + ng:])
            for o_ref, o in zip(outs, f(r, t, p, g, c)):
                o_ref[0] = o.astype(o_ref.dtype)

        return _pcall(
            body, name=name + "_fwd", grid=(bsz, length // tl),
            in_specs=rs + ts + ps + gs + cs,
            out_specs=[pl.BlockSpec((1, tl, w), lambda b, l: (b, l, 0)) for w, _ in out_specs],
            out_shape=[jax.ShapeDtypeStruct((bsz, length, w), d) for w, d in out_specs],
            compiler_params=_params(("parallel", "parallel")),
        )(*rows, *tabs, *pers, *glbs, *consts)

    def call_bwd(rows, tabs, pers, glbs, consts, cts):
        bsz, length = rows[0].shape[:2]
        nr, nt, npp, ng, nc, no = len(rows), len(tabs), len(pers), len(glbs), len(consts), len(cts)
        rs, ts, ps, gs, cs = specs(rows, tabs, pers, glbs, consts)
        n_in = nr + nt + npp + ng + nc

        def body(*refs):
            ins, ct_refs, outs = refs[:n_in], refs[n_in:n_in + no], refs[n_in + no:]
            r, t, p, g, c = load(ins[:nr], ins[nr:nr + nt], ins[nr + nt:nr + nt + npp], ins[nr + nt + npp:nr + nt + npp + ng], ins[nr + nt + npp + ng:])
            _, vjp = jax.vjp(lambda r_, p_, g_: tuple(f(r_, t, p_, g_, c)), r, p, g)
            dr, dp, dg = vjp(tuple(ct[0].astype(F32) for ct in ct_refs))
            dr_refs, dp_refs, dg_refs = outs[:nr], outs[nr:nr + npp], outs[nr + npp:]
            for ref, d in zip(dr_refs, dr):
                ref[0] = d.astype(ref.dtype)
            b, l = pl.program_id(0), pl.program_id(1)
            first_of_segment = (l == 0) | (l == ctx_blocks)
            for ref, d in zip(dp_refs, dp):
                @pl.when(first_of_segment)
                def _(ref=ref, d=d):
                    ref[0, 0] = d

                @pl.when(jnp.logical_not(first_of_segment))
                def _(ref=ref, d=d):
                    ref[0, 0] += d
            first = (b == 0) & (l == 0)
            for ref, d in zip(dg_refs, dg):
                @pl.when(first)
                def _(ref=ref, d=d):
                    ref[...] = d

                @pl.when(jnp.logical_not(first))
                def _(ref=ref, d=d):
                    ref[...] += d

        ct_specs = [pl.BlockSpec((1, tl, w), lambda b, l: (b, l, 0)) for w, _ in out_specs]
        outs = _pcall(
            body, name=name + "_bwd", grid=(bsz, length // tl),
            in_specs=rs + ts + ps + gs + cs + ct_specs,
            out_specs=rs + ps + gs,
            out_shape=[jax.ShapeDtypeStruct(r.shape, r.dtype) for r in rows]
            + [jax.ShapeDtypeStruct(p.shape, F32) for p in pers] + [jax.ShapeDtypeStruct(g.shape, F32) for g in glbs],
            compiler_params=_params(("arbitrary", "arbitrary")),
        )(*rows, *tabs, *pers, *glbs, *consts, *cts)
        return tuple(outs[:nr]), tuple(outs[nr:nr + npp]), tuple(outs[nr + npp:])

    @jax.custom_vjp
    def op(rows, tabs, pers, glbs, consts):
        return tuple(call_fwd(rows, tabs, pers, glbs, consts))

    def fwd(rows, tabs, pers, glbs, consts):
        return op(rows, tabs, pers, glbs, consts), (rows, tabs, pers, glbs, consts)

    def bwd(res, cts):
        rows, tabs, pers, glbs, consts = res
        dr, dp, dg = call_bwd(rows, tabs, pers, glbs, consts, tuple(cts))
        dp = tuple(d.astype(p.dtype) for d, p in zip(dp, pers))
        dg = tuple(d.astype(g.dtype) for d, g in zip(dg, glbs))
        return dr, tuple(jnp.zeros_like(t) for t in tabs), dp, dg, tuple(jnp.zeros_like(c) for c in consts)

    op.defvjp(fwd, bwd)
    return op


def _rms(x, g):
    return x * lax.rsqrt(jnp.mean(x * x, axis=-1, keepdims=True) + EPS) * g


def _f_silu(r, t, p, g, c):
    return (jax.nn.silu(r[0]),)


def _f_modulate(r, t, p, g, c):
    shift, scale = p
    return (_rms(r[0], g[0]) * (1.0 + scale) + shift,)


def _f_res_modulate(r, t, p, g, c):
    x, y = r
    gate, shift, scale = p
    xn = x + gate * y
    return xn, _rms(xn, g[0]) * (1.0 + scale) + shift


def _f_res(r, t, p, g, c):
    return (r[0] + p[0] * r[1],)


def _f_headnorm_rope(r, t, p, g, c):
    x = r[0]
    cos, sin = t
    xn = x * lax.rsqrt(_sym_dot(x * x, c[0]) + EPS) * g[0]
    return (xn * cos + _swap_pairs(xn) * sin,)


def _f_rope(r, t, p, g, c):
    x = r[0]
    cos, sin = t
    return (x * cos + _swap_pairs(x) * sin,)


def _f_rope_narrow(r, t, p, g, c):
    x = r[0]
    cos, sin = t
    return (x * cos + _sym_dot(x, c[0]) * sin,)


def _f_rms(r, t, p, g, c):
    return (_rms(r[0], g[0]),)


def _f_sgu_pre(r, t, p, g, c):
    u = jax.nn.gelu(r[0])
    v = jax.nn.gelu(r[1])
    vn = v * lax.rsqrt(_sym_dot(v * v, c[0]) + EPS) * g[0]
    return u, vn


def _f_sgu_mix(r, t, p, g, c):
    u, vn = r
    bias = g[B_GROUPS]
    group = lax.broadcasted_iota(jnp.int32, vn.shape, 1) // B_GROUP_DIM
    sv = bias
    for k in range(B_GROUPS):
        sv = sv + jnp.where(group == k, _bdot(g[k], vn), 0.0)
    return (u * sv,)


def _f_glu(r, t, p, g, c):
    return (r[0] * jax.nn.sigmoid(r[1]),)


def _f_ln_silu(r, t, p, g, c):
    x = r[0]
    mu = jnp.mean(x, axis=-1, keepdims=True)
    var = jnp.mean(jnp.square(x - mu), axis=-1, keepdims=True)
    return (jax.nn.silu((x - mu) * lax.rsqrt(var + EPS) * g[0] + g[1]),)


LOG2_E = 1.4426950408889634
LN_2 = 0.6931471805599453


def _attention(name, ctx_len, kv_per_step):
    def tiles(q):
        tq = _pick(q.shape[2], (256, 128))
        if ctx_len:
            tq = math.gcd(tq, ctx_len)
        return tq, ctx_len // tq

    def scores(q_tile, kk, dk):
        qs = (q_tile.astype(F32) * (LOG2_E * dk ** -0.5)).astype(BF16)
        return qs, lax.dot_general(qs, kk, (((1,), (1,)), ((), ())), preferred_element_type=F32)

    def by_segment(qi, cb, lk, run):
        if cb > 0:
            @pl.when(qi < cb)
            def _():
                run(ctx_len)

            @pl.when(qi >= cb)
            def _():
                run(lk)
        else:
            run(lk)

    def call_fwd(q, k, v):
        bsz, hq, lq, dk = q.shape
        _, hk, lk, dv = v.shape
        grp = hq // hk
        tq, cb = tiles(q)
        v_ones = jnp.concatenate([v, jnp.ones_like(v)], axis=-1)

        def body(q_ref, k_ref, v_ref, o_ref, lse_ref):
            def run(nk):
                for j in range(kv_per_step):
                    kk, vv = k_ref[0, j, :nk], v_ref[0, j, :nk]
                    for h in range(j * grp, (j + 1) * grp):
                        _, s = scores(q_ref[0, h], kk, dk)
                        m = jnp.max(s, axis=-1, keepdims=True)
                        acc = jnp.dot(jnp.exp2(s - m).astype(BF16), vv, preferred_element_type=F32)
                        l = acc[:, dv:dv + 1]
                        o_ref[0, h] = (acc[:, :dv] / l).astype(o_ref.dtype)
                        lse_ref[0, h] = m + jnp.log2(l)

            by_segment(pl.program_id(2), cb, lk, run)

        qh = grp * kv_per_step
        return _pcall(
            body, name=name + "_fwd", grid=(bsz, hk // kv_per_step, lq // tq),
            in_specs=[pl.BlockSpec((1, qh, tq, dk), lambda b, h, i: (b, h, i, 0)),
                      pl.BlockSpec((1, kv_per_step, lk, dk), lambda b, h, i: (b, h, 0, 0)),
                      pl.BlockSpec((1, kv_per_step, lk, 2 * dv), lambda b, h, i: (b, h, 0, 0))],
            out_specs=[pl.BlockSpec((1, qh, tq, dv), lambda b, h, i: (b, h, i, 0)), pl.BlockSpec((1, qh, tq, 1), lambda b, h, i: (b, h, i, 0))],
            out_shape=[jax.ShapeDtypeStruct((bsz, hq, lq, dv), BF16), jax.ShapeDtypeStruct((bsz, hq, lq, 1), F32)],
            compiler_params=_params(("parallel", "parallel", "parallel")),
        )(q, k, v_ones)

    def call_bwd(q, k, v, o, lse, do):
        bsz, hq, lq, dk = q.shape
        _, hk, lk, dv = v.shape
        grp = hq // hk
        tq, cb = tiles(q)
        nq = lq // tq

        def body(q_ref, k_ref, v_ref, o_ref, lse_ref, do_ref, dq_ref, dk_ref, dv_ref):
            qi = pl.program_id(2)

            @pl.when(qi == 0)
            def _():
                dk_ref[...] = jnp.zeros_like(dk_ref)
                dv_ref[...] = jnp.zeros_like(dv_ref)

            def run(nk):
                for j in range(kv_per_step):
                    kk, vv = k_ref[0, j, :nk], v_ref[0, j, :nk]
                    dk_sum = dv_sum = None
                    for h in range(j * grp, (j + 1) * grp):
                        dd = do_ref[0, h]
                        qs, s = scores(q_ref[0, h], kk, dk)
                        p = jnp.exp2(s - lse_ref[0, h])
                        delta = jnp.sum(dd.astype(F32) * o_ref[0, h].astype(F32), axis=-1, keepdims=True)
                        dp = lax.dot_general(dd, vv, (((1,), (1,)), ((), ())), preferred_element_type=F32)
                        t = (p * (dp - delta)).astype(BF16)
                        dq_ref[0, h] = (jnp.dot(t, kk, preferred_element_type=F32) * dk ** -0.5).astype(dq_ref.dtype)
                        dk_h = lax.dot_general(t, qs, (((0,), (0,)), ((), ())), preferred_element_type=F32)
                        dv_h = lax.dot_general(p.astype(BF16), dd, (((0,), (0,)), ((), ())), preferred_element_type=F32)
                        dk_sum = dk_h if dk_sum is None else dk_sum + dk_h
                        dv_sum = dv_h if dv_sum is None else dv_sum + dv_h
                    dk_ref[0, j, :nk] += dk_sum
                    dv_ref[0, j, :nk] += dv_sum

            by_segment(qi, cb, lk, run)

            @pl.when(qi == nq - 1)
            def _():
                dk_ref[...] = dk_ref[...] * LN_2

        qh = grp * kv_per_step
        q_spec = pl.BlockSpec((1, qh, tq, dk), lambda b, h, i: (b, h, i, 0))
        k_spec = pl.BlockSpec((1, kv_per_step, lk, dk), lambda b, h, i: (b, h, 0, 0))
        v_spec = pl.BlockSpec((1, kv_per_step, lk, dv), lambda b, h, i: (b, h, 0, 0))
        o_spec = pl.BlockSpec((1, qh, tq, dv), lambda b, h, i: (b, h, i, 0))
        lse_spec = pl.BlockSpec((1, qh, tq, 1), lambda b, h, i: (b, h, i, 0))
        return _pcall(
            body, name=name + "_bwd", grid=(bsz, hk // kv_per_step, nq),
            in_specs=[q_spec, k_spec, v_spec, o_spec, lse_spec, o_spec],
            out_specs=[q_spec, k_spec, v_spec],
            out_shape=[jax.ShapeDtypeStruct(q.shape, BF16), jax.ShapeDtypeStruct(k.shape, F32), jax.ShapeDtypeStruct(v.shape, F32)],
            compiler_params=_params(("parallel", "parallel", "arbitrary")),
        )(q, k, v, o, lse, do)

    @jax.custom_vjp
    def op(q, k, v):
        return call_fwd(q, k, v)[0]

    def fwd(q, k, v):
        o, lse = call_fwd(q, k, v)
        return o, (q, k, v, o, lse)

    def bwd(res, do):
        q, k, v, o, lse = res
        dq, dk, dv = call_bwd(q, k, v, o, lse, do)
        return dq, dk.astype(k.dtype), dv.astype(v.dtype)

    op.defvjp(fwd, bwd)
    return op


def _attention_rows(name, ctx_len, heads, kv_heads):
    grp = heads // kv_heads

    def by_segment(qi, cb, lk, run):
        if cb > 0:
            @pl.when(qi < cb)
            def _():
                run(ctx_len)

            @pl.when(qi >= cb)
            def _():
                run(lk)
        else:
            run(lk)

    def head(ref, h, hd):
        return ref[0, :, h * hd:(h + 1) * hd]

    def scores(q_tile, kk, hd):
        qs = (q_tile.astype(F32) * (LOG2_E * hd ** -0.5)).astype(BF16)
        return qs, lax.dot_general(qs, kk, _NT, preferred_element_type=F32)

    def shapes(q):
        bsz, length, width = q.shape
        tq = math.gcd(_pick(length, (256, 128)), ctx_len) if ctx_len else _pick(length, (256, 128))
        return bsz, length, width // heads, tq, ctx_len // tq

    def call_fwd(q, k, v):
        bsz, length, hd, tq, cb = shapes(q)

        def body(q_ref, k_ref, v_ref, o_ref, lse_ref):
            def run(nk):
                k_all, v_all = k_ref[0, :nk], v_ref[0, :nk]
                outs = []
                for j in range(kv_heads):
                    kk = k_all[:, j * hd:(j + 1) * hd]
                    v_ones = jnp.concatenate([v_all[:, j * hd:(j + 1) * hd], jnp.ones((nk, hd), BF16)], axis=1)
                    for h in range(j * grp, (j + 1) * grp):
                        _, s = scores(head(q_ref, h, hd), kk, hd)
                        m = jnp.max(s, axis=-1, keepdims=True)
                        acc = jnp.dot(jnp.exp2(s - m).astype(BF16), v_ones, preferred_element_type=F32)
                        l = acc[:, hd:hd + 1]
                        outs.append((acc[:, :hd] / l).astype(BF16))
                        lse_ref[0, h] = m + jnp.log2(l)
                o_ref[0] = jnp.concatenate(outs, axis=1)

            by_segment(pl.program_id(1), cb, length, run)

        q_spec = pl.BlockSpec((1, tq, heads * hd), lambda b, i: (b, i, 0))
        kv_spec = pl.BlockSpec((1, length, kv_heads * hd), lambda b, i: (b, 0, 0))
        return _pcall(
            body, name=name + "_fwd", grid=(bsz, length // tq), in_specs=[q_spec, kv_spec, kv_spec],
            out_specs=[q_spec, pl.BlockSpec((1, heads, tq, 1), lambda b, i: (b, 0, i, 0))],
            out_shape=[jax.ShapeDtypeStruct(q.shape, BF16), jax.ShapeDtypeStruct((bsz, heads, length, 1), F32)],
            compiler_params=_params(("parallel", "parallel")),
        )(q, k, v)

    def call_bwd(q, k, v, o, lse, do):
        bsz, length, hd, tq, cb = shapes(q)
        nq = length // tq

        def body(q_ref, k_ref, v_ref, o_ref, lse_ref, do_ref, dq_ref, dk_ref, dv_ref):
            qi = pl.program_id(1)

            @pl.when(qi == 0)
            def _():
                dk_ref[...] = jnp.zeros_like(dk_ref)
                dv_ref[...] = jnp.zeros_like(dv_ref)

            def run(nk):
                k_all, v_all = k_ref[0, :nk], v_ref[0, :nk]
                dqs, dks, dvs = [], [], []
                for j in range(kv_heads):
                    kk, vv = k_all[:, j * hd:(j + 1) * hd], v_all[:, j * hd:(j + 1) * hd]
                    dk_sum = dv_sum = None
                    for h in range(j * grp, (j + 1) * grp):
                        dd = head(do_ref, h, hd)
                        qs, s = scores(head(q_ref, h, hd), kk, hd)
                        p = jnp.exp2(s - lse_ref[0, h])
                        delta = jnp.sum(dd.astype(F32) * head(o_ref, h, hd).astype(F32), axis=-1, keepdims=True)
                        t = (p * (lax.dot_general(dd, vv, _NT, preferred_element_type=F32) - delta)).astype(BF16)
                        dqs.append((jnp.dot(t, kk, preferred_element_type=F32) * hd ** -0.5).astype(BF16))
                        dk_h = lax.dot_general(t, qs, _TN, preferred_element_type=F32)
                        dv_h = lax.dot_general(p.astype(BF16), dd, _TN, preferred_element_type=F32)
                        dk_sum = dk_h if dk_sum is None else dk_sum + dk_h
                        dv_sum = dv_h if dv_sum is None else dv_sum + dv_h
                    dks.append(dk_sum)
                    dvs.append(dv_sum)
                dq_ref[0] = jnp.concatenate(dqs, axis=1)
                dk_ref[0, :nk] += jnp.concatenate(dks, axis=1)
                dv_ref[0, :nk] += jnp.concatenate(dvs, axis=1)

            by_segment(qi, cb, length, run)

            @pl.when(qi == nq - 1)
            def _():
                dk_ref[...] = dk_ref[...] * LN_2

        q_spec = pl.BlockSpec((1, tq, heads * hd), lambda b, i: (b, i, 0))
        kv_spec = pl.BlockSpec((1, length, kv_heads * hd), lambda b, i: (b, 0, 0))
        lse_spec = pl.BlockSpec((1, heads, tq, 1), lambda b, i: (b, 0, i, 0))
        return _pcall(
            body, name=name + "_bwd", grid=(bsz, nq), in_specs=[q_spec, kv_spec, kv_spec, q_spec, lse_spec, q_spec],
            out_specs=[q_spec, kv_spec, kv_spec],
            out_shape=[jax.ShapeDtypeStruct(q.shape, BF16), jax.ShapeDtypeStruct(k.shape, F32), jax.ShapeDtypeStruct(v.shape, F32)],
            compiler_params=_params(("parallel", "arbitrary")),
        )(q, k, v, o, lse, do)

    @jax.custom_vjp
    def op(q, k, v):
        return call_fwd(q, k, v)[0]

    def fwd(q, k, v):
        o, lse = call_fwd(q, k, v)
        return o, (q, k, v, o, lse)

    def bwd(res, do):
        q, k, v, o, lse = res
        dq, dk, dv = call_bwd(q, k, v, o, lse, do)
        return dq, dk.astype(k.dtype), dv.astype(v.dtype)

    op.defvjp(fwd, bwd)
    return op


def _conv_call(ypad, taps, name):
    bsz, lp, ch = ypad.shape
    length = lp - 2 * ROW_ALIGN
    tl = _pick(length, (256, 128))

    def body(y_ref, w_ref, o_ref):
        base = pl.multiple_of(pl.program_id(1) * tl, tl)
        win = y_ref[0, pl.ds(base, tl + 2 * ROW_ALIGN), :]
        acc = jnp.broadcast_to(w_ref[pl.ds(D_CONV, 1), :], (tl, ch))
        for k in range(D_CONV):
            acc = acc + win[k:k + tl, :] * w_ref[pl.ds(k, 1), :]
        o_ref[0] = acc

    return _pcall(
        body, name=name, grid=(bsz, length // tl),
        in_specs=[pl.BlockSpec((1, lp, ch), lambda b, l: (b, 0, 0)), pl.BlockSpec((D_CONV + 1, ch), lambda b, l: (0, 0))],
        out_specs=pl.BlockSpec((1, tl, ch), lambda b, l: (b, l, 0)),
        out_shape=jax.ShapeDtypeStruct((bsz, length, ch), F32),
        compiler_params=_params(("parallel", "parallel")),
    )(ypad, taps)


def _conv_dw_call(ypad, dout, name):
    bsz, lp, ch = ypad.shape
    length = lp - 2 * ROW_ALIGN
    tl = _pick(length, (256, 128))

    def body(y_ref, d_ref, o_ref):
        b, l = pl.program_id(0), pl.program_id(1)

        @pl.when((b == 0) & (l == 0))
        def _():
            o_ref[...] = jnp.zeros_like(o_ref)

        base = pl.multiple_of(l * tl, tl)
        win = y_ref[0, pl.ds(base, tl + 2 * ROW_ALIGN), :]
        dd = d_ref[0]
        for k in range(D_CONV):
            o_ref[pl.ds(k, 1), :] += jnp.sum(win[k:k + tl, :] * dd, axis=0, keepdims=True)
        o_ref[pl.ds(D_CONV, 1), :] += jnp.sum(dd, axis=0, keepdims=True)

    return _pcall(
        body, name=name, grid=(bsz, length // tl),
        in_specs=[pl.BlockSpec((1, lp, ch), lambda b, l: (b, 0, 0)), pl.BlockSpec((1, tl, ch), lambda b, l: (b, l, 0))],
        out_specs=pl.BlockSpec((D_CONV + 1, ch), lambda b, l: (0, 0)),
        out_shape=jax.ShapeDtypeStruct((D_CONV + 1, ch), F32),
        compiler_params=_params(("arbitrary", "arbitrary")),
    )(ypad, dout)


def _pad_rows(y):
    return jnp.pad(y, ((0, 0), (CONV_PAD, 2 * ROW_ALIGN - CONV_PAD), (0, 0)))


@jax.custom_vjp
def _dwconv(y, taps):
    return _conv_call(_pad_rows(y), taps, "conv_fwd")


def _dwconv_fwd(y, taps):
    return _dwconv(y, taps), (y, taps)


def _dwconv_bwd(res, dout):
    y, taps = res
    flipped = jnp.concatenate([taps[:D_CONV][::-1], jnp.zeros_like(taps[D_CONV:])], axis=0)
    dy = _conv_call(_pad_rows(dout), flipped, "conv_dy")
    dtaps = _conv_dw_call(_pad_rows(y), dout, "conv_dw")
    return dy, dtaps


_dwconv.defvjp(_dwconv_fwd, _dwconv_bwd)


def _loss_head(x, target, g):
    bsz, length, d = x.shape
    tl = _pick(length, (256, 128))

    def f(xb, tb, gb):
        err = _rms(xb, gb) - tb
        return 0.5 * jnp.sum(jnp.sum(err * err, axis=-1, keepdims=True), axis=0, keepdims=True) / d

    def body(x_ref, t_ref, g_ref, loss_ref, dx_ref, dg_ref):
        val, vjp = jax.vjp(lambda xb, gb: f(xb, t_ref[0], gb), x_ref[0], g_ref[...])
        dx, dg = vjp(jnp.ones((1, 1), F32))
        dx_ref[0] = dx
        first = (pl.program_id(0) == 0) & (pl.program_id(1) == 0)

        @pl.when(first)
        def _():
            loss_ref[...] = val
            dg_ref[...] = dg

        @pl.when(jnp.logical_not(first))
        def _():
            loss_ref[...] += val
            dg_ref[...] += dg

    row = pl.BlockSpec((1, tl, d), lambda b, l: (b, l, 0))
    return _pcall(
        body, name="loss_head", grid=(bsz, length // tl),
        in_specs=[row, row, pl.BlockSpec((1, d), lambda b, l: (0, 0))],
        out_specs=[pl.BlockSpec((1, 1), lambda b, l: (0, 0)), row, pl.BlockSpec((1, d), lambda b, l: (0, 0))],
        out_shape=[jax.ShapeDtypeStruct((1, 1), F32), jax.ShapeDtypeStruct(x.shape, F32), jax.ShapeDtypeStruct((1, d), F32)],
        compiler_params=_params(("arbitrary", "arbitrary")),
    )(x, target, g)


def _adamw(w, g, m, v, name):
    shape = w.shape
    cols = shape[-1]
    rows = w.size // cols
    tr = _pick(rows, (512, 256, 128))
    w2, g2, m2, v2 = (t.reshape(rows, cols) for t in (w, g, m, v))

    def body(w_ref, g_ref, m_ref, v_ref, d_ref, nm_ref, nv_ref):
        gg = g_ref[...]
        nm = ADAM_B1 * m_ref[...] + (1.0 - ADAM_B1) * gg
        nv = ADAM_B2 * v_ref[...] + (1.0 - ADAM_B2) * jnp.square(gg)
        m_hat = nm / (1.0 - ADAM_B1 ** ADAM_STEP)
        v_hat = nv / (1.0 - ADAM_B2 ** ADAM_STEP)
        d_ref[...] = -ADAM_LR * (m_hat / (jnp.sqrt(v_hat) + ADAM_EPS) + ADAM_WD * w_ref[...])
        nm_ref[...] = nm
        nv_ref[...] = nv

    spec = pl.BlockSpec((tr, cols), lambda i: (i, 0))
    outs = _pcall(
        body, name=name, grid=(rows // tr,), in_specs=[spec] * 4, out_specs=[spec] * 3,
        out_shape=[jax.ShapeDtypeStruct((rows, cols), F32)] * 3,
        compiler_params=_params(("parallel",)),
    )(w2, g2, m2, v2)
    return tuple(o.reshape(shape) for o in outs)


def _mesh_pos():
    return lax.axis_index("x"), lax.axis_index("y"), lax.axis_index("c")


_RELATIONS = [(dx, dy, dc) for dx in (0, 1) for dy in (0, 1) for dc in (0, 1)][1:]


def _peer(pos, rel):
    return tuple(jnp.where(r == 1, 1 - p, p) if r else p for p, r in zip(pos, rel))


def _block_index(pos):
    return 4 * pos[0] + 2 * pos[1] + pos[2]


_HBM = pl.BlockSpec(memory_space=pltpu.HBM)


def _all_gather(xs, name):
    n = len(xs)

    def body(*refs):
        x_refs, out_refs, (send_sems, recv_sems, local_sems) = refs[:n], refs[n:2 * n], refs[2 * n:]
        x_, y_, c_ = _mesh_pos()
        me, sibling = (x_, y_, c_), (x_, y_, 1 - c_)
        chips = [(1 - x_, y_), (x_, 1 - y_), (1 - x_, 1 - y_)]

        def copy(t, k, block, to, own=False):
            slot = out_refs[t].at[_block_index(block)]
            return pltpu.make_async_remote_copy(
                src_ref=x_refs[t] if own else slot, dst_ref=slot, send_sem=send_sems.at[7 * t + k], recv_sem=recv_sems.at[7 * t + k],
                device_id=to, device_id_type=pl.DeviceIdType.MESH)

        mine = [pltpu.make_async_copy(x_refs[t], out_refs[t].at[_block_index(me)], local_sems.at[t]) for t in range(n)]
        first = [[copy(t, 0, me, sibling, own=True)] + [copy(t, 1 + j, me, (*chip, c_), own=True) for j, chip in enumerate(chips)]
                 for t in range(n)]
        passed = [[copy(t, 4 + j, (*chip, c_), sibling) for j, chip in enumerate(chips)] for t in range(n)]
        for t in range(n):
            mine[t].start()
            for cp in first[t]:
                cp.start()
        for t in range(n):
            for j, chip in enumerate(chips):
                copy(t, 1 + j, (*chip, c_), me).wait_recv()
                passed[t][j].start()
        for t in range(n):
            copy(t, 0, sibling, me).wait_recv()
            for j, chip in enumerate(chips):
                copy(t, 4 + j, (*chip, 1 - c_), me).wait_recv()
            for cp in first[t] + passed[t]:
                cp.wait_send()
            mine[t].wait()

    return _pcall(
        body, name=name, in_specs=[_HBM] * n, out_specs=[_HBM] * n,
        out_shape=[jax.ShapeDtypeStruct((N_DEV,) + x.shape, x.dtype) for x in xs],
        scratch_shapes=[pltpu.SemaphoreType.DMA((7 * n,)), pltpu.SemaphoreType.DMA((7 * n,)), pltpu.SemaphoreType.DMA((n,))],
    )(*xs)


def _exchange_blocks(ps, name):
    n = len(ps)

    def body(*refs):
        p_refs, out_refs, (send_sems, recv_sems, local_sems) = refs[:n], refs[n:2 * n], refs[2 * n:]
        me = _mesh_pos()

        def copy(t, k, rel, arriving=False):
            peer = _peer(me, rel)
            return pltpu.make_async_remote_copy(
                src_ref=p_refs[t].at[_block_index(peer)], dst_ref=out_refs[t].at[_block_index(peer if arriving else me)],
                send_sem=send_sems.at[7 * t + k], recv_sem=recv_sems.at[7 * t + k], device_id=peer, device_id_type=pl.DeviceIdType.MESH)

        mine = [pltpu.make_async_copy(p_refs[t].at[_block_index(me)], out_refs[t].at[_block_index(me)], local_sems.at[t]) for t in range(n)]
        sends = [copy(t, k, rel) for t in range(n) for k, rel in enumerate(_RELATIONS)]
        for cp in mine + sends:
            cp.start()
        for t in range(n):
            for k, rel in enumerate(_RELATIONS):
                copy(t, k, rel, arriving=True).wait_recv()
        for cp in sends:
            cp.wait_send()
        for cp in mine:
            cp.wait()

    return _pcall(
        body, name=name, in_specs=[_HBM] * n, out_specs=[_HBM] * n,
        out_shape=[jax.ShapeDtypeStruct(p.shape, p.dtype) for p in ps],
        scratch_shapes=[pltpu.SemaphoreType.DMA((7 * n,)), pltpu.SemaphoreType.DMA((7 * n,)), pltpu.SemaphoreType.DMA((n,))],
    )(*ps)


_SEM = pl.BlockSpec(memory_space=pltpu.SEMAPHORE)
_EFFECT = pltpu.SideEffectType.DATAFLOW_SIDE_EFFECTING


def _push_start(srcs, after, name):
    n = len(srcs)
    lands = [lax.empty((N_DEV,) + s.shape[-2:], s.dtype) for s in srcs]

    def body(*refs):
        src_refs, land_refs = refs[:n], refs[n:2 * n]
        send_sems, recv_sems, token = refs[2 * n + 1:3 * n + 1], refs[3 * n + 1:4 * n + 1], refs[-1]
        me = _mesh_pos()
        for t in range(n):
            for rel in _RELATIONS:
                peer = _peer(me, rel)
                pltpu.make_async_remote_copy(
                    src_ref=src_refs[t].at[_block_index(peer)] if srcs[t].ndim == 3 else src_refs[t], dst_ref=land_refs[t].at[_block_index(me)],
                    send_sem=send_sems[t], recv_sem=recv_sems[t], device_id=peer, device_id_type=pl.DeviceIdType.MESH).start()
        token[...] = jnp.zeros_like(token)

    outs = _pcall(
        body, name=name,
        out_shape=[pltpu.SemaphoreType.DMA(())] * (2 * n) + [pltpu.HBM(s.shape, s.dtype) for s in srcs]
        + [pltpu.HBM(l.shape, l.dtype) for l in lands] + [jax.ShapeDtypeStruct((8, 128), F32)],
        in_specs=[_HBM] * (2 * n) + [pl.BlockSpec(memory_space=pl.ANY)],
        out_specs=[_SEM] * (2 * n) + [_HBM] * (2 * n) + [pl.BlockSpec(memory_space=pltpu.VMEM)],
        input_output_aliases={i: 2 * n + i for i in range(2 * n)}, compiler_params=pltpu.CompilerParams(has_side_effects=_EFFECT),
    )(*[pltpu.with_memory_space_constraint(t, pltpu.HBM) for t in list(srcs) + lands], after)
    return (outs[:n], outs[n:2 * n], outs[2 * n:3 * n], outs[3 * n:4 * n]), outs[-1]


def _push_wait(handle, after, owns, me, name):
    send_sems, recv_sems, src_thrus, land_thrus = handle
    n = len(land_thrus)

    def body(*refs):
        land_refs, sends, recvs = refs[n:2 * n], refs[2 * n:3 * n], refs[3 * n:4 * n]
        for t in range(n):
            seven = land_refs[t].at[pl.ds(0, N_DEV - 1)]
            all_seven = pltpu.make_async_remote_copy(src_ref=seven, dst_ref=seven, send_sem=sends[t], recv_sem=recvs[t],
                                                     device_id=_mesh_pos(), device_id_type=pl.DeviceIdType.MESH)
            all_seven.wait_send()
            all_seven.wait_recv()

    outs = _pcall(
        body, name=name,
        out_shape=[pltpu.HBM(t.shape, t.dtype) for t in list(src_thrus) + list(land_thrus)],
        in_specs=[_HBM] * (2 * n) + [_SEM] * (2 * n) + [pl.BlockSpec(memory_space=pl.ANY)], out_specs=[_HBM] * (2 * n),
        input_output_aliases={i: i for i in range(2 * n)}, compiler_params=pltpu.CompilerParams(has_side_effects=_EFFECT),
    )(*src_thrus, *land_thrus, *send_sems, *recv_sems, after)
    return [lax.dynamic_update_slice(land, own[None], (me, 0, 0)) for land, own in zip(outs[n:], owns)]


def _sum_blocks(p, name):
    n, rows, cols = p.shape
    tr = _pick(rows, (256, 128, 64, PACK_ROWS, 16, 8))

    def body(p_ref, o_ref):
        acc = p_ref[0].astype(F32)
        for s in range(1, n):
            acc = acc + p_ref[s].astype(F32)
        o_ref[...] = acc

    return _pcall(
        body, name=name, grid=(rows // tr,),
        in_specs=[pl.BlockSpec((n, tr, cols), lambda i: (0, i, 0))], out_specs=pl.BlockSpec((tr, cols), lambda i: (i, 0)),
        out_shape=jax.ShapeDtypeStruct((rows, cols), F32), compiler_params=_params(("parallel",)),
    )(p)


def _sum_rows(t, name):
    def body(t_ref, o_ref):
        o_ref[...] = jnp.sum(t_ref[...], axis=0, keepdims=True)

    return _pcall(body, name=name, out_shape=jax.ShapeDtypeStruct((1, t.shape[1]), F32))(t)


class _Packing:
    def __init__(self, sizes, align, total_align=None):
        self.offsets, self.sizes, self.align = {}, dict(sizes), align
        row = 0
        for name, size in sizes:
            self.offsets[name] = row
            row += -(-size // (align * PACK_COLS)) * align
        total_align = total_align or align
        self.rows = -(-row // total_align) * total_align
        self.tail = self.rows - row

    def pack(self, pieces):
        return self.pack_blocks({n: pieces[n].reshape(1, -1) for n in self.sizes})[0]

    def pack_blocks(self, pieces):
        out = []
        for n, size in self.sizes.items():
            padded = -(-size // (self.align * PACK_COLS)) * self.align * PACK_COLS
            out.append(jnp.pad(pieces[n], ((0, 0), (0, padded - size))).reshape(pieces[n].shape[0], -1, PACK_COLS))
        if self.tail:
            out.append(jnp.zeros((out[0].shape[0], self.tail, PACK_COLS), out[0].dtype))
        return jnp.concatenate(out, axis=1)

    def piece(self, packed, name, lead=()):
        start, size = self.offsets[name], self.sizes[name]
        nrow = -(-size // (self.align * PACK_COLS)) * self.align
        sl = packed[..., start:start + nrow, :]
        return sl.reshape(lead + (nrow * PACK_COLS,))[..., :size]


_PIECES = (("ev_w_in", "ev_w_in", 0, 1, "a"), ("w_out0", "w_out", 0, 0, "a"),
           ("mlp_w1_0", "mlp_w1", 0, 1, "b"), ("mlp_w2_0", "mlp_w2", 0, 0, "b"),
           ("od_w_in", "od_w_in", 0, 1, "c"), ("od_w_uq", "od_w_uq", 0, 1, "c"), ("od_w_ukv", "od_w_ukv", 0, 1, "c"),
           ("w_out1", "w_out", 1, 0, "c"), ("mlp_w1_1", "mlp_w1", 1, 1, "c"), ("mlp_w2_1", "mlp_w2", 1, 0, "c"))
_SMALL_SHARDED = (("od_q_norm_g", 1), ("od_conv_w", 2), ("od_conv_b", 1), ("od_ln_g", 1), ("od_ln_b", 1))
_REPLICATED = ("c_ctx", "norm1_g", "norm2_g", "ev_q_norm_g", "ev_k_norm_g", "ev_sgu_norm_g", "ev_sgu_w", "ev_sgu_b",
               "od_kv_norm_g", "final_g")


def _unshard(blocks, axis):
    moved = jnp.moveaxis(blocks, 0, axis)
    shape = moved.shape
    return moved.reshape(shape[:axis] + (shape[axis] * shape[axis + 1],) + shape[axis + 2:])


def _shard_blocks(full, axis):
    shape = full.shape
    split = full.reshape(shape[:axis] + (N_DEV, shape[axis] // N_DEV) + shape[axis + 1:])
    return jnp.moveaxis(split, axis, 0)


def _group_mean_matrix(width, group):
    idx = jnp.arange(width) // group
    return (idx[:, None] == idx[None, :]).astype(F32) / group


def _swap_matrix(width):
    idx = jnp.arange(width)
    return ((idx[:, None] ^ 1) == idx[None, :]).astype(F32)


def _angles(length, d_rot):
    rows = length // GRID_W
    row = jnp.broadcast_to(jnp.arange(rows)[:, None], (rows, GRID_W)).reshape(-1).astype(F32)
    col = jnp.broadcast_to(jnp.arange(GRID_W)[None, :], (rows, GRID_W)).reshape(-1).astype(F32)
    d_axis = d_rot // 2
    inv = ROPE_THETA ** (-jnp.arange(0, d_axis, 2, dtype=F32) / d_axis)
    return jnp.concatenate([row[:, None] * inv, col[:, None] * inv], axis=-1)


def _rope_tables(length, d_rot, head_dim, heads, ctx_len):
    ang = _angles(length, d_rot)
    cos = jnp.repeat(jnp.cos(ang), 2, axis=1)
    sin = jnp.repeat(jnp.sin(ang), 2, axis=1) * jnp.tile(jnp.array([-1.0, 1.0], F32), d_rot // 2)
    keep = head_dim - d_rot
    cos = jnp.concatenate([jnp.ones((length, keep), F32), cos], axis=1)
    sin = jnp.concatenate([jnp.zeros((length, keep), F32), sin], axis=1)
    cos, sin = jnp.tile(cos, (1, heads)), jnp.tile(sin, (1, heads))
    cos = jnp.concatenate([jnp.ones((ctx_len, cos.shape[1]), F32), cos], axis=0)
    sin = jnp.concatenate([jnp.zeros((ctx_len, sin.shape[1]), F32), sin], axis=0)
    return cos, sin


def _to_heads(t, heads):
    b, l, w = t.shape
    return t.reshape(b, l, heads, w // heads).transpose(0, 2, 1, 3)


def _from_heads(t):
    b, h, l, d = t.shape
    return t.transpose(0, 2, 1, 3).reshape(b, l, h * d)


def _segment_params(mod, bsz):
    parts = jnp.split(mod, N_MOD, axis=-1)
    out = []
    for part in parts:
        lat = part[:bsz]
        ctx = jnp.broadcast_to(part[bsz:bsz + 1], lat.shape)
        out.append(jnp.stack([ctx, lat], axis=1)[:, :, None, :])
    return out


def _flat(t):
    return t.reshape(-1, t.shape[-1])


def _sequence_rowwise(ctx_len):
    tl = math.gcd(256, ctx_len)

    def make(name, f, out_specs, rows_per_block=tl, ctx_blocks=ctx_len // tl):
        return _rowwise(name, f, out_specs, rows_per_block, ctx_blocks)

    return make


def _mixer0(xall, modrows0, w, bsz, length, ctx_len):
    d = xall.shape[-1]
    total = ctx_len + length
    rowwise, flat = _sequence_rowwise(ctx_len), _flat
    sh1, sc1, g1, sh2, sc2, _ = _segment_params(modrows0, bsz)
    (h,) = rowwise("mod0", _f_modulate, [(d, BF16)])((xall,), (), (sh1, sc1), (w["norm1_g0"],), ())
    ev_q, ev_kv = A_Q_HEADS * A_HEAD_DIM, A_KV_HEADS * A_HEAD_DIM
    half = B_GROUPS * B_GROUP_DIM
    groups = tuple(jnp.split(w["ev_w_in"], [ev_q, ev_q + ev_kv, ev_q + 2 * ev_kv, ev_q + 2 * ev_kv + half], axis=0))
    qp, kp, vp, zu, zv = [t.reshape(bsz, total, -1) for t in _linear_multi("ev_in", (F32, F32, BF16, F32, F32))(flat(h), groups)]
    cos_q, sin_q = _rope_tables(length, A_HEAD_DIM, A_HEAD_DIM, A_Q_HEADS, ctx_len)
    cos_k, sin_k = cos_q[:, :ev_kv], sin_q[:, :ev_kv]
    (q,) = rowwise("ev_q", _f_headnorm_rope, [(ev_q, BF16)])(
        (qp,), (cos_q, sin_q), (), (jnp.tile(w["ev_q_norm_g"][0], A_Q_HEADS)[None],), (_group_mean_matrix(ev_q, A_HEAD_DIM),))
    (k,) = rowwise("ev_k", _f_headnorm_rope, [(ev_kv, BF16)])(
        (kp,), (cos_k, sin_k), (), (jnp.tile(w["ev_k_norm_g"][0], A_KV_HEADS)[None],), (_group_mean_matrix(ev_kv, A_HEAD_DIM),))
    o_att = _attention_rows("gqa", ctx_len, A_Q_HEADS, A_KV_HEADS)(q, k, vp)
    u, vn = rowwise("sgu_pre", _f_sgu_pre, [(half, F32), (half, BF16)])(
        (zu, zv), (), (), (w["ev_sgu_norm_g"][0].reshape(1, half),), (_group_mean_matrix(half, B_GROUP_DIM),))
    bias = jnp.repeat(w["ev_sgu_b"][0].T, B_GROUP_DIM, axis=1)
    (o_sgu,) = rowwise("sgu_mix", _f_sgu_mix, [(half, BF16)], rows_per_block=B_CHUNK, ctx_blocks=0)(
        (u, vn), (), (), tuple(w["ev_sgu_w"][0][g] for g in range(B_GROUPS)) + (bias,), ())
    y = _linear_sum("out0")((flat(o_att), flat(o_sgu)), tuple(jnp.split(w["w_out0"], 2, axis=0))).reshape(bsz, total, d)
    x1, h = rowwise("res_mod0a", _f_res_modulate, [(d, F32), (d, BF16)])((xall, y), (), (g1, sh2, sc2), (w["norm2_g0"],), ())
    return x1, h


def _mlp0(x1, h, modrows0, modrows1, w, bsz, length, ctx_len):
    d = x1.shape[-1]
    total = ctx_len + length
    g2 = _segment_params(modrows0, bsz)[5]
    sh1, sc1 = _segment_params(modrows1, bsz)[:2]
    y = _mlp("mlp0")(_flat(h), w["mlp_w1_0"], w["mlp_w2_0"]).reshape(bsz, total, d)
    return _sequence_rowwise(ctx_len)("res_mod0b", _f_res_modulate, [(d, F32), (d, BF16)])((x1, y), (), (g2, sh1, sc1), (w["norm1_g1"],), ())


def _layer1(x2, h, modrows1, w, bsz, length, ctx_len):
    d = x2.shape[-1]
    total = ctx_len + length
    half = B_GROUPS * B_GROUP_DIM
    rowwise, flat = _sequence_rowwise(ctx_len), _flat
    _, _, g1n, sh2n, sc2n, g2n = _segment_params(modrows1, bsz)
    groups = tuple(jnp.split(w["od_w_in"], [C_Q_RANK, C_Q_RANK + C_KV_RANK, C_Q_RANK + C_KV_RANK + C_ROPE,
                                            C_Q_RANK + C_KV_RANK + C_ROPE + half], axis=0))
    cq, ckv, kr, za, zg = [t.reshape(bsz, total, -1) for t in _linear_multi("od_in", (F32,) * 5)(flat(h), groups)]
    lat = slice(ctx_len, total)
    lat_tl = math.gcd(256, length)
    (cqn,) = _rowwise("od_qn", _f_rms, [(C_Q_RANK, BF16)], lat_tl)((cq[:, lat],), (), (), (w["od_q_norm_g"],), ())
    c_qk = C_NOPE + C_ROPE
    qf = _linear("od_uq", transposed=True)(flat(cqn), w["od_w_uq"]).reshape(bsz, length, C_HEADS * c_qk)
    cos_q, sin_q = _rope_tables(length, C_ROPE, c_qk, C_HEADS, 0)
    (q,) = _rowwise("od_qrope", _f_rope, [(C_HEADS * c_qk, BF16)], lat_tl)((qf,), (cos_q, sin_q), (), (), ())
    (ckvn,) = rowwise("od_kvn", _f_rms, [(C_KV_RANK, BF16)])((ckv,), (), (), (w["od_kv_norm_g"],), ())
    per_head = w["od_w_ukv"].reshape(C_HEADS, C_NOPE + C_V, C_KV_RANK)
    kn, vv = _linear_multi("od_ukv", (BF16, BF16))(
        flat(ckvn), (per_head[:, :C_NOPE].reshape(-1, C_KV_RANK), per_head[:, C_NOPE:].reshape(-1, C_KV_RANK)))
    kn, vv = _to_heads(kn.reshape(bsz, total, -1), C_HEADS), _to_heads(vv.reshape(bsz, total, -1), C_HEADS)
    cos_r, sin_r = _rope_tables(length, C_ROPE, C_ROPE, 1, ctx_len)
    (krr,) = rowwise("od_krope", _f_rope_narrow, [(C_ROPE, BF16)])((kr,), (cos_r, sin_r), (), (), (_swap_matrix(C_ROPE),))
    kfull = jnp.concatenate([kn, jnp.broadcast_to(krr[:, None], (bsz, C_HEADS, total, C_ROPE))], axis=-1)
    o_att = _attention("mla", 0, 2)(_to_heads(q, C_HEADS), kfull, vv)
    (glu,) = _rowwise("glu", _f_glu, [(half, F32)], lat_tl)((za[:, lat], zg[:, lat]), (), (), (), ())
    taps = jnp.concatenate([w["od_conv_w"][0], w["od_conv_b"]], axis=0)
    conv = _dwconv(glu, taps)
    (o_conv,) = _rowwise("ln_silu", _f_ln_silu, [(half, BF16)], lat_tl)((conv,), (), (), (w["od_ln_g"], w["od_ln_b"]), ())
    y = _linear_sum("out1")((flat(_from_heads(o_att)), flat(o_conv)), tuple(jnp.split(w["w_out1"], 2, axis=0))).reshape(bsz, length, d)
    lat_param = lambda p: p[:, 1:]
    x3, h = _rowwise("res_mod1a", _f_res_modulate, [(d, F32), (d, BF16)], lat_tl)(
        (x2[:, lat], y), (), (lat_param(g1n), lat_param(sh2n), lat_param(sc2n)), (w["norm2_g1"],), ())
    y = _mlp("mlp1")(flat(h), w["mlp_w1_1"], w["mlp_w2_1"]).reshape(bsz, length, d)
    (x4,) = _rowwise("res1b", _f_res, [(d, F32)], lat_tl)((x3, y), (), (lat_param(g2n),), (), ())
    return x4


def kernel(x, c, ctx, c_ctx, ada_w, ada_b, norm1_g, norm2_g, w_out, mlp_w1, mlp_w2, ev_w_in, ev_q_norm_g, ev_k_norm_g, ev_sgu_norm_g, ev_sgu_w, ev_sgu_b, od_w_in, od_q_norm_g, od_kv_norm_g, od_w_uq, od_w_ukv, od_conv_w, od_conv_b, od_ln_g, od_ln_b, final_g, loss_target, m_c_ctx, m_ada_w, m_ada_b, m_norm1_g, m_norm2_g, m_w_out, m_mlp_w1, m_mlp_w2, m_ev_w_in, m_ev_q_norm_g, m_ev_k_norm_g, m_ev_sgu_norm_g, m_ev_sgu_w, m_ev_sgu_b, m_od_w_in, m_od_q_norm_g, m_od_kv_norm_g, m_od_w_uq, m_od_w_ukv, m_od_conv_w, m_od_conv_b, m_od_ln_g, m_od_ln_b, m_final_g, v_c_ctx, v_ada_w, v_ada_b, v_norm1_g, v_norm2_g, v_w_out, v_mlp_w1, v_mlp_w2, v_ev_w_in, v_ev_q_norm_g, v_ev_k_norm_g, v_ev_sgu_norm_g, v_ev_sgu_w, v_ev_sgu_b, v_od_w_in, v_od_q_norm_g, v_od_kv_norm_g, v_od_w_uq, v_od_w_ukv, v_od_conv_w, v_od_conv_b, v_od_ln_g, v_od_ln_b, v_final_g):
    names = ["c_ctx", "ada_w", "ada_b", "norm1_g", "norm2_g", "w_out", "mlp_w1", "mlp_w2", "ev_w_in", "ev_q_norm_g", "ev_k_norm_g",
             "ev_sgu_norm_g", "ev_sgu_w", "ev_sgu_b", "od_w_in", "od_q_norm_g", "od_kv_norm_g", "od_w_uq", "od_w_ukv", "od_conv_w",
             "od_conv_b", "od_ln_g", "od_ln_b", "final_g"]
    local = dict(zip(names, [c_ctx, ada_w, ada_b, norm1_g, norm2_g, w_out, mlp_w1, mlp_w2, ev_w_in, ev_q_norm_g, ev_k_norm_g, ev_sgu_norm_g, ev_sgu_w, ev_sgu_b, od_w_in, od_q_norm_g, od_kv_norm_g, od_w_uq, od_w_ukv, od_conv_w, od_conv_b, od_ln_g, od_ln_b, final_g]))
    mom1 = dict(zip(names, [m_c_ctx, m_ada_w, m_ada_b, m_norm1_g, m_norm2_g, m_w_out, m_mlp_w1, m_mlp_w2, m_ev_w_in, m_ev_q_norm_g, m_ev_k_norm_g, m_ev_sgu_norm_g, m_ev_sgu_w, m_ev_sgu_b, m_od_w_in, m_od_q_norm_g, m_od_kv_norm_g, m_od_w_uq, m_od_w_ukv, m_od_conv_w, m_od_conv_b, m_od_ln_g, m_od_ln_b, m_final_g]))
    mom2 = dict(zip(names, [v_c_ctx, v_ada_w, v_ada_b, v_norm1_g, v_norm2_g, v_w_out, v_mlp_w1, v_mlp_w2, v_ev_w_in, v_ev_q_norm_g, v_ev_k_norm_g, v_ev_sgu_norm_g, v_ev_sgu_w, v_ev_sgu_b, v_od_w_in, v_od_q_norm_g, v_od_kv_norm_g, v_od_w_uq, v_od_w_ukv, v_od_conv_w, v_od_conv_b, v_od_ln_g, v_od_ln_b, v_final_g]))
    bsz, length, d = x.shape
    ctx_len = ctx.shape[1]
    me = _block_index(_mesh_pos())

    shard = {p: local[wn][layer] for p, wn, layer, _, _ in _PIECES}
    by_columns = {p: axis == 1 for p, _, _, axis, _ in _PIECES}
    stages = {s: [p for p, _, _, _, st in _PIECES if st == s] for s in "abc"}
    block_rows = {p: shard[p].shape[1] if by_columns[p] else shard[p].shape[0] for p in shard}

    def pad_block_rows(t, p):
        extra = -block_rows[p] % ROW_ALIGN
        return jnp.pad(t, [(0, 0)] * (t.ndim - 2) + [(0, extra), (0, 0)]) if extra else t

    def travelling(p):
        t = shard[p].astype(BF16)
        return pad_block_rows(t.T if by_columns[p] else t, p)

    mine = {s: [travelling(p) for p in stages[s]] for s in "abc"}
    tiny_pack = _Packing([(n, local[n].size) for n, _ in _SMALL_SHARDED], 8)

    def unpack_weights(s, gathered):
        return {p: g[:, :block_rows[p]].reshape(N_DEV * block_rows[p], g.shape[2]) for p, g in zip(stages[s], gathered)}

    def pack_grads(s, g):
        return [pad_block_rows(g[p].reshape(N_DEV, block_rows[p], g[p].shape[1]), p) for p in stages[s]]

    cond_local = jnp.concatenate([c, c_ctx[None], jnp.zeros((COND_ROWS - bsz - 1, d), F32)], axis=0)
    cond, gathered_tiny, *gathered_a = _all_gather(
        [cond_local, tiny_pack.pack({n: local[n] for n, _ in _SMALL_SHARDED})] + mine["a"], "gather_inputs")
    cond = cond.reshape(N_DEV * COND_ROWS, d)
    silu_op = _rowwise("silu", _f_silu, [(d, F32)], N_DEV * COND_ROWS)
    silu_rows, silu_pullback = jax.vjp(lambda r: silu_op((r[None],), (), (), (), ())[0][0], cond)
    mod_cols = ada_w.shape[2]
    mod_part = jnp.concatenate([_mm(silu_rows, ada_w[i], "nn", (F32,), f"ada{i}_fwd")[0] for i in range(2)], axis=0)
    (mod_all,) = _all_gather([mod_part], "gather_mod")
    mod_all = mod_all.reshape(N_DEV, 2, N_DEV * COND_ROWS, mod_cols)
    modrows = []
    for i in range(2):
        whole = mod_all[:, i].transpose(1, 0, 2).reshape(N_DEV * COND_ROWS, N_DEV * mod_cols) + ada_b[i]
        modrows.append(lax.dynamic_slice_in_dim(whole, me * COND_ROWS, COND_ROWS, axis=0)[:bsz + 1])

    weights_a = unpack_weights("a", gathered_a)
    gather_b, token_b = _push_start(mine["b"], mod_all, "gather_weights_b_start")
    gather_c, token_c = _push_start(mine["c"], token_b, "gather_weights_c_start")
    full = {n: local[n] for n in _REPLICATED}
    for n, axis in _SMALL_SHARDED:
        full[n] = _unshard(tiny_pack.piece(gathered_tiny, n, (N_DEV,)).reshape((N_DEV,) + local[n].shape), axis)

    xall = jnp.concatenate([ctx, x], axis=1)
    modrows0, modrows1 = modrows[0] + token_c[0, 0], modrows[1]
    w_a = dict(weights_a, norm1_g0=norm1_g[0][None], norm2_g0=norm2_g[0][None],
               **{n: full[n] for n in ("ev_q_norm_g", "ev_k_norm_g", "ev_sgu_norm_g", "ev_sgu_w", "ev_sgu_b")})
    (x1, h0), pull_a = jax.vjp(lambda x_, m0, w: _mixer0(x_, m0, w, bsz, length, ctx_len), xall, modrows0, w_a)
    w_b = dict(unpack_weights("b", _push_wait(gather_b, x1, mine["b"], me, "gather_weights_b_wait")), norm1_g1=norm1_g[1][None])
    (x2, h1), pull_b = jax.vjp(lambda x_, h_, m0, m1, w: _mlp0(x_, h_, m0, m1, w, bsz, length, ctx_len), x1, h0, modrows0, modrows1, w_b)
    w_c = dict(unpack_weights("c", _push_wait(gather_c, x2, mine["c"], me, "gather_weights_c_wait")), norm2_g1=norm2_g[1][None],
               **{n: full[n] for n in ("od_q_norm_g", "od_kv_norm_g", "od_conv_w", "od_conv_b", "od_ln_g", "od_ln_b")})
    x4, pull_c = jax.vjp(lambda x_, h_, m1, w: _layer1(x_, h_, m1, w, bsz, length, ctx_len), x2, h1, modrows1, w_c)
    loss_part, dx4, dfinal = _loss_head(x4, loss_target, final_g[None])
    loss = lax.psum(loss_part[0, 0], MESH_AXES)

    dx2, dh1, dmod1_c, g_c = pull_c(dx4)
    grads_c = pack_grads("c", g_c)
    exchange_c, token = _push_start(grads_c, dx2, "exchange_grads_c_start")
    dx1, dh0, dmod0_b, dmod1_b, g_b = pull_b((dx2, dh1 + token[0, 0].astype(dh1.dtype)))
    grads_b = pack_grads("b", g_b)
    exchange_b, token = _push_start(grads_b, dx1, "exchange_grads_b_start")
    dxall, dmod0_a, g_a = pull_a((dx1, dh0 + token[0, 0].astype(dh0.dtype)))
    grad_x = dxall[:, ctx_len:]
    dmodrows = [dmod0_a + dmod0_b, dmod1_b + dmod1_c]
    grads = {n: g[n] for g in (g_a, g_c) for n in g if n in full}
    grads["norm1_g"] = jnp.concatenate([g_a["norm1_g0"], g_b["norm1_g1"]], axis=0)
    grads["norm2_g"] = jnp.concatenate([g_a["norm2_g0"], g_c["norm2_g1"]], axis=0)
    grads["final_g"] = dfinal[0]

    dmod_local = jnp.concatenate([jnp.pad(dm, ((0, COND_ROWS - bsz - 1), (0, 0))) for dm in dmodrows], axis=0)
    (dmod_all,) = _all_gather([dmod_local], "gather_dmod")
    dmod_all = dmod_all.reshape(N_DEV, 2, COND_ROWS, N_DEV * mod_cols)
    reduced = {}
    grad_ada_w, grad_ada_b, dmod_mine = [], [], []
    for i in range(2):
        dmod = dmod_all[:, i].reshape(N_DEV * COND_ROWS, N_DEV * mod_cols)
        grad_ada_b.append(_sum_rows(dmod, f"ada{i}_db")[0])
        dmod_mine.append(lax.dynamic_slice_in_dim(dmod, me * mod_cols, mod_cols, axis=1))
        grad_ada_w.append(_mm(silu_rows, dmod_mine[i], "tn", (F32,), f"ada{i}_dw")[0])
    reduced["ada_w"], reduced["ada_b"] = jnp.stack(grad_ada_w), jnp.stack(grad_ada_b)
    dsilu = _mm(jnp.concatenate(dmod_mine, axis=1), jnp.concatenate([ada_w[0], ada_w[1]], axis=1), "nt", (F32,), "ada_dx")[0]
    (dcond,) = silu_pullback(dsilu)
    grads["c_ctx"] = _sum_rows(dcond.reshape(N_DEV, COND_ROWS, d)[:, bsz], "c_ctx_rows")[0]

    def own_blocks(blocks):
        return [lax.dynamic_index_in_dim(t, me, 0, keepdims=False) for t in blocks]

    received = {"a": _exchange_blocks(pack_grads("a", g_a), "exchange_grads_a"),
                "b": _push_wait(exchange_b, dmod_all, own_blocks(grads_b), me, "exchange_grads_b_wait"),
                "c": _push_wait(exchange_c, dmod_all, own_blocks(grads_c), me, "exchange_grads_c_wait")}
    piece_grad = {}
    for s in "abc":
        for p, blocks in zip(stages[s], received[s]):
            summed = _sum_blocks(blocks, "sum_grads_" + p)[:block_rows[p]]
            piece_grad[p] = summed.T if by_columns[p] else summed
    for n in ("w_out", "mlp_w1", "mlp_w2"):
        reduced[n] = jnp.stack([piece_grad[p] for p, wn, _, _, _ in _PIECES if wn == n])
    for n in ("ev_w_in", "od_w_in", "od_w_uq", "od_w_ukv"):
        reduced[n] = piece_grad[n][None]

    small_names = list(_REPLICATED) + [n for n, _ in _SMALL_SHARDED]
    small_pack = _Packing([(n, full[n].size) for n in small_names], 8)
    (small_all,) = _all_gather([small_pack.pack({n: grads[n].astype(F32) for n in small_names})], "gather_small_grads")
    small_sum = _sum_blocks(small_all, "sum_small_grads")
    for n in _REPLICATED:
        reduced[n] = small_pack.piece(small_sum, n).reshape(local[n].shape)
    for n, axis in _SMALL_SHARDED:
        whole = small_pack.piece(small_sum, n).reshape(full[n].shape)
        reduced[n] = lax.dynamic_slice_in_dim(whole, me * local[n].shape[axis], local[n].shape[axis], axis=axis)

    delta, new_m, new_v = {}, {}, {}
    for n in names:
        delta[n], new_m[n], new_v[n] = _adamw(local[n], reduced[n], mom1[n], mom2[n], "adamw_" + n)
    return (loss, grad_x, *[reduced[n] for n in names], *[delta[n] for n in names], *[new_m[n] for n in names], *[new_v[n] for n in names])
```

```python
import functools
import math

import jax
import jax.numpy as jnp
from jax import lax
from jax.experimental import pallas as pl
from jax.experimental.pallas import tpu as pltpu

F32, BF16 = jnp.float32, jnp.bfloat16

EPS = 1e-6
GRID_W = 64
ROPE_THETA = 10000.0
A_HEAD_DIM, A_Q_HEADS, A_KV_HEADS = 64, 8, 2
B_GROUPS, B_GROUP_DIM, B_CHUNK = 8, 64, 128
C_HEADS, C_NOPE, C_ROPE, C_V, C_Q_RANK, C_KV_RANK = 8, 64, 32, 64, 256, 128
D_CONV = 31
CONV_PAD = D_CONV // 2
N_MOD = 6
N_DEV = 8
MESH_AXES = ("x", "y", "c")

ADAM_LR, ADAM_B1, ADAM_B2, ADAM_EPS, ADAM_WD, ADAM_STEP = 0.001, 0.9, 0.999, 1e-08, 0.01, 10

VMEM_LIMIT = 56 * 1024 * 1024
PACK_COLS = 1024
ROW_ALIGN = 16
PACK_ROWS = 32
COND_ROWS = 8


def _pcall(body, **kw):
    return pl.pallas_call(body, **kw)


def _params(sem=None):
    return pltpu.CompilerParams(dimension_semantics=sem, vmem_limit_bytes=VMEM_LIMIT)


def _pick(n, cands):
    for c in cands:
        if n % c == 0:
            return c
    return n


def _mm(a, b, mode, out_dtypes, name, epi=None, extras=()):
    if mode == "tn":
        kk, m = a.shape
    else:
        m, kk = a.shape
    n = b.shape[0] if mode == "nt" else b.shape[1]
    tm = _pick(m, (1152, 1024, 896, 768, 512, 256, 128))
    tn = _pick(n, (1024, 896, 768, 512, 256, 128))
    tk = kk if kk <= 1024 else _pick(kk, (1024, 896, 768, 512, 256, 128))
    nk = kk // tk
    ne, no = len(extras), len(out_dtypes)
    a_spec = pl.BlockSpec((tk, tm), lambda i, j, k: (k, i)) if mode == "tn" else pl.BlockSpec((tm, tk), lambda i, j, k: (i, k))
    b_spec = pl.BlockSpec((tn, tk), lambda i, j, k: (j, k)) if mode == "nt" else pl.BlockSpec((tk, tn), lambda i, j, k: (k, j))
    t_spec = pl.BlockSpec((tm, tn), lambda i, j, k: (i, j))
    dn = {"nn": ((1,), (0,)), "nt": ((1,), (1,)), "tn": ((0,), (0,))}[mode]

    def body(a_ref, b_ref, *rest):
        extra_refs, out_refs = rest[:ne], rest[ne:ne + no]

        def finish(acc):
            outs = (acc,) if epi is None else epi(acc, *[r[...] for r in extra_refs])
            for r, o in zip(out_refs, outs):
                r[...] = o.astype(r.dtype)

        part = lax.dot_general(a_ref[...].astype(BF16), b_ref[...].astype(BF16), (dn, ((), ())), preferred_element_type=F32)
        if nk == 1:
            finish(part)
        else:
            acc_ref = rest[-1]
            k = pl.program_id(2)

            @pl.when(k == 0)
            def _():
                acc_ref[...] = part

            @pl.when(k > 0)
            def _():
                acc_ref[...] += part

            @pl.when(k == nk - 1)
            def _():
                finish(acc_ref[...])

    outs = _pcall(
        body, name=name, grid=(m // tm, n // tn, nk),
        in_specs=[a_spec, b_spec] + [t_spec] * ne,
        out_specs=[t_spec] * no,
        out_shape=[jax.ShapeDtypeStruct((m, n), d) for d in out_dtypes],
        scratch_shapes=[pltpu.VMEM((tm, tn), F32)] if nk > 1 else [],
        compiler_params=_params(("parallel", "parallel", "arbitrary")),
    )(a, b, *extras)
    return outs


def _linear(name, transposed=False, out_dtype=F32):
    fwd_mode, dx_mode = ("nt", "nn") if transposed else ("nn", "nt")

    @jax.custom_vjp
    def op(x, w):
        return _mm(x, w, fwd_mode, (out_dtype,), name + "_fwd")[0]

    def fwd(x, w):
        return op(x, w), (x, w)

    def bwd(res, dy):
        x, w = res
        dx = _mm(dy, w, dx_mode, (x.dtype,), name + "_dx")[0]
        dw = _mm(dy, x, "tn", (w.dtype,), name + "_dw")[0] if transposed else _mm(x, dy, "tn", (w.dtype,), name + "_dw")[0]
        return dx, dw

    op.defvjp(fwd, bwd)
    return op


_NT, _NN, _TN = (((1,), (1,)), ((), ())), (((1,), (0,)), ((), ())), (((0,), (0,)), ((), ()))
_ROW_TILES = (1152, 1024, 768, 512, 256, 128)


def _whole(w):
    return pl.BlockSpec(w.shape, lambda i: (0, 0))


def _groups_apply(x, ws, out_dtypes, name):
    n, (m, kk) = len(ws), x.shape
    tm = _pick(m, _ROW_TILES)

    def body(x_ref, *refs):
        a = x_ref[...].astype(BF16)
        for w_ref, o_ref in zip(refs[:n], refs[n:]):
            o_ref[...] = lax.dot_general(a, w_ref[...], _NT, preferred_element_type=F32).astype(o_ref.dtype)

    return _pcall(
        body, name=name, grid=(m // tm,),
        in_specs=[pl.BlockSpec((tm, kk), lambda i: (i, 0))] + [_whole(w) for w in ws],
        out_specs=[pl.BlockSpec((tm, w.shape[0]), lambda i: (i, 0)) for w in ws],
        out_shape=[jax.ShapeDtypeStruct((m, w.shape[0]), dt) for w, dt in zip(ws, out_dtypes)],
        compiler_params=_params(("parallel",)),
    )(x, *ws)


def _groups_sum(xs, ws, out_dtype, name):
    n, m, kk = len(ws), xs[0].shape[0], ws[0].shape[1]
    tm = _pick(m, _ROW_TILES)

    def body(*refs):
        acc = None
        for x_ref, w_ref in zip(refs[:n], refs[n:2 * n]):
            part = lax.dot_general(x_ref[...].astype(BF16), w_ref[...], _NN, preferred_element_type=F32)
            acc = part if acc is None else acc + part
        refs[2 * n][...] = acc.astype(out_dtype)

    return _pcall(
        body, name=name, grid=(m // tm,),
        in_specs=[pl.BlockSpec((tm, w.shape[0]), lambda i: (i, 0)) for w in ws] + [_whole(w) for w in ws],
        out_specs=pl.BlockSpec((tm, kk), lambda i: (i, 0)), out_shape=jax.ShapeDtypeStruct((m, kk), out_dtype),
        compiler_params=_params(("parallel",)),
    )(*xs, *ws)


def _groups_outer(xs, y, ws, name):
    n, (m, kk) = len(ws), y.shape
    tk = _pick(m, (768, 512, 256, 128))
    steps = m // tk

    def body(y_ref, *refs):
        x_refs, o_refs, acc_refs = refs[:n], refs[n:2 * n], refs[2 * n:]
        k = pl.program_id(0)
        b = y_ref[...].astype(BF16)
        for x_ref, o_ref, acc_ref in zip(x_refs, o_refs, acc_refs):
            part = lax.dot_general(x_ref[...].astype(BF16), b, _TN, preferred_element_type=F32)

            @pl.when(k == 0)
            def _(acc_ref=acc_ref, part=part):
                acc_ref[...] = part

            @pl.when(k > 0)
            def _(acc_ref=acc_ref, part=part):
                acc_ref[...] += part

            @pl.when(k == steps - 1)
            def _(acc_ref=acc_ref, o_ref=o_ref):
                o_ref[...] = acc_ref[...].astype(o_ref.dtype)

    return _pcall(
        body, name=name, grid=(steps,),
        in_specs=[pl.BlockSpec((tk, kk), lambda k: (k, 0))] + [pl.BlockSpec((tk, w.shape[0]), lambda k: (k, 0)) for w in ws],
        out_specs=[_whole(w) for w in ws], out_shape=[jax.ShapeDtypeStruct(w.shape, w.dtype) for w in ws],
        scratch_shapes=[pltpu.VMEM(w.shape, F32) for w in ws], compiler_params=_params(("arbitrary",)),
    )(y, *xs)


def _linear_multi(name, out_dtypes):
    @jax.custom_vjp
    def op(x, wts):
        return tuple(_groups_apply(x, wts, out_dtypes, name + "_fwd"))

    def fwd(x, wts):
        return op(x, wts), (x, wts)

    def bwd(res, dys):
        x, wts = res
        return _groups_sum(dys, wts, x.dtype, name + "_dx"), tuple(_groups_outer(dys, x, wts, name + "_dw"))

    op.defvjp(fwd, bwd)
    return op


def _linear_sum(name):
    @jax.custom_vjp
    def op(xs, ws):
        return _groups_sum(xs, ws, F32, name + "_fwd")

    def fwd(xs, ws):
        return op(xs, ws), (xs, ws)

    def bwd(res, dy):
        xs, ws = res
        return tuple(_groups_apply(dy, ws, [x.dtype for x in xs], name + "_dx")), tuple(_groups_outer(xs, dy, ws, name + "_dw"))

    op.defvjp(fwd, bwd)
    return op


def _relu2_epi(acc):
    return jnp.square(jnp.maximum(acc, 0.0)), acc


def _relu2_bwd_epi(acc, a):
    return (acc * (2.0 * jnp.maximum(a.astype(F32), 0.0)),)


def _mlp(name):
    @jax.custom_vjp
    def op(h, w1t, w2):
        s, _ = _mm(h, w1t, "nt", (BF16, BF16), name + "_up", epi=_relu2_epi)
        return _mm(s, w2, "nn", (F32,), name + "_down")[0]

    def fwd(h, w1t, w2):
        s, a = _mm(h, w1t, "nt", (BF16, BF16), name + "_up", epi=_relu2_epi)
        return _mm(s, w2, "nn", (F32,), name + "_down")[0], (h, w1t, w2, s, a)

    def bwd(res, dy):
        h, w1t, w2, s, a = res
        da = _mm(dy, w2, "nt", (BF16,), name + "_ds", epi=_relu2_bwd_epi, extras=(a,))[0]
        dw2 = _mm(s, dy, "tn", (w2.dtype,), name + "_dw2")[0]
        dw1t = _mm(da, h, "tn", (w1t.dtype,), name + "_dw1")[0]
        dh = _mm(da, w1t, "nn", (h.dtype,), name + "_dh")[0]
        return dh, dw1t, dw2

    op.defvjp(fwd, bwd)
    return op


def _two_pass_dot(x, m):
    hi = x.astype(BF16)
    lo = (x - hi.astype(F32)).astype(BF16)
    mb = m.astype(BF16)
    return jnp.dot(hi, mb, preferred_element_type=F32) + jnp.dot(lo, mb, preferred_element_type=F32)


@jax.custom_vjp
def _sym_dot(x, m):
    return _two_pass_dot(x, m)


def _sym_dot_fwd(x, m):
    return _two_pass_dot(x, m), m


def _sym_dot_bwd(m, g):
    return _two_pass_dot(g, m), jnp.zeros_like(m)


_sym_dot.defvjp(_sym_dot_fwd, _sym_dot_bwd)


def _neighbour(x):
    lane = lax.broadcasted_iota(jnp.int32, x.shape, 1)
    return jnp.where(lane % 2 == 0, pltpu.roll(x, x.shape[1] - 1, 1), pltpu.roll(x, 1, 1))


@jax.custom_vjp
def _swap_pairs(x):
    return _neighbour(x)


_swap_pairs.defvjp(lambda x: (_neighbour(x), None), lambda _, g: (_neighbour(g),))


@jax.custom_vjp
def _bdot(a, b):
    return jnp.dot(a.astype(BF16), b.astype(BF16), preferred_element_type=F32)


def _bdot_fwd(a, b):
    return _bdot(a, b), (a, b)


def _bdot_bwd(res, g):
    a, b = res
    gb = g.astype(BF16)
    da = lax.dot_general(gb, b.astype(BF16), (((1,), (1,)), ((), ())), preferred_element_type=F32)
    db = lax.dot_general(a.astype(BF16), gb, (((0,), (0,)), ((), ())), preferred_element_type=F32)
    return da, db


_bdot.defvjp(_bdot_fwd, _bdot_bwd)


def _rowwise(name, f, out_specs, tl, ctx_blocks=0):
    def seg(l, s):
        return jnp.where(l >= ctx_blocks, s - 1, 0) if s > 1 else 0

    def specs(rows, tabs, pers, glbs, consts):
        row_specs = [pl.BlockSpec((1, tl, r.shape[2]), lambda b, l: (b, l, 0)) for r in rows]
        tab_specs = [pl.BlockSpec((tl, t.shape[1]), lambda b, l: (l, 0)) for t in tabs]
        per_specs = [pl.BlockSpec((1, 1, 1, p.shape[3]), functools.partial(lambda b, l, s: (b, seg(l, s), 0, 0), s=p.shape[1])) for p in pers]
        glb_specs = [pl.BlockSpec(g.shape, functools.partial(lambda b, l, nd: (0,) * nd, nd=g.ndim)) for g in glbs]
        const_specs = [pl.BlockSpec(c.shape, functools.partial(lambda b, l, nd: (0,) * nd, nd=c.ndim)) for c in consts]
        return row_specs, tab_specs, per_specs, glb_specs, const_specs

    def load(refs_rows, refs_tabs, refs_pers, refs_glbs, refs_consts):
        return (tuple(r[0].astype(F32) for r in refs_rows), tuple(t[...] for t in refs_tabs),
                tuple(p[0, 0].astype(F32) for p in refs_pers), tuple(g[...].astype(F32) for g in refs_glbs),
                tuple(c[...] for c in refs_consts))

    def call_fwd(rows, tabs, pers, glbs, consts):
        bsz, length = rows[0].shape[:2]
        nr, nt, npp, ng, nc = len(rows), len(tabs), len(pers), len(glbs), len(consts)
        rs, ts, ps, gs, cs = specs(rows, tabs, pers, glbs, consts)

        def body(*refs):
            ins, outs = refs[:nr + nt + npp + ng + nc], refs[nr + nt + npp + ng + nc:]
            r, t, p, g, c = load(ins[:nr], ins[nr:nr + nt], ins[nr + nt:nr + nt + npp], ins[nr + nt + npp:nr + nt + npp + ng], ins[nr + nt + npp + ng:])
            for o_ref, o in zip(outs, f(r, t, p, g, c)):
                o_ref[0] = o.astype(o_ref.dtype)

        return _pcall(
            body, name=name + "_fwd", grid=(bsz, length // tl),
            in_specs=rs + ts + ps + gs + cs,
            out_specs=[pl.BlockSpec((1, tl, w), lambda b, l: (b, l, 0)) for w, _ in out_specs],
            out_shape=[jax.ShapeDtypeStruct((bsz, length, w), d) for w, d in out_specs],
            compiler_params=_params(("parallel", "parallel")),
        )(*rows, *tabs, *pers, *glbs, *consts)

    def call_bwd(rows, tabs, pers, glbs, consts, cts):
        bsz, length = rows[0].shape[:2]
        nr, nt, npp, ng, nc, no = len(rows), len(tabs), len(pers), len(glbs), len(consts), len(cts)
        rs, ts, ps, gs, cs = specs(rows, tabs, pers, glbs, consts)
        n_in = nr + nt + npp + ng + nc

        def body(*refs):
            ins, ct_refs, outs = refs[:n_in], refs[n_in:n_in + no], refs[n_in + no:]
            r, t, p, g, c = load(ins[:nr], ins[nr:nr + nt], ins[nr + nt:nr + nt + npp], ins[nr + nt + npp:nr + nt + npp + ng], ins[nr + nt + npp + ng:])
            _, vjp = jax.vjp(lambda r_, p_, g_: tuple(f(r_, t, p_, g_, c)), r, p, g)
            dr, dp, dg = vjp(tuple(ct[0].astype(F32) for ct in ct_refs))
            dr_refs, dp_refs, dg_refs = outs[:nr], outs[nr:nr + npp], outs[nr + npp:]
            for ref, d in zip(dr_refs, dr):
                ref[0] = d.astype(ref.dtype)
            b, l = pl.program_id(0), pl.program_id(1)
            first_of_segment = (l == 0) | (l == ctx_blocks)
            for ref, d in zip(dp_refs, dp):
                @pl.when(first_of_segment)
                def _(ref=ref, d=d):
                    ref[0, 0] = d

                @pl.when(jnp.logical_not(first_of_segment))
                def _(ref=ref, d=d):
                    ref[0, 0] += d
            first = (b == 0) & (l == 0)
            for ref, d in zip(dg_refs, dg):
                @pl.when(first)
                def _(ref=ref, d=d):
                    ref[...] = d

                @pl.when(jnp.logical_not(first))
                def _(ref=ref, d=d):
                    ref[...] += d

        ct_specs = [pl.BlockSpec((1, tl, w), lambda b, l: (b, l, 0)) for w, _ in out_specs]
        outs = _pcall(
            body, name=name + "_bwd", grid=(bsz, length // tl),
            in_specs=rs + ts + ps + gs + cs + ct_specs,
            out_specs=rs + ps + gs,
            out_shape=[jax.ShapeDtypeStruct(r.shape, r.dtype) for r in rows]
            + [jax.ShapeDtypeStruct(p.shape, F32) for p in pers] + [jax.ShapeDtypeStruct(g.shape, F32) for g in glbs],
            compiler_params=_params(("arbitrary", "arbitrary")),
        )(*rows, *tabs, *pers, *glbs, *consts, *cts)
        return tuple(outs[:nr]), tuple(outs[nr:nr + npp]), tuple(outs[nr + npp:])

    @jax.custom_vjp
    def op(rows, tabs, pers, glbs, consts):
        return tuple(call_fwd(rows, tabs, pers, glbs, consts))

    def fwd(rows, tabs, pers, glbs, consts):
        return op(rows, tabs, pers, glbs, consts), (rows, tabs, pers, glbs, consts)

    def bwd(res, cts):
        rows, tabs, pers, glbs, consts = res
        dr, dp, dg = call_bwd(rows, tabs, pers, glbs, consts, tuple(cts))
        dp = tuple(d.astype(p.dtype) for d, p in zip(dp, pers))
        dg = tuple(d.astype(g.dtype) for d, g in zip(dg, glbs))
        return dr, tuple(jnp.zeros_like(t) for t in tabs), dp, dg, tuple(jnp.zeros_like(c) for c in consts)

    op.defvjp(fwd, bwd)
    return op


def _rms(x, g):
    return x * lax.rsqrt(jnp.mean(x * x, axis=-1, keepdims=True) + EPS) * g


def _f_silu(r, t, p, g, c):
    return (jax.nn.silu(r[0]),)


def _f_modulate(r, t, p, g, c):
    shift, scale = p
    return (_rms(r[0], g[0]) * (1.0 + scale) + shift,)


def _f_res_modulate(r, t, p, g, c):
    x, y = r
    gate, shift, scale = p
    xn = x + gate * y
    return xn, _rms(xn, g[0]) * (1.0 + scale) + shift


def _f_res(r, t, p, g, c):
    return (r[0] + p[0] * r[1],)


def _f_headnorm_rope(r, t, p, g, c):
    x = r[0]
    cos, sin = t
    xn = x * lax.rsqrt(_sym_dot(x * x, c[0]) + EPS) * g[0]
    return (xn * cos + _swap_pairs(xn) * sin,)


def _f_rope(r, t, p, g, c):
    x = r[0]
    cos, sin = t
    return (x * cos + _swap_pairs(x) * sin,)


def _f_rope_narrow(r, t, p, g, c):
    x = r[0]
    cos, sin = t
    return (x * cos + _sym_dot(x, c[0]) * sin,)


def _f_rms(r, t, p, g, c):
    return (_rms(r[0], g[0]),)


def _f_sgu_pre(r, t, p, g, c):
    u = jax.nn.gelu(r[0])
    v = jax.nn.gelu(r[1])
    vn = v * lax.rsqrt(_sym_dot(v * v, c[0]) + EPS) * g[0]
    return u, vn


def _f_sgu_mix(r, t, p, g, c):
    u, vn = r
    bias = g[B_GROUPS]
    group = lax.broadcasted_iota(jnp.int32, vn.shape, 1) // B_GROUP_DIM
    sv = bias
    for k in range(B_GROUPS):
        sv = sv + jnp.where(group == k, _bdot(g[k], vn), 0.0)
    return (u * sv,)


def _f_glu(r, t, p, g, c):
    return (r[0] * jax.nn.sigmoid(r[1]),)


def _f_ln_silu(r, t, p, g, c):
    x = r[0]
    mu = jnp.mean(x, axis=-1, keepdims=True)
    var = jnp.mean(jnp.square(x - mu), axis=-1, keepdims=True)
    return (jax.nn.silu((x - mu) * lax.rsqrt(var + EPS) * g[0] + g[1]),)


LOG2_E = 1.4426950408889634
LN_2 = 0.6931471805599453


def _attention(name, ctx_len, kv_per_step):
    def tiles(q):
        tq = _pick(q.shape[2], (256, 128))
        if ctx_len:
            tq = math.gcd(tq, ctx_len)
        return tq, ctx_len // tq

    def scores(q_tile, kk, dk):
        qs = (q_tile.astype(F32) * (LOG2_E * dk ** -0.5)).astype(BF16)
        return qs, lax.dot_general(qs, kk, (((1,), (1,)), ((), ())), preferred_element_type=F32)

    def by_segment(qi, cb, lk, run):
        if cb > 0:
            @pl.when(qi < cb)
            def _():
                run(ctx_len)

            @pl.when(qi >= cb)
            def _():
                run(lk)
        else:
            run(lk)

    def call_fwd(q, k, v):
        bsz, hq, lq, dk = q.shape
        _, hk, lk, dv = v.shape
        grp = hq // hk
        tq, cb = tiles(q)
        v_ones = jnp.concatenate([v, jnp.ones_like(v)], axis=-1)

        def body(q_ref, k_ref, v_ref, o_ref, lse_ref):
            def run(nk):
                for j in range(kv_per_step):
                    kk, vv = k_ref[0, j, :nk], v_ref[0, j, :nk]
                    for h in range(j * grp, (j + 1) * grp):
                        _, s = scores(q_ref[0, h], kk, dk)
                        m = jnp.max(s, axis=-1, keepdims=True)
                        acc = jnp.dot(jnp.exp2(s - m).astype(BF16), vv, preferred_element_type=F32)
                        l = acc[:, dv:dv + 1]
                        o_ref[0, h] = (acc[:, :dv] / l).astype(o_ref.dtype)
                        lse_ref[0, h] = m + jnp.log2(l)

            by_segment(pl.program_id(2), cb, lk, run)

        qh = grp * kv_per_step
        return _pcall(
            body, name=name + "_fwd", grid=(bsz, hk // kv_per_step, lq // tq),
            in_specs=[pl.BlockSpec((1, qh, tq, dk), lambda b, h, i: (b, h, i, 0)),
                      pl.BlockSpec((1, kv_per_step, lk, dk), lambda b, h, i: (b, h, 0, 0)),
                      pl.BlockSpec((1, kv_per_step, lk, 2 * dv), lambda b, h, i: (b, h, 0, 0))],
            out_specs=[pl.BlockSpec((1, qh, tq, dv), lambda b, h, i: (b, h, i, 0)), pl.BlockSpec((1, qh, tq, 1), lambda b, h, i: (b, h, i, 0))],
            out_shape=[jax.ShapeDtypeStruct((bsz, hq, lq, dv), BF16), jax.ShapeDtypeStruct((bsz, hq, lq, 1), F32)],
            compiler_params=_params(("parallel", "parallel", "parallel")),
        )(q, k, v_ones)

    def call_bwd(q, k, v, o, lse, do):
        bsz, hq, lq, dk = q.shape
        _, hk, lk, dv = v.shape
        grp = hq // hk
        tq, cb = tiles(q)
        nq = lq // tq

        def body(q_ref, k_ref, v_ref, o_ref, lse_ref, do_ref, dq_ref, dk_ref, dv_ref):
            qi = pl.program_id(2)

            @pl.when(qi == 0)
            def _():
                dk_ref[...] = jnp.zeros_like(dk_ref)
                dv_ref[...] = jnp.zeros_like(dv_ref)

            def run(nk):
                for j in range(kv_per_step):
                    kk, vv = k_ref[0, j, :nk], v_ref[0, j, :nk]
                    dk_sum = dv_sum = None
                    for h in range(j * grp, (j + 1) * grp):
                        dd = do_ref[0, h]
                        qs, s = scores(q_ref[0, h], kk, dk)
                        p = jnp.exp2(s - lse_ref[0, h])
                        delta = jnp.sum(dd.astype(F32) * o_ref[0, h].astype(F32), axis=-1, keepdims=True)
                        dp = lax.dot_general(dd, vv, (((1,), (1,)), ((), ())), preferred_element_type=F32)
                        t = (p * (dp - delta)).astype(BF16)
                        dq_ref[0, h] = (jnp.dot(t, kk, preferred_element_type=F32) * dk ** -0.5).astype(dq_ref.dtype)
                        dk_h = lax.dot_general(t, qs, (((0,), (0,)), ((), ())), preferred_element_type=F32)
                        dv_h = lax.dot_general(p.astype(BF16), dd, (((0,), (0,)), ((), ())), preferred_element_type=F32)
                        dk_sum = dk_h if dk_sum is None else dk_sum + dk_h
                        dv_sum = dv_h if dv_sum is None else dv_sum + dv_h
                    dk_ref[0, j, :nk] += dk_sum
                    dv_ref[0, j, :nk] += dv_sum

            by_segment(qi, cb, lk, run)

            @pl.when(qi == nq - 1)
            def _():
                dk_ref[...] = dk_ref[...] * LN_2

        qh = grp * kv_per_step
        q_spec = pl.BlockSpec((1, qh, tq, dk), lambda b, h, i: (b, h, i, 0))
        k_spec = pl.BlockSpec((1, kv_per_step, lk, dk), lambda b, h, i: (b, h, 0, 0))
        v_spec = pl.BlockSpec((1, kv_per_step, lk, dv), lambda b, h, i: (b, h, 0, 0))
        o_spec = pl.BlockSpec((1, qh, tq, dv), lambda b, h, i: (b, h, i, 0))
        lse_spec = pl.BlockSpec((1, qh, tq, 1), lambda b, h, i: (b, h, i, 0))
        return _pcall(
            body, name=name + "_bwd", grid=(bsz, hk // kv_per_step, nq),
            in_specs=[q_spec, k_spec, v_spec, o_spec, lse_spec, o_spec],
            out_specs=[q_spec, k_spec, v_spec],
            out_shape=[jax.ShapeDtypeStruct(q.shape, BF16), jax.ShapeDtypeStruct(k.shape, F32), jax.ShapeDtypeStruct(v.shape, F32)],
            compiler_params=_params(("parallel", "parallel", "arbitrary")),
        )(q, k, v, o, lse, do)

    @jax.custom_vjp
    def op(q, k, v):
        return call_fwd(q, k, v)[0]

    def fwd(q, k, v):
        o, lse = call_fwd(q, k, v)
        return o, (q, k, v, o, lse)

    def bwd(res, do):
        q, k, v, o, lse = res
        dq, dk, dv = call_bwd(q, k, v, o, lse, do)
        return dq, dk.astype(k.dtype), dv.astype(v.dtype)

    op.defvjp(fwd, bwd)
    return op


def _attention_rows(name, ctx_len, heads, kv_heads):
    grp = heads // kv_heads

    def by_segment(qi, cb, lk, run):
        if cb > 0:
            @pl.when(qi < cb)
            def _():
                run(ctx_len)

            @pl.when(qi >= cb)
            def _():
                run(lk)
        else:
            run(lk)

    def head(ref, h, hd):
        return ref[0, :, h * hd:(h + 1) * hd]

    def scores(q_tile, kk, hd):
        qs = (q_tile.astype(F32) * (LOG2_E * hd ** -0.5)).astype(BF16)
        return qs, lax.dot_general(qs, kk, _NT, preferred_element_type=F32)

    def shapes(q):
        bsz, length, width = q.shape
        tq = math.gcd(_pick(length, (256, 128)), ctx_len) if ctx_len else _pick(length, (256, 128))
        return bsz, length, width // heads, tq, ctx_len // tq

    def call_fwd(q, k, v):
        bsz, length, hd, tq, cb = shapes(q)

        def body(q_ref, k_ref, v_ref, o_ref, lse_ref):
            def run(nk):
                k_all, v_all = k_ref[0, :nk], v_ref[0, :nk]
                outs = []
                for j in range(kv_heads):
                    kk = k_all[:, j * hd:(j + 1) * hd]
                    v_ones = jnp.concatenate([v_all[:, j * hd:(j + 1) * hd], jnp.ones((nk, hd), BF16)], axis=1)
                    for h in range(j * grp, (j + 1) * grp):
                        _, s = scores(head(q_ref, h, hd), kk, hd)
                        m = jnp.max(s, axis=-1, keepdims=True)
                        acc = jnp.dot(jnp.exp2(s - m).astype(BF16), v_ones, preferred_element_type=F32)
                        l = acc[:, hd:hd + 1]
                        outs.append((acc[:, :hd] / l).astype(BF16))
                        lse_ref[0, h] = m + jnp.log2(l)
                o_ref[0] = jnp.concatenate(outs, axis=1)

            by_segment(pl.program_id(1), cb, length, run)

        q_spec = pl.BlockSpec((1, tq, heads * hd), lambda b, i: (b, i, 0))
        kv_spec = pl.BlockSpec((1, length, kv_heads * hd), lambda b, i: (b, 0, 0))
        return _pcall(
            body, name=name + "_fwd", grid=(bsz, length // tq), in_specs=[q_spec, kv_spec, kv_spec],
            out_specs=[q_spec, pl.BlockSpec((1, heads, tq, 1), lambda b, i: (b, 0, i, 0))],
            out_shape=[jax.ShapeDtypeStruct(q.shape, BF16), jax.ShapeDtypeStruct((bsz, heads, length, 1), F32)],
            compiler_params=_params(("parallel", "parallel")),
        )(q, k, v)

    def call_bwd(q, k, v, o, lse, do):
        bsz, length, hd, tq, cb = shapes(q)
        nq = length // tq

        def body(q_ref, k_ref, v_ref, o_ref, lse_ref, do_ref, dq_ref, dk_ref, dv_ref):
            qi = pl.program_id(1)

            @pl.when(qi == 0)
            def _():
                dk_ref[...] = jnp.zeros_like(dk_ref)
                dv_ref[...] = jnp.zeros_like(dv_ref)

            def run(nk):
                k_all, v_all = k_ref[0, :nk], v_ref[0, :nk]
                dqs, dks, dvs = [], [], []
                for j in range(kv_heads):
                    kk, vv = k_all[:, j * hd:(j + 1) * hd], v_all[:, j * hd:(j + 1) * hd]
                    dk_sum = dv_sum = None
                    for h in range(j * grp, (j + 1) * grp):
                        dd = head(do_ref, h, hd)
                        qs, s = scores(head(q_ref, h, hd), kk, hd)
                        p = jnp.exp2(s - lse_ref[0, h])
                        delta = jnp.sum(dd.astype(F32) * head(o_ref, h, hd).astype(F32), axis=-1, keepdims=True)
                        t = (p * (lax.dot_general(dd, vv, _NT, preferred_element_type=F32) - delta)).astype(BF16)
                        dqs.append((jnp.dot(t, kk, preferred_element_type=F32) * hd ** -0.5).astype(BF16))
                        dk_h = lax.dot_general(t, qs, _TN, preferred_element_type=F32)
                        dv_h = lax.dot_general(p.astype(BF16), dd, _TN, preferred_element_type=F32)
                        dk_sum = dk_h if dk_sum is None else dk_sum + dk_h
                        dv_sum = dv_h if dv_sum is None else dv_sum + dv_h
                    dks.append(dk_sum)
                    dvs.append(dv_sum)
                dq_ref[0] = jnp.concatenate(dqs, axis=1)
                dk_ref[0, :nk] += jnp.concatenate(dks, axis=1)
                dv_ref[0, :nk] += jnp.concatenate(dvs, axis=1)

            by_segment(qi, cb, length, run)

            @pl.when(qi == nq - 1)
            def _():
                dk_ref[...] = dk_ref[...] * LN_2

        q_spec = pl.BlockSpec((1, tq, heads * hd), lambda b, i: (b, i, 0))
        kv_spec = pl.BlockSpec((1, length, kv_heads * hd), lambda b, i: (b, 0, 0))
        lse_spec = pl.BlockSpec((1, heads, tq, 1), lambda b, i: (b, 0, i, 0))
        return _pcall(
            body, name=name + "_bwd", grid=(bsz, nq), in_specs=[q_spec, kv_spec, kv_spec, q_spec, lse_spec, q_spec],
            out_specs=[q_spec, kv_spec, kv_spec],
            out_shape=[jax.ShapeDtypeStruct(q.shape, BF16), jax.ShapeDtypeStruct(k.shape, F32), jax.ShapeDtypeStruct(v.shape, F32)],
            compiler_params=_params(("parallel", "arbitrary")),
        )(q, k, v, o, lse, do)

    @jax.custom_vjp
    def op(q, k, v):
        return call_fwd(q, k, v)[0]

    def fwd(q, k, v):
        o, lse = call_fwd(q, k, v)
        return o, (q, k, v, o, lse)

    def bwd(res, do):
        q, k, v, o, lse = res
        dq, dk, dv = call_bwd(q, k, v, o, lse, do)
        return dq, dk.astype(k.dtype), dv.astype(v.dtype)

    op.defvjp(fwd, bwd)
    return op


def _attention_latent_rows(name, heads, nope, rope, dv, heads_per_step):
    lanes = 128
    scale = (nope + rope) ** -0.5

    def shapes(q, kn):
        bsz, lq, _ = q.shape
        return bsz, lq, kn.shape[1], _pick(lq, (256, 128))

    def scores(q_tile, kk):
        qs = (q_tile.astype(F32) * (LOG2_E * scale)).astype(BF16)
        return qs, lax.dot_general(qs, kk, _NT, preferred_element_type=F32)

    def call_fwd(q, kn, kr, v):
        bsz, lq, lk, tq = shapes(q, kn)

        def body(q_ref, kn_ref, kr_ref, v_ref, o_ref, lse_ref):
            real = lax.broadcasted_iota(jnp.int32, (lk, lanes), 1) < dv
            outs = []
            for h in range(heads_per_step):
                kk = kn_ref[0, :, h * lanes:(h + 1) * lanes] + kr_ref[0]
                v_ones = jnp.where(real, v_ref[0, :, h * lanes:(h + 1) * lanes], jnp.ones((lk, lanes), BF16))
                _, s = scores(q_ref[0, :, h * lanes:(h + 1) * lanes], kk)
                m = jnp.max(s, axis=-1, keepdims=True)
                acc = jnp.dot(jnp.exp2(s - m).astype(BF16), v_ones, preferred_element_type=F32)
                l = acc[:, dv:dv + 1]
                outs.append((acc[:, :dv] / l).astype(BF16))
                lse_ref[0, h] = m + jnp.log2(l)
            o_ref[0] = jnp.concatenate(outs, axis=1)

        wide = heads_per_step * lanes
        return _pcall(
            body, name=name + "_fwd", grid=(bsz, heads // heads_per_step, lq // tq),
            in_specs=[pl.BlockSpec((1, tq, wide), lambda b, g, i: (b, i, g)), pl.BlockSpec((1, lk, wide), lambda b, g, i: (b, 0, g)),
                      pl.BlockSpec((1, lk, lanes), lambda b, g, i: (b, 0, 0)), pl.BlockSpec((1, lk, wide), lambda b, g, i: (b, 0, g))],
            out_specs=[pl.BlockSpec((1, tq, heads_per_step * dv), lambda b, g, i: (b, i, g)),
                       pl.BlockSpec((1, heads_per_step, tq, 1), lambda b, g, i: (b, g, i, 0))],
            out_shape=[jax.ShapeDtypeStruct((bsz, lq, heads * dv), BF16), jax.ShapeDtypeStruct((bsz, heads, lq, 1), F32)],
            compiler_params=_params(("parallel", "parallel", "parallel")),
        )(q, kn, kr, v)

    def call_bwd(q, kn, kr, v, o, lse, do):
        bsz, lq, lk, tq = shapes(q, kn)

        def body(q_ref, kn_ref, kr_ref, v_ref, o_ref, lse_ref, do_ref, dq_ref, dkn_ref, dkr_ref, dv_ref):
            g, qi = pl.program_id(1), pl.program_id(2)

            @pl.when(qi == 0)
            def _():
                dkn_ref[...] = jnp.zeros_like(dkn_ref)
                dv_ref[...] = jnp.zeros_like(dv_ref)

            @pl.when((qi == 0) & (g == 0))
            def _():
                dkr_ref[...] = jnp.zeros_like(dkr_ref)

            dqs, dks, dvs = [], [], []
            for h in range(heads_per_step):
                kk = kn_ref[0, :, h * lanes:(h + 1) * lanes] + kr_ref[0]
                vv = v_ref[0, :, h * lanes:h * lanes + dv]
                dd = do_ref[0, :, h * dv:(h + 1) * dv]
                qs, s = scores(q_ref[0, :, h * lanes:(h + 1) * lanes], kk)
                p = jnp.exp2(s - lse_ref[0, h])
                delta = jnp.sum(dd.astype(F32) * o_ref[0, :, h * dv:(h + 1) * dv].astype(F32), axis=-1, keepdims=True)
                t = (p * (lax.dot_general(dd, vv, _NT, preferred_element_type=F32) - delta)).astype(BF16)
                dqs.append((jnp.dot(t, kk, preferred_element_type=F32) * scale).astype(BF16))
                dks.append(lax.dot_general(t, qs, _TN, preferred_element_type=F32) * LN_2)
                dv_h = lax.dot_general(p.astype(BF16), dd, _TN, preferred_element_type=F32)
                dvs.append(jnp.concatenate([dv_h, jnp.zeros((lk, lanes - dv), F32)], axis=1))
            dq_ref[0] = jnp.concatenate(dqs, axis=1)
            dkn_ref[0] += jnp.concatenate(dks, axis=1)
            dv_ref[0] += jnp.concatenate(dvs, axis=1)
            shared = dks[0]
            for d_h in dks[1:]:
                shared = shared + d_h
            dkr_ref[0] += shared

        wide = heads_per_step * lanes
        q_spec = pl.BlockSpec((1, tq, wide), lambda b, g, i: (b, i, g))
        k_spec = pl.BlockSpec((1, lk, wide), lambda b, g, i: (b, 0, g))
        kr_spec = pl.BlockSpec((1, lk, lanes), lambda b, g, i: (b, 0, 0))
        o_spec = pl.BlockSpec((1, tq, heads_per_step * dv), lambda b, g, i: (b, i, g))
        lse_spec = pl.BlockSpec((1, heads_per_step, tq, 1), lambda b, g, i: (b, g, i, 0))
        return _pcall(
            body, name=name + "_bwd", grid=(bsz, heads // heads_per_step, lq // tq),
            in_specs=[q_spec, k_spec, kr_spec, k_spec, o_spec, lse_spec, o_spec],
            out_specs=[q_spec, k_spec, kr_spec, k_spec],
            out_shape=[jax.ShapeDtypeStruct(q.shape, BF16), jax.ShapeDtypeStruct(kn.shape, F32), jax.ShapeDtypeStruct(kr.shape, F32),
                       jax.ShapeDtypeStruct(v.shape, F32)],
            compiler_params=_params(("parallel", "arbitrary", "arbitrary")),
        )(q, kn, kr, v, o, lse, do)

    @jax.custom_vjp
    def op(q, kn, kr, v):
        return call_fwd(q, kn, kr, v)[0]

    def fwd(q, kn, kr, v):
        o, lse = call_fwd(q, kn, kr, v)
        return o, (q, kn, kr, v, o, lse)

    def bwd(res, do):
        q, kn, kr, v, o, lse = res
        dq, dkn, dkr, dv_ = call_bwd(q, kn, kr, v, o, lse, do)
        return dq, dkn.astype(kn.dtype), dkr.astype(kr.dtype), dv_.astype(v.dtype)

    op.defvjp(fwd, bwd)
    return op


def _conv_call(ypad, taps, name):
    bsz, lp, ch = ypad.shape
    length = lp - 2 * ROW_ALIGN
    tl = _pick(length, (256, 128))

    def body(y_ref, w_ref, o_ref):
        base = pl.multiple_of(pl.program_id(1) * tl, tl)
        win = y_ref[0, pl.ds(base, tl + 2 * ROW_ALIGN), :]
        acc = jnp.broadcast_to(w_ref[pl.ds(D_CONV, 1), :], (tl, ch))
        for k in range(D_CONV):
            acc = acc + win[k:k + tl, :] * w_ref[pl.ds(k, 1), :]
        o_ref[0] = acc

    return _pcall(
        body, name=name, grid=(bsz, length // tl),
        in_specs=[pl.BlockSpec((1, lp, ch), lambda b, l: (b, 0, 0)), pl.BlockSpec((D_CONV + 1, ch), lambda b, l: (0, 0))],
        out_specs=pl.BlockSpec((1, tl, ch), lambda b, l: (b, l, 0)),
        out_shape=jax.ShapeDtypeStruct((bsz, length, ch), F32),
        compiler_params=_params(("parallel", "parallel")),
    )(ypad, taps)


def _conv_dw_call(ypad, dout, name):
    bsz, lp, ch = ypad.shape
    length = lp - 2 * ROW_ALIGN
    tl = _pick(length, (256, 128))

    def body(y_ref, d_ref, o_ref):
        b, l = pl.program_id(0), pl.program_id(1)

        @pl.when((b == 0) & (l == 0))
        def _():
            o_ref[...] = jnp.zeros_like(o_ref)

        base = pl.multiple_of(l * tl, tl)
        win = y_ref[0, pl.ds(base, tl + 2 * ROW_ALIGN), :]
        dd = d_ref[0]
        for k in range(D_CONV):
            o_ref[pl.ds(k, 1), :] += jnp.sum(win[k:k + tl, :] * dd, axis=0, keepdims=True)
        o_ref[pl.ds(D_CONV, 1), :] += jnp.sum(dd, axis=0, keepdims=True)

    return _pcall(
        body, name=name, grid=(bsz, length // tl),
        in_specs=[pl.BlockSpec((1, lp, ch), lambda b, l: (b, 0, 0)), pl.BlockSpec((1, tl, ch), lambda b, l: (b, l, 0))],
        out_specs=pl.BlockSpec((D_CONV + 1, ch), lambda b, l: (0, 0)),
        out_shape=jax.ShapeDtypeStruct((D_CONV + 1, ch), F32),
        compiler_params=_params(("arbitrary", "arbitrary")),
    )(ypad, dout)


def _pad_rows(y):
    return jnp.pad(y, ((0, 0), (CONV_PAD, 2 * ROW_ALIGN - CONV_PAD), (0, 0)))


@jax.custom_vjp
def _dwconv(y, taps):
    return _conv_call(_pad_rows(y), taps, "conv_fwd")


def _dwconv_fwd(y, taps):
    return _dwconv(y, taps), (y, taps)


def _dwconv_bwd(res, dout):
    y, taps = res
    flipped = jnp.concatenate([taps[:D_CONV][::-1], jnp.zeros_like(taps[D_CONV:])], axis=0)
    dy = _conv_call(_pad_rows(dout), flipped, "conv_dy")
    dtaps = _conv_dw_call(_pad_rows(y), dout, "conv_dw")
    return dy, dtaps


_dwconv.defvjp(_dwconv_fwd, _dwconv_bwd)


def _loss_head(x, target, g):
    bsz, length, d = x.shape
    tl = _pick(length, (256, 128))

    def f(xb, tb, gb):
        err = _rms(xb, gb) - tb
        return 0.5 * jnp.sum(jnp.sum(err * err, axis=-1, keepdims=True), axis=0, keepdims=True) / d

    def body(x_ref, t_ref, g_ref, loss_ref, dx_ref, dg_ref):
        val, vjp = jax.vjp(lambda xb, gb: f(xb, t_ref[0], gb), x_ref[0], g_ref[...])
        dx, dg = vjp(jnp.ones((1, 1), F32))
        dx_ref[0] = dx
        first = (pl.program_id(0) == 0) & (pl.program_id(1) == 0)

        @pl.when(first)
        def _():
            loss_ref[...] = val
            dg_ref[...] = dg

        @pl.when(jnp.logical_not(first))
        def _():
            loss_ref[...] += val
            dg_ref[...] += dg

    row = pl.BlockSpec((1, tl, d), lambda b, l: (b, l, 0))
    return _pcall(
        body, name="loss_head", grid=(bsz, length // tl),
        in_specs=[row, row, pl.BlockSpec((1, d), lambda b, l: (0, 0))],
        out_specs=[pl.BlockSpec((1, 1), lambda b, l: (0, 0)), row, pl.BlockSpec((1, d), lambda b, l: (0, 0))],
        out_shape=[jax.ShapeDtypeStruct((1, 1), F32), jax.ShapeDtypeStruct(x.shape, F32), jax.ShapeDtypeStruct((1, d), F32)],
        compiler_params=_params(("arbitrary", "arbitrary")),
    )(x, target, g)


def _adamw(w, g, m, v, name):
    shape = w.shape
    cols = shape[-1]
    rows = w.size // cols
    tr = _pick(rows, (512, 256, 128))
    w2, g2, m2, v2 = (t.reshape(rows, cols) for t in (w, g, m, v))

    def body(w_ref, g_ref, m_ref, v_ref, d_ref, nm_ref, nv_ref):
        gg = g_ref[...]
        nm = ADAM_B1 * m_ref[...] + (1.0 - ADAM_B1) * gg
        nv = ADAM_B2 * v_ref[...] + (1.0 - ADAM_B2) * jnp.square(gg)
        m_hat = nm / (1.0 - ADAM_B1 ** ADAM_STEP)
        v_hat = nv / (1.0 - ADAM_B2 ** ADAM_STEP)
        d_ref[...] = -ADAM_LR * (m_hat / (jnp.sqrt(v_hat) + ADAM_EPS) + ADAM_WD * w_ref[...])
        nm_ref[...] = nm
        nv_ref[...] = nv

    spec = pl.BlockSpec((tr, cols), lambda i: (i, 0))
    outs = _pcall(
        body, name=name, grid=(rows // tr,), in_specs=[spec] * 4, out_specs=[spec] * 3,
        out_shape=[jax.ShapeDtypeStruct((rows, cols), F32)] * 3,
        compiler_params=_params(("parallel",)),
    )(w2, g2, m2, v2)
    return tuple(o.reshape(shape) for o in outs)


def _mesh_pos():
    return lax.axis_index("x"), lax.axis_index("y"), lax.axis_index("c")


_RELATIONS = [(dx, dy, dc) for dx in (0, 1) for dy in (0, 1) for dc in (0, 1)][1:]


def _peer(pos, rel):
    return tuple(jnp.where(r == 1, 1 - p, p) if r else p for p, r in zip(pos, rel))


def _block_index(pos):
    return 4 * pos[0] + 2 * pos[1] + pos[2]


_HBM = pl.BlockSpec(memory_space=pltpu.HBM)


def _all_gather(xs, name):
    n = len(xs)

    def body(*refs):
        x_refs, out_refs, (send_sems, recv_sems, local_sems) = refs[:n], refs[n:2 * n], refs[2 * n:]
        x_, y_, c_ = _mesh_pos()
        me, sibling = (x_, y_, c_), (x_, y_, 1 - c_)
        chips = [(1 - x_, y_), (x_, 1 - y_), (1 - x_, 1 - y_)]

        def copy(t, k, block, to, own=False):
            slot = out_refs[t].at[_block_index(block)]
            return pltpu.make_async_remote_copy(
                src_ref=x_refs[t] if own else slot, dst_ref=slot, send_sem=send_sems.at[7 * t + k], recv_sem=recv_sems.at[7 * t + k],
                device_id=to, device_id_type=pl.DeviceIdType.MESH)

        mine = [pltpu.make_async_copy(x_refs[t], out_refs[t].at[_block_index(me)], local_sems.at[t]) for t in range(n)]
        first = [[copy(t, 0, me, sibling, own=True)] + [copy(t, 1 + j, me, (*chip, c_), own=True) for j, chip in enumerate(chips)]
                 for t in range(n)]
        passed = [[copy(t, 4 + j, (*chip, c_), sibling) for j, chip in enumerate(chips)] for t in range(n)]
        for t in range(n):
            mine[t].start()
            for cp in first[t]:
                cp.start()
        for t in range(n):
            for j, chip in enumerate(chips):
                copy(t, 1 + j, (*chip, c_), me).wait_recv()
                passed[t][j].start()
        for t in range(n):
            copy(t, 0, sibling, me).wait_recv()
            for j, chip in enumerate(chips):
                copy(t, 4 + j, (*chip, 1 - c_), me).wait_recv()
            for cp in first[t] + passed[t]:
                cp.wait_send()
            mine[t].wait()

    return _pcall(
        body, name=name, in_specs=[_HBM] * n, out_specs=[_HBM] * n,
        out_shape=[jax.ShapeDtypeStruct((N_DEV,) + x.shape, x.dtype) for x in xs],
        scratch_shapes=[pltpu.SemaphoreType.DMA((7 * n,)), pltpu.SemaphoreType.DMA((7 * n,)), pltpu.SemaphoreType.DMA((n,))],
    )(*xs)


def _exchange_blocks(ps, name):
    n = len(ps)

    def body(*refs):
        p_refs, out_refs, (send_sems, recv_sems, local_sems) = refs[:n], refs[n:2 * n], refs[2 * n:]
        me = _mesh_pos()

        def copy(t, k, rel, arriving=False):
            peer = _peer(me, rel)
            return pltpu.make_async_remote_copy(
                src_ref=p_refs[t].at[_block_index(peer)], dst_ref=out_refs[t].at[_block_index(peer if arriving else me)],
                send_sem=send_sems.at[7 * t + k], recv_sem=recv_sems.at[7 * t + k], device_id=peer, device_id_type=pl.DeviceIdType.MESH)

        mine = [pltpu.make_async_copy(p_refs[t].at[_block_index(me)], out_refs[t].at[_block_index(me)], local_sems.at[t]) for t in range(n)]
        sends = [copy(t, k, rel) for t in range(n) for k, rel in enumerate(_RELATIONS)]
        for cp in mine + sends:
            cp.start()
        for t in range(n):
            for k, rel in enumerate(_RELATIONS):
                copy(t, k, rel, arriving=True).wait_recv()
        for cp in sends:
            cp.wait_send()
        for cp in mine:
            cp.wait()

    return _pcall(
        body, name=name, in_specs=[_HBM] * n, out_specs=[_HBM] * n,
        out_shape=[jax.ShapeDtypeStruct(p.shape, p.dtype) for p in ps],
        scratch_shapes=[pltpu.SemaphoreType.DMA((7 * n,)), pltpu.SemaphoreType.DMA((7 * n,)), pltpu.SemaphoreType.DMA((n,))],
    )(*ps)


_SEM = pl.BlockSpec(memory_space=pltpu.SEMAPHORE)
_EFFECT = pltpu.SideEffectType.DATAFLOW_SIDE_EFFECTING


def _push_start(srcs, after, name):
    n = len(srcs)
    lands = [lax.empty((N_DEV,) + s.shape[-2:], s.dtype) for s in srcs]

    def body(*refs):
        src_refs, land_refs = refs[:n], refs[n:2 * n]
        send_sems, recv_sems, token = refs[2 * n + 1:3 * n + 1], refs[3 * n + 1:4 * n + 1], refs[-1]
        me = _mesh_pos()
        for t in range(n):
            for rel in _RELATIONS:
                peer = _peer(me, rel)
                pltpu.make_async_remote_copy(
                    src_ref=src_refs[t].at[_block_index(peer)] if srcs[t].ndim == 3 else src_refs[t], dst_ref=land_refs[t].at[_block_index(me)],
                    send_sem=send_sems[t], recv_sem=recv_sems[t], device_id=peer, device_id_type=pl.DeviceIdType.MESH).start()
        token[...] = jnp.zeros_like(token)

    outs = _pcall(
        body, name=name,
        out_shape=[pltpu.SemaphoreType.DMA(())] * (2 * n) + [pltpu.HBM(s.shape, s.dtype) for s in srcs]
        + [pltpu.HBM(l.shape, l.dtype) for l in lands] + [jax.ShapeDtypeStruct((8, 128), F32)],
        in_specs=[_HBM] * (2 * n) + [pl.BlockSpec(memory_space=pl.ANY)],
        out_specs=[_SEM] * (2 * n) + [_HBM] * (2 * n) + [pl.BlockSpec(memory_space=pltpu.VMEM)],
        input_output_aliases={i: 2 * n + i for i in range(2 * n)}, compiler_params=pltpu.CompilerParams(has_side_effects=_EFFECT),
    )(*[pltpu.with_memory_space_constraint(t, pltpu.HBM) for t in list(srcs) + lands], after)
    return (outs[:n], outs[n:2 * n], outs[2 * n:3 * n], outs[3 * n:4 * n]), outs[-1]


def _push_wait(handle, after, owns, me, name):
    send_sems, recv_sems, src_thrus, land_thrus = handle
    n = len(land_thrus)

    def body(*refs):
        land_refs, sends, recvs = refs[n:2 * n], refs[2 * n:3 * n], refs[3 * n:4 * n]
        for t in range(n):
            seven = land_refs[t].at[pl.ds(0, N_DEV - 1)]
            all_seven = pltpu.make_async_remote_copy(src_ref=seven, dst_ref=seven, send_sem=sends[t], recv_sem=recvs[t],
                                                     device_id=_mesh_pos(), device_id_type=pl.DeviceIdType.MESH)
            all_seven.wait_send()
            all_seven.wait_recv()

    outs = _pcall(
        body, name=name,
        out_shape=[pltpu.HBM(t.shape, t.dtype) for t in list(src_thrus) + list(land_thrus)],
        in_specs=[_HBM] * (2 * n) + [_SEM] * (2 * n) + [pl.BlockSpec(memory_space=pl.ANY)], out_specs=[_HBM] * (2 * n),
        input_output_aliases={i: i for i in range(2 * n)}, compiler_params=pltpu.CompilerParams(has_side_effects=_EFFECT),
    )(*src_thrus, *land_thrus, *send_sems, *recv_sems, after)
    return [lax.dynamic_update_slice(land, own[None], (me, 0, 0)) for land, own in zip(outs[n:], owns)]


def _sum_blocks(p, name):
    n, rows, cols = p.shape
    tr = _pick(rows, (256, 128, 64, PACK_ROWS, 16, 8))

    def body(p_ref, o_ref):
        acc = p_ref[0].astype(F32)
        for s in range(1, n):
            acc = acc + p_ref[s].astype(F32)
        o_ref[...] = acc

    return _pcall(
        body, name=name, grid=(rows // tr,),
        in_specs=[pl.BlockSpec((n, tr, cols), lambda i: (0, i, 0))], out_specs=pl.BlockSpec((tr, cols), lambda i: (i, 0)),
        out_shape=jax.ShapeDtypeStruct((rows, cols), F32), compiler_params=_params(("parallel",)),
    )(p)


def _sum_rows(t, name):
    def body(t_ref, o_ref):
        o_ref[...] = jnp.sum(t_ref[...], axis=0, keepdims=True)

    return _pcall(body, name=name, out_shape=jax.ShapeDtypeStruct((1, t.shape[1]), F32))(t)


class _Packing:
    def __init__(self, sizes, align, total_align=None):
        self.offsets, self.sizes, self.align = {}, dict(sizes), align
        row = 0
        for name, size in sizes:
            self.offsets[name] = row
            row += -(-size // (align * PACK_COLS)) * align
        total_align = total_align or align
        self.rows = -(-row // total_align) * total_align
        self.tail = self.rows - row

    def pack(self, pieces):
        return self.pack_blocks({n: pieces[n].reshape(1, -1) for n in self.sizes})[0]

    def pack_blocks(self, pieces):
        out = []
        for n, size in self.sizes.items():
            padded = -(-size // (self.align * PACK_COLS)) * self.align * PACK_COLS
            out.append(jnp.pad(pieces[n], ((0, 0), (0, padded - size))).reshape(pieces[n].shape[0], -1, PACK_COLS))
        if self.tail:
            out.append(jnp.zeros((out[0].shape[0], self.tail, PACK_COLS), out[0].dtype))
        return jnp.concatenate(out, axis=1)

    def piece(self, packed, name, lead=()):
        start, size = self.offsets[name], self.sizes[name]
        nrow = -(-size // (self.align * PACK_COLS)) * self.align
        sl = packed[..., start:start + nrow, :]
        return sl.reshape(lead + (nrow * PACK_COLS,))[..., :size]


_PIECES = (("ev_w_in", "ev_w_in", 0, 1, "a"), ("w_out0", "w_out", 0, 0, "a"),
           ("mlp_w1_0", "mlp_w1", 0, 1, "b"), ("mlp_w2_0", "mlp_w2", 0, 0, "b"),
           ("od_w_in", "od_w_in", 0, 1, "c"), ("od_w_uq", "od_w_uq", 0, 1, "c"), ("od_w_ukv", "od_w_ukv", 0, 1, "c"),
           ("w_out1", "w_out", 1, 0, "c"), ("mlp_w1_1", "mlp_w1", 1, 1, "c"), ("mlp_w2_1", "mlp_w2", 1, 0, "c"))
_SMALL_SHARDED = (("od_q_norm_g", 1), ("od_conv_w", 2), ("od_conv_b", 1), ("od_ln_g", 1), ("od_ln_b", 1))
_REPLICATED = ("c_ctx", "norm1_g", "norm2_g", "ev_q_norm_g", "ev_k_norm_g", "ev_sgu_norm_g", "ev_sgu_w", "ev_sgu_b",
               "od_kv_norm_g", "final_g")


def _unshard(blocks, axis):
    moved = jnp.moveaxis(blocks, 0, axis)
    shape = moved.shape
    return moved.reshape(shape[:axis] + (shape[axis] * shape[axis + 1],) + shape[axis + 2:])


def _shard_blocks(full, axis):
    shape = full.shape
    split = full.reshape(shape[:axis] + (N_DEV, shape[axis] // N_DEV) + shape[axis + 1:])
    return jnp.moveaxis(split, axis, 0)


def _group_mean_matrix(width, group):
    idx = jnp.arange(width) // group
    return (idx[:, None] == idx[None, :]).astype(F32) / group


def _swap_matrix(width):
    idx = jnp.arange(width)
    return ((idx[:, None] ^ 1) == idx[None, :]).astype(F32)


def _angles(length, d_rot):
    rows = length // GRID_W
    row = jnp.broadcast_to(jnp.arange(rows)[:, None], (rows, GRID_W)).reshape(-1).astype(F32)
    col = jnp.broadcast_to(jnp.arange(GRID_W)[None, :], (rows, GRID_W)).reshape(-1).astype(F32)
    d_axis = d_rot // 2
    inv = ROPE_THETA ** (-jnp.arange(0, d_axis, 2, dtype=F32) / d_axis)
    return jnp.concatenate([row[:, None] * inv, col[:, None] * inv], axis=-1)


def _rope_tables(length, d_rot, head_dim, heads, ctx_len, tail=0):
    ang = _angles(length, d_rot)
    cos = jnp.repeat(jnp.cos(ang), 2, axis=1)
    sin = jnp.repeat(jnp.sin(ang), 2, axis=1) * jnp.tile(jnp.array([-1.0, 1.0], F32), d_rot // 2)
    keep = head_dim - d_rot - tail
    cos = jnp.concatenate([jnp.ones((length, keep), F32), cos, jnp.ones((length, tail), F32)], axis=1)
    sin = jnp.concatenate([jnp.zeros((length, keep), F32), sin, jnp.zeros((length, tail), F32)], axis=1)
    cos, sin = jnp.tile(cos, (1, heads)), jnp.tile(sin, (1, heads))
    cos = jnp.concatenate([jnp.ones((ctx_len, cos.shape[1]), F32), cos], axis=0)
    sin = jnp.concatenate([jnp.zeros((ctx_len, sin.shape[1]), F32), sin], axis=0)
    return cos, sin


def _to_heads(t, heads):
    b, l, w = t.shape
    return t.reshape(b, l, heads, w // heads).transpose(0, 2, 1, 3)


def _from_heads(t):
    b, h, l, d = t.shape
    return t.transpose(0, 2, 1, 3).reshape(b, l, h * d)


def _segment_params(mod, bsz):
    parts = jnp.split(mod, N_MOD, axis=-1)
    out = []
    for part in parts:
        lat = part[:bsz]
        ctx = jnp.broadcast_to(part[bsz:bsz + 1], lat.shape)
        out.append(jnp.stack([ctx, lat], axis=1)[:, :, None, :])
    return out


def _flat(t):
    return t.reshape(-1, t.shape[-1])


def _sequence_rowwise(ctx_len):
    tl = math.gcd(256, ctx_len)

    def make(name, f, out_specs, rows_per_block=tl, ctx_blocks=ctx_len // tl):
        return _rowwise(name, f, out_specs, rows_per_block, ctx_blocks)

    return make


def _mixer0(xall, modrows0, w, bsz, length, ctx_len):
    d = xall.shape[-1]
    total = ctx_len + length
    rowwise, flat = _sequence_rowwise(ctx_len), _flat
    sh1, sc1, g1, sh2, sc2, _ = _segment_params(modrows0, bsz)
    (h,) = rowwise("mod0", _f_modulate, [(d, BF16)])((xall,), (), (sh1, sc1), (w["norm1_g0"],), ())
    ev_q, ev_kv = A_Q_HEADS * A_HEAD_DIM, A_KV_HEADS * A_HEAD_DIM
    half = B_GROUPS * B_GROUP_DIM
    groups = tuple(jnp.split(w["ev_w_in"], [ev_q, ev_q + ev_kv, ev_q + 2 * ev_kv, ev_q + 2 * ev_kv + half], axis=0))
    qp, kp, vp, zu, zv = [t.reshape(bsz, total, -1) for t in _linear_multi("ev_in", (F32, F32, BF16, F32, F32))(flat(h), groups)]
    cos_q, sin_q = _rope_tables(length, A_HEAD_DIM, A_HEAD_DIM, A_Q_HEADS, ctx_len)
    cos_k, sin_k = cos_q[:, :ev_kv], sin_q[:, :ev_kv]
    (q,) = rowwise("ev_q", _f_headnorm_rope, [(ev_q, BF16)])(
        (qp,), (cos_q, sin_q), (), (jnp.tile(w["ev_q_norm_g"][0], A_Q_HEADS)[None],), (_group_mean_matrix(ev_q, A_HEAD_DIM),))
    (k,) = rowwise("ev_k", _f_headnorm_rope, [(ev_kv, BF16)])(
        (kp,), (cos_k, sin_k), (), (jnp.tile(w["ev_k_norm_g"][0], A_KV_HEADS)[None],), (_group_mean_matrix(ev_kv, A_HEAD_DIM),))
    o_att = _attention_rows("gqa", ctx_len, A_Q_HEADS, A_KV_HEADS)(q, k, vp)
    u, vn = rowwise("sgu_pre", _f_sgu_pre, [(half, F32), (half, BF16)])(
        (zu, zv), (), (), (w["ev_sgu_norm_g"][0].reshape(1, half),), (_group_mean_matrix(half, B_GROUP_DIM),))
    bias = jnp.repeat(w["ev_sgu_b"][0].T, B_GROUP_DIM, axis=1)
    (o_sgu,) = rowwise("sgu_mix", _f_sgu_mix, [(half, BF16)], rows_per_block=B_CHUNK, ctx_blocks=0)(
        (u, vn), (), (), tuple(w["ev_sgu_w"][0][g] for g in range(B_GROUPS)) + (bias,), ())
    y = _linear_sum("out0")((flat(o_att), flat(o_sgu)), tuple(jnp.split(w["w_out0"], 2, axis=0))).reshape(bsz, total, d)
    x1, h = rowwise("res_mod0a", _f_res_modulate, [(d, F32), (d, BF16)])((xall, y), (), (g1, sh2, sc2), (w["norm2_g0"],), ())
    return x1, h


def _mlp0(x1, h, modrows0, modrows1, w, bsz, length, ctx_len):
    d = x1.shape[-1]
    total = ctx_len + length
    g2 = _segment_params(modrows0, bsz)[5]
    sh1, sc1 = _segment_params(modrows1, bsz)[:2]
    y = _mlp("mlp0")(_flat(h), w["mlp_w1_0"], w["mlp_w2_0"]).reshape(bsz, total, d)
    return _sequence_rowwise(ctx_len)("res_mod0b", _f_res_modulate, [(d, F32), (d, BF16)])((x1, y), (), (g2, sh1, sc1), (w["norm1_g1"],), ())


def _layer1(x2, h, modrows1, w, bsz, length, ctx_len):
    d = x2.shape[-1]
    total = ctx_len + length
    half = B_GROUPS * B_GROUP_DIM
    rowwise, flat = _sequence_rowwise(ctx_len), _flat
    _, _, g1n, sh2n, sc2n, g2n = _segment_params(modrows1, bsz)
    g_cq, g_ckv, g_kr, g_za, g_zg = jnp.split(w["od_w_in"], [C_Q_RANK, C_Q_RANK + C_KV_RANK, C_Q_RANK + C_KV_RANK + C_ROPE,
                                                             C_Q_RANK + C_KV_RANK + C_ROPE + half], axis=0)
    lanes, c_qk = 128, C_NOPE + C_ROPE
    g_kr = jnp.pad(g_kr, ((C_NOPE, lanes - c_qk), (0, 0)))
    cq, ckv, kr, za, zg = [t.reshape(bsz, total, -1) for t in _linear_multi("od_in", (F32,) * 5)(flat(h), (g_cq, g_ckv, g_kr, g_za, g_zg))]
    lat = slice(ctx_len, total)
    lat_tl = math.gcd(256, length)
    (cqn,) = _rowwise("od_qn", _f_rms, [(C_Q_RANK, BF16)], lat_tl)((cq[:, lat],), (), (), (w["od_q_norm_g"],), ())
    w_uq = jnp.pad(w["od_w_uq"].reshape(C_HEADS, c_qk, C_Q_RANK), ((0, 0), (0, lanes - c_qk), (0, 0))).reshape(C_HEADS * lanes, C_Q_RANK)
    qf = _linear("od_uq", transposed=True)(flat(cqn), w_uq).reshape(bsz, length, C_HEADS * lanes)
    cos_q, sin_q = _rope_tables(length, C_ROPE, lanes, C_HEADS, 0, tail=lanes - c_qk)
    (q,) = _rowwise("od_qrope", _f_rope, [(C_HEADS * lanes, BF16)], lat_tl)((qf,), (cos_q, sin_q), (), (), ())
    (ckvn,) = rowwise("od_kvn", _f_rms, [(C_KV_RANK, BF16)])((ckv,), (), (), (w["od_kv_norm_g"],), ())
    per_head = w["od_w_ukv"].reshape(C_HEADS, C_NOPE + C_V, C_KV_RANK)
    w_kn = jnp.pad(per_head[:, :C_NOPE], ((0, 0), (0, lanes - C_NOPE), (0, 0))).reshape(C_HEADS * lanes, C_KV_RANK)
    w_v = jnp.pad(per_head[:, C_NOPE:], ((0, 0), (0, lanes - C_V), (0, 0))).reshape(C_HEADS * lanes, C_KV_RANK)
    kn, vv = [t.reshape(bsz, total, -1) for t in _linear_multi("od_ukv", (BF16, BF16))(flat(ckvn), (w_kn, w_v))]
    cos_r, sin_r = _rope_tables(length, C_ROPE, lanes, 1, ctx_len, tail=lanes - c_qk)
    (krr,) = rowwise("od_krope", _f_rope, [(lanes, BF16)])((kr,), (cos_r, sin_r), (), (), ())
    o_att = _attention_latent_rows("mla", C_HEADS, C_NOPE, C_ROPE, C_V, 2)(q, kn, krr, vv)
    (glu,) = _rowwise("glu", _f_glu, [(half, F32)], lat_tl)((za[:, lat], zg[:, lat]), (), (), (), ())
    taps = jnp.concatenate([w["od_conv_w"][0], w["od_conv_b"]], axis=0)
    conv = _dwconv(glu, taps)
    (o_conv,) = _rowwise("ln_silu", _f_ln_silu, [(half, BF16)], lat_tl)((conv,), (), (), (w["od_ln_g"], w["od_ln_b"]), ())
    y = _linear_sum("out1")((flat(o_att), flat(o_conv)), tuple(jnp.split(w["w_out1"], 2, axis=0))).reshape(bsz, length, d)
    lat_param = lambda p: p[:, 1:]
    x3, h = _rowwise("res_mod1a", _f_res_modulate, [(d, F32), (d, BF16)], lat_tl)(
        (x2[:, lat], y), (), (lat_param(g1n), lat_param(sh2n), lat_param(sc2n)), (w["norm2_g1"],), ())
    y = _mlp("mlp1")(flat(h), w["mlp_w1_1"], w["mlp_w2_1"]).reshape(bsz, length, d)
    (x4,) = _rowwise("res1b", _f_res, [(d, F32)], lat_tl)((x3, y), (), (lat_param(g2n),), (), ())
    return x4


def kernel(x, c, ctx, c_ctx, ada_w, ada_b, norm1_g, norm2_g, w_out, mlp_w1, mlp_w2, ev_w_in, ev_q_norm_g, ev_k_norm_g, ev_sgu_norm_g, ev_sgu_w, ev_sgu_b, od_w_in, od_q_norm_g, od_kv_norm_g, od_w_uq, od_w_ukv, od_conv_w, od_conv_b, od_ln_g, od_ln_b, final_g, loss_target, m_c_ctx, m_ada_w, m_ada_b, m_norm1_g, m_norm2_g, m_w_out, m_mlp_w1, m_mlp_w2, m_ev_w_in, m_ev_q_norm_g, m_ev_k_norm_g, m_ev_sgu_norm_g, m_ev_sgu_w, m_ev_sgu_b, m_od_w_in, m_od_q_norm_g, m_od_kv_norm_g, m_od_w_uq, m_od_w_ukv, m_od_conv_w, m_od_conv_b, m_od_ln_g, m_od_ln_b, m_final_g, v_c_ctx, v_ada_w, v_ada_b, v_norm1_g, v_norm2_g, v_w_out, v_mlp_w1, v_mlp_w2, v_ev_w_in, v_ev_q_norm_g, v_ev_k_norm_g, v_ev_sgu_norm_g, v_ev_sgu_w, v_ev_sgu_b, v_od_w_in, v_od_q_norm_g, v_od_kv_norm_g, v_od_w_uq, v_od_w_ukv, v_od_conv_w, v_od_conv_b, v_od_ln_g, v_od_ln_b, v_final_g):
    names = ["c_ctx", "ada_w", "ada_b", "norm1_g", "norm2_g", "w_out", "mlp_w1", "mlp_w2", "ev_w_in", "ev_q_norm_g", "ev_k_norm_g",
             "ev_sgu_norm_g", "ev_sgu_w", "ev_sgu_b", "od_w_in", "od_q_norm_g", "od_kv_norm_g", "od_w_uq", "od_w_ukv", "od_conv_w",
             "od_conv_b", "od_ln_g", "od_ln_b", "final_g"]
    local = dict(zip(names, [c_ctx, ada_w, ada_b, norm1_g, norm2_g, w_out, mlp_w1, mlp_w2, ev_w_in, ev_q_norm_g, ev_k_norm_g, ev_sgu_norm_g, ev_sgu_w, ev_sgu_b, od_w_in, od_q_norm_g, od_kv_norm_g, od_w_uq, od_w_ukv, od_conv_w, od_conv_b, od_ln_g, od_ln_b, final_g]))
    mom1 = dict(zip(names, [m_c_ctx, m_ada_w, m_ada_b, m_norm1_g, m_norm2_g, m_w_out, m_mlp_w1, m_mlp_w2, m_ev_w_in, m_ev_q_norm_g, m_ev_k_norm_g, m_ev_sgu_norm_g, m_ev_sgu_w, m_ev_sgu_b, m_od_w_in, m_od_q_norm_g, m_od_kv_norm_g, m_od_w_uq, m_od_w_ukv, m_od_conv_w, m_od_conv_b, m_od_ln_g, m_od_ln_b, m_final_g]))
    mom2 = dict(zip(names, [v_c_ctx, v_ada_w, v_ada_b, v_norm1_g, v_norm2_g, v_w_out, v_mlp_w1, v_mlp_w2, v_ev_w_in, v_ev_q_norm_g, v_ev_k_norm_g, v_ev_sgu_norm_g, v_ev_sgu_w, v_ev_sgu_b, v_od_w_in, v_od_q_norm_g, v_od_kv_norm_g, v_od_w_uq, v_od_w_ukv, v_od_conv_w, v_od_conv_b, v_od_ln_g, v_od_ln_b, v_final_g]))
    bsz, length, d = x.shape
    ctx_len = ctx.shape[1]
    me = _block_index(_mesh_pos())

    shard = {p: local[wn][layer] for p, wn, layer, _, _ in _PIECES}
    by_columns = {p: axis == 1 for p, _, _, axis, _ in _PIECES}
    stages = {s: [p for p, _, _, _, st in _PIECES if st == s] for s in "abc"}
    block_rows = {p: shard[p].shape[1] if by_columns[p] else shard[p].shape[0] for p in shard}

    def pad_block_rows(t, p):
        extra = -block_rows[p] % ROW_ALIGN
        return jnp.pad(t, [(0, 0)] * (t.ndim - 2) + [(0, extra), (0, 0)]) if extra else t

    def travelling(p):
        t = shard[p].astype(BF16)
        return pad_block_rows(t.T if by_columns[p] else t, p)

    mine = {s: [travelling(p) for p in stages[s]] for s in "abc"}
    tiny_pack = _Packing([(n, local[n].size) for n, _ in _SMALL_SHARDED], 8)

    def unpack_weights(s, gathered):
        return {p: g[:, :block_rows[p]].reshape(N_DEV * block_rows[p], g.shape[2]) for p, g in zip(stages[s], gathered)}

    def pack_grads(s, g):
        return [pad_block_rows(g[p].reshape(N_DEV, block_rows[p], g[p].shape[1]), p) for p in stages[s]]

    cond_local = jnp.concatenate([c, c_ctx[None], jnp.zeros((COND_ROWS - bsz - 1, d), F32)], axis=0)
    cond, gathered_tiny, *gathered_a = _all_gather(
        [cond_local, tiny_pack.pack({n: local[n] for n, _ in _SMALL_SHARDED})] + mine["a"], "gather_inputs")
    cond = cond.reshape(N_DEV * COND_ROWS, d)
    silu_op = _rowwise("silu", _f_silu, [(d, F32)], N_DEV * COND_ROWS)
    silu_rows, silu_pullback = jax.vjp(lambda r: silu_op((r[None],), (), (), (), ())[0][0], cond)
    mod_cols = ada_w.shape[2]
    mod_part = jnp.concatenate([_mm(silu_rows, ada_w[i], "nn", (F32,), f"ada{i}_fwd")[0] for i in range(2)], axis=0)
    (mod_all,) = _all_gather([mod_part], "gather_mod")
    mod_all = mod_all.reshape(N_DEV, 2, N_DEV * COND_ROWS, mod_cols)
    modrows = []
    for i in range(2):
        whole = mod_all[:, i].transpose(1, 0, 2).reshape(N_DEV * COND_ROWS, N_DEV * mod_cols) + ada_b[i]
        modrows.append(lax.dynamic_slice_in_dim(whole, me * COND_ROWS, COND_ROWS, axis=0)[:bsz + 1])

    weights_a = unpack_weights("a", gathered_a)
    gather_b, token_b = _push_start(mine["b"], mod_all, "gather_weights_b_start")
    gather_c, token_c = _push_start(mine["c"], token_b, "gather_weights_c_start")
    full = {n: local[n] for n in _REPLICATED}
    for n, axis in _SMALL_SHARDED:
        full[n] = _unshard(tiny_pack.piece(gathered_tiny, n, (N_DEV,)).reshape((N_DEV,) + local[n].shape), axis)

    xall = jnp.concatenate([ctx, x], axis=1)
    modrows0, modrows1 = modrows[0] + token_c[0, 0], modrows[1]
    w_a = dict(weights_a, norm1_g0=norm1_g[0][None], norm2_g0=norm2_g[0][None],
               **{n: full[n] for n in ("ev_q_norm_g", "ev_k_norm_g", "ev_sgu_norm_g", "ev_sgu_w", "ev_sgu_b")})
    (x1, h0), pull_a = jax.vjp(lambda x_, m0, w: _mixer0(x_, m0, w, bsz, length, ctx_len), xall, modrows0, w_a)
    w_b = dict(unpack_weights("b", _push_wait(gather_b, x1, mine["b"], me, "gather_weights_b_wait")), norm1_g1=norm1_g[1][None])
    (x2, h1), pull_b = jax.vjp(lambda x_, h_, m0, m1, w: _mlp0(x_, h_, m0, m1, w, bsz, length, ctx_len), x1, h0, modrows0, modrows1, w_b)
    w_c = dict(unpack_weights("c", _push_wait(gather_c, x2, mine["c"], me, "gather_weights_c_wait")), norm2_g1=norm2_g[1][None],
               **{n: full[n] for n in ("od_q_norm_g", "od_kv_norm_g", "od_conv_w", "od_conv_b", "od_ln_g", "od_ln_b")})
    x4, pull_c = jax.vjp(lambda x_, h_, m1, w: _layer1(x_, h_, m1, w, bsz, length, ctx_len), x2, h1, modrows1, w_c)
    loss_part, dx4, dfinal = _loss_head(x4, loss_target, final_g[None])
    loss = lax.psum(loss_part[0, 0], MESH_AXES)

    dx2, dh1, dmod1_c, g_c = pull_c(dx4)
    grads_c = pack_grads("c", g_c)
    exchange_c, token = _push_start(grads_c, dx2, "exchange_grads_c_start")
    dx1, dh0, dmod0_b, dmod1_b, g_b = pull_b((dx2, dh1 + token[0, 0].astype(dh1.dtype)))
    grads_b = pack_grads("b", g_b)
    exchange_b, token = _push_start(grads_b, dx1, "exchange_grads_b_start")
    dxall, dmod0_a, g_a = pull_a((dx1, dh0 + token[0, 0].astype(dh0.dtype)))
    grad_x = dxall[:, ctx_len:]
    dmodrows = [dmod0_a + dmod0_b, dmod1_b + dmod1_c]
    grads = {n: g[n] for g in (g_a, g_c) for n in g if n in full}
    grads["norm1_g"] = jnp.concatenate([g_a["norm1_g0"], g_b["norm1_g1"]], axis=0)
    grads["norm2_g"] = jnp.concatenate([g_a["norm2_g0"], g_c["norm2_g1"]], axis=0)
    grads["final_g"] = dfinal[0]

    dmod_local = jnp.concatenate([jnp.pad(dm, ((0, COND_ROWS - bsz - 1), (0, 0))) for dm in dmodrows], axis=0)
    (dmod_all,) = _all_gather([dmod_local], "gather_dmod")
    dmod_all = dmod_all.reshape(N_DEV, 2, COND_ROWS, N_DEV * mod_cols)
    reduced = {}
    grad_ada_w, grad_ada_b, dmod_mine = [], [], []
    for i in range(2):
        dmod = dmod_all[:, i].reshape(N_DEV * COND_ROWS, N_DEV * mod_cols)
        grad_ada_b.append(_sum_rows(dmod, f"ada{i}_db")[0])
        dmod_mine.append(lax.dynamic_slice_in_dim(dmod, me * mod_cols, mod_cols, axis=1))
        grad_ada_w.append(_mm(silu_rows, dmod_mine[i], "tn", (F32,), f"ada{i}_dw")[0])
    reduced["ada_w"], reduced["ada_b"] = jnp.stack(grad_ada_w), jnp.stack(grad_ada_b)
    dsilu = _mm(jnp.concatenate(dmod_mine, axis=1), jnp.concatenate([ada_w[0], ada_w[1]], axis=1), "nt", (F32,), "ada_dx")[0]
    (dcond,) = silu_pullback(dsilu)
    grads["c_ctx"] = _sum_rows(dcond.reshape(N_DEV, COND_ROWS, d)[:, bsz], "c_ctx_rows")[0]

    def own_blocks(blocks):
        return [lax.dynamic_index_in_dim(t, me, 0, keepdims=False) for t in blocks]

    received = {"a": _exchange_blocks(pack_grads("a", g_a), "exchange_grads_a"),
                "b": _push_wait(exchange_b, dmod_all, own_blocks(grads_b), me, "exchange_grads_b_wait"),
                "c": _push_wait(exchange_c, dmod_all, own_blocks(grads_c), me, "exchange_grads_c_wait")}
    piece_grad = {}
    for s in "abc":
        for p, blocks in zip(stages[s], received[s]):
            summed = _sum_blocks(blocks, "sum_grads_" + p)[:block_rows[p]]
            piece_grad[p] = summed.T if by_columns[p] else summed
    for n in ("w_out", "mlp_w1", "mlp_w2"):
        reduced[n] = jnp.stack([piece_grad[p] for p, wn, _, _, _ in _PIECES if wn == n])
    for n in ("ev_w_in", "od_w_in", "od_w_uq", "od_w_ukv"):
        reduced[n] = piece_grad[n][None]

    small_names = list(_REPLICATED) + [n for n, _ in _SMALL_SHARDED]
    small_pack = _Packing([(n, full[n].size) for n in small_names], 8)
    (small_all,) = _all_gather([small_pack.pack({n: grads[n].astype(F32) for n in small_names})], "gather_small_grads")
    small_sum = _sum_blocks(small_all, "sum_small_grads")
    for n in _REPLICATED:
        reduced[n] = small_pack.piece(small_sum, n).reshape(local[n].shape)
    for n, axis in _SMALL_SHARDED:
        whole = small_pack.piece(small_sum, n).reshape(full[n].shape)
        reduced[n] = lax.dynamic_slice_in_dim(whole, me * local[n].shape[axis], local[n].shape[axis], axis=axis)

    delta, new_m, new_v = {}, {}, {}
    for n in names:
        delta[n], new_m[n], new_v[n] = _adamw(local[n], reduced[n], mom1[n], mom2[n], "adamw_" + n)
    return (loss, grad_x, *[reduced[n] for n in names], *[delta[n] for n in names], *[new_m[n] for n in names], *[new_v[n] for n in names])
```

```python
import functools
import math

import jax
import jax.numpy as jnp
from jax import lax
from jax.experimental import pallas as pl
from jax.experimental.pallas import tpu as pltpu

F32, BF16 = jnp.float32, jnp.bfloat16

EPS = 1e-6
GRID_W = 64
ROPE_THETA = 10000.0
A_HEAD_DIM, A_Q_HEADS, A_KV_HEADS = 64, 8, 2
B_GROUPS, B_GROUP_DIM, B_CHUNK = 8, 64, 128
C_HEADS, C_NOPE, C_ROPE, C_V, C_Q_RANK, C_KV_RANK = 8, 64, 32, 64, 256, 128
D_CONV = 31
CONV_PAD = D_CONV // 2
N_MOD = 6
N_DEV = 8
MESH_AXES = ("x", "y", "c")

ADAM_LR, ADAM_B1, ADAM_B2, ADAM_EPS, ADAM_WD, ADAM_STEP = 0.001, 0.9, 0.999, 1e-08, 0.01, 10

VMEM_LIMIT = 56 * 1024 * 1024
PACK_COLS = 1024
ROW_ALIGN = 16
PACK_ROWS = 32
COND_ROWS = 8


def _pcall(body, **kw):
    return pl.pallas_call(body, **kw)


def _params(sem=None):
    return pltpu.CompilerParams(dimension_semantics=sem, vmem_limit_bytes=VMEM_LIMIT)


def _pick(n, cands):
    for c in cands:
        if n % c == 0:
            return c
    return n


def _mm(a, b, mode, out_dtypes, name, epi=None, extras=()):
    if mode == "tn":
        kk, m = a.shape
    else:
        m, kk = a.shape
    n = b.shape[0] if mode == "nt" else b.shape[1]
    tm = _pick(m, (1152, 1024, 896, 768, 512, 256, 128))
    tn = _pick(n, (1024, 896, 768, 512, 256, 128))
    tk = kk if kk <= 1024 else _pick(kk, (1024, 896, 768, 512, 256, 128))
    nk = kk // tk
    ne, no = len(extras), len(out_dtypes)
    a_spec = pl.BlockSpec((tk, tm), lambda i, j, k: (k, i)) if mode == "tn" else pl.BlockSpec((tm, tk), lambda i, j, k: (i, k))
    b_spec = pl.BlockSpec((tn, tk), lambda i, j, k: (j, k)) if mode == "nt" else pl.BlockSpec((tk, tn), lambda i, j, k: (k, j))
    t_spec = pl.BlockSpec((tm, tn), lambda i, j, k: (i, j))
    dn = {"nn": ((1,), (0,)), "nt": ((1,), (1,)), "tn": ((0,), (0,))}[mode]

    def body(a_ref, b_ref, *rest):
        extra_refs, out_refs = rest[:ne], rest[ne:ne + no]

        def finish(acc):
            outs = (acc,) if epi is None else epi(acc, *[r[...] for r in extra_refs])
            for r, o in zip(out_refs, outs):
                r[...] = o.astype(r.dtype)

        part = lax.dot_general(a_ref[...].astype(BF16), b_ref[...].astype(BF16), (dn, ((), ())), preferred_element_type=F32)
        if nk == 1:
            finish(part)
        else:
            acc_ref = rest[-1]
            k = pl.program_id(2)

            @pl.when(k == 0)
            def _():
                acc_ref[...] = part

            @pl.when(k > 0)
            def _():
                acc_ref[...] += part

            @pl.when(k == nk - 1)
            def _():
                finish(acc_ref[...])

    outs = _pcall(
        body, name=name, grid=(m // tm, n // tn, nk),
        in_specs=[a_spec, b_spec] + [t_spec] * ne,
        out_specs=[t_spec] * no,
        out_shape=[jax.ShapeDtypeStruct((m, n), d) for d in out_dtypes],
        scratch_shapes=[pltpu.VMEM((tm, tn), F32)] if nk > 1 else [],
        compiler_params=_params(("parallel", "parallel", "arbitrary")),
    )(a, b, *extras)
    return outs


def _linear(name, transposed=False, out_dtype=F32):
    fwd_mode, dx_mode = ("nt", "nn") if transposed else ("nn", "nt")

    @jax.custom_vjp
    def op(x, w):
        return _mm(x, w, fwd_mode, (out_dtype,), name + "_fwd")[0]

    def fwd(x, w):
        return op(x, w), (x, w)

    def bwd(res, dy):
        x, w = res
        dx = _mm(dy, w, dx_mode, (x.dtype,), name + "_dx")[0]
        dw = _mm(dy, x, "tn", (w.dtype,), name + "_dw")[0] if transposed else _mm(x, dy, "tn", (w.dtype,), name + "_dw")[0]
        return dx, dw

    op.defvjp(fwd, bwd)
    return op


_NT, _NN, _TN = (((1,), (1,)), ((), ())), (((1,), (0,)), ((), ())), (((0,), (0,)), ((), ()))
_ROW_TILES = (1152, 1024, 768, 512, 256, 128)


def _whole(w):
    return pl.BlockSpec(w.shape, lambda i: (0, 0))


def _groups_apply(x, ws, out_dtypes, name):
    n, (m, kk) = len(ws), x.shape
    tm = _pick(m, _ROW_TILES)

    def body(x_ref, *refs):
        a = x_ref[...].astype(BF16)
        for w_ref, o_ref in zip(refs[:n], refs[n:]):
            o_ref[...] = lax.dot_general(a, w_ref[...], _NT, preferred_element_type=F32).astype(o_ref.dtype)

    return _pcall(
        body, name=name, grid=(m // tm,),
        in_specs=[pl.BlockSpec((tm, kk), lambda i: (i, 0))] + [_whole(w) for w in ws],
        out_specs=[pl.BlockSpec((tm, w.shape[0]), lambda i: (i, 0)) for w in ws],
        out_shape=[jax.ShapeDtypeStruct((m, w.shape[0]), dt) for w, dt in zip(ws, out_dtypes)],
        compiler_params=_params(("parallel",)),
    )(x, *ws)


def _groups_sum(xs, ws, out_dtype, name):
    n, m, kk = len(ws), xs[0].shape[0], ws[0].shape[1]
    tm = _pick(m, _ROW_TILES)

    def body(*refs):
        acc = None
        for x_ref, w_ref in zip(refs[:n], refs[n:2 * n]):
            part = lax.dot_general(x_ref[...].astype(BF16), w_ref[...], _NN, preferred_element_type=F32)
            acc = part if acc is None else acc + part
        refs[2 * n][...] = acc.astype(out_dtype)

    return _pcall(
        body, name=name, grid=(m // tm,),
        in_specs=[pl.BlockSpec((tm, w.shape[0]), lambda i: (i, 0)) for w in ws] + [_whole(w) for w in ws],
        out_specs=pl.BlockSpec((tm, kk), lambda i: (i, 0)), out_shape=jax.ShapeDtypeStruct((m, kk), out_dtype),
        compiler_params=_params(("parallel",)),
    )(*xs, *ws)


def _groups_outer(xs, y, ws, name):
    n, (m, kk) = len(ws), y.shape
    tk = _pick(m, (768, 512, 256, 128))
    steps = m // tk

    def body(y_ref, *refs):
        x_refs, o_refs, acc_refs = refs[:n], refs[n:2 * n], refs[2 * n:]
        k = pl.program_id(0)
        b = y_ref[...].astype(BF16)
        for x_ref, o_ref, acc_ref in zip(x_refs, o_refs, acc_refs):
            part = lax.dot_general(x_ref[...].astype(BF16), b, _TN, preferred_element_type=F32)

            @pl.when(k == 0)
            def _(acc_ref=acc_ref, part=part):
                acc_ref[...] = part

            @pl.when(k > 0)
            def _(acc_ref=acc_ref, part=part):
                acc_ref[...] += part

            @pl.when(k == steps - 1)
            def _(acc_ref=acc_ref, o_ref=o_ref):
                o_ref[...] = acc_ref[...].astype(o_ref.dtype)

    return _pcall(
        body, name=name, grid=(steps,),
        in_specs=[pl.BlockSpec((tk, kk), lambda k: (k, 0))] + [pl.BlockSpec((tk, w.shape[0]), lambda k: (k, 0)) for w in ws],
        out_specs=[_whole(w) for w in ws], out_shape=[jax.ShapeDtypeStruct(w.shape, w.dtype) for w in ws],
        scratch_shapes=[pltpu.VMEM(w.shape, F32) for w in ws], compiler_params=_params(("arbitrary",)),
    )(y, *xs)


def _linear_multi(name, out_dtypes):
    @jax.custom_vjp
    def op(x, wts):
        return tuple(_groups_apply(x, wts, out_dtypes, name + "_fwd"))

    def fwd(x, wts):
        return op(x, wts), (x, wts)

    def bwd(res, dys):
        x, wts = res
        return _groups_sum(dys, wts, x.dtype, name + "_dx"), tuple(_groups_outer(dys, x, wts, name + "_dw"))

    op.defvjp(fwd, bwd)
    return op


def _linear_sum(name):
    @jax.custom_vjp
    def op(xs, ws):
        return _groups_sum(xs, ws, F32, name + "_fwd")

    def fwd(xs, ws):
        return op(xs, ws), (xs, ws)

    def bwd(res, dy):
        xs, ws = res
        return tuple(_groups_apply(dy, ws, [x.dtype for x in xs], name + "_dx")), tuple(_groups_outer(xs, dy, ws, name + "_dw"))

    op.defvjp(fwd, bwd)
    return op


def _relu2_epi(acc):
    return jnp.square(jnp.maximum(acc, 0.0)), acc


def _relu2_bwd_epi(acc, a):
    return (acc * (2.0 * jnp.maximum(a.astype(F32), 0.0)),)


def _mlp(name):
    @jax.custom_vjp
    def op(h, w1t, w2):
        s, _ = _mm(h, w1t, "nt", (BF16, BF16), name + "_up", epi=_relu2_epi)
        return _mm(s, w2, "nn", (F32,), name + "_down")[0]

    def fwd(h, w1t, w2):
        s, a = _mm(h, w1t, "nt", (BF16, BF16), name + "_up", epi=_relu2_epi)
        return _mm(s, w2, "nn", (F32,), name + "_down")[0], (h, w1t, w2, s, a)

    def bwd(res, dy):
        h, w1t, w2, s, a = res
        da = _mm(dy, w2, "nt", (BF16,), name + "_ds", epi=_relu2_bwd_epi, extras=(a,))[0]
        dw2 = _mm(s, dy, "tn", (w2.dtype,), name + "_dw2")[0]
        dw1t = _mm(da, h, "tn", (w1t.dtype,), name + "_dw1")[0]
        dh = _mm(da, w1t, "nn", (h.dtype,), name + "_dh")[0]
        return dh, dw1t, dw2

    op.defvjp(fwd, bwd)
    return op


def _two_pass_dot(x, m):
    hi = x.astype(BF16)
    lo = (x - hi.astype(F32)).astype(BF16)
    mb = m.astype(BF16)
    return jnp.dot(hi, mb, preferred_element_type=F32) + jnp.dot(lo, mb, preferred_element_type=F32)


@jax.custom_vjp
def _sym_dot(x, m):
    return _two_pass_dot(x, m)


def _sym_dot_fwd(x, m):
    return _two_pass_dot(x, m), m


def _sym_dot_bwd(m, g):
    return _two_pass_dot(g, m), jnp.zeros_like(m)


_sym_dot.defvjp(_sym_dot_fwd, _sym_dot_bwd)


def _neighbour(x):
    lane = lax.broadcasted_iota(jnp.int32, x.shape, 1)
    return jnp.where(lane % 2 == 0, pltpu.roll(x, x.shape[1] - 1, 1), pltpu.roll(x, 1, 1))


@jax.custom_vjp
def _swap_pairs(x):
    return _neighbour(x)


_swap_pairs.defvjp(lambda x: (_neighbour(x), None), lambda _, g: (_neighbour(g),))


def _lane_group(x, k):
    return x[:, k * B_GROUP_DIM:(k + 1) * B_GROUP_DIM]


@jax.custom_vjp
def _group_mix(ws, v):
    return jnp.concatenate([jnp.dot(w.astype(BF16), _lane_group(v, k).astype(BF16), preferred_element_type=F32)
                            for k, w in enumerate(ws)], axis=1)


def _group_mix_fwd(ws, v):
    return _group_mix(ws, v), (ws, v)


def _group_mix_bwd(res, d):
    ws, v = res
    parts = [(_lane_group(d, k).astype(BF16), _lane_group(v, k).astype(BF16)) for k in range(len(ws))]
    dws = tuple(lax.dot_general(dk, vk, (((1,), (1,)), ((), ())), preferred_element_type=F32) for dk, vk in parts)
    dv = jnp.concatenate([lax.dot_general(w.astype(BF16), dk, (((0,), (0,)), ((), ())), preferred_element_type=F32)
                          for w, (dk, _) in zip(ws, parts)], axis=1)
    return dws, dv


_group_mix.defvjp(_group_mix_fwd, _group_mix_bwd)


def _rowwise(name, f, out_specs, tl, ctx_blocks=0):
    def seg(l, s):
        return jnp.where(l >= ctx_blocks, s - 1, 0) if s > 1 else 0

    def specs(rows, tabs, pers, glbs, consts):
        row_specs = [pl.BlockSpec((1, tl, r.shape[2]), lambda b, l: (b, l, 0)) for r in rows]
        tab_specs = [pl.BlockSpec((tl, t.shape[1]), lambda b, l: (l, 0)) for t in tabs]
        per_specs = [pl.BlockSpec((1, 1, 1, p.shape[3]), functools.partial(lambda b, l, s: (b, seg(l, s), 0, 0), s=p.shape[1])) for p in pers]
        glb_specs = [pl.BlockSpec(g.shape, functools.partial(lambda b, l, nd: (0,) * nd, nd=g.ndim)) for g in glbs]
        const_specs = [pl.BlockSpec(c.shape, functools.partial(lambda b, l, nd: (0,) * nd, nd=c.ndim)) for c in consts]
        return row_specs, tab_specs, per_specs, glb_specs, const_specs

    def load(refs_rows, refs_tabs, refs_pers, refs_glbs, refs_consts):
        return (tuple(r[0].astype(F32) for r in refs_rows), tuple(t[...] for t in refs_tabs),
                tuple(p[0, 0].astype(F32) for p in refs_pers), tuple(g[...].astype(F32) for g in refs_glbs),
                tuple(c[...] for c in refs_consts))

    def call_fwd(rows, tabs, pers, glbs, consts):
        bsz, length = rows[0].shape[:2]
        nr, nt, npp, ng, nc = len(rows), len(tabs), len(pers), len(glbs), len(consts)
        rs, ts, ps, gs, cs = specs(rows, tabs, pers, glbs, consts)

        def body(*refs):
            ins, outs = refs[:nr + nt + npp + ng + nc], refs[nr + nt + npp + ng + nc:]
            r, t, p, g, c = load(ins[:nr], ins[nr:nr + nt], ins[nr + nt:nr + nt + npp], ins[nr + nt + npp:nr + nt + npp + ng], ins[nr + nt + npp + ng:])
            for o_ref, o in zip(outs, f(r, t, p, g, c)):
                o_ref[0] = o.astype(o_ref.dtype)

        return _pcall(
            body, name=name + "_fwd", grid=(bsz, length // tl),
            in_specs=rs + ts + ps + gs + cs,
            out_specs=[pl.BlockSpec((1, tl, w), lambda b, l: (b, l, 0)) for w, _ in out_specs],
            out_shape=[jax.ShapeDtypeStruct((bsz, length, w), d) for w, d in out_specs],
            compiler_params=_params(("parallel", "parallel")),
        )(*rows, *tabs, *pers, *glbs, *consts)

    def call_bwd(rows, tabs, pers, glbs, consts, cts):
        bsz, length = rows[0].shape[:2]
        nr, nt, npp, ng, nc, no = len(rows), len(tabs), len(pers), len(glbs), len(consts), len(cts)
        rs, ts, ps, gs, cs = specs(rows, tabs, pers, glbs, consts)
        n_in = nr + nt + npp + ng + nc

        def body(*refs):
            ins, ct_refs, outs = refs[:n_in], refs[n_in:n_in + no], refs[n_in + no:]
            r, t, p, g, c = load(ins[:nr], ins[nr:nr + nt], ins[nr + nt:nr + nt + npp], ins[nr + nt + npp:nr + nt + npp + ng], ins[nr + nt + npp + ng:])
            _, vjp = jax.vjp(lambda r_, p_, g_: tuple(f(r_, t, p_, g_, c)), r, p, g)
            dr, dp, dg = vjp(tuple(ct[0].astype(F32) for ct in ct_refs))
            dr_refs, dp_refs, dg_refs = outs[:nr], outs[nr:nr + npp], outs[nr + npp:]
            for ref, d in zip(dr_refs, dr):
                ref[0] = d.astype(ref.dtype)
            b, l = pl.program_id(0), pl.program_id(1)
            first_of_segment = (l == 0) | (l == ctx_blocks)
            for ref, d in zip(dp_refs, dp):
                @pl.when(first_of_segment)
                def _(ref=ref, d=d):
                    ref[0, 0] = d

                @pl.when(jnp.logical_not(first_of_segment))
                def _(ref=ref, d=d):
                    ref[0, 0] += d
            first = (b == 0) & (l == 0)
            for ref, d in zip(dg_refs, dg):
                @pl.when(first)
                def _(ref=ref, d=d):
                    ref[...] = d

                @pl.when(jnp.logical_not(first))
                def _(ref=ref, d=d):
                    ref[...] += d

        ct_specs = [pl.BlockSpec((1, tl, w), lambda b, l: (b, l, 0)) for w, _ in out_specs]
        outs = _pcall(
            body, name=name + "_bwd", grid=(bsz, length // tl),
            in_specs=rs + ts + ps + gs + cs + ct_specs,
            out_specs=rs + ps + gs,
            out_shape=[jax.ShapeDtypeStruct(r.shape, r.dtype) for r in rows]
            + [jax.ShapeDtypeStruct(p.shape, F32) for p in pers] + [jax.ShapeDtypeStruct(g.shape, F32) for g in glbs],
            compiler_params=_params(("arbitrary", "arbitrary")),
        )(*rows, *tabs, *pers, *glbs, *consts, *cts)
        return tuple(outs[:nr]), tuple(outs[nr:nr + npp]), tuple(outs[nr + npp:])

    @jax.custom_vjp
    def op(rows, tabs, pers, glbs, consts):
        return tuple(call_fwd(rows, tabs, pers, glbs, consts))

    def fwd(rows, tabs, pers, glbs, consts):
        return op(rows, tabs, pers, glbs, consts), (rows, tabs, pers, glbs, consts)

    def bwd(res, cts):
        rows, tabs, pers, glbs, consts = res
        dr, dp, dg = call_bwd(rows, tabs, pers, glbs, consts, tuple(cts))
        dp = tuple(d.astype(p.dtype) for d, p in zip(dp, pers))
        dg = tuple(d.astype(g.dtype) for d, g in zip(dg, glbs))
        return dr, tuple(jnp.zeros_like(t) for t in tabs), dp, dg, tuple(jnp.zeros_like(c) for c in consts)

    op.defvjp(fwd, bwd)
    return op


def _rms(x, g):
    return x * lax.rsqrt(jnp.mean(x * x, axis=-1, keepdims=True) + EPS) * g


def _f_silu(r, t, p, g, c):
    return (jax.nn.silu(r[0]),)


def _f_modulate(r, t, p, g, c):
    shift, scale = p
    return (_rms(r[0], g[0]) * (1.0 + scale) + shift,)


def _f_res_modulate(r, t, p, g, c):
    x, y = r
    gate, shift, scale = p
    xn = x + gate * y
    return xn, _rms(xn, g[0]) * (1.0 + scale) + shift


def _f_res(r, t, p, g, c):
    return (r[0] + p[0] * r[1],)


def _f_headnorm_rope(r, t, p, g, c):
    x = r[0]
    cos, sin = t
    xn = x * lax.rsqrt(_sym_dot(x * x, c[0]) + EPS) * g[0]
    return (xn * cos + _swap_pairs(xn) * sin,)


def _f_rope(r, t, p, g, c):
    x = r[0]
    cos, sin = t
    return (x * cos + _swap_pairs(x) * sin,)


def _f_rope_narrow(r, t, p, g, c):
    x = r[0]
    cos, sin = t
    return (x * cos + _sym_dot(x, c[0]) * sin,)


def _f_rms(r, t, p, g, c):
    return (_rms(r[0], g[0]),)


def _f_sgu_pre(r, t, p, g, c):
    u = jax.nn.gelu(r[0])
    v = jax.nn.gelu(r[1])
    vn = v * lax.rsqrt(_sym_dot(v * v, c[0]) + EPS) * g[0]
    return u, vn


def _f_sgu_mix(r, t, p, g, c):
    u, vn = r
    return (u * (g[B_GROUPS] + _group_mix(tuple(g[:B_GROUPS]), vn)),)


def _f_glu(r, t, p, g, c):
    return (r[0] * jax.nn.sigmoid(r[1]),)


def _f_ln_silu(r, t, p, g, c):
    x = r[0]
    mu = jnp.mean(x, axis=-1, keepdims=True)
    var = jnp.mean(jnp.square(x - mu), axis=-1, keepdims=True)
    return (jax.nn.silu((x - mu) * lax.rsqrt(var + EPS) * g[0] + g[1]),)


LOG2_E = 1.4426950408889634
LN_2 = 0.6931471805599453


def _attention(name, ctx_len, kv_per_step):
    def tiles(q):
        tq = _pick(q.shape[2], (256, 128))
        if ctx_len:
            tq = math.gcd(tq, ctx_len)
        return tq, ctx_len // tq

    def scores(q_tile, kk, dk):
        qs = (q_tile.astype(F32) * (LOG2_E * dk ** -0.5)).astype(BF16)
        return qs, lax.dot_general(qs, kk, (((1,), (1,)), ((), ())), preferred_element_type=F32)

    def by_segment(qi, cb, lk, run):
        if cb > 0:
            @pl.when(qi < cb)
            def _():
                run(ctx_len)

            @pl.when(qi >= cb)
            def _():
                run(lk)
        else:
            run(lk)

    def call_fwd(q, k, v):
        bsz, hq, lq, dk = q.shape
        _, hk, lk, dv = v.shape
        grp = hq // hk
        tq, cb = tiles(q)
        v_ones = jnp.concatenate([v, jnp.ones_like(v)], axis=-1)

        def body(q_ref, k_ref, v_ref, o_ref, lse_ref):
            def run(nk):
                for j in range(kv_per_step):
                    kk, vv = k_ref[0, j, :nk], v_ref[0, j, :nk]
                    for h in range(j * grp, (j + 1) * grp):
                        _, s = scores(q_ref[0, h], kk, dk)
                        m = jnp.max(s, axis=-1, keepdims=True)
                        acc = jnp.dot(jnp.exp2(s - m).astype(BF16), vv, preferred_element_type=F32)
                        l = acc[:, dv:dv + 1]
                        o_ref[0, h] = (acc[:, :dv] / l).astype(o_ref.dtype)
                        lse_ref[0, h] = m + jnp.log2(l)

            by_segment(pl.program_id(2), cb, lk, run)

        qh = grp * kv_per_step
        return _pcall(
            body, name=name + "_fwd", grid=(bsz, hk // kv_per_step, lq // tq),
            in_specs=[pl.BlockSpec((1, qh, tq, dk), lambda b, h, i: (b, h, i, 0)),
                      pl.BlockSpec((1, kv_per_step, lk, dk), lambda b, h, i: (b, h, 0, 0)),
                      pl.BlockSpec((1, kv_per_step, lk, 2 * dv), lambda b, h, i: (b, h, 0, 0))],
            out_specs=[pl.BlockSpec((1, qh, tq, dv), lambda b, h, i: (b, h, i, 0)), pl.BlockSpec((1, qh, tq, 1), lambda b, h, i: (b, h, i, 0))],
            out_shape=[jax.ShapeDtypeStruct((bsz, hq, lq, dv), BF16), jax.ShapeDtypeStruct((bsz, hq, lq, 1), F32)],
            compiler_params=_params(("parallel", "parallel", "parallel")),
        )(q, k, v_ones)

    def call_bwd(q, k, v, o, lse, do):
        bsz, hq, lq, dk = q.shape
        _, hk, lk, dv = v.shape
        grp = hq // hk
        tq, cb = tiles(q)
        nq = lq // tq

        def body(q_ref, k_ref, v_ref, o_ref, lse_ref, do_ref, dq_ref, dk_ref, dv_ref):
            qi = pl.program_id(2)

            @pl.when(qi == 0)
            def _():
                dk_ref[...] = jnp.zeros_like(dk_ref)
                dv_ref[...] = jnp.zeros_like(dv_ref)

            def run(nk):
                for j in range(kv_per_step):
                    kk, vv = k_ref[0, j, :nk], v_ref[0, j, :nk]
                    dk_sum = dv_sum = None
                    for h in range(j * grp, (j + 1) * grp):
                        dd = do_ref[0, h]
                        qs, s = scores(q_ref[0, h], kk, dk)
                        p = jnp.exp2(s - lse_ref[0, h])
                        delta = jnp.sum(dd.astype(F32) * o_ref[0, h].astype(F32), axis=-1, keepdims=True)
                        dp = lax.dot_general(dd, vv, (((1,), (1,)), ((), ())), preferred_element_type=F32)
                        t = (p * (dp - delta)).astype(BF16)
                        dq_ref[0, h] = (jnp.dot(t, kk, preferred_element_type=F32) * dk ** -0.5).astype(dq_ref.dtype)
                        dk_h = lax.dot_general(t, qs, (((0,), (0,)), ((), ())), preferred_element_type=F32)
                        dv_h = lax.dot_general(p.astype(BF16), dd, (((0,), (0,)), ((), ())), preferred_element_type=F32)
                        dk_sum = dk_h if dk_sum is None else dk_sum + dk_h
                        dv_sum = dv_h if dv_sum is None else dv_sum + dv_h
                    dk_ref[0, j, :nk] += dk_sum
                    dv_ref[0, j, :nk] += dv_sum

            by_segment(qi, cb, lk, run)

            @pl.when(qi == nq - 1)
            def _():
                dk_ref[...] = dk_ref[...] * LN_2

        qh = grp * kv_per_step
        q_spec = pl.BlockSpec((1, qh, tq, dk), lambda b, h, i: (b, h, i, 0))
        k_spec = pl.BlockSpec((1, kv_per_step, lk, dk), lambda b, h, i: (b, h, 0, 0))
        v_spec = pl.BlockSpec((1, kv_per_step, lk, dv), lambda b, h, i: (b, h, 0, 0))
        o_spec = pl.BlockSpec((1, qh, tq, dv), lambda b, h, i: (b, h, i, 0))
        lse_spec = pl.BlockSpec((1, qh, tq, 1), lambda b, h, i: (b, h, i, 0))
        return _pcall(
            body, name=name + "_bwd", grid=(bsz, hk // kv_per_step, nq),
            in_specs=[q_spec, k_spec, v_spec, o_spec, lse_spec, o_spec],
            out_specs=[q_spec, k_spec, v_spec],
            out_shape=[jax.ShapeDtypeStruct(q.shape, BF16), jax.ShapeDtypeStruct(k.shape, F32), jax.ShapeDtypeStruct(v.shape, F32)],
            compiler_params=_params(("parallel", "parallel", "arbitrary")),
        )(q, k, v, o, lse, do)

    @jax.custom_vjp
    def op(q, k, v):
        return call_fwd(q, k, v)[0]

    def fwd(q, k, v):
        o, lse = call_fwd(q, k, v)
        return o, (q, k, v, o, lse)

    def bwd(res, do):
        q, k, v, o, lse = res
        dq, dk, dv = call_bwd(q, k, v, o, lse, do)
        return dq, dk.astype(k.dtype), dv.astype(v.dtype)

    op.defvjp(fwd, bwd)
    return op


def _attention_rows(name, ctx_len, heads, kv_heads):
    grp = heads // kv_heads

    def by_segment(qi, cb, lk, run):
        if cb > 0:
            @pl.when(qi < cb)
            def _():
                run(ctx_len)

            @pl.when(qi >= cb)
            def _():
                run(lk)
        else:
            run(lk)

    def head(ref, h, hd):
        return ref[0, :, h * hd:(h + 1) * hd]

    def scores(q_tile, kk, hd):
        qs = (q_tile.astype(F32) * (LOG2_E * hd ** -0.5)).astype(BF16)
        return qs, lax.dot_general(qs, kk, _NT, preferred_element_type=F32)

    def shapes(q):
        bsz, length, width = q.shape
        tq = math.gcd(_pick(length, (256, 128)), ctx_len) if ctx_len else _pick(length, (256, 128))
        return bsz, length, width // heads, tq, ctx_len // tq

    def call_fwd(q, k, v):
        bsz, length, hd, tq, cb = shapes(q)

        def body(q_ref, k_ref, v_ref, o_ref, lse_ref):
            def run(nk):
                k_all, v_all = k_ref[0, :nk], v_ref[0, :nk]
                outs = []
                for j in range(kv_heads):
                    kk = k_all[:, j * hd:(j + 1) * hd]
                    v_ones = jnp.concatenate([v_all[:, j * hd:(j + 1) * hd], jnp.ones((nk, hd), BF16)], axis=1)
                    for h in range(j * grp, (j + 1) * grp):
                        _, s = scores(head(q_ref, h, hd), kk, hd)
                        m = jnp.max(s, axis=-1, keepdims=True)
                        acc = jnp.dot(jnp.exp2(s - m).astype(BF16), v_ones, preferred_element_type=F32)
                        l = acc[:, hd:hd + 1]
                        outs.append((acc[:, :hd] / l).astype(BF16))
                        lse_ref[0, h] = m + jnp.log2(l)
                o_ref[0] = jnp.concatenate(outs, axis=1)

            by_segment(pl.program_id(1), cb, length, run)

        q_spec = pl.BlockSpec((1, tq, heads * hd), lambda b, i: (b, i, 0))
        kv_spec = pl.BlockSpec((1, length, kv_heads * hd), lambda b, i: (b, 0, 0))
        return _pcall(
            body, name=name + "_fwd", grid=(bsz, length // tq), in_specs=[q_spec, kv_spec, kv_spec],
            out_specs=[q_spec, pl.BlockSpec((1, heads, tq, 1), lambda b, i: (b, 0, i, 0))],
            out_shape=[jax.ShapeDtypeStruct(q.shape, BF16), jax.ShapeDtypeStruct((bsz, heads, length, 1), F32)],
            compiler_params=_params(("parallel", "parallel")),
        )(q, k, v)

    def call_bwd(q, k, v, o, lse, do):
        bsz, length, hd, tq, cb = shapes(q)
        nq = length // tq

        def body(q_ref, k_ref, v_ref, o_ref, lse_ref, do_ref, dq_ref, dk_ref, dv_ref):
            qi = pl.program_id(1)

            @pl.when(qi == 0)
            def _():
                dk_ref[...] = jnp.zeros_like(dk_ref)
                dv_ref[...] = jnp.zeros_like(dv_ref)

            def run(nk):
                k_all, v_all = k_ref[0, :nk], v_ref[0, :nk]
                dqs, dks, dvs = [], [], []
                for j in range(kv_heads):
                    kk, vv = k_all[:, j * hd:(j + 1) * hd], v_all[:, j * hd:(j + 1) * hd]
                    dk_sum = dv_sum = None
                    for h in range(j * grp, (j + 1) * grp):
                        dd = head(do_ref, h, hd)
                        qs, s = scores(head(q_ref, h, hd), kk, hd)
                        p = jnp.exp2(s - lse_ref[0, h])
                        delta = jnp.sum(dd.astype(F32) * head(o_ref, h, hd).astype(F32), axis=-1, keepdims=True)
                        t = (p * (lax.dot_general(dd, vv, _NT, preferred_element_type=F32) - delta)).astype(BF16)
                        dqs.append((jnp.dot(t, kk, preferred_element_type=F32) * hd ** -0.5).astype(BF16))
                        dk_h = lax.dot_general(t, qs, _TN, preferred_element_type=F32)
                        dv_h = lax.dot_general(p.astype(BF16), dd, _TN, preferred_element_type=F32)
                        dk_sum = dk_h if dk_sum is None else dk_sum + dk_h
                        dv_sum = dv_h if dv_sum is None else dv_sum + dv_h
                    dks.append(dk_sum)
                    dvs.append(dv_sum)
                dq_ref[0] = jnp.concatenate(dqs, axis=1)
                dk_ref[0, :nk] += jnp.concatenate(dks, axis=1)
                dv_ref[0, :nk] += jnp.concatenate(dvs, axis=1)

            by_segment(qi, cb, length, run)

            @pl.when(qi == nq - 1)
            def _():
                dk_ref[...] = dk_ref[...] * LN_2

        q_spec = pl.BlockSpec((1, tq, heads * hd), lambda b, i: (b, i, 0))
        kv_spec = pl.BlockSpec((1, length, kv_heads * hd), lambda b, i: (b, 0, 0))
        lse_spec = pl.BlockSpec((1, heads, tq, 1), lambda b, i: (b, 0, i, 0))
        return _pcall(
            body, name=name + "_bwd", grid=(bsz, nq), in_specs=[q_spec, kv_spec, kv_spec, q_spec, lse_spec, q_spec],
            out_specs=[q_spec, kv_spec, kv_spec],
            out_shape=[jax.ShapeDtypeStruct(q.shape, BF16), jax.ShapeDtypeStruct(k.shape, F32), jax.ShapeDtypeStruct(v.shape, F32)],
            compiler_params=_params(("parallel", "arbitrary")),
        )(q, k, v, o, lse, do)

    @jax.custom_vjp
    def op(q, k, v):
        return call_fwd(q, k, v)[0]

    def fwd(q, k, v):
        o, lse = call_fwd(q, k, v)
        return o, (q, k, v, o, lse)

    def bwd(res, do):
        q, k, v, o, lse = res
        dq, dk, dv = call_bwd(q, k, v, o, lse, do)
        return dq, dk.astype(k.dtype), dv.astype(v.dtype)

    op.defvjp(fwd, bwd)
    return op


def _attention_latent_rows(name, heads, nope, rope, dv, heads_per_step):
    lanes = 128
    scale = (nope + rope) ** -0.5

    def shapes(q, kn):
        bsz, lq, _ = q.shape
        return bsz, lq, kn.shape[1], _pick(lq, (256, 128))

    def scores(q_tile, kk):
        qs = (q_tile.astype(F32) * (LOG2_E * scale)).astype(BF16)
        return qs, lax.dot_general(qs, kk, _NT, preferred_element_type=F32)

    def call_fwd(q, kn, kr, v):
        bsz, lq, lk, tq = shapes(q, kn)

        def body(q_ref, kn_ref, kr_ref, v_ref, o_ref, lse_ref):
            real = lax.broadcasted_iota(jnp.int32, (lk, lanes), 1) < dv
            outs = []
            for h in range(heads_per_step):
                kk = kn_ref[0, :, h * lanes:(h + 1) * lanes] + kr_ref[0]
                v_ones = jnp.where(real, v_ref[0, :, h * lanes:(h + 1) * lanes], jnp.ones((lk, lanes), BF16))
                _, s = scores(q_ref[0, :, h * lanes:(h + 1) * lanes], kk)
                m = jnp.max(s, axis=-1, keepdims=True)
                acc = jnp.dot(jnp.exp2(s - m).astype(BF16), v_ones, preferred_element_type=F32)
                l = acc[:, dv:dv + 1]
                outs.append((acc[:, :dv] / l).astype(BF16))
                lse_ref[0, h] = m + jnp.log2(l)
            o_ref[0] = jnp.concatenate(outs, axis=1)

        wide = heads_per_step * lanes
        return _pcall(
            body, name=name + "_fwd", grid=(bsz, heads // heads_per_step, lq // tq),
            in_specs=[pl.BlockSpec((1, tq, wide), lambda b, g, i: (b, i, g)), pl.BlockSpec((1, lk, wide), lambda b, g, i: (b, 0, g)),
                      pl.BlockSpec((1, lk, lanes), lambda b, g, i: (b, 0, 0)), pl.BlockSpec((1, lk, wide), lambda b, g, i: (b, 0, g))],
            out_specs=[pl.BlockSpec((1, tq, heads_per_step * dv), lambda b, g, i: (b, i, g)),
                       pl.BlockSpec((1, heads_per_step, tq, 1), lambda b, g, i: (b, g, i, 0))],
            out_shape=[jax.ShapeDtypeStruct((bsz, lq, heads * dv), BF16), jax.ShapeDtypeStruct((bsz, heads, lq, 1), F32)],
            compiler_params=_params(("parallel", "parallel", "parallel")),
        )(q, kn, kr, v)

    def call_bwd(q, kn, kr, v, o, lse, do):
        bsz, lq, lk, tq = shapes(q, kn)

        def body(q_ref, kn_ref, kr_ref, v_ref, o_ref, lse_ref, do_ref, dq_ref, dkn_ref, dkr_ref, dv_ref):
            g, qi = pl.program_id(1), pl.program_id(2)

            @pl.when(qi == 0)
            def _():
                dkn_ref[...] = jnp.zeros_like(dkn_ref)
                dv_ref[...] = jnp.zeros_like(dv_ref)

            @pl.when((qi == 0) & (g == 0))
            def _():
                dkr_ref[...] = jnp.zeros_like(dkr_ref)

            dqs, dks, dvs = [], [], []
            for h in range(heads_per_step):
                kk = kn_ref[0, :, h * lanes:(h + 1) * lanes] + kr_ref[0]
                vv = v_ref[0, :, h * lanes:h * lanes + dv]
                dd = do_ref[0, :, h * dv:(h + 1) * dv]
                qs, s = scores(q_ref[0, :, h * lanes:(h + 1) * lanes], kk)
                p = jnp.exp2(s - lse_ref[0, h])
                delta = jnp.sum(dd.astype(F32) * o_ref[0, :, h * dv:(h + 1) * dv].astype(F32), axis=-1, keepdims=True)
                t = (p * (lax.dot_general(dd, vv, _NT, preferred_element_type=F32) - delta)).astype(BF16)
                dqs.append((jnp.dot(t, kk, preferred_element_type=F32) * scale).astype(BF16))
                dks.append(lax.dot_general(t, qs, _TN, preferred_element_type=F32) * LN_2)
                dv_h = lax.dot_general(p.astype(BF16), dd, _TN, preferred_element_type=F32)
                dvs.append(jnp.concatenate([dv_h, jnp.zeros((lk, lanes - dv), F32)], axis=1))
            dq_ref[0] = jnp.concatenate(dqs, axis=1)
            dkn_ref[0] += jnp.concatenate(dks, axis=1)
            dv_ref[0] += jnp.concatenate(dvs, axis=1)
            shared = dks[0]
            for d_h in dks[1:]:
                shared = shared + d_h
            dkr_ref[0] += shared

        wide = heads_per_step * lanes
        q_spec = pl.BlockSpec((1, tq, wide), lambda b, g, i: (b, i, g))
        k_spec = pl.BlockSpec((1, lk, wide), lambda b, g, i: (b, 0, g))
        kr_spec = pl.BlockSpec((1, lk, lanes), lambda b, g, i: (b, 0, 0))
        o_spec = pl.BlockSpec((1, tq, heads_per_step * dv), lambda b, g, i: (b, i, g))
        lse_spec = pl.BlockSpec((1, heads_per_step, tq, 1), lambda b, g, i: (b, g, i, 0))
        return _pcall(
            body, name=name + "_bwd", grid=(bsz, heads // heads_per_step, lq // tq),
            in_specs=[q_spec, k_spec, kr_spec, k_spec, o_spec, lse_spec, o_spec],
            out_specs=[q_spec, k_spec, kr_spec, k_spec],
            out_shape=[jax.ShapeDtypeStruct(q.shape, BF16), jax.ShapeDtypeStruct(kn.shape, F32), jax.ShapeDtypeStruct(kr.shape, F32),
                       jax.ShapeDtypeStruct(v.shape, F32)],
            compiler_params=_params(("parallel", "arbitrary", "arbitrary")),
        )(q, kn, kr, v, o, lse, do)

    @jax.custom_vjp
    def op(q, kn, kr, v):
        return call_fwd(q, kn, kr, v)[0]

    def fwd(q, kn, kr, v):
        o, lse = call_fwd(q, kn, kr, v)
        return o, (q, kn, kr, v, o, lse)

    def bwd(res, do):
        q, kn, kr, v, o, lse = res
        dq, dkn, dkr, dv_ = call_bwd(q, kn, kr, v, o, lse, do)
        return dq, dkn.astype(kn.dtype), dkr.astype(kr.dtype), dv_.astype(v.dtype)

    op.defvjp(fwd, bwd)
    return op


def _conv_call(ypad, taps, name):
    bsz, lp, ch = ypad.shape
    length = lp - 2 * ROW_ALIGN
    tl = _pick(length, (256, 128))

    def body(y_ref, w_ref, o_ref):
        base = pl.multiple_of(pl.program_id(1) * tl, tl)
        win = y_ref[0, pl.ds(base, tl + 2 * ROW_ALIGN), :]
        acc = jnp.broadcast_to(w_ref[pl.ds(D_CONV, 1), :], (tl, ch))
        for k in range(D_CONV):
            acc = acc + win[k:k + tl, :] * w_ref[pl.ds(k, 1), :]
        o_ref[0] = acc

    return _pcall(
        body, name=name, grid=(bsz, length // tl),
        in_specs=[pl.BlockSpec((1, lp, ch), lambda b, l: (b, 0, 0)), pl.BlockSpec((D_CONV + 1, ch), lambda b, l: (0, 0))],
        out_specs=pl.BlockSpec((1, tl, ch), lambda b, l: (b, l, 0)),
        out_shape=jax.ShapeDtypeStruct((bsz, length, ch), F32),
        compiler_params=_params(("parallel", "parallel")),
    )(ypad, taps)


def _conv_dw_call(ypad, dout, name):
    bsz, lp, ch = ypad.shape
    length = lp - 2 * ROW_ALIGN
    tl = _pick(length, (256, 128))

    def body(y_ref, d_ref, o_ref):
        b, l = pl.program_id(0), pl.program_id(1)

        @pl.when((b == 0) & (l == 0))
        def _():
            o_ref[...] = jnp.zeros_like(o_ref)

        base = pl.multiple_of(l * tl, tl)
        win = y_ref[0, pl.ds(base, tl + 2 * ROW_ALIGN), :]
        dd = d_ref[0]
        for k in range(D_CONV):
            o_ref[pl.ds(k, 1), :] += jnp.sum(win[k:k + tl, :] * dd, axis=0, keepdims=True)
        o_ref[pl.ds(D_CONV, 1), :] += jnp.sum(dd, axis=0, keepdims=True)

    return _pcall(
        body, name=name, grid=(bsz, length // tl),
        in_specs=[pl.BlockSpec((1, lp, ch), lambda b, l: (b, 0, 0)), pl.BlockSpec((1, tl, ch), lambda b, l: (b, l, 0))],
        out_specs=pl.BlockSpec((D_CONV + 1, ch), lambda b, l: (0, 0)),
        out_shape=jax.ShapeDtypeStruct((D_CONV + 1, ch), F32),
        compiler_params=_params(("arbitrary", "arbitrary")),
    )(ypad, dout)


def _pad_rows(y):
    return jnp.pad(y, ((0, 0), (CONV_PAD, 2 * ROW_ALIGN - CONV_PAD), (0, 0)))


@jax.custom_vjp
def _dwconv(y, taps):
    return _conv_call(_pad_rows(y), taps, "conv_fwd")


def _dwconv_fwd(y, taps):
    return _dwconv(y, taps), (y, taps)


def _dwconv_bwd(res, dout):
    y, taps = res
    flipped = jnp.concatenate([taps[:D_CONV][::-1], jnp.zeros_like(taps[D_CONV:])], axis=0)
    dy = _conv_call(_pad_rows(dout), flipped, "conv_dy")
    dtaps = _conv_dw_call(_pad_rows(y), dout, "conv_dw")
    return dy, dtaps


_dwconv.defvjp(_dwconv_fwd, _dwconv_bwd)


def _loss_head(x, target, g):
    bsz, length, d = x.shape
    tl = _pick(length, (256, 128))

    def f(xb, tb, gb):
        err = _rms(xb, gb) - tb
        return 0.5 * jnp.sum(jnp.sum(err * err, axis=-1, keepdims=True), axis=0, keepdims=True) / d

    def body(x_ref, t_ref, g_ref, loss_ref, dx_ref, dg_ref):
        val, vjp = jax.vjp(lambda xb, gb: f(xb, t_ref[0], gb), x_ref[0], g_ref[...])
        dx, dg = vjp(jnp.ones((1, 1), F32))
        dx_ref[0] = dx
        first = (pl.program_id(0) == 0) & (pl.program_id(1) == 0)

        @pl.when(first)
        def _():
            loss_ref[...] = val
            dg_ref[...] = dg

        @pl.when(jnp.logical_not(first))
        def _():
            loss_ref[...] += val
            dg_ref[...] += dg

    row = pl.BlockSpec((1, tl, d), lambda b, l: (b, l, 0))
    return _pcall(
        body, name="loss_head", grid=(bsz, length // tl),
        in_specs=[row, row, pl.BlockSpec((1, d), lambda b, l: (0, 0))],
        out_specs=[pl.BlockSpec((1, 1), lambda b, l: (0, 0)), row, pl.BlockSpec((1, d), lambda b, l: (0, 0))],
        out_shape=[jax.ShapeDtypeStruct((1, 1), F32), jax.ShapeDtypeStruct(x.shape, F32), jax.ShapeDtypeStruct((1, d), F32)],
        compiler_params=_params(("arbitrary", "arbitrary")),
    )(x, target, g)


def _adamw(w, g, m, v, name):
    shape = w.shape
    cols = shape[-1]
    rows = w.size // cols
    tr = _pick(rows, (512, 256, 128))
    w2, g2, m2, v2 = (t.reshape(rows, cols) for t in (w, g, m, v))

    def body(w_ref, g_ref, m_ref, v_ref, d_ref, nm_ref, nv_ref):
        gg = g_ref[...]
        nm = ADAM_B1 * m_ref[...] + (1.0 - ADAM_B1) * gg
        nv = ADAM_B2 * v_ref[...] + (1.0 - ADAM_B2) * jnp.square(gg)
        m_hat = nm / (1.0 - ADAM_B1 ** ADAM_STEP)
        v_hat = nv / (1.0 - ADAM_B2 ** ADAM_STEP)
        d_ref[...] = -ADAM_LR * (m_hat / (jnp.sqrt(v_hat) + ADAM_EPS) + ADAM_WD * w_ref[...])
        nm_ref[...] = nm
        nv_ref[...] = nv

    spec = pl.BlockSpec((tr, cols), lambda i: (i, 0))
    outs = _pcall(
        body, name=name, grid=(rows // tr,), in_specs=[spec] * 4, out_specs=[spec] * 3,
        out_shape=[jax.ShapeDtypeStruct((rows, cols), F32)] * 3,
        compiler_params=_params(("parallel",)),
    )(w2, g2, m2, v2)
    return tuple(o.reshape(shape) for o in outs)


def _mesh_pos():
    return lax.axis_index("x"), lax.axis_index("y"), lax.axis_index("c")


_RELATIONS = [(dx, dy, dc) for dx in (0, 1) for dy in (0, 1) for dc in (0, 1)][1:]


def _peer(pos, rel):
    return tuple(jnp.where(r == 1, 1 - p, p) if r else p for p, r in zip(pos, rel))


def _block_index(pos):
    return 4 * pos[0] + 2 * pos[1] + pos[2]


_HBM = pl.BlockSpec(memory_space=pltpu.HBM)


def _all_gather(xs, name):
    n = len(xs)

    def body(*refs):
        x_refs, out_refs, (send_sems, recv_sems, local_sems) = refs[:n], refs[n:2 * n], refs[2 * n:]
        x_, y_, c_ = _mesh_pos()
        me, sibling = (x_, y_, c_), (x_, y_, 1 - c_)
        chips = [(1 - x_, y_), (x_, 1 - y_), (1 - x_, 1 - y_)]

        def copy(t, k, block, to, own=False):
            slot = out_refs[t].at[_block_index(block)]
            return pltpu.make_async_remote_copy(
                src_ref=x_refs[t] if own else slot, dst_ref=slot, send_sem=send_sems.at[7 * t + k], recv_sem=recv_sems.at[7 * t + k],
                device_id=to, device_id_type=pl.DeviceIdType.MESH)

        mine = [pltpu.make_async_copy(x_refs[t], out_refs[t].at[_block_index(me)], local_sems.at[t]) for t in range(n)]
        first = [[copy(t, 0, me, sibling, own=True)] + [copy(t, 1 + j, me, (*chip, c_), own=True) for j, chip in enumerate(chips)]
                 for t in range(n)]
        passed = [[copy(t, 4 + j, (*chip, c_), sibling) for j, chip in enumerate(chips)] for t in range(n)]
        for t in range(n):
            mine[t].start()
            for cp in first[t]:
                cp.start()
        for t in range(n):
            for j, chip in enumerate(chips):
                copy(t, 1 + j, (*chip, c_), me).wait_recv()
                passed[t][j].start()
        for t in range(n):
            copy(t, 0, sibling, me).wait_recv()
            for j, chip in enumerate(chips):
                copy(t, 4 + j, (*chip, 1 - c_), me).wait_recv()
            for cp in first[t] + passed[t]:
                cp.wait_send()
            mine[t].wait()

    return _pcall(
        body, name=name, in_specs=[_HBM] * n, out_specs=[_HBM] * n,
        out_shape=[jax.ShapeDtypeStruct((N_DEV,) + x.shape, x.dtype) for x in xs],
        scratch_shapes=[pltpu.SemaphoreType.DMA((7 * n,)), pltpu.SemaphoreType.DMA((7 * n,)), pltpu.SemaphoreType.DMA((n,))],
    )(*xs)


def _exchange_blocks(ps, name):
    n = len(ps)

    def body(*refs):
        p_refs, out_refs, (send_sems, recv_sems, local_sems) = refs[:n], refs[n:2 * n], refs[2 * n:]
        me = _mesh_pos()

        def copy(t, k, rel, arriving=False):
            peer = _peer(me, rel)
            return pltpu.make_async_remote_copy(
                src_ref=p_refs[t].at[_block_index(peer)], dst_ref=out_refs[t].at[_block_index(peer if arriving else me)],
                send_sem=send_sems.at[7 * t + k], recv_sem=recv_sems.at[7 * t + k], device_id=peer, device_id_type=pl.DeviceIdType.MESH)

        mine = [pltpu.make_async_copy(p_refs[t].at[_block_index(me)], out_refs[t].at[_block_index(me)], local_sems.at[t]) for t in range(n)]
        sends = [copy(t, k, rel) for t in range(n) for k, rel in enumerate(_RELATIONS)]
        for cp in mine + sends:
            cp.start()
        for t in range(n):
            for k, rel in enumerate(_RELATIONS):
                copy(t, k, rel, arriving=True).wait_recv()
        for cp in sends:
            cp.wait_send()
        for cp in mine:
            cp.wait()

    return _pcall(
        body, name=name, in_specs=[_HBM] * n, out_specs=[_HBM] * n,
        out_shape=[jax.ShapeDtypeStruct(p.shape, p.dtype) for p in ps],
        scratch_shapes=[pltpu.SemaphoreType.DMA((7 * n,)), pltpu.SemaphoreType.DMA((7 * n,)), pltpu.SemaphoreType.DMA((n,))],
    )(*ps)


_SEM = pl.BlockSpec(memory_space=pltpu.SEMAPHORE)
_EFFECT = pltpu.SideEffectType.DATAFLOW_SIDE_EFFECTING


def _push_start(srcs, after, name):
    n = len(srcs)
    lands = [lax.empty((N_DEV,) + s.shape[-2:], s.dtype) for s in srcs]

    def body(*refs):
        src_refs, land_refs = refs[:n], refs[n:2 * n]
        send_sems, recv_sems, token = refs[2 * n + 1:3 * n + 1], refs[3 * n + 1:4 * n + 1], refs[-1]
        me = _mesh_pos()
        for t in range(n):
            for rel in _RELATIONS:
                peer = _peer(me, rel)
                pltpu.make_async_remote_copy(
                    src_ref=src_refs[t].at[_block_index(peer)] if srcs[t].ndim == 3 else src_refs[t], dst_ref=land_refs[t].at[_block_index(me)],
                    send_sem=send_sems[t], recv_sem=recv_sems[t], device_id=peer, device_id_type=pl.DeviceIdType.MESH).start()
        token[...] = jnp.zeros_like(token)

    outs = _pcall(
        body, name=name,
        out_shape=[pltpu.SemaphoreType.DMA(())] * (2 * n) + [pltpu.HBM(s.shape, s.dtype) for s in srcs]
        + [pltpu.HBM(l.shape, l.dtype) for l in lands] + [jax.ShapeDtypeStruct((8, 128), F32)],
        in_specs=[_HBM] * (2 * n) + [pl.BlockSpec(memory_space=pl.ANY)],
        out_specs=[_SEM] * (2 * n) + [_HBM] * (2 * n) + [pl.BlockSpec(memory_space=pltpu.VMEM)],
        input_output_aliases={i: 2 * n + i for i in range(2 * n)}, compiler_params=pltpu.CompilerParams(has_side_effects=_EFFECT),
    )(*[pltpu.with_memory_space_constraint(t, pltpu.HBM) for t in list(srcs) + lands], after)
    return (outs[:n], outs[n:2 * n], outs[2 * n:3 * n], outs[3 * n:4 * n]), outs[-1]


def _push_wait(handle, after, owns, me, name):
    send_sems, recv_sems, src_thrus, land_thrus = handle
    n = len(land_thrus)

    def body(*refs):
        land_refs, sends, recvs = refs[n:2 * n], refs[2 * n:3 * n], refs[3 * n:4 * n]
        for t in range(n):
            seven = land_refs[t].at[pl.ds(0, N_DEV - 1)]
            all_seven = pltpu.make_async_remote_copy(src_ref=seven, dst_ref=seven, send_sem=sends[t], recv_sem=recvs[t],
                                                     device_id=_mesh_pos(), device_id_type=pl.DeviceIdType.MESH)
            all_seven.wait_send()
            all_seven.wait_recv()

    outs = _pcall(
        body, name=name,
        out_shape=[pltpu.HBM(t.shape, t.dtype) for t in list(src_thrus) + list(land_thrus)],
        in_specs=[_HBM] * (2 * n) + [_SEM] * (2 * n) + [pl.BlockSpec(memory_space=pl.ANY)], out_specs=[_HBM] * (2 * n),
        input_output_aliases={i: i for i in range(2 * n)}, compiler_params=pltpu.CompilerParams(has_side_effects=_EFFECT),
    )(*src_thrus, *land_thrus, *send_sems, *recv_sems, after)
    return [lax.dynamic_update_slice(land, own[None], (me, 0, 0)) for land, own in zip(outs[n:], owns)]


def _sum_blocks(p, name):
    n, rows, cols = p.shape
    tr = _pick(rows, (256, 128, 64, PACK_ROWS, 16, 8))

    def body(p_ref, o_ref):
        acc = p_ref[0].astype(F32)
        for s in range(1, n):
            acc = acc + p_ref[s].astype(F32)
        o_ref[...] = acc

    return _pcall(
        body, name=name, grid=(rows // tr,),
        in_specs=[pl.BlockSpec((n, tr, cols), lambda i: (0, i, 0))], out_specs=pl.BlockSpec((tr, cols), lambda i: (i, 0)),
        out_shape=jax.ShapeDtypeStruct((rows, cols), F32), compiler_params=_params(("parallel",)),
    )(p)


def _sum_rows(t, name):
    def body(t_ref, o_ref):
        o_ref[...] = jnp.sum(t_ref[...], axis=0, keepdims=True)

    return _pcall(body, name=name, out_shape=jax.ShapeDtypeStruct((1, t.shape[1]), F32))(t)


class _Packing:
    def __init__(self, sizes, align, total_align=None):
        self.offsets, self.sizes, self.align = {}, dict(sizes), align
        row = 0
        for name, size in sizes:
            self.offsets[name] = row
            row += -(-size // (align * PACK_COLS)) * align
        total_align = total_align or align
        self.rows = -(-row // total_align) * total_align
        self.tail = self.rows - row

    def pack(self, pieces):
        return self.pack_blocks({n: pieces[n].reshape(1, -1) for n in self.sizes})[0]

    def pack_blocks(self, pieces):
        out = []
        for n, size in self.sizes.items():
            padded = -(-size // (self.align * PACK_COLS)) * self.align * PACK_COLS
            out.append(jnp.pad(pieces[n], ((0, 0), (0, padded - size))).reshape(pieces[n].shape[0], -1, PACK_COLS))
        if self.tail:
            out.append(jnp.zeros((out[0].shape[0], self.tail, PACK_COLS), out[0].dtype))
        return jnp.concatenate(out, axis=1)

    def piece(self, packed, name, lead=()):
        start, size = self.offsets[name], self.sizes[name]
        nrow = -(-size // (self.align * PACK_COLS)) * self.align
        sl = packed[..., start:start + nrow, :]
        return sl.reshape(lead + (nrow * PACK_COLS,))[..., :size]


_PIECES = (("ev_w_in", "ev_w_in", 0, 1, "a"), ("w_out0", "w_out", 0, 0, "a"),
           ("mlp_w1_0", "mlp_w1", 0, 1, "b"), ("mlp_w2_0", "mlp_w2", 0, 0, "b"),
           ("od_w_in", "od_w_in", 0, 1, "c"), ("od_w_uq", "od_w_uq", 0, 1, "c"), ("od_w_ukv", "od_w_ukv", 0, 1, "c"),
           ("w_out1", "w_out", 1, 0, "c"), ("mlp_w1_1", "mlp_w1", 1, 1, "c"), ("mlp_w2_1", "mlp_w2", 1, 0, "c"))
_SMALL_SHARDED = (("od_q_norm_g", 1), ("od_conv_w", 2), ("od_conv_b", 1), ("od_ln_g", 1), ("od_ln_b", 1))
_REPLICATED = ("c_ctx", "norm1_g", "norm2_g", "ev_q_norm_g", "ev_k_norm_g", "ev_sgu_norm_g", "ev_sgu_w", "ev_sgu_b",
               "od_kv_norm_g", "final_g")


def _unshard(blocks, axis):
    moved = jnp.moveaxis(blocks, 0, axis)
    shape = moved.shape
    return moved.reshape(shape[:axis] + (shape[axis] * shape[axis + 1],) + shape[axis + 2:])


def _shard_blocks(full, axis):
    shape = full.shape
    split = full.reshape(shape[:axis] + (N_DEV, shape[axis] // N_DEV) + shape[axis + 1:])
    return jnp.moveaxis(split, axis, 0)


def _group_mean_matrix(width, group):
    idx = jnp.arange(width) // group
    return (idx[:, None] == idx[None, :]).astype(F32) / group


def _swap_matrix(width):
    idx = jnp.arange(width)
    return ((idx[:, None] ^ 1) == idx[None, :]).astype(F32)


def _angles(length, d_rot):
    rows = length // GRID_W
    row = jnp.broadcast_to(jnp.arange(rows)[:, None], (rows, GRID_W)).reshape(-1).astype(F32)
    col = jnp.broadcast_to(jnp.arange(GRID_W)[None, :], (rows, GRID_W)).reshape(-1).astype(F32)
    d_axis = d_rot // 2
    inv = ROPE_THETA ** (-jnp.arange(0, d_axis, 2, dtype=F32) / d_axis)
    return jnp.concatenate([row[:, None] * inv, col[:, None] * inv], axis=-1)


def _rope_tables(length, d_rot, head_dim, heads, ctx_len, tail=0):
    ang = _angles(length, d_rot)
    cos = jnp.repeat(jnp.cos(ang), 2, axis=1)
    sin = jnp.repeat(jnp.sin(ang), 2, axis=1) * jnp.tile(jnp.array([-1.0, 1.0], F32), d_rot // 2)
    keep = head_dim - d_rot - tail
    cos = jnp.concatenate([jnp.ones((length, keep), F32), cos, jnp.ones((length, tail), F32)], axis=1)
    sin = jnp.concatenate([jnp.zeros((length, keep), F32), sin, jnp.zeros((length, tail), F32)], axis=1)
    cos, sin = jnp.tile(cos, (1, heads)), jnp.tile(sin, (1, heads))
    cos = jnp.concatenate([jnp.ones((ctx_len, cos.shape[1]), F32), cos], axis=0)
    sin = jnp.concatenate([jnp.zeros((ctx_len, sin.shape[1]), F32), sin], axis=0)
    return cos, sin


def _to_heads(t, heads):
    b, l, w = t.shape
    return t.reshape(b, l, heads, w // heads).transpose(0, 2, 1, 3)


def _from_heads(t):
    b, h, l, d = t.shape
    return t.transpose(0, 2, 1, 3).reshape(b, l, h * d)


def _segment_params(mod, bsz):
    parts = jnp.split(mod, N_MOD, axis=-1)
    out = []
    for part in parts:
        lat = part[:bsz]
        ctx = jnp.broadcast_to(part[bsz:bsz + 1], lat.shape)
        out.append(jnp.stack([ctx, lat], axis=1)[:, :, None, :])
    return out


def _flat(t):
    return t.reshape(-1, t.shape[-1])


def _sequence_rowwise(ctx_len):
    tl = math.gcd(256, ctx_len)

    def make(name, f, out_specs, rows_per_block=tl, ctx_blocks=ctx_len // tl):
        return _rowwise(name, f, out_specs, rows_per_block, ctx_blocks)

    return make


def _mixer0(xall, modrows0, w, bsz, length, ctx_len):
    d = xall.shape[-1]
    total = ctx_len + length
    rowwise, flat = _sequence_rowwise(ctx_len), _flat
    sh1, sc1, g1, sh2, sc2, _ = _segment_params(modrows0, bsz)
    (h,) = rowwise("mod0", _f_modulate, [(d, BF16)])((xall,), (), (sh1, sc1), (w["norm1_g0"],), ())
    ev_q, ev_kv = A_Q_HEADS * A_HEAD_DIM, A_KV_HEADS * A_HEAD_DIM
    half = B_GROUPS * B_GROUP_DIM
    groups = tuple(jnp.split(w["ev_w_in"], [ev_q, ev_q + ev_kv, ev_q + 2 * ev_kv, ev_q + 2 * ev_kv + half], axis=0))
    qp, kp, vp, zu, zv = [t.reshape(bsz, total, -1) for t in _linear_multi("ev_in", (F32, F32, BF16, F32, F32))(flat(h), groups)]
    cos_q, sin_q = _rope_tables(length, A_HEAD_DIM, A_HEAD_DIM, A_Q_HEADS, ctx_len)
    cos_k, sin_k = cos_q[:, :ev_kv], sin_q[:, :ev_kv]
    (q,) = rowwise("ev_q", _f_headnorm_rope, [(ev_q, BF16)])(
        (qp,), (cos_q, sin_q), (), (jnp.tile(w["ev_q_norm_g"][0], A_Q_HEADS)[None],), (_group_mean_matrix(ev_q, A_HEAD_DIM),))
    (k,) = rowwise("ev_k", _f_headnorm_rope, [(ev_kv, BF16)])(
        (kp,), (cos_k, sin_k), (), (jnp.tile(w["ev_k_norm_g"][0], A_KV_HEADS)[None],), (_group_mean_matrix(ev_kv, A_HEAD_DIM),))
    o_att = _attention_rows("gqa", ctx_len, A_Q_HEADS, A_KV_HEADS)(q, k, vp)
    u, vn = rowwise("sgu_pre", _f_sgu_pre, [(half, F32), (half, BF16)])(
        (zu, zv), (), (), (w["ev_sgu_norm_g"][0].reshape(1, half),), (_group_mean_matrix(half, B_GROUP_DIM),))
    bias = jnp.repeat(w["ev_sgu_b"][0].T, B_GROUP_DIM, axis=1)
    (o_sgu,) = rowwise("sgu_mix", _f_sgu_mix, [(half, BF16)], rows_per_block=B_CHUNK, ctx_blocks=0)(
        (u, vn), (), (), tuple(w["ev_sgu_w"][0][g] for g in range(B_GROUPS)) + (bias,), ())
    y = _linear_sum("out0")((flat(o_att), flat(o_sgu)), tuple(jnp.split(w["w_out0"], 2, axis=0))).reshape(bsz, total, d)
    x1, h = rowwise("res_mod0a", _f_res_modulate, [(d, F32), (d, BF16)])((xall, y), (), (g1, sh2, sc2), (w["norm2_g0"],), ())
    return x1, h


def _mlp0(x1, h, modrows0, modrows1, w, bsz, length, ctx_len):
    d = x1.shape[-1]
    total = ctx_len + length
    g2 = _segment_params(modrows0, bsz)[5]
    sh1, sc1 = _segment_params(modrows1, bsz)[:2]
    y = _mlp("mlp0")(_flat(h), w["mlp_w1_0"], w["mlp_w2_0"]).reshape(bsz, total, d)
    return _sequence_rowwise(ctx_len)("res_mod0b", _f_res_modulate, [(d, F32), (d, BF16)])((x1, y), (), (g2, sh1, sc1), (w["norm1_g1"],), ())


def _layer1(x2, h, modrows1, w, bsz, length, ctx_len):
    d = x2.shape[-1]
    total = ctx_len + length
    half = B_GROUPS * B_GROUP_DIM
    rowwise, flat = _sequence_rowwise(ctx_len), _flat
    _, _, g1n, sh2n, sc2n, g2n = _segment_params(modrows1, bsz)
    g_cq, g_ckv, g_kr, g_za, g_zg = jnp.split(w["od_w_in"], [C_Q_RANK, C_Q_RANK + C_KV_RANK, C_Q_RANK + C_KV_RANK + C_ROPE,
                                                             C_Q_RANK + C_KV_RANK + C_ROPE + half], axis=0)
    lanes, c_qk = 128, C_NOPE + C_ROPE
    g_kr = jnp.pad(g_kr, ((C_NOPE, lanes - c_qk), (0, 0)))
    cq, ckv, kr, za, zg = [t.reshape(bsz, total, -1) for t in _linear_multi("od_in", (F32,) * 5)(flat(h), (g_cq, g_ckv, g_kr, g_za, g_zg))]
    lat = slice(ctx_len, total)
    lat_tl = math.gcd(256, length)
    (cqn,) = _rowwise("od_qn", _f_rms, [(C_Q_RANK, BF16)], lat_tl)((cq[:, lat],), (), (), (w["od_q_norm_g"],), ())
    w_uq = jnp.pad(w["od_w_uq"].reshape(C_HEADS, c_qk, C_Q_RANK), ((0, 0), (0, lanes - c_qk), (0, 0))).reshape(C_HEADS * lanes, C_Q_RANK)
    qf = _linear("od_uq", transposed=True)(flat(cqn), w_uq).reshape(bsz, length, C_HEADS * lanes)
    cos_q, sin_q = _rope_tables(length, C_ROPE, lanes, C_HEADS, 0, tail=lanes - c_qk)
    (q,) = _rowwise("od_qrope", _f_rope, [(C_HEADS * lanes, BF16)], lat_tl)((qf,), (cos_q, sin_q), (), (), ())
    (ckvn,) = rowwise("od_kvn", _f_rms, [(C_KV_RANK, BF16)])((ckv,), (), (), (w["od_kv_norm_g"],), ())
    per_head = w["od_w_ukv"].reshape(C_HEADS, C_NOPE + C_V, C_KV_RANK)
    w_kn = jnp.pad(per_head[:, :C_NOPE], ((0, 0), (0, lanes - C_NOPE), (0, 0))).reshape(C_HEADS * lanes, C_KV_RANK)
    w_v = jnp.pad(per_head[:, C_NOPE:], ((0, 0), (0, lanes - C_V), (0, 0))).reshape(C_HEADS * lanes, C_KV_RANK)
    kn, vv = [t.reshape(bsz, total, -1) for t in _linear_multi("od_ukv", (BF16, BF16))(flat(ckvn), (w_kn, w_v))]
    cos_r, sin_r = _rope_tables(length, C_ROPE, lanes, 1, ctx_len, tail=lanes - c_qk)
    (krr,) = rowwise("od_krope", _f_rope, [(lanes, BF16)])((kr,), (cos_r, sin_r), (), (), ())
    o_att = _attention_latent_rows("mla", C_HEADS, C_NOPE, C_ROPE, C_V, 2)(q, kn, krr, vv)
    (glu,) = _rowwise("glu", _f_glu, [(half, F32)], lat_tl)((za[:, lat], zg[:, lat]), (), (), (), ())
    taps = jnp.concatenate([w["od_conv_w"][0], w["od_conv_b"]], axis=0)
    conv = _dwconv(glu, taps)
    (o_conv,) = _rowwise("ln_silu", _f_ln_silu, [(half, BF16)], lat_tl)((conv,), (), (), (w["od_ln_g"], w["od_ln_b"]), ())
    y = _linear_sum("out1")((flat(o_att), flat(o_conv)), tuple(jnp.split(w["w_out1"], 2, axis=0))).reshape(bsz, length, d)
    lat_param = lambda p: p[:, 1:]
    x3, h = _rowwise("res_mod1a", _f_res_modulate, [(d, F32), (d, BF16)], lat_tl)(
        (x2[:, lat], y), (), (lat_param(g1n), lat_param(sh2n), lat_param(sc2n)), (w["norm2_g1"],), ())
    y = _mlp("mlp1")(flat(h), w["mlp_w1_1"], w["mlp_w2_1"]).reshape(bsz, length, d)
    (x4,) = _rowwise("res1b", _f_res, [(d, F32)], lat_tl)((x3, y), (), (lat_param(g2n),), (), ())
    return x4


def kernel(x, c, ctx, c_ctx, ada_w, ada_b, norm1_g, norm2_g, w_out, mlp_w1, mlp_w2, ev_w_in, ev_q_norm_g, ev_k_norm_g, ev_sgu_norm_g, ev_sgu_w, ev_sgu_b, od_w_in, od_q_norm_g, od_kv_norm_g, od_w_uq, od_w_ukv, od_conv_w, od_conv_b, od_ln_g, od_ln_b, final_g, loss_target, m_c_ctx, m_ada_w, m_ada_b, m_norm1_g, m_norm2_g, m_w_out, m_mlp_w1, m_mlp_w2, m_ev_w_in, m_ev_q_norm_g, m_ev_k_norm_g, m_ev_sgu_norm_g, m_ev_sgu_w, m_ev_sgu_b, m_od_w_in, m_od_q_norm_g, m_od_kv_norm_g, m_od_w_uq, m_od_w_ukv, m_od_conv_w, m_od_conv_b, m_od_ln_g, m_od_ln_b, m_final_g, v_c_ctx, v_ada_w, v_ada_b, v_norm1_g, v_norm2_g, v_w_out, v_mlp_w1, v_mlp_w2, v_ev_w_in, v_ev_q_norm_g, v_ev_k_norm_g, v_ev_sgu_norm_g, v_ev_sgu_w, v_ev_sgu_b, v_od_w_in, v_od_q_norm_g, v_od_kv_norm_g, v_od_w_uq, v_od_w_ukv, v_od_conv_w, v_od_conv_b, v_od_ln_g, v_od_ln_b, v_final_g):
    names = ["c_ctx", "ada_w", "ada_b", "norm1_g", "norm2_g", "w_out", "mlp_w1", "mlp_w2", "ev_w_in", "ev_q_norm_g", "ev_k_norm_g",
             "ev_sgu_norm_g", "ev_sgu_w", "ev_sgu_b", "od_w_in", "od_q_norm_g", "od_kv_norm_g", "od_w_uq", "od_w_ukv", "od_conv_w",
             "od_conv_b", "od_ln_g", "od_ln_b", "final_g"]
    local = dict(zip(names, [c_ctx, ada_w, ada_b, norm1_g, norm2_g, w_out, mlp_w1, mlp_w2, ev_w_in, ev_q_norm_g, ev_k_norm_g, ev_sgu_norm_g, ev_sgu_w, ev_sgu_b, od_w_in, od_q_norm_g, od_kv_norm_g, od_w_uq, od_w_ukv, od_conv_w, od_conv_b, od_ln_g, od_ln_b, final_g]))
    mom1 = dict(zip(names, [m_c_ctx, m_ada_w, m_ada_b, m_norm1_g, m_norm2_g, m_w_out, m_mlp_w1, m_mlp_w2, m_ev_w_in, m_ev_q_norm_g, m_ev_k_norm_g, m_ev_sgu_norm_g, m_ev_sgu_w, m_ev_sgu_b, m_od_w_in, m_od_q_norm_g, m_od_kv_norm_g, m_od_w_uq, m_od_w_ukv, m_od_conv_w, m_od_conv_b, m_od_ln_g, m_od_ln_b, m_final_g]))
    mom2 = dict(zip(names, [v_c_ctx, v_ada_w, v_ada_b, v_norm1_g, v_norm2_g, v_w_out, v_mlp_w1, v_mlp_w2, v_ev_w_in, v_ev_q_norm_g, v_ev_k_norm_g, v_ev_sgu_norm_g, v_ev_sgu_w, v_ev_sgu_b, v_od_w_in, v_od_q_norm_g, v_od_kv_norm_g, v_od_w_uq, v_od_w_ukv, v_od_conv_w, v_od_conv_b, v_od_ln_g, v_od_ln_b, v_final_g]))
    bsz, length, d = x.shape
    ctx_len = ctx.shape[1]
    me = _block_index(_mesh_pos())

    shard = {p: local[wn][layer] for p, wn, layer, _, _ in _PIECES}
    by_columns = {p: axis == 1 for p, _, _, axis, _ in _PIECES}
    stages = {s: [p for p, _, _, _, st in _PIECES if st == s] for s in "abc"}
    block_rows = {p: shard[p].shape[1] if by_columns[p] else shard[p].shape[0] for p in shard}

    def pad_block_rows(t, p):
        extra = -block_rows[p] % ROW_ALIGN
        return jnp.pad(t, [(0, 0)] * (t.ndim - 2) + [(0, extra), (0, 0)]) if extra else t

    def travelling(p):
        t = shard[p].astype(BF16)
        return pad_block_rows(t.T if by_columns[p] else t, p)

    mine = {s: [travelling(p) for p in stages[s]] for s in "abc"}
    tiny_pack = _Packing([(n, local[n].size) for n, _ in _SMALL_SHARDED], 8)

    def unpack_weights(s, gathered):
        return {p: g[:, :block_rows[p]].reshape(N_DEV * block_rows[p], g.shape[2]) for p, g in zip(stages[s], gathered)}

    def pack_grads(s, g):
        return [pad_block_rows(g[p].reshape(N_DEV, block_rows[p], g[p].shape[1]), p) for p in stages[s]]

    cond_local = jnp.concatenate([c, c_ctx[None], jnp.zeros((COND_ROWS - bsz - 1, d), F32)], axis=0)
    cond, gathered_tiny, *gathered_a = _all_gather(
        [cond_local, tiny_pack.pack({n: local[n] for n, _ in _SMALL_SHARDED})] + mine["a"], "gather_inputs")
    cond = cond.reshape(N_DEV * COND_ROWS, d)
    silu_op = _rowwise("silu", _f_silu, [(d, F32)], N_DEV * COND_ROWS)
    silu_rows, silu_pullback = jax.vjp(lambda r: silu_op((r[None],), (), (), (), ())[0][0], cond)
    mod_cols = ada_w.shape[2]
    mod_part = jnp.concatenate([_mm(silu_rows, ada_w[i], "nn", (F32,), f"ada{i}_fwd")[0] for i in range(2)], axis=0)
    (mod_all,) = _all_gather([mod_part], "gather_mod")
    mod_all = mod_all.reshape(N_DEV, 2, N_DEV * COND_ROWS, mod_cols)
    modrows = []
    for i in range(2):
        whole = mod_all[:, i].transpose(1, 0, 2).reshape(N_DEV * COND_ROWS, N_DEV * mod_cols) + ada_b[i]
        modrows.append(lax.dynamic_slice_in_dim(whole, me * COND_ROWS, COND_ROWS, axis=0)[:bsz + 1])

    weights_a = unpack_weights("a", gathered_a)
    gather_b, token_b = _push_start(mine["b"], mod_all, "gather_weights_b_start")
    gather_c, token_c = _push_start(mine["c"], token_b, "gather_weights_c_start")
    full = {n: local[n] for n in _REPLICATED}
    for n, axis in _SMALL_SHARDED:
        full[n] = _unshard(tiny_pack.piece(gathered_tiny, n, (N_DEV,)).reshape((N_DEV,) + local[n].shape), axis)

    xall = jnp.concatenate([ctx, x], axis=1)
    modrows0, modrows1 = modrows[0] + token_c[0, 0], modrows[1]
    w_a = dict(weights_a, norm1_g0=norm1_g[0][None], norm2_g0=norm2_g[0][None],
               **{n: full[n] for n in ("ev_q_norm_g", "ev_k_norm_g", "ev_sgu_norm_g", "ev_sgu_w", "ev_sgu_b")})
    (x1, h0), pull_a = jax.vjp(lambda x_, m0, w: _mixer0(x_, m0, w, bsz, length, ctx_len), xall, modrows0, w_a)
    w_b = dict(unpack_weights("b", _push_wait(gather_b, x1, mine["b"], me, "gather_weights_b_wait")), norm1_g1=norm1_g[1][None])
    (x2, h1), pull_b = jax.vjp(lambda x_, h_, m0, m1, w: _mlp0(x_, h_, m0, m1, w, bsz, length, ctx_len), x1, h0, modrows0, modrows1, w_b)
    w_c = dict(unpack_weights("c", _push_wait(gather_c, x2, mine["c"], me, "gather_weights_c_wait")), norm2_g1=norm2_g[1][None],
               **{n: full[n] for n in ("od_q_norm_g", "od_kv_norm_g", "od_conv_w", "od_conv_b", "od_ln_g", "od_ln_b")})
    x4, pull_c = jax.vjp(lambda x_, h_, m1, w: _layer1(x_, h_, m1, w, bsz, length, ctx_len), x2, h1, modrows1, w_c)
    loss_part, dx4, dfinal = _loss_head(x4, loss_target, final_g[None])
    loss = lax.psum(loss_part[0, 0], MESH_AXES)

    dx2, dh1, dmod1_c, g_c = pull_c(dx4)
    grads_c = pack_grads("c", g_c)
    exchange_c, token = _push_start(grads_c, dx2, "exchange_grads_c_start")
    dx1, dh0, dmod0_b, dmod1_b, g_b = pull_b((dx2 + token[0, 0], dh1))
    grads_b = pack_grads("b", g_b)
    exchange_b, token = _push_start(grads_b, dx1, "exchange_grads_b_start")
    dxall, dmod0_a, g_a = pull_a((dx1, dh0 + token[0, 0].astype(dh0.dtype)))
    grad_x = dxall[:, ctx_len:]
    dmodrows = [dmod0_a + dmod0_b, dmod1_b + dmod1_c]
    grads = {n: g[n] for g in (g_a, g_c) for n in g if n in full}
    grads["norm1_g"] = jnp.concatenate([g_a["norm1_g0"], g_b["norm1_g1"]], axis=0)
    grads["norm2_g"] = jnp.concatenate([g_a["norm2_g0"], g_c["norm2_g1"]], axis=0)
    grads["final_g"] = dfinal[0]

    dmod_local = jnp.concatenate([jnp.pad(dm, ((0, COND_ROWS - bsz - 1), (0, 0))) for dm in dmodrows], axis=0)
    (dmod_all,) = _all_gather([dmod_local], "gather_dmod")
    dmod_all = dmod_all.reshape(N_DEV, 2, COND_ROWS, N_DEV * mod_cols)
    grads_a = pack_grads("a", g_a)
    exchange_a, _ = _push_start(grads_a, dmod_all, "exchange_grads_a_start")
    reduced = {}
    grad_ada_w, grad_ada_b, dmod_mine = [], [], []
    for i in range(2):
        dmod = dmod_all[:, i].reshape(N_DEV * COND_ROWS, N_DEV * mod_cols)
        grad_ada_b.append(_sum_rows(dmod, f"ada{i}_db")[0])
        dmod_mine.append(lax.dynamic_slice_in_dim(dmod, me * mod_cols, mod_cols, axis=1))
        grad_ada_w.append(_mm(silu_rows, dmod_mine[i], "tn", (F32,), f"ada{i}_dw")[0])
    reduced["ada_w"], reduced["ada_b"] = jnp.stack(grad_ada_w), jnp.stack(grad_ada_b)
    dsilu = _mm(jnp.concatenate(dmod_mine, axis=1), jnp.concatenate([ada_w[0], ada_w[1]], axis=1), "nt", (F32,), "ada_dx")[0]
    (dcond,) = silu_pullback(dsilu)
    grads["c_ctx"] = _sum_rows(dcond.reshape(N_DEV, COND_ROWS, d)[:, bsz], "c_ctx_rows")[0]

    small_names = list(_REPLICATED) + [n for n, _ in _SMALL_SHARDED]
    small_pack = _Packing([(n, full[n].size) for n in small_names], 8)
    (small_all,) = _all_gather([small_pack.pack({n: grads[n].astype(F32) for n in small_names})], "gather_small_grads")
    small_sum = _sum_blocks(small_all, "sum_small_grads")
    for n in _REPLICATED:
        reduced[n] = small_pack.piece(small_sum, n).reshape(local[n].shape)
    for n, axis in _SMALL_SHARDED:
        whole = small_pack.piece(small_sum, n).reshape(full[n].shape)
        reduced[n] = lax.dynamic_slice_in_dim(whole, me * local[n].shape[axis], local[n].shape[axis], axis=axis)

    def own_blocks(blocks):
        return [lax.dynamic_index_in_dim(t, me, 0, keepdims=False) for t in blocks]

    received = {s: _push_wait(handle, small_sum, own_blocks(blocks), me, f"exchange_grads_{s}_wait")
                for s, handle, blocks in (("c", exchange_c, grads_c), ("b", exchange_b, grads_b), ("a", exchange_a, grads_a))}
    piece_grad = {}
    for s in "abc":
        for p, blocks in zip(stages[s], received[s]):
            summed = _sum_blocks(blocks, "sum_grads_" + p)[:block_rows[p]]
            piece_grad[p] = summed.T if by_columns[p] else summed
    for n in ("w_out", "mlp_w1", "mlp_w2"):
        reduced[n] = jnp.stack([piece_grad[p] for p, wn, _, _, _ in _PIECES if wn == n])
    for n in ("ev_w_in", "od_w_in", "od_w_uq", "od_w_ukv"):
        reduced[n] = piece_grad[n][None]

    delta, new_m, new_v = {}, {}, {}
    for n in names:
        delta[n], new_m[n], new_v[n] = _adamw(local[n], reduced[n], mom1[n], mom2[n], "adamw_" + n)
    return (loss, grad_x, *[reduced[n] for n in names], *[delta[n] for n in names], *[new_m[n] for n in names], *[new_v[n] for n in names])
```

```python
import functools
import math

import jax
import jax.numpy as jnp
from jax import lax
from jax.experimental import pallas as pl
from jax.experimental.pallas import tpu as pltpu

F32, BF16 = jnp.float32, jnp.bfloat16

EPS = 1e-6
GRID_W = 64
ROPE_THETA = 10000.0
A_HEAD_DIM, A_Q_HEADS, A_KV_HEADS = 64, 8, 2
B_GROUPS, B_GROUP_DIM, B_CHUNK = 8, 64, 128
C_HEADS, C_NOPE, C_ROPE, C_V, C_Q_RANK, C_KV_RANK = 8, 64, 32, 64, 256, 128
D_CONV = 31
CONV_PAD = D_CONV // 2
N_MOD = 6
N_DEV = 8
MESH_AXES = ("x", "y", "c")

ADAM_LR, ADAM_B1, ADAM_B2, ADAM_EPS, ADAM_WD, ADAM_STEP = 0.001, 0.9, 0.999, 1e-08, 0.01, 10

VMEM_LIMIT = 56 * 1024 * 1024
PACK_COLS = 1024
ROW_ALIGN = 16
PACK_ROWS = 32
COND_ROWS = 8


def _pcall(body, **kw):
    return pl.pallas_call(body, **kw)


def _params(sem=None):
    return pltpu.CompilerParams(dimension_semantics=sem, vmem_limit_bytes=VMEM_LIMIT)


def _pick(n, cands):
    for c in cands:
        if n % c == 0:
            return c
    return n


def _mm(a, b, mode, out_dtypes, name, epi=None, extras=()):
    if mode == "tn":
        kk, m = a.shape
    else:
        m, kk = a.shape
    n = b.shape[0] if mode == "nt" else b.shape[1]
    tm = _pick(m, (1152, 1024, 896, 768, 512, 256, 128))
    tn = _pick(n, (1024, 896, 768, 512, 256, 128))
    tk = kk if kk <= 1024 else _pick(kk, (1024, 896, 768, 512, 256, 128))
    nk = kk // tk
    ne, no = len(extras), len(out_dtypes)
    a_spec = pl.BlockSpec((tk, tm), lambda i, j, k: (k, i)) if mode == "tn" else pl.BlockSpec((tm, tk), lambda i, j, k: (i, k))
    b_spec = pl.BlockSpec((tn, tk), lambda i, j, k: (j, k)) if mode == "nt" else pl.BlockSpec((tk, tn), lambda i, j, k: (k, j))
    t_spec = pl.BlockSpec((tm, tn), lambda i, j, k: (i, j))
    dn = {"nn": ((1,), (0,)), "nt": ((1,), (1,)), "tn": ((0,), (0,))}[mode]

    def body(a_ref, b_ref, *rest):
        extra_refs, out_refs = rest[:ne], rest[ne:ne + no]

        def finish(acc):
            outs = (acc,) if epi is None else epi(acc, *[r[...] for r in extra_refs])
            for r, o in zip(out_refs, outs):
                r[...] = o.astype(r.dtype)

        part = lax.dot_general(a_ref[...].astype(BF16), b_ref[...].astype(BF16), (dn, ((), ())), preferred_element_type=F32)
        if nk == 1:
            finish(part)
        else:
            acc_ref = rest[-1]
            k = pl.program_id(2)

            @pl.when(k == 0)
            def _():
                acc_ref[...] = part

            @pl.when(k > 0)
            def _():
                acc_ref[...] += part

            @pl.when(k == nk - 1)
            def _():
                finish(acc_ref[...])

    outs = _pcall(
        body, name=name, grid=(m // tm, n // tn, nk),
        in_specs=[a_spec, b_spec] + [t_spec] * ne,
        out_specs=[t_spec] * no,
        out_shape=[jax.ShapeDtypeStruct((m, n), d) for d in out_dtypes],
        scratch_shapes=[pltpu.VMEM((tm, tn), F32)] if nk > 1 else [],
        compiler_params=_params(("parallel", "parallel", "arbitrary")),
    )(a, b, *extras)
    return outs


def _linear(name, transposed=False, out_dtype=F32):
    fwd_mode, dx_mode = ("nt", "nn") if transposed else ("nn", "nt")

    @jax.custom_vjp
    def op(x, w):
        return _mm(x, w, fwd_mode, (out_dtype,), name + "_fwd")[0]

    def fwd(x, w):
        return op(x, w), (x, w)

    def bwd(res, dy):
        x, w = res
        dx = _mm(dy, w, dx_mode, (x.dtype,), name + "_dx")[0]
        dw = _mm(dy, x, "tn", (w.dtype,), name + "_dw")[0] if transposed else _mm(x, dy, "tn", (w.dtype,), name + "_dw")[0]
        return dx, dw

    op.defvjp(fwd, bwd)
    return op


_NT, _NN, _TN = (((1,), (1,)), ((), ())), (((1,), (0,)), ((), ())), (((0,), (0,)), ((), ()))
_ROW_TILES = (1152, 1024, 768, 512, 256, 128)


def _whole(w):
    return pl.BlockSpec(w.shape, lambda i: (0, 0))


def _groups_apply(x, ws, out_dtypes, name):
    n, (m, kk) = len(ws), x.shape
    tm = _pick(m, _ROW_TILES)

    def body(x_ref, *refs):
        a = x_ref[...].astype(BF16)
        for w_ref, o_ref in zip(refs[:n], refs[n:]):
            o_ref[...] = lax.dot_general(a, w_ref[...], _NT, preferred_element_type=F32).astype(o_ref.dtype)

    return _pcall(
        body, name=name, grid=(m // tm,),
        in_specs=[pl.BlockSpec((tm, kk), lambda i: (i, 0))] + [_whole(w) for w in ws],
        out_specs=[pl.BlockSpec((tm, w.shape[0]), lambda i: (i, 0)) for w in ws],
        out_shape=[jax.ShapeDtypeStruct((m, w.shape[0]), dt) for w, dt in zip(ws, out_dtypes)],
        compiler_params=_params(("parallel",)),
    )(x, *ws)


def _groups_sum(xs, ws, out_dtype, name):
    n, m, kk = len(ws), xs[0].shape[0], ws[0].shape[1]
    tm = _pick(m, _ROW_TILES)

    def body(*refs):
        acc = None
        for x_ref, w_ref in zip(refs[:n], refs[n:2 * n]):
            part = lax.dot_general(x_ref[...].astype(BF16), w_ref[...], _NN, preferred_element_type=F32)
            acc = part if acc is None else acc + part
        refs[2 * n][...] = acc.astype(out_dtype)

    return _pcall(
        body, name=name, grid=(m // tm,),
        in_specs=[pl.BlockSpec((tm, w.shape[0]), lambda i: (i, 0)) for w in ws] + [_whole(w) for w in ws],
        out_specs=pl.BlockSpec((tm, kk), lambda i: (i, 0)), out_shape=jax.ShapeDtypeStruct((m, kk), out_dtype),
        compiler_params=_params(("parallel",)),
    )(*xs, *ws)


def _groups_outer(xs, y, ws, name):
    n, (m, kk) = len(ws), y.shape
    tk = _pick(m, (768, 512, 256, 128))
    steps = m // tk

    def body(y_ref, *refs):
        x_refs, o_refs, acc_refs = refs[:n], refs[n:2 * n], refs[2 * n:]
        k = pl.program_id(0)
        b = y_ref[...].astype(BF16)
        for x_ref, o_ref, acc_ref in zip(x_refs, o_refs, acc_refs):
            part = lax.dot_general(x_ref[...].astype(BF16), b, _TN, preferred_element_type=F32)

            @pl.when(k == 0)
            def _(acc_ref=acc_ref, part=part):
                acc_ref[...] = part

            @pl.when(k > 0)
            def _(acc_ref=acc_ref, part=part):
                acc_ref[...] += part

            @pl.when(k == steps - 1)
            def _(acc_ref=acc_ref, o_ref=o_ref):
                o_ref[...] = acc_ref[...].astype(o_ref.dtype)

    return _pcall(
        body, name=name, grid=(steps,),
        in_specs=[pl.BlockSpec((tk, kk), lambda k: (k, 0))] + [pl.BlockSpec((tk, w.shape[0]), lambda k: (k, 0)) for w in ws],
        out_specs=[_whole(w) for w in ws], out_shape=[jax.ShapeDtypeStruct(w.shape, w.dtype) for w in ws],
        scratch_shapes=[pltpu.VMEM(w.shape, F32) for w in ws], compiler_params=_params(("arbitrary",)),
    )(y, *xs)


def _linear_multi(name, out_dtypes):
    @jax.custom_vjp
    def op(x, wts):
        return tuple(_groups_apply(x, wts, out_dtypes, name + "_fwd"))

    def fwd(x, wts):
        return op(x, wts), (x, wts)

    def bwd(res, dys):
        x, wts = res
        return _groups_sum(dys, wts, x.dtype, name + "_dx"), tuple(_groups_outer(dys, x, wts, name + "_dw"))

    op.defvjp(fwd, bwd)
    return op


def _linear_sum(name):
    @jax.custom_vjp
    def op(xs, ws):
        return _groups_sum(xs, ws, F32, name + "_fwd")

    def fwd(xs, ws):
        return op(xs, ws), (xs, ws)

    def bwd(res, dy):
        xs, ws = res
        return tuple(_groups_apply(dy, ws, [x.dtype for x in xs], name + "_dx")), tuple(_groups_outer(xs, dy, ws, name + "_dw"))

    op.defvjp(fwd, bwd)
    return op


def _relu2_epi(acc):
    return jnp.square(jnp.maximum(acc, 0.0)), acc


def _relu2_bwd_epi(acc, a):
    return (acc * (2.0 * jnp.maximum(a.astype(F32), 0.0)),)


def _mlp(name):
    @jax.custom_vjp
    def op(h, w1t, w2):
        s, _ = _mm(h, w1t, "nt", (BF16, BF16), name + "_up", epi=_relu2_epi)
        return _mm(s, w2, "nn", (F32,), name + "_down")[0]

    def fwd(h, w1t, w2):
        s, a = _mm(h, w1t, "nt", (BF16, BF16), name + "_up", epi=_relu2_epi)
        return _mm(s, w2, "nn", (F32,), name + "_down")[0], (h, w1t, w2, s, a)

    def bwd(res, dy):
        h, w1t, w2, s, a = res
        da = _mm(dy, w2, "nt", (BF16,), name + "_ds", epi=_relu2_bwd_epi, extras=(a,))[0]
        dw2 = _mm(s, dy, "tn", (w2.dtype,), name + "_dw2")[0]
        dw1t = _mm(da, h, "tn", (w1t.dtype,), name + "_dw1")[0]
        dh = _mm(da, w1t, "nn", (h.dtype,), name + "_dh")[0]
        return dh, dw1t, dw2

    op.defvjp(fwd, bwd)
    return op


def _two_pass_dot(x, m):
    hi = x.astype(BF16)
    lo = (x - hi.astype(F32)).astype(BF16)
    mb = m.astype(BF16)
    return jnp.dot(hi, mb, preferred_element_type=F32) + jnp.dot(lo, mb, preferred_element_type=F32)


@jax.custom_vjp
def _sym_dot(x, m):
    return _two_pass_dot(x, m)


def _sym_dot_fwd(x, m):
    return _two_pass_dot(x, m), m


def _sym_dot_bwd(m, g):
    return _two_pass_dot(g, m), jnp.zeros_like(m)


_sym_dot.defvjp(_sym_dot_fwd, _sym_dot_bwd)


def _neighbour(x):
    lane = lax.broadcasted_iota(jnp.int32, x.shape, 1)
    return jnp.where(lane % 2 == 0, pltpu.roll(x, x.shape[1] - 1, 1), pltpu.roll(x, 1, 1))


@jax.custom_vjp
def _swap_pairs(x):
    return _neighbour(x)


_swap_pairs.defvjp(lambda x: (_neighbour(x), None), lambda _, g: (_neighbour(g),))


def _lane_group(x, k):
    return x[:, k * B_GROUP_DIM:(k + 1) * B_GROUP_DIM]


@jax.custom_vjp
def _group_mix(ws, v):
    return jnp.concatenate([jnp.dot(w.astype(BF16), _lane_group(v, k).astype(BF16), preferred_element_type=F32)
                            for k, w in enumerate(ws)], axis=1)


def _group_mix_fwd(ws, v):
    return _group_mix(ws, v), (ws, v)


def _group_mix_bwd(res, d):
    ws, v = res
    parts = [(_lane_group(d, k).astype(BF16), _lane_group(v, k).astype(BF16)) for k in range(len(ws))]
    dws = tuple(lax.dot_general(dk, vk, (((1,), (1,)), ((), ())), preferred_element_type=F32) for dk, vk in parts)
    dv = jnp.concatenate([lax.dot_general(w.astype(BF16), dk, (((0,), (0,)), ((), ())), preferred_element_type=F32)
                          for w, (dk, _) in zip(ws, parts)], axis=1)
    return dws, dv


_group_mix.defvjp(_group_mix_fwd, _group_mix_bwd)


def _rowwise(name, f, out_specs, tl, ctx_blocks=0):
    def seg(l, s):
        return jnp.where(l >= ctx_blocks, s - 1, 0) if s > 1 else 0

    def specs(rows, tabs, pers, glbs, consts):
        row_specs = [pl.BlockSpec((1, tl, r.shape[2]), lambda b, l: (b, l, 0)) for r in rows]
        tab_specs = [pl.BlockSpec((tl, t.shape[1]), lambda b, l: (l, 0)) for t in tabs]
        per_specs = [pl.BlockSpec((1, 1, 1, p.shape[3]), functools.partial(lambda b, l, s: (b, seg(l, s), 0, 0), s=p.shape[1])) for p in pers]
        glb_specs = [pl.BlockSpec(g.shape, functools.partial(lambda b, l, nd: (0,) * nd, nd=g.ndim)) for g in glbs]
        const_specs = [pl.BlockSpec(c.shape, functools.partial(lambda b, l, nd: (0,) * nd, nd=c.ndim)) for c in consts]
        return row_specs, tab_specs, per_specs, glb_specs, const_specs

    def load(refs_rows, refs_tabs, refs_pers, refs_glbs, refs_consts):
        return (tuple(r[0].astype(F32) for r in refs_rows), tuple(t[...] for t in refs_tabs),
                tuple(p[0, 0].astype(F32) for p in refs_pers), tuple(g[...].astype(F32) for g in refs_glbs),
                tuple(c[...] for c in refs_consts))

    def call_fwd(rows, tabs, pers, glbs, consts):
        bsz, length = rows[0].shape[:2]
        nr, nt, npp, ng, nc = len(rows), len(tabs), len(pers), len(glbs), len(consts)
        rs, ts, ps, gs, cs = specs(rows, tabs, pers, glbs, consts)

        def body(*refs):
            ins, outs = refs[:nr + nt + npp + ng + nc], refs[nr + nt + npp + ng + nc:]
            r, t, p, g, c = load(ins[:nr], ins[nr:nr + nt], ins[nr + nt:nr + nt + npp], ins[nr + nt + npp:nr + nt + npp + ng], ins[nr + nt + npp + ng:])
            for o_ref, o in zip(outs, f(r, t, p, g, c)):
                o_ref[0] = o.astype(o_ref.dtype)

        return _pcall(
            body, name=name + "_fwd", grid=(bsz, length // tl),
            in_specs=rs + ts + ps + gs + cs,
            out_specs=[pl.BlockSpec((1, tl, w), lambda b, l: (b, l, 0)) for w, _ in out_specs],
            out_shape=[jax.ShapeDtypeStruct((bsz, length, w), d) for w, d in out_specs],
            compiler_params=_params(("parallel", "parallel")),
        )(*rows, *tabs, *pers, *glbs, *consts)

    def call_bwd(rows, tabs, pers, glbs, consts, cts):
        bsz, length = rows[0].shape[:2]
        nr, nt, npp, ng, nc, no = len(rows), len(tabs), len(pers), len(glbs), len(consts), len(cts)
        rs, ts, ps, gs, cs = specs(rows, tabs, pers, glbs, consts)
        n_in = nr + nt + npp + ng + nc

        def body(*refs):
            ins, ct_refs, outs = refs[:n_in], refs[n_in:n_in + no], refs[n_in + no:]
            r, t, p, g, c = load(ins[:nr], ins[nr:nr + nt], ins[nr + nt:nr + nt + npp], ins[nr + nt + npp:nr + nt + npp + ng], ins[nr + nt + npp + ng:])
            _, vjp = jax.vjp(lambda r_, p_, g_: tuple(f(r_, t, p_, g_, c)), r, p, g)
            dr, dp, dg = vjp(tuple(ct[0].astype(F32) for ct in ct_refs))
            dr_refs, dp_refs, dg_refs = outs[:nr], outs[nr:nr + npp], outs[nr + npp:]
            for ref, d in zip(dr_refs, dr):
                ref[0] = d.astype(ref.dtype)
            b, l = pl.program_id(0), pl.program_id(1)
            first_of_segment = (l == 0) | (l == ctx_blocks)
            for ref, d in zip(dp_refs, dp):
                @pl.when(first_of_segment)
                def _(ref=ref, d=d):
                    ref[0, 0] = d

                @pl.when(jnp.logical_not(first_of_segment))
                def _(ref=ref, d=d):
                    ref[0, 0] += d
            first = (b == 0) & (l == 0)
            for ref, d in zip(dg_refs, dg):
                @pl.when(first)
                def _(ref=ref, d=d):
                    ref[...] = d

                @pl.when(jnp.logical_not(first))
                def _(ref=ref, d=d):
                    ref[...] += d

        ct_specs = [pl.BlockSpec((1, tl, w), lambda b, l: (b, l, 0)) for w, _ in out_specs]
        outs = _pcall(
            body, name=name + "_bwd", grid=(bsz, length // tl),
            in_specs=rs + ts + ps + gs + cs + ct_specs,
            out_specs=rs + ps + gs,
            out_shape=[jax.ShapeDtypeStruct(r.shape, r.dtype) for r in rows]
            + [jax.ShapeDtypeStruct(p.shape, F32) for p in pers] + [jax.ShapeDtypeStruct(g.shape, F32) for g in glbs],
            compiler_params=_params(("arbitrary", "arbitrary")),
        )(*rows, *tabs, *pers, *glbs, *consts, *cts)
        return tuple(outs[:nr]), tuple(outs[nr:nr + npp]), tuple(outs[nr + npp:])

    @jax.custom_vjp
    def op(rows, tabs, pers, glbs, consts):
        return tuple(call_fwd(rows, tabs, pers, glbs, consts))

    def fwd(rows, tabs, pers, glbs, consts):
        return op(rows, tabs, pers, glbs, consts), (rows, tabs, pers, glbs, consts)

    def bwd(res, cts):
        rows, tabs, pers, glbs, consts = res
        dr, dp, dg = call_bwd(rows, tabs, pers, glbs, consts, tuple(cts))
        dp = tuple(d.astype(p.dtype) for d, p in zip(dp, pers))
        dg = tuple(d.astype(g.dtype) for d, g in zip(dg, glbs))
        return dr, tuple(jnp.zeros_like(t) for t in tabs), dp, dg, tuple(jnp.zeros_like(c) for c in consts)

    op.defvjp(fwd, bwd)
    return op


def _rms(x, g):
    return x * lax.rsqrt(jnp.mean(x * x, axis=-1, keepdims=True) + EPS) * g


def _f_silu(r, t, p, g, c):
    return (jax.nn.silu(r[0]),)


def _f_modulate(r, t, p, g, c):
    shift, scale = p
    return (_rms(r[0], g[0]) * (1.0 + scale) + shift,)


def _f_res_modulate(r, t, p, g, c):
    x, y = r
    gate, shift, scale = p
    xn = x + gate * y
    return xn, _rms(xn, g[0]) * (1.0 + scale) + shift


def _f_res(r, t, p, g, c):
    return (r[0] + p[0] * r[1],)


def _f_headnorm_rope(r, t, p, g, c):
    x = r[0]
    cos, sin = t
    xn = x * lax.rsqrt(_sym_dot(x * x, c[0]) + EPS) * g[0]
    return (xn * cos + _swap_pairs(xn) * sin,)


def _f_rope(r, t, p, g, c):
    x = r[0]
    cos, sin = t
    return (x * cos + _swap_pairs(x) * sin,)


def _f_rope_narrow(r, t, p, g, c):
    x = r[0]
    cos, sin = t
    return (x * cos + _sym_dot(x, c[0]) * sin,)


def _f_rms(r, t, p, g, c):
    return (_rms(r[0], g[0]),)


def _f_sgu_pre(r, t, p, g, c):
    u = jax.nn.gelu(r[0])
    v = jax.nn.gelu(r[1])
    vn = v * lax.rsqrt(_sym_dot(v * v, c[0]) + EPS) * g[0]
    return u, vn


def _f_sgu_mix(r, t, p, g, c):
    u, vn = r
    return (u * (g[B_GROUPS] + _group_mix(tuple(g[:B_GROUPS]), vn)),)


def _f_glu(r, t, p, g, c):
    return (r[0] * jax.nn.sigmoid(r[1]),)


def _f_ln_silu(r, t, p, g, c):
    x = r[0]
    mu = jnp.mean(x, axis=-1, keepdims=True)
    var = jnp.mean(jnp.square(x - mu), axis=-1, keepdims=True)
    return (jax.nn.silu((x - mu) * lax.rsqrt(var + EPS) * g[0] + g[1]),)


LOG2_E = 1.4426950408889634
LN_2 = 0.6931471805599453


def _attention(name, ctx_len, kv_per_step):
    def tiles(q):
        tq = _pick(q.shape[2], (256, 128))
        if ctx_len:
            tq = math.gcd(tq, ctx_len)
        return tq, ctx_len // tq

    def scores(q_tile, kk, dk):
        qs = (q_tile.astype(F32) * (LOG2_E * dk ** -0.5)).astype(BF16)
        return qs, lax.dot_general(qs, kk, (((1,), (1,)), ((), ())), preferred_element_type=F32)

    def by_segment(qi, cb, lk, run):
        if cb > 0:
            @pl.when(qi < cb)
            def _():
                run(ctx_len)

            @pl.when(qi >= cb)
            def _():
                run(lk)
        else:
            run(lk)

    def call_fwd(q, k, v):
        bsz, hq, lq, dk = q.shape
        _, hk, lk, dv = v.shape
        grp = hq // hk
        tq, cb = tiles(q)
        v_ones = jnp.concatenate([v, jnp.ones_like(v)], axis=-1)

        def body(q_ref, k_ref, v_ref, o_ref, lse_ref):
            def run(nk):
                for j in range(kv_per_step):
                    kk, vv = k_ref[0, j, :nk], v_ref[0, j, :nk]
                    for h in range(j * grp, (j + 1) * grp):
                        _, s = scores(q_ref[0, h], kk, dk)
                        m = jnp.max(s, axis=-1, keepdims=True)
                        acc = jnp.dot(jnp.exp2(s - m).astype(BF16), vv, preferred_element_type=F32)
                        l = acc[:, dv:dv + 1]
                        o_ref[0, h] = (acc[:, :dv] / l).astype(o_ref.dtype)
                        lse_ref[0, h] = m + jnp.log2(l)

            by_segment(pl.program_id(2), cb, lk, run)

        qh = grp * kv_per_step
        return _pcall(
            body, name=name + "_fwd", grid=(bsz, hk // kv_per_step, lq // tq),
            in_specs=[pl.BlockSpec((1, qh, tq, dk), lambda b, h, i: (b, h, i, 0)),
                      pl.BlockSpec((1, kv_per_step, lk, dk), lambda b, h, i: (b, h, 0, 0)),
                      pl.BlockSpec((1, kv_per_step, lk, 2 * dv), lambda b, h, i: (b, h, 0, 0))],
            out_specs=[pl.BlockSpec((1, qh, tq, dv), lambda b, h, i: (b, h, i, 0)), pl.BlockSpec((1, qh, tq, 1), lambda b, h, i: (b, h, i, 0))],
            out_shape=[jax.ShapeDtypeStruct((bsz, hq, lq, dv), BF16), jax.ShapeDtypeStruct((bsz, hq, lq, 1), F32)],
            compiler_params=_params(("parallel", "parallel", "parallel")),
        )(q, k, v_ones)

    def call_bwd(q, k, v, o, lse, do):
        bsz, hq, lq, dk = q.shape
        _, hk, lk, dv = v.shape
        grp = hq // hk
        tq, cb = tiles(q)
        nq = lq // tq

        def body(q_ref, k_ref, v_ref, o_ref, lse_ref, do_ref, dq_ref, dk_ref, dv_ref):
            qi = pl.program_id(2)

            @pl.when(qi == 0)
            def _():
                dk_ref[...] = jnp.zeros_like(dk_ref)
                dv_ref[...] = jnp.zeros_like(dv_ref)

            def run(nk):
                for j in range(kv_per_step):
                    kk, vv = k_ref[0, j, :nk], v_ref[0, j, :nk]
                    dk_sum = dv_sum = None
                    for h in range(j * grp, (j + 1) * grp):
                        dd = do_ref[0, h]
                        qs, s = scores(q_ref[0, h], kk, dk)
                        p = jnp.exp2(s - lse_ref[0, h])
                        delta = jnp.sum(dd.astype(F32) * o_ref[0, h].astype(F32), axis=-1, keepdims=True)
                        dp = lax.dot_general(dd, vv, (((1,), (1,)), ((), ())), preferred_element_type=F32)
                        t = (p * (dp - delta)).astype(BF16)
                        dq_ref[0, h] = (jnp.dot(t, kk, preferred_element_type=F32) * dk ** -0.5).astype(dq_ref.dtype)
                        dk_h = lax.dot_general(t, qs, (((0,), (0,)), ((), ())), preferred_element_type=F32)
                        dv_h = lax.dot_general(p.astype(BF16), dd, (((0,), (0,)), ((), ())), preferred_element_type=F32)
                        dk_sum = dk_h if dk_sum is None else dk_sum + dk_h
                        dv_sum = dv_h if dv_sum is None else dv_sum + dv_h
                    dk_ref[0, j, :nk] += dk_sum
                    dv_ref[0, j, :nk] += dv_sum

            by_segment(qi, cb, lk, run)

            @pl.when(qi == nq - 1)
            def _():
                dk_ref[...] = dk_ref[...] * LN_2

        qh = grp * kv_per_step
        q_spec = pl.BlockSpec((1, qh, tq, dk), lambda b, h, i: (b, h, i, 0))
        k_spec = pl.BlockSpec((1, kv_per_step, lk, dk), lambda b, h, i: (b, h, 0, 0))
        v_spec = pl.BlockSpec((1, kv_per_step, lk, dv), lambda b, h, i: (b, h, 0, 0))
        o_spec = pl.BlockSpec((1, qh, tq, dv), lambda b, h, i: (b, h, i, 0))
        lse_spec = pl.BlockSpec((1, qh, tq, 1), lambda b, h, i: (b, h, i, 0))
        return _pcall(
            body, name=name + "_bwd", grid=(bsz, hk // kv_per_step, nq),
            in_specs=[q_spec, k_spec, v_spec, o_spec, lse_spec, o_spec],
            out_specs=[q_spec, k_spec, v_spec],
            out_shape=[jax.ShapeDtypeStruct(q.shape, BF16), jax.ShapeDtypeStruct(k.shape, F32), jax.ShapeDtypeStruct(v.shape, F32)],
            compiler_params=_params(("parallel", "parallel", "arbitrary")),
        )(q, k, v, o, lse, do)

    @jax.custom_vjp
    def op(q, k, v):
        return call_fwd(q, k, v)[0]

    def fwd(q, k, v):
        o, lse = call_fwd(q, k, v)
        return o, (q, k, v, o, lse)

    def bwd(res, do):
        q, k, v, o, lse = res
        dq, dk, dv = call_bwd(q, k, v, o, lse, do)
        return dq, dk.astype(k.dtype), dv.astype(v.dtype)

    op.defvjp(fwd, bwd)
    return op


def _attention_rows(name, ctx_len, heads, kv_heads):
    grp = heads // kv_heads

    def by_segment(qi, cb, lk, run):
        if cb > 0:
            @pl.when(qi < cb)
            def _():
                run(ctx_len)

            @pl.when(qi >= cb)
            def _():
                run(lk)
        else:
            run(lk)

    def head(ref, h, hd):
        return ref[0, :, h * hd:(h + 1) * hd]

    def scores(q_tile, kk, hd):
        qs = (q_tile.astype(F32) * (LOG2_E * hd ** -0.5)).astype(BF16)
        return qs, lax.dot_general(qs, kk, _NT, preferred_element_type=F32)

    def shapes(q):
        bsz, length, width = q.shape
        tq = math.gcd(_pick(length, (256, 128)), ctx_len) if ctx_len else _pick(length, (256, 128))
        return bsz, length, width // heads, tq, ctx_len // tq

    def call_fwd(q, k, v):
        bsz, length, hd, tq, cb = shapes(q)

        def body(q_ref, k_ref, v_ref, o_ref, lse_ref):
            def run(nk):
                k_all, v_all = k_ref[0, :nk], v_ref[0, :nk]
                outs = []
                for j in range(kv_heads):
                    kk = k_all[:, j * hd:(j + 1) * hd]
                    v_ones = jnp.concatenate([v_all[:, j * hd:(j + 1) * hd], jnp.ones((nk, hd), BF16)], axis=1)
                    for h in range(j * grp, (j + 1) * grp):
                        _, s = scores(head(q_ref, h, hd), kk, hd)
                        m = jnp.max(s, axis=-1, keepdims=True)
                        acc = jnp.dot(jnp.exp2(s - m).astype(BF16), v_ones, preferred_element_type=F32)
                        l = acc[:, hd:hd + 1]
                        outs.append((acc[:, :hd] / l).astype(BF16))
                        lse_ref[0, h] = m + jnp.log2(l)
                o_ref[0] = jnp.concatenate(outs, axis=1)

            by_segment(pl.program_id(1), cb, length, run)

        q_spec = pl.BlockSpec((1, tq, heads * hd), lambda b, i: (b, i, 0))
        kv_spec = pl.BlockSpec((1, length, kv_heads * hd), lambda b, i: (b, 0, 0))
        return _pcall(
            body, name=name + "_fwd", grid=(bsz, length // tq), in_specs=[q_spec, kv_spec, kv_spec],
            out_specs=[q_spec, pl.BlockSpec((1, heads, tq, 1), lambda b, i: (b, 0, i, 0))],
            out_shape=[jax.ShapeDtypeStruct(q.shape, BF16), jax.ShapeDtypeStruct((bsz, heads, length, 1), F32)],
            compiler_params=_params(("parallel", "parallel")),
        )(q, k, v)

    def call_bwd(q, k, v, o, lse, do):
        bsz, length, hd, tq, cb = shapes(q)
        nq = length // tq

        def body(q_ref, k_ref, v_ref, o_ref, lse_ref, do_ref, dq_ref, dk_ref, dv_ref, dk_acc, dv_acc):
            qi = pl.program_id(1)

            @pl.when(qi == 0)
            def _():
                dk_acc[...] = jnp.zeros_like(dk_acc)
                dv_acc[...] = jnp.zeros_like(dv_acc)

            def run(nk):
                k_all, v_all = k_ref[0, :nk], v_ref[0, :nk]
                dqs, dks, dvs = [], [], []
                for j in range(kv_heads):
                    kk, vv = k_all[:, j * hd:(j + 1) * hd], v_all[:, j * hd:(j + 1) * hd]
                    dk_sum = dv_sum = None
                    for h in range(j * grp, (j + 1) * grp):
                        dd = head(do_ref, h, hd)
                        qs, s = scores(head(q_ref, h, hd), kk, hd)
                        p = jnp.exp2(s - lse_ref[0, h])
                        delta = jnp.sum(dd.astype(F32) * head(o_ref, h, hd).astype(F32), axis=-1, keepdims=True)
                        t = (p * (lax.dot_general(dd, vv, _NT, preferred_element_type=F32) - delta)).astype(BF16)
                        dqs.append((jnp.dot(t, kk, preferred_element_type=F32) * hd ** -0.5).astype(BF16))
                        dk_h = lax.dot_general(t, qs, _TN, preferred_element_type=F32)
                        dv_h = lax.dot_general(p.astype(BF16), dd, _TN, preferred_element_type=F32)
                        dk_sum = dk_h if dk_sum is None else dk_sum + dk_h
                        dv_sum = dv_h if dv_sum is None else dv_sum + dv_h
                    dks.append(dk_sum)
                    dvs.append(dv_sum)
                dq_ref[0] = jnp.concatenate(dqs, axis=1)
                dk_acc[:nk] += jnp.concatenate(dks, axis=1)
                dv_acc[:nk] += jnp.concatenate(dvs, axis=1)

            by_segment(qi, cb, length, run)

            @pl.when(qi == nq - 1)
            def _():
                dk_ref[0] = (dk_acc[...] * LN_2).astype(dk_ref.dtype)
                dv_ref[0] = dv_acc[...].astype(dv_ref.dtype)

        q_spec = pl.BlockSpec((1, tq, heads * hd), lambda b, i: (b, i, 0))
        kv_spec = pl.BlockSpec((1, length, kv_heads * hd), lambda b, i: (b, 0, 0))
        lse_spec = pl.BlockSpec((1, heads, tq, 1), lambda b, i: (b, 0, i, 0))
        return _pcall(
            body, name=name + "_bwd", grid=(bsz, nq), in_specs=[q_spec, kv_spec, kv_spec, q_spec, lse_spec, q_spec],
            out_specs=[q_spec, kv_spec, kv_spec],
            out_shape=[jax.ShapeDtypeStruct(q.shape, BF16), jax.ShapeDtypeStruct(k.shape, BF16), jax.ShapeDtypeStruct(v.shape, BF16)],
            scratch_shapes=[pltpu.VMEM(k.shape[1:], F32), pltpu.VMEM(v.shape[1:], F32)],
            compiler_params=_params(("parallel", "arbitrary")),
        )(q, k, v, o, lse, do)

    @jax.custom_vjp
    def op(q, k, v):
        return call_fwd(q, k, v)[0]

    def fwd(q, k, v):
        o, lse = call_fwd(q, k, v)
        return o, (q, k, v, o, lse)

    def bwd(res, do):
        q, k, v, o, lse = res
        dq, dk, dv = call_bwd(q, k, v, o, lse, do)
        return dq, dk.astype(k.dtype), dv.astype(v.dtype)

    op.defvjp(fwd, bwd)
    return op


def _attention_latent_rows(name, heads, nope, rope, dv, heads_per_step):
    lanes = 128
    scale = (nope + rope) ** -0.5

    def shapes(q, kn):
        bsz, lq, _ = q.shape
        return bsz, lq, kn.shape[1], _pick(lq, (256, 128))

    def scores(q_tile, kk):
        qs = (q_tile.astype(F32) * (LOG2_E * scale)).astype(BF16)
        return qs, lax.dot_general(qs, kk, _NT, preferred_element_type=F32)

    def call_fwd(q, kn, kr, v):
        bsz, lq, lk, tq = shapes(q, kn)

        def body(q_ref, kn_ref, kr_ref, v_ref, o_ref, lse_ref):
            real = lax.broadcasted_iota(jnp.int32, (lk, lanes), 1) < dv
            outs = []
            for h in range(heads_per_step):
                kk = kn_ref[0, :, h * lanes:(h + 1) * lanes] + kr_ref[0]
                v_ones = jnp.where(real, v_ref[0, :, h * lanes:(h + 1) * lanes], jnp.ones((lk, lanes), BF16))
                _, s = scores(q_ref[0, :, h * lanes:(h + 1) * lanes], kk)
                m = jnp.max(s, axis=-1, keepdims=True)
                acc = jnp.dot(jnp.exp2(s - m).astype(BF16), v_ones, preferred_element_type=F32)
                l = acc[:, dv:dv + 1]
                outs.append((acc[:, :dv] / l).astype(BF16))
                lse_ref[0, h] = m + jnp.log2(l)
            o_ref[0] = jnp.concatenate(outs, axis=1)

        wide = heads_per_step * lanes
        return _pcall(
            body, name=name + "_fwd", grid=(bsz, heads // heads_per_step, lq // tq),
            in_specs=[pl.BlockSpec((1, tq, wide), lambda b, g, i: (b, i, g)), pl.BlockSpec((1, lk, wide), lambda b, g, i: (b, 0, g)),
                      pl.BlockSpec((1, lk, lanes), lambda b, g, i: (b, 0, 0)), pl.BlockSpec((1, lk, wide), lambda b, g, i: (b, 0, g))],
            out_specs=[pl.BlockSpec((1, tq, heads_per_step * dv), lambda b, g, i: (b, i, g)),
                       pl.BlockSpec((1, heads_per_step, tq, 1), lambda b, g, i: (b, g, i, 0))],
            out_shape=[jax.ShapeDtypeStruct((bsz, lq, heads * dv), BF16), jax.ShapeDtypeStruct((bsz, heads, lq, 1), F32)],
            compiler_params=_params(("parallel", "parallel", "parallel")),
        )(q, kn, kr, v)

    def call_bwd(q, kn, kr, v, o, lse, do):
        bsz, lq, lk, tq = shapes(q, kn)

        def body(q_ref, kn_ref, kr_ref, v_ref, o_ref, lse_ref, do_ref, dq_ref, dkn_ref, dkr_ref, dv_ref, dkn_acc, dkr_acc, dv_acc):
            g, qi = pl.program_id(1), pl.program_id(2)
            last_q = qi == lq // tq - 1

            @pl.when(qi == 0)
            def _():
                dkn_acc[...] = jnp.zeros_like(dkn_acc)
                dv_acc[...] = jnp.zeros_like(dv_acc)

            @pl.when((qi == 0) & (g == 0))
            def _():
                dkr_acc[...] = jnp.zeros_like(dkr_acc)

            dqs, dks, dvs = [], [], []
            for h in range(heads_per_step):
                kk = kn_ref[0, :, h * lanes:(h + 1) * lanes] + kr_ref[0]
                vv = v_ref[0, :, h * lanes:h * lanes + dv]
                dd = do_ref[0, :, h * dv:(h + 1) * dv]
                qs, s = scores(q_ref[0, :, h * lanes:(h + 1) * lanes], kk)
                p = jnp.exp2(s - lse_ref[0, h])
                delta = jnp.sum(dd.astype(F32) * o_ref[0, :, h * dv:(h + 1) * dv].astype(F32), axis=-1, keepdims=True)
                t = (p * (lax.dot_general(dd, vv, _NT, preferred_element_type=F32) - delta)).astype(BF16)
                dqs.append((jnp.dot(t, kk, preferred_element_type=F32) * scale).astype(BF16))
                dks.append(lax.dot_general(t, qs, _TN, preferred_element_type=F32) * LN_2)
                dv_h = lax.dot_general(p.astype(BF16), dd, _TN, preferred_element_type=F32)
                dvs.append(jnp.concatenate([dv_h, jnp.zeros((lk, lanes - dv), F32)], axis=1))
            dq_ref[0] = jnp.concatenate(dqs, axis=1)
            dkn_acc[...] += jnp.concatenate(dks, axis=1)
            dv_acc[...] += jnp.concatenate(dvs, axis=1)
            shared = dks[0]
            for d_h in dks[1:]:
                shared = shared + d_h
            dkr_acc[...] += shared

            @pl.when(last_q)
            def _():
                dkn_ref[0] = dkn_acc[...].astype(dkn_ref.dtype)
                dv_ref[0] = dv_acc[...].astype(dv_ref.dtype)

            @pl.when(last_q & (g == heads // heads_per_step - 1))
            def _():
                dkr_ref[0] = dkr_acc[...].astype(dkr_ref.dtype)

        wide = heads_per_step * lanes
        q_spec = pl.BlockSpec((1, tq, wide), lambda b, g, i: (b, i, g))
        k_spec = pl.BlockSpec((1, lk, wide), lambda b, g, i: (b, 0, g))
        kr_spec = pl.BlockSpec((1, lk, lanes), lambda b, g, i: (b, 0, 0))
        o_spec = pl.BlockSpec((1, tq, heads_per_step * dv), lambda b, g, i: (b, i, g))
        lse_spec = pl.BlockSpec((1, heads_per_step, tq, 1), lambda b, g, i: (b, g, i, 0))
        return _pcall(
            body, name=name + "_bwd", grid=(bsz, heads // heads_per_step, lq // tq),
            in_specs=[q_spec, k_spec, kr_spec, k_spec, o_spec, lse_spec, o_spec],
            out_specs=[q_spec, k_spec, kr_spec, k_spec],
            out_shape=[jax.ShapeDtypeStruct(q.shape, BF16), jax.ShapeDtypeStruct(kn.shape, BF16), jax.ShapeDtypeStruct(kr.shape, BF16),
                       jax.ShapeDtypeStruct(v.shape, BF16)],
            scratch_shapes=[pltpu.VMEM((lk, wide), F32), pltpu.VMEM((lk, lanes), F32), pltpu.VMEM((lk, wide), F32)],
            compiler_params=_params(("parallel", "arbitrary", "arbitrary")),
        )(q, kn, kr, v, o, lse, do)

    @jax.custom_vjp
    def op(q, kn, kr, v):
        return call_fwd(q, kn, kr, v)[0]

    def fwd(q, kn, kr, v):
        o, lse = call_fwd(q, kn, kr, v)
        return o, (q, kn, kr, v, o, lse)

    def bwd(res, do):
        q, kn, kr, v, o, lse = res
        dq, dkn, dkr, dv_ = call_bwd(q, kn, kr, v, o, lse, do)
        return dq, dkn.astype(kn.dtype), dkr.astype(kr.dtype), dv_.astype(v.dtype)

    op.defvjp(fwd, bwd)
    return op


def _conv_call(ypad, taps, name):
    bsz, lp, ch = ypad.shape
    length = lp - 2 * ROW_ALIGN
    tl = _pick(length, (256, 128))

    def body(y_ref, w_ref, o_ref):
        base = pl.multiple_of(pl.program_id(1) * tl, tl)
        win = y_ref[0, pl.ds(base, tl + 2 * ROW_ALIGN), :]
        acc = jnp.broadcast_to(w_ref[pl.ds(D_CONV, 1), :], (tl, ch))
        for k in range(D_CONV):
            acc = acc + win[k:k + tl, :] * w_ref[pl.ds(k, 1), :]
        o_ref[0] = acc

    return _pcall(
        body, name=name, grid=(bsz, length // tl),
        in_specs=[pl.BlockSpec((1, lp, ch), lambda b, l: (b, 0, 0)), pl.BlockSpec((D_CONV + 1, ch), lambda b, l: (0, 0))],
        out_specs=pl.BlockSpec((1, tl, ch), lambda b, l: (b, l, 0)),
        out_shape=jax.ShapeDtypeStruct((bsz, length, ch), F32),
        compiler_params=_params(("parallel", "parallel")),
    )(ypad, taps)


def _conv_dw_call(ypad, dout, name):
    bsz, lp, ch = ypad.shape
    length = lp - 2 * ROW_ALIGN
    tl = _pick(length, (256, 128))

    def body(y_ref, d_ref, o_ref):
        b, l = pl.program_id(0), pl.program_id(1)

        @pl.when((b == 0) & (l == 0))
        def _():
            o_ref[...] = jnp.zeros_like(o_ref)

        base = pl.multiple_of(l * tl, tl)
        win = y_ref[0, pl.ds(base, tl + 2 * ROW_ALIGN), :]
        dd = d_ref[0]
        for k in range(D_CONV):
            o_ref[pl.ds(k, 1), :] += jnp.sum(win[k:k + tl, :] * dd, axis=0, keepdims=True)
        o_ref[pl.ds(D_CONV, 1), :] += jnp.sum(dd, axis=0, keepdims=True)

    return _pcall(
        body, name=name, grid=(bsz, length // tl),
        in_specs=[pl.BlockSpec((1, lp, ch), lambda b, l: (b, 0, 0)), pl.BlockSpec((1, tl, ch), lambda b, l: (b, l, 0))],
        out_specs=pl.BlockSpec((D_CONV + 1, ch), lambda b, l: (0, 0)),
        out_shape=jax.ShapeDtypeStruct((D_CONV + 1, ch), F32),
        compiler_params=_params(("arbitrary", "arbitrary")),
    )(ypad, dout)


def _pad_rows(y):
    return jnp.pad(y, ((0, 0), (CONV_PAD, 2 * ROW_ALIGN - CONV_PAD), (0, 0)))


@jax.custom_vjp
def _dwconv(y, taps):
    return _conv_call(_pad_rows(y), taps, "conv_fwd")


def _dwconv_fwd(y, taps):
    return _dwconv(y, taps), (y, taps)


def _dwconv_bwd(res, dout):
    y, taps = res
    flipped = jnp.concatenate([taps[:D_CONV][::-1], jnp.zeros_like(taps[D_CONV:])], axis=0)
    dy = _conv_call(_pad_rows(dout), flipped, "conv_dy")
    dtaps = _conv_dw_call(_pad_rows(y), dout, "conv_dw")
    return dy, dtaps


_dwconv.defvjp(_dwconv_fwd, _dwconv_bwd)


def _loss_head(x, target, g):
    bsz, length, d = x.shape
    tl = _pick(length, (256, 128))

    def f(xb, tb, gb):
        err = _rms(xb, gb) - tb
        return 0.5 * jnp.sum(jnp.sum(err * err, axis=-1, keepdims=True), axis=0, keepdims=True) / d

    def body(x_ref, t_ref, g_ref, loss_ref, dx_ref, dg_ref):
        val, vjp = jax.vjp(lambda xb, gb: f(xb, t_ref[0], gb), x_ref[0], g_ref[...])
        dx, dg = vjp(jnp.ones((1, 1), F32))
        dx_ref[0] = dx
        first = (pl.program_id(0) == 0) & (pl.program_id(1) == 0)

        @pl.when(first)
        def _():
            loss_ref[...] = val
            dg_ref[...] = dg

        @pl.when(jnp.logical_not(first))
        def _():
            loss_ref[...] += val
            dg_ref[...] += dg

    row = pl.BlockSpec((1, tl, d), lambda b, l: (b, l, 0))
    return _pcall(
        body, name="loss_head", grid=(bsz, length // tl),
        in_specs=[row, row, pl.BlockSpec((1, d), lambda b, l: (0, 0))],
        out_specs=[pl.BlockSpec((1, 1), lambda b, l: (0, 0)), row, pl.BlockSpec((1, d), lambda b, l: (0, 0))],
        out_shape=[jax.ShapeDtypeStruct((1, 1), F32), jax.ShapeDtypeStruct(x.shape, F32), jax.ShapeDtypeStruct((1, d), F32)],
        compiler_params=_params(("arbitrary", "arbitrary")),
    )(x, target, g)


def _adamw(w, g, m, v, name):
    shape = w.shape
    cols = shape[-1]
    rows = w.size // cols
    tr = _pick(rows, (512, 256, 128))
    w2, g2, m2, v2 = (t.reshape(rows, cols) for t in (w, g, m, v))

    def body(w_ref, g_ref, m_ref, v_ref, d_ref, nm_ref, nv_ref):
        gg = g_ref[...]
        nm = ADAM_B1 * m_ref[...] + (1.0 - ADAM_B1) * gg
        nv = ADAM_B2 * v_ref[...] + (1.0 - ADAM_B2) * jnp.square(gg)
        m_hat = nm / (1.0 - ADAM_B1 ** ADAM_STEP)
        v_hat = nv / (1.0 - ADAM_B2 ** ADAM_STEP)
        d_ref[...] = -ADAM_LR * (m_hat / (jnp.sqrt(v_hat) + ADAM_EPS) + ADAM_WD * w_ref[...])
        nm_ref[...] = nm
        nv_ref[...] = nv

    spec = pl.BlockSpec((tr, cols), lambda i: (i, 0))
    outs = _pcall(
        body, name=name, grid=(rows // tr,), in_specs=[spec] * 4, out_specs=[spec] * 3,
        out_shape=[jax.ShapeDtypeStruct((rows, cols), F32)] * 3,
        compiler_params=_params(("parallel",)),
    )(w2, g2, m2, v2)
    return tuple(o.reshape(shape) for o in outs)


def _mesh_pos():
    return lax.axis_index("x"), lax.axis_index("y"), lax.axis_index("c")


_RELATIONS = [(dx, dy, dc) for dx in (0, 1) for dy in (0, 1) for dc in (0, 1)][1:]


def _peer(pos, rel):
    return tuple(jnp.where(r == 1, 1 - p, p) if r else p for p, r in zip(pos, rel))


def _block_index(pos):
    return 4 * pos[0] + 2 * pos[1] + pos[2]


_HBM = pl.BlockSpec(memory_space=pltpu.HBM)


def _all_gather(xs, name):
    n = len(xs)

    def body(*refs):
        x_refs, out_refs, (send_sems, recv_sems, local_sems) = refs[:n], refs[n:2 * n], refs[2 * n:]
        x_, y_, c_ = _mesh_pos()
        me, sibling = (x_, y_, c_), (x_, y_, 1 - c_)
        chips = [(1 - x_, y_), (x_, 1 - y_), (1 - x_, 1 - y_)]

        def copy(t, k, block, to, own=False):
            slot = out_refs[t].at[_block_index(block)]
            return pltpu.make_async_remote_copy(
                src_ref=x_refs[t] if own else slot, dst_ref=slot, send_sem=send_sems.at[7 * t + k], recv_sem=recv_sems.at[7 * t + k],
                device_id=to, device_id_type=pl.DeviceIdType.MESH)

        mine = [pltpu.make_async_copy(x_refs[t], out_refs[t].at[_block_index(me)], local_sems.at[t]) for t in range(n)]
        first = [[copy(t, 0, me, sibling, own=True)] + [copy(t, 1 + j, me, (*chip, c_), own=True) for j, chip in enumerate(chips)]
                 for t in range(n)]
        passed = [[copy(t, 4 + j, (*chip, c_), sibling) for j, chip in enumerate(chips)] for t in range(n)]
        for t in range(n):
            mine[t].start()
            for cp in first[t]:
                cp.start()
        for t in range(n):
            for j, chip in enumerate(chips):
                copy(t, 1 + j, (*chip, c_), me).wait_recv()
                passed[t][j].start()
        for t in range(n):
            copy(t, 0, sibling, me).wait_recv()
            for j, chip in enumerate(chips):
                copy(t, 4 + j, (*chip, 1 - c_), me).wait_recv()
            for cp in first[t] + passed[t]:
                cp.wait_send()
            mine[t].wait()

    return _pcall(
        body, name=name, in_specs=[_HBM] * n, out_specs=[_HBM] * n,
        out_shape=[jax.ShapeDtypeStruct((N_DEV,) + x.shape, x.dtype) for x in xs],
        scratch_shapes=[pltpu.SemaphoreType.DMA((7 * n,)), pltpu.SemaphoreType.DMA((7 * n,)), pltpu.SemaphoreType.DMA((n,))],
    )(*xs)


def _exchange_blocks(ps, name):
    n = len(ps)

    def body(*refs):
        p_refs, out_refs, (send_sems, recv_sems, local_sems) = refs[:n], refs[n:2 * n], refs[2 * n:]
        me = _mesh_pos()

        def copy(t, k, rel, arriving=False):
            peer = _peer(me, rel)
            return pltpu.make_async_remote_copy(
                src_ref=p_refs[t].at[_block_index(peer)], dst_ref=out_refs[t].at[_block_index(peer if arriving else me)],
                send_sem=send_sems.at[7 * t + k], recv_sem=recv_sems.at[7 * t + k], device_id=peer, device_id_type=pl.DeviceIdType.MESH)

        mine = [pltpu.make_async_copy(p_refs[t].at[_block_index(me)], out_refs[t].at[_block_index(me)], local_sems.at[t]) for t in range(n)]
        sends = [copy(t, k, rel) for t in range(n) for k, rel in enumerate(_RELATIONS)]
        for cp in mine + sends:
            cp.start()
        for t in range(n):
            for k, rel in enumerate(_RELATIONS):
                copy(t, k, rel, arriving=True).wait_recv()
        for cp in sends:
            cp.wait_send()
        for cp in mine:
            cp.wait()

    return _pcall(
        body, name=name, in_specs=[_HBM] * n, out_specs=[_HBM] * n,
        out_shape=[jax.ShapeDtypeStruct(p.shape, p.dtype) for p in ps],
        scratch_shapes=[pltpu.SemaphoreType.DMA((7 * n,)), pltpu.SemaphoreType.DMA((7 * n,)), pltpu.SemaphoreType.DMA((n,))],
    )(*ps)


_SEM = pl.BlockSpec(memory_space=pltpu.SEMAPHORE)
_EFFECT = pltpu.SideEffectType.DATAFLOW_SIDE_EFFECTING


def _push_start(srcs, after, name):
    n = len(srcs)
    lands = [lax.empty((N_DEV,) + s.shape[-2:], s.dtype) for s in srcs]

    def body(*refs):
        src_refs, land_refs = refs[:n], refs[n:2 * n]
        send_sems, recv_sems, token = refs[2 * n + 1:3 * n + 1], refs[3 * n + 1:4 * n + 1], refs[-1]
        me = _mesh_pos()
        for t in range(n):
            for rel in _RELATIONS:
                peer = _peer(me, rel)
                pltpu.make_async_remote_copy(
                    src_ref=src_refs[t].at[_block_index(peer)] if srcs[t].ndim == 3 else src_refs[t], dst_ref=land_refs[t].at[_block_index(me)],
                    send_sem=send_sems[t], recv_sem=recv_sems[t], device_id=peer, device_id_type=pl.DeviceIdType.MESH).start()
        token[...] = jnp.zeros_like(token)

    outs = _pcall(
        body, name=name,
        out_shape=[pltpu.SemaphoreType.DMA(())] * (2 * n) + [pltpu.HBM(s.shape, s.dtype) for s in srcs]
        + [pltpu.HBM(l.shape, l.dtype) for l in lands] + [jax.ShapeDtypeStruct((8, 128), F32)],
        in_specs=[_HBM] * (2 * n) + [pl.BlockSpec(memory_space=pl.ANY)],
        out_specs=[_SEM] * (2 * n) + [_HBM] * (2 * n) + [pl.BlockSpec(memory_space=pltpu.VMEM)],
        input_output_aliases={i: 2 * n + i for i in range(2 * n)}, compiler_params=pltpu.CompilerParams(has_side_effects=_EFFECT),
    )(*[pltpu.with_memory_space_constraint(t, pltpu.HBM) for t in list(srcs) + lands], after)
    return (outs[:n], outs[n:2 * n], outs[2 * n:3 * n], outs[3 * n:4 * n]), outs[-1]


def _push_wait(handle, after, owns, me, name):
    send_sems, recv_sems, src_thrus, land_thrus = handle
    n = len(land_thrus)

    def body(*refs):
        land_refs, sends, recvs = refs[n:2 * n], refs[2 * n:3 * n], refs[3 * n:4 * n]
        for t in range(n):
            seven = land_refs[t].at[pl.ds(0, N_DEV - 1)]
            all_seven = pltpu.make_async_remote_copy(src_ref=seven, dst_ref=seven, send_sem=sends[t], recv_sem=recvs[t],
                                                     device_id=_mesh_pos(), device_id_type=pl.DeviceIdType.MESH)
            all_seven.wait_send()
            all_seven.wait_recv()

    outs = _pcall(
        body, name=name,
        out_shape=[pltpu.HBM(t.shape, t.dtype) for t in list(src_thrus) + list(land_thrus)],
        in_specs=[_HBM] * (2 * n) + [_SEM] * (2 * n) + [pl.BlockSpec(memory_space=pl.ANY)], out_specs=[_HBM] * (2 * n),
        input_output_aliases={i: i for i in range(2 * n)}, compiler_params=pltpu.CompilerParams(has_side_effects=_EFFECT),
    )(*src_thrus, *land_thrus, *send_sems, *recv_sems, after)
    return [lax.dynamic_update_slice(land, own[None], (me, 0, 0)) for land, own in zip(outs[n:], owns)]


def _sum_blocks(p, name):
    n, rows, cols = p.shape
    tr = _pick(rows, (256, 128, 64, PACK_ROWS, 16, 8))

    def body(p_ref, o_ref):
        acc = p_ref[0].astype(F32)
        for s in range(1, n):
            acc = acc + p_ref[s].astype(F32)
        o_ref[...] = acc

    return _pcall(
        body, name=name, grid=(rows // tr,),
        in_specs=[pl.BlockSpec((n, tr, cols), lambda i: (0, i, 0))], out_specs=pl.BlockSpec((tr, cols), lambda i: (i, 0)),
        out_shape=jax.ShapeDtypeStruct((rows, cols), F32), compiler_params=_params(("parallel",)),
    )(p)


def _sum_rows(t, name):
    def body(t_ref, o_ref):
        o_ref[...] = jnp.sum(t_ref[...], axis=0, keepdims=True)

    return _pcall(body, name=name, out_shape=jax.ShapeDtypeStruct((1, t.shape[1]), F32))(t)


class _Packing:
    def __init__(self, sizes, align, total_align=None):
        self.offsets, self.sizes, self.align = {}, dict(sizes), align
        row = 0
        for name, size in sizes:
            self.offsets[name] = row
            row += -(-size // (align * PACK_COLS)) * align
        total_align = total_align or align
        self.rows = -(-row // total_align) * total_align
        self.tail = self.rows - row

    def pack(self, pieces):
        return self.pack_blocks({n: pieces[n].reshape(1, -1) for n in self.sizes})[0]

    def pack_blocks(self, pieces):
        out = []
        for n, size in self.sizes.items():
            padded = -(-size // (self.align * PACK_COLS)) * self.align * PACK_COLS
            out.append(jnp.pad(pieces[n], ((0, 0), (0, padded - size))).reshape(pieces[n].shape[0], -1, PACK_COLS))
        if self.tail:
            out.append(jnp.zeros((out[0].shape[0], self.tail, PACK_COLS), out[0].dtype))
        return jnp.concatenate(out, axis=1)

    def piece(self, packed, name, lead=()):
        start, size = self.offsets[name], self.sizes[name]
        nrow = -(-size // (self.align * PACK_COLS)) * self.align
        sl = packed[..., start:start + nrow, :]
        return sl.reshape(lead + (nrow * PACK_COLS,))[..., :size]


_PIECES = (("ev_w_in", "ev_w_in", 0, 1, "a"), ("w_out0", "w_out", 0, 0, "a"),
           ("mlp_w1_0", "mlp_w1", 0, 1, "b"), ("mlp_w2_0", "mlp_w2", 0, 0, "b"),
           ("od_w_in", "od_w_in", 0, 1, "c"), ("od_w_uq", "od_w_uq", 0, 1, "c"), ("od_w_ukv", "od_w_ukv", 0, 1, "c"),
           ("w_out1", "w_out", 1, 0, "c"), ("mlp_w1_1", "mlp_w1", 1, 1, "c"), ("mlp_w2_1", "mlp_w2", 1, 0, "c"))
_SMALL_SHARDED = (("od_q_norm_g", 1), ("od_conv_w", 2), ("od_conv_b", 1), ("od_ln_g", 1), ("od_ln_b", 1))
_REPLICATED = ("c_ctx", "norm1_g", "norm2_g", "ev_q_norm_g", "ev_k_norm_g", "ev_sgu_norm_g", "ev_sgu_w", "ev_sgu_b",
               "od_kv_norm_g", "final_g")


def _unshard(blocks, axis):
    moved = jnp.moveaxis(blocks, 0, axis)
    shape = moved.shape
    return moved.reshape(shape[:axis] + (shape[axis] * shape[axis + 1],) + shape[axis + 2:])


def _shard_blocks(full, axis):
    shape = full.shape
    split = full.reshape(shape[:axis] + (N_DEV, shape[axis] // N_DEV) + shape[axis + 1:])
    return jnp.moveaxis(split, axis, 0)


def _group_mean_matrix(width, group):
    idx = jnp.arange(width) // group
    return (idx[:, None] == idx[None, :]).astype(F32) / group


def _swap_matrix(width):
    idx = jnp.arange(width)
    return ((idx[:, None] ^ 1) == idx[None, :]).astype(F32)


def _angles(length, d_rot):
    rows = length // GRID_W
    row = jnp.broadcast_to(jnp.arange(rows)[:, None], (rows, GRID_W)).reshape(-1).astype(F32)
    col = jnp.broadcast_to(jnp.arange(GRID_W)[None, :], (rows, GRID_W)).reshape(-1).astype(F32)
    d_axis = d_rot // 2
    inv = ROPE_THETA ** (-jnp.arange(0, d_axis, 2, dtype=F32) / d_axis)
    return jnp.concatenate([row[:, None] * inv, col[:, None] * inv], axis=-1)


def _rope_tables(length, d_rot, head_dim, heads, ctx_len, tail=0):
    ang = _angles(length, d_rot)
    cos = jnp.repeat(jnp.cos(ang), 2, axis=1)
    sin = jnp.repeat(jnp.sin(ang), 2, axis=1) * jnp.tile(jnp.array([-1.0, 1.0], F32), d_rot // 2)
    keep = head_dim - d_rot - tail
    cos = jnp.concatenate([jnp.ones((length, keep), F32), cos, jnp.ones((length, tail), F32)], axis=1)
    sin = jnp.concatenate([jnp.zeros((length, keep), F32), sin, jnp.zeros((length, tail), F32)], axis=1)
    cos, sin = jnp.tile(cos, (1, heads)), jnp.tile(sin, (1, heads))
    cos = jnp.concatenate([jnp.ones((ctx_len, cos.shape[1]), F32), cos], axis=0)
    sin = jnp.concatenate([jnp.zeros((ctx_len, sin.shape[1]), F32), sin], axis=0)
    return cos, sin


def _to_heads(t, heads):
    b, l, w = t.shape
    return t.reshape(b, l, heads, w // heads).transpose(0, 2, 1, 3)


def _from_heads(t):
    b, h, l, d = t.shape
    return t.transpose(0, 2, 1, 3).reshape(b, l, h * d)


def _segment_params(mod, bsz):
    parts = jnp.split(mod, N_MOD, axis=-1)
    out = []
    for part in parts:
        lat = part[:bsz]
        ctx = jnp.broadcast_to(part[bsz:bsz + 1], lat.shape)
        out.append(jnp.stack([ctx, lat], axis=1)[:, :, None, :])
    return out


def _flat(t):
    return t.reshape(-1, t.shape[-1])


def _sequence_rowwise(ctx_len):
    tl = math.gcd(256, ctx_len)

    def make(name, f, out_specs, rows_per_block=tl, ctx_blocks=ctx_len // tl):
        return _rowwise(name, f, out_specs, rows_per_block, ctx_blocks)

    return make


def _mixer0(xall, modrows0, w, bsz, length, ctx_len):
    d = xall.shape[-1]
    total = ctx_len + length
    rowwise, flat = _sequence_rowwise(ctx_len), _flat
    sh1, sc1, g1, sh2, sc2, _ = _segment_params(modrows0, bsz)
    (h,) = rowwise("mod0", _f_modulate, [(d, BF16)])((xall,), (), (sh1, sc1), (w["norm1_g0"],), ())
    ev_q, ev_kv = A_Q_HEADS * A_HEAD_DIM, A_KV_HEADS * A_HEAD_DIM
    half = B_GROUPS * B_GROUP_DIM
    groups = tuple(jnp.split(w["ev_w_in"], [ev_q, ev_q + ev_kv, ev_q + 2 * ev_kv, ev_q + 2 * ev_kv + half], axis=0))
    qp, kp, vp, zu, zv = [t.reshape(bsz, total, -1) for t in _linear_multi("ev_in", (F32, F32, BF16, F32, F32))(flat(h), groups)]
    cos_q, sin_q = _rope_tables(length, A_HEAD_DIM, A_HEAD_DIM, A_Q_HEADS, ctx_len)
    cos_k, sin_k = cos_q[:, :ev_kv], sin_q[:, :ev_kv]
    (q,) = rowwise("ev_q", _f_headnorm_rope, [(ev_q, BF16)])(
        (qp,), (cos_q, sin_q), (), (jnp.tile(w["ev_q_norm_g"][0], A_Q_HEADS)[None],), (_group_mean_matrix(ev_q, A_HEAD_DIM),))
    (k,) = rowwise("ev_k", _f_headnorm_rope, [(ev_kv, BF16)])(
        (kp,), (cos_k, sin_k), (), (jnp.tile(w["ev_k_norm_g"][0], A_KV_HEADS)[None],), (_group_mean_matrix(ev_kv, A_HEAD_DIM),))
    o_att = _attention_rows("gqa", ctx_len, A_Q_HEADS, A_KV_HEADS)(q, k, vp)
    u, vn = rowwise("sgu_pre", _f_sgu_pre, [(half, F32), (half, BF16)])(
        (zu, zv), (), (), (w["ev_sgu_norm_g"][0].reshape(1, half),), (_group_mean_matrix(half, B_GROUP_DIM),))
    bias = jnp.repeat(w["ev_sgu_b"][0].T, B_GROUP_DIM, axis=1)
    (o_sgu,) = rowwise("sgu_mix", _f_sgu_mix, [(half, BF16)], rows_per_block=B_CHUNK, ctx_blocks=0)(
        (u, vn), (), (), tuple(w["ev_sgu_w"][0][g] for g in range(B_GROUPS)) + (bias,), ())
    y = _linear_sum("out0")((flat(o_att), flat(o_sgu)), tuple(jnp.split(w["w_out0"], 2, axis=0))).reshape(bsz, total, d)
    x1, h = rowwise("res_mod0a", _f_res_modulate, [(d, F32), (d, BF16)])((xall, y), (), (g1, sh2, sc2), (w["norm2_g0"],), ())
    return x1, h


def _mlp0(x1, h, modrows0, modrows1, w, bsz, length, ctx_len):
    d = x1.shape[-1]
    total = ctx_len + length
    g2 = _segment_params(modrows0, bsz)[5]
    sh1, sc1 = _segment_params(modrows1, bsz)[:2]
    y = _mlp("mlp0")(_flat(h), w["mlp_w1_0"], w["mlp_w2_0"]).reshape(bsz, total, d)
    return _sequence_rowwise(ctx_len)("res_mod0b", _f_res_modulate, [(d, F32), (d, BF16)])((x1, y), (), (g2, sh1, sc1), (w["norm1_g1"],), ())


def _layer1(x2, h, modrows1, w, bsz, length, ctx_len):
    d = x2.shape[-1]
    total = ctx_len + length
    half = B_GROUPS * B_GROUP_DIM
    rowwise, flat = _sequence_rowwise(ctx_len), _flat
    _, _, g1n, sh2n, sc2n, g2n = _segment_params(modrows1, bsz)
    g_cq, g_ckv, g_kr, g_za, g_zg = jnp.split(w["od_w_in"], [C_Q_RANK, C_Q_RANK + C_KV_RANK, C_Q_RANK + C_KV_RANK + C_ROPE,
                                                             C_Q_RANK + C_KV_RANK + C_ROPE + half], axis=0)
    lanes, c_qk = 128, C_NOPE + C_ROPE
    g_kr = jnp.pad(g_kr, ((C_NOPE, lanes - c_qk), (0, 0)))
    cq, ckv, kr, za, zg = [t.reshape(bsz, total, -1) for t in _linear_multi("od_in", (F32,) * 5)(flat(h), (g_cq, g_ckv, g_kr, g_za, g_zg))]
    lat = slice(ctx_len, total)
    lat_tl = math.gcd(256, length)
    (cqn,) = _rowwise("od_qn", _f_rms, [(C_Q_RANK, BF16)], lat_tl)((cq[:, lat],), (), (), (w["od_q_norm_g"],), ())
    w_uq = jnp.pad(w["od_w_uq"].reshape(C_HEADS, c_qk, C_Q_RANK), ((0, 0), (0, lanes - c_qk), (0, 0))).reshape(C_HEADS * lanes, C_Q_RANK)
    qf = _linear("od_uq", transposed=True)(flat(cqn), w_uq).reshape(bsz, length, C_HEADS * lanes)
    cos_q, sin_q = _rope_tables(length, C_ROPE, lanes, C_HEADS, 0, tail=lanes - c_qk)
    (q,) = _rowwise("od_qrope", _f_rope, [(C_HEADS * lanes, BF16)], lat_tl)((qf,), (cos_q, sin_q), (), (), ())
    (ckvn,) = rowwise("od_kvn", _f_rms, [(C_KV_RANK, BF16)])((ckv,), (), (), (w["od_kv_norm_g"],), ())
    per_head = w["od_w_ukv"].reshape(C_HEADS, C_NOPE + C_V, C_KV_RANK)
    w_kn = jnp.pad(per_head[:, :C_NOPE], ((0, 0), (0, lanes - C_NOPE), (0, 0))).reshape(C_HEADS * lanes, C_KV_RANK)
    w_v = jnp.pad(per_head[:, C_NOPE:], ((0, 0), (0, lanes - C_V), (0, 0))).reshape(C_HEADS * lanes, C_KV_RANK)
    kn, vv = [t.reshape(bsz, total, -1) for t in _linear_multi("od_ukv", (BF16, BF16))(flat(ckvn), (w_kn, w_v))]
    cos_r, sin_r = _rope_tables(length, C_ROPE, lanes, 1, ctx_len, tail=lanes - c_qk)
    (krr,) = rowwise("od_krope", _f_rope, [(lanes, BF16)])((kr,), (cos_r, sin_r), (), (), ())
    o_att = _attention_latent_rows("mla", C_HEADS, C_NOPE, C_ROPE, C_V, 2)(q, kn, krr, vv)
    (glu,) = _rowwise("glu", _f_glu, [(half, F32)], lat_tl)((za[:, lat], zg[:, lat]), (), (), (), ())
    taps = jnp.concatenate([w["od_conv_w"][0], w["od_conv_b"]], axis=0)
    conv = _dwconv(glu, taps)
    (o_conv,) = _rowwise("ln_silu", _f_ln_silu, [(half, BF16)], lat_tl)((conv,), (), (), (w["od_ln_g"], w["od_ln_b"]), ())
    y = _linear_sum("out1")((flat(o_att), flat(o_conv)), tuple(jnp.split(w["w_out1"], 2, axis=0))).reshape(bsz, length, d)
    lat_param = lambda p: p[:, 1:]
    x3, h = _rowwise("res_mod1a", _f_res_modulate, [(d, F32), (d, BF16)], lat_tl)(
        (x2[:, lat], y), (), (lat_param(g1n), lat_param(sh2n), lat_param(sc2n)), (w["norm2_g1"],), ())
    y = _mlp("mlp1")(flat(h), w["mlp_w1_1"], w["mlp_w2_1"]).reshape(bsz, length, d)
    (x4,) = _rowwise("res1b", _f_res, [(d, F32)], lat_tl)((x3, y), (), (lat_param(g2n),), (), ())
    return x4


def kernel(x, c, ctx, c_ctx, ada_w, ada_b, norm1_g, norm2_g, w_out, mlp_w1, mlp_w2, ev_w_in, ev_q_norm_g, ev_k_norm_g, ev_sgu_norm_g, ev_sgu_w, ev_sgu_b, od_w_in, od_q_norm_g, od_kv_norm_g, od_w_uq, od_w_ukv, od_conv_w, od_conv_b, od_ln_g, od_ln_b, final_g, loss_target, m_c_ctx, m_ada_w, m_ada_b, m_norm1_g, m_norm2_g, m_w_out, m_mlp_w1, m_mlp_w2, m_ev_w_in, m_ev_q_norm_g, m_ev_k_norm_g, m_ev_sgu_norm_g, m_ev_sgu_w, m_ev_sgu_b, m_od_w_in, m_od_q_norm_g, m_od_kv_norm_g, m_od_w_uq, m_od_w_ukv, m_od_conv_w, m_od_conv_b, m_od_ln_g, m_od_ln_b, m_final_g, v_c_ctx, v_ada_w, v_ada_b, v_norm1_g, v_norm2_g, v_w_out, v_mlp_w1, v_mlp_w2, v_ev_w_in, v_ev_q_norm_g, v_ev_k_norm_g, v_ev_sgu_norm_g, v_ev_sgu_w, v_ev_sgu_b, v_od_w_in, v_od_q_norm_g, v_od_kv_norm_g, v_od_w_uq, v_od_w_ukv, v_od_conv_w, v_od_conv_b, v_od_ln_g, v_od_ln_b, v_final_g):
    names = ["c_ctx", "ada_w", "ada_b", "norm1_g", "norm2_g", "w_out", "mlp_w1", "mlp_w2", "ev_w_in", "ev_q_norm_g", "ev_k_norm_g",
             "ev_sgu_norm_g", "ev_sgu_w", "ev_sgu_b", "od_w_in", "od_q_norm_g", "od_kv_norm_g", "od_w_uq", "od_w_ukv", "od_conv_w",
             "od_conv_b", "od_ln_g", "od_ln_b", "final_g"]
    local = dict(zip(names, [c_ctx, ada_w, ada_b, norm1_g, norm2_g, w_out, mlp_w1, mlp_w2, ev_w_in, ev_q_norm_g, ev_k_norm_g, ev_sgu_norm_g, ev_sgu_w, ev_sgu_b, od_w_in, od_q_norm_g, od_kv_norm_g, od_w_uq, od_w_ukv, od_conv_w, od_conv_b, od_ln_g, od_ln_b, final_g]))
    mom1 = dict(zip(names, [m_c_ctx, m_ada_w, m_ada_b, m_norm1_g, m_norm2_g, m_w_out, m_mlp_w1, m_mlp_w2, m_ev_w_in, m_ev_q_norm_g, m_ev_k_norm_g, m_ev_sgu_norm_g, m_ev_sgu_w, m_ev_sgu_b, m_od_w_in, m_od_q_norm_g, m_od_kv_norm_g, m_od_w_uq, m_od_w_ukv, m_od_conv_w, m_od_conv_b, m_od_ln_g, m_od_ln_b, m_final_g]))
    mom2 = dict(zip(names, [v_c_ctx, v_ada_w, v_ada_b, v_norm1_g, v_norm2_g, v_w_out, v_mlp_w1, v_mlp_w2, v_ev_w_in, v_ev_q_norm_g, v_ev_k_norm_g, v_ev_sgu_norm_g, v_ev_sgu_w, v_ev_sgu_b, v_od_w_in, v_od_q_norm_g, v_od_kv_norm_g, v_od_w_uq, v_od_w_ukv, v_od_conv_w, v_od_conv_b, v_od_ln_g, v_od_ln_b, v_final_g]))
    bsz, length, d = x.shape
    ctx_len = ctx.shape[1]
    me = _block_index(_mesh_pos())

    shard = {p: local[wn][layer] for p, wn, layer, _, _ in _PIECES}
    by_columns = {p: axis == 1 for p, _, _, axis, _ in _PIECES}
    stages = {s: [p for p, _, _, _, st in _PIECES if st == s] for s in "abc"}
    block_rows = {p: shard[p].shape[1] if by_columns[p] else shard[p].shape[0] for p in shard}

    def pad_block_rows(t, p):
        extra = -block_rows[p] % ROW_ALIGN
        return jnp.pad(t, [(0, 0)] * (t.ndim - 2) + [(0, extra), (0, 0)]) if extra else t

    def travelling(p):
        t = shard[p].astype(BF16)
        return pad_block_rows(t.T if by_columns[p] else t, p)

    mine = {s: [travelling(p) for p in stages[s]] for s in "abc"}
    tiny_pack = _Packing([(n, local[n].size) for n, _ in _SMALL_SHARDED], 8)

    def unpack_weights(s, gathered):
        return {p: g[:, :block_rows[p]].reshape(N_DEV * block_rows[p], g.shape[2]) for p, g in zip(stages[s], gathered)}

    def pack_grads(s, g):
        return [pad_block_rows(g[p].reshape(N_DEV, block_rows[p], g[p].shape[1]), p) for p in stages[s]]

    cond_local = jnp.concatenate([c, c_ctx[None], jnp.zeros((COND_ROWS - bsz - 1, d), F32)], axis=0)
    cond, gathered_tiny, *gathered_a = _all_gather(
        [cond_local, tiny_pack.pack({n: local[n] for n, _ in _SMALL_SHARDED})] + mine["a"], "gather_inputs")
    cond = cond.reshape(N_DEV * COND_ROWS, d)
    silu_op = _rowwise("silu", _f_silu, [(d, F32)], N_DEV * COND_ROWS)
    silu_rows, silu_pullback = jax.vjp(lambda r: silu_op((r[None],), (), (), (), ())[0][0], cond)
    mod_cols = ada_w.shape[2]
    mod_part = jnp.concatenate([_mm(silu_rows, ada_w[i], "nn", (F32,), f"ada{i}_fwd")[0] for i in range(2)], axis=0)
    (mod_all,) = _all_gather([mod_part], "gather_mod")
    mod_all = mod_all.reshape(N_DEV, 2, N_DEV * COND_ROWS, mod_cols)
    modrows = []
    for i in range(2):
        whole = mod_all[:, i].transpose(1, 0, 2).reshape(N_DEV * COND_ROWS, N_DEV * mod_cols) + ada_b[i]
        modrows.append(lax.dynamic_slice_in_dim(whole, me * COND_ROWS, COND_ROWS, axis=0)[:bsz + 1])

    weights_a = unpack_weights("a", gathered_a)
    gather_b, token_b = _push_start(mine["b"], mod_all, "gather_weights_b_start")
    gather_c, token_c = _push_start(mine["c"], token_b, "gather_weights_c_start")
    full = {n: local[n] for n in _REPLICATED}
    for n, axis in _SMALL_SHARDED:
        full[n] = _unshard(tiny_pack.piece(gathered_tiny, n, (N_DEV,)).reshape((N_DEV,) + local[n].shape), axis)

    xall = jnp.concatenate([ctx, x], axis=1)
    modrows0, modrows1 = modrows[0] + token_c[0, 0], modrows[1]
    w_a = dict(weights_a, norm1_g0=norm1_g[0][None], norm2_g0=norm2_g[0][None],
               **{n: full[n] for n in ("ev_q_norm_g", "ev_k_norm_g", "ev_sgu_norm_g", "ev_sgu_w", "ev_sgu_b")})
    (x1, h0), pull_a = jax.vjp(lambda x_, m0, w: _mixer0(x_, m0, w, bsz, length, ctx_len), xall, modrows0, w_a)
    w_b = dict(unpack_weights("b", _push_wait(gather_b, x1, mine["b"], me, "gather_weights_b_wait")), norm1_g1=norm1_g[1][None])
    (x2, h1), pull_b = jax.vjp(lambda x_, h_, m0, m1, w: _mlp0(x_, h_, m0, m1, w, bsz, length, ctx_len), x1, h0, modrows0, modrows1, w_b)
    w_c = dict(unpack_weights("c", _push_wait(gather_c, x2, mine["c"], me, "gather_weights_c_wait")), norm2_g1=norm2_g[1][None],
               **{n: full[n] for n in ("od_q_norm_g", "od_kv_norm_g", "od_conv_w", "od_conv_b", "od_ln_g", "od_ln_b")})
    x4, pull_c = jax.vjp(lambda x_, h_, m1, w: _layer1(x_, h_, m1, w, bsz, length, ctx_len), x2, h1, modrows1, w_c)
    loss_part, dx4, dfinal = _loss_head(x4, loss_target, final_g[None])
    loss = lax.psum(loss_part[0, 0], MESH_AXES)

    dx2, dh1, dmod1_c, g_c = pull_c(dx4)
    grads_c = pack_grads("c", g_c)
    exchange_c, token = _push_start(grads_c, dx2, "exchange_grads_c_start")
    dx1, dh0, dmod0_b, dmod1_b, g_b = pull_b((dx2, dh1 + token[0, 0].astype(dh1.dtype)))
    grads_b = pack_grads("b", g_b)
    exchange_b, token = _push_start(grads_b, dx1, "exchange_grads_b_start")
    dxall, dmod0_a, g_a = pull_a((dx1, dh0 + token[0, 0].astype(dh0.dtype)))
    grad_x = dxall[:, ctx_len:]
    dmodrows = [dmod0_a + dmod0_b, dmod1_b + dmod1_c]
    grads = {n: g[n] for g in (g_a, g_c) for n in g if n in full}
    grads["norm1_g"] = jnp.concatenate([g_a["norm1_g0"], g_b["norm1_g1"]], axis=0)
    grads["norm2_g"] = jnp.concatenate([g_a["norm2_g0"], g_c["norm2_g1"]], axis=0)
    grads["final_g"] = dfinal[0]

    dmod_local = jnp.concatenate([jnp.pad(dm, ((0, COND_ROWS - bsz - 1), (0, 0))) for dm in dmodrows], axis=0)
    (dmod_all,) = _all_gather([dmod_local], "gather_dmod")
    dmod_all = dmod_all.reshape(N_DEV, 2, COND_ROWS, N_DEV * mod_cols)
    grads_a = pack_grads("a", g_a)
    exchange_a, token = _push_start(grads_a, dmod_all, "exchange_grads_a_start")
    dmod_all = dmod_all + token[0, 0]
    reduced = {}
    grad_ada_w, grad_ada_b, dmod_mine = [], [], []
    for i in range(2):
        dmod = dmod_all[:, i].reshape(N_DEV * COND_ROWS, N_DEV * mod_cols)
        grad_ada_b.append(_sum_rows(dmod, f"ada{i}_db")[0])
        dmod_mine.append(lax.dynamic_slice_in_dim(dmod, me * mod_cols, mod_cols, axis=1))
        grad_ada_w.append(_mm(silu_rows, dmod_mine[i], "tn", (F32,), f"ada{i}_dw")[0])
    reduced["ada_w"], reduced["ada_b"] = jnp.stack(grad_ada_w), jnp.stack(grad_ada_b)
    dsilu = _mm(jnp.concatenate(dmod_mine, axis=1), jnp.concatenate([ada_w[0], ada_w[1]], axis=1), "nt", (F32,), "ada_dx")[0]
    (dcond,) = silu_pullback(dsilu)
    grads["c_ctx"] = _sum_rows(dcond.reshape(N_DEV, COND_ROWS, d)[:, bsz], "c_ctx_rows")[0]

    small_names = list(_REPLICATED) + [n for n, _ in _SMALL_SHARDED]
    small_pack = _Packing([(n, full[n].size) for n in small_names], 8)
    (small_all,) = _all_gather([small_pack.pack({n: grads[n].astype(F32) for n in small_names})], "gather_small_grads")
    small_sum = _sum_blocks(small_all, "sum_small_grads")
    for n in _REPLICATED:
        reduced[n] = small_pack.piece(small_sum, n).reshape(local[n].shape)
    for n, axis in _SMALL_SHARDED:
        whole = small_pack.piece(small_sum, n).reshape(full[n].shape)
        reduced[n] = lax.dynamic_slice_in_dim(whole, me * local[n].shape[axis], local[n].shape[axis], axis=axis)

    def own_blocks(blocks):
        return [lax.dynamic_index_in_dim(t, me, 0, keepdims=False) for t in blocks]

    received = {s: _push_wait(handle, small_sum, own_blocks(blocks), me, f"exchange_grads_{s}_wait")
                for s, handle, blocks in (("c", exchange_c, grads_c), ("b", exchange_b, grads_b), ("a", exchange_a, grads_a))}
    piece_grad = {}
    for s in "abc":
        for p, blocks in zip(stages[s], received[s]):
            summed = _sum_blocks(blocks, "sum_grads_" + p)[:block_rows[p]]
            piece_grad[p] = summed.T if by_columns[p] else summed
    for n in ("w_out", "mlp_w1", "mlp_w2"):
        reduced[n] = jnp.stack([piece_grad[p] for p, wn, _, _, _ in _PIECES if wn == n])
    for n in ("ev_w_in", "od_w_in", "od_w_uq", "od_w_ukv"):
        reduced[n] = piece_grad[n][None]

    delta, new_m, new_v = {}, {}, {}
    for n in names:
        delta[n], new_m[n], new_v[n] = _adamw(local[n], reduced[n], mom1[n], mom2[n], "adamw_" + n)
    return (loss, grad_x, *[reduced[n] for n in names], *[delta[n] for n in names], *[new_m[n] for n in names], *[new_v[n] for n in names])
```

```python
import functools
import math

import jax
import jax.numpy as jnp
from jax import lax
from jax.experimental import pallas as pl
from jax.experimental.pallas import tpu as pltpu

F32, BF16 = jnp.float32, jnp.bfloat16

EPS = 1e-6
GRID_W = 64
ROPE_THETA = 10000.0
A_HEAD_DIM, A_Q_HEADS, A_KV_HEADS = 64, 8, 2
B_GROUPS, B_GROUP_DIM, B_CHUNK = 8, 64, 128
C_HEADS, C_NOPE, C_ROPE, C_V, C_Q_RANK, C_KV_RANK = 8, 64, 32, 64, 256, 128
D_CONV = 31
CONV_PAD = D_CONV // 2
N_MOD = 6
N_DEV = 8
MESH_AXES = ("x", "y", "c")

ADAM_LR, ADAM_B1, ADAM_B2, ADAM_EPS, ADAM_WD, ADAM_STEP = 0.001, 0.9, 0.999, 1e-08, 0.01, 10

VMEM_LIMIT = 56 * 1024 * 1024
PACK_COLS = 1024
ROW_ALIGN = 16
SUBLANES = 8
COND_ROWS = 8


def _pcall(body, **kw):
    return pl.pallas_call(body, **kw)


def _params(sem=None):
    return pltpu.CompilerParams(dimension_semantics=sem, vmem_limit_bytes=VMEM_LIMIT)


def _pick(n, cands):
    for c in cands:
        if n % c == 0:
            return c
    return n


def _mm(a, b, mode, out_dtypes, name, epi=None, extras=()):
    if mode == "tn":
        kk, m = a.shape
    else:
        m, kk = a.shape
    n = b.shape[0] if mode == "nt" else b.shape[1]
    tm = _pick(m, (1152, 1024, 896, 768, 512, 256, 128))
    tn = _pick(n, (1024, 896, 768, 512, 256, 128))
    tk = kk if kk <= 1024 else _pick(kk, (1024, 896, 768, 512, 256, 128))
    nk = kk // tk
    ne, no = len(extras), len(out_dtypes)
    a_spec = pl.BlockSpec((tk, tm), lambda i, j, k: (k, i)) if mode == "tn" else pl.BlockSpec((tm, tk), lambda i, j, k: (i, k))
    b_spec = pl.BlockSpec((tn, tk), lambda i, j, k: (j, k)) if mode == "nt" else pl.BlockSpec((tk, tn), lambda i, j, k: (k, j))
    t_spec = pl.BlockSpec((tm, tn), lambda i, j, k: (i, j))
    dn = {"nn": ((1,), (0,)), "nt": ((1,), (1,)), "tn": ((0,), (0,))}[mode]

    def body(a_ref, b_ref, *rest):
        extra_refs, out_refs = rest[:ne], rest[ne:ne + no]

        def finish(acc):
            outs = (acc,) if epi is None else epi(acc, *[r[...] for r in extra_refs])
            for r, o in zip(out_refs, outs):
                r[...] = o.astype(r.dtype)

        part = lax.dot_general(a_ref[...].astype(BF16), b_ref[...].astype(BF16), (dn, ((), ())), preferred_element_type=F32)
        if nk == 1:
            finish(part)
        else:
            acc_ref = rest[-1]
            k = pl.program_id(2)

            @pl.when(k == 0)
            def _():
                acc_ref[...] = part

            @pl.when(k > 0)
            def _():
                acc_ref[...] += part

            @pl.when(k == nk - 1)
            def _():
                finish(acc_ref[...])

    outs = _pcall(
        body, name=name, grid=(m // tm, n // tn, nk),
        in_specs=[a_spec, b_spec] + [t_spec] * ne,
        out_specs=[t_spec] * no,
        out_shape=[jax.ShapeDtypeStruct((m, n), d) for d in out_dtypes],
        scratch_shapes=[pltpu.VMEM((tm, tn), F32)] if nk > 1 else [],
        compiler_params=_params(("parallel", "parallel", "arbitrary")),
    )(a, b, *extras)
    return outs


def _linear(name, out_dtype=F32):
    @jax.custom_vjp
    def op(x, wt):
        return _mm(x, wt, "nt", (out_dtype,), name + "_fwd")[0]

    def fwd(x, wt):
        return op(x, wt), (x, wt)

    def bwd(res, dy):
        x, wt = res
        return _mm(dy, wt, "nn", (x.dtype,), name + "_dx")[0], _mm(dy, x, "tn", (wt.dtype,), name + "_dw")[0]

    op.defvjp(fwd, bwd)
    return op


_NT, _NN, _TN = (((1,), (1,)), ((), ())), (((1,), (0,)), ((), ())), (((0,), (0,)), ((), ()))
_ROW_TILES = (1152, 1024, 768, 512, 256, 128)


def _whole(w):
    return pl.BlockSpec(w.shape, lambda i: (0, 0))


def _groups_apply(x, ws, out_dtypes, name):
    n, (m, kk) = len(ws), x.shape
    tm = _pick(m, _ROW_TILES)

    def body(x_ref, *refs):
        a = x_ref[...].astype(BF16)
        for w_ref, o_ref in zip(refs[:n], refs[n:]):
            o_ref[...] = lax.dot_general(a, w_ref[...], _NT, preferred_element_type=F32).astype(o_ref.dtype)

    return _pcall(
        body, name=name, grid=(m // tm,),
        in_specs=[pl.BlockSpec((tm, kk), lambda i: (i, 0))] + [_whole(w) for w in ws],
        out_specs=[pl.BlockSpec((tm, w.shape[0]), lambda i: (i, 0)) for w in ws],
        out_shape=[jax.ShapeDtypeStruct((m, w.shape[0]), dt) for w, dt in zip(ws, out_dtypes)],
        compiler_params=_params(("parallel",)),
    )(x, *ws)


def _groups_sum(xs, ws, out_dtype, name):
    n, m, kk = len(ws), xs[0].shape[0], ws[0].shape[1]
    tm = _pick(m, _ROW_TILES)

    def body(*refs):
        acc = None
        for x_ref, w_ref in zip(refs[:n], refs[n:2 * n]):
            part = lax.dot_general(x_ref[...].astype(BF16), w_ref[...], _NN, preferred_element_type=F32)
            acc = part if acc is None else acc + part
        refs[2 * n][...] = acc.astype(out_dtype)

    return _pcall(
        body, name=name, grid=(m // tm,),
        in_specs=[pl.BlockSpec((tm, w.shape[0]), lambda i: (i, 0)) for w in ws] + [_whole(w) for w in ws],
        out_specs=pl.BlockSpec((tm, kk), lambda i: (i, 0)), out_shape=jax.ShapeDtypeStruct((m, kk), out_dtype),
        compiler_params=_params(("parallel",)),
    )(*xs, *ws)


def _groups_outer(xs, y, ws, name):
    n, (m, kk) = len(ws), y.shape
    tk = _pick(m, (768, 512, 256, 128))
    steps = m // tk

    def body(y_ref, *refs):
        x_refs, o_refs, acc_refs = refs[:n], refs[n:2 * n], refs[2 * n:]
        k = pl.program_id(0)
        b = y_ref[...].astype(BF16)
        for x_ref, o_ref, acc_ref in zip(x_refs, o_refs, acc_refs):
            part = lax.dot_general(x_ref[...].astype(BF16), b, _TN, preferred_element_type=F32)

            @pl.when(k == 0)
            def _(acc_ref=acc_ref, part=part):
                acc_ref[...] = part

            @pl.when(k > 0)
            def _(acc_ref=acc_ref, part=part):
                acc_ref[...] += part

            @pl.when(k == steps - 1)
            def _(acc_ref=acc_ref, o_ref=o_ref):
                o_ref[...] = acc_ref[...].astype(o_ref.dtype)

    return _pcall(
        body, name=name, grid=(steps,),
        in_specs=[pl.BlockSpec((tk, kk), lambda k: (k, 0))] + [pl.BlockSpec((tk, w.shape[0]), lambda k: (k, 0)) for w in ws],
        out_specs=[_whole(w) for w in ws], out_shape=[jax.ShapeDtypeStruct(w.shape, w.dtype) for w in ws],
        scratch_shapes=[pltpu.VMEM(w.shape, F32) for w in ws], compiler_params=_params(("arbitrary",)),
    )(y, *xs)


def _linear_multi(name, out_dtypes):
    @jax.custom_vjp
    def op(x, wts):
        return tuple(_groups_apply(x, wts, out_dtypes, name + "_fwd"))

    def fwd(x, wts):
        return op(x, wts), (x, wts)

    def bwd(res, dys):
        x, wts = res
        return _groups_sum(dys, wts, x.dtype, name + "_dx"), tuple(_groups_outer(dys, x, wts, name + "_dw"))

    op.defvjp(fwd, bwd)
    return op


def _linear_sum(name):
    @jax.custom_vjp
    def op(xs, ws):
        return _groups_sum(xs, ws, F32, name + "_fwd")

    def fwd(xs, ws):
        return op(xs, ws), (xs, ws)

    def bwd(res, dy):
        xs, ws = res
        return tuple(_groups_apply(dy, ws, [x.dtype for x in xs], name + "_dx")), tuple(_groups_outer(xs, dy, ws, name + "_dw"))

    op.defvjp(fwd, bwd)
    return op


def _relu2_epi(acc):
    return jnp.square(jnp.maximum(acc, 0.0)), acc


def _relu2_bwd_epi(acc, a):
    return (acc * (2.0 * jnp.maximum(a.astype(F32), 0.0)),)


def _mlp(name):
    @jax.custom_vjp
    def op(h, w1t, w2):
        s, _ = _mm(h, w1t, "nt", (BF16, BF16), name + "_up", epi=_relu2_epi)
        return _mm(s, w2, "nn", (F32,), name + "_down")[0]

    def fwd(h, w1t, w2):
        s, a = _mm(h, w1t, "nt", (BF16, BF16), name + "_up", epi=_relu2_epi)
        return _mm(s, w2, "nn", (F32,), name + "_down")[0], (h, w1t, w2, s, a)

    def bwd(res, dy):
        h, w1t, w2, s, a = res
        da = _mm(dy, w2, "nt", (BF16,), name + "_ds", epi=_relu2_bwd_epi, extras=(a,))[0]
        dw2 = _mm(s, dy, "tn", (w2.dtype,), name + "_dw2")[0]
        dw1t = _mm(da, h, "tn", (w1t.dtype,), name + "_dw1")[0]
        dh = _mm(da, w1t, "nn", (h.dtype,), name + "_dh")[0]
        return dh, dw1t, dw2

    op.defvjp(fwd, bwd)
    return op


def _two_pass_dot(x, m):
    hi = x.astype(BF16)
    lo = (x - hi.astype(F32)).astype(BF16)
    mb = m.astype(BF16)
    return jnp.dot(hi, mb, preferred_element_type=F32) + jnp.dot(lo, mb, preferred_element_type=F32)


@jax.custom_vjp
def _sym_dot(x, m):
    return _two_pass_dot(x, m)


def _sym_dot_fwd(x, m):
    return _two_pass_dot(x, m), m


def _sym_dot_bwd(m, g):
    return _two_pass_dot(g, m), jnp.zeros_like(m)


_sym_dot.defvjp(_sym_dot_fwd, _sym_dot_bwd)


def _neighbour(x):
    lane = lax.broadcasted_iota(jnp.int32, x.shape, 1)
    return jnp.where(lane % 2 == 0, pltpu.roll(x, x.shape[1] - 1, 1), pltpu.roll(x, 1, 1))


@jax.custom_vjp
def _swap_pairs(x):
    return _neighbour(x)


_swap_pairs.defvjp(lambda x: (_neighbour(x), None), lambda _, g: (_neighbour(g),))


def _lane_group(x, k):
    return x[:, k * B_GROUP_DIM:(k + 1) * B_GROUP_DIM]


@jax.custom_vjp
def _group_mix(ws, v):
    return jnp.concatenate([jnp.dot(w.astype(BF16), _lane_group(v, k).astype(BF16), preferred_element_type=F32)
                            for k, w in enumerate(ws)], axis=1)


def _group_mix_fwd(ws, v):
    return _group_mix(ws, v), (ws, v)


def _group_mix_bwd(res, d):
    ws, v = res
    parts = [(_lane_group(d, k).astype(BF16), _lane_group(v, k).astype(BF16)) for k in range(len(ws))]
    dws = tuple(lax.dot_general(dk, vk, (((1,), (1,)), ((), ())), preferred_element_type=F32) for dk, vk in parts)
    dv = jnp.concatenate([lax.dot_general(w.astype(BF16), dk, (((0,), (0,)), ((), ())), preferred_element_type=F32)
                          for w, (dk, _) in zip(ws, parts)], axis=1)
    return dws, dv


_group_mix.defvjp(_group_mix_fwd, _group_mix_bwd)


def _rowwise(name, f, out_specs, tl, ctx_blocks=0):
    def seg(l, s):
        return jnp.where(l >= ctx_blocks, s - 1, 0) if s > 1 else 0

    def specs(rows, tabs, pers, glbs, consts):
        row_specs = [pl.BlockSpec((1, tl, r.shape[2]), lambda b, l: (b, l, 0)) for r in rows]
        tab_specs = [pl.BlockSpec((tl, t.shape[1]), lambda b, l: (l, 0)) for t in tabs]
        per_specs = [pl.BlockSpec((1, 1, 1, p.shape[3]), functools.partial(lambda b, l, s: (b, seg(l, s), 0, 0), s=p.shape[1])) for p in pers]
        glb_specs = [pl.BlockSpec(g.shape, functools.partial(lambda b, l, nd: (0,) * nd, nd=g.ndim)) for g in glbs]
        const_specs = [pl.BlockSpec(c.shape, functools.partial(lambda b, l, nd: (0,) * nd, nd=c.ndim)) for c in consts]
        return row_specs, tab_specs, per_specs, glb_specs, const_specs

    def load(refs_rows, refs_tabs, refs_pers, refs_glbs, refs_consts):
        return (tuple(r[0].astype(F32) for r in refs_rows), tuple(t[...] for t in refs_tabs),
                tuple(p[0, 0].astype(F32) for p in refs_pers), tuple(g[...].astype(F32) for g in refs_glbs),
                tuple(c[...] for c in refs_consts))

    def call_fwd(rows, tabs, pers, glbs, consts):
        bsz, length = rows[0].shape[:2]
        nr, nt, npp, ng, nc = len(rows), len(tabs), len(pers), len(glbs), len(consts)
        rs, ts, ps, gs, cs = specs(rows, tabs, pers, glbs, consts)

        def body(*refs):
            ins, outs = refs[:nr + nt + npp + ng + nc], refs[nr + nt + npp + ng + nc:]
            r, t, p, g, c = load(ins[:nr], ins[nr:nr + nt], ins[nr + nt:nr + nt + npp], ins[nr + nt + npp:nr + nt + npp + ng], ins[nr + nt + npp + ng:])
            for o_ref, o in zip(outs, f(r, t, p, g, c)):
                o_ref[0] = o.astype(o_ref.dtype)

        return _pcall(
            body, name=name + "_fwd", grid=(bsz, length // tl),
            in_specs=rs + ts + ps + gs + cs,
            out_specs=[pl.BlockSpec((1, tl, w), lambda b, l: (b, l, 0)) for w, _ in out_specs],
            out_shape=[jax.ShapeDtypeStruct((bsz, length, w), d) for w, d in out_specs],
            compiler_params=_params(("parallel", "parallel")),
        )(*rows, *tabs, *pers, *glbs, *consts)

    def call_bwd(rows, tabs, pers, glbs, consts, cts):
        bsz, length = rows[0].shape[:2]
        nr, nt, npp, ng, nc, no = len(rows), len(tabs), len(pers), len(glbs), len(consts), len(cts)
        rs, ts, ps, gs, cs = specs(rows, tabs, pers, glbs, consts)
        n_in = nr + nt + npp + ng + nc

        def body(*refs):
            ins, ct_refs, outs = refs[:n_in], refs[n_in:n_in + no], refs[n_in + no:]
            r, t, p, g, c = load(ins[:nr], ins[nr:nr + nt], ins[nr + nt:nr + nt + npp], ins[nr + nt + npp:nr + nt + npp + ng], ins[nr + nt + npp + ng:])
            _, vjp = jax.vjp(lambda r_, p_, g_: tuple(f(r_, t, p_, g_, c)), r, p, g)
            dr, dp, dg = vjp(tuple(ct[0].astype(F32) for ct in ct_refs))
            dr_refs, dp_refs, dg_refs = outs[:nr], outs[nr:nr + npp], outs[nr + npp:]
            for ref, d in zip(dr_refs, dr):
                ref[0] = d.astype(ref.dtype)
            b, l = pl.program_id(0), pl.program_id(1)
            first_of_segment = (l == 0) | (l == ctx_blocks)
            for ref, d in zip(dp_refs, dp):
                @pl.when(first_of_segment)
                def _(ref=ref, d=d):
                    ref[0, 0] = d

                @pl.when(jnp.logical_not(first_of_segment))
                def _(ref=ref, d=d):
                    ref[0, 0] += d
            first = (b == 0) & (l == 0)
            for ref, d in zip(dg_refs, dg):
                @pl.when(first)
                def _(ref=ref, d=d):
                    ref[...] = d

                @pl.when(jnp.logical_not(first))
                def _(ref=ref, d=d):
                    ref[...] += d

        ct_specs = [pl.BlockSpec((1, tl, w), lambda b, l: (b, l, 0)) for w, _ in out_specs]
        outs = _pcall(
            body, name=name + "_bwd", grid=(bsz, length // tl),
            in_specs=rs + ts + ps + gs + cs + ct_specs,
            out_specs=rs + ps + gs,
            out_shape=[jax.ShapeDtypeStruct(r.shape, r.dtype) for r in rows]
            + [jax.ShapeDtypeStruct(p.shape, F32) for p in pers] + [jax.ShapeDtypeStruct(g.shape, F32) for g in glbs],
            compiler_params=_params(("arbitrary", "arbitrary")),
        )(*rows, *tabs, *pers, *glbs, *consts, *cts)
        return tuple(outs[:nr]), tuple(outs[nr:nr + npp]), tuple(outs[nr + npp:])

    @jax.custom_vjp
    def op(rows, tabs, pers, glbs, consts):
        return tuple(call_fwd(rows, tabs, pers, glbs, consts))

    def fwd(rows, tabs, pers, glbs, consts):
        return op(rows, tabs, pers, glbs, consts), (rows, tabs, pers, glbs, consts)

    def bwd(res, cts):
        rows, tabs, pers, glbs, consts = res
        dr, dp, dg = call_bwd(rows, tabs, pers, glbs, consts, tuple(cts))
        dp = tuple(d.astype(p.dtype) for d, p in zip(dp, pers))
        dg = tuple(d.astype(g.dtype) for d, g in zip(dg, glbs))
        return dr, tuple(jnp.zeros_like(t) for t in tabs), dp, dg, tuple(jnp.zeros_like(c) for c in consts)

    op.defvjp(fwd, bwd)
    return op


def _rms(x, g):
    return x * lax.rsqrt(jnp.mean(x * x, axis=-1, keepdims=True) + EPS) * g


def _f_silu(r, t, p, g, c):
    return (jax.nn.silu(r[0]),)


def _f_modulate(r, t, p, g, c):
    shift, scale = p
    return (_rms(r[0], g[0]) * (1.0 + scale) + shift,)


def _f_res_modulate(r, t, p, g, c):
    x, y = r
    gate, shift, scale = p
    xn = x + gate * y
    return xn, _rms(xn, g[0]) * (1.0 + scale) + shift


def _f_res(r, t, p, g, c):
    return (r[0] + p[0] * r[1],)


def _f_headnorm_rope(r, t, p, g, c):
    x = r[0]
    cos, sin = t
    xn = x * lax.rsqrt(_sym_dot(x * x, c[0]) + EPS) * g[0]
    return (xn * cos + _swap_pairs(xn) * sin,)


def _f_rope(r, t, p, g, c):
    x = r[0]
    cos, sin = t
    return (x * cos + _swap_pairs(x) * sin,)


def _f_rms(r, t, p, g, c):
    return (_rms(r[0], g[0]),)


def _f_sgu_pre(r, t, p, g, c):
    u = jax.nn.gelu(r[0])
    v = jax.nn.gelu(r[1])
    vn = v * lax.rsqrt(_sym_dot(v * v, c[0]) + EPS) * g[0]
    return u, vn


def _f_sgu_mix(r, t, p, g, c):
    u, vn = r
    return (u * (g[B_GROUPS] + _group_mix(tuple(g[:B_GROUPS]), vn)),)


def _f_glu(r, t, p, g, c):
    return (r[0] * jax.nn.sigmoid(r[1]),)


def _f_ln_silu(r, t, p, g, c):
    x = r[0]
    mu = jnp.mean(x, axis=-1, keepdims=True)
    var = jnp.mean(jnp.square(x - mu), axis=-1, keepdims=True)
    return (jax.nn.silu((x - mu) * lax.rsqrt(var + EPS) * g[0] + g[1]),)


LOG2_E = 1.4426950408889634
LN_2 = 0.6931471805599453


def _attention_rows(name, ctx_len, heads, kv_heads):
    grp = heads // kv_heads

    def by_segment(qi, cb, lk, run):
        if cb > 0:
            @pl.when(qi < cb)
            def _():
                run(ctx_len)

            @pl.when(qi >= cb)
            def _():
                run(lk)
        else:
            run(lk)

    def head(ref, h, hd):
        return ref[0, :, h * hd:(h + 1) * hd]

    def scores(q_tile, kk, hd):
        qs = (q_tile.astype(F32) * (LOG2_E * hd ** -0.5)).astype(BF16)
        return qs, lax.dot_general(qs, kk, _NT, preferred_element_type=F32)

    def shapes(q):
        bsz, length, width = q.shape
        tq = math.gcd(_pick(length, (256, 128)), ctx_len) if ctx_len else _pick(length, (256, 128))
        return bsz, length, width // heads, tq, ctx_len // tq

    def call_fwd(q, k, v):
        bsz, length, hd, tq, cb = shapes(q)

        def body(q_ref, k_ref, v_ref, o_ref, lse_ref):
            def run(nk):
                k_all, v_all = k_ref[0, :nk], v_ref[0, :nk]
                outs = []
                for j in range(kv_heads):
                    kk = k_all[:, j * hd:(j + 1) * hd]
                    v_ones = jnp.concatenate([v_all[:, j * hd:(j + 1) * hd], jnp.ones((nk, hd), BF16)], axis=1)
                    for h in range(j * grp, (j + 1) * grp):
                        _, s = scores(head(q_ref, h, hd), kk, hd)
                        m = jnp.max(s, axis=-1, keepdims=True)
                        acc = jnp.dot(jnp.exp2(s - m).astype(BF16), v_ones, preferred_element_type=F32)
                        l = acc[:, hd:hd + 1]
                        outs.append((acc[:, :hd] / l).astype(BF16))
                        lse_ref[0, h] = m + jnp.log2(l)
                o_ref[0] = jnp.concatenate(outs, axis=1)

            by_segment(pl.program_id(1), cb, length, run)

        q_spec = pl.BlockSpec((1, tq, heads * hd), lambda b, i: (b, i, 0))
        kv_spec = pl.BlockSpec((1, length, kv_heads * hd), lambda b, i: (b, 0, 0))
        return _pcall(
            body, name=name + "_fwd", grid=(bsz, length // tq), in_specs=[q_spec, kv_spec, kv_spec],
            out_specs=[q_spec, pl.BlockSpec((1, heads, tq, 1), lambda b, i: (b, 0, i, 0))],
            out_shape=[jax.ShapeDtypeStruct(q.shape, BF16), jax.ShapeDtypeStruct((bsz, heads, length, 1), F32)],
            compiler_params=_params(("parallel", "parallel")),
        )(q, k, v)

    def call_bwd(q, k, v, o, lse, do):
        bsz, length, hd, tq, cb = shapes(q)
        nq = length // tq

        def body(q_ref, k_ref, v_ref, o_ref, lse_ref, do_ref, dq_ref, dk_ref, dv_ref, dk_acc, dv_acc):
            qi = pl.program_id(1)

            @pl.when(qi == 0)
            def _():
                dk_acc[...] = jnp.zeros_like(dk_acc)
                dv_acc[...] = jnp.zeros_like(dv_acc)

            def run(nk):
                k_all, v_all = k_ref[0, :nk], v_ref[0, :nk]
                dqs, dks, dvs = [], [], []
                for j in range(kv_heads):
                    kk, vv = k_all[:, j * hd:(j + 1) * hd], v_all[:, j * hd:(j + 1) * hd]
                    dk_sum = dv_sum = None
                    for h in range(j * grp, (j + 1) * grp):
                        dd = head(do_ref, h, hd)
                        qs, s = scores(head(q_ref, h, hd), kk, hd)
                        p = jnp.exp2(s - lse_ref[0, h])
                        delta = jnp.sum(dd.astype(F32) * head(o_ref, h, hd).astype(F32), axis=-1, keepdims=True)
                        t = (p * (lax.dot_general(dd, vv, _NT, preferred_element_type=F32) - delta)).astype(BF16)
                        dqs.append((jnp.dot(t, kk, preferred_element_type=F32) * hd ** -0.5).astype(BF16))
                        dk_h = lax.dot_general(t, qs, _TN, preferred_element_type=F32)
                        dv_h = lax.dot_general(p.astype(BF16), dd, _TN, preferred_element_type=F32)
                        dk_sum = dk_h if dk_sum is None else dk_sum + dk_h
                        dv_sum = dv_h if dv_sum is None else dv_sum + dv_h
                    dks.append(dk_sum)
                    dvs.append(dv_sum)
                dq_ref[0] = jnp.concatenate(dqs, axis=1)
                dk_acc[:nk] += jnp.concatenate(dks, axis=1)
                dv_acc[:nk] += jnp.concatenate(dvs, axis=1)

            by_segment(qi, cb, length, run)

            @pl.when(qi == nq - 1)
            def _():
                dk_ref[0] = (dk_acc[...] * LN_2).astype(dk_ref.dtype)
                dv_ref[0] = dv_acc[...].astype(dv_ref.dtype)

        q_spec = pl.BlockSpec((1, tq, heads * hd), lambda b, i: (b, i, 0))
        kv_spec = pl.BlockSpec((1, length, kv_heads * hd), lambda b, i: (b, 0, 0))
        lse_spec = pl.BlockSpec((1, heads, tq, 1), lambda b, i: (b, 0, i, 0))
        return _pcall(
            body, name=name + "_bwd", grid=(bsz, nq), in_specs=[q_spec, kv_spec, kv_spec, q_spec, lse_spec, q_spec],
            out_specs=[q_spec, kv_spec, kv_spec],
            out_shape=[jax.ShapeDtypeStruct(q.shape, BF16), jax.ShapeDtypeStruct(k.shape, BF16), jax.ShapeDtypeStruct(v.shape, BF16)],
            scratch_shapes=[pltpu.VMEM(k.shape[1:], F32), pltpu.VMEM(v.shape[1:], F32)],
            compiler_params=_params(("parallel", "arbitrary")),
        )(q, k, v, o, lse, do)

    @jax.custom_vjp
    def op(q, k, v):
        return call_fwd(q, k, v)[0]

    def fwd(q, k, v):
        o, lse = call_fwd(q, k, v)
        return o, (q, k, v, o, lse)

    def bwd(res, do):
        q, k, v, o, lse = res
        dq, dk, dv = call_bwd(q, k, v, o, lse, do)
        return dq, dk.astype(k.dtype), dv.astype(v.dtype)

    op.defvjp(fwd, bwd)
    return op


def _attention_latent_rows(name, heads, nope, rope, dv, heads_per_step):
    lanes = 128
    scale = (nope + rope) ** -0.5

    def shapes(q, kn):
        bsz, lq, _ = q.shape
        return bsz, lq, kn.shape[1], _pick(lq, (256, 128))

    def scores(q_tile, kk):
        qs = (q_tile.astype(F32) * (LOG2_E * scale)).astype(BF16)
        return qs, lax.dot_general(qs, kk, _NT, preferred_element_type=F32)

    def call_fwd(q, kn, kr, v):
        bsz, lq, lk, tq = shapes(q, kn)

        def body(q_ref, kn_ref, kr_ref, v_ref, o_ref, lse_ref):
            real = lax.broadcasted_iota(jnp.int32, (lk, lanes), 1) < dv
            outs = []
            for h in range(heads_per_step):
                kk = kn_ref[0, :, h * lanes:(h + 1) * lanes] + kr_ref[0]
                v_ones = jnp.where(real, v_ref[0, :, h * lanes:(h + 1) * lanes], jnp.ones((lk, lanes), BF16))
                _, s = scores(q_ref[0, :, h * lanes:(h + 1) * lanes], kk)
                m = jnp.max(s, axis=-1, keepdims=True)
                acc = jnp.dot(jnp.exp2(s - m).astype(BF16), v_ones, preferred_element_type=F32)
                l = acc[:, dv:dv + 1]
                outs.append((acc[:, :dv] / l).astype(BF16))
                lse_ref[0, h] = m + jnp.log2(l)
            o_ref[0] = jnp.concatenate(outs, axis=1)

        wide = heads_per_step * lanes
        return _pcall(
            body, name=name + "_fwd", grid=(bsz, heads // heads_per_step, lq // tq),
            in_specs=[pl.BlockSpec((1, tq, wide), lambda b, g, i: (b, i, g)), pl.BlockSpec((1, lk, wide), lambda b, g, i: (b, 0, g)),
                      pl.BlockSpec((1, lk, lanes), lambda b, g, i: (b, 0, 0)), pl.BlockSpec((1, lk, wide), lambda b, g, i: (b, 0, g))],
            out_specs=[pl.BlockSpec((1, tq, heads_per_step * dv), lambda b, g, i: (b, i, g)),
                       pl.BlockSpec((1, heads_per_step, tq, 1), lambda b, g, i: (b, g, i, 0))],
            out_shape=[jax.ShapeDtypeStruct((bsz, lq, heads * dv), BF16), jax.ShapeDtypeStruct((bsz, heads, lq, 1), F32)],
            compiler_params=_params(("parallel", "parallel", "parallel")),
        )(q, kn, kr, v)

    def call_bwd(q, kn, kr, v, o, lse, do):
        bsz, lq, lk, tq = shapes(q, kn)

        def body(q_ref, kn_ref, kr_ref, v_ref, o_ref, lse_ref, do_ref, dq_ref, dkn_ref, dkr_ref, dv_ref, dkn_acc, dkr_acc, dv_acc):
            g, qi = pl.program_id(1), pl.program_id(2)
            last_q = qi == lq // tq - 1

            @pl.when(qi == 0)
            def _():
                dkn_acc[...] = jnp.zeros_like(dkn_acc)
                dv_acc[...] = jnp.zeros_like(dv_acc)

            @pl.when((qi == 0) & (g == 0))
            def _():
                dkr_acc[...] = jnp.zeros_like(dkr_acc)

            dqs, dks, dvs = [], [], []
            for h in range(heads_per_step):
                kk = kn_ref[0, :, h * lanes:(h + 1) * lanes] + kr_ref[0]
                vv = v_ref[0, :, h * lanes:h * lanes + dv]
                dd = do_ref[0, :, h * dv:(h + 1) * dv]
                qs, s = scores(q_ref[0, :, h * lanes:(h + 1) * lanes], kk)
                p = jnp.exp2(s - lse_ref[0, h])
                delta = jnp.sum(dd.astype(F32) * o_ref[0, :, h * dv:(h + 1) * dv].astype(F32), axis=-1, keepdims=True)
                t = (p * (lax.dot_general(dd, vv, _NT, preferred_element_type=F32) - delta)).astype(BF16)
                dqs.append((jnp.dot(t, kk, preferred_element_type=F32) * scale).astype(BF16))
                dks.append(lax.dot_general(t, qs, _TN, preferred_element_type=F32) * LN_2)
                dv_h = lax.dot_general(p.astype(BF16), dd, _TN, preferred_element_type=F32)
                dvs.append(jnp.concatenate([dv_h, jnp.zeros((lk, lanes - dv), F32)], axis=1))
            dq_ref[0] = jnp.concatenate(dqs, axis=1)
            dkn_acc[...] += jnp.concatenate(dks, axis=1)
            dv_acc[...] += jnp.concatenate(dvs, axis=1)
            shared = dks[0]
            for d_h in dks[1:]:
                shared = shared + d_h
            dkr_acc[...] += shared

            @pl.when(last_q)
            def _():
                dkn_ref[0] = dkn_acc[...].astype(dkn_ref.dtype)
                dv_ref[0] = dv_acc[...].astype(dv_ref.dtype)

            @pl.when(last_q & (g == heads // heads_per_step - 1))
            def _():
                dkr_ref[0] = dkr_acc[...].astype(dkr_ref.dtype)

        wide = heads_per_step * lanes
        q_spec = pl.BlockSpec((1, tq, wide), lambda b, g, i: (b, i, g))
        k_spec = pl.BlockSpec((1, lk, wide), lambda b, g, i: (b, 0, g))
        kr_spec = pl.BlockSpec((1, lk, lanes), lambda b, g, i: (b, 0, 0))
        o_spec = pl.BlockSpec((1, tq, heads_per_step * dv), lambda b, g, i: (b, i, g))
        lse_spec = pl.BlockSpec((1, heads_per_step, tq, 1), lambda b, g, i: (b, g, i, 0))
        return _pcall(
            body, name=name + "_bwd", grid=(bsz, heads // heads_per_step, lq // tq),
            in_specs=[q_spec, k_spec, kr_spec, k_spec, o_spec, lse_spec, o_spec],
            out_specs=[q_spec, k_spec, kr_spec, k_spec],
            out_shape=[jax.ShapeDtypeStruct(q.shape, BF16), jax.ShapeDtypeStruct(kn.shape, BF16), jax.ShapeDtypeStruct(kr.shape, BF16),
                       jax.ShapeDtypeStruct(v.shape, BF16)],
            scratch_shapes=[pltpu.VMEM((lk, wide), F32), pltpu.VMEM((lk, lanes), F32), pltpu.VMEM((lk, wide), F32)],
            compiler_params=_params(("parallel", "arbitrary", "arbitrary")),
        )(q, kn, kr, v, o, lse, do)

    @jax.custom_vjp
    def op(q, kn, kr, v):
        return call_fwd(q, kn, kr, v)[0]

    def fwd(q, kn, kr, v):
        o, lse = call_fwd(q, kn, kr, v)
        return o, (q, kn, kr, v, o, lse)

    def bwd(res, do):
        q, kn, kr, v, o, lse = res
        dq, dkn, dkr, dv_ = call_bwd(q, kn, kr, v, o, lse, do)
        return dq, dkn.astype(kn.dtype), dkr.astype(kr.dtype), dv_.astype(v.dtype)

    op.defvjp(fwd, bwd)
    return op


def _conv_call(ypad, taps, name):
    bsz, lp, ch = ypad.shape
    length = lp - 2 * ROW_ALIGN
    tl = _pick(length, (256, 128))

    def body(y_ref, w_ref, o_ref):
        base = pl.multiple_of(pl.program_id(1) * tl, tl)
        win = y_ref[0, pl.ds(base, tl + 2 * ROW_ALIGN), :]
        acc = jnp.broadcast_to(w_ref[pl.ds(D_CONV, 1), :], (tl, ch))
        for shift in range(SUBLANES):
            shifted = win[shift:shift + tl + 2 * ROW_ALIGN - SUBLANES, :]
            for k in range(shift, D_CONV, SUBLANES):
                acc = acc + shifted[k - shift:k - shift + tl, :] * w_ref[pl.ds(k, 1), :]
        o_ref[0] = acc

    return _pcall(
        body, name=name, grid=(bsz, length // tl),
        in_specs=[pl.BlockSpec((1, lp, ch), lambda b, l: (b, 0, 0)), pl.BlockSpec((D_CONV + 1, ch), lambda b, l: (0, 0))],
        out_specs=pl.BlockSpec((1, tl, ch), lambda b, l: (b, l, 0)),
        out_shape=jax.ShapeDtypeStruct((bsz, length, ch), F32),
        compiler_params=_params(("parallel", "parallel")),
    )(ypad, taps)


def _conv_dw_call(ypad, dout, name):
    bsz, lp, ch = ypad.shape
    length = lp - 2 * ROW_ALIGN
    tl = _pick(length, (256, 128))

    def body(y_ref, d_ref, o_ref):
        b, l = pl.program_id(0), pl.program_id(1)

        @pl.when((b == 0) & (l == 0))
        def _():
            o_ref[...] = jnp.zeros_like(o_ref)

        base = pl.multiple_of(l * tl, tl)
        win = y_ref[0, pl.ds(base, tl + 2 * ROW_ALIGN), :]
        dd = d_ref[0]
        for shift in range(SUBLANES):
            shifted = win[shift:shift + tl + 2 * ROW_ALIGN - SUBLANES, :]
            for k in range(shift, D_CONV, SUBLANES):
                o_ref[pl.ds(k, 1), :] += jnp.sum(shifted[k - shift:k - shift + tl, :] * dd, axis=0, keepdims=True)
        o_ref[pl.ds(D_CONV, 1), :] += jnp.sum(dd, axis=0, keepdims=True)

    return _pcall(
        body, name=name, grid=(bsz, length // tl),
        in_specs=[pl.BlockSpec((1, lp, ch), lambda b, l: (b, 0, 0)), pl.BlockSpec((1, tl, ch), lambda b, l: (b, l, 0))],
        out_specs=pl.BlockSpec((D_CONV + 1, ch), lambda b, l: (0, 0)),
        out_shape=jax.ShapeDtypeStruct((D_CONV + 1, ch), F32),
        compiler_params=_params(("arbitrary", "arbitrary")),
    )(ypad, dout)


def _pad_rows(y):
    return jnp.pad(y, ((0, 0), (CONV_PAD, 2 * ROW_ALIGN - CONV_PAD), (0, 0)))


@jax.custom_vjp
def _dwconv(y, taps):
    return _conv_call(_pad_rows(y), taps, "conv_fwd")


def _dwconv_fwd(y, taps):
    return _dwconv(y, taps), (y, taps)


def _dwconv_bwd(res, dout):
    y, taps = res
    flipped = jnp.concatenate([taps[:D_CONV][::-1], jnp.zeros_like(taps[D_CONV:])], axis=0)
    dy = _conv_call(_pad_rows(dout), flipped, "conv_dy")
    dtaps = _conv_dw_call(_pad_rows(y), dout, "conv_dw")
    return dy, dtaps


_dwconv.defvjp(_dwconv_fwd, _dwconv_bwd)


def _loss_head(x, target, g):
    bsz, length, d = x.shape
    tl = _pick(length, (256, 128))

    def f(xb, tb, gb):
        err = _rms(xb, gb) - tb
        return 0.5 * jnp.sum(jnp.sum(err * err, axis=-1, keepdims=True), axis=0, keepdims=True) / d

    def body(x_ref, t_ref, g_ref, loss_ref, dx_ref, dg_ref):
        val, vjp = jax.vjp(lambda xb, gb: f(xb, t_ref[0], gb), x_ref[0], g_ref[...])
        dx, dg = vjp(jnp.ones((1, 1), F32))
        dx_ref[0] = dx
        first = (pl.program_id(0) == 0) & (pl.program_id(1) == 0)

        @pl.when(first)
        def _():
            loss_ref[...] = val
            dg_ref[...] = dg

        @pl.when(jnp.logical_not(first))
        def _():
            loss_ref[...] += val
            dg_ref[...] += dg

    row = pl.BlockSpec((1, tl, d), lambda b, l: (b, l, 0))
    return _pcall(
        body, name="loss_head", grid=(bsz, length // tl),
        in_specs=[row, row, pl.BlockSpec((1, d), lambda b, l: (0, 0))],
        out_specs=[pl.BlockSpec((1, 1), lambda b, l: (0, 0)), row, pl.BlockSpec((1, d), lambda b, l: (0, 0))],
        out_shape=[jax.ShapeDtypeStruct((1, 1), F32), jax.ShapeDtypeStruct(x.shape, F32), jax.ShapeDtypeStruct((1, d), F32)],
        compiler_params=_params(("arbitrary", "arbitrary")),
    )(x, target, g)


def _adamw(w, g, m, v, name):
    shape = w.shape
    cols = shape[-1]
    rows = w.size // cols
    tr = _pick(rows, (512, 256, 128))
    w2, g2, m2, v2 = (t.reshape(rows, cols) for t in (w, g, m, v))

    def body(w_ref, g_ref, m_ref, v_ref, d_ref, nm_ref, nv_ref):
        gg = g_ref[...]
        nm = ADAM_B1 * m_ref[...] + (1.0 - ADAM_B1) * gg
        nv = ADAM_B2 * v_ref[...] + (1.0 - ADAM_B2) * jnp.square(gg)
        m_hat = nm / (1.0 - ADAM_B1 ** ADAM_STEP)
        v_hat = nv / (1.0 - ADAM_B2 ** ADAM_STEP)
        d_ref[...] = -ADAM_LR * (m_hat / (jnp.sqrt(v_hat) + ADAM_EPS) + ADAM_WD * w_ref[...])
        nm_ref[...] = nm
        nv_ref[...] = nv

    spec = pl.BlockSpec((tr, cols), lambda i: (i, 0))
    outs = _pcall(
        body, name=name, grid=(rows // tr,), in_specs=[spec] * 4, out_specs=[spec] * 3,
        out_shape=[jax.ShapeDtypeStruct((rows, cols), F32)] * 3,
        compiler_params=_params(("parallel",)),
    )(w2, g2, m2, v2)
    return tuple(o.reshape(shape) for o in outs)


def _mesh_pos():
    return lax.axis_index("x"), lax.axis_index("y"), lax.axis_index("c")


_RELATIONS = [(dx, dy, dc) for dx in (0, 1) for dy in (0, 1) for dc in (0, 1)][1:]


def _peer(pos, rel):
    return tuple(jnp.where(r == 1, 1 - p, p) if r else p for p, r in zip(pos, rel))


def _block_index(pos):
    return 4 * pos[0] + 2 * pos[1] + pos[2]


_HBM = pl.BlockSpec(memory_space=pltpu.HBM)


def _all_gather(xs, name):
    n = len(xs)

    def body(*refs):
        x_refs, out_refs, (send_sems, recv_sems, local_sems) = refs[:n], refs[n:2 * n], refs[2 * n:]
        x_, y_, c_ = _mesh_pos()
        me, sibling = (x_, y_, c_), (x_, y_, 1 - c_)
        chips = [(1 - x_, y_), (x_, 1 - y_), (1 - x_, 1 - y_)]

        def copy(t, k, block, to, own=False):
            slot = out_refs[t].at[_block_index(block)]
            return pltpu.make_async_remote_copy(
                src_ref=x_refs[t] if own else slot, dst_ref=slot, send_sem=send_sems.at[7 * t + k], recv_sem=recv_sems.at[7 * t + k],
                device_id=to, device_id_type=pl.DeviceIdType.MESH)

        mine = [pltpu.make_async_copy(x_refs[t], out_refs[t].at[_block_index(me)], local_sems.at[t]) for t in range(n)]
        first = [[copy(t, 0, me, sibling, own=True)] + [copy(t, 1 + j, me, (*chip, c_), own=True) for j, chip in enumerate(chips)]
                 for t in range(n)]
        passed = [[copy(t, 4 + j, (*chip, c_), sibling) for j, chip in enumerate(chips)] for t in range(n)]
        for t in range(n):
            mine[t].start()
            for cp in first[t]:
                cp.start()
        for t in range(n):
            for j, chip in enumerate(chips):
                copy(t, 1 + j, (*chip, c_), me).wait_recv()
                passed[t][j].start()
        for t in range(n):
            copy(t, 0, sibling, me).wait_recv()
            for j, chip in enumerate(chips):
                copy(t, 4 + j, (*chip, 1 - c_), me).wait_recv()
            for cp in first[t] + passed[t]:
                cp.wait_send()
            mine[t].wait()

    return _pcall(
        body, name=name, in_specs=[_HBM] * n, out_specs=[_HBM] * n,
        out_shape=[jax.ShapeDtypeStruct((N_DEV,) + x.shape, x.dtype) for x in xs],
        scratch_shapes=[pltpu.SemaphoreType.DMA((7 * n,)), pltpu.SemaphoreType.DMA((7 * n,)), pltpu.SemaphoreType.DMA((n,))],
    )(*xs)


_SEM = pl.BlockSpec(memory_space=pltpu.SEMAPHORE)
_EFFECT = pltpu.SideEffectType.DATAFLOW_SIDE_EFFECTING


def _push_start(srcs, after, name):
    n = len(srcs)
    lands = [lax.empty((N_DEV,) + s.shape[-2:], s.dtype) for s in srcs]

    def body(*refs):
        src_refs, land_refs = refs[:n], refs[n:2 * n]
        send_sems, recv_sems, token = refs[2 * n + 1:3 * n + 1], refs[3 * n + 1:4 * n + 1], refs[-1]
        me = _mesh_pos()
        for t in range(n):
            for rel in _RELATIONS:
                peer = _peer(me, rel)
                pltpu.make_async_remote_copy(
                    src_ref=src_refs[t].at[_block_index(peer)] if srcs[t].ndim == 3 else src_refs[t], dst_ref=land_refs[t].at[_block_index(me)],
                    send_sem=send_sems[t], recv_sem=recv_sems[t], device_id=peer, device_id_type=pl.DeviceIdType.MESH).start()
        token[...] = jnp.zeros_like(token)

    outs = _pcall(
        body, name=name,
        out_shape=[pltpu.SemaphoreType.DMA(())] * (2 * n) + [pltpu.HBM(s.shape, s.dtype) for s in srcs]
        + [pltpu.HBM(l.shape, l.dtype) for l in lands] + [jax.ShapeDtypeStruct((8, 128), F32)],
        in_specs=[_HBM] * (2 * n) + [pl.BlockSpec(memory_space=pl.ANY)],
        out_specs=[_SEM] * (2 * n) + [_HBM] * (2 * n) + [pl.BlockSpec(memory_space=pltpu.VMEM)],
        input_output_aliases={i: 2 * n + i for i in range(2 * n)}, compiler_params=pltpu.CompilerParams(has_side_effects=_EFFECT),
    )(*[pltpu.with_memory_space_constraint(t, pltpu.HBM) for t in list(srcs) + lands], after)
    return (outs[:n], outs[n:2 * n], outs[2 * n:3 * n], outs[3 * n:4 * n]), outs[-1]


def _push_wait(handle, after, owns, me, name):
    send_sems, recv_sems, src_thrus, land_thrus = handle
    n = len(land_thrus)

    def body(*refs):
        land_refs, sends, recvs = refs[n:2 * n], refs[2 * n:3 * n], refs[3 * n:4 * n]
        for t in range(n):
            seven = land_refs[t].at[pl.ds(0, N_DEV - 1)]
            all_seven = pltpu.make_async_remote_copy(src_ref=seven, dst_ref=seven, send_sem=sends[t], recv_sem=recvs[t],
                                                     device_id=_mesh_pos(), device_id_type=pl.DeviceIdType.MESH)
            all_seven.wait_send()
            all_seven.wait_recv()

    outs = _pcall(
        body, name=name,
        out_shape=[pltpu.HBM(t.shape, t.dtype) for t in list(src_thrus) + list(land_thrus)],
        in_specs=[_HBM] * (2 * n) + [_SEM] * (2 * n) + [pl.BlockSpec(memory_space=pl.ANY)], out_specs=[_HBM] * (2 * n),
        input_output_aliases={i: i for i in range(2 * n)}, compiler_params=pltpu.CompilerParams(has_side_effects=_EFFECT),
    )(*src_thrus, *land_thrus, *send_sems, *recv_sems, after)
    return [lax.dynamic_update_slice(land, own[None], (me, 0, 0)) for land, own in zip(outs[n:], owns)]


def _sum_blocks(p, name):
    n, rows, cols = p.shape
    tr = _pick(rows, (256, 128, 64, 32, 16, 8))

    def body(p_ref, o_ref):
        acc = p_ref[0].astype(F32)
        for s in range(1, n):
            acc = acc + p_ref[s].astype(F32)
        o_ref[...] = acc

    return _pcall(
        body, name=name, grid=(rows // tr,),
        in_specs=[pl.BlockSpec((n, tr, cols), lambda i: (0, i, 0))], out_specs=pl.BlockSpec((tr, cols), lambda i: (i, 0)),
        out_shape=jax.ShapeDtypeStruct((rows, cols), F32), compiler_params=_params(("parallel",)),
    )(p)


def _sum_rows(t, name):
    def body(t_ref, o_ref):
        o_ref[...] = jnp.sum(t_ref[...], axis=0, keepdims=True)

    return _pcall(body, name=name, out_shape=jax.ShapeDtypeStruct((1, t.shape[1]), F32))(t)


class _Packing:
    def __init__(self, sizes, align):
        self.offsets, self.sizes, self.align = {}, dict(sizes), align
        row = 0
        for name, size in sizes:
            self.offsets[name] = row
            row += -(-size // (align * PACK_COLS)) * align
        self.rows = row

    def pack(self, pieces):
        return self.pack_blocks({n: pieces[n].reshape(1, -1) for n in self.sizes})[0]

    def pack_blocks(self, pieces):
        out = []
        for n, size in self.sizes.items():
            padded = -(-size // (self.align * PACK_COLS)) * self.align * PACK_COLS
            out.append(jnp.pad(pieces[n], ((0, 0), (0, padded - size))).reshape(pieces[n].shape[0], -1, PACK_COLS))
        return jnp.concatenate(out, axis=1)

    def piece(self, packed, name, lead=()):
        start, size = self.offsets[name], self.sizes[name]
        nrow = -(-size // (self.align * PACK_COLS)) * self.align
        sl = packed[..., start:start + nrow, :]
        return sl.reshape(lead + (nrow * PACK_COLS,))[..., :size]


_PIECES = (("ev_w_in", "ev_w_in", 0, 1, "a"), ("w_out0", "w_out", 0, 0, "a"),
           ("mlp_w1_0", "mlp_w1", 0, 1, "b"), ("mlp_w2_0", "mlp_w2", 0, 0, "b"),
           ("od_w_in", "od_w_in", 0, 1, "c"), ("od_w_uq", "od_w_uq", 0, 1, "c"), ("od_w_ukv", "od_w_ukv", 0, 1, "c"),
           ("w_out1", "w_out", 1, 0, "c"), ("mlp_w1_1", "mlp_w1", 1, 1, "c"), ("mlp_w2_1", "mlp_w2", 1, 0, "c"))
_SMALL_SHARDED = (("od_q_norm_g", 1), ("od_conv_w", 2), ("od_conv_b", 1), ("od_ln_g", 1), ("od_ln_b", 1))
_REPLICATED = ("c_ctx", "norm1_g", "norm2_g", "ev_q_norm_g", "ev_k_norm_g", "ev_sgu_norm_g", "ev_sgu_w", "ev_sgu_b",
               "od_kv_norm_g", "final_g")


def _unshard(blocks, axis):
    moved = jnp.moveaxis(blocks, 0, axis)
    shape = moved.shape
    return moved.reshape(shape[:axis] + (shape[axis] * shape[axis + 1],) + shape[axis + 2:])


def _group_mean_matrix(width, group):
    idx = jnp.arange(width) // group
    return (idx[:, None] == idx[None, :]).astype(F32) / group


def _angles(length, d_rot):
    rows = length // GRID_W
    row = jnp.broadcast_to(jnp.arange(rows)[:, None], (rows, GRID_W)).reshape(-1).astype(F32)
    col = jnp.broadcast_to(jnp.arange(GRID_W)[None, :], (rows, GRID_W)).reshape(-1).astype(F32)
    d_axis = d_rot // 2
    inv = ROPE_THETA ** (-jnp.arange(0, d_axis, 2, dtype=F32) / d_axis)
    return jnp.concatenate([row[:, None] * inv, col[:, None] * inv], axis=-1)


def _rope_tables(length, d_rot, head_dim, heads, ctx_len, tail=0):
    ang = _angles(length, d_rot)
    cos = jnp.repeat(jnp.cos(ang), 2, axis=1)
    sin = jnp.repeat(jnp.sin(ang), 2, axis=1) * jnp.tile(jnp.array([-1.0, 1.0], F32), d_rot // 2)
    keep = head_dim - d_rot - tail
    cos = jnp.concatenate([jnp.ones((length, keep), F32), cos, jnp.ones((length, tail), F32)], axis=1)
    sin = jnp.concatenate([jnp.zeros((length, keep), F32), sin, jnp.zeros((length, tail), F32)], axis=1)
    cos, sin = jnp.tile(cos, (1, heads)), jnp.tile(sin, (1, heads))
    cos = jnp.concatenate([jnp.ones((ctx_len, cos.shape[1]), F32), cos], axis=0)
    sin = jnp.concatenate([jnp.zeros((ctx_len, sin.shape[1]), F32), sin], axis=0)
    return cos, sin


def _segment_params(mod, bsz):
    parts = jnp.split(mod, N_MOD, axis=-1)
    out = []
    for part in parts:
        lat = part[:bsz]
        ctx = jnp.broadcast_to(part[bsz:bsz + 1], lat.shape)
        out.append(jnp.stack([ctx, lat], axis=1)[:, :, None, :])
    return out


def _flat(t):
    return t.reshape(-1, t.shape[-1])


def _sequence_rowwise(ctx_len):
    tl = math.gcd(256, ctx_len)

    def make(name, f, out_specs, rows_per_block=tl, ctx_blocks=ctx_len // tl):
        return _rowwise(name, f, out_specs, rows_per_block, ctx_blocks)

    return make


def _mixer0(xall, modrows0, w, bsz, length, ctx_len):
    d = xall.shape[-1]
    total = ctx_len + length
    rowwise, flat = _sequence_rowwise(ctx_len), _flat
    sh1, sc1, g1, sh2, sc2, _ = _segment_params(modrows0, bsz)
    (h,) = rowwise("mod0", _f_modulate, [(d, BF16)])((xall,), (), (sh1, sc1), (w["norm1_g0"],), ())
    ev_q, ev_kv = A_Q_HEADS * A_HEAD_DIM, A_KV_HEADS * A_HEAD_DIM
    half = B_GROUPS * B_GROUP_DIM
    groups = tuple(jnp.split(w["ev_w_in"], [ev_q, ev_q + ev_kv, ev_q + 2 * ev_kv, ev_q + 2 * ev_kv + half], axis=0))
    qp, kp, vp, zu, zv = [t.reshape(bsz, total, -1) for t in _linear_multi("ev_in", (F32, F32, BF16, F32, F32))(flat(h), groups)]
    cos_q, sin_q = _rope_tables(length, A_HEAD_DIM, A_HEAD_DIM, A_Q_HEADS, ctx_len)
    cos_k, sin_k = cos_q[:, :ev_kv], sin_q[:, :ev_kv]
    (q,) = rowwise("ev_q", _f_headnorm_rope, [(ev_q, BF16)])(
        (qp,), (cos_q, sin_q), (), (jnp.tile(w["ev_q_norm_g"][0], A_Q_HEADS)[None],), (_group_mean_matrix(ev_q, A_HEAD_DIM),))
    (k,) = rowwise("ev_k", _f_headnorm_rope, [(ev_kv, BF16)])(
        (kp,), (cos_k, sin_k), (), (jnp.tile(w["ev_k_norm_g"][0], A_KV_HEADS)[None],), (_group_mean_matrix(ev_kv, A_HEAD_DIM),))
    o_att = _attention_rows("gqa", ctx_len, A_Q_HEADS, A_KV_HEADS)(q, k, vp)
    u, vn = rowwise("sgu_pre", _f_sgu_pre, [(half, F32), (half, BF16)])(
        (zu, zv), (), (), (w["ev_sgu_norm_g"][0].reshape(1, half),), (_group_mean_matrix(half, B_GROUP_DIM),))
    bias = jnp.repeat(w["ev_sgu_b"][0].T, B_GROUP_DIM, axis=1)
    (o_sgu,) = rowwise("sgu_mix", _f_sgu_mix, [(half, BF16)], rows_per_block=B_CHUNK, ctx_blocks=0)(
        (u, vn), (), (), tuple(w["ev_sgu_w"][0][g] for g in range(B_GROUPS)) + (bias,), ())
    y = _linear_sum("out0")((flat(o_att), flat(o_sgu)), tuple(jnp.split(w["w_out0"], 2, axis=0))).reshape(bsz, total, d)
    x1, h = rowwise("res_mod0a", _f_res_modulate, [(d, F32), (d, BF16)])((xall, y), (), (g1, sh2, sc2), (w["norm2_g0"],), ())
    return x1, h


def _mlp0(x1, h, modrows0, modrows1, w, bsz, length, ctx_len):
    d = x1.shape[-1]
    total = ctx_len + length
    g2 = _segment_params(modrows0, bsz)[5]
    sh1, sc1 = _segment_params(modrows1, bsz)[:2]
    y = _mlp("mlp0")(_flat(h), w["mlp_w1_0"], w["mlp_w2_0"]).reshape(bsz, total, d)
    return _sequence_rowwise(ctx_len)("res_mod0b", _f_res_modulate, [(d, F32), (d, BF16)])((x1, y), (), (g2, sh1, sc1), (w["norm1_g1"],), ())


def _layer1(x2, h, modrows1, w, bsz, length, ctx_len):
    d = x2.shape[-1]
    total = ctx_len + length
    half = B_GROUPS * B_GROUP_DIM
    rowwise, flat = _sequence_rowwise(ctx_len), _flat
    _, _, g1n, sh2n, sc2n, g2n = _segment_params(modrows1, bsz)
    g_cq, g_ckv, g_kr, g_za, g_zg = jnp.split(w["od_w_in"], [C_Q_RANK, C_Q_RANK + C_KV_RANK, C_Q_RANK + C_KV_RANK + C_ROPE,
                                                             C_Q_RANK + C_KV_RANK + C_ROPE + half], axis=0)
    lanes, c_qk = 128, C_NOPE + C_ROPE
    g_kr = jnp.pad(g_kr, ((C_NOPE, lanes - c_qk), (0, 0)))
    cq, ckv, kr, za, zg = [t.reshape(bsz, total, -1) for t in _linear_multi("od_in", (F32,) * 5)(flat(h), (g_cq, g_ckv, g_kr, g_za, g_zg))]
    lat = slice(ctx_len, total)
    lat_tl = math.gcd(256, length)
    (cqn,) = _rowwise("od_qn", _f_rms, [(C_Q_RANK, BF16)], lat_tl)((cq[:, lat],), (), (), (w["od_q_norm_g"],), ())
    w_uq = jnp.pad(w["od_w_uq"].reshape(C_HEADS, c_qk, C_Q_RANK), ((0, 0), (0, lanes - c_qk), (0, 0))).reshape(C_HEADS * lanes, C_Q_RANK)
    qf = _linear("od_uq")(flat(cqn), w_uq).reshape(bsz, length, C_HEADS * lanes)
    cos_q, sin_q = _rope_tables(length, C_ROPE, lanes, C_HEADS, 0, tail=lanes - c_qk)
    (q,) = _rowwise("od_qrope", _f_rope, [(C_HEADS * lanes, BF16)], lat_tl)((qf,), (cos_q, sin_q), (), (), ())
    (ckvn,) = rowwise("od_kvn", _f_rms, [(C_KV_RANK, BF16)])((ckv,), (), (), (w["od_kv_norm_g"],), ())
    per_head = w["od_w_ukv"].reshape(C_HEADS, C_NOPE + C_V, C_KV_RANK)
    w_kn = jnp.pad(per_head[:, :C_NOPE], ((0, 0), (0, lanes - C_NOPE), (0, 0))).reshape(C_HEADS * lanes, C_KV_RANK)
    w_v = jnp.pad(per_head[:, C_NOPE:], ((0, 0), (0, lanes - C_V), (0, 0))).reshape(C_HEADS * lanes, C_KV_RANK)
    kn, vv = [t.reshape(bsz, total, -1) for t in _linear_multi("od_ukv", (BF16, BF16))(flat(ckvn), (w_kn, w_v))]
    cos_r, sin_r = _rope_tables(length, C_ROPE, lanes, 1, ctx_len, tail=lanes - c_qk)
    (krr,) = rowwise("od_krope", _f_rope, [(lanes, BF16)])((kr,), (cos_r, sin_r), (), (), ())
    o_att = _attention_latent_rows("mla", C_HEADS, C_NOPE, C_ROPE, C_V, 2)(q, kn, krr, vv)
    (glu,) = _rowwise("glu", _f_glu, [(half, F32)], lat_tl)((za[:, lat], zg[:, lat]), (), (), (), ())
    taps = jnp.concatenate([w["od_conv_w"][0], w["od_conv_b"]], axis=0)
    conv = _dwconv(glu, taps)
    (o_conv,) = _rowwise("ln_silu", _f_ln_silu, [(half, BF16)], lat_tl)((conv,), (), (), (w["od_ln_g"], w["od_ln_b"]), ())
    y = _linear_sum("out1")((flat(o_att), flat(o_conv)), tuple(jnp.split(w["w_out1"], 2, axis=0))).reshape(bsz, length, d)
    lat_param = lambda p: p[:, 1:]
    x3, h = _rowwise("res_mod1a", _f_res_modulate, [(d, F32), (d, BF16)], lat_tl)(
        (x2[:, lat], y), (), (lat_param(g1n), lat_param(sh2n), lat_param(sc2n)), (w["norm2_g1"],), ())
    y = _mlp("mlp1")(flat(h), w["mlp_w1_1"], w["mlp_w2_1"]).reshape(bsz, length, d)
    (x4,) = _rowwise("res1b", _f_res, [(d, F32)], lat_tl)((x3, y), (), (lat_param(g2n),), (), ())
    return x4


def kernel(x, c, ctx, c_ctx, ada_w, ada_b, norm1_g, norm2_g, w_out, mlp_w1, mlp_w2, ev_w_in, ev_q_norm_g, ev_k_norm_g, ev_sgu_norm_g, ev_sgu_w, ev_sgu_b, od_w_in, od_q_norm_g, od_kv_norm_g, od_w_uq, od_w_ukv, od_conv_w, od_conv_b, od_ln_g, od_ln_b, final_g, loss_target, m_c_ctx, m_ada_w, m_ada_b, m_norm1_g, m_norm2_g, m_w_out, m_mlp_w1, m_mlp_w2, m_ev_w_in, m_ev_q_norm_g, m_ev_k_norm_g, m_ev_sgu_norm_g, m_ev_sgu_w, m_ev_sgu_b, m_od_w_in, m_od_q_norm_g, m_od_kv_norm_g, m_od_w_uq, m_od_w_ukv, m_od_conv_w, m_od_conv_b, m_od_ln_g, m_od_ln_b, m_final_g, v_c_ctx, v_ada_w, v_ada_b, v_norm1_g, v_norm2_g, v_w_out, v_mlp_w1, v_mlp_w2, v_ev_w_in, v_ev_q_norm_g, v_ev_k_norm_g, v_ev_sgu_norm_g, v_ev_sgu_w, v_ev_sgu_b, v_od_w_in, v_od_q_norm_g, v_od_kv_norm_g, v_od_w_uq, v_od_w_ukv, v_od_conv_w, v_od_conv_b, v_od_ln_g, v_od_ln_b, v_final_g):
    names = ["c_ctx", "ada_w", "ada_b", "norm1_g", "norm2_g", "w_out", "mlp_w1", "mlp_w2", "ev_w_in", "ev_q_norm_g", "ev_k_norm_g",
             "ev_sgu_norm_g", "ev_sgu_w", "ev_sgu_b", "od_w_in", "od_q_norm_g", "od_kv_norm_g", "od_w_uq", "od_w_ukv", "od_conv_w",
             "od_conv_b", "od_ln_g", "od_ln_b", "final_g"]
    local = dict(zip(names, [c_ctx, ada_w, ada_b, norm1_g, norm2_g, w_out, mlp_w1, mlp_w2, ev_w_in, ev_q_norm_g, ev_k_norm_g, ev_sgu_norm_g, ev_sgu_w, ev_sgu_b, od_w_in, od_q_norm_g, od_kv_norm_g, od_w_uq, od_w_ukv, od_conv_w, od_conv_b, od_ln_g, od_ln_b, final_g]))
    mom1 = dict(zip(names, [m_c_ctx, m_ada_w, m_ada_b, m_norm1_g, m_norm2_g, m_w_out, m_mlp_w1, m_mlp_w2, m_ev_w_in, m_ev_q_norm_g, m_ev_k_norm_g, m_ev_sgu_norm_g, m_ev_sgu_w, m_ev_sgu_b, m_od_w_in, m_od_q_norm_g, m_od_kv_norm_g, m_od_w_uq, m_od_w_ukv, m_od_conv_w, m_od_conv_b, m_od_ln_g, m_od_ln_b, m_final_g]))
    mom2 = dict(zip(names, [v_c_ctx, v_ada_w, v_ada_b, v_norm1_g, v_norm2_g, v_w_out, v_mlp_w1, v_mlp_w2, v_ev_w_in, v_ev_q_norm_g, v_ev_k_norm_g, v_ev_sgu_norm_g, v_ev_sgu_w, v_ev_sgu_b, v_od_w_in, v_od_q_norm_g, v_od_kv_norm_g, v_od_w_uq, v_od_w_ukv, v_od_conv_w, v_od_conv_b, v_od_ln_g, v_od_ln_b, v_final_g]))
    bsz, length, d = x.shape
    ctx_len = ctx.shape[1]
    me = _block_index(_mesh_pos())

    shard = {p: local[wn][layer] for p, wn, layer, _, _ in _PIECES}
    by_columns = {p: axis == 1 for p, _, _, axis, _ in _PIECES}
    stages = {s: [p for p, _, _, _, st in _PIECES if st == s] for s in "abc"}
    block_rows = {p: shard[p].shape[1] if by_columns[p] else shard[p].shape[0] for p in shard}

    def pad_block_rows(t, p):
        extra = -block_rows[p] % ROW_ALIGN
        return jnp.pad(t, [(0, 0)] * (t.ndim - 2) + [(0, extra), (0, 0)]) if extra else t

    def travelling(p):
        t = shard[p].astype(BF16)
        return pad_block_rows(t.T if by_columns[p] else t, p)

    mine = {s: [travelling(p) for p in stages[s]] for s in "abc"}
    tiny_pack = _Packing([(n, local[n].size) for n, _ in _SMALL_SHARDED], 8)

    def unpack_weights(s, gathered):
        return {p: g[:, :block_rows[p]].reshape(N_DEV * block_rows[p], g.shape[2]) for p, g in zip(stages[s], gathered)}

    def pack_grads(s, g):
        return [pad_block_rows(g[p].reshape(N_DEV, block_rows[p], g[p].shape[1]), p) for p in stages[s]]

    cond_local = jnp.concatenate([c, c_ctx[None], jnp.zeros((COND_ROWS - bsz - 1, d), F32)], axis=0)
    cond, gathered_tiny, *gathered_a = _all_gather(
        [cond_local, tiny_pack.pack({n: local[n] for n, _ in _SMALL_SHARDED})] + mine["a"], "gather_inputs")
    cond = cond.reshape(N_DEV * COND_ROWS, d)
    silu_op = _rowwise("silu", _f_silu, [(d, F32)], N_DEV * COND_ROWS)
    silu_rows, silu_pullback = jax.vjp(lambda r: silu_op((r[None],), (), (), (), ())[0][0], cond)
    mod_cols = ada_w.shape[2]
    mod_part = jnp.concatenate([_mm(silu_rows, ada_w[i], "nn", (F32,), f"ada{i}_fwd")[0] for i in range(2)], axis=0)
    (mod_all,) = _all_gather([mod_part], "gather_mod")
    mod_all = mod_all.reshape(N_DEV, 2, N_DEV * COND_ROWS, mod_cols)
    modrows = []
    for i in range(2):
        whole = mod_all[:, i].transpose(1, 0, 2).reshape(N_DEV * COND_ROWS, N_DEV * mod_cols) + ada_b[i]
        modrows.append(lax.dynamic_slice_in_dim(whole, me * COND_ROWS, COND_ROWS, axis=0)[:bsz + 1])

    weights_a = unpack_weights("a", gathered_a)
    gather_b, token_b = _push_start(mine["b"], mod_all, "gather_weights_b_start")
    gather_c, token_c = _push_start(mine["c"], token_b, "gather_weights_c_start")
    full = {n: local[n] for n in _REPLICATED}
    for n, axis in _SMALL_SHARDED:
        full[n] = _unshard(tiny_pack.piece(gathered_tiny, n, (N_DEV,)).reshape((N_DEV,) + local[n].shape), axis)

    xall = jnp.concatenate([ctx, x], axis=1)
    modrows0, modrows1 = modrows[0] + token_c[0, 0], modrows[1]
    w_a = dict(weights_a, norm1_g0=norm1_g[0][None], norm2_g0=norm2_g[0][None],
               **{n: full[n] for n in ("ev_q_norm_g", "ev_k_norm_g", "ev_sgu_norm_g", "ev_sgu_w", "ev_sgu_b")})
    (x1, h0), pull_a = jax.vjp(lambda x_, m0, w: _mixer0(x_, m0, w, bsz, length, ctx_len), xall, modrows0, w_a)
    w_b = dict(unpack_weights("b", _push_wait(gather_b, x1, mine["b"], me, "gather_weights_b_wait")), norm1_g1=norm1_g[1][None])
    (x2, h1), pull_b = jax.vjp(lambda x_, h_, m0, m1, w: _mlp0(x_, h_, m0, m1, w, bsz, length, ctx_len), x1, h0, modrows0, modrows1, w_b)
    w_c = dict(unpack_weights("c", _push_wait(gather_c, x2, mine["c"], me, "gather_weights_c_wait")), norm2_g1=norm2_g[1][None],
               **{n: full[n] for n in ("od_q_norm_g", "od_kv_norm_g", "od_conv_w", "od_conv_b", "od_ln_g", "od_ln_b")})
    x4, pull_c = jax.vjp(lambda x_, h_, m1, w: _layer1(x_, h_, m1, w, bsz, length, ctx_len), x2, h1, modrows1, w_c)
    loss_part, dx4, dfinal = _loss_head(x4, loss_target, final_g[None])
    loss = lax.psum(loss_part[0, 0], MESH_AXES)

    dx2, dh1, dmod1_c, g_c = pull_c(dx4)
    grads_c = pack_grads("c", g_c)
    exchange_c, token = _push_start(grads_c, dx2, "exchange_grads_c_start")
    dx1, dh0, dmod0_b, dmod1_b, g_b = pull_b((dx2, dh1 + token[0, 0].astype(dh1.dtype)))
    grads_b = pack_grads("b", g_b)
    exchange_b, token = _push_start(grads_b, dx1, "exchange_grads_b_start")
    dxall, dmod0_a, g_a = pull_a((dx1, dh0 + token[0, 0].astype(dh0.dtype)))
    grad_x = dxall[:, ctx_len:]
    dmodrows = [dmod0_a + dmod0_b, dmod1_b + dmod1_c]
    grads = {n: g[n] for g in (g_a, g_c) for n in g if n in full}
    grads["norm1_g"] = jnp.concatenate([g_a["norm1_g0"], g_b["norm1_g1"]], axis=0)
    grads["norm2_g"] = jnp.concatenate([g_a["norm2_g0"], g_c["norm2_g1"]], axis=0)
    grads["final_g"] = dfinal[0]

    dmod_local = jnp.concatenate([jnp.pad(dm, ((0, COND_ROWS - bsz - 1), (0, 0))) for dm in dmodrows], axis=0)
    (dmod_all,) = _all_gather([dmod_local], "gather_dmod")
    dmod_all = dmod_all.reshape(N_DEV, 2, COND_ROWS, N_DEV * mod_cols)
    grads_a = pack_grads("a", g_a)
    exchange_a, token = _push_start(grads_a, dmod_all, "exchange_grads_a_start")
    dmod_all = dmod_all + token[0, 0]
    reduced = {}
    grad_ada_w, grad_ada_b, dmod_mine = [], [], []
    for i in range(2):
        dmod = dmod_all[:, i].reshape(N_DEV * COND_ROWS, N_DEV * mod_cols)
        grad_ada_b.append(_sum_rows(dmod, f"ada{i}_db")[0])
        dmod_mine.append(lax.dynamic_slice_in_dim(dmod, me * mod_cols, mod_cols, axis=1))
        grad_ada_w.append(_mm(silu_rows, dmod_mine[i], "tn", (F32,), f"ada{i}_dw")[0])
    reduced["ada_w"], reduced["ada_b"] = jnp.stack(grad_ada_w), jnp.stack(grad_ada_b)
    dsilu = _mm(jnp.concatenate(dmod_mine, axis=1), jnp.concatenate([ada_w[0], ada_w[1]], axis=1), "nt", (F32,), "ada_dx")[0]
    (dcond,) = silu_pullback(dsilu)
    grads["c_ctx"] = _sum_rows(dcond.reshape(N_DEV, COND_ROWS, d)[:, bsz], "c_ctx_rows")[0]

    small_names = list(_REPLICATED) + [n for n, _ in _SMALL_SHARDED]
    small_pack = _Packing([(n, full[n].size) for n in small_names], 8)
    (small_all,) = _all_gather([small_pack.pack({n: grads[n].astype(F32) for n in small_names})], "gather_small_grads")
    small_sum = _sum_blocks(small_all, "sum_small_grads")
    for n in _REPLICATED:
        reduced[n] = small_pack.piece(small_sum, n).reshape(local[n].shape)
    for n, axis in _SMALL_SHARDED:
        whole = small_pack.piece(small_sum, n).reshape(full[n].shape)
        reduced[n] = lax.dynamic_slice_in_dim(whole, me * local[n].shape[axis], local[n].shape[axis], axis=axis)

    def own_blocks(blocks):
        return [lax.dynamic_index_in_dim(t, me, 0, keepdims=False) for t in blocks]

    received = {s: _push_wait(handle, small_sum, own_blocks(blocks), me, f"exchange_grads_{s}_wait")
                for s, handle, blocks in (("c", exchange_c, grads_c), ("b", exchange_b, grads_b), ("a", exchange_a, grads_a))}
    piece_grad = {}
    for s in "abc":
        for p, blocks in zip(stages[s], received[s]):
            summed = _sum_blocks(blocks, "sum_grads_" + p)[:block_rows[p]]
            piece_grad[p] = summed.T if by_columns[p] else summed
    for n in ("w_out", "mlp_w1", "mlp_w2"):
        reduced[n] = jnp.stack([piece_grad[p] for p, wn, _, _, _ in _PIECES if wn == n])
    for n in ("ev_w_in", "od_w_in", "od_w_uq", "od_w_ukv"):
        reduced[n] = piece_grad[n][None]

    delta, new_m, new_v = {}, {}, {}
    for n in names:
        delta[n], new_m[n], new_v[n] = _adamw(local[n], reduced[n], mom1[n], mom2[n], "adamw_" + n)
    return (loss, grad_x, *[reduced[n] for n in names], *[delta[n] for n in names], *[new_m[n] for n in names], *[new_v[n] for n in names])
```

```python
import functools
import math

import jax
import jax.numpy as jnp
from jax import lax
from jax.experimental import pallas as pl
from jax.experimental.pallas import tpu as pltpu

F32, BF16 = jnp.float32, jnp.bfloat16

EPS = 1e-6
GRID_W = 64
ROPE_THETA = 10000.0
A_HEAD_DIM, A_Q_HEADS, A_KV_HEADS = 64, 8, 2
B_GROUPS, B_GROUP_DIM, B_CHUNK = 8, 64, 128
C_HEADS, C_NOPE, C_ROPE, C_V, C_Q_RANK, C_KV_RANK = 8, 64, 32, 64, 256, 128
D_CONV = 31
CONV_PAD = D_CONV // 2
N_MOD = 6
N_DEV = 8
MESH_AXES = ("x", "y", "c")

ADAM_LR, ADAM_B1, ADAM_B2, ADAM_EPS, ADAM_WD, ADAM_STEP = 0.001, 0.9, 0.999, 1e-08, 0.01, 10

VMEM_LIMIT = 56 * 1024 * 1024
PACK_COLS = 1024
ROW_ALIGN = 16
SUBLANES = 8
CONV_ROWS = 32
COND_ROWS = 8


def _pcall(body, **kw):
    return pl.pallas_call(body, **kw)


def _params(sem=None):
    return pltpu.CompilerParams(dimension_semantics=sem, vmem_limit_bytes=VMEM_LIMIT)


def _pick(n, cands):
    for c in cands:
        if n % c == 0:
            return c
    return n


def _mm(a, b, mode, out_dtypes, name, epi=None, extras=()):
    if mode == "tn":
        kk, m = a.shape
    else:
        m, kk = a.shape
    n = b.shape[0] if mode == "nt" else b.shape[1]
    tm = _pick(m, (1152, 1024, 896, 768, 512, 256, 128))
    tn = _pick(n, (1024, 896, 768, 512, 256, 128))
    tk = kk if kk <= 1024 else _pick(kk, (1024, 896, 768, 512, 256, 128))
    nk = kk // tk
    ne, no = len(extras), len(out_dtypes)
    a_spec = pl.BlockSpec((tk, tm), lambda i, j, k: (k, i)) if mode == "tn" else pl.BlockSpec((tm, tk), lambda i, j, k: (i, k))
    b_spec = pl.BlockSpec((tn, tk), lambda i, j, k: (j, k)) if mode == "nt" else pl.BlockSpec((tk, tn), lambda i, j, k: (k, j))
    t_spec = pl.BlockSpec((tm, tn), lambda i, j, k: (i, j))
    dn = {"nn": ((1,), (0,)), "nt": ((1,), (1,)), "tn": ((0,), (0,))}[mode]

    def body(a_ref, b_ref, *rest):
        extra_refs, out_refs = rest[:ne], rest[ne:ne + no]

        def finish(acc):
            outs = (acc,) if epi is None else epi(acc, *[r[...] for r in extra_refs])
            for r, o in zip(out_refs, outs):
                r[...] = o.astype(r.dtype)

        part = lax.dot_general(a_ref[...].astype(BF16), b_ref[...].astype(BF16), (dn, ((), ())), preferred_element_type=F32)
        if nk == 1:
            finish(part)
        else:
            acc_ref = rest[-1]
            k = pl.program_id(2)

            @pl.when(k == 0)
            def _():
                acc_ref[...] = part

            @pl.when(k > 0)
            def _():
                acc_ref[...] += part

            @pl.when(k == nk - 1)
            def _():
                finish(acc_ref[...])

    outs = _pcall(
        body, name=name, grid=(m // tm, n // tn, nk),
        in_specs=[a_spec, b_spec] + [t_spec] * ne,
        out_specs=[t_spec] * no,
        out_shape=[jax.ShapeDtypeStruct((m, n), d) for d in out_dtypes],
        scratch_shapes=[pltpu.VMEM((tm, tn), F32)] if nk > 1 else [],
        compiler_params=_params(("parallel", "parallel", "arbitrary")),
    )(a, b, *extras)
    return outs


def _linear(name, out_dtype=F32):
    @jax.custom_vjp
    def op(x, wt):
        return _mm(x, wt, "nt", (out_dtype,), name + "_fwd")[0]

    def fwd(x, wt):
        return op(x, wt), (x, wt)

    def bwd(res, dy):
        x, wt = res
        return _mm(dy, wt, "nn", (x.dtype,), name + "_dx")[0], _mm(dy, x, "tn", (wt.dtype,), name + "_dw")[0]

    op.defvjp(fwd, bwd)
    return op


_NT, _NN, _TN = (((1,), (1,)), ((), ())), (((1,), (0,)), ((), ())), (((0,), (0,)), ((), ()))
_ROW_TILES = (1152, 1024, 768, 512, 256, 128)


def _whole(w):
    return pl.BlockSpec(w.shape, lambda i: (0, 0))


def _groups_apply(x, ws, out_dtypes, name):
    n, (m, kk) = len(ws), x.shape
    tm = _pick(m, _ROW_TILES)

    def body(x_ref, *refs):
        a = x_ref[...].astype(BF16)
        for w_ref, o_ref in zip(refs[:n], refs[n:]):
            o_ref[...] = lax.dot_general(a, w_ref[...], _NT, preferred_element_type=F32).astype(o_ref.dtype)

    return _pcall(
        body, name=name, grid=(m // tm,),
        in_specs=[pl.BlockSpec((tm, kk), lambda i: (i, 0))] + [_whole(w) for w in ws],
        out_specs=[pl.BlockSpec((tm, w.shape[0]), lambda i: (i, 0)) for w in ws],
        out_shape=[jax.ShapeDtypeStruct((m, w.shape[0]), dt) for w, dt in zip(ws, out_dtypes)],
        compiler_params=_params(("parallel",)),
    )(x, *ws)


def _groups_sum(xs, ws, out_dtype, name):
    n, m, kk = len(ws), xs[0].shape[0], ws[0].shape[1]
    tm = _pick(m, _ROW_TILES)

    def body(*refs):
        acc = None
        for x_ref, w_ref in zip(refs[:n], refs[n:2 * n]):
            part = lax.dot_general(x_ref[...].astype(BF16), w_ref[...], _NN, preferred_element_type=F32)
            acc = part if acc is None else acc + part
        refs[2 * n][...] = acc.astype(out_dtype)

    return _pcall(
        body, name=name, grid=(m // tm,),
        in_specs=[pl.BlockSpec((tm, w.shape[0]), lambda i: (i, 0)) for w in ws] + [_whole(w) for w in ws],
        out_specs=pl.BlockSpec((tm, kk), lambda i: (i, 0)), out_shape=jax.ShapeDtypeStruct((m, kk), out_dtype),
        compiler_params=_params(("parallel",)),
    )(*xs, *ws)


def _groups_outer(xs, y, ws, name):
    n, (m, kk) = len(ws), y.shape
    tk = _pick(m, (768, 512, 256, 128))
    steps = m // tk

    def body(y_ref, *refs):
        x_refs, o_refs, acc_refs = refs[:n], refs[n:2 * n], refs[2 * n:]
        k = pl.program_id(0)
        b = y_ref[...].astype(BF16)
        for x_ref, o_ref, acc_ref in zip(x_refs, o_refs, acc_refs):
            part = lax.dot_general(x_ref[...].astype(BF16), b, _TN, preferred_element_type=F32)

            @pl.when(k == 0)
            def _(acc_ref=acc_ref, part=part):
                acc_ref[...] = part

            @pl.when(k > 0)
            def _(acc_ref=acc_ref, part=part):
                acc_ref[...] += part

            @pl.when(k == steps - 1)
            def _(acc_ref=acc_ref, o_ref=o_ref):
                o_ref[...] = acc_ref[...].astype(o_ref.dtype)

    return _pcall(
        body, name=name, grid=(steps,),
        in_specs=[pl.BlockSpec((tk, kk), lambda k: (k, 0))] + [pl.BlockSpec((tk, w.shape[0]), lambda k: (k, 0)) for w in ws],
        out_specs=[_whole(w) for w in ws], out_shape=[jax.ShapeDtypeStruct(w.shape, w.dtype) for w in ws],
        scratch_shapes=[pltpu.VMEM(w.shape, F32) for w in ws], compiler_params=_params(("arbitrary",)),
    )(y, *xs)


def _linear_multi(name, out_dtypes):
    @jax.custom_vjp
    def op(x, wts):
        return tuple(_groups_apply(x, wts, out_dtypes, name + "_fwd"))

    def fwd(x, wts):
        return op(x, wts), (x, wts)

    def bwd(res, dys):
        x, wts = res
        return _groups_sum(dys, wts, x.dtype, name + "_dx"), tuple(_groups_outer(dys, x, wts, name + "_dw"))

    op.defvjp(fwd, bwd)
    return op


def _linear_sum(name):
    @jax.custom_vjp
    def op(xs, ws):
        return _groups_sum(xs, ws, F32, name + "_fwd")

    def fwd(xs, ws):
        return op(xs, ws), (xs, ws)

    def bwd(res, dy):
        xs, ws = res
        return tuple(_groups_apply(dy, ws, [x.dtype for x in xs], name + "_dx")), tuple(_groups_outer(xs, dy, ws, name + "_dw"))

    op.defvjp(fwd, bwd)
    return op


def _relu2_epi(acc):
    return jnp.square(jnp.maximum(acc, 0.0)), acc


def _relu2_bwd_epi(acc, a):
    return (acc * (2.0 * jnp.maximum(a.astype(F32), 0.0)),)


def _mlp(name):
    @jax.custom_vjp
    def op(h, w1t, w2):
        s, _ = _mm(h, w1t, "nt", (BF16, BF16), name + "_up", epi=_relu2_epi)
        return _mm(s, w2, "nn", (F32,), name + "_down")[0]

    def fwd(h, w1t, w2):
        s, a = _mm(h, w1t, "nt", (BF16, BF16), name + "_up", epi=_relu2_epi)
        return _mm(s, w2, "nn", (F32,), name + "_down")[0], (h, w1t, w2, s, a)

    def bwd(res, dy):
        h, w1t, w2, s, a = res
        da = _mm(dy, w2, "nt", (BF16,), name + "_ds", epi=_relu2_bwd_epi, extras=(a,))[0]
        dw2 = _mm(s, dy, "tn", (w2.dtype,), name + "_dw2")[0]
        dw1t = _mm(da, h, "tn", (w1t.dtype,), name + "_dw1")[0]
        dh = _mm(da, w1t, "nn", (h.dtype,), name + "_dh")[0]
        return dh, dw1t, dw2

    op.defvjp(fwd, bwd)
    return op


def _two_pass_dot(x, m):
    hi = x.astype(BF16)
    lo = (x - hi.astype(F32)).astype(BF16)
    mb = m.astype(BF16)
    return jnp.dot(hi, mb, preferred_element_type=F32) + jnp.dot(lo, mb, preferred_element_type=F32)


@jax.custom_vjp
def _sym_dot(x, m):
    return _two_pass_dot(x, m)


def _sym_dot_fwd(x, m):
    return _two_pass_dot(x, m), m


def _sym_dot_bwd(m, g):
    return _two_pass_dot(g, m), jnp.zeros_like(m)


_sym_dot.defvjp(_sym_dot_fwd, _sym_dot_bwd)


def _neighbour(x):
    lane = lax.broadcasted_iota(jnp.int32, x.shape, 1)
    return jnp.where(lane % 2 == 0, pltpu.roll(x, x.shape[1] - 1, 1), pltpu.roll(x, 1, 1))


@jax.custom_vjp
def _swap_pairs(x):
    return _neighbour(x)


_swap_pairs.defvjp(lambda x: (_neighbour(x), None), lambda _, g: (_neighbour(g),))


def _lane_group(x, k):
    return x[:, k * B_GROUP_DIM:(k + 1) * B_GROUP_DIM]


@jax.custom_vjp
def _group_mix(ws, v):
    return jnp.concatenate([jnp.dot(w.astype(BF16), _lane_group(v, k).astype(BF16), preferred_element_type=F32)
                            for k, w in enumerate(ws)], axis=1)


def _group_mix_fwd(ws, v):
    return _group_mix(ws, v), (ws, v)


def _group_mix_bwd(res, d):
    ws, v = res
    parts = [(_lane_group(d, k).astype(BF16), _lane_group(v, k).astype(BF16)) for k in range(len(ws))]
    dws = tuple(lax.dot_general(dk, vk, (((1,), (1,)), ((), ())), preferred_element_type=F32) for dk, vk in parts)
    dv = jnp.concatenate([lax.dot_general(w.astype(BF16), dk, (((0,), (0,)), ((), ())), preferred_element_type=F32)
                          for w, (dk, _) in zip(ws, parts)], axis=1)
    return dws, dv


_group_mix.defvjp(_group_mix_fwd, _group_mix_bwd)


def _rowwise(name, f, out_specs, tl, ctx_blocks=0):
    def seg(l, s):
        return jnp.where(l >= ctx_blocks, s - 1, 0) if s > 1 else 0

    def specs(rows, tabs, pers, glbs, consts):
        row_specs = [pl.BlockSpec((1, tl, r.shape[2]), lambda b, l: (b, l, 0)) for r in rows]
        tab_specs = [pl.BlockSpec((tl, t.shape[1]), lambda b, l: (l, 0)) for t in tabs]
        per_specs = [pl.BlockSpec((1, 1, 1, p.shape[3]), functools.partial(lambda b, l, s: (b, seg(l, s), 0, 0), s=p.shape[1])) for p in pers]
        glb_specs = [pl.BlockSpec(g.shape, functools.partial(lambda b, l, nd: (0,) * nd, nd=g.ndim)) for g in glbs]
        const_specs = [pl.BlockSpec(c.shape, functools.partial(lambda b, l, nd: (0,) * nd, nd=c.ndim)) for c in consts]
        return row_specs, tab_specs, per_specs, glb_specs, const_specs

    def load(refs_rows, refs_tabs, refs_pers, refs_glbs, refs_consts):
        return (tuple(r[0].astype(F32) for r in refs_rows), tuple(t[...] for t in refs_tabs),
                tuple(p[0, 0].astype(F32) for p in refs_pers), tuple(g[...].astype(F32) for g in refs_glbs),
                tuple(c[...] for c in refs_consts))

    def call_fwd(rows, tabs, pers, glbs, consts):
        bsz, length = rows[0].shape[:2]
        nr, nt, npp, ng, nc = len(rows), len(tabs), len(pers), len(glbs), len(consts)
        rs, ts, ps, gs, cs = specs(rows, tabs, pers, glbs, consts)

        def body(*refs):
            ins, outs = refs[:nr + nt + npp + ng + nc], refs[nr + nt + npp + ng + nc:]
            r, t, p, g, c = load(ins[:nr], ins[nr:nr + nt], ins[nr + nt:nr + nt + npp], ins[nr + nt + npp:nr + nt + npp + ng], ins[nr + nt + npp + ng:])
            for o_ref, o in zip(outs, f(r, t, p, g, c)):
                o_ref[0] = o.astype(o_ref.dtype)

        return _pcall(
            body, name=name + "_fwd", grid=(bsz, length // tl),
            in_specs=rs + ts + ps + gs + cs,
            out_specs=[pl.BlockSpec((1, tl, w), lambda b, l: (b, l, 0)) for w, _ in out_specs],
            out_shape=[jax.ShapeDtypeStruct((bsz, length, w), d) for w, d in out_specs],
            compiler_params=_params(("parallel", "parallel")),
        )(*rows, *tabs, *pers, *glbs, *consts)

    def call_bwd(rows, tabs, pers, glbs, consts, cts):
        bsz, length = rows[0].shape[:2]
        nr, nt, npp, ng, nc, no = len(rows), len(tabs), len(pers), len(glbs), len(consts), len(cts)
        rs, ts, ps, gs, cs = specs(rows, tabs, pers, glbs, consts)
        n_in = nr + nt + npp + ng + nc

        def body(*refs):
            ins, ct_refs, outs = refs[:n_in], refs[n_in:n_in + no], refs[n_in + no:]
            r, t, p, g, c = load(ins[:nr], ins[nr:nr + nt], ins[nr + nt:nr + nt + npp], ins[nr + nt + npp:nr + nt + npp + ng], ins[nr + nt + npp + ng:])
            _, vjp = jax.vjp(lambda r_, p_, g_: tuple(f(r_, t, p_, g_, c)), r, p, g)
            dr, dp, dg = vjp(tuple(ct[0].astype(F32) for ct in ct_refs))
            dr_refs, dp_refs, dg_refs = outs[:nr], outs[nr:nr + npp], outs[nr + npp:]
            for ref, d in zip(dr_refs, dr):
                ref[0] = d.astype(ref.dtype)
            b, l = pl.program_id(0), pl.program_id(1)
            first_of_segment = (l == 0) | (l == ctx_blocks)
            for ref, d in zip(dp_refs, dp):
                @pl.when(first_of_segment)
                def _(ref=ref, d=d):
                    ref[0, 0] = d

                @pl.when(jnp.logical_not(first_of_segment))
                def _(ref=ref, d=d):
                    ref[0, 0] += d
            first = (b == 0) & (l == 0)
            for ref, d in zip(dg_refs, dg):
                @pl.when(first)
                def _(ref=ref, d=d):
                    ref[...] = d

                @pl.when(jnp.logical_not(first))
                def _(ref=ref, d=d):
                    ref[...] += d

        ct_specs = [pl.BlockSpec((1, tl, w), lambda b, l: (b, l, 0)) for w, _ in out_specs]
        outs = _pcall(
            body, name=name + "_bwd", grid=(bsz, length // tl),
            in_specs=rs + ts + ps + gs + cs + ct_specs,
            out_specs=rs + ps + gs,
            out_shape=[jax.ShapeDtypeStruct(r.shape, r.dtype) for r in rows]
            + [jax.ShapeDtypeStruct(p.shape, F32) for p in pers] + [jax.ShapeDtypeStruct(g.shape, F32) for g in glbs],
            compiler_params=_params(("arbitrary", "arbitrary")),
        )(*rows, *tabs, *pers, *glbs, *consts, *cts)
        return tuple(outs[:nr]), tuple(outs[nr:nr + npp]), tuple(outs[nr + npp:])

    @jax.custom_vjp
    def op(rows, tabs, pers, glbs, consts):
        return tuple(call_fwd(rows, tabs, pers, glbs, consts))

    def fwd(rows, tabs, pers, glbs, consts):
        return op(rows, tabs, pers, glbs, consts), (rows, tabs, pers, glbs, consts)

    def bwd(res, cts):
        rows, tabs, pers, glbs, consts = res
        dr, dp, dg = call_bwd(rows, tabs, pers, glbs, consts, tuple(cts))
        dp = tuple(d.astype(p.dtype) for d, p in zip(dp, pers))
        dg = tuple(d.astype(g.dtype) for d, g in zip(dg, glbs))
        return dr, tuple(jnp.zeros_like(t) for t in tabs), dp, dg, tuple(jnp.zeros_like(c) for c in consts)

    op.defvjp(fwd, bwd)
    return op


def _rms(x, g):
    return x * lax.rsqrt(jnp.mean(x * x, axis=-1, keepdims=True) + EPS) * g


def _f_silu(r, t, p, g, c):
    return (jax.nn.silu(r[0]),)


def _f_modulate(r, t, p, g, c):
    shift, scale = p
    return (_rms(r[0], g[0]) * (1.0 + scale) + shift,)


def _f_res_modulate(r, t, p, g, c):
    x, y = r
    gate, shift, scale = p
    xn = x + gate * y
    return xn, _rms(xn, g[0]) * (1.0 + scale) + shift


def _f_res(r, t, p, g, c):
    return (r[0] + p[0] * r[1],)


def _f_headnorm_rope(r, t, p, g, c):
    x = r[0]
    cos, sin = t
    xn = x * lax.rsqrt(_sym_dot(x * x, c[0]) + EPS) * g[0]
    return (xn * cos + _swap_pairs(xn) * sin,)


def _f_rope(r, t, p, g, c):
    x = r[0]
    cos, sin = t
    return (x * cos + _swap_pairs(x) * sin,)


def _f_rms(r, t, p, g, c):
    return (_rms(r[0], g[0]),)


def _f_sgu_pre(r, t, p, g, c):
    u = jax.nn.gelu(r[0])
    v = jax.nn.gelu(r[1])
    vn = v * lax.rsqrt(_sym_dot(v * v, c[0]) + EPS) * g[0]
    return u, vn


def _f_sgu_mix(r, t, p, g, c):
    u, vn = r
    return (u * (g[B_GROUPS] + _group_mix(tuple(g[:B_GROUPS]), vn)),)


def _f_glu(r, t, p, g, c):
    return (r[0] * jax.nn.sigmoid(r[1]),)


def _f_ln_silu(r, t, p, g, c):
    x = r[0]
    mu = jnp.mean(x, axis=-1, keepdims=True)
    var = jnp.mean(jnp.square(x - mu), axis=-1, keepdims=True)
    return (jax.nn.silu((x - mu) * lax.rsqrt(var + EPS) * g[0] + g[1]),)


LOG2_E = 1.4426950408889634
LN_2 = 0.6931471805599453


def _attention_rows(name, ctx_len, heads, kv_heads):
    grp = heads // kv_heads

    def by_segment(qi, cb, lk, run):
        if cb > 0:
            @pl.when(qi < cb)
            def _():
                run(ctx_len)

            @pl.when(qi >= cb)
            def _():
                run(lk)
        else:
            run(lk)

    def head(ref, h, hd):
        return ref[0, :, h * hd:(h + 1) * hd]

    def scores(q_tile, kk, hd):
        qs = (q_tile.astype(F32) * (LOG2_E * hd ** -0.5)).astype(BF16)
        return qs, lax.dot_general(qs, kk, _NT, preferred_element_type=F32)

    def shapes(q):
        bsz, length, width = q.shape
        tq = math.gcd(_pick(length, (256, 128)), ctx_len) if ctx_len else _pick(length, (256, 128))
        return bsz, length, width // heads, tq, ctx_len // tq

    def call_fwd(q, k, v):
        bsz, length, hd, tq, cb = shapes(q)

        def body(q_ref, k_ref, v_ref, o_ref, lse_ref):
            def run(nk):
                k_all, v_all = k_ref[0, :nk], v_ref[0, :nk]
                outs = []
                for j in range(kv_heads):
                    kk = k_all[:, j * hd:(j + 1) * hd]
                    v_ones = jnp.concatenate([v_all[:, j * hd:(j + 1) * hd], jnp.ones((nk, hd), BF16)], axis=1)
                    for h in range(j * grp, (j + 1) * grp):
                        _, s = scores(head(q_ref, h, hd), kk, hd)
                        m = jnp.max(s, axis=-1, keepdims=True)
                        acc = jnp.dot(jnp.exp2(s - m).astype(BF16), v_ones, preferred_element_type=F32)
                        l = acc[:, hd:hd + 1]
                        outs.append((acc[:, :hd] / l).astype(BF16))
                        lse_ref[0, h] = m + jnp.log2(l)
                o_ref[0] = jnp.concatenate(outs, axis=1)

            by_segment(pl.program_id(1), cb, length, run)

        q_spec = pl.BlockSpec((1, tq, heads * hd), lambda b, i: (b, i, 0))
        kv_spec = pl.BlockSpec((1, length, kv_heads * hd), lambda b, i: (b, 0, 0))
        return _pcall(
            body, name=name + "_fwd", grid=(bsz, length // tq), in_specs=[q_spec, kv_spec, kv_spec],
            out_specs=[q_spec, pl.BlockSpec((1, heads, tq, 1), lambda b, i: (b, 0, i, 0))],
            out_shape=[jax.ShapeDtypeStruct(q.shape, BF16), jax.ShapeDtypeStruct((bsz, heads, length, 1), F32)],
            compiler_params=_params(("parallel", "parallel")),
        )(q, k, v)

    def call_bwd(q, k, v, o, lse, do):
        bsz, length, hd, tq, cb = shapes(q)
        nq = length // tq

        def body(q_ref, k_ref, v_ref, o_ref, lse_ref, do_ref, dq_ref, dk_ref, dv_ref, dk_acc, dv_acc):
            qi = pl.program_id(1)

            @pl.when(qi == 0)
            def _():
                dk_acc[...] = jnp.zeros_like(dk_acc)
                dv_acc[...] = jnp.zeros_like(dv_acc)

            def run(nk):
                k_all, v_all = k_ref[0, :nk], v_ref[0, :nk]
                dqs, dks, dvs = [], [], []
                for j in range(kv_heads):
                    kk, vv = k_all[:, j * hd:(j + 1) * hd], v_all[:, j * hd:(j + 1) * hd]
                    dk_sum = dv_sum = None
                    for h in range(j * grp, (j + 1) * grp):
                        dd = head(do_ref, h, hd)
                        qs, s = scores(head(q_ref, h, hd), kk, hd)
                        p = jnp.exp2(s - lse_ref[0, h])
                        delta = jnp.sum(dd.astype(F32) * head(o_ref, h, hd).astype(F32), axis=-1, keepdims=True)
                        t = (p * (lax.dot_general(dd, vv, _NT, preferred_element_type=F32) - delta)).astype(BF16)
                        dqs.append((jnp.dot(t, kk, preferred_element_type=F32) * hd ** -0.5).astype(BF16))
                        dk_h = lax.dot_general(t, qs, _TN, preferred_element_type=F32)
                        dv_h = lax.dot_general(p.astype(BF16), dd, _TN, preferred_element_type=F32)
                        dk_sum = dk_h if dk_sum is None else dk_sum + dk_h
                        dv_sum = dv_h if dv_sum is None else dv_sum + dv_h
                    dks.append(dk_sum)
                    dvs.append(dv_sum)
                dq_ref[0] = jnp.concatenate(dqs, axis=1)
                dk_acc[:nk] += jnp.concatenate(dks, axis=1)
                dv_acc[:nk] += jnp.concatenate(dvs, axis=1)

            by_segment(qi, cb, length, run)

            @pl.when(qi == nq - 1)
            def _():
                dk_ref[0] = (dk_acc[...] * LN_2).astype(dk_ref.dtype)
                dv_ref[0] = dv_acc[...].astype(dv_ref.dtype)

        q_spec = pl.BlockSpec((1, tq, heads * hd), lambda b, i: (b, i, 0))
        kv_spec = pl.BlockSpec((1, length, kv_heads * hd), lambda b, i: (b, 0, 0))
        lse_spec = pl.BlockSpec((1, heads, tq, 1), lambda b, i: (b, 0, i, 0))
        return _pcall(
            body, name=name + "_bwd", grid=(bsz, nq), in_specs=[q_spec, kv_spec, kv_spec, q_spec, lse_spec, q_spec],
            out_specs=[q_spec, kv_spec, kv_spec],
            out_shape=[jax.ShapeDtypeStruct(q.shape, BF16), jax.ShapeDtypeStruct(k.shape, BF16), jax.ShapeDtypeStruct(v.shape, BF16)],
            scratch_shapes=[pltpu.VMEM(k.shape[1:], F32), pltpu.VMEM(v.shape[1:], F32)],
            compiler_params=_params(("parallel", "arbitrary")),
        )(q, k, v, o, lse, do)

    @jax.custom_vjp
    def op(q, k, v):
        return call_fwd(q, k, v)[0]

    def fwd(q, k, v):
        o, lse = call_fwd(q, k, v)
        return o, (q, k, v, o, lse)

    def bwd(res, do):
        q, k, v, o, lse = res
        dq, dk, dv = call_bwd(q, k, v, o, lse, do)
        return dq, dk.astype(k.dtype), dv.astype(v.dtype)

    op.defvjp(fwd, bwd)
    return op


def _attention_latent_rows(name, heads, nope, rope, dv, heads_forward, heads_backward):
    lanes = 128
    scale = (nope + rope) ** -0.5

    def shapes(q, kn):
        bsz, lq, _ = q.shape
        return bsz, lq, kn.shape[1], _pick(lq, (256, 128))

    def scores(q_tile, kk):
        qs = (q_tile.astype(F32) * (LOG2_E * scale)).astype(BF16)
        return qs, lax.dot_general(qs, kk, _NT, preferred_element_type=F32)

    def call_fwd(q, kn, kr, v):
        bsz, lq, lk, tq = shapes(q, kn)
        heads_per_step = heads_forward

        def body(q_ref, kn_ref, kr_ref, v_ref, o_ref, lse_ref):
            real = lax.broadcasted_iota(jnp.int32, (lk, lanes), 1) < dv
            outs = []
            for h in range(heads_per_step):
                kk = kn_ref[0, :, h * lanes:(h + 1) * lanes] + kr_ref[0]
                v_ones = jnp.where(real, v_ref[0, :, h * lanes:(h + 1) * lanes], jnp.ones((lk, lanes), BF16))
                _, s = scores(q_ref[0, :, h * lanes:(h + 1) * lanes], kk)
                m = jnp.max(s, axis=-1, keepdims=True)
                acc = jnp.dot(jnp.exp2(s - m).astype(BF16), v_ones, preferred_element_type=F32)
                l = acc[:, dv:dv + 1]
                outs.append((acc[:, :dv] / l).astype(BF16))
                lse_ref[0, h] = m + jnp.log2(l)
            o_ref[0] = jnp.concatenate(outs, axis=1)

        wide = heads_per_step * lanes
        return _pcall(
            body, name=name + "_fwd", grid=(bsz, heads // heads_per_step, lq // tq),
            in_specs=[pl.BlockSpec((1, tq, wide), lambda b, g, i: (b, i, g)), pl.BlockSpec((1, lk, wide), lambda b, g, i: (b, 0, g)),
                      pl.BlockSpec((1, lk, lanes), lambda b, g, i: (b, 0, 0)), pl.BlockSpec((1, lk, wide), lambda b, g, i: (b, 0, g))],
            out_specs=[pl.BlockSpec((1, tq, heads_per_step * dv), lambda b, g, i: (b, i, g)),
                       pl.BlockSpec((1, heads_per_step, tq, 1), lambda b, g, i: (b, g, i, 0))],
            out_shape=[jax.ShapeDtypeStruct((bsz, lq, heads * dv), BF16), jax.ShapeDtypeStruct((bsz, heads, lq, 1), F32)],
            compiler_params=_params(("parallel", "parallel", "parallel")),
        )(q, kn, kr, v)

    def call_bwd(q, kn, kr, v, o, lse, do):
        bsz, lq, lk, tq = shapes(q, kn)
        heads_per_step = heads_backward

        def body(q_ref, kn_ref, kr_ref, v_ref, o_ref, lse_ref, do_ref, dq_ref, dkn_ref, dkr_ref, dv_ref, dkn_acc, dkr_acc, dv_acc):
            g, qi = pl.program_id(1), pl.program_id(2)
            last_q = qi == lq // tq - 1

            @pl.when(qi == 0)
            def _():
                dkn_acc[...] = jnp.zeros_like(dkn_acc)
                dv_acc[...] = jnp.zeros_like(dv_acc)

            @pl.when((qi == 0) & (g == 0))
            def _():
                dkr_acc[...] = jnp.zeros_like(dkr_acc)

            dqs, dks, dvs = [], [], []
            for h in range(heads_per_step):
                kk = kn_ref[0, :, h * lanes:(h + 1) * lanes] + kr_ref[0]
                vv = v_ref[0, :, h * lanes:h * lanes + dv]
                dd = do_ref[0, :, h * dv:(h + 1) * dv]
                qs, s = scores(q_ref[0, :, h * lanes:(h + 1) * lanes], kk)
                p = jnp.exp2(s - lse_ref[0, h])
                delta = jnp.sum(dd.astype(F32) * o_ref[0, :, h * dv:(h + 1) * dv].astype(F32), axis=-1, keepdims=True)
                t = (p * (lax.dot_general(dd, vv, _NT, preferred_element_type=F32) - delta)).astype(BF16)
                dqs.append((jnp.dot(t, kk, preferred_element_type=F32) * scale).astype(BF16))
                dks.append(lax.dot_general(t, qs, _TN, preferred_element_type=F32) * LN_2)
                dv_h = lax.dot_general(p.astype(BF16), dd, _TN, preferred_element_type=F32)
                dvs.append(jnp.concatenate([dv_h, jnp.zeros((lk, lanes - dv), F32)], axis=1))
            dq_ref[0] = jnp.concatenate(dqs, axis=1)
            dkn_acc[...] += jnp.concatenate(dks, axis=1)
            dv_acc[...] += jnp.concatenate(dvs, axis=1)
            shared = dks[0]
            for d_h in dks[1:]:
                shared = shared + d_h
            dkr_acc[...] += shared

            @pl.when(last_q)
            def _():
                dkn_ref[0] = dkn_acc[...].astype(dkn_ref.dtype)
                dv_ref[0] = dv_acc[...].astype(dv_ref.dtype)

            @pl.when(last_q & (g == heads // heads_per_step - 1))
            def _():
                dkr_ref[0] = dkr_acc[...].astype(dkr_ref.dtype)

        wide = heads_per_step * lanes
        q_spec = pl.BlockSpec((1, tq, wide), lambda b, g, i: (b, i, g))
        k_spec = pl.BlockSpec((1, lk, wide), lambda b, g, i: (b, 0, g))
        kr_spec = pl.BlockSpec((1, lk, lanes), lambda b, g, i: (b, 0, 0))
        o_spec = pl.BlockSpec((1, tq, heads_per_step * dv), lambda b, g, i: (b, i, g))
        lse_spec = pl.BlockSpec((1, heads_per_step, tq, 1), lambda b, g, i: (b, g, i, 0))
        return _pcall(
            body, name=name + "_bwd", grid=(bsz, heads // heads_per_step, lq // tq),
            in_specs=[q_spec, k_spec, kr_spec, k_spec, o_spec, lse_spec, o_spec],
            out_specs=[q_spec, k_spec, kr_spec, k_spec],
            out_shape=[jax.ShapeDtypeStruct(q.shape, BF16), jax.ShapeDtypeStruct(kn.shape, BF16), jax.ShapeDtypeStruct(kr.shape, BF16),
                       jax.ShapeDtypeStruct(v.shape, BF16)],
            scratch_shapes=[pltpu.VMEM((lk, wide), F32), pltpu.VMEM((lk, lanes), F32), pltpu.VMEM((lk, wide), F32)],
            compiler_params=_params(("parallel", "arbitrary", "arbitrary")),
        )(q, kn, kr, v, o, lse, do)

    @jax.custom_vjp
    def op(q, kn, kr, v):
        return call_fwd(q, kn, kr, v)[0]

    def fwd(q, kn, kr, v):
        o, lse = call_fwd(q, kn, kr, v)
        return o, (q, kn, kr, v, o, lse)

    def bwd(res, do):
        q, kn, kr, v, o, lse = res
        dq, dkn, dkr, dv_ = call_bwd(q, kn, kr, v, o, lse, do)
        return dq, dkn.astype(kn.dtype), dkr.astype(kr.dtype), dv_.astype(v.dtype)

    op.defvjp(fwd, bwd)
    return op


def _conv_call(ypad, taps, name):
    bsz, lp, ch = ypad.shape
    length = lp - 2 * ROW_ALIGN
    tl = _pick(length, (256, 128))

    def body(y_ref, w_ref, o_ref):
        base = pl.multiple_of(pl.program_id(1) * tl, tl)
        for r0 in range(0, tl, CONV_ROWS):
            win = y_ref[0, pl.ds(base + r0, CONV_ROWS + 2 * ROW_ALIGN), :]
            acc = jnp.broadcast_to(w_ref[pl.ds(D_CONV, 1), :], (CONV_ROWS, ch))
            for shift in range(SUBLANES):
                shifted = win[shift:shift + CONV_ROWS + 2 * ROW_ALIGN - SUBLANES, :]
                for k in range(shift, D_CONV, SUBLANES):
                    acc = acc + shifted[k - shift:k - shift + CONV_ROWS, :] * w_ref[pl.ds(k, 1), :]
            o_ref[0, pl.ds(r0, CONV_ROWS), :] = acc

    return _pcall(
        body, name=name, grid=(bsz, length // tl),
        in_specs=[pl.BlockSpec((1, lp, ch), lambda b, l: (b, 0, 0)), pl.BlockSpec((D_CONV + 1, ch), lambda b, l: (0, 0))],
        out_specs=pl.BlockSpec((1, tl, ch), lambda b, l: (b, l, 0)),
        out_shape=jax.ShapeDtypeStruct((bsz, length, ch), F32),
        compiler_params=_params(("parallel", "parallel")),
    )(ypad, taps)


def _conv_dw_call(ypad, dout, name):
    bsz, lp, ch = ypad.shape
    length = lp - 2 * ROW_ALIGN
    tl = _pick(length, (256, 128))

    def body(y_ref, d_ref, o_ref):
        b, l = pl.program_id(0), pl.program_id(1)

        @pl.when((b == 0) & (l == 0))
        def _():
            o_ref[...] = jnp.zeros_like(o_ref)

        base = pl.multiple_of(l * tl, tl)
        win = y_ref[0, pl.ds(base, tl + 2 * ROW_ALIGN), :]
        dd = d_ref[0]
        for shift in range(SUBLANES):
            shifted = win[shift:shift + tl + 2 * ROW_ALIGN - SUBLANES, :]
            for k in range(shift, D_CONV, SUBLANES):
                o_ref[pl.ds(k, 1), :] += jnp.sum(shifted[k - shift:k - shift + tl, :] * dd, axis=0, keepdims=True)
        o_ref[pl.ds(D_CONV, 1), :] += jnp.sum(dd, axis=0, keepdims=True)

    return _pcall(
        body, name=name, grid=(bsz, length // tl),
        in_specs=[pl.BlockSpec((1, lp, ch), lambda b, l: (b, 0, 0)), pl.BlockSpec((1, tl, ch), lambda b, l: (b, l, 0))],
        out_specs=pl.BlockSpec((D_CONV + 1, ch), lambda b, l: (0, 0)),
        out_shape=jax.ShapeDtypeStruct((D_CONV + 1, ch), F32),
        compiler_params=_params(("arbitrary", "arbitrary")),
    )(ypad, dout)


def _pad_rows(y):
    return jnp.pad(y, ((0, 0), (CONV_PAD, 2 * ROW_ALIGN - CONV_PAD), (0, 0)))


@jax.custom_vjp
def _dwconv(y, taps):
    return _conv_call(_pad_rows(y), taps, "conv_fwd")


def _dwconv_fwd(y, taps):
    return _dwconv(y, taps), (y, taps)


def _dwconv_bwd(res, dout):
    y, taps = res
    flipped = jnp.concatenate([taps[:D_CONV][::-1], jnp.zeros_like(taps[D_CONV:])], axis=0)
    dy = _conv_call(_pad_rows(dout), flipped, "conv_dy")
    dtaps = _conv_dw_call(_pad_rows(y), dout, "conv_dw")
    return dy, dtaps


_dwconv.defvjp(_dwconv_fwd, _dwconv_bwd)


def _loss_head(x, target, g):
    bsz, length, d = x.shape
    tl = _pick(length, (256, 128))

    def f(xb, tb, gb):
        err = _rms(xb, gb) - tb
        return 0.5 * jnp.sum(jnp.sum(err * err, axis=-1, keepdims=True), axis=0, keepdims=True) / d

    def body(x_ref, t_ref, g_ref, loss_ref, dx_ref, dg_ref):
        val, vjp = jax.vjp(lambda xb, gb: f(xb, t_ref[0], gb), x_ref[0], g_ref[...])
        dx, dg = vjp(jnp.ones((1, 1), F32))
        dx_ref[0] = dx
        first = (pl.program_id(0) == 0) & (pl.program_id(1) == 0)

        @pl.when(first)
        def _():
            loss_ref[...] = val
            dg_ref[...] = dg

        @pl.when(jnp.logical_not(first))
        def _():
            loss_ref[...] += val
            dg_ref[...] += dg

    row = pl.BlockSpec((1, tl, d), lambda b, l: (b, l, 0))
    return _pcall(
        body, name="loss_head", grid=(bsz, length // tl),
        in_specs=[row, row, pl.BlockSpec((1, d), lambda b, l: (0, 0))],
        out_specs=[pl.BlockSpec((1, 1), lambda b, l: (0, 0)), row, pl.BlockSpec((1, d), lambda b, l: (0, 0))],
        out_shape=[jax.ShapeDtypeStruct((1, 1), F32), jax.ShapeDtypeStruct(x.shape, F32), jax.ShapeDtypeStruct((1, d), F32)],
        compiler_params=_params(("arbitrary", "arbitrary")),
    )(x, target, g)


def _adamw(w, g, m, v, name):
    shape = w.shape
    cols = shape[-1]
    rows = w.size // cols
    tr = _pick(rows, (512, 256, 128))
    w2, g2, m2, v2 = (t.reshape(rows, cols) for t in (w, g, m, v))

    def body(w_ref, g_ref, m_ref, v_ref, d_ref, nm_ref, nv_ref):
        gg = g_ref[...]
        nm = ADAM_B1 * m_ref[...] + (1.0 - ADAM_B1) * gg
        nv = ADAM_B2 * v_ref[...] + (1.0 - ADAM_B2) * jnp.square(gg)
        m_hat = nm / (1.0 - ADAM_B1 ** ADAM_STEP)
        v_hat = nv / (1.0 - ADAM_B2 ** ADAM_STEP)
        d_ref[...] = -ADAM_LR * (m_hat / (jnp.sqrt(v_hat) + ADAM_EPS) + ADAM_WD * w_ref[...])
        nm_ref[...] = nm
        nv_ref[...] = nv

    spec = pl.BlockSpec((tr, cols), lambda i: (i, 0))
    outs = _pcall(
        body, name=name, grid=(rows // tr,), in_specs=[spec] * 4, out_specs=[spec] * 3,
        out_shape=[jax.ShapeDtypeStruct((rows, cols), F32)] * 3,
        compiler_params=_params(("parallel",)),
    )(w2, g2, m2, v2)
    return tuple(o.reshape(shape) for o in outs)


def _mesh_pos():
    return lax.axis_index("x"), lax.axis_index("y"), lax.axis_index("c")


_RELATIONS = [(dx, dy, dc) for dx in (0, 1) for dy in (0, 1) for dc in (0, 1)][1:]


def _peer(pos, rel):
    return tuple(jnp.where(r == 1, 1 - p, p) if r else p for p, r in zip(pos, rel))


def _block_index(pos):
    return 4 * pos[0] + 2 * pos[1] + pos[2]


_HBM = pl.BlockSpec(memory_space=pltpu.HBM)


def _all_gather(xs, name):
    n = len(xs)

    def body(*refs):
        x_refs, out_refs, (send_sems, recv_sems, local_sems) = refs[:n], refs[n:2 * n], refs[2 * n:]
        x_, y_, c_ = _mesh_pos()
        me, sibling = (x_, y_, c_), (x_, y_, 1 - c_)
        chips = [(1 - x_, y_), (x_, 1 - y_), (1 - x_, 1 - y_)]

        def copy(t, k, block, to, own=False):
            slot = out_refs[t].at[_block_index(block)]
            return pltpu.make_async_remote_copy(
                src_ref=x_refs[t] if own else slot, dst_ref=slot, send_sem=send_sems.at[7 * t + k], recv_sem=recv_sems.at[7 * t + k],
                device_id=to, device_id_type=pl.DeviceIdType.MESH)

        mine = [pltpu.make_async_copy(x_refs[t], out_refs[t].at[_block_index(me)], local_sems.at[t]) for t in range(n)]
        first = [[copy(t, 0, me, sibling, own=True)] + [copy(t, 1 + j, me, (*chip, c_), own=True) for j, chip in enumerate(chips)]
                 for t in range(n)]
        passed = [[copy(t, 4 + j, (*chip, c_), sibling) for j, chip in enumerate(chips)] for t in range(n)]
        for t in range(n):
            mine[t].start()
            for cp in first[t]:
                cp.start()
        for t in range(n):
            for j, chip in enumerate(chips):
                copy(t, 1 + j, (*chip, c_), me).wait_recv()
                passed[t][j].start()
        for t in range(n):
            copy(t, 0, sibling, me).wait_recv()
            for j, chip in enumerate(chips):
                copy(t, 4 + j, (*chip, 1 - c_), me).wait_recv()
            for cp in first[t] + passed[t]:
                cp.wait_send()
            mine[t].wait()

    return _pcall(
        body, name=name, in_specs=[_HBM] * n, out_specs=[_HBM] * n,
        out_shape=[jax.ShapeDtypeStruct((N_DEV,) + x.shape, x.dtype) for x in xs],
        scratch_shapes=[pltpu.SemaphoreType.DMA((7 * n,)), pltpu.SemaphoreType.DMA((7 * n,)), pltpu.SemaphoreType.DMA((n,))],
    )(*xs)


_SEM = pl.BlockSpec(memory_space=pltpu.SEMAPHORE)
_EFFECT = pltpu.SideEffectType.DATAFLOW_SIDE_EFFECTING


def _push_start(srcs, after, name):
    n = len(srcs)
    lands = [lax.empty((N_DEV,) + s.shape[-2:], s.dtype) for s in srcs]

    def body(*refs):
        src_refs, land_refs = refs[:n], refs[n:2 * n]
        send_sems, recv_sems, token = refs[2 * n + 1:3 * n + 1], refs[3 * n + 1:4 * n + 1], refs[-1]
        me = _mesh_pos()
        for t in range(n):
            for rel in _RELATIONS:
                peer = _peer(me, rel)
                pltpu.make_async_remote_copy(
                    src_ref=src_refs[t].at[_block_index(peer)] if srcs[t].ndim == 3 else src_refs[t], dst_ref=land_refs[t].at[_block_index(me)],
                    send_sem=send_sems[t], recv_sem=recv_sems[t], device_id=peer, device_id_type=pl.DeviceIdType.MESH).start()
        token[...] = jnp.zeros_like(token)

    outs = _pcall(
        body, name=name,
        out_shape=[pltpu.SemaphoreType.DMA(())] * (2 * n) + [pltpu.HBM(s.shape, s.dtype) for s in srcs]
        + [pltpu.HBM(l.shape, l.dtype) for l in lands] + [jax.ShapeDtypeStruct((8, 128), F32)],
        in_specs=[_HBM] * (2 * n) + [pl.BlockSpec(memory_space=pl.ANY)],
        out_specs=[_SEM] * (2 * n) + [_HBM] * (2 * n) + [pl.BlockSpec(memory_space=pltpu.VMEM)],
        input_output_aliases={i: 2 * n + i for i in range(2 * n)}, compiler_params=pltpu.CompilerParams(has_side_effects=_EFFECT),
    )(*[pltpu.with_memory_space_constraint(t, pltpu.HBM) for t in list(srcs) + lands], after)
    return (outs[:n], outs[n:2 * n], outs[2 * n:3 * n], outs[3 * n:4 * n]), outs[-1]


def _push_wait(handle, after, owns, me, name):
    send_sems, recv_sems, src_thrus, land_thrus = handle
    n = len(land_thrus)

    def body(*refs):
        land_refs, sends, recvs = refs[n:2 * n], refs[2 * n:3 * n], refs[3 * n:4 * n]
        for t in range(n):
            seven = land_refs[t].at[pl.ds(0, N_DEV - 1)]
            all_seven = pltpu.make_async_remote_copy(src_ref=seven, dst_ref=seven, send_sem=sends[t], recv_sem=recvs[t],
                                                     device_id=_mesh_pos(), device_id_type=pl.DeviceIdType.MESH)
            all_seven.wait_send()
            all_seven.wait_recv()

    outs = _pcall(
        body, name=name,
        out_shape=[pltpu.HBM(t.shape, t.dtype) for t in list(src_thrus) + list(land_thrus)],
        in_specs=[_HBM] * (2 * n) + [_SEM] * (2 * n) + [pl.BlockSpec(memory_space=pl.ANY)], out_specs=[_HBM] * (2 * n),
        input_output_aliases={i: i for i in range(2 * n)}, compiler_params=pltpu.CompilerParams(has_side_effects=_EFFECT),
    )(*src_thrus, *land_thrus, *send_sems, *recv_sems, after)
    return [lax.dynamic_update_slice(land, own[None], (me, 0, 0)) for land, own in zip(outs[n:], owns)]


def _sum_blocks(p, name):
    n, rows, cols = p.shape
    tr = _pick(rows, (256, 128, 64, 32, 16, 8))

    def body(p_ref, o_ref):
        acc = p_ref[0].astype(F32)
        for s in range(1, n):
            acc = acc + p_ref[s].astype(F32)
        o_ref[...] = acc

    return _pcall(
        body, name=name, grid=(rows // tr,),
        in_specs=[pl.BlockSpec((n, tr, cols), lambda i: (0, i, 0))], out_specs=pl.BlockSpec((tr, cols), lambda i: (i, 0)),
        out_shape=jax.ShapeDtypeStruct((rows, cols), F32), compiler_params=_params(("parallel",)),
    )(p)


def _sum_rows(t, name):
    def body(t_ref, o_ref):
        o_ref[...] = jnp.sum(t_ref[...], axis=0, keepdims=True)

    return _pcall(body, name=name, out_shape=jax.ShapeDtypeStruct((1, t.shape[1]), F32))(t)


class _Packing:
    def __init__(self, sizes, align):
        self.offsets, self.sizes, self.align = {}, dict(sizes), align
        row = 0
        for name, size in sizes:
            self.offsets[name] = row
            row += -(-size // (align * PACK_COLS)) * align
        self.rows = row

    def pack(self, pieces):
        return self.pack_blocks({n: pieces[n].reshape(1, -1) for n in self.sizes})[0]

    def pack_blocks(self, pieces):
        out = []
        for n, size in self.sizes.items():
            padded = -(-size // (self.align * PACK_COLS)) * self.align * PACK_COLS
            out.append(jnp.pad(pieces[n], ((0, 0), (0, padded - size))).reshape(pieces[n].shape[0], -1, PACK_COLS))
        return jnp.concatenate(out, axis=1)

    def piece(self, packed, name, lead=()):
        start, size = self.offsets[name], self.sizes[name]
        nrow = -(-size // (self.align * PACK_COLS)) * self.align
        sl = packed[..., start:start + nrow, :]
        return sl.reshape(lead + (nrow * PACK_COLS,))[..., :size]


_PIECES = (("ev_w_in", "ev_w_in", 0, 1, "a"), ("w_out0", "w_out", 0, 0, "a"),
           ("mlp_w1_0", "mlp_w1", 0, 1, "b"), ("mlp_w2_0", "mlp_w2", 0, 0, "b"),
           ("od_w_in", "od_w_in", 0, 1, "c"), ("od_w_uq", "od_w_uq", 0, 1, "c"), ("od_w_ukv", "od_w_ukv", 0, 1, "c"),
           ("w_out1", "w_out", 1, 0, "c"), ("mlp_w1_1", "mlp_w1", 1, 1, "c"), ("mlp_w2_1", "mlp_w2", 1, 0, "c"))
_SMALL_SHARDED = (("od_q_norm_g", 1), ("od_conv_w", 2), ("od_conv_b", 1), ("od_ln_g", 1), ("od_ln_b", 1))
_REPLICATED = ("c_ctx", "norm1_g", "norm2_g", "ev_q_norm_g", "ev_k_norm_g", "ev_sgu_norm_g", "ev_sgu_w", "ev_sgu_b",
               "od_kv_norm_g", "final_g")


def _unshard(blocks, axis):
    moved = jnp.moveaxis(blocks, 0, axis)
    shape = moved.shape
    return moved.reshape(shape[:axis] + (shape[axis] * shape[axis + 1],) + shape[axis + 2:])


def _group_mean_matrix(width, group):
    idx = jnp.arange(width) // group
    return (idx[:, None] == idx[None, :]).astype(F32) / group


def _angles(length, d_rot):
    rows = length // GRID_W
    row = jnp.broadcast_to(jnp.arange(rows)[:, None], (rows, GRID_W)).reshape(-1).astype(F32)
    col = jnp.broadcast_to(jnp.arange(GRID_W)[None, :], (rows, GRID_W)).reshape(-1).astype(F32)
    d_axis = d_rot // 2
    inv = ROPE_THETA ** (-jnp.arange(0, d_axis, 2, dtype=F32) / d_axis)
    return jnp.concatenate([row[:, None] * inv, col[:, None] * inv], axis=-1)


def _rope_tables(length, d_rot, head_dim, heads, ctx_len, tail=0):
    ang = _angles(length, d_rot)
    cos = jnp.repeat(jnp.cos(ang), 2, axis=1)
    sin = jnp.repeat(jnp.sin(ang), 2, axis=1) * jnp.tile(jnp.array([-1.0, 1.0], F32), d_rot // 2)
    keep = head_dim - d_rot - tail
    cos = jnp.concatenate([jnp.ones((length, keep), F32), cos, jnp.ones((length, tail), F32)], axis=1)
    sin = jnp.concatenate([jnp.zeros((length, keep), F32), sin, jnp.zeros((length, tail), F32)], axis=1)
    cos, sin = jnp.tile(cos, (1, heads)), jnp.tile(sin, (1, heads))
    cos = jnp.concatenate([jnp.ones((ctx_len, cos.shape[1]), F32), cos], axis=0)
    sin = jnp.concatenate([jnp.zeros((ctx_len, sin.shape[1]), F32), sin], axis=0)
    return cos, sin


def _segment_params(mod, bsz):
    parts = jnp.split(mod, N_MOD, axis=-1)
    out = []
    for part in parts:
        lat = part[:bsz]
        ctx = jnp.broadcast_to(part[bsz:bsz + 1], lat.shape)
        out.append(jnp.stack([ctx, lat], axis=1)[:, :, None, :])
    return out


def _flat(t):
    return t.reshape(-1, t.shape[-1])


def _sequence_rowwise(ctx_len):
    tl = math.gcd(256, ctx_len)

    def make(name, f, out_specs, rows_per_block=tl, ctx_blocks=ctx_len // tl):
        return _rowwise(name, f, out_specs, rows_per_block, ctx_blocks)

    return make


def _mixer0(xall, modrows0, w, bsz, length, ctx_len):
    d = xall.shape[-1]
    total = ctx_len + length
    rowwise, flat = _sequence_rowwise(ctx_len), _flat
    sh1, sc1, g1, sh2, sc2, _ = _segment_params(modrows0, bsz)
    (h,) = rowwise("mod0", _f_modulate, [(d, BF16)])((xall,), (), (sh1, sc1), (w["norm1_g0"],), ())
    ev_q, ev_kv = A_Q_HEADS * A_HEAD_DIM, A_KV_HEADS * A_HEAD_DIM
    half = B_GROUPS * B_GROUP_DIM
    groups = tuple(jnp.split(w["ev_w_in"], [ev_q, ev_q + ev_kv, ev_q + 2 * ev_kv, ev_q + 2 * ev_kv + half], axis=0))
    qp, kp, vp, zu, zv = [t.reshape(bsz, total, -1) for t in _linear_multi("ev_in", (F32, F32, BF16, F32, F32))(flat(h), groups)]
    cos_q, sin_q = _rope_tables(length, A_HEAD_DIM, A_HEAD_DIM, A_Q_HEADS, ctx_len)
    cos_k, sin_k = cos_q[:, :ev_kv], sin_q[:, :ev_kv]
    (q,) = rowwise("ev_q", _f_headnorm_rope, [(ev_q, BF16)])(
        (qp,), (cos_q, sin_q), (), (jnp.tile(w["ev_q_norm_g"][0], A_Q_HEADS)[None],), (_group_mean_matrix(ev_q, A_HEAD_DIM),))
    (k,) = rowwise("ev_k", _f_headnorm_rope, [(ev_kv, BF16)])(
        (kp,), (cos_k, sin_k), (), (jnp.tile(w["ev_k_norm_g"][0], A_KV_HEADS)[None],), (_group_mean_matrix(ev_kv, A_HEAD_DIM),))
    o_att = _attention_rows("gqa", ctx_len, A_Q_HEADS, A_KV_HEADS)(q, k, vp)
    u, vn = rowwise("sgu_pre", _f_sgu_pre, [(half, F32), (half, BF16)])(
        (zu, zv), (), (), (w["ev_sgu_norm_g"][0].reshape(1, half),), (_group_mean_matrix(half, B_GROUP_DIM),))
    bias = jnp.repeat(w["ev_sgu_b"][0].T, B_GROUP_DIM, axis=1)
    (o_sgu,) = rowwise("sgu_mix", _f_sgu_mix, [(half, BF16)], rows_per_block=B_CHUNK, ctx_blocks=0)(
        (u, vn), (), (), tuple(w["ev_sgu_w"][0][g] for g in range(B_GROUPS)) + (bias,), ())
    y = _linear_sum("out0")((flat(o_att), flat(o_sgu)), tuple(jnp.split(w["w_out0"], 2, axis=0))).reshape(bsz, total, d)
    x1, h = rowwise("res_mod0a", _f_res_modulate, [(d, F32), (d, BF16)])((xall, y), (), (g1, sh2, sc2), (w["norm2_g0"],), ())
    return x1, h


def _mlp0(x1, h, modrows0, modrows1, w, bsz, length, ctx_len):
    d = x1.shape[-1]
    total = ctx_len + length
    g2 = _segment_params(modrows0, bsz)[5]
    sh1, sc1 = _segment_params(modrows1, bsz)[:2]
    y = _mlp("mlp0")(_flat(h), w["mlp_w1_0"], w["mlp_w2_0"]).reshape(bsz, total, d)
    return _sequence_rowwise(ctx_len)("res_mod0b", _f_res_modulate, [(d, F32), (d, BF16)])((x1, y), (), (g2, sh1, sc1), (w["norm1_g1"],), ())


def _layer1(x2, h, modrows1, w, bsz, length, ctx_len):
    d = x2.shape[-1]
    total = ctx_len + length
    half = B_GROUPS * B_GROUP_DIM
    rowwise, flat = _sequence_rowwise(ctx_len), _flat
    _, _, g1n, sh2n, sc2n, g2n = _segment_params(modrows1, bsz)
    g_cq, g_ckv, g_kr, g_za, g_zg = jnp.split(w["od_w_in"], [C_Q_RANK, C_Q_RANK + C_KV_RANK, C_Q_RANK + C_KV_RANK + C_ROPE,
                                                             C_Q_RANK + C_KV_RANK + C_ROPE + half], axis=0)
    lanes, c_qk = 128, C_NOPE + C_ROPE
    g_kr = jnp.pad(g_kr, ((C_NOPE, lanes - c_qk), (0, 0)))
    cq, ckv, kr, za, zg = [t.reshape(bsz, total, -1) for t in _linear_multi("od_in", (F32,) * 5)(flat(h), (g_cq, g_ckv, g_kr, g_za, g_zg))]
    lat = slice(ctx_len, total)
    lat_tl = math.gcd(256, length)
    (cqn,) = _rowwise("od_qn", _f_rms, [(C_Q_RANK, BF16)], lat_tl)((cq[:, lat],), (), (), (w["od_q_norm_g"],), ())
    w_uq = jnp.pad(w["od_w_uq"].reshape(C_HEADS, c_qk, C_Q_RANK), ((0, 0), (0, lanes - c_qk), (0, 0))).reshape(C_HEADS * lanes, C_Q_RANK)
    qf = _linear("od_uq")(flat(cqn), w_uq).reshape(bsz, length, C_HEADS * lanes)
    cos_q, sin_q = _rope_tables(length, C_ROPE, lanes, C_HEADS, 0, tail=lanes - c_qk)
    (q,) = _rowwise("od_qrope", _f_rope, [(C_HEADS * lanes, BF16)], lat_tl)((qf,), (cos_q, sin_q), (), (), ())
    (ckvn,) = rowwise("od_kvn", _f_rms, [(C_KV_RANK, BF16)])((ckv,), (), (), (w["od_kv_norm_g"],), ())
    per_head = w["od_w_ukv"].reshape(C_HEADS, C_NOPE + C_V, C_KV_RANK)
    w_kn = jnp.pad(per_head[:, :C_NOPE], ((0, 0), (0, lanes - C_NOPE), (0, 0))).reshape(C_HEADS * lanes, C_KV_RANK)
    w_v = jnp.pad(per_head[:, C_NOPE:], ((0, 0), (0, lanes - C_V), (0, 0))).reshape(C_HEADS * lanes, C_KV_RANK)
    kn, vv = [t.reshape(bsz, total, -1) for t in _linear_multi("od_ukv", (BF16, BF16))(flat(ckvn), (w_kn, w_v))]
    cos_r, sin_r = _rope_tables(length, C_ROPE, lanes, 1, ctx_len, tail=lanes - c_qk)
    (krr,) = rowwise("od_krope", _f_rope, [(lanes, BF16)])((kr,), (cos_r, sin_r), (), (), ())
    o_att = _attention_latent_rows("mla", C_HEADS, C_NOPE, C_ROPE, C_V, 4, 2)(q, kn, krr, vv)
    (glu,) = _rowwise("glu", _f_glu, [(half, F32)], lat_tl)((za[:, lat], zg[:, lat]), (), (), (), ())
    taps = jnp.concatenate([w["od_conv_w"][0], w["od_conv_b"]], axis=0)
    conv = _dwconv(glu, taps)
    (o_conv,) = _rowwise("ln_silu", _f_ln_silu, [(half, BF16)], lat_tl)((conv,), (), (), (w["od_ln_g"], w["od_ln_b"]), ())
    y = _linear_sum("out1")((flat(o_att), flat(o_conv)), tuple(jnp.split(w["w_out1"], 2, axis=0))).reshape(bsz, length, d)
    lat_param = lambda p: p[:, 1:]
    x3, h = _rowwise("res_mod1a", _f_res_modulate, [(d, F32), (d, BF16)], lat_tl)(
        (x2[:, lat], y), (), (lat_param(g1n), lat_param(sh2n), lat_param(sc2n)), (w["norm2_g1"],), ())
    y = _mlp("mlp1")(flat(h), w["mlp_w1_1"], w["mlp_w2_1"]).reshape(bsz, length, d)
    (x4,) = _rowwise("res1b", _f_res, [(d, F32)], lat_tl)((x3, y), (), (lat_param(g2n),), (), ())
    return x4


def kernel(x, c, ctx, c_ctx, ada_w, ada_b, norm1_g, norm2_g, w_out, mlp_w1, mlp_w2, ev_w_in, ev_q_norm_g, ev_k_norm_g, ev_sgu_norm_g, ev_sgu_w, ev_sgu_b, od_w_in, od_q_norm_g, od_kv_norm_g, od_w_uq, od_w_ukv, od_conv_w, od_conv_b, od_ln_g, od_ln_b, final_g, loss_target, m_c_ctx, m_ada_w, m_ada_b, m_norm1_g, m_norm2_g, m_w_out, m_mlp_w1, m_mlp_w2, m_ev_w_in, m_ev_q_norm_g, m_ev_k_norm_g, m_ev_sgu_norm_g, m_ev_sgu_w, m_ev_sgu_b, m_od_w_in, m_od_q_norm_g, m_od_kv_norm_g, m_od_w_uq, m_od_w_ukv, m_od_conv_w, m_od_conv_b, m_od_ln_g, m_od_ln_b, m_final_g, v_c_ctx, v_ada_w, v_ada_b, v_norm1_g, v_norm2_g, v_w_out, v_mlp_w1, v_mlp_w2, v_ev_w_in, v_ev_q_norm_g, v_ev_k_norm_g, v_ev_sgu_norm_g, v_ev_sgu_w, v_ev_sgu_b, v_od_w_in, v_od_q_norm_g, v_od_kv_norm_g, v_od_w_uq, v_od_w_ukv, v_od_conv_w, v_od_conv_b, v_od_ln_g, v_od_ln_b, v_final_g):
    names = ["c_ctx", "ada_w", "ada_b", "norm1_g", "norm2_g", "w_out", "mlp_w1", "mlp_w2", "ev_w_in", "ev_q_norm_g", "ev_k_norm_g",
             "ev_sgu_norm_g", "ev_sgu_w", "ev_sgu_b", "od_w_in", "od_q_norm_g", "od_kv_norm_g", "od_w_uq", "od_w_ukv", "od_conv_w",
             "od_conv_b", "od_ln_g", "od_ln_b", "final_g"]
    local = dict(zip(names, [c_ctx, ada_w, ada_b, norm1_g, norm2_g, w_out, mlp_w1, mlp_w2, ev_w_in, ev_q_norm_g, ev_k_norm_g, ev_sgu_norm_g, ev_sgu_w, ev_sgu_b, od_w_in, od_q_norm_g, od_kv_norm_g, od_w_uq, od_w_ukv, od_conv_w, od_conv_b, od_ln_g, od_ln_b, final_g]))
    mom1 = dict(zip(names, [m_c_ctx, m_ada_w, m_ada_b, m_norm1_g, m_norm2_g, m_w_out, m_mlp_w1, m_mlp_w2, m_ev_w_in, m_ev_q_norm_g, m_ev_k_norm_g, m_ev_sgu_norm_g, m_ev_sgu_w, m_ev_sgu_b, m_od_w_in, m_od_q_norm_g, m_od_kv_norm_g, m_od_w_uq, m_od_w_ukv, m_od_conv_w, m_od_conv_b, m_od_ln_g, m_od_ln_b, m_final_g]))
    mom2 = dict(zip(names, [v_c_ctx, v_ada_w, v_ada_b, v_norm1_g, v_norm2_g, v_w_out, v_mlp_w1, v_mlp_w2, v_ev_w_in, v_ev_q_norm_g, v_ev_k_norm_g, v_ev_sgu_norm_g, v_ev_sgu_w, v_ev_sgu_b, v_od_w_in, v_od_q_norm_g, v_od_kv_norm_g, v_od_w_uq, v_od_w_ukv, v_od_conv_w, v_od_conv_b, v_od_ln_g, v_od_ln_b, v_final_g]))
    bsz, length, d = x.shape
    ctx_len = ctx.shape[1]
    me = _block_index(_mesh_pos())

    shard = {p: local[wn][layer] for p, wn, layer, _, _ in _PIECES}
    by_columns = {p: axis == 1 for p, _, _, axis, _ in _PIECES}
    stages = {s: [p for p, _, _, _, st in _PIECES if st == s] for s in "abc"}
    block_rows = {p: shard[p].shape[1] if by_columns[p] else shard[p].shape[0] for p in shard}

    def pad_block_rows(t, p):
        extra = -block_rows[p] % ROW_ALIGN
        return jnp.pad(t, [(0, 0)] * (t.ndim - 2) + [(0, extra), (0, 0)]) if extra else t

    def travelling(p):
        t = shard[p].astype(BF16)
        return pad_block_rows(t.T if by_columns[p] else t, p)

    mine = {s: [travelling(p) for p in stages[s]] for s in "abc"}
    tiny_pack = _Packing([(n, local[n].size) for n, _ in _SMALL_SHARDED], 8)

    def unpack_weights(s, gathered):
        return {p: g[:, :block_rows[p]].reshape(N_DEV * block_rows[p], g.shape[2]) for p, g in zip(stages[s], gathered)}

    def pack_grads(s, g):
        return [pad_block_rows(g[p].reshape(N_DEV, block_rows[p], g[p].shape[1]), p) for p in stages[s]]

    cond_local = jnp.concatenate([c, c_ctx[None], jnp.zeros((COND_ROWS - bsz - 1, d), F32)], axis=0)
    cond, gathered_tiny, *gathered_a = _all_gather(
        [cond_local, tiny_pack.pack({n: local[n] for n, _ in _SMALL_SHARDED})] + mine["a"], "gather_inputs")
    cond = cond.reshape(N_DEV * COND_ROWS, d)
    silu_op = _rowwise("silu", _f_silu, [(d, F32)], N_DEV * COND_ROWS)
    silu_rows, silu_pullback = jax.vjp(lambda r: silu_op((r[None],), (), (), (), ())[0][0], cond)
    mod_cols = ada_w.shape[2]
    mod_part = jnp.concatenate([_mm(silu_rows, ada_w[i], "nn", (F32,), f"ada{i}_fwd")[0] for i in range(2)], axis=0)
    (mod_all,) = _all_gather([mod_part], "gather_mod")
    mod_all = mod_all.reshape(N_DEV, 2, N_DEV * COND_ROWS, mod_cols)
    modrows = []
    for i in range(2):
        whole = mod_all[:, i].transpose(1, 0, 2).reshape(N_DEV * COND_ROWS, N_DEV * mod_cols) + ada_b[i]
        modrows.append(lax.dynamic_slice_in_dim(whole, me * COND_ROWS, COND_ROWS, axis=0)[:bsz + 1])

    weights_a = unpack_weights("a", gathered_a)
    gather_b, token_b = _push_start(mine["b"], mod_all, "gather_weights_b_start")
    gather_c, token_c = _push_start(mine["c"], token_b, "gather_weights_c_start")
    full = {n: local[n] for n in _REPLICATED}
    for n, axis in _SMALL_SHARDED:
        full[n] = _unshard(tiny_pack.piece(gathered_tiny, n, (N_DEV,)).reshape((N_DEV,) + local[n].shape), axis)

    xall = jnp.concatenate([ctx, x], axis=1)
    modrows0, modrows1 = modrows[0] + token_c[0, 0], modrows[1]
    w_a = dict(weights_a, norm1_g0=norm1_g[0][None], norm2_g0=norm2_g[0][None],
               **{n: full[n] for n in ("ev_q_norm_g", "ev_k_norm_g", "ev_sgu_norm_g", "ev_sgu_w", "ev_sgu_b")})
    (x1, h0), pull_a = jax.vjp(lambda x_, m0, w: _mixer0(x_, m0, w, bsz, length, ctx_len), xall, modrows0, w_a)
    w_b = dict(unpack_weights("b", _push_wait(gather_b, x1, mine["b"], me, "gather_weights_b_wait")), norm1_g1=norm1_g[1][None])
    (x2, h1), pull_b = jax.vjp(lambda x_, h_, m0, m1, w: _mlp0(x_, h_, m0, m1, w, bsz, length, ctx_len), x1, h0, modrows0, modrows1, w_b)
    w_c = dict(unpack_weights("c", _push_wait(gather_c, x2, mine["c"], me, "gather_weights_c_wait")), norm2_g1=norm2_g[1][None],
               **{n: full[n] for n in ("od_q_norm_g", "od_kv_norm_g", "od_conv_w", "od_conv_b", "od_ln_g", "od_ln_b")})
    x4, pull_c = jax.vjp(lambda x_, h_, m1, w: _layer1(x_, h_, m1, w, bsz, length, ctx_len), x2, h1, modrows1, w_c)
    loss_part, dx4, dfinal = _loss_head(x4, loss_target, final_g[None])
    loss = lax.psum(loss_part[0, 0], MESH_AXES)

    dx2, dh1, dmod1_c, g_c = pull_c(dx4)
    grads_c = pack_grads("c", g_c)
    exchange_c, token = _push_start(grads_c, dx2, "exchange_grads_c_start")
    dx1, dh0, dmod0_b, dmod1_b, g_b = pull_b((dx2, dh1 + token[0, 0].astype(dh1.dtype)))
    grads_b = pack_grads("b", g_b)
    exchange_b, token = _push_start(grads_b, dx1, "exchange_grads_b_start")
    dxall, dmod0_a, g_a = pull_a((dx1, dh0 + token[0, 0].astype(dh0.dtype)))
    grad_x = dxall[:, ctx_len:]
    dmodrows = [dmod0_a + dmod0_b, dmod1_b + dmod1_c]
    grads = {n: g[n] for g in (g_a, g_c) for n in g if n in full}
    grads["norm1_g"] = jnp.concatenate([g_a["norm1_g0"], g_b["norm1_g1"]], axis=0)
    grads["norm2_g"] = jnp.concatenate([g_a["norm2_g0"], g_c["norm2_g1"]], axis=0)
    grads["final_g"] = dfinal[0]

    dmod_local = jnp.concatenate([jnp.pad(dm, ((0, COND_ROWS - bsz - 1), (0, 0))) for dm in dmodrows], axis=0)
    (dmod_all,) = _all_gather([dmod_local], "gather_dmod")
    dmod_all = dmod_all.reshape(N_DEV, 2, COND_ROWS, N_DEV * mod_cols)
    grads_a = pack_grads("a", g_a)
    exchange_a, token = _push_start(grads_a, dmod_all, "exchange_grads_a_start")
    dmod_all = dmod_all + token[0, 0]
    reduced = {}
    grad_ada_w, grad_ada_b, dmod_mine = [], [], []
    for i in range(2):
        dmod = dmod_all[:, i].reshape(N_DEV * COND_ROWS, N_DEV * mod_cols)
        grad_ada_b.append(_sum_rows(dmod, f"ada{i}_db")[0])
        dmod_mine.append(lax.dynamic_slice_in_dim(dmod, me * mod_cols, mod_cols, axis=1))
        grad_ada_w.append(_mm(silu_rows, dmod_mine[i], "tn", (F32,), f"ada{i}_dw")[0])
    reduced["ada_w"], reduced["ada_b"] = jnp.stack(grad_ada_w), jnp.stack(grad_ada_b)
    dsilu = _mm(jnp.concatenate(dmod_mine, axis=1), jnp.concatenate([ada_w[0], ada_w[1]], axis=1), "nt", (F32,), "ada_dx")[0]
    (dcond,) = silu_pullback(dsilu)
    grads["c_ctx"] = _sum_rows(dcond.reshape(N_DEV, COND_ROWS, d)[:, bsz], "c_ctx_rows")[0]

    small_names = list(_REPLICATED) + [n for n, _ in _SMALL_SHARDED]
    small_pack = _Packing([(n, full[n].size) for n in small_names], 8)
    (small_all,) = _all_gather([small_pack.pack({n: grads[n].astype(F32) for n in small_names})], "gather_small_grads")
    small_sum = _sum_blocks(small_all, "sum_small_grads")
    for n in _REPLICATED:
        reduced[n] = small_pack.piece(small_sum, n).reshape(local[n].shape)
    for n, axis in _SMALL_SHARDED:
        whole = small_pack.piece(small_sum, n).reshape(full[n].shape)
        reduced[n] = lax.dynamic_slice_in_dim(whole, me * local[n].shape[axis], local[n].shape[axis], axis=axis)

    def own_blocks(blocks):
        return [lax.dynamic_index_in_dim(t, me, 0, keepdims=False) for t in blocks]

    received = {s: _push_wait(handle, small_sum, own_blocks(blocks), me, f"exchange_grads_{s}_wait")
                for s, handle, blocks in (("c", exchange_c, grads_c), ("b", exchange_b, grads_b), ("a", exchange_a, grads_a))}
    piece_grad = {}
    for s in "abc":
        for p, blocks in zip(stages[s], received[s]):
            summed = _sum_blocks(blocks, "sum_grads_" + p)[:block_rows[p]]
            piece_grad[p] = summed.T if by_columns[p] else summed
    for n in ("w_out", "mlp_w1", "mlp_w2"):
        reduced[n] = jnp.stack([piece_grad[p] for p, wn, _, _, _ in _PIECES if wn == n])
    for n in ("ev_w_in", "od_w_in", "od_w_uq", "od_w_ukv"):
        reduced[n] = piece_grad[n][None]

    delta, new_m, new_v = {}, {}, {}
    for n in names:
        delta[n], new_m[n], new_v[n] = _adamw(local[n], reduced[n], mom1[n], mom2[n], "adamw_" + n)
    return (loss, grad_x, *[reduced[n] for n in names], *[delta[n] for n in names], *[new_m[n] for n in names], *[new_v[n] for n in names])
```

```python
import functools
import math

import jax
import jax.numpy as jnp
from jax import lax
from jax.experimental import pallas as pl
from jax.experimental.pallas import tpu as pltpu

F32, BF16 = jnp.float32, jnp.bfloat16

EPS = 1e-6
GRID_W = 64
ROPE_THETA = 10000.0
A_HEAD_DIM, A_Q_HEADS, A_KV_HEADS = 64, 8, 2
B_GROUPS, B_GROUP_DIM, B_CHUNK = 8, 64, 128
C_HEADS, C_NOPE, C_ROPE, C_V, C_Q_RANK, C_KV_RANK = 8, 64, 32, 64, 256, 128
D_CONV = 31
CONV_PAD = D_CONV // 2
N_MOD = 6
N_DEV = 8
MESH_AXES = ("x", "y", "c")

ADAM_LR, ADAM_B1, ADAM_B2, ADAM_EPS, ADAM_WD, ADAM_STEP = 0.001, 0.9, 0.999, 1e-08, 0.01, 10

VMEM_LIMIT = 56 * 1024 * 1024
PACK_COLS = 1024
ROW_ALIGN = 16
SUBLANES = 8
CONV_ROWS = 32
COND_ROWS = 8


def _pcall(body, **kw):
    return pl.pallas_call(body, **kw)


def _params(sem=None):
    return pltpu.CompilerParams(dimension_semantics=sem, vmem_limit_bytes=VMEM_LIMIT)


def _pick(n, cands):
    for c in cands:
        if n % c == 0:
            return c
    return n


def _mm(a, b, mode, out_dtypes, name, epi=None, extras=()):
    if mode == "tn":
        kk, m = a.shape
    else:
        m, kk = a.shape
    n = b.shape[0] if mode == "nt" else b.shape[1]
    tm = _pick(m, (1152, 1024, 896, 768, 512, 256, 128))
    tn = _pick(n, (1024, 896, 768, 512, 256, 128))
    tk = kk if kk <= 1024 else _pick(kk, (2048, 1536, 1024, 896, 768, 512, 256, 128))
    nk = kk // tk
    ne, no = len(extras), len(out_dtypes)
    a_spec = pl.BlockSpec((tk, tm), lambda i, j, k: (k, i)) if mode == "tn" else pl.BlockSpec((tm, tk), lambda i, j, k: (i, k))
    b_spec = pl.BlockSpec((tn, tk), lambda i, j, k: (j, k)) if mode == "nt" else pl.BlockSpec((tk, tn), lambda i, j, k: (k, j))
    t_spec = pl.BlockSpec((tm, tn), lambda i, j, k: (i, j))
    dn = {"nn": ((1,), (0,)), "nt": ((1,), (1,)), "tn": ((0,), (0,))}[mode]

    def body(a_ref, b_ref, *rest):
        extra_refs, out_refs = rest[:ne], rest[ne:ne + no]

        def finish(acc):
            outs = (acc,) if epi is None else epi(acc, *[r[...] for r in extra_refs])
            for r, o in zip(out_refs, outs):
                r[...] = o.astype(r.dtype)

        part = lax.dot_general(a_ref[...].astype(BF16), b_ref[...].astype(BF16), (dn, ((), ())), preferred_element_type=F32)
        if nk == 1:
            finish(part)
        else:
            acc_ref = rest[-1]
            k = pl.program_id(2)

            @pl.when(k == 0)
            def _():
                acc_ref[...] = part

            @pl.when(k > 0)
            def _():
                acc_ref[...] += part

            @pl.when(k == nk - 1)
            def _():
                finish(acc_ref[...])

    outs = _pcall(
        body, name=name, grid=(m // tm, n // tn, nk),
        in_specs=[a_spec, b_spec] + [t_spec] * ne,
        out_specs=[t_spec] * no,
        out_shape=[jax.ShapeDtypeStruct((m, n), d) for d in out_dtypes],
        scratch_shapes=[pltpu.VMEM((tm, tn), F32)] if nk > 1 else [],
        compiler_params=_params(("parallel", "parallel", "arbitrary")),
    )(a, b, *extras)
    return outs


def _linear(name, out_dtype=F32):
    @jax.custom_vjp
    def op(x, wt):
        return _mm(x, wt, "nt", (out_dtype,), name + "_fwd")[0]

    def fwd(x, wt):
        return op(x, wt), (x, wt)

    def bwd(res, dy):
        x, wt = res
        return _mm(dy, wt, "nn", (x.dtype,), name + "_dx")[0], _mm(dy, x, "tn", (wt.dtype,), name + "_dw")[0]

    op.defvjp(fwd, bwd)
    return op


_NT, _NN, _TN = (((1,), (1,)), ((), ())), (((1,), (0,)), ((), ())), (((0,), (0,)), ((), ()))
_ROW_TILES = (1152, 1024, 768, 512, 256, 128)


def _whole(w):
    return pl.BlockSpec(w.shape, lambda i: (0, 0))


def _groups_apply(x, ws, out_dtypes, name):
    n, (m, kk) = len(ws), x.shape
    tm = _pick(m, _ROW_TILES)

    def body(x_ref, *refs):
        a = x_ref[...].astype(BF16)
        for w_ref, o_ref in zip(refs[:n], refs[n:]):
            o_ref[...] = lax.dot_general(a, w_ref[...], _NT, preferred_element_type=F32).astype(o_ref.dtype)

    return _pcall(
        body, name=name, grid=(m // tm,),
        in_specs=[pl.BlockSpec((tm, kk), lambda i: (i, 0))] + [_whole(w) for w in ws],
        out_specs=[pl.BlockSpec((tm, w.shape[0]), lambda i: (i, 0)) for w in ws],
        out_shape=[jax.ShapeDtypeStruct((m, w.shape[0]), dt) for w, dt in zip(ws, out_dtypes)],
        compiler_params=_params(("parallel",)),
    )(x, *ws)


def _groups_sum(xs, ws, out_dtype, name):
    n, m, kk = len(ws), xs[0].shape[0], ws[0].shape[1]
    tm = _pick(m, _ROW_TILES)

    def body(*refs):
        acc = None
        for x_ref, w_ref in zip(refs[:n], refs[n:2 * n]):
            part = lax.dot_general(x_ref[...].astype(BF16), w_ref[...], _NN, preferred_element_type=F32)
            acc = part if acc is None else acc + part
        refs[2 * n][...] = acc.astype(out_dtype)

    return _pcall(
        body, name=name, grid=(m // tm,),
        in_specs=[pl.BlockSpec((tm, w.shape[0]), lambda i: (i, 0)) for w in ws] + [_whole(w) for w in ws],
        out_specs=pl.BlockSpec((tm, kk), lambda i: (i, 0)), out_shape=jax.ShapeDtypeStruct((m, kk), out_dtype),
        compiler_params=_params(("parallel",)),
    )(*xs, *ws)


def _groups_outer(xs, y, ws, name):
    n, (m, kk) = len(ws), y.shape
    tk = _pick(m, (768, 512, 256, 128))
    steps = m // tk

    def body(y_ref, *refs):
        x_refs, o_refs, acc_refs = refs[:n], refs[n:2 * n], refs[2 * n:]
        k = pl.program_id(0)
        b = y_ref[...].astype(BF16)
        for x_ref, o_ref, acc_ref in zip(x_refs, o_refs, acc_refs):
            part = lax.dot_general(x_ref[...].astype(BF16), b, _TN, preferred_element_type=F32)

            @pl.when(k == 0)
            def _(acc_ref=acc_ref, part=part):
                acc_ref[...] = part

            @pl.when(k > 0)
            def _(acc_ref=acc_ref, part=part):
                acc_ref[...] += part

            @pl.when(k == steps - 1)
            def _(acc_ref=acc_ref, o_ref=o_ref):
                o_ref[...] = acc_ref[...].astype(o_ref.dtype)

    return _pcall(
        body, name=name, grid=(steps,),
        in_specs=[pl.BlockSpec((tk, kk), lambda k: (k, 0))] + [pl.BlockSpec((tk, w.shape[0]), lambda k: (k, 0)) for w in ws],
        out_specs=[_whole(w) for w in ws], out_shape=[jax.ShapeDtypeStruct(w.shape, w.dtype) for w in ws],
        scratch_shapes=[pltpu.VMEM(w.shape, F32) for w in ws], compiler_params=_params(("arbitrary",)),
    )(y, *xs)


def _linear_multi(name, out_dtypes):
    @jax.custom_vjp
    def op(x, wts):
        return tuple(_groups_apply(x, wts, out_dtypes, name + "_fwd"))

    def fwd(x, wts):
        return op(x, wts), (x, wts)

    def bwd(res, dys):
        x, wts = res
        return _groups_sum(dys, wts, x.dtype, name + "_dx"), tuple(_groups_outer(dys, x, wts, name + "_dw"))

    op.defvjp(fwd, bwd)
    return op


def _linear_sum(name):
    @jax.custom_vjp
    def op(xs, ws):
        return _groups_sum(xs, ws, F32, name + "_fwd")

    def fwd(xs, ws):
        return op(xs, ws), (xs, ws)

    def bwd(res, dy):
        xs, ws = res
        return tuple(_groups_apply(dy, ws, [x.dtype for x in xs], name + "_dx")), tuple(_groups_outer(xs, dy, ws, name + "_dw"))

    op.defvjp(fwd, bwd)
    return op


def _relu2_epi(acc):
    return jnp.square(jnp.maximum(acc, 0.0)), acc


def _relu2_bwd_epi(acc, a):
    return (acc * (2.0 * jnp.maximum(a.astype(F32), 0.0)),)


def _mlp(name):
    @jax.custom_vjp
    def op(h, w1t, w2):
        s, _ = _mm(h, w1t, "nt", (BF16, BF16), name + "_up", epi=_relu2_epi)
        return _mm(s, w2, "nn", (F32,), name + "_down")[0]

    def fwd(h, w1t, w2):
        s, a = _mm(h, w1t, "nt", (BF16, BF16), name + "_up", epi=_relu2_epi)
        return _mm(s, w2, "nn", (F32,), name + "_down")[0], (h, w1t, w2, s, a)

    def bwd(res, dy):
        h, w1t, w2, s, a = res
        da = _mm(dy, w2, "nt", (BF16,), name + "_ds", epi=_relu2_bwd_epi, extras=(a,))[0]
        dw2 = _mm(s, dy, "tn", (w2.dtype,), name + "_dw2")[0]
        dw1t = _mm(da, h, "tn", (w1t.dtype,), name + "_dw1")[0]
        dh = _mm(da, w1t, "nn", (h.dtype,), name + "_dh")[0]
        return dh, dw1t, dw2

    op.defvjp(fwd, bwd)
    return op


def _two_pass_dot(x, m):
    hi = x.astype(BF16)
    lo = (x - hi.astype(F32)).astype(BF16)
    mb = m.astype(BF16)
    return jnp.dot(hi, mb, preferred_element_type=F32) + jnp.dot(lo, mb, preferred_element_type=F32)


@jax.custom_vjp
def _sym_dot(x, m):
    return _two_pass_dot(x, m)


def _sym_dot_fwd(x, m):
    return _two_pass_dot(x, m), m


def _sym_dot_bwd(m, g):
    return _two_pass_dot(g, m), jnp.zeros_like(m)


_sym_dot.defvjp(_sym_dot_fwd, _sym_dot_bwd)


def _neighbour(x):
    lane = lax.broadcasted_iota(jnp.int32, x.shape, 1)
    return jnp.where(lane % 2 == 0, pltpu.roll(x, x.shape[1] - 1, 1), pltpu.roll(x, 1, 1))


@jax.custom_vjp
def _swap_pairs(x):
    return _neighbour(x)


_swap_pairs.defvjp(lambda x: (_neighbour(x), None), lambda _, g: (_neighbour(g),))


def _lane_group(x, k):
    return x[:, k * B_GROUP_DIM:(k + 1) * B_GROUP_DIM]


@jax.custom_vjp
def _group_mix(ws, v):
    return jnp.concatenate([jnp.dot(w.astype(BF16), _lane_group(v, k).astype(BF16), preferred_element_type=F32)
                            for k, w in enumerate(ws)], axis=1)


def _group_mix_fwd(ws, v):
    return _group_mix(ws, v), (ws, v)


def _group_mix_bwd(res, d):
    ws, v = res
    parts = [(_lane_group(d, k).astype(BF16), _lane_group(v, k).astype(BF16)) for k in range(len(ws))]
    dws = tuple(lax.dot_general(dk, vk, (((1,), (1,)), ((), ())), preferred_element_type=F32) for dk, vk in parts)
    dv = jnp.concatenate([lax.dot_general(w.astype(BF16), dk, (((0,), (0,)), ((), ())), preferred_element_type=F32)
                          for w, (dk, _) in zip(ws, parts)], axis=1)
    return dws, dv


_group_mix.defvjp(_group_mix_fwd, _group_mix_bwd)


def _rowwise(name, f, out_specs, tl, ctx_blocks=0):
    def seg(l, s):
        return jnp.where(l >= ctx_blocks, s - 1, 0) if s > 1 else 0

    def specs(rows, tabs, pers, glbs, consts):
        row_specs = [pl.BlockSpec((1, tl, r.shape[2]), lambda b, l: (b, l, 0)) for r in rows]
        tab_specs = [pl.BlockSpec((tl, t.shape[1]), lambda b, l: (l, 0)) for t in tabs]
        per_specs = [pl.BlockSpec((1, 1, 1, p.shape[3]), functools.partial(lambda b, l, s: (b, seg(l, s), 0, 0), s=p.shape[1])) for p in pers]
        glb_specs = [pl.BlockSpec(g.shape, functools.partial(lambda b, l, nd: (0,) * nd, nd=g.ndim)) for g in glbs]
        const_specs = [pl.BlockSpec(c.shape, functools.partial(lambda b, l, nd: (0,) * nd, nd=c.ndim)) for c in consts]
        return row_specs, tab_specs, per_specs, glb_specs, const_specs

    def load(refs_rows, refs_tabs, refs_pers, refs_glbs, refs_consts):
        return (tuple(r[0].astype(F32) for r in refs_rows), tuple(t[...] for t in refs_tabs),
                tuple(p[0, 0].astype(F32) for p in refs_pers), tuple(g[...].astype(F32) for g in refs_glbs),
                tuple(c[...] for c in refs_consts))

    def call_fwd(rows, tabs, pers, glbs, consts):
        bsz, length = rows[0].shape[:2]
        nr, nt, npp, ng, nc = len(rows), len(tabs), len(pers), len(glbs), len(consts)
        rs, ts, ps, gs, cs = specs(rows, tabs, pers, glbs, consts)

        def body(*refs):
            ins, outs = refs[:nr + nt + npp + ng + nc], refs[nr + nt + npp + ng + nc:]
            r, t, p, g, c = load(ins[:nr], ins[nr:nr + nt], ins[nr + nt:nr + nt + npp], ins[nr + nt + npp:nr + nt + npp + ng], ins[nr + nt + npp + ng:])
            for o_ref, o in zip(outs, f(r, t, p, g, c)):
                o_ref[0] = o.astype(o_ref.dtype)

        return _pcall(
            body, name=name + "_fwd", grid=(bsz, length // tl),
            in_specs=rs + ts + ps + gs + cs,
            out_specs=[pl.BlockSpec((1, tl, w), lambda b, l: (b, l, 0)) for w, _ in out_specs],
            out_shape=[jax.ShapeDtypeStruct((bsz, length, w), d) for w, d in out_specs],
            compiler_params=_params(("parallel", "parallel")),
        )(*rows, *tabs, *pers, *glbs, *consts)

    def call_bwd(rows, tabs, pers, glbs, consts, cts):
        bsz, length = rows[0].shape[:2]
        nr, nt, npp, ng, nc, no = len(rows), len(tabs), len(pers), len(glbs), len(consts), len(cts)
        rs, ts, ps, gs, cs = specs(rows, tabs, pers, glbs, consts)
        n_in = nr + nt + npp + ng + nc

        def body(*refs):
            ins, ct_refs, outs = refs[:n_in], refs[n_in:n_in + no], refs[n_in + no:]
            r, t, p, g, c = load(ins[:nr], ins[nr:nr + nt], ins[nr + nt:nr + nt + npp], ins[nr + nt + npp:nr + nt + npp + ng], ins[nr + nt + npp + ng:])
            _, vjp = jax.vjp(lambda r_, p_, g_: tuple(f(r_, t, p_, g_, c)), r, p, g)
            dr, dp, dg = vjp(tuple(ct[0].astype(F32) for ct in ct_refs))
            dr_refs, dp_refs, dg_refs = outs[:nr], outs[nr:nr + npp], outs[nr + npp:]
            for ref, d in zip(dr_refs, dr):
                ref[0] = d.astype(ref.dtype)
            b, l = pl.program_id(0), pl.program_id(1)
            first_of_segment = (l == 0) | (l == ctx_blocks)
            for ref, d in zip(dp_refs, dp):
                @pl.when(first_of_segment)
                def _(ref=ref, d=d):
                    ref[0, 0] = d

                @pl.when(jnp.logical_not(first_of_segment))
                def _(ref=ref, d=d):
                    ref[0, 0] += d
            first = (b == 0) & (l == 0)
            for ref, d in zip(dg_refs, dg):
                @pl.when(first)
                def _(ref=ref, d=d):
                    ref[...] = d

                @pl.when(jnp.logical_not(first))
                def _(ref=ref, d=d):
                    ref[...] += d

        ct_specs = [pl.BlockSpec((1, tl, w), lambda b, l: (b, l, 0)) for w, _ in out_specs]
        outs = _pcall(
            body, name=name + "_bwd", grid=(bsz, length // tl),
            in_specs=rs + ts + ps + gs + cs + ct_specs,
            out_specs=rs + ps + gs,
            out_shape=[jax.ShapeDtypeStruct(r.shape, r.dtype) for r in rows]
            + [jax.ShapeDtypeStruct(p.shape, F32) for p in pers] + [jax.ShapeDtypeStruct(g.shape, F32) for g in glbs],
            compiler_params=_params(("arbitrary", "arbitrary")),
        )(*rows, *tabs, *pers, *glbs, *consts, *cts)
        return tuple(outs[:nr]), tuple(outs[nr:nr + npp]), tuple(outs[nr + npp:])

    @jax.custom_vjp
    def op(rows, tabs, pers, glbs, consts):
        return tuple(call_fwd(rows, tabs, pers, glbs, consts))

    def fwd(rows, tabs, pers, glbs, consts):
        return op(rows, tabs, pers, glbs, consts), (rows, tabs, pers, glbs, consts)

    def bwd(res, cts):
        rows, tabs, pers, glbs, consts = res
        dr, dp, dg = call_bwd(rows, tabs, pers, glbs, consts, tuple(cts))
        dp = tuple(d.astype(p.dtype) for d, p in zip(dp, pers))
        dg = tuple(d.astype(g.dtype) for d, g in zip(dg, glbs))
        return dr, tuple(jnp.zeros_like(t) for t in tabs), dp, dg, tuple(jnp.zeros_like(c) for c in consts)

    op.defvjp(fwd, bwd)
    return op


def _rms(x, g):
    return x * lax.rsqrt(jnp.mean(x * x, axis=-1, keepdims=True) + EPS) * g


def _f_silu(r, t, p, g, c):
    return (jax.nn.silu(r[0]),)


def _f_modulate(r, t, p, g, c):
    shift, scale = p
    return (_rms(r[0], g[0]) * (1.0 + scale) + shift,)


def _f_res_modulate(r, t, p, g, c):
    x, y = r
    gate, shift, scale = p
    xn = x + gate * y
    return xn, _rms(xn, g[0]) * (1.0 + scale) + shift


def _f_res(r, t, p, g, c):
    return (r[0] + p[0] * r[1],)


def _f_headnorm_rope(r, t, p, g, c):
    x = r[0]
    cos, sin = t
    xn = x * lax.rsqrt(_sym_dot(x * x, c[0]) + EPS) * g[0]
    return (xn * cos + _swap_pairs(xn) * sin,)


def _f_rope(r, t, p, g, c):
    x = r[0]
    cos, sin = t
    return (x * cos + _swap_pairs(x) * sin,)


def _f_rms(r, t, p, g, c):
    return (_rms(r[0], g[0]),)


def _f_sgu_pre(r, t, p, g, c):
    u = jax.nn.gelu(r[0])
    v = jax.nn.gelu(r[1])
    vn = v * lax.rsqrt(_sym_dot(v * v, c[0]) + EPS) * g[0]
    return u, vn


def _f_sgu_mix(r, t, p, g, c):
    u, vn = r
    return (u * (g[B_GROUPS] + _group_mix(tuple(g[:B_GROUPS]), vn)),)


def _f_glu(r, t, p, g, c):
    return (r[0] * jax.nn.sigmoid(r[1]),)


def _f_ln_silu(r, t, p, g, c):
    x = r[0]
    mu = jnp.mean(x, axis=-1, keepdims=True)
    var = jnp.mean(jnp.square(x - mu), axis=-1, keepdims=True)
    return (jax.nn.silu((x - mu) * lax.rsqrt(var + EPS) * g[0] + g[1]),)


LOG2_E = 1.4426950408889634
LN_2 = 0.6931471805599453


def _attention_rows(name, ctx_len, heads, kv_heads):
    grp = heads // kv_heads

    def by_segment(qi, cb, lk, run):
        if cb > 0:
            @pl.when(qi < cb)
            def _():
                run(ctx_len)

            @pl.when(qi >= cb)
            def _():
                run(lk)
        else:
            run(lk)

    def head(ref, h, hd):
        return ref[0, :, h * hd:(h + 1) * hd]

    def scores(q_tile, kk, hd):
        qs = (q_tile.astype(F32) * (LOG2_E * hd ** -0.5)).astype(BF16)
        return qs, lax.dot_general(qs, kk, _NT, preferred_element_type=F32)

    def shapes(q):
        bsz, length, width = q.shape
        tq = math.gcd(_pick(length, (256, 128)), ctx_len) if ctx_len else _pick(length, (256, 128))
        return bsz, length, width // heads, tq, ctx_len // tq

    def call_fwd(q, k, v):
        bsz, length, hd, tq, cb = shapes(q)

        def body(q_ref, k_ref, v_ref, o_ref, lse_ref):
            def run(nk):
                k_all, v_all = k_ref[0, :nk], v_ref[0, :nk]
                outs = []
                for j in range(kv_heads):
                    kk = k_all[:, j * hd:(j + 1) * hd]
                    v_ones = jnp.concatenate([v_all[:, j * hd:(j + 1) * hd], jnp.ones((nk, hd), BF16)], axis=1)
                    for h in range(j * grp, (j + 1) * grp):
                        _, s = scores(head(q_ref, h, hd), kk, hd)
                        m = jnp.max(s, axis=-1, keepdims=True)
                        acc = jnp.dot(jnp.exp2(s - m).astype(BF16), v_ones, preferred_element_type=F32)
                        l = acc[:, hd:hd + 1]
                        outs.append((acc[:, :hd] / l).astype(BF16))
                        lse_ref[0, h] = m + jnp.log2(l)
                o_ref[0] = jnp.concatenate(outs, axis=1)

            by_segment(pl.program_id(1), cb, length, run)

        q_spec = pl.BlockSpec((1, tq, heads * hd), lambda b, i: (b, i, 0))
        kv_spec = pl.BlockSpec((1, length, kv_heads * hd), lambda b, i: (b, 0, 0))
        return _pcall(
            body, name=name + "_fwd", grid=(bsz, length // tq), in_specs=[q_spec, kv_spec, kv_spec],
            out_specs=[q_spec, pl.BlockSpec((1, heads, tq, 1), lambda b, i: (b, 0, i, 0))],
            out_shape=[jax.ShapeDtypeStruct(q.shape, BF16), jax.ShapeDtypeStruct((bsz, heads, length, 1), F32)],
            compiler_params=_params(("parallel", "parallel")),
        )(q, k, v)

    def call_bwd(q, k, v, o, lse, do):
        bsz, length, hd, tq, cb = shapes(q)
        nq = length // tq

        def body(q_ref, k_ref, v_ref, o_ref, lse_ref, do_ref, dq_ref, dk_ref, dv_ref, dk_acc, dv_acc):
            qi = pl.program_id(1)

            @pl.when(qi == 0)
            def _():
                dk_acc[...] = jnp.zeros_like(dk_acc)
                dv_acc[...] = jnp.zeros_like(dv_acc)

            def run(nk):
                k_all, v_all = k_ref[0, :nk], v_ref[0, :nk]
                dqs, dks, dvs = [], [], []
                for j in range(kv_heads):
                    kk, vv = k_all[:, j * hd:(j + 1) * hd], v_all[:, j * hd:(j + 1) * hd]
                    dk_sum = dv_sum = None
                    for h in range(j * grp, (j + 1) * grp):
                        dd = head(do_ref, h, hd)
                        qs, s = scores(head(q_ref, h, hd), kk, hd)
                        p = jnp.exp2(s - lse_ref[0, h])
                        delta = jnp.sum(dd.astype(F32) * head(o_ref, h, hd).astype(F32), axis=-1, keepdims=True)
                        t = (p * (lax.dot_general(dd, vv, _NT, preferred_element_type=F32) - delta)).astype(BF16)
                        dqs.append((jnp.dot(t, kk, preferred_element_type=F32) * hd ** -0.5).astype(BF16))
                        dk_h = lax.dot_general(t, qs, _TN, preferred_element_type=F32)
                        dv_h = lax.dot_general(p.astype(BF16), dd, _TN, preferred_element_type=F32)
                        dk_sum = dk_h if dk_sum is None else dk_sum + dk_h
                        dv_sum = dv_h if dv_sum is None else dv_sum + dv_h
                    dks.append(dk_sum)
                    dvs.append(dv_sum)
                dq_ref[0] = jnp.concatenate(dqs, axis=1)
                dk_acc[:nk] += jnp.concatenate(dks, axis=1)
                dv_acc[:nk] += jnp.concatenate(dvs, axis=1)

            by_segment(qi, cb, length, run)

            @pl.when(qi == nq - 1)
            def _():
                dk_ref[0] = (dk_acc[...] * LN_2).astype(dk_ref.dtype)
                dv_ref[0] = dv_acc[...].astype(dv_ref.dtype)

        q_spec = pl.BlockSpec((1, tq, heads * hd), lambda b, i: (b, i, 0))
        kv_spec = pl.BlockSpec((1, length, kv_heads * hd), lambda b, i: (b, 0, 0))
        lse_spec = pl.BlockSpec((1, heads, tq, 1), lambda b, i: (b, 0, i, 0))
        return _pcall(
            body, name=name + "_bwd", grid=(bsz, nq), in_specs=[q_spec, kv_spec, kv_spec, q_spec, lse_spec, q_spec],
            out_specs=[q_spec, kv_spec, kv_spec],
            out_shape=[jax.ShapeDtypeStruct(q.shape, BF16), jax.ShapeDtypeStruct(k.shape, BF16), jax.ShapeDtypeStruct(v.shape, BF16)],
            scratch_shapes=[pltpu.VMEM(k.shape[1:], F32), pltpu.VMEM(v.shape[1:], F32)],
            compiler_params=_params(("parallel", "arbitrary")),
        )(q, k, v, o, lse, do)

    @jax.custom_vjp
    def op(q, k, v):
        return call_fwd(q, k, v)[0]

    def fwd(q, k, v):
        o, lse = call_fwd(q, k, v)
        return o, (q, k, v, o, lse)

    def bwd(res, do):
        q, k, v, o, lse = res
        dq, dk, dv = call_bwd(q, k, v, o, lse, do)
        return dq, dk.astype(k.dtype), dv.astype(v.dtype)

    op.defvjp(fwd, bwd)
    return op


def _attention_latent_rows(name, heads, nope, rope, dv, heads_forward, heads_backward):
    lanes = 128
    scale = (nope + rope) ** -0.5

    def shapes(q, kn):
        bsz, lq, _ = q.shape
        return bsz, lq, kn.shape[1], _pick(lq, (256, 128))

    def scores(q_tile, kk):
        qs = (q_tile.astype(F32) * (LOG2_E * scale)).astype(BF16)
        return qs, lax.dot_general(qs, kk, _NT, preferred_element_type=F32)

    def call_fwd(q, kn, kr, v):
        bsz, lq, lk, tq = shapes(q, kn)
        heads_per_step = heads_forward

        def body(q_ref, kn_ref, kr_ref, v_ref, o_ref, lse_ref):
            real = lax.broadcasted_iota(jnp.int32, (lk, lanes), 1) < dv
            outs = []
            for h in range(heads_per_step):
                kk = kn_ref[0, :, h * lanes:(h + 1) * lanes] + kr_ref[0]
                v_ones = jnp.where(real, v_ref[0, :, h * lanes:(h + 1) * lanes], jnp.ones((lk, lanes), BF16))
                _, s = scores(q_ref[0, :, h * lanes:(h + 1) * lanes], kk)
                m = jnp.max(s, axis=-1, keepdims=True)
                acc = jnp.dot(jnp.exp2(s - m).astype(BF16), v_ones, preferred_element_type=F32)
                l = acc[:, dv:dv + 1]
                outs.append((acc[:, :dv] / l).astype(BF16))
                lse_ref[0, h] = m + jnp.log2(l)
            o_ref[0] = jnp.concatenate(outs, axis=1)

        wide = heads_per_step * lanes
        return _pcall(
            body, name=name + "_fwd", grid=(bsz, heads // heads_per_step, lq // tq),
            in_specs=[pl.BlockSpec((1, tq, wide), lambda b, g, i: (b, i, g)), pl.BlockSpec((1, lk, wide), lambda b, g, i: (b, 0, g)),
                      pl.BlockSpec((1, lk, lanes), lambda b, g, i: (b, 0, 0)), pl.BlockSpec((1, lk, wide), lambda b, g, i: (b, 0, g))],
            out_specs=[pl.BlockSpec((1, tq, heads_per_step * dv), lambda b, g, i: (b, i, g)),
                       pl.BlockSpec((1, heads_per_step, tq, 1), lambda b, g, i: (b, g, i, 0))],
            out_shape=[jax.ShapeDtypeStruct((bsz, lq, heads * dv), BF16), jax.ShapeDtypeStruct((bsz, heads, lq, 1), F32)],
            compiler_params=_params(("parallel", "parallel", "parallel")),
        )(q, kn, kr, v)

    def call_bwd(q, kn, kr, v, o, lse, do):
        bsz, lq, lk, tq = shapes(q, kn)
        heads_per_step = heads_backward

        def body(q_ref, kn_ref, kr_ref, v_ref, o_ref, lse_ref, do_ref, dq_ref, dkn_ref, dkr_ref, dv_ref, dkn_acc, dkr_acc, dv_acc):
            g, qi = pl.program_id(1), pl.program_id(2)
            last_q = qi == lq // tq - 1

            @pl.when(qi == 0)
            def _():
                dkn_acc[...] = jnp.zeros_like(dkn_acc)
                dv_acc[...] = jnp.zeros_like(dv_acc)

            @pl.when((qi == 0) & (g == 0))
            def _():
                dkr_acc[...] = jnp.zeros_like(dkr_acc)

            dqs, dks, dvs = [], [], []
            for h in range(heads_per_step):
                kk = kn_ref[0, :, h * lanes:(h + 1) * lanes] + kr_ref[0]
                vv = v_ref[0, :, h * lanes:h * lanes + dv]
                dd = do_ref[0, :, h * dv:(h + 1) * dv]
                qs, s = scores(q_ref[0, :, h * lanes:(h + 1) * lanes], kk)
                p = jnp.exp2(s - lse_ref[0, h])
                delta = jnp.sum(dd.astype(F32) * o_ref[0, :, h * dv:(h + 1) * dv].astype(F32), axis=-1, keepdims=True)
                t = (p * (lax.dot_general(dd, vv, _NT, preferred_element_type=F32) - delta)).astype(BF16)
                dqs.append((jnp.dot(t, kk, preferred_element_type=F32) * scale).astype(BF16))
                dks.append(lax.dot_general(t, qs, _TN, preferred_element_type=F32) * LN_2)
                dv_h = lax.dot_general(p.astype(BF16), dd, _TN, preferred_element_type=F32)
                dvs.append(jnp.concatenate([dv_h, jnp.zeros((lk, lanes - dv), F32)], axis=1))
            dq_ref[0] = jnp.concatenate(dqs, axis=1)
            dkn_acc[...] += jnp.concatenate(dks, axis=1)
            dv_acc[...] += jnp.concatenate(dvs, axis=1)
            shared = dks[0]
            for d_h in dks[1:]:
                shared = shared + d_h
            dkr_acc[...] += shared

            @pl.when(last_q)
            def _():
                dkn_ref[0] = dkn_acc[...].astype(dkn_ref.dtype)
                dv_ref[0] = dv_acc[...].astype(dv_ref.dtype)

            @pl.when(last_q & (g == heads // heads_per_step - 1))
            def _():
                dkr_ref[0] = dkr_acc[...].astype(dkr_ref.dtype)

        wide = heads_per_step * lanes
        q_spec = pl.BlockSpec((1, tq, wide), lambda b, g, i: (b, i, g))
        k_spec = pl.BlockSpec((1, lk, wide), lambda b, g, i: (b, 0, g))
        kr_spec = pl.BlockSpec((1, lk, lanes), lambda b, g, i: (b, 0, 0))
        o_spec = pl.BlockSpec((1, tq, heads_per_step * dv), lambda b, g, i: (b, i, g))
        lse_spec = pl.BlockSpec((1, heads_per_step, tq, 1), lambda b, g, i: (b, g, i, 0))
        return _pcall(
            body, name=name + "_bwd", grid=(bsz, heads // heads_per_step, lq // tq),
            in_specs=[q_spec, k_spec, kr_spec, k_spec, o_spec, lse_spec, o_spec],
            out_specs=[q_spec, k_spec, kr_spec, k_spec],
            out_shape=[jax.ShapeDtypeStruct(q.shape, BF16), jax.ShapeDtypeStruct(kn.shape, BF16), jax.ShapeDtypeStruct(kr.shape, BF16),
                       jax.ShapeDtypeStruct(v.shape, BF16)],
            scratch_shapes=[pltpu.VMEM((lk, wide), F32), pltpu.VMEM((lk, lanes), F32), pltpu.VMEM((lk, wide), F32)],
            compiler_params=_params(("parallel", "arbitrary", "arbitrary")),
        )(q, kn, kr, v, o, lse, do)

    @jax.custom_vjp
    def op(q, kn, kr, v):
        return call_fwd(q, kn, kr, v)[0]

    def fwd(q, kn, kr, v):
        o, lse = call_fwd(q, kn, kr, v)
        return o, (q, kn, kr, v, o, lse)

    def bwd(res, do):
        q, kn, kr, v, o, lse = res
        dq, dkn, dkr, dv_ = call_bwd(q, kn, kr, v, o, lse, do)
        return dq, dkn.astype(kn.dtype), dkr.astype(kr.dtype), dv_.astype(v.dtype)

    op.defvjp(fwd, bwd)
    return op


def _conv_call(ypad, taps, name):
    bsz, lp, ch = ypad.shape
    length = lp - 2 * ROW_ALIGN
    tl = _pick(length, (256, 128))

    def body(y_ref, w_ref, o_ref):
        base = pl.multiple_of(pl.program_id(1) * tl, tl)
        for r0 in range(0, tl, CONV_ROWS):
            win = y_ref[0, pl.ds(base + r0, CONV_ROWS + 2 * ROW_ALIGN), :]
            acc = jnp.broadcast_to(w_ref[pl.ds(D_CONV, 1), :], (CONV_ROWS, ch))
            for shift in range(SUBLANES):
                shifted = win[shift:shift + CONV_ROWS + 2 * ROW_ALIGN - SUBLANES, :]
                for k in range(shift, D_CONV, SUBLANES):
                    acc = acc + shifted[k - shift:k - shift + CONV_ROWS, :] * w_ref[pl.ds(k, 1), :]
            o_ref[0, pl.ds(r0, CONV_ROWS), :] = acc

    return _pcall(
        body, name=name, grid=(bsz, length // tl),
        in_specs=[pl.BlockSpec((1, lp, ch), lambda b, l: (b, 0, 0)), pl.BlockSpec((D_CONV + 1, ch), lambda b, l: (0, 0))],
        out_specs=pl.BlockSpec((1, tl, ch), lambda b, l: (b, l, 0)),
        out_shape=jax.ShapeDtypeStruct((bsz, length, ch), F32),
        compiler_params=_params(("parallel", "parallel")),
    )(ypad, taps)


def _conv_dw_call(ypad, dout, name):
    bsz, lp, ch = ypad.shape
    length = lp - 2 * ROW_ALIGN
    tl = _pick(length, (256, 128))

    def body(y_ref, d_ref, o_ref):
        b, l = pl.program_id(0), pl.program_id(1)

        @pl.when((b == 0) & (l == 0))
        def _():
            o_ref[...] = jnp.zeros_like(o_ref)

        base = pl.multiple_of(l * tl, tl)
        win = y_ref[0, pl.ds(base, tl + 2 * ROW_ALIGN), :]
        dd = d_ref[0]
        for shift in range(SUBLANES):
            shifted = win[shift:shift + tl + 2 * ROW_ALIGN - SUBLANES, :]
            for k in range(shift, D_CONV, SUBLANES):
                o_ref[pl.ds(k, 1), :] += jnp.sum(shifted[k - shift:k - shift + tl, :] * dd, axis=0, keepdims=True)
        o_ref[pl.ds(D_CONV, 1), :] += jnp.sum(dd, axis=0, keepdims=True)

    return _pcall(
        body, name=name, grid=(bsz, length // tl),
        in_specs=[pl.BlockSpec((1, lp, ch), lambda b, l: (b, 0, 0)), pl.BlockSpec((1, tl, ch), lambda b, l: (b, l, 0))],
        out_specs=pl.BlockSpec((D_CONV + 1, ch), lambda b, l: (0, 0)),
        out_shape=jax.ShapeDtypeStruct((D_CONV + 1, ch), F32),
        compiler_params=_params(("arbitrary", "arbitrary")),
    )(ypad, dout)


def _pad_rows(y):
    return jnp.pad(y, ((0, 0), (CONV_PAD, 2 * ROW_ALIGN - CONV_PAD), (0, 0)))


@jax.custom_vjp
def _dwconv(y, taps):
    return _conv_call(_pad_rows(y), taps, "conv_fwd")


def _dwconv_fwd(y, taps):
    return _dwconv(y, taps), (y, taps)


def _dwconv_bwd(res, dout):
    y, taps = res
    flipped = jnp.concatenate([taps[:D_CONV][::-1], jnp.zeros_like(taps[D_CONV:])], axis=0)
    dy = _conv_call(_pad_rows(dout), flipped, "conv_dy")
    dtaps = _conv_dw_call(_pad_rows(y), dout, "conv_dw")
    return dy, dtaps


_dwconv.defvjp(_dwconv_fwd, _dwconv_bwd)


def _loss_head(x, target, g):
    bsz, length, d = x.shape
    tl = _pick(length, (256, 128))

    def f(xb, tb, gb):
        err = _rms(xb, gb) - tb
        return 0.5 * jnp.sum(jnp.sum(err * err, axis=-1, keepdims=True), axis=0, keepdims=True) / d

    def body(x_ref, t_ref, g_ref, loss_ref, dx_ref, dg_ref):
        val, vjp = jax.vjp(lambda xb, gb: f(xb, t_ref[0], gb), x_ref[0], g_ref[...])
        dx, dg = vjp(jnp.ones((1, 1), F32))
        dx_ref[0] = dx
        first = (pl.program_id(0) == 0) & (pl.program_id(1) == 0)

        @pl.when(first)
        def _():
            loss_ref[...] = val
            dg_ref[...] = dg

        @pl.when(jnp.logical_not(first))
        def _():
            loss_ref[...] += val
            dg_ref[...] += dg

    row = pl.BlockSpec((1, tl, d), lambda b, l: (b, l, 0))
    return _pcall(
        body, name="loss_head", grid=(bsz, length // tl),
        in_specs=[row, row, pl.BlockSpec((1, d), lambda b, l: (0, 0))],
        out_specs=[pl.BlockSpec((1, 1), lambda b, l: (0, 0)), row, pl.BlockSpec((1, d), lambda b, l: (0, 0))],
        out_shape=[jax.ShapeDtypeStruct((1, 1), F32), jax.ShapeDtypeStruct(x.shape, F32), jax.ShapeDtypeStruct((1, d), F32)],
        compiler_params=_params(("arbitrary", "arbitrary")),
    )(x, target, g)


def _adamw(w, g, m, v, name):
    shape = w.shape
    cols = shape[-1]
    rows = w.size // cols
    tr = _pick(rows, (512, 256, 128))
    w2, g2, m2, v2 = (t.reshape(rows, cols) for t in (w, g, m, v))

    def body(w_ref, g_ref, m_ref, v_ref, d_ref, nm_ref, nv_ref):
        gg = g_ref[...]
        nm = ADAM_B1 * m_ref[...] + (1.0 - ADAM_B1) * gg
        nv = ADAM_B2 * v_ref[...] + (1.0 - ADAM_B2) * jnp.square(gg)
        m_hat = nm / (1.0 - ADAM_B1 ** ADAM_STEP)
        v_hat = nv / (1.0 - ADAM_B2 ** ADAM_STEP)
        d_ref[...] = -ADAM_LR * (m_hat / (jnp.sqrt(v_hat) + ADAM_EPS) + ADAM_WD * w_ref[...])
        nm_ref[...] = nm
        nv_ref[...] = nv

    spec = pl.BlockSpec((tr, cols), lambda i: (i, 0))
    outs = _pcall(
        body, name=name, grid=(rows // tr,), in_specs=[spec] * 4, out_specs=[spec] * 3,
        out_shape=[jax.ShapeDtypeStruct((rows, cols), F32)] * 3,
        compiler_params=_params(("parallel",)),
    )(w2, g2, m2, v2)
    return tuple(o.reshape(shape) for o in outs)


def _mesh_pos():
    return lax.axis_index("x"), lax.axis_index("y"), lax.axis_index("c")


_RELATIONS = [(dx, dy, dc) for dx in (0, 1) for dy in (0, 1) for dc in (0, 1)][1:]


def _peer(pos, rel):
    return tuple(jnp.where(r == 1, 1 - p, p) if r else p for p, r in zip(pos, rel))


def _block_index(pos):
    return 4 * pos[0] + 2 * pos[1] + pos[2]


_HBM = pl.BlockSpec(memory_space=pltpu.HBM)


def _all_gather(xs, name):
    n = len(xs)

    def body(*refs):
        x_refs, out_refs, (send_sems, recv_sems, local_sems) = refs[:n], refs[n:2 * n], refs[2 * n:]
        x_, y_, c_ = _mesh_pos()
        me, sibling = (x_, y_, c_), (x_, y_, 1 - c_)
        chips = [(1 - x_, y_), (x_, 1 - y_), (1 - x_, 1 - y_)]

        def copy(t, k, block, to, own=False):
            slot = out_refs[t].at[_block_index(block)]
            return pltpu.make_async_remote_copy(
                src_ref=x_refs[t] if own else slot, dst_ref=slot, send_sem=send_sems.at[7 * t + k], recv_sem=recv_sems.at[7 * t + k],
                device_id=to, device_id_type=pl.DeviceIdType.MESH)

        mine = [pltpu.make_async_copy(x_refs[t], out_refs[t].at[_block_index(me)], local_sems.at[t]) for t in range(n)]
        first = [[copy(t, 0, me, sibling, own=True)] + [copy(t, 1 + j, me, (*chip, c_), own=True) for j, chip in enumerate(chips)]
                 for t in range(n)]
        passed = [[copy(t, 4 + j, (*chip, c_), sibling) for j, chip in enumerate(chips)] for t in range(n)]
        for t in range(n):
            mine[t].start()
            for cp in first[t]:
                cp.start()
        for t in range(n):
            for j, chip in enumerate(chips):
                copy(t, 1 + j, (*chip, c_), me).wait_recv()
                passed[t][j].start()
        for t in range(n):
            copy(t, 0, sibling, me).wait_recv()
            for j, chip in enumerate(chips):
                copy(t, 4 + j, (*chip, 1 - c_), me).wait_recv()
            for cp in first[t] + passed[t]:
                cp.wait_send()
            mine[t].wait()

    return _pcall(
        body, name=name, in_specs=[_HBM] * n, out_specs=[_HBM] * n,
        out_shape=[jax.ShapeDtypeStruct((N_DEV,) + x.shape, x.dtype) for x in xs],
        scratch_shapes=[pltpu.SemaphoreType.DMA((7 * n,)), pltpu.SemaphoreType.DMA((7 * n,)), pltpu.SemaphoreType.DMA((n,))],
    )(*xs)


_SEM = pl.BlockSpec(memory_space=pltpu.SEMAPHORE)
_EFFECT = pltpu.SideEffectType.DATAFLOW_SIDE_EFFECTING


def _push_start(srcs, after, name):
    n = len(srcs)
    lands = [lax.empty((N_DEV,) + s.shape[-2:], s.dtype) for s in srcs]

    def body(*refs):
        src_refs, land_refs = refs[:n], refs[n:2 * n]
        send_sems, recv_sems, token = refs[2 * n + 1:3 * n + 1], refs[3 * n + 1:4 * n + 1], refs[-1]
        me = _mesh_pos()
        for t in range(n):
            for rel in _RELATIONS:
                peer = _peer(me, rel)
                pltpu.make_async_remote_copy(
                    src_ref=src_refs[t].at[_block_index(peer)] if srcs[t].ndim == 3 else src_refs[t], dst_ref=land_refs[t].at[_block_index(me)],
                    send_sem=send_sems[t], recv_sem=recv_sems[t], device_id=peer, device_id_type=pl.DeviceIdType.MESH).start()
        token[...] = jnp.zeros_like(token)

    outs = _pcall(
        body, name=name,
        out_shape=[pltpu.SemaphoreType.DMA(())] * (2 * n) + [pltpu.HBM(s.shape, s.dtype) for s in srcs]
        + [pltpu.HBM(l.shape, l.dtype) for l in lands] + [jax.ShapeDtypeStruct((8, 128), F32)],
        in_specs=[_HBM] * (2 * n) + [pl.BlockSpec(memory_space=pl.ANY)],
        out_specs=[_SEM] * (2 * n) + [_HBM] * (2 * n) + [pl.BlockSpec(memory_space=pltpu.VMEM)],
        input_output_aliases={i: 2 * n + i for i in range(2 * n)}, compiler_params=pltpu.CompilerParams(has_side_effects=_EFFECT),
    )(*[pltpu.with_memory_space_constraint(t, pltpu.HBM) for t in list(srcs) + lands], after)
    return (outs[:n], outs[n:2 * n], outs[2 * n:3 * n], outs[3 * n:4 * n]), outs[-1]


def _push_wait(handle, after, owns, me, name):
    send_sems, recv_sems, src_thrus, land_thrus = handle
    n = len(land_thrus)

    def body(*refs):
        land_refs, sends, recvs = refs[n:2 * n], refs[2 * n:3 * n], refs[3 * n:4 * n]
        for t in range(n):
            seven = land_refs[t].at[pl.ds(0, N_DEV - 1)]
            all_seven = pltpu.make_async_remote_copy(src_ref=seven, dst_ref=seven, send_sem=sends[t], recv_sem=recvs[t],
                                                     device_id=_mesh_pos(), device_id_type=pl.DeviceIdType.MESH)
            all_seven.wait_send()
            all_seven.wait_recv()

    outs = _pcall(
        body, name=name,
        out_shape=[pltpu.HBM(t.shape, t.dtype) for t in list(src_thrus) + list(land_thrus)],
        in_specs=[_HBM] * (2 * n) + [_SEM] * (2 * n) + [pl.BlockSpec(memory_space=pl.ANY)], out_specs=[_HBM] * (2 * n),
        input_output_aliases={i: i for i in range(2 * n)}, compiler_params=pltpu.CompilerParams(has_side_effects=_EFFECT),
    )(*src_thrus, *land_thrus, *send_sems, *recv_sems, after)
    return [lax.dynamic_update_slice(land, own[None], (me, 0, 0)) for land, own in zip(outs[n:], owns)]


def _sum_blocks(p, name):
    n, rows, cols = p.shape
    tr = _pick(rows, (256, 128, 64, 32, 16, 8))

    def body(p_ref, o_ref):
        acc = p_ref[0].astype(F32)
        for s in range(1, n):
            acc = acc + p_ref[s].astype(F32)
        o_ref[...] = acc

    return _pcall(
        body, name=name, grid=(rows // tr,),
        in_specs=[pl.BlockSpec((n, tr, cols), lambda i: (0, i, 0))], out_specs=pl.BlockSpec((tr, cols), lambda i: (i, 0)),
        out_shape=jax.ShapeDtypeStruct((rows, cols), F32), compiler_params=_params(("parallel",)),
    )(p)


def _sum_rows(t, name):
    def body(t_ref, o_ref):
        o_ref[...] = jnp.sum(t_ref[...], axis=0, keepdims=True)

    return _pcall(body, name=name, out_shape=jax.ShapeDtypeStruct((1, t.shape[1]), F32))(t)


class _Packing:
    def __init__(self, sizes, align):
        self.offsets, self.sizes, self.align = {}, dict(sizes), align
        row = 0
        for name, size in sizes:
            self.offsets[name] = row
            row += -(-size // (align * PACK_COLS)) * align
        self.rows = row

    def pack(self, pieces):
        return self.pack_blocks({n: pieces[n].reshape(1, -1) for n in self.sizes})[0]

    def pack_blocks(self, pieces):
        out = []
        for n, size in self.sizes.items():
            padded = -(-size // (self.align * PACK_COLS)) * self.align * PACK_COLS
            out.append(jnp.pad(pieces[n], ((0, 0), (0, padded - size))).reshape(pieces[n].shape[0], -1, PACK_COLS))
        return jnp.concatenate(out, axis=1)

    def piece(self, packed, name, lead=()):
        start, size = self.offsets[name], self.sizes[name]
        nrow = -(-size // (self.align * PACK_COLS)) * self.align
        sl = packed[..., start:start + nrow, :]
        return sl.reshape(lead + (nrow * PACK_COLS,))[..., :size]


_PIECES = (("ev_w_in", "ev_w_in", 0, 1, "a"), ("w_out0", "w_out", 0, 0, "a"),
           ("mlp_w1_0", "mlp_w1", 0, 1, "b"), ("mlp_w2_0", "mlp_w2", 0, 0, "b"),
           ("od_w_in", "od_w_in", 0, 1, "c"), ("od_w_uq", "od_w_uq", 0, 1, "c"), ("od_w_ukv", "od_w_ukv", 0, 1, "c"),
           ("w_out1", "w_out", 1, 0, "c"), ("mlp_w1_1", "mlp_w1", 1, 1, "c"), ("mlp_w2_1", "mlp_w2", 1, 0, "c"))
_SMALL_SHARDED = (("od_q_norm_g", 1), ("od_conv_w", 2), ("od_conv_b", 1), ("od_ln_g", 1), ("od_ln_b", 1))
_REPLICATED = ("c_ctx", "norm1_g", "norm2_g", "ev_q_norm_g", "ev_k_norm_g", "ev_sgu_norm_g", "ev_sgu_w", "ev_sgu_b",
               "od_kv_norm_g", "final_g")


def _unshard(blocks, axis):
    moved = jnp.moveaxis(blocks, 0, axis)
    shape = moved.shape
    return moved.reshape(shape[:axis] + (shape[axis] * shape[axis + 1],) + shape[axis + 2:])


def _group_mean_matrix(width, group):
    idx = jnp.arange(width) // group
    return (idx[:, None] == idx[None, :]).astype(F32) / group


def _angles(length, d_rot):
    rows = length // GRID_W
    row = jnp.broadcast_to(jnp.arange(rows)[:, None], (rows, GRID_W)).reshape(-1).astype(F32)
    col = jnp.broadcast_to(jnp.arange(GRID_W)[None, :], (rows, GRID_W)).reshape(-1).astype(F32)
    d_axis = d_rot // 2
    inv = ROPE_THETA ** (-jnp.arange(0, d_axis, 2, dtype=F32) / d_axis)
    return jnp.concatenate([row[:, None] * inv, col[:, None] * inv], axis=-1)


def _rope_tables(length, d_rot, head_dim, heads, ctx_len, tail=0):
    ang = _angles(length, d_rot)
    cos = jnp.repeat(jnp.cos(ang), 2, axis=1)
    sin = jnp.repeat(jnp.sin(ang), 2, axis=1) * jnp.tile(jnp.array([-1.0, 1.0], F32), d_rot // 2)
    keep = head_dim - d_rot - tail
    cos = jnp.concatenate([jnp.ones((length, keep), F32), cos, jnp.ones((length, tail), F32)], axis=1)
    sin = jnp.concatenate([jnp.zeros((length, keep), F32), sin, jnp.zeros((length, tail), F32)], axis=1)
    cos, sin = jnp.tile(cos, (1, heads)), jnp.tile(sin, (1, heads))
    cos = jnp.concatenate([jnp.ones((ctx_len, cos.shape[1]), F32), cos], axis=0)
    sin = jnp.concatenate([jnp.zeros((ctx_len, sin.shape[1]), F32), sin], axis=0)
    return cos, sin


def _segment_params(mod, bsz):
    parts = jnp.split(mod, N_MOD, axis=-1)
    out = []
    for part in parts:
        lat = part[:bsz]
        ctx = jnp.broadcast_to(part[bsz:bsz + 1], lat.shape)
        out.append(jnp.stack([ctx, lat], axis=1)[:, :, None, :])
    return out


def _flat(t):
    return t.reshape(-1, t.shape[-1])


def _sequence_rowwise(ctx_len):
    tl = math.gcd(256, ctx_len)

    def make(name, f, out_specs, rows_per_block=tl, ctx_blocks=ctx_len // tl):
        return _rowwise(name, f, out_specs, rows_per_block, ctx_blocks)

    return make


def _mixer0(xall, modrows0, w, bsz, length, ctx_len):
    d = xall.shape[-1]
    total = ctx_len + length
    rowwise, flat = _sequence_rowwise(ctx_len), _flat
    sh1, sc1, g1, sh2, sc2, _ = _segment_params(modrows0, bsz)
    (h,) = rowwise("mod0", _f_modulate, [(d, BF16)])((xall,), (), (sh1, sc1), (w["norm1_g0"],), ())
    ev_q, ev_kv = A_Q_HEADS * A_HEAD_DIM, A_KV_HEADS * A_HEAD_DIM
    half = B_GROUPS * B_GROUP_DIM
    groups = tuple(jnp.split(w["ev_w_in"], [ev_q, ev_q + ev_kv, ev_q + 2 * ev_kv, ev_q + 2 * ev_kv + half], axis=0))
    qp, kp, vp, zu, zv = [t.reshape(bsz, total, -1) for t in _linear_multi("ev_in", (F32, F32, BF16, F32, F32))(flat(h), groups)]
    cos_q, sin_q = _rope_tables(length, A_HEAD_DIM, A_HEAD_DIM, A_Q_HEADS, ctx_len)
    cos_k, sin_k = cos_q[:, :ev_kv], sin_q[:, :ev_kv]
    (q,) = rowwise("ev_q", _f_headnorm_rope, [(ev_q, BF16)])(
        (qp,), (cos_q, sin_q), (), (jnp.tile(w["ev_q_norm_g"][0], A_Q_HEADS)[None],), (_group_mean_matrix(ev_q, A_HEAD_DIM),))
    (k,) = rowwise("ev_k", _f_headnorm_rope, [(ev_kv, BF16)])(
        (kp,), (cos_k, sin_k), (), (jnp.tile(w["ev_k_norm_g"][0], A_KV_HEADS)[None],), (_group_mean_matrix(ev_kv, A_HEAD_DIM),))
    o_att = _attention_rows("gqa", ctx_len, A_Q_HEADS, A_KV_HEADS)(q, k, vp)
    u, vn = rowwise("sgu_pre", _f_sgu_pre, [(half, F32), (half, BF16)])(
        (zu, zv), (), (), (w["ev_sgu_norm_g"][0].reshape(1, half),), (_group_mean_matrix(half, B_GROUP_DIM),))
    bias = jnp.repeat(w["ev_sgu_b"][0].T, B_GROUP_DIM, axis=1)
    (o_sgu,) = rowwise("sgu_mix", _f_sgu_mix, [(half, BF16)], rows_per_block=B_CHUNK, ctx_blocks=0)(
        (u, vn), (), (), tuple(w["ev_sgu_w"][0][g] for g in range(B_GROUPS)) + (bias,), ())
    y = _linear_sum("out0")((flat(o_att), flat(o_sgu)), tuple(jnp.split(w["w_out0"], 2, axis=0))).reshape(bsz, total, d)
    x1, h = rowwise("res_mod0a", _f_res_modulate, [(d, F32), (d, BF16)])((xall, y), (), (g1, sh2, sc2), (w["norm2_g0"],), ())
    return x1, h


def _mlp0(x1, h, modrows0, modrows1, w, bsz, length, ctx_len):
    d = x1.shape[-1]
    total = ctx_len + length
    g2 = _segment_params(modrows0, bsz)[5]
    sh1, sc1 = _segment_params(modrows1, bsz)[:2]
    y = _mlp("mlp0")(_flat(h), w["mlp_w1_0"], w["mlp_w2_0"]).reshape(bsz, total, d)
    return _sequence_rowwise(ctx_len)("res_mod0b", _f_res_modulate, [(d, F32), (d, BF16)])((x1, y), (), (g2, sh1, sc1), (w["norm1_g1"],), ())


def _layer1(x2, h, modrows1, w, bsz, length, ctx_len):
    d = x2.shape[-1]
    total = ctx_len + length
    half = B_GROUPS * B_GROUP_DIM
    rowwise, flat = _sequence_rowwise(ctx_len), _flat
    _, _, g1n, sh2n, sc2n, g2n = _segment_params(modrows1, bsz)
    g_cq, g_ckv, g_kr, g_za, g_zg = jnp.split(w["od_w_in"], [C_Q_RANK, C_Q_RANK + C_KV_RANK, C_Q_RANK + C_KV_RANK + C_ROPE,
                                                             C_Q_RANK + C_KV_RANK + C_ROPE + half], axis=0)
    lanes, c_qk = 128, C_NOPE + C_ROPE
    g_kr = jnp.pad(g_kr, ((C_NOPE, lanes - c_qk), (0, 0)))
    cq, ckv, kr, za, zg = [t.reshape(bsz, total, -1) for t in _linear_multi("od_in", (F32,) * 5)(flat(h), (g_cq, g_ckv, g_kr, g_za, g_zg))]
    lat = slice(ctx_len, total)
    lat_tl = math.gcd(256, length)
    (cqn,) = _rowwise("od_qn", _f_rms, [(C_Q_RANK, BF16)], lat_tl)((cq[:, lat],), (), (), (w["od_q_norm_g"],), ())
    w_uq = jnp.pad(w["od_w_uq"].reshape(C_HEADS, c_qk, C_Q_RANK), ((0, 0), (0, lanes - c_qk), (0, 0))).reshape(C_HEADS * lanes, C_Q_RANK)
    qf = _linear("od_uq")(flat(cqn), w_uq).reshape(bsz, length, C_HEADS * lanes)
    cos_q, sin_q = _rope_tables(length, C_ROPE, lanes, C_HEADS, 0, tail=lanes - c_qk)
    (q,) = _rowwise("od_qrope", _f_rope, [(C_HEADS * lanes, BF16)], lat_tl)((qf,), (cos_q, sin_q), (), (), ())
    (ckvn,) = rowwise("od_kvn", _f_rms, [(C_KV_RANK, BF16)])((ckv,), (), (), (w["od_kv_norm_g"],), ())
    per_head = w["od_w_ukv"].reshape(C_HEADS, C_NOPE + C_V, C_KV_RANK)
    w_kn = jnp.pad(per_head[:, :C_NOPE], ((0, 0), (0, lanes - C_NOPE), (0, 0))).reshape(C_HEADS * lanes, C_KV_RANK)
    w_v = jnp.pad(per_head[:, C_NOPE:], ((0, 0), (0, lanes - C_V), (0, 0))).reshape(C_HEADS * lanes, C_KV_RANK)
    kn, vv = [t.reshape(bsz, total, -1) for t in _linear_multi("od_ukv", (BF16, BF16))(flat(ckvn), (w_kn, w_v))]
    cos_r, sin_r = _rope_tables(length, C_ROPE, lanes, 1, ctx_len, tail=lanes - c_qk)
    (krr,) = rowwise("od_krope", _f_rope, [(lanes, BF16)])((kr,), (cos_r, sin_r), (), (), ())
    o_att = _attention_latent_rows("mla", C_HEADS, C_NOPE, C_ROPE, C_V, 4, 2)(q, kn, krr, vv)
    (glu,) = _rowwise("glu", _f_glu, [(half, F32)], lat_tl)((za[:, lat], zg[:, lat]), (), (), (), ())
    taps = jnp.concatenate([w["od_conv_w"][0], w["od_conv_b"]], axis=0)
    conv = _dwconv(glu, taps)
    (o_conv,) = _rowwise("ln_silu", _f_ln_silu, [(half, BF16)], lat_tl)((conv,), (), (), (w["od_ln_g"], w["od_ln_b"]), ())
    y = _linear_sum("out1")((flat(o_att), flat(o_conv)), tuple(jnp.split(w["w_out1"], 2, axis=0))).reshape(bsz, length, d)
    lat_param = lambda p: p[:, 1:]
    x3, h = _rowwise("res_mod1a", _f_res_modulate, [(d, F32), (d, BF16)], lat_tl)(
        (x2[:, lat], y), (), (lat_param(g1n), lat_param(sh2n), lat_param(sc2n)), (w["norm2_g1"],), ())
    y = _mlp("mlp1")(flat(h), w["mlp_w1_1"], w["mlp_w2_1"]).reshape(bsz, length, d)
    (x4,) = _rowwise("res1b", _f_res, [(d, F32)], lat_tl)((x3, y), (), (lat_param(g2n),), (), ())
    return x4


def kernel(x, c, ctx, c_ctx, ada_w, ada_b, norm1_g, norm2_g, w_out, mlp_w1, mlp_w2, ev_w_in, ev_q_norm_g, ev_k_norm_g, ev_sgu_norm_g, ev_sgu_w, ev_sgu_b, od_w_in, od_q_norm_g, od_kv_norm_g, od_w_uq, od_w_ukv, od_conv_w, od_conv_b, od_ln_g, od_ln_b, final_g, loss_target, m_c_ctx, m_ada_w, m_ada_b, m_norm1_g, m_norm2_g, m_w_out, m_mlp_w1, m_mlp_w2, m_ev_w_in, m_ev_q_norm_g, m_ev_k_norm_g, m_ev_sgu_norm_g, m_ev_sgu_w, m_ev_sgu_b, m_od_w_in, m_od_q_norm_g, m_od_kv_norm_g, m_od_w_uq, m_od_w_ukv, m_od_conv_w, m_od_conv_b, m_od_ln_g, m_od_ln_b, m_final_g, v_c_ctx, v_ada_w, v_ada_b, v_norm1_g, v_norm2_g, v_w_out, v_mlp_w1, v_mlp_w2, v_ev_w_in, v_ev_q_norm_g, v_ev_k_norm_g, v_ev_sgu_norm_g, v_ev_sgu_w, v_ev_sgu_b, v_od_w_in, v_od_q_norm_g, v_od_kv_norm_g, v_od_w_uq, v_od_w_ukv, v_od_conv_w, v_od_conv_b, v_od_ln_g, v_od_ln_b, v_final_g):
    names = ["c_ctx", "ada_w", "ada_b", "norm1_g", "norm2_g", "w_out", "mlp_w1", "mlp_w2", "ev_w_in", "ev_q_norm_g", "ev_k_norm_g",
             "ev_sgu_norm_g", "ev_sgu_w", "ev_sgu_b", "od_w_in", "od_q_norm_g", "od_kv_norm_g", "od_w_uq", "od_w_ukv", "od_conv_w",
             "od_conv_b", "od_ln_g", "od_ln_b", "final_g"]
    local = dict(zip(names, [c_ctx, ada_w, ada_b, norm1_g, norm2_g, w_out, mlp_w1, mlp_w2, ev_w_in, ev_q_norm_g, ev_k_norm_g, ev_sgu_norm_g, ev_sgu_w, ev_sgu_b, od_w_in, od_q_norm_g, od_kv_norm_g, od_w_uq, od_w_ukv, od_conv_w, od_conv_b, od_ln_g, od_ln_b, final_g]))
    mom1 = dict(zip(names, [m_c_ctx, m_ada_w, m_ada_b, m_norm1_g, m_norm2_g, m_w_out, m_mlp_w1, m_mlp_w2, m_ev_w_in, m_ev_q_norm_g, m_ev_k_norm_g, m_ev_sgu_norm_g, m_ev_sgu_w, m_ev_sgu_b, m_od_w_in, m_od_q_norm_g, m_od_kv_norm_g, m_od_w_uq, m_od_w_ukv, m_od_conv_w, m_od_conv_b, m_od_ln_g, m_od_ln_b, m_final_g]))
    mom2 = dict(zip(names, [v_c_ctx, v_ada_w, v_ada_b, v_norm1_g, v_norm2_g, v_w_out, v_mlp_w1, v_mlp_w2, v_ev_w_in, v_ev_q_norm_g, v_ev_k_norm_g, v_ev_sgu_norm_g, v_ev_sgu_w, v_ev_sgu_b, v_od_w_in, v_od_q_norm_g, v_od_kv_norm_g, v_od_w_uq, v_od_w_ukv, v_od_conv_w, v_od_conv_b, v_od_ln_g, v_od_ln_b, v_final_g]))
    bsz, length, d = x.shape
    ctx_len = ctx.shape[1]
    me = _block_index(_mesh_pos())

    shard = {p: local[wn][layer] for p, wn, layer, _, _ in _PIECES}
    by_columns = {p: axis == 1 for p, _, _, axis, _ in _PIECES}
    stages = {s: [p for p, _, _, _, st in _PIECES if st == s] for s in "abc"}
    block_rows = {p: shard[p].shape[1] if by_columns[p] else shard[p].shape[0] for p in shard}

    def pad_block_rows(t, p):
        extra = -block_rows[p] % ROW_ALIGN
        return jnp.pad(t, [(0, 0)] * (t.ndim - 2) + [(0, extra), (0, 0)]) if extra else t

    def travelling(p):
        t = shard[p].astype(BF16)
        return pad_block_rows(t.T if by_columns[p] else t, p)

    mine = {s: [travelling(p) for p in stages[s]] for s in "abc"}
    tiny_pack = _Packing([(n, local[n].size) for n, _ in _SMALL_SHARDED], 8)

    def unpack_weights(s, gathered):
        return {p: g[:, :block_rows[p]].reshape(N_DEV * block_rows[p], g.shape[2]) for p, g in zip(stages[s], gathered)}

    def pack_grads(s, g):
        return [pad_block_rows(g[p].reshape(N_DEV, block_rows[p], g[p].shape[1]), p) for p in stages[s]]

    cond_local = jnp.concatenate([c, c_ctx[None], jnp.zeros((COND_ROWS - bsz - 1, d), F32)], axis=0)
    cond, gathered_tiny, *gathered_a = _all_gather(
        [cond_local, tiny_pack.pack({n: local[n] for n, _ in _SMALL_SHARDED})] + mine["a"], "gather_inputs")
    cond = cond.reshape(N_DEV * COND_ROWS, d)
    silu_op = _rowwise("silu", _f_silu, [(d, F32)], N_DEV * COND_ROWS)
    silu_rows, silu_pullback = jax.vjp(lambda r: silu_op((r[None],), (), (), (), ())[0][0], cond)
    mod_cols = ada_w.shape[2]
    mod_part = jnp.concatenate([_mm(silu_rows, ada_w[i], "nn", (F32,), f"ada{i}_fwd")[0] for i in range(2)], axis=0)
    (mod_all,) = _all_gather([mod_part], "gather_mod")
    mod_all = mod_all.reshape(N_DEV, 2, N_DEV * COND_ROWS, mod_cols)
    modrows = []
    for i in range(2):
        whole = mod_all[:, i].transpose(1, 0, 2).reshape(N_DEV * COND_ROWS, N_DEV * mod_cols) + ada_b[i]
        modrows.append(lax.dynamic_slice_in_dim(whole, me * COND_ROWS, COND_ROWS, axis=0)[:bsz + 1])

    weights_a = unpack_weights("a", gathered_a)
    gather_b, token_b = _push_start(mine["b"], mod_all, "gather_weights_b_start")
    gather_c, token_c = _push_start(mine["c"], token_b, "gather_weights_c_start")
    full = {n: local[n] for n in _REPLICATED}
    for n, axis in _SMALL_SHARDED:
        full[n] = _unshard(tiny_pack.piece(gathered_tiny, n, (N_DEV,)).reshape((N_DEV,) + local[n].shape), axis)

    xall = jnp.concatenate([ctx, x], axis=1)
    modrows0, modrows1 = modrows[0] + token_c[0, 0], modrows[1]
    w_a = dict(weights_a, norm1_g0=norm1_g[0][None], norm2_g0=norm2_g[0][None],
               **{n: full[n] for n in ("ev_q_norm_g", "ev_k_norm_g", "ev_sgu_norm_g", "ev_sgu_w", "ev_sgu_b")})
    (x1, h0), pull_a = jax.vjp(lambda x_, m0, w: _mixer0(x_, m0, w, bsz, length, ctx_len), xall, modrows0, w_a)
    w_b = dict(unpack_weights("b", _push_wait(gather_b, x1, mine["b"], me, "gather_weights_b_wait")), norm1_g1=norm1_g[1][None])
    (x2, h1), pull_b = jax.vjp(lambda x_, h_, m0, m1, w: _mlp0(x_, h_, m0, m1, w, bsz, length, ctx_len), x1, h0, modrows0, modrows1, w_b)
    w_c = dict(unpack_weights("c", _push_wait(gather_c, x2, mine["c"], me, "gather_weights_c_wait")), norm2_g1=norm2_g[1][None],
               **{n: full[n] for n in ("od_q_norm_g", "od_kv_norm_g", "od_conv_w", "od_conv_b", "od_ln_g", "od_ln_b")})
    x4, pull_c = jax.vjp(lambda x_, h_, m1, w: _layer1(x_, h_, m1, w, bsz, length, ctx_len), x2, h1, modrows1, w_c)
    loss_part, dx4, dfinal = _loss_head(x4, loss_target, final_g[None])
    loss = lax.psum(loss_part[0, 0], MESH_AXES)

    dx2, dh1, dmod1_c, g_c = pull_c(dx4)
    grads_c = pack_grads("c", g_c)
    exchange_c, token = _push_start(grads_c, dx2, "exchange_grads_c_start")
    dx1, dh0, dmod0_b, dmod1_b, g_b = pull_b((dx2, dh1 + token[0, 0].astype(dh1.dtype)))
    grads_b = pack_grads("b", g_b)
    exchange_b, token = _push_start(grads_b, dx1, "exchange_grads_b_start")
    dxall, dmod0_a, g_a = pull_a((dx1, dh0 + token[0, 0].astype(dh0.dtype)))
    grad_x = dxall[:, ctx_len:]
    dmodrows = [dmod0_a + dmod0_b, dmod1_b + dmod1_c]
    grads = {n: g[n] for g in (g_a, g_c) for n in g if n in full}
    grads["norm1_g"] = jnp.concatenate([g_a["norm1_g0"], g_b["norm1_g1"]], axis=0)
    grads["norm2_g"] = jnp.concatenate([g_a["norm2_g0"], g_c["norm2_g1"]], axis=0)
    grads["final_g"] = dfinal[0]

    dmod_local = jnp.concatenate([jnp.pad(dm, ((0, COND_ROWS - bsz - 1), (0, 0))) for dm in dmodrows], axis=0)
    (dmod_all,) = _all_gather([dmod_local], "gather_dmod")
    dmod_all = dmod_all.reshape(N_DEV, 2, COND_ROWS, N_DEV * mod_cols)
    grads_a = pack_grads("a", g_a)
    exchange_a, token = _push_start(grads_a, dmod_all, "exchange_grads_a_start")
    dmod_all = dmod_all + token[0, 0]
    reduced = {}
    grad_ada_w, grad_ada_b, dmod_mine = [], [], []
    for i in range(2):
        dmod = dmod_all[:, i].reshape(N_DEV * COND_ROWS, N_DEV * mod_cols)
        grad_ada_b.append(_sum_rows(dmod, f"ada{i}_db")[0])
        dmod_mine.append(lax.dynamic_slice_in_dim(dmod, me * mod_cols, mod_cols, axis=1))
        grad_ada_w.append(_mm(silu_rows, dmod_mine[i], "tn", (F32,), f"ada{i}_dw")[0])
    reduced["ada_w"], reduced["ada_b"] = jnp.stack(grad_ada_w), jnp.stack(grad_ada_b)
    dsilu = _mm(jnp.concatenate(dmod_mine, axis=1), jnp.concatenate([ada_w[0], ada_w[1]], axis=1), "nt", (F32,), "ada_dx")[0]
    (dcond,) = silu_pullback(dsilu)
    grads["c_ctx"] = _sum_rows(dcond.reshape(N_DEV, COND_ROWS, d)[:, bsz], "c_ctx_rows")[0]

    small_names = list(_REPLICATED) + [n for n, _ in _SMALL_SHARDED]
    small_pack = _Packing([(n, full[n].size) for n in small_names], 8)
    (small_all,) = _all_gather([small_pack.pack({n: grads[n].astype(F32) for n in small_names})], "gather_small_grads")
    small_sum = _sum_blocks(small_all, "sum_small_grads")
    for n in _REPLICATED:
        reduced[n] = small_pack.piece(small_sum, n).reshape(local[n].shape)
    for n, axis in _SMALL_SHARDED:
        whole = small_pack.piece(small_sum, n).reshape(full[n].shape)
        reduced[n] = lax.dynamic_slice_in_dim(whole, me * local[n].shape[axis], local[n].shape[axis], axis=axis)

    def own_blocks(blocks):
        return [lax.dynamic_index_in_dim(t, me, 0, keepdims=False) for t in blocks]

    received = {s: _push_wait(handle, small_sum, own_blocks(blocks), me, f"exchange_grads_{s}_wait")
                for s, handle, blocks in (("c", exchange_c, grads_c), ("b", exchange_b, grads_b), ("a", exchange_a, grads_a))}
    piece_grad = {}
    for s in "abc":
        for p, blocks in zip(stages[s], received[s]):
            summed = _sum_blocks(blocks, "sum_grads_" + p)[:block_rows[p]]
            piece_grad[p] = summed.T if by_columns[p] else summed
    for n in ("w_out", "mlp_w1", "mlp_w2"):
        reduced[n] = jnp.stack([piece_grad[p] for p, wn, _, _, _ in _PIECES if wn == n])
    for n in ("ev_w_in", "od_w_in", "od_w_uq", "od_w_ukv"):
        reduced[n] = piece_grad[n][None]

    delta, new_m, new_v = {}, {}, {}
    for n in names:
        delta[n], new_m[n], new_v[n] = _adamw(local[n], reduced[n], mom1[n], mom2[n], "adamw_" + n)
    return (loss, grad_x, *[reduced[n] for n in names], *[delta[n] for n in names], *[new_m[n] for n in names], *[new_v[n] for n in names])
```

```python
import functools
import math

import jax
import jax.numpy as jnp
from jax import lax
from jax.experimental import pallas as pl
from jax.experimental.pallas import tpu as pltpu

F32, BF16 = jnp.float32, jnp.bfloat16

EPS = 1e-6
GRID_W = 64
ROPE_THETA = 10000.0
A_HEAD_DIM, A_Q_HEADS, A_KV_HEADS = 64, 8, 2
B_GROUPS, B_GROUP_DIM, B_CHUNK = 8, 64, 128
C_HEADS, C_NOPE, C_ROPE, C_V, C_Q_RANK, C_KV_RANK = 8, 64, 32, 64, 256, 128
D_CONV = 31
CONV_PAD = D_CONV // 2
N_MOD = 6
N_DEV = 8
MESH_AXES = ("x", "y", "c")

ADAM_LR, ADAM_B1, ADAM_B2, ADAM_EPS, ADAM_WD, ADAM_STEP = 0.001, 0.9, 0.999, 1e-08, 0.01, 10

VMEM_LIMIT = 56 * 1024 * 1024
PACK_COLS = 1024
ROW_ALIGN = 16
SUBLANES = 8
CONV_ROWS = 32
COND_ROWS = 8


def _pcall(body, **kw):
    return pl.pallas_call(body, **kw)


def _params(sem=None):
    return pltpu.CompilerParams(dimension_semantics=sem, vmem_limit_bytes=VMEM_LIMIT)


def _pick(n, cands):
    for c in cands:
        if n % c == 0:
            return c
    return n


def _mm(a, b, mode, out_dtypes, name, epi=None, extras=()):
    if mode == "tn":
        kk, m = a.shape
    else:
        m, kk = a.shape
    n = b.shape[0] if mode == "nt" else b.shape[1]
    tm = _pick(m, (1152, 1024, 896, 768, 512, 256, 128))
    tn = _pick(n, (1024, 896, 768, 512, 256, 128))
    tk = kk if kk <= 1024 else _pick(kk, (2048, 1536, 1024, 896, 768, 512, 256, 128))
    nk = kk // tk
    ne, no = len(extras), len(out_dtypes)
    a_spec = pl.BlockSpec((tk, tm), lambda i, j, k: (k, i)) if mode == "tn" else pl.BlockSpec((tm, tk), lambda i, j, k: (i, k))
    b_spec = pl.BlockSpec((tn, tk), lambda i, j, k: (j, k)) if mode == "nt" else pl.BlockSpec((tk, tn), lambda i, j, k: (k, j))
    t_spec = pl.BlockSpec((tm, tn), lambda i, j, k: (i, j))
    dn = {"nn": ((1,), (0,)), "nt": ((1,), (1,)), "tn": ((0,), (0,))}[mode]

    def body(a_ref, b_ref, *rest):
        extra_refs, out_refs = rest[:ne], rest[ne:ne + no]

        def finish(acc):
            outs = (acc,) if epi is None else epi(acc, *[r[...] for r in extra_refs])
            for r, o in zip(out_refs, outs):
                r[...] = o.astype(r.dtype)

        part = lax.dot_general(a_ref[...].astype(BF16), b_ref[...].astype(BF16), (dn, ((), ())), preferred_element_type=F32)
        if nk == 1:
            finish(part)
        else:
            acc_ref = rest[-1]
            k = pl.program_id(2)

            @pl.when(k == 0)
            def _():
                acc_ref[...] = part

            @pl.when(k > 0)
            def _():
                acc_ref[...] += part

            @pl.when(k == nk - 1)
            def _():
                finish(acc_ref[...])

    outs = _pcall(
        body, name=name, grid=(m // tm, n // tn, nk),
        in_specs=[a_spec, b_spec] + [t_spec] * ne,
        out_specs=[t_spec] * no,
        out_shape=[jax.ShapeDtypeStruct((m, n), d) for d in out_dtypes],
        scratch_shapes=[pltpu.VMEM((tm, tn), F32)] if nk > 1 else [],
        compiler_params=_params(("parallel", "parallel", "arbitrary")),
    )(a, b, *extras)
    return outs


def _linear(name, out_dtype=F32):
    @jax.custom_vjp
    def op(x, wt):
        return _mm(x, wt, "nt", (out_dtype,), name + "_fwd")[0]

    def fwd(x, wt):
        return op(x, wt), (x, wt)

    def bwd(res, dy):
        x, wt = res
        return _mm(dy, wt, "nn", (x.dtype,), name + "_dx")[0], _mm(dy, x, "tn", (wt.dtype,), name + "_dw")[0]

    op.defvjp(fwd, bwd)
    return op


_NT, _NN, _TN = (((1,), (1,)), ((), ())), (((1,), (0,)), ((), ())), (((0,), (0,)), ((), ()))
_ROW_TILES = (1152, 1024, 768, 512, 256, 128)


def _whole(w):
    return pl.BlockSpec(w.shape, lambda i: (0, 0))


def _groups_apply(x, ws, out_dtypes, name):
    n, (m, kk) = len(ws), x.shape
    tm = _pick(m, _ROW_TILES)

    def body(x_ref, *refs):
        a = x_ref[...].astype(BF16)
        for w_ref, o_ref in zip(refs[:n], refs[n:]):
            o_ref[...] = lax.dot_general(a, w_ref[...], _NT, preferred_element_type=F32).astype(o_ref.dtype)

    return _pcall(
        body, name=name, grid=(m // tm,),
        in_specs=[pl.BlockSpec((tm, kk), lambda i: (i, 0))] + [_whole(w) for w in ws],
        out_specs=[pl.BlockSpec((tm, w.shape[0]), lambda i: (i, 0)) for w in ws],
        out_shape=[jax.ShapeDtypeStruct((m, w.shape[0]), dt) for w, dt in zip(ws, out_dtypes)],
        compiler_params=_params(("parallel",)),
    )(x, *ws)


def _groups_sum(xs, ws, out_dtype, name):
    n, m, kk = len(ws), xs[0].shape[0], ws[0].shape[1]
    tm = _pick(m, _ROW_TILES)

    def body(*refs):
        acc = None
        for x_ref, w_ref in zip(refs[:n], refs[n:2 * n]):
            part = lax.dot_general(x_ref[...].astype(BF16), w_ref[...], _NN, preferred_element_type=F32)
            acc = part if acc is None else acc + part
        refs[2 * n][...] = acc.astype(out_dtype)

    return _pcall(
        body, name=name, grid=(m // tm,),
        in_specs=[pl.BlockSpec((tm, w.shape[0]), lambda i: (i, 0)) for w in ws] + [_whole(w) for w in ws],
        out_specs=pl.BlockSpec((tm, kk), lambda i: (i, 0)), out_shape=jax.ShapeDtypeStruct((m, kk), out_dtype),
        compiler_params=_params(("parallel",)),
    )(*xs, *ws)


def _groups_outer(xs, y, ws, name):
    n, (m, kk) = len(ws), y.shape
    tk = _pick(m, (768, 512, 256, 128))
    steps = m // tk

    def body(y_ref, *refs):
        x_refs, o_refs, acc_refs = refs[:n], refs[n:2 * n], refs[2 * n:]
        k = pl.program_id(0)
        b = y_ref[...].astype(BF16)
        for x_ref, o_ref, acc_ref in zip(x_refs, o_refs, acc_refs):
            part = lax.dot_general(x_ref[...].astype(BF16), b, _TN, preferred_element_type=F32)

            @pl.when(k == 0)
            def _(acc_ref=acc_ref, part=part):
                acc_ref[...] = part

            @pl.when(k > 0)
            def _(acc_ref=acc_ref, part=part):
                acc_ref[...] += part

            @pl.when(k == steps - 1)
            def _(acc_ref=acc_ref, o_ref=o_ref):
                o_ref[...] = acc_ref[...].astype(o_ref.dtype)

    return _pcall(
        body, name=name, grid=(steps,),
        in_specs=[pl.BlockSpec((tk, kk), lambda k: (k, 0))] + [pl.BlockSpec((tk, w.shape[0]), lambda k: (k, 0)) for w in ws],
        out_specs=[_whole(w) for w in ws], out_shape=[jax.ShapeDtypeStruct(w.shape, w.dtype) for w in ws],
        scratch_shapes=[pltpu.VMEM(w.shape, F32) for w in ws], compiler_params=_params(("arbitrary",)),
    )(y, *xs)


def _linear_multi(name, out_dtypes):
    @jax.custom_vjp
    def op(x, wts):
        return tuple(_groups_apply(x, wts, out_dtypes, name + "_fwd"))

    def fwd(x, wts):
        return op(x, wts), (x, wts)

    def bwd(res, dys):
        x, wts = res
        return _groups_sum(dys, wts, x.dtype, name + "_dx"), tuple(_groups_outer(dys, x, wts, name + "_dw"))

    op.defvjp(fwd, bwd)
    return op


def _linear_sum(name):
    @jax.custom_vjp
    def op(xs, ws):
        return _groups_sum(xs, ws, F32, name + "_fwd")

    def fwd(xs, ws):
        return op(xs, ws), (xs, ws)

    def bwd(res, dy):
        xs, ws = res
        return tuple(_groups_apply(dy, ws, [x.dtype for x in xs], name + "_dx")), tuple(_groups_outer(xs, dy, ws, name + "_dw"))

    op.defvjp(fwd, bwd)
    return op


def _relu2_epi(acc):
    return (jnp.square(jnp.maximum(acc, 0.0)),)


def _relu2_bwd_epi(acc, s):
    return (acc * (2.0 * jnp.sqrt(s.astype(F32))),)


def _mlp(name):
    @jax.custom_vjp
    def op(h, w1t, w2):
        (s,) = _mm(h, w1t, "nt", (BF16,), name + "_up", epi=_relu2_epi)
        return _mm(s, w2, "nn", (F32,), name + "_down")[0]

    def fwd(h, w1t, w2):
        (s,) = _mm(h, w1t, "nt", (BF16,), name + "_up", epi=_relu2_epi)
        return _mm(s, w2, "nn", (F32,), name + "_down")[0], (h, w1t, w2, s)

    def bwd(res, dy):
        h, w1t, w2, s = res
        da = _mm(dy, w2, "nt", (BF16,), name + "_ds", epi=_relu2_bwd_epi, extras=(s,))[0]
        dw2 = _mm(s, dy, "tn", (w2.dtype,), name + "_dw2")[0]
        dw1t = _mm(da, h, "tn", (w1t.dtype,), name + "_dw1")[0]
        dh = _mm(da, w1t, "nn", (h.dtype,), name + "_dh")[0]
        return dh, dw1t, dw2

    op.defvjp(fwd, bwd)
    return op


def _two_pass_dot(x, m):
    hi = x.astype(BF16)
    lo = (x - hi.astype(F32)).astype(BF16)
    mb = m.astype(BF16)
    return jnp.dot(hi, mb, preferred_element_type=F32) + jnp.dot(lo, mb, preferred_element_type=F32)


@jax.custom_vjp
def _sym_dot(x, m):
    return _two_pass_dot(x, m)


def _sym_dot_fwd(x, m):
    return _two_pass_dot(x, m), m


def _sym_dot_bwd(m, g):
    return _two_pass_dot(g, m), jnp.zeros_like(m)


_sym_dot.defvjp(_sym_dot_fwd, _sym_dot_bwd)


def _neighbour(x):
    lane = lax.broadcasted_iota(jnp.int32, x.shape, 1)
    return jnp.where(lane % 2 == 0, pltpu.roll(x, x.shape[1] - 1, 1), pltpu.roll(x, 1, 1))


@jax.custom_vjp
def _swap_pairs(x):
    return _neighbour(x)


_swap_pairs.defvjp(lambda x: (_neighbour(x), None), lambda _, g: (_neighbour(g),))


def _lane_group(x, k):
    return x[:, k * B_GROUP_DIM:(k + 1) * B_GROUP_DIM]


@jax.custom_vjp
def _group_mix(ws, v):
    return jnp.concatenate([jnp.dot(w.astype(BF16), _lane_group(v, k).astype(BF16), preferred_element_type=F32)
                            for k, w in enumerate(ws)], axis=1)


def _group_mix_fwd(ws, v):
    return _group_mix(ws, v), (ws, v)


def _group_mix_bwd(res, d):
    ws, v = res
    parts = [(_lane_group(d, k).astype(BF16), _lane_group(v, k).astype(BF16)) for k in range(len(ws))]
    dws = tuple(lax.dot_general(dk, vk, (((1,), (1,)), ((), ())), preferred_element_type=F32) for dk, vk in parts)
    dv = jnp.concatenate([lax.dot_general(w.astype(BF16), dk, (((0,), (0,)), ((), ())), preferred_element_type=F32)
                          for w, (dk, _) in zip(ws, parts)], axis=1)
    return dws, dv


_group_mix.defvjp(_group_mix_fwd, _group_mix_bwd)


def _rowwise(name, f, out_specs, tl, ctx_blocks=0):
    def seg(l, s):
        return jnp.where(l >= ctx_blocks, s - 1, 0) if s > 1 else 0

    def specs(rows, tabs, pers, glbs, consts):
        row_specs = [pl.BlockSpec((1, tl, r.shape[2]), lambda b, l: (b, l, 0)) for r in rows]
        tab_specs = [pl.BlockSpec((tl, t.shape[1]), lambda b, l: (l, 0)) for t in tabs]
        per_specs = [pl.BlockSpec((1, 1, 1, p.shape[3]), functools.partial(lambda b, l, s: (b, seg(l, s), 0, 0), s=p.shape[1])) for p in pers]
        glb_specs = [pl.BlockSpec(g.shape, functools.partial(lambda b, l, nd: (0,) * nd, nd=g.ndim)) for g in glbs]
        const_specs = [pl.BlockSpec(c.shape, functools.partial(lambda b, l, nd: (0,) * nd, nd=c.ndim)) for c in consts]
        return row_specs, tab_specs, per_specs, glb_specs, const_specs

    def load(refs_rows, refs_tabs, refs_pers, refs_glbs, refs_consts):
        return (tuple(r[0].astype(F32) for r in refs_rows), tuple(t[...] for t in refs_tabs),
                tuple(p[0, 0].astype(F32) for p in refs_pers), tuple(g[...].astype(F32) for g in refs_glbs),
                tuple(c[...] for c in refs_consts))

    def call_fwd(rows, tabs, pers, glbs, consts):
        bsz, length = rows[0].shape[:2]
        nr, nt, npp, ng, nc = len(rows), len(tabs), len(pers), len(glbs), len(consts)
        rs, ts, ps, gs, cs = specs(rows, tabs, pers, glbs, consts)

        def body(*refs):
            ins, outs = refs[:nr + nt + npp + ng + nc], refs[nr + nt + npp + ng + nc:]
            r, t, p, g, c = load(ins[:nr], ins[nr:nr + nt], ins[nr + nt:nr + nt + npp], ins[nr + nt + npp:nr + nt + npp + ng], ins[nr + nt + npp + ng:])
            for o_ref, o in zip(outs, f(r, t, p, g, c)):
                o_ref[0] = o.astype(o_ref.dtype)

        return _pcall(
            body, name=name + "_fwd", grid=(bsz, length // tl),
            in_specs=rs + ts + ps + gs + cs,
            out_specs=[pl.BlockSpec((1, tl, w), lambda b, l: (b, l, 0)) for w, _ in out_specs],
            out_shape=[jax.ShapeDtypeStruct((bsz, length, w), d) for w, d in out_specs],
            compiler_params=_params(("parallel", "parallel")),
        )(*rows, *tabs, *pers, *glbs, *consts)

    def call_bwd(rows, tabs, pers, glbs, consts, cts):
        bsz, length = rows[0].shape[:2]
        nr, nt, npp, ng, nc, no = len(rows), len(tabs), len(pers), len(glbs), len(consts), len(cts)
        rs, ts, ps, gs, cs = specs(rows, tabs, pers, glbs, consts)
        n_in = nr + nt + npp + ng + nc

        def body(*refs):
            ins, ct_refs, outs = refs[:n_in], refs[n_in:n_in + no], refs[n_in + no:]
            r, t, p, g, c = load(ins[:nr], ins[nr:nr + nt], ins[nr + nt:nr + nt + npp], ins[nr + nt + npp:nr + nt + npp + ng], ins[nr + nt + npp + ng:])
            _, vjp = jax.vjp(lambda r_, p_, g_: tuple(f(r_, t, p_, g_, c)), r, p, g)
            dr, dp, dg = vjp(tuple(ct[0].astype(F32) for ct in ct_refs))
            dr_refs, dp_refs, dg_refs = outs[:nr], outs[nr:nr + npp], outs[nr + npp:]
            for ref, d in zip(dr_refs, dr):
                ref[0] = d.astype(ref.dtype)
            b, l = pl.program_id(0), pl.program_id(1)
            first_of_segment = (l == 0) | (l == ctx_blocks)
            for ref, d in zip(dp_refs, dp):
                @pl.when(first_of_segment)
                def _(ref=ref, d=d):
                    ref[0, 0] = d

                @pl.when(jnp.logical_not(first_of_segment))
                def _(ref=ref, d=d):
                    ref[0, 0] += d
            first = (b == 0) & (l == 0)
            for ref, d in zip(dg_refs, dg):
                @pl.when(first)
                def _(ref=ref, d=d):
                    ref[...] = d

                @pl.when(jnp.logical_not(first))
                def _(ref=ref, d=d):
                    ref[...] += d

        ct_specs = [pl.BlockSpec((1, tl, w), lambda b, l: (b, l, 0)) for w, _ in out_specs]
        outs = _pcall(
            body, name=name + "_bwd", grid=(bsz, length // tl),
            in_specs=rs + ts + ps + gs + cs + ct_specs,
            out_specs=rs + ps + gs,
            out_shape=[jax.ShapeDtypeStruct(r.shape, r.dtype) for r in rows]
            + [jax.ShapeDtypeStruct(p.shape, F32) for p in pers] + [jax.ShapeDtypeStruct(g.shape, F32) for g in glbs],
            compiler_params=_params(("arbitrary", "arbitrary")),
        )(*rows, *tabs, *pers, *glbs, *consts, *cts)
        return tuple(outs[:nr]), tuple(outs[nr:nr + npp]), tuple(outs[nr + npp:])

    @jax.custom_vjp
    def op(rows, tabs, pers, glbs, consts):
        return tuple(call_fwd(rows, tabs, pers, glbs, consts))

    def fwd(rows, tabs, pers, glbs, consts):
        return op(rows, tabs, pers, glbs, consts), (rows, tabs, pers, glbs, consts)

    def bwd(res, cts):
        rows, tabs, pers, glbs, consts = res
        dr, dp, dg = call_bwd(rows, tabs, pers, glbs, consts, tuple(cts))
        dp = tuple(d.astype(p.dtype) for d, p in zip(dp, pers))
        dg = tuple(d.astype(g.dtype) for d, g in zip(dg, glbs))
        return dr, tuple(jnp.zeros_like(t) for t in tabs), dp, dg, tuple(jnp.zeros_like(c) for c in consts)

    op.defvjp(fwd, bwd)
    return op


def _rms(x, g):
    return x * lax.rsqrt(jnp.mean(x * x, axis=-1, keepdims=True) + EPS) * g


def _f_silu(r, t, p, g, c):
    return (jax.nn.silu(r[0]),)


def _f_modulate(r, t, p, g, c):
    shift, scale = p
    return (_rms(r[0], g[0]) * (1.0 + scale) + shift,)


def _f_res_modulate(r, t, p, g, c):
    x, y = r
    gate, shift, scale = p
    xn = x + gate * y
    return xn, _rms(xn, g[0]) * (1.0 + scale) + shift


def _f_res(r, t, p, g, c):
    return (r[0] + p[0] * r[1],)


def _f_headnorm_rope(r, t, p, g, c):
    x = r[0]
    cos, sin = t
    xn = x * lax.rsqrt(_sym_dot(x * x, c[0]) + EPS) * g[0]
    return (xn * cos + _swap_pairs(xn) * sin,)


def _f_rope(r, t, p, g, c):
    x = r[0]
    cos, sin = t
    return (x * cos + _swap_pairs(x) * sin,)


def _f_rms(r, t, p, g, c):
    return (_rms(r[0], g[0]),)


def _f_sgu_pre(r, t, p, g, c):
    u = jax.nn.gelu(r[0])
    v = jax.nn.gelu(r[1])
    vn = v * lax.rsqrt(_sym_dot(v * v, c[0]) + EPS) * g[0]
    return u, vn


def _f_sgu_mix(r, t, p, g, c):
    u, vn = r
    return (u * (g[B_GROUPS] + _group_mix(tuple(g[:B_GROUPS]), vn)),)


def _f_glu(r, t, p, g, c):
    return (r[0] * jax.nn.sigmoid(r[1]),)


def _f_ln_silu(r, t, p, g, c):
    x = r[0]
    mu = jnp.mean(x, axis=-1, keepdims=True)
    var = jnp.mean(jnp.square(x - mu), axis=-1, keepdims=True)
    return (jax.nn.silu((x - mu) * lax.rsqrt(var + EPS) * g[0] + g[1]),)


LOG2_E = 1.4426950408889634
LN_2 = 0.6931471805599453


def _attention_rows(name, ctx_len, heads, kv_heads):
    grp = heads // kv_heads

    def by_segment(qi, cb, lk, run):
        if cb > 0:
            @pl.when(qi < cb)
            def _():
                run(ctx_len)

            @pl.when(qi >= cb)
            def _():
                run(lk)
        else:
            run(lk)

    def head(ref, h, hd):
        return ref[0, :, h * hd:(h + 1) * hd]

    def scores(q_tile, kk, hd):
        qs = (q_tile.astype(F32) * (LOG2_E * hd ** -0.5)).astype(BF16)
        return qs, lax.dot_general(qs, kk, _NT, preferred_element_type=F32)

    def shapes(q):
        bsz, length, width = q.shape
        tq = math.gcd(_pick(length, (256, 128)), ctx_len) if ctx_len else _pick(length, (256, 128))
        return bsz, length, width // heads, tq, ctx_len // tq

    def call_fwd(q, k, v):
        bsz, length, hd, tq, cb = shapes(q)

        def body(q_ref, k_ref, v_ref, o_ref, lse_ref):
            def run(nk):
                k_all, v_all = k_ref[0, :nk], v_ref[0, :nk]
                outs = []
                for j in range(kv_heads):
                    kk = k_all[:, j * hd:(j + 1) * hd]
                    v_ones = jnp.concatenate([v_all[:, j * hd:(j + 1) * hd], jnp.ones((nk, hd), BF16)], axis=1)
                    for h in range(j * grp, (j + 1) * grp):
                        _, s = scores(head(q_ref, h, hd), kk, hd)
                        m = jnp.max(s, axis=-1, keepdims=True)
                        acc = jnp.dot(jnp.exp2(s - m).astype(BF16), v_ones, preferred_element_type=F32)
                        l = acc[:, hd:hd + 1]
                        outs.append((acc[:, :hd] / l).astype(BF16))
                        lse_ref[0, h] = m + jnp.log2(l)
                o_ref[0] = jnp.concatenate(outs, axis=1)

            by_segment(pl.program_id(1), cb, length, run)

        q_spec = pl.BlockSpec((1, tq, heads * hd), lambda b, i: (b, i, 0))
        kv_spec = pl.BlockSpec((1, length, kv_heads * hd), lambda b, i: (b, 0, 0))
        return _pcall(
            body, name=name + "_fwd", grid=(bsz, length // tq), in_specs=[q_spec, kv_spec, kv_spec],
            out_specs=[q_spec, pl.BlockSpec((1, heads, tq, 1), lambda b, i: (b, 0, i, 0))],
            out_shape=[jax.ShapeDtypeStruct(q.shape, BF16), jax.ShapeDtypeStruct((bsz, heads, length, 1), F32)],
            compiler_params=_params(("parallel", "parallel")),
        )(q, k, v)

    def call_bwd(q, k, v, o, lse, do):
        bsz, length, hd, tq, cb = shapes(q)
        nq = length // tq

        def body(q_ref, k_ref, v_ref, o_ref, lse_ref, do_ref, dq_ref, dk_ref, dv_ref, dk_acc, dv_acc):
            qi = pl.program_id(1)

            @pl.when(qi == 0)
            def _():
                dk_acc[...] = jnp.zeros_like(dk_acc)
                dv_acc[...] = jnp.zeros_like(dv_acc)

            def run(nk):
                k_all, v_all = k_ref[0, :nk], v_ref[0, :nk]
                dqs, dks, dvs = [], [], []
                for j in range(kv_heads):
                    kk, vv = k_all[:, j * hd:(j + 1) * hd], v_all[:, j * hd:(j + 1) * hd]
                    dk_sum = dv_sum = None
                    for h in range(j * grp, (j + 1) * grp):
                        dd = head(do_ref, h, hd)
                        qs, s = scores(head(q_ref, h, hd), kk, hd)
                        p = jnp.exp2(s - lse_ref[0, h])
                        delta = jnp.sum(dd.astype(F32) * head(o_ref, h, hd).astype(F32), axis=-1, keepdims=True)
                        t = (p * (lax.dot_general(dd, vv, _NT, preferred_element_type=F32) - delta)).astype(BF16)
                        dqs.append((jnp.dot(t, kk, preferred_element_type=F32) * hd ** -0.5).astype(BF16))
                        dk_h = lax.dot_general(t, qs, _TN, preferred_element_type=F32)
                        dv_h = lax.dot_general(p.astype(BF16), dd, _TN, preferred_element_type=F32)
                        dk_sum = dk_h if dk_sum is None else dk_sum + dk_h
                        dv_sum = dv_h if dv_sum is None else dv_sum + dv_h
                    dks.append(dk_sum)
                    dvs.append(dv_sum)
                dq_ref[0] = jnp.concatenate(dqs, axis=1)
                dk_acc[:nk] += jnp.concatenate(dks, axis=1)
                dv_acc[:nk] += jnp.concatenate(dvs, axis=1)

            by_segment(qi, cb, length, run)

            @pl.when(qi == nq - 1)
            def _():
                dk_ref[0] = (dk_acc[...] * LN_2).astype(dk_ref.dtype)
                dv_ref[0] = dv_acc[...].astype(dv_ref.dtype)

        q_spec = pl.BlockSpec((1, tq, heads * hd), lambda b, i: (b, i, 0))
        kv_spec = pl.BlockSpec((1, length, kv_heads * hd), lambda b, i: (b, 0, 0))
        lse_spec = pl.BlockSpec((1, heads, tq, 1), lambda b, i: (b, 0, i, 0))
        return _pcall(
            body, name=name + "_bwd", grid=(bsz, nq), in_specs=[q_spec, kv_spec, kv_spec, q_spec, lse_spec, q_spec],
            out_specs=[q_spec, kv_spec, kv_spec],
            out_shape=[jax.ShapeDtypeStruct(q.shape, BF16), jax.ShapeDtypeStruct(k.shape, BF16), jax.ShapeDtypeStruct(v.shape, BF16)],
            scratch_shapes=[pltpu.VMEM(k.shape[1:], F32), pltpu.VMEM(v.shape[1:], F32)],
            compiler_params=_params(("parallel", "arbitrary")),
        )(q, k, v, o, lse, do)

    @jax.custom_vjp
    def op(q, k, v):
        return call_fwd(q, k, v)[0]

    def fwd(q, k, v):
        o, lse = call_fwd(q, k, v)
        return o, (q, k, v, o, lse)

    def bwd(res, do):
        q, k, v, o, lse = res
        dq, dk, dv = call_bwd(q, k, v, o, lse, do)
        return dq, dk.astype(k.dtype), dv.astype(v.dtype)

    op.defvjp(fwd, bwd)
    return op


def _attention_latent_rows(name, heads, nope, rope, dv, heads_forward, heads_backward):
    lanes = 128
    scale = (nope + rope) ** -0.5

    def shapes(q, kn):
        bsz, lq, _ = q.shape
        return bsz, lq, kn.shape[1], _pick(lq, (256, 128))

    def scores(q_tile, kk):
        qs = (q_tile.astype(F32) * (LOG2_E * scale)).astype(BF16)
        return qs, lax.dot_general(qs, kk, _NT, preferred_element_type=F32)

    def call_fwd(q, kn, kr, v):
        bsz, lq, lk, tq = shapes(q, kn)
        heads_per_step = heads_forward

        def body(q_ref, kn_ref, kr_ref, v_ref, o_ref, lse_ref):
            real = lax.broadcasted_iota(jnp.int32, (lk, lanes), 1) < dv
            outs = []
            for h in range(heads_per_step):
                kk = kn_ref[0, :, h * lanes:(h + 1) * lanes] + kr_ref[0]
                v_ones = jnp.where(real, v_ref[0, :, h * lanes:(h + 1) * lanes], jnp.ones((lk, lanes), BF16))
                _, s = scores(q_ref[0, :, h * lanes:(h + 1) * lanes], kk)
                m = jnp.max(s, axis=-1, keepdims=True)
                acc = jnp.dot(jnp.exp2(s - m).astype(BF16), v_ones, preferred_element_type=F32)
                l = acc[:, dv:dv + 1]
                outs.append((acc[:, :dv] / l).astype(BF16))
                lse_ref[0, h] = m + jnp.log2(l)
            o_ref[0] = jnp.concatenate(outs, axis=1)

        wide = heads_per_step * lanes
        return _pcall(
            body, name=name + "_fwd", grid=(bsz, heads // heads_per_step, lq // tq),
            in_specs=[pl.BlockSpec((1, tq, wide), lambda b, g, i: (b, i, g)), pl.BlockSpec((1, lk, wide), lambda b, g, i: (b, 0, g)),
                      pl.BlockSpec((1, lk, lanes), lambda b, g, i: (b, 0, 0)), pl.BlockSpec((1, lk, wide), lambda b, g, i: (b, 0, g))],
            out_specs=[pl.BlockSpec((1, tq, heads_per_step * dv), lambda b, g, i: (b, i, g)),
                       pl.BlockSpec((1, heads_per_step, tq, 1), lambda b, g, i: (b, g, i, 0))],
            out_shape=[jax.ShapeDtypeStruct((bsz, lq, heads * dv), BF16), jax.ShapeDtypeStruct((bsz, heads, lq, 1), F32)],
            compiler_params=_params(("parallel", "parallel", "parallel")),
        )(q, kn, kr, v)

    def call_bwd(q, kn, kr, v, o, lse, do):
        bsz, lq, lk, tq = shapes(q, kn)
        heads_per_step = heads_backward

        def body(q_ref, kn_ref, kr_ref, v_ref, o_ref, lse_ref, do_ref, dq_ref, dkn_ref, dkr_ref, dv_ref, dkn_acc, dkr_acc, dv_acc):
            g, qi = pl.program_id(1), pl.program_id(2)
            last_q = qi == lq // tq - 1

            @pl.when(qi == 0)
            def _():
                dkn_acc[...] = jnp.zeros_like(dkn_acc)
                dv_acc[...] = jnp.zeros_like(dv_acc)

            @pl.when((qi == 0) & (g == 0))
            def _():
                dkr_acc[...] = jnp.zeros_like(dkr_acc)

            dqs, dks, dvs = [], [], []
            for h in range(heads_per_step):
                kk = kn_ref[0, :, h * lanes:(h + 1) * lanes] + kr_ref[0]
                vv = v_ref[0, :, h * lanes:h * lanes + dv]
                dd = do_ref[0, :, h * dv:(h + 1) * dv]
                qs, s = scores(q_ref[0, :, h * lanes:(h + 1) * lanes], kk)
                p = jnp.exp2(s - lse_ref[0, h])
                delta = jnp.sum(dd.astype(F32) * o_ref[0, :, h * dv:(h + 1) * dv].astype(F32), axis=-1, keepdims=True)
                t = (p * (lax.dot_general(dd, vv, _NT, preferred_element_type=F32) - delta)).astype(BF16)
                dqs.append((jnp.dot(t, kk, preferred_element_type=F32) * scale).astype(BF16))
                dks.append(lax.dot_general(t, qs, _TN, preferred_element_type=F32) * LN_2)
                dv_h = lax.dot_general(p.astype(BF16), dd, _TN, preferred_element_type=F32)
                dvs.append(jnp.concatenate([dv_h, jnp.zeros((lk, lanes - dv), F32)], axis=1))
            dq_ref[0] = jnp.concatenate(dqs, axis=1)
            dkn_acc[...] += jnp.concatenate(dks, axis=1)
            dv_acc[...] += jnp.concatenate(dvs, axis=1)
            shared = dks[0]
            for d_h in dks[1:]:
                shared = shared + d_h
            dkr_acc[...] += shared

            @pl.when(last_q)
            def _():
                dkn_ref[0] = dkn_acc[...].astype(dkn_ref.dtype)
                dv_ref[0] = dv_acc[...].astype(dv_ref.dtype)

            @pl.when(last_q & (g == heads // heads_per_step - 1))
            def _():
                dkr_ref[0] = dkr_acc[...].astype(dkr_ref.dtype)

        wide = heads_per_step * lanes
        q_spec = pl.BlockSpec((1, tq, wide), lambda b, g, i: (b, i, g))
        k_spec = pl.BlockSpec((1, lk, wide), lambda b, g, i: (b, 0, g))
        kr_spec = pl.BlockSpec((1, lk, lanes), lambda b, g, i: (b, 0, 0))
        o_spec = pl.BlockSpec((1, tq, heads_per_step * dv), lambda b, g, i: (b, i, g))
        lse_spec = pl.BlockSpec((1, heads_per_step, tq, 1), lambda b, g, i: (b, g, i, 0))
        return _pcall(
            body, name=name + "_bwd", grid=(bsz, heads // heads_per_step, lq // tq),
            in_specs=[q_spec, k_spec, kr_spec, k_spec, o_spec, lse_spec, o_spec],
            out_specs=[q_spec, k_spec, kr_spec, k_spec],
            out_shape=[jax.ShapeDtypeStruct(q.shape, BF16), jax.ShapeDtypeStruct(kn.shape, BF16), jax.ShapeDtypeStruct(kr.shape, BF16),
                       jax.ShapeDtypeStruct(v.shape, BF16)],
            scratch_shapes=[pltpu.VMEM((lk, wide), F32), pltpu.VMEM((lk, lanes), F32), pltpu.VMEM((lk, wide), F32)],
            compiler_params=_params(("parallel", "arbitrary", "arbitrary")),
        )(q, kn, kr, v, o, lse, do)

    @jax.custom_vjp
    def op(q, kn, kr, v):
        return call_fwd(q, kn, kr, v)[0]

    def fwd(q, kn, kr, v):
        o, lse = call_fwd(q, kn, kr, v)
        return o, (q, kn, kr, v, o, lse)

    def bwd(res, do):
        q, kn, kr, v, o, lse = res
        dq, dkn, dkr, dv_ = call_bwd(q, kn, kr, v, o, lse, do)
        return dq, dkn.astype(kn.dtype), dkr.astype(kr.dtype), dv_.astype(v.dtype)

    op.defvjp(fwd, bwd)
    return op


def _conv_call(ypad, taps, name):
    bsz, lp, ch = ypad.shape
    length = lp - 2 * ROW_ALIGN
    tl = _pick(length, (256, 128))

    def body(y_ref, w_ref, o_ref):
        base = pl.multiple_of(pl.program_id(1) * tl, tl)
        for r0 in range(0, tl, CONV_ROWS):
            win = y_ref[0, pl.ds(base + r0, CONV_ROWS + 2 * ROW_ALIGN), :]
            acc = jnp.broadcast_to(w_ref[pl.ds(D_CONV, 1), :], (CONV_ROWS, ch))
            for shift in range(SUBLANES):
                shifted = win[shift:shift + CONV_ROWS + 2 * ROW_ALIGN - SUBLANES, :]
                for k in range(shift, D_CONV, SUBLANES):
                    acc = acc + shifted[k - shift:k - shift + CONV_ROWS, :] * w_ref[pl.ds(k, 1), :]
            o_ref[0, pl.ds(r0, CONV_ROWS), :] = acc

    return _pcall(
        body, name=name, grid=(bsz, length // tl),
        in_specs=[pl.BlockSpec((1, lp, ch), lambda b, l: (b, 0, 0)), pl.BlockSpec((D_CONV + 1, ch), lambda b, l: (0, 0))],
        out_specs=pl.BlockSpec((1, tl, ch), lambda b, l: (b, l, 0)),
        out_shape=jax.ShapeDtypeStruct((bsz, length, ch), F32),
        compiler_params=_params(("parallel", "parallel")),
    )(ypad, taps)


def _conv_dw_call(ypad, dout, name):
    bsz, lp, ch = ypad.shape
    length = lp - 2 * ROW_ALIGN
    tl = _pick(length, (256, 128))

    def body(y_ref, d_ref, o_ref):
        b, l = pl.program_id(0), pl.program_id(1)

        @pl.when((b == 0) & (l == 0))
        def _():
            o_ref[...] = jnp.zeros_like(o_ref)

        base = pl.multiple_of(l * tl, tl)
        win = y_ref[0, pl.ds(base, tl + 2 * ROW_ALIGN), :]
        dd = d_ref[0]
        for shift in range(SUBLANES):
            shifted = win[shift:shift + tl + 2 * ROW_ALIGN - SUBLANES, :]
            for k in range(shift, D_CONV, SUBLANES):
                o_ref[pl.ds(k, 1), :] += jnp.sum(shifted[k - shift:k - shift + tl, :] * dd, axis=0, keepdims=True)
        o_ref[pl.ds(D_CONV, 1), :] += jnp.sum(dd, axis=0, keepdims=True)

    return _pcall(
        body, name=name, grid=(bsz, length // tl),
        in_specs=[pl.BlockSpec((1, lp, ch), lambda b, l: (b, 0, 0)), pl.BlockSpec((1, tl, ch), lambda b, l: (b, l, 0))],
        out_specs=pl.BlockSpec((D_CONV + 1, ch), lambda b, l: (0, 0)),
        out_shape=jax.ShapeDtypeStruct((D_CONV + 1, ch), F32),
        compiler_params=_params(("arbitrary", "arbitrary")),
    )(ypad, dout)


def _pad_rows(y):
    return jnp.pad(y, ((0, 0), (CONV_PAD, 2 * ROW_ALIGN - CONV_PAD), (0, 0)))


@jax.custom_vjp
def _dwconv(y, taps):
    return _conv_call(_pad_rows(y), taps, "conv_fwd")


def _dwconv_fwd(y, taps):
    return _dwconv(y, taps), (y, taps)


def _dwconv_bwd(res, dout):
    y, taps = res
    flipped = jnp.concatenate([taps[:D_CONV][::-1], jnp.zeros_like(taps[D_CONV:])], axis=0)
    dy = _conv_call(_pad_rows(dout), flipped, "conv_dy")
    dtaps = _conv_dw_call(_pad_rows(y), dout, "conv_dw")
    return dy, dtaps


_dwconv.defvjp(_dwconv_fwd, _dwconv_bwd)


def _loss_head(x, target, g):
    bsz, length, d = x.shape
    tl = _pick(length, (256, 128))

    def f(xb, tb, gb):
        err = _rms(xb, gb) - tb
        return 0.5 * jnp.sum(jnp.sum(err * err, axis=-1, keepdims=True), axis=0, keepdims=True) / d

    def body(x_ref, t_ref, g_ref, loss_ref, dx_ref, dg_ref):
        val, vjp = jax.vjp(lambda xb, gb: f(xb, t_ref[0], gb), x_ref[0], g_ref[...])
        dx, dg = vjp(jnp.ones((1, 1), F32))
        dx_ref[0] = dx
        first = (pl.program_id(0) == 0) & (pl.program_id(1) == 0)

        @pl.when(first)
        def _():
            loss_ref[...] = val
            dg_ref[...] = dg

        @pl.when(jnp.logical_not(first))
        def _():
            loss_ref[...] += val
            dg_ref[...] += dg

    row = pl.BlockSpec((1, tl, d), lambda b, l: (b, l, 0))
    return _pcall(
        body, name="loss_head", grid=(bsz, length // tl),
        in_specs=[row, row, pl.BlockSpec((1, d), lambda b, l: (0, 0))],
        out_specs=[pl.BlockSpec((1, 1), lambda b, l: (0, 0)), row, pl.BlockSpec((1, d), lambda b, l: (0, 0))],
        out_shape=[jax.ShapeDtypeStruct((1, 1), F32), jax.ShapeDtypeStruct(x.shape, F32), jax.ShapeDtypeStruct((1, d), F32)],
        compiler_params=_params(("arbitrary", "arbitrary")),
    )(x, target, g)


def _adamw(w, g, m, v, name):
    shape = w.shape
    cols = shape[-1]
    rows = w.size // cols
    tr = _pick(rows, (512, 256, 128))
    w2, g2, m2, v2 = (t.reshape(rows, cols) for t in (w, g, m, v))

    def body(w_ref, g_ref, m_ref, v_ref, d_ref, nm_ref, nv_ref):
        gg = g_ref[...]
        nm = ADAM_B1 * m_ref[...] + (1.0 - ADAM_B1) * gg
        nv = ADAM_B2 * v_ref[...] + (1.0 - ADAM_B2) * jnp.square(gg)
        m_hat = nm / (1.0 - ADAM_B1 ** ADAM_STEP)
        v_hat = nv / (1.0 - ADAM_B2 ** ADAM_STEP)
        d_ref[...] = -ADAM_LR * (m_hat / (jnp.sqrt(v_hat) + ADAM_EPS) + ADAM_WD * w_ref[...])
        nm_ref[...] = nm
        nv_ref[...] = nv

    spec = pl.BlockSpec((tr, cols), lambda i: (i, 0))
    outs = _pcall(
        body, name=name, grid=(rows // tr,), in_specs=[spec] * 4, out_specs=[spec] * 3,
        out_shape=[jax.ShapeDtypeStruct((rows, cols), F32)] * 3,
        compiler_params=_params(("parallel",)),
    )(w2, g2, m2, v2)
    return tuple(o.reshape(shape) for o in outs)


def _mesh_pos():
    return lax.axis_index("x"), lax.axis_index("y"), lax.axis_index("c")


_RELATIONS = [(dx, dy, dc) for dx in (0, 1) for dy in (0, 1) for dc in (0, 1)][1:]


def _peer(pos, rel):
    return tuple(jnp.where(r == 1, 1 - p, p) if r else p for p, r in zip(pos, rel))


def _block_index(pos):
    return 4 * pos[0] + 2 * pos[1] + pos[2]


_HBM = pl.BlockSpec(memory_space=pltpu.HBM)


def _all_gather(xs, name):
    n = len(xs)

    def body(*refs):
        x_refs, out_refs, (send_sems, recv_sems, local_sems) = refs[:n], refs[n:2 * n], refs[2 * n:]
        x_, y_, c_ = _mesh_pos()
        me, sibling = (x_, y_, c_), (x_, y_, 1 - c_)
        chips = [(1 - x_, y_), (x_, 1 - y_), (1 - x_, 1 - y_)]

        def copy(t, k, block, to, own=False):
            slot = out_refs[t].at[_block_index(block)]
            return pltpu.make_async_remote_copy(
                src_ref=x_refs[t] if own else slot, dst_ref=slot, send_sem=send_sems.at[7 * t + k], recv_sem=recv_sems.at[7 * t + k],
                device_id=to, device_id_type=pl.DeviceIdType.MESH)

        mine = [pltpu.make_async_copy(x_refs[t], out_refs[t].at[_block_index(me)], local_sems.at[t]) for t in range(n)]
        first = [[copy(t, 0, me, sibling, own=True)] + [copy(t, 1 + j, me, (*chip, c_), own=True) for j, chip in enumerate(chips)]
                 for t in range(n)]
        passed = [[copy(t, 4 + j, (*chip, c_), sibling) for j, chip in enumerate(chips)] for t in range(n)]
        for t in range(n):
            mine[t].start()
            for cp in first[t]:
                cp.start()
        for t in range(n):
            for j, chip in enumerate(chips):
                copy(t, 1 + j, (*chip, c_), me).wait_recv()
                passed[t][j].start()
        for t in range(n):
            copy(t, 0, sibling, me).wait_recv()
            for j, chip in enumerate(chips):
                copy(t, 4 + j, (*chip, 1 - c_), me).wait_recv()
            for cp in first[t] + passed[t]:
                cp.wait_send()
            mine[t].wait()

    return _pcall(
        body, name=name, in_specs=[_HBM] * n, out_specs=[_HBM] * n,
        out_shape=[jax.ShapeDtypeStruct((N_DEV,) + x.shape, x.dtype) for x in xs],
        scratch_shapes=[pltpu.SemaphoreType.DMA((7 * n,)), pltpu.SemaphoreType.DMA((7 * n,)), pltpu.SemaphoreType.DMA((n,))],
    )(*xs)


_SEM = pl.BlockSpec(memory_space=pltpu.SEMAPHORE)
_EFFECT = pltpu.SideEffectType.DATAFLOW_SIDE_EFFECTING


def _push_start(srcs, after, name):
    n = len(srcs)
    lands = [lax.empty((N_DEV,) + s.shape[-2:], s.dtype) for s in srcs]

    def body(*refs):
        src_refs, land_refs = refs[:n], refs[n:2 * n]
        send_sems, recv_sems, token = refs[2 * n + 1:3 * n + 1], refs[3 * n + 1:4 * n + 1], refs[-1]
        me = _mesh_pos()
        for t in range(n):
            for rel in _RELATIONS:
                peer = _peer(me, rel)
                pltpu.make_async_remote_copy(
                    src_ref=src_refs[t].at[_block_index(peer)] if srcs[t].ndim == 3 else src_refs[t], dst_ref=land_refs[t].at[_block_index(me)],
                    send_sem=send_sems[t], recv_sem=recv_sems[t], device_id=peer, device_id_type=pl.DeviceIdType.MESH).start()
        token[...] = jnp.zeros_like(token)

    outs = _pcall(
        body, name=name,
        out_shape=[pltpu.SemaphoreType.DMA(())] * (2 * n) + [pltpu.HBM(s.shape, s.dtype) for s in srcs]
        + [pltpu.HBM(l.shape, l.dtype) for l in lands] + [jax.ShapeDtypeStruct((8, 128), F32)],
        in_specs=[_HBM] * (2 * n) + [pl.BlockSpec(memory_space=pl.ANY)],
        out_specs=[_SEM] * (2 * n) + [_HBM] * (2 * n) + [pl.BlockSpec(memory_space=pltpu.VMEM)],
        input_output_aliases={i: 2 * n + i for i in range(2 * n)}, compiler_params=pltpu.CompilerParams(has_side_effects=_EFFECT),
    )(*[pltpu.with_memory_space_constraint(t, pltpu.HBM) for t in list(srcs) + lands], after)
    return (outs[:n], outs[n:2 * n], outs[2 * n:3 * n], outs[3 * n:4 * n]), outs[-1]


def _push_wait(handle, after, owns, me, name):
    send_sems, recv_sems, src_thrus, land_thrus = handle
    n = len(land_thrus)

    def body(*refs):
        land_refs, sends, recvs = refs[n:2 * n], refs[2 * n:3 * n], refs[3 * n:4 * n]
        for t in range(n):
            seven = land_refs[t].at[pl.ds(0, N_DEV - 1)]
            all_seven = pltpu.make_async_remote_copy(src_ref=seven, dst_ref=seven, send_sem=sends[t], recv_sem=recvs[t],
                                                     device_id=_mesh_pos(), device_id_type=pl.DeviceIdType.MESH)
            all_seven.wait_send()
            all_seven.wait_recv()

    outs = _pcall(
        body, name=name,
        out_shape=[pltpu.HBM(t.shape, t.dtype) for t in list(src_thrus) + list(land_thrus)],
        in_specs=[_HBM] * (2 * n) + [_SEM] * (2 * n) + [pl.BlockSpec(memory_space=pl.ANY)], out_specs=[_HBM] * (2 * n),
        input_output_aliases={i: i for i in range(2 * n)}, compiler_params=pltpu.CompilerParams(has_side_effects=_EFFECT),
    )(*src_thrus, *land_thrus, *send_sems, *recv_sems, after)
    return [lax.dynamic_update_slice(land, own[None], (me, 0, 0)) for land, own in zip(outs[n:], owns)]


def _sum_blocks(p, name):
    n, rows, cols = p.shape
    tr = _pick(rows, (256, 128, 64, 32, 16, 8))

    def body(p_ref, o_ref):
        acc = p_ref[0].astype(F32)
        for s in range(1, n):
            acc = acc + p_ref[s].astype(F32)
        o_ref[...] = acc

    return _pcall(
        body, name=name, grid=(rows // tr,),
        in_specs=[pl.BlockSpec((n, tr, cols), lambda i: (0, i, 0))], out_specs=pl.BlockSpec((tr, cols), lambda i: (i, 0)),
        out_shape=jax.ShapeDtypeStruct((rows, cols), F32), compiler_params=_params(("parallel",)),
    )(p)


def _sum_rows(t, name):
    def body(t_ref, o_ref):
        o_ref[...] = jnp.sum(t_ref[...], axis=0, keepdims=True)

    return _pcall(body, name=name, out_shape=jax.ShapeDtypeStruct((1, t.shape[1]), F32))(t)


class _Packing:
    def __init__(self, sizes, align):
        self.offsets, self.sizes, self.align = {}, dict(sizes), align
        row = 0
        for name, size in sizes:
            self.offsets[name] = row
            row += -(-size // (align * PACK_COLS)) * align
        self.rows = row

    def pack(self, pieces):
        return self.pack_blocks({n: pieces[n].reshape(1, -1) for n in self.sizes})[0]

    def pack_blocks(self, pieces):
        out = []
        for n, size in self.sizes.items():
            padded = -(-size // (self.align * PACK_COLS)) * self.align * PACK_COLS
            out.append(jnp.pad(pieces[n], ((0, 0), (0, padded - size))).reshape(pieces[n].shape[0], -1, PACK_COLS))
        return jnp.concatenate(out, axis=1)

    def piece(self, packed, name, lead=()):
        start, size = self.offsets[name], self.sizes[name]
        nrow = -(-size // (self.align * PACK_COLS)) * self.align
        sl = packed[..., start:start + nrow, :]
        return sl.reshape(lead + (nrow * PACK_COLS,))[..., :size]


_PIECES = (("ev_w_in", "ev_w_in", 0, 1, "a"), ("w_out0", "w_out", 0, 0, "a"),
           ("mlp_w1_0", "mlp_w1", 0, 1, "b"), ("mlp_w2_0", "mlp_w2", 0, 0, "b"),
           ("od_w_in", "od_w_in", 0, 1, "c"), ("od_w_uq", "od_w_uq", 0, 1, "c"), ("od_w_ukv", "od_w_ukv", 0, 1, "c"),
           ("w_out1", "w_out", 1, 0, "c"), ("mlp_w1_1", "mlp_w1", 1, 1, "c"), ("mlp_w2_1", "mlp_w2", 1, 0, "c"))
_SMALL_SHARDED = (("od_q_norm_g", 1), ("od_conv_w", 2), ("od_conv_b", 1), ("od_ln_g", 1), ("od_ln_b", 1))
_REPLICATED = ("c_ctx", "norm1_g", "norm2_g", "ev_q_norm_g", "ev_k_norm_g", "ev_sgu_norm_g", "ev_sgu_w", "ev_sgu_b",
               "od_kv_norm_g", "final_g")


def _unshard(blocks, axis):
    moved = jnp.moveaxis(blocks, 0, axis)
    shape = moved.shape
    return moved.reshape(shape[:axis] + (shape[axis] * shape[axis + 1],) + shape[axis + 2:])


def _group_mean_matrix(width, group):
    idx = jnp.arange(width) // group
    return (idx[:, None] == idx[None, :]).astype(F32) / group


def _angles(length, d_rot):
    rows = length // GRID_W
    row = jnp.broadcast_to(jnp.arange(rows)[:, None], (rows, GRID_W)).reshape(-1).astype(F32)
    col = jnp.broadcast_to(jnp.arange(GRID_W)[None, :], (rows, GRID_W)).reshape(-1).astype(F32)
    d_axis = d_rot // 2
    inv = ROPE_THETA ** (-jnp.arange(0, d_axis, 2, dtype=F32) / d_axis)
    return jnp.concatenate([row[:, None] * inv, col[:, None] * inv], axis=-1)


def _rope_tables(length, d_rot, head_dim, heads, ctx_len, tail=0):
    ang = _angles(length, d_rot)
    cos = jnp.repeat(jnp.cos(ang), 2, axis=1)
    sin = jnp.repeat(jnp.sin(ang), 2, axis=1) * jnp.tile(jnp.array([-1.0, 1.0], F32), d_rot // 2)
    keep = head_dim - d_rot - tail
    cos = jnp.concatenate([jnp.ones((length, keep), F32), cos, jnp.ones((length, tail), F32)], axis=1)
    sin = jnp.concatenate([jnp.zeros((length, keep), F32), sin, jnp.zeros((length, tail), F32)], axis=1)
    cos, sin = jnp.tile(cos, (1, heads)), jnp.tile(sin, (1, heads))
    cos = jnp.concatenate([jnp.ones((ctx_len, cos.shape[1]), F32), cos], axis=0)
    sin = jnp.concatenate([jnp.zeros((ctx_len, sin.shape[1]), F32), sin], axis=0)
    return cos, sin


def _segment_params(mod, bsz):
    parts = jnp.split(mod, N_MOD, axis=-1)
    out = []
    for part in parts:
        lat = part[:bsz]
        ctx = jnp.broadcast_to(part[bsz:bsz + 1], lat.shape)
        out.append(jnp.stack([ctx, lat], axis=1)[:, :, None, :])
    return out


def _flat(t):
    return t.reshape(-1, t.shape[-1])


def _sequence_rowwise(ctx_len):
    tl = math.gcd(256, ctx_len)

    def make(name, f, out_specs, rows_per_block=tl, ctx_blocks=ctx_len // tl):
        return _rowwise(name, f, out_specs, rows_per_block, ctx_blocks)

    return make


def _mixer0(xall, modrows0, w, bsz, length, ctx_len):
    d = xall.shape[-1]
    total = ctx_len + length
    rowwise, flat = _sequence_rowwise(ctx_len), _flat
    sh1, sc1, g1, sh2, sc2, _ = _segment_params(modrows0, bsz)
    (h,) = rowwise("mod0", _f_modulate, [(d, BF16)])((xall,), (), (sh1, sc1), (w["norm1_g0"],), ())
    ev_q, ev_kv = A_Q_HEADS * A_HEAD_DIM, A_KV_HEADS * A_HEAD_DIM
    half = B_GROUPS * B_GROUP_DIM
    groups = tuple(jnp.split(w["ev_w_in"], [ev_q, ev_q + ev_kv, ev_q + 2 * ev_kv, ev_q + 2 * ev_kv + half], axis=0))
    qp, kp, vp, zu, zv = [t.reshape(bsz, total, -1) for t in _linear_multi("ev_in", (F32, F32, BF16, F32, F32))(flat(h), groups)]
    cos_q, sin_q = _rope_tables(length, A_HEAD_DIM, A_HEAD_DIM, A_Q_HEADS, ctx_len)
    cos_k, sin_k = cos_q[:, :ev_kv], sin_q[:, :ev_kv]
    (q,) = rowwise("ev_q", _f_headnorm_rope, [(ev_q, BF16)])(
        (qp,), (cos_q, sin_q), (), (jnp.tile(w["ev_q_norm_g"][0], A_Q_HEADS)[None],), (_group_mean_matrix(ev_q, A_HEAD_DIM),))
    (k,) = rowwise("ev_k", _f_headnorm_rope, [(ev_kv, BF16)])(
        (kp,), (cos_k, sin_k), (), (jnp.tile(w["ev_k_norm_g"][0], A_KV_HEADS)[None],), (_group_mean_matrix(ev_kv, A_HEAD_DIM),))
    o_att = _attention_rows("gqa", ctx_len, A_Q_HEADS, A_KV_HEADS)(q, k, vp)
    u, vn = rowwise("sgu_pre", _f_sgu_pre, [(half, F32), (half, BF16)])(
        (zu, zv), (), (), (w["ev_sgu_norm_g"][0].reshape(1, half),), (_group_mean_matrix(half, B_GROUP_DIM),))
    bias = jnp.repeat(w["ev_sgu_b"][0].T, B_GROUP_DIM, axis=1)
    (o_sgu,) = rowwise("sgu_mix", _f_sgu_mix, [(half, BF16)], rows_per_block=B_CHUNK, ctx_blocks=0)(
        (u, vn), (), (), tuple(w["ev_sgu_w"][0][g] for g in range(B_GROUPS)) + (bias,), ())
    y = _linear_sum("out0")((flat(o_att), flat(o_sgu)), tuple(jnp.split(w["w_out0"], 2, axis=0))).reshape(bsz, total, d)
    x1, h = rowwise("res_mod0a", _f_res_modulate, [(d, F32), (d, BF16)])((xall, y), (), (g1, sh2, sc2), (w["norm2_g0"],), ())
    return x1, h


def _mlp0(x1, h, modrows0, modrows1, w, bsz, length, ctx_len):
    d = x1.shape[-1]
    total = ctx_len + length
    g2 = _segment_params(modrows0, bsz)[5]
    sh1, sc1 = _segment_params(modrows1, bsz)[:2]
    y = _mlp("mlp0")(_flat(h), w["mlp_w1_0"], w["mlp_w2_0"]).reshape(bsz, total, d)
    return _sequence_rowwise(ctx_len)("res_mod0b", _f_res_modulate, [(d, F32), (d, BF16)])((x1, y), (), (g2, sh1, sc1), (w["norm1_g1"],), ())


def _layer1(x2, h, modrows1, w, bsz, length, ctx_len):
    d = x2.shape[-1]
    total = ctx_len + length
    half = B_GROUPS * B_GROUP_DIM
    rowwise, flat = _sequence_rowwise(ctx_len), _flat
    _, _, g1n, sh2n, sc2n, g2n = _segment_params(modrows1, bsz)
    g_cq, g_ckv, g_kr, g_za, g_zg = jnp.split(w["od_w_in"], [C_Q_RANK, C_Q_RANK + C_KV_RANK, C_Q_RANK + C_KV_RANK + C_ROPE,
                                                             C_Q_RANK + C_KV_RANK + C_ROPE + half], axis=0)
    lanes, c_qk = 128, C_NOPE + C_ROPE
    g_kr = jnp.pad(g_kr, ((C_NOPE, lanes - c_qk), (0, 0)))
    cq, ckv, kr, za, zg = [t.reshape(bsz, total, -1) for t in _linear_multi("od_in", (F32,) * 5)(flat(h), (g_cq, g_ckv, g_kr, g_za, g_zg))]
    lat = slice(ctx_len, total)
    lat_tl = math.gcd(256, length)
    (cqn,) = _rowwise("od_qn", _f_rms, [(C_Q_RANK, BF16)], lat_tl)((cq[:, lat],), (), (), (w["od_q_norm_g"],), ())
    w_uq = jnp.pad(w["od_w_uq"].reshape(C_HEADS, c_qk, C_Q_RANK), ((0, 0), (0, lanes - c_qk), (0, 0))).reshape(C_HEADS * lanes, C_Q_RANK)
    qf = _linear("od_uq")(flat(cqn), w_uq).reshape(bsz, length, C_HEADS * lanes)
    cos_q, sin_q = _rope_tables(length, C_ROPE, lanes, C_HEADS, 0, tail=lanes - c_qk)
    (q,) = _rowwise("od_qrope", _f_rope, [(C_HEADS * lanes, BF16)], lat_tl)((qf,), (cos_q, sin_q), (), (), ())
    (ckvn,) = rowwise("od_kvn", _f_rms, [(C_KV_RANK, BF16)])((ckv,), (), (), (w["od_kv_norm_g"],), ())
    per_head = w["od_w_ukv"].reshape(C_HEADS, C_NOPE + C_V, C_KV_RANK)
    w_kn = jnp.pad(per_head[:, :C_NOPE], ((0, 0), (0, lanes - C_NOPE), (0, 0))).reshape(C_HEADS * lanes, C_KV_RANK)
    w_v = jnp.pad(per_head[:, C_NOPE:], ((0, 0), (0, lanes - C_V), (0, 0))).reshape(C_HEADS * lanes, C_KV_RANK)
    kn, vv = [t.reshape(bsz, total, -1) for t in _linear_multi("od_ukv", (BF16, BF16))(flat(ckvn), (w_kn, w_v))]
    cos_r, sin_r = _rope_tables(length, C_ROPE, lanes, 1, ctx_len, tail=lanes - c_qk)
    (krr,) = rowwise("od_krope", _f_rope, [(lanes, BF16)])((kr,), (cos_r, sin_r), (), (), ())
    o_att = _attention_latent_rows("mla", C_HEADS, C_NOPE, C_ROPE, C_V, 4, 2)(q, kn, krr, vv)
    (glu,) = _rowwise("glu", _f_glu, [(half, F32)], lat_tl)((za[:, lat], zg[:, lat]), (), (), (), ())
    taps = jnp.concatenate([w["od_conv_w"][0], w["od_conv_b"]], axis=0)
    conv = _dwconv(glu, taps)
    (o_conv,) = _rowwise("ln_silu", _f_ln_silu, [(half, BF16)], lat_tl)((conv,), (), (), (w["od_ln_g"], w["od_ln_b"]), ())
    y = _linear_sum("out1")((flat(o_att), flat(o_conv)), tuple(jnp.split(w["w_out1"], 2, axis=0))).reshape(bsz, length, d)
    lat_param = lambda p: p[:, 1:]
    x3, h = _rowwise("res_mod1a", _f_res_modulate, [(d, F32), (d, BF16)], lat_tl)(
        (x2[:, lat], y), (), (lat_param(g1n), lat_param(sh2n), lat_param(sc2n)), (w["norm2_g1"],), ())
    y = _mlp("mlp1")(flat(h), w["mlp_w1_1"], w["mlp_w2_1"]).reshape(bsz, length, d)
    (x4,) = _rowwise("res1b", _f_res, [(d, F32)], lat_tl)((x3, y), (), (lat_param(g2n),), (), ())
    return x4


def kernel(x, c, ctx, c_ctx, ada_w, ada_b, norm1_g, norm2_g, w_out, mlp_w1, mlp_w2, ev_w_in, ev_q_norm_g, ev_k_norm_g, ev_sgu_norm_g, ev_sgu_w, ev_sgu_b, od_w_in, od_q_norm_g, od_kv_norm_g, od_w_uq, od_w_ukv, od_conv_w, od_conv_b, od_ln_g, od_ln_b, final_g, loss_target, m_c_ctx, m_ada_w, m_ada_b, m_norm1_g, m_norm2_g, m_w_out, m_mlp_w1, m_mlp_w2, m_ev_w_in, m_ev_q_norm_g, m_ev_k_norm_g, m_ev_sgu_norm_g, m_ev_sgu_w, m_ev_sgu_b, m_od_w_in, m_od_q_norm_g, m_od_kv_norm_g, m_od_w_uq, m_od_w_ukv, m_od_conv_w, m_od_conv_b, m_od_ln_g, m_od_ln_b, m_final_g, v_c_ctx, v_ada_w, v_ada_b, v_norm1_g, v_norm2_g, v_w_out, v_mlp_w1, v_mlp_w2, v_ev_w_in, v_ev_q_norm_g, v_ev_k_norm_g, v_ev_sgu_norm_g, v_ev_sgu_w, v_ev_sgu_b, v_od_w_in, v_od_q_norm_g, v_od_kv_norm_g, v_od_w_uq, v_od_w_ukv, v_od_conv_w, v_od_conv_b, v_od_ln_g, v_od_ln_b, v_final_g):
    names = ["c_ctx", "ada_w", "ada_b", "norm1_g", "norm2_g", "w_out", "mlp_w1", "mlp_w2", "ev_w_in", "ev_q_norm_g", "ev_k_norm_g",
             "ev_sgu_norm_g", "ev_sgu_w", "ev_sgu_b", "od_w_in", "od_q_norm_g", "od_kv_norm_g", "od_w_uq", "od_w_ukv", "od_conv_w",
             "od_conv_b", "od_ln_g", "od_ln_b", "final_g"]
    local = dict(zip(names, [c_ctx, ada_w, ada_b, norm1_g, norm2_g, w_out, mlp_w1, mlp_w2, ev_w_in, ev_q_norm_g, ev_k_norm_g, ev_sgu_norm_g, ev_sgu_w, ev_sgu_b, od_w_in, od_q_norm_g, od_kv_norm_g, od_w_uq, od_w_ukv, od_conv_w, od_conv_b, od_ln_g, od_ln_b, final_g]))
    mom1 = dict(zip(names, [m_c_ctx, m_ada_w, m_ada_b, m_norm1_g, m_norm2_g, m_w_out, m_mlp_w1, m_mlp_w2, m_ev_w_in, m_ev_q_norm_g, m_ev_k_norm_g, m_ev_sgu_norm_g, m_ev_sgu_w, m_ev_sgu_b, m_od_w_in, m_od_q_norm_g, m_od_kv_norm_g, m_od_w_uq, m_od_w_ukv, m_od_conv_w, m_od_conv_b, m_od_ln_g, m_od_ln_b, m_final_g]))
    mom2 = dict(zip(names, [v_c_ctx, v_ada_w, v_ada_b, v_norm1_g, v_norm2_g, v_w_out, v_mlp_w1, v_mlp_w2, v_ev_w_in, v_ev_q_norm_g, v_ev_k_norm_g, v_ev_sgu_norm_g, v_ev_sgu_w, v_ev_sgu_b, v_od_w_in, v_od_q_norm_g, v_od_kv_norm_g, v_od_w_uq, v_od_w_ukv, v_od_conv_w, v_od_conv_b, v_od_ln_g, v_od_ln_b, v_final_g]))
    bsz, length, d = x.shape
    ctx_len = ctx.shape[1]
    me = _block_index(_mesh_pos())

    shard = {p: local[wn][layer] for p, wn, layer, _, _ in _PIECES}
    by_columns = {p: axis == 1 for p, _, _, axis, _ in _PIECES}
    stages = {s: [p for p, _, _, _, st in _PIECES if st == s] for s in "abc"}
    block_rows = {p: shard[p].shape[1] if by_columns[p] else shard[p].shape[0] for p in shard}

    def pad_block_rows(t, p):
        extra = -block_rows[p] % ROW_ALIGN
        return jnp.pad(t, [(0, 0)] * (t.ndim - 2) + [(0, extra), (0, 0)]) if extra else t

    def travelling(p):
        t = shard[p].astype(BF16)
        return pad_block_rows(t.T if by_columns[p] else t, p)

    mine = {s: [travelling(p) for p in stages[s]] for s in "abc"}
    tiny_pack = _Packing([(n, local[n].size) for n, _ in _SMALL_SHARDED], 8)

    def unpack_weights(s, gathered):
        return {p: g[:, :block_rows[p]].reshape(N_DEV * block_rows[p], g.shape[2]) for p, g in zip(stages[s], gathered)}

    def pack_grads(s, g):
        return [pad_block_rows(g[p].reshape(N_DEV, block_rows[p], g[p].shape[1]), p) for p in stages[s]]

    cond_local = jnp.concatenate([c, c_ctx[None], jnp.zeros((COND_ROWS - bsz - 1, d), F32)], axis=0)
    cond, gathered_tiny, *gathered_a = _all_gather(
        [cond_local, tiny_pack.pack({n: local[n] for n, _ in _SMALL_SHARDED})] + mine["a"], "gather_inputs")
    cond = cond.reshape(N_DEV * COND_ROWS, d)
    silu_op = _rowwise("silu", _f_silu, [(d, F32)], N_DEV * COND_ROWS)
    silu_rows, silu_pullback = jax.vjp(lambda r: silu_op((r[None],), (), (), (), ())[0][0], cond)
    mod_cols = ada_w.shape[2]
    mod_part = jnp.concatenate([_mm(silu_rows, ada_w[i], "nn", (F32,), f"ada{i}_fwd")[0] for i in range(2)], axis=0)
    (mod_all,) = _all_gather([mod_part], "gather_mod")
    mod_all = mod_all.reshape(N_DEV, 2, N_DEV * COND_ROWS, mod_cols)
    modrows = []
    for i in range(2):
        whole = mod_all[:, i].transpose(1, 0, 2).reshape(N_DEV * COND_ROWS, N_DEV * mod_cols) + ada_b[i]
        modrows.append(lax.dynamic_slice_in_dim(whole, me * COND_ROWS, COND_ROWS, axis=0)[:bsz + 1])

    weights_a = unpack_weights("a", gathered_a)
    gather_b, token_b = _push_start(mine["b"], mod_all, "gather_weights_b_start")
    gather_c, token_c = _push_start(mine["c"], token_b, "gather_weights_c_start")
    full = {n: local[n] for n in _REPLICATED}
    for n, axis in _SMALL_SHARDED:
        full[n] = _unshard(tiny_pack.piece(gathered_tiny, n, (N_DEV,)).reshape((N_DEV,) + local[n].shape), axis)

    xall = jnp.concatenate([ctx, x], axis=1)
    modrows0, modrows1 = modrows[0] + token_c[0, 0], modrows[1]
    w_a = dict(weights_a, norm1_g0=norm1_g[0][None], norm2_g0=norm2_g[0][None],
               **{n: full[n] for n in ("ev_q_norm_g", "ev_k_norm_g", "ev_sgu_norm_g", "ev_sgu_w", "ev_sgu_b")})
    (x1, h0), pull_a = jax.vjp(lambda x_, m0, w: _mixer0(x_, m0, w, bsz, length, ctx_len), xall, modrows0, w_a)
    w_b = dict(unpack_weights("b", _push_wait(gather_b, x1, mine["b"], me, "gather_weights_b_wait")), norm1_g1=norm1_g[1][None])
    (x2, h1), pull_b = jax.vjp(lambda x_, h_, m0, m1, w: _mlp0(x_, h_, m0, m1, w, bsz, length, ctx_len), x1, h0, modrows0, modrows1, w_b)
    w_c = dict(unpack_weights("c", _push_wait(gather_c, x2, mine["c"], me, "gather_weights_c_wait")), norm2_g1=norm2_g[1][None],
               **{n: full[n] for n in ("od_q_norm_g", "od_kv_norm_g", "od_conv_w", "od_conv_b", "od_ln_g", "od_ln_b")})
    x4, pull_c = jax.vjp(lambda x_, h_, m1, w: _layer1(x_, h_, m1, w, bsz, length, ctx_len), x2, h1, modrows1, w_c)
    loss_part, dx4, dfinal = _loss_head(x4, loss_target, final_g[None])
    loss = lax.psum(loss_part[0, 0], MESH_AXES)

    dx2, dh1, dmod1_c, g_c = pull_c(dx4)
    grads_c = pack_grads("c", g_c)
    exchange_c, token = _push_start(grads_c, dx2, "exchange_grads_c_start")
    dx1, dh0, dmod0_b, dmod1_b, g_b = pull_b((dx2, dh1 + token[0, 0].astype(dh1.dtype)))
    grads_b = pack_grads("b", g_b)
    exchange_b, token = _push_start(grads_b, dx1, "exchange_grads_b_start")
    dxall, dmod0_a, g_a = pull_a((dx1, dh0 + token[0, 0].astype(dh0.dtype)))
    grad_x = dxall[:, ctx_len:]
    dmodrows = [dmod0_a + dmod0_b, dmod1_b + dmod1_c]
    grads = {n: g[n] for g in (g_a, g_c) for n in g if n in full}
    grads["norm1_g"] = jnp.concatenate([g_a["norm1_g0"], g_b["norm1_g1"]], axis=0)
    grads["norm2_g"] = jnp.concatenate([g_a["norm2_g0"], g_c["norm2_g1"]], axis=0)
    grads["final_g"] = dfinal[0]

    dmod_local = jnp.concatenate([jnp.pad(dm, ((0, COND_ROWS - bsz - 1), (0, 0))) for dm in dmodrows], axis=0)
    (dmod_all,) = _all_gather([dmod_local], "gather_dmod")
    dmod_all = dmod_all.reshape(N_DEV, 2, COND_ROWS, N_DEV * mod_cols)
    grads_a = pack_grads("a", g_a)
    exchange_a, token = _push_start(grads_a, dmod_all, "exchange_grads_a_start")
    dmod_all = dmod_all + token[0, 0]
    reduced = {}
    grad_ada_w, grad_ada_b, dmod_mine = [], [], []
    for i in range(2):
        dmod = dmod_all[:, i].reshape(N_DEV * COND_ROWS, N_DEV * mod_cols)
        grad_ada_b.append(_sum_rows(dmod, f"ada{i}_db")[0])
        dmod_mine.append(lax.dynamic_slice_in_dim(dmod, me * mod_cols, mod_cols, axis=1))
        grad_ada_w.append(_mm(silu_rows, dmod_mine[i], "tn", (F32,), f"ada{i}_dw")[0])
    reduced["ada_w"], reduced["ada_b"] = jnp.stack(grad_ada_w), jnp.stack(grad_ada_b)
    dsilu = _mm(jnp.concatenate(dmod_mine, axis=1), jnp.concatenate([ada_w[0], ada_w[1]], axis=1), "nt", (F32,), "ada_dx")[0]
    (dcond,) = silu_pullback(dsilu)
    grads["c_ctx"] = _sum_rows(dcond.reshape(N_DEV, COND_ROWS, d)[:, bsz], "c_ctx_rows")[0]

    small_names = list(_REPLICATED) + [n for n, _ in _SMALL_SHARDED]
    small_pack = _Packing([(n, full[n].size) for n in small_names], 8)
    (small_all,) = _all_gather([small_pack.pack({n: grads[n].astype(F32) for n in small_names})], "gather_small_grads")
    small_sum = _sum_blocks(small_all, "sum_small_grads")
    for n in _REPLICATED:
        reduced[n] = small_pack.piece(small_sum, n).reshape(local[n].shape)
    for n, axis in _SMALL_SHARDED:
        whole = small_pack.piece(small_sum, n).reshape(full[n].shape)
        reduced[n] = lax.dynamic_slice_in_dim(whole, me * local[n].shape[axis], local[n].shape[axis], axis=axis)

    def own_blocks(blocks):
        return [lax.dynamic_index_in_dim(t, me, 0, keepdims=False) for t in blocks]

    received = {s: _push_wait(handle, small_sum, own_blocks(blocks), me, f"exchange_grads_{s}_wait")
                for s, handle, blocks in (("c", exchange_c, grads_c), ("b", exchange_b, grads_b), ("a", exchange_a, grads_a))}
    piece_grad = {}
    for s in "abc":
        for p, blocks in zip(stages[s], received[s]):
            summed = _sum_blocks(blocks, "sum_grads_" + p)[:block_rows[p]]
            piece_grad[p] = summed.T if by_columns[p] else summed
    for n in ("w_out", "mlp_w1", "mlp_w2"):
        reduced[n] = jnp.stack([piece_grad[p] for p, wn, _, _, _ in _PIECES if wn == n])
    for n in ("ev_w_in", "od_w_in", "od_w_uq", "od_w_ukv"):
        reduced[n] = piece_grad[n][None]

    delta, new_m, new_v = {}, {}, {}
    for n in names:
        delta[n], new_m[n], new_v[n] = _adamw(local[n], reduced[n], mom1[n], mom2[n], "adamw_" + n)
    return (loss, grad_x, *[reduced[n] for n in names], *[delta[n] for n in names], *[new_m[n] for n in names], *[new_v[n] for n in names])
```

```python
import functools
import math

import jax
import jax.numpy as jnp
from jax import lax
from jax.experimental import pallas as pl
from jax.experimental.pallas import tpu as pltpu

F32, BF16 = jnp.float32, jnp.bfloat16

EPS = 1e-6
GRID_W = 64
ROPE_THETA = 10000.0
A_HEAD_DIM, A_Q_HEADS, A_KV_HEADS = 64, 8, 2
B_GROUPS, B_GROUP_DIM, B_CHUNK = 8, 64, 128
C_HEADS, C_NOPE, C_ROPE, C_V, C_Q_RANK, C_KV_RANK = 8, 64, 32, 64, 256, 128
D_CONV = 31
CONV_PAD = D_CONV // 2
N_MOD = 6
N_DEV = 8
MESH_AXES = ("x", "y", "c")

ADAM_LR, ADAM_B1, ADAM_B2, ADAM_EPS, ADAM_WD, ADAM_STEP = 0.001, 0.9, 0.999, 1e-08, 0.01, 10

VMEM_LIMIT = 56 * 1024 * 1024
PACK_COLS = 1024
ROW_ALIGN = 16
SUBLANES = 8
CONV_ROWS = 32
COND_ROWS = 8


def _pcall(body, **kw):
    return pl.pallas_call(body, **kw)


def _params(sem=None):
    return pltpu.CompilerParams(dimension_semantics=sem, vmem_limit_bytes=VMEM_LIMIT)


def _pick(n, cands):
    for c in cands:
        if n % c == 0:
            return c
    return n


def _mm(a, b, mode, out_dtypes, name, epi=None, extras=()):
    if mode == "tn":
        kk, m = a.shape
    else:
        m, kk = a.shape
    n = b.shape[0] if mode == "nt" else b.shape[1]
    tm = _pick(m, (1152, 1024, 896, 768, 512, 256, 128))
    tn = _pick(n, (1024, 896, 768, 512, 256, 128))
    tk = kk if kk <= 1024 else _pick(kk, (2048, 1536, 1024, 896, 768, 512, 256, 128))
    nk = kk // tk
    ne, no = len(extras), len(out_dtypes)
    a_spec = pl.BlockSpec((tk, tm), lambda i, j, k: (k, i)) if mode == "tn" else pl.BlockSpec((tm, tk), lambda i, j, k: (i, k))
    b_spec = pl.BlockSpec((tn, tk), lambda i, j, k: (j, k)) if mode == "nt" else pl.BlockSpec((tk, tn), lambda i, j, k: (k, j))
    t_spec = pl.BlockSpec((tm, tn), lambda i, j, k: (i, j))
    dn = {"nn": ((1,), (0,)), "nt": ((1,), (1,)), "tn": ((0,), (0,))}[mode]

    def body(a_ref, b_ref, *rest):
        extra_refs, out_refs = rest[:ne], rest[ne:ne + no]

        def finish(acc):
            outs = (acc,) if epi is None else epi(acc, *[r[...] for r in extra_refs])
            for r, o in zip(out_refs, outs):
                r[...] = o.astype(r.dtype)

        part = lax.dot_general(a_ref[...].astype(BF16), b_ref[...].astype(BF16), (dn, ((), ())), preferred_element_type=F32)
        if nk == 1:
            finish(part)
        else:
            acc_ref = rest[-1]
            k = pl.program_id(2)

            @pl.when(k == 0)
            def _():
                acc_ref[...] = part

            @pl.when(k > 0)
            def _():
                acc_ref[...] += part

            @pl.when(k == nk - 1)
            def _():
                finish(acc_ref[...])

    outs = _pcall(
        body, name=name, grid=(m // tm, n // tn, nk),
        in_specs=[a_spec, b_spec] + [t_spec] * ne,
        out_specs=[t_spec] * no,
        out_shape=[jax.ShapeDtypeStruct((m, n), d) for d in out_dtypes],
        scratch_shapes=[pltpu.VMEM((tm, tn), F32)] if nk > 1 else [],
        compiler_params=_params(("parallel", "parallel", "arbitrary")),
    )(a, b, *extras)
    return outs


def _linear(name, out_dtype=F32):
    @jax.custom_vjp
    def op(x, wt):
        return _mm(x, wt, "nt", (out_dtype,), name + "_fwd")[0]

    def fwd(x, wt):
        return op(x, wt), (x, wt)

    def bwd(res, dy):
        x, wt = res
        return _mm(dy, wt, "nn", (x.dtype,), name + "_dx")[0], _mm(dy, x, "tn", (wt.dtype,), name + "_dw")[0]

    op.defvjp(fwd, bwd)
    return op


_NT, _NN, _TN = (((1,), (1,)), ((), ())), (((1,), (0,)), ((), ())), (((0,), (0,)), ((), ()))
_ROW_TILES = (1152, 1024, 768, 512, 256, 128)


def _whole(w):
    return pl.BlockSpec(w.shape, lambda i: (0, 0))


def _groups_apply(x, ws, out_dtypes, name):
    n, (m, kk) = len(ws), x.shape
    tm = _pick(m, _ROW_TILES)

    def body(x_ref, *refs):
        a = x_ref[...].astype(BF16)
        for w_ref, o_ref in zip(refs[:n], refs[n:]):
            o_ref[...] = lax.dot_general(a, w_ref[...], _NT, preferred_element_type=F32).astype(o_ref.dtype)

    return _pcall(
        body, name=name, grid=(m // tm,),
        in_specs=[pl.BlockSpec((tm, kk), lambda i: (i, 0))] + [_whole(w) for w in ws],
        out_specs=[pl.BlockSpec((tm, w.shape[0]), lambda i: (i, 0)) for w in ws],
        out_shape=[jax.ShapeDtypeStruct((m, w.shape[0]), dt) for w, dt in zip(ws, out_dtypes)],
        compiler_params=_params(("parallel",)),
    )(x, *ws)


def _groups_sum(xs, ws, out_dtype, name):
    n, m, kk = len(ws), xs[0].shape[0], ws[0].shape[1]
    tm = _pick(m, _ROW_TILES)

    def body(*refs):
        acc = None
        for x_ref, w_ref in zip(refs[:n], refs[n:2 * n]):
            part = lax.dot_general(x_ref[...].astype(BF16), w_ref[...], _NN, preferred_element_type=F32)
            acc = part if acc is None else acc + part
        refs[2 * n][...] = acc.astype(out_dtype)

    return _pcall(
        body, name=name, grid=(m // tm,),
        in_specs=[pl.BlockSpec((tm, w.shape[0]), lambda i: (i, 0)) for w in ws] + [_whole(w) for w in ws],
        out_specs=pl.BlockSpec((tm, kk), lambda i: (i, 0)), out_shape=jax.ShapeDtypeStruct((m, kk), out_dtype),
        compiler_params=_params(("parallel",)),
    )(*xs, *ws)


def _groups_outer(xs, y, ws, name):
    n, (m, kk) = len(ws), y.shape
    tk = _pick(m, (768, 512, 256, 128))
    steps = m // tk

    def body(y_ref, *refs):
        x_refs, o_refs, acc_refs = refs[:n], refs[n:2 * n], refs[2 * n:]
        k = pl.program_id(0)
        b = y_ref[...].astype(BF16)
        for x_ref, o_ref, acc_ref in zip(x_refs, o_refs, acc_refs):
            part = lax.dot_general(x_ref[...].astype(BF16), b, _TN, preferred_element_type=F32)

            @pl.when(k == 0)
            def _(acc_ref=acc_ref, part=part):
                acc_ref[...] = part

            @pl.when(k > 0)
            def _(acc_ref=acc_ref, part=part):
                acc_ref[...] += part

            @pl.when(k == steps - 1)
            def _(acc_ref=acc_ref, o_ref=o_ref):
                o_ref[...] = acc_ref[...].astype(o_ref.dtype)

    return _pcall(
        body, name=name, grid=(steps,),
        in_specs=[pl.BlockSpec((tk, kk), lambda k: (k, 0))] + [pl.BlockSpec((tk, w.shape[0]), lambda k: (k, 0)) for w in ws],
        out_specs=[_whole(w) for w in ws], out_shape=[jax.ShapeDtypeStruct(w.shape, w.dtype) for w in ws],
        scratch_shapes=[pltpu.VMEM(w.shape, F32) for w in ws], compiler_params=_params(("arbitrary",)),
    )(y, *xs)


def _linear_multi(name, out_dtypes):
    @jax.custom_vjp
    def op(x, wts):
        return tuple(_groups_apply(x, wts, out_dtypes, name + "_fwd"))

    def fwd(x, wts):
        return op(x, wts), (x, wts)

    def bwd(res, dys):
        x, wts = res
        return _groups_sum(dys, wts, x.dtype, name + "_dx"), tuple(_groups_outer(dys, x, wts, name + "_dw"))

    op.defvjp(fwd, bwd)
    return op


def _linear_sum(name):
    @jax.custom_vjp
    def op(xs, ws):
        return _groups_sum(xs, ws, F32, name + "_fwd")

    def fwd(xs, ws):
        return op(xs, ws), (xs, ws)

    def bwd(res, dy):
        xs, ws = res
        return tuple(_groups_apply(dy, ws, [x.dtype for x in xs], name + "_dx")), tuple(_groups_outer(xs, dy, ws, name + "_dw"))

    op.defvjp(fwd, bwd)
    return op


def _relu2_epi(acc):
    return jnp.square(jnp.maximum(acc, 0.0)), acc


def _relu2_bwd_epi(acc, a):
    return (acc * (2.0 * jnp.maximum(a.astype(F32), 0.0)),)


def _mlp(name):
    @jax.custom_vjp
    def op(h, w1t, w2):
        s, _ = _mm(h, w1t, "nt", (BF16, BF16), name + "_up", epi=_relu2_epi)
        return _mm(s, w2, "nn", (F32,), name + "_down")[0]

    def fwd(h, w1t, w2):
        s, a = _mm(h, w1t, "nt", (BF16, BF16), name + "_up", epi=_relu2_epi)
        return _mm(s, w2, "nn", (F32,), name + "_down")[0], (h, w1t, w2, s, a)

    def bwd(res, dy):
        h, w1t, w2, s, a = res
        da = _mm(dy, w2, "nt", (BF16,), name + "_ds", epi=_relu2_bwd_epi, extras=(a,))[0]
        dw2 = _mm(s, dy, "tn", (w2.dtype,), name + "_dw2")[0]
        dw1t = _mm(da, h, "tn", (w1t.dtype,), name + "_dw1")[0]
        dh = _mm(da, w1t, "nn", (h.dtype,), name + "_dh")[0]
        return dh, dw1t, dw2

    op.defvjp(fwd, bwd)
    return op


def _two_pass_dot(x, m):
    hi = x.astype(BF16)
    lo = (x - hi.astype(F32)).astype(BF16)
    mb = m.astype(BF16)
    return jnp.dot(hi, mb, preferred_element_type=F32) + jnp.dot(lo, mb, preferred_element_type=F32)


@jax.custom_vjp
def _sym_dot(x, m):
    return _two_pass_dot(x, m)


def _sym_dot_fwd(x, m):
    return _two_pass_dot(x, m), m


def _sym_dot_bwd(m, g):
    return _two_pass_dot(g, m), jnp.zeros_like(m)


_sym_dot.defvjp(_sym_dot_fwd, _sym_dot_bwd)


def _neighbour(x):
    lane = lax.broadcasted_iota(jnp.int32, x.shape, 1)
    return jnp.where(lane % 2 == 0, pltpu.roll(x, x.shape[1] - 1, 1), pltpu.roll(x, 1, 1))


@jax.custom_vjp
def _swap_pairs(x):
    return _neighbour(x)


_swap_pairs.defvjp(lambda x: (_neighbour(x), None), lambda _, g: (_neighbour(g),))


def _lane_group(x, k):
    return x[:, k * B_GROUP_DIM:(k + 1) * B_GROUP_DIM]


@jax.custom_vjp
def _group_mix(ws, v):
    return jnp.concatenate([jnp.dot(w.astype(BF16), _lane_group(v, k).astype(BF16), preferred_element_type=F32)
                            for k, w in enumerate(ws)], axis=1)


def _group_mix_fwd(ws, v):
    return _group_mix(ws, v), (ws, v)


def _group_mix_bwd(res, d):
    ws, v = res
    parts = [(_lane_group(d, k).astype(BF16), _lane_group(v, k).astype(BF16)) for k in range(len(ws))]
    dws = tuple(lax.dot_general(dk, vk, (((1,), (1,)), ((), ())), preferred_element_type=F32) for dk, vk in parts)
    dv = jnp.concatenate([lax.dot_general(w.astype(BF16), dk, (((0,), (0,)), ((), ())), preferred_element_type=F32)
                          for w, (dk, _) in zip(ws, parts)], axis=1)
    return dws, dv


_group_mix.defvjp(_group_mix_fwd, _group_mix_bwd)


def _rowwise(name, f, out_specs, tl, ctx_blocks=0):
    def seg(l, s):
        return jnp.where(l >= ctx_blocks, s - 1, 0) if s > 1 else 0

    def specs(rows, tabs, pers, glbs, consts):
        row_specs = [pl.BlockSpec((1, tl, r.shape[2]), lambda b, l: (b, l, 0)) for r in rows]
        tab_specs = [pl.BlockSpec((tl, t.shape[1]), lambda b, l: (l, 0)) for t in tabs]
        per_specs = [pl.BlockSpec((1, 1, 1, p.shape[3]), functools.partial(lambda b, l, s: (b, seg(l, s), 0, 0), s=p.shape[1])) for p in pers]
        glb_specs = [pl.BlockSpec(g.shape, functools.partial(lambda b, l, nd: (0,) * nd, nd=g.ndim)) for g in glbs]
        const_specs = [pl.BlockSpec(c.shape, functools.partial(lambda b, l, nd: (0,) * nd, nd=c.ndim)) for c in consts]
        return row_specs, tab_specs, per_specs, glb_specs, const_specs

    def load(refs_rows, refs_tabs, refs_pers, refs_glbs, refs_consts):
        return (tuple(r[0].astype(F32) for r in refs_rows), tuple(t[...] for t in refs_tabs),
                tuple(p[0, 0].astype(F32) for p in refs_pers), tuple(g[...].astype(F32) for g in refs_glbs),
                tuple(c[...] for c in refs_consts))

    def call_fwd(rows, tabs, pers, glbs, consts):
        bsz, length = rows[0].shape[:2]
        nr, nt, npp, ng, nc = len(rows), len(tabs), len(pers), len(glbs), len(consts)
        rs, ts, ps, gs, cs = specs(rows, tabs, pers, glbs, consts)

        def body(*refs):
            ins, outs = refs[:nr + nt + npp + ng + nc], refs[nr + nt + npp + ng + nc:]
            r, t, p, g, c = load(ins[:nr], ins[nr:nr + nt], ins[nr + nt:nr + nt + npp], ins[nr + nt + npp:nr + nt + npp + ng], ins[nr + nt + npp + ng:])
            for o_ref, o in zip(outs, f(r, t, p, g, c)):
                o_ref[0] = o.astype(o_ref.dtype)

        return _pcall(
            body, name=name + "_fwd", grid=(bsz, length // tl),
            in_specs=rs + ts + ps + gs + cs,
            out_specs=[pl.BlockSpec((1, tl, w), lambda b, l: (b, l, 0)) for w, _ in out_specs],
            out_shape=[jax.ShapeDtypeStruct((bsz, length, w), d) for w, d in out_specs],
            compiler_params=_params(("parallel", "parallel")),
        )(*rows, *tabs, *pers, *glbs, *consts)

    def call_bwd(rows, tabs, pers, glbs, consts, cts):
        bsz, length = rows[0].shape[:2]
        nr, nt, npp, ng, nc, no = len(rows), len(tabs), len(pers), len(glbs), len(consts), len(cts)
        rs, ts, ps, gs, cs = specs(rows, tabs, pers, glbs, consts)
        n_in = nr + nt + npp + ng + nc

        def body(*refs):
            ins, ct_refs, outs = refs[:n_in], refs[n_in:n_in + no], refs[n_in + no:]
            r, t, p, g, c = load(ins[:nr], ins[nr:nr + nt], ins[nr + nt:nr + nt + npp], ins[nr + nt + npp:nr + nt + npp + ng], ins[nr + nt + npp + ng:])
            _, vjp = jax.vjp(lambda r_, p_, g_: tuple(f(r_, t, p_, g_, c)), r, p, g)
            dr, dp, dg = vjp(tuple(ct[0].astype(F32) for ct in ct_refs))
            dr_refs, dp_refs, dg_refs = outs[:nr], outs[nr:nr + npp], outs[nr + npp:]
            for ref, d in zip(dr_refs, dr):
                ref[0] = d.astype(ref.dtype)
            b, l = pl.program_id(0), pl.program_id(1)
            first_of_segment = (l == 0) | (l == ctx_blocks)
            for ref, d in zip(dp_refs, dp):
                @pl.when(first_of_segment)
                def _(ref=ref, d=d):
                    ref[0, 0] = d

                @pl.when(jnp.logical_not(first_of_segment))
                def _(ref=ref, d=d):
                    ref[0, 0] += d
            first = (b == 0) & (l == 0)
            for ref, d in zip(dg_refs, dg):
                @pl.when(first)
                def _(ref=ref, d=d):
                    ref[...] = d

                @pl.when(jnp.logical_not(first))
                def _(ref=ref, d=d):
                    ref[...] += d

        ct_specs = [pl.BlockSpec((1, tl, w), lambda b, l: (b, l, 0)) for w, _ in out_specs]
        outs = _pcall(
            body, name=name + "_bwd", grid=(bsz, length // tl),
            in_specs=rs + ts + ps + gs + cs + ct_specs,
            out_specs=rs + ps + gs,
            out_shape=[jax.ShapeDtypeStruct(r.shape, r.dtype) for r in rows]
            + [jax.ShapeDtypeStruct(p.shape, F32) for p in pers] + [jax.ShapeDtypeStruct(g.shape, F32) for g in glbs],
            compiler_params=_params(("arbitrary", "arbitrary")),
        )(*rows, *tabs, *pers, *glbs, *consts, *cts)
        return tuple(outs[:nr]), tuple(outs[nr:nr + npp]), tuple(outs[nr + npp:])

    @jax.custom_vjp
    def op(rows, tabs, pers, glbs, consts):
        return tuple(call_fwd(rows, tabs, pers, glbs, consts))

    def fwd(rows, tabs, pers, glbs, consts):
        return op(rows, tabs, pers, glbs, consts), (rows, tabs, pers, glbs, consts)

    def bwd(res, cts):
        rows, tabs, pers, glbs, consts = res
        dr, dp, dg = call_bwd(rows, tabs, pers, glbs, consts, tuple(cts))
        dp = tuple(d.astype(p.dtype) for d, p in zip(dp, pers))
        dg = tuple(d.astype(g.dtype) for d, g in zip(dg, glbs))
        return dr, tuple(jnp.zeros_like(t) for t in tabs), dp, dg, tuple(jnp.zeros_like(c) for c in consts)

    op.defvjp(fwd, bwd)
    return op


def _rms(x, g):
    return x * lax.rsqrt(jnp.mean(x * x, axis=-1, keepdims=True) + EPS) * g


def _f_silu(r, t, p, g, c):
    return (jax.nn.silu(r[0]),)


def _f_modulate(r, t, p, g, c):
    shift, scale = p
    return (_rms(r[0], g[0]) * (1.0 + scale) + shift,)


def _f_res_modulate(r, t, p, g, c):
    x, y = r
    gate, shift, scale = p
    xn = x + gate * y
    return xn, _rms(xn, g[0]) * (1.0 + scale) + shift


def _f_res(r, t, p, g, c):
    return (r[0] + p[0] * r[1],)


def _f_headnorm_rope(r, t, p, g, c):
    x = r[0]
    cos, sin = t
    xn = x * lax.rsqrt(_sym_dot(x * x, c[0]) + EPS) * g[0]
    return (xn * cos + _swap_pairs(xn) * sin,)


def _f_rope(r, t, p, g, c):
    x = r[0]
    cos, sin = t
    return (x * cos + _swap_pairs(x) * sin,)


def _f_rms(r, t, p, g, c):
    return (_rms(r[0], g[0]),)


def _f_sgu_pre(r, t, p, g, c):
    u = jax.nn.gelu(r[0])
    v = jax.nn.gelu(r[1])
    vn = v * lax.rsqrt(_sym_dot(v * v, c[0]) + EPS) * g[0]
    return u, vn


def _f_sgu_mix(r, t, p, g, c):
    u, vn = r
    return (u * (g[B_GROUPS] + _group_mix(tuple(g[:B_GROUPS]), vn)),)


def _f_glu(r, t, p, g, c):
    return (r[0] * jax.nn.sigmoid(r[1]),)


def _f_ln_silu(r, t, p, g, c):
    x = r[0]
    mu = jnp.mean(x, axis=-1, keepdims=True)
    var = jnp.mean(jnp.square(x - mu), axis=-1, keepdims=True)
    return (jax.nn.silu((x - mu) * lax.rsqrt(var + EPS) * g[0] + g[1]),)


LOG2_E = 1.4426950408889634
LN_2 = 0.6931471805599453


def _attention_rows(name, ctx_len, heads, kv_heads):
    grp = heads // kv_heads

    def by_segment(qi, cb, lk, run):
        if cb > 0:
            @pl.when(qi < cb)
            def _():
                run(ctx_len)

            @pl.when(qi >= cb)
            def _():
                run(lk)
        else:
            run(lk)

    def head(ref, h, hd):
        return ref[0, :, h * hd:(h + 1) * hd]

    def scores(q_tile, kk, hd):
        qs = (q_tile.astype(F32) * (LOG2_E * hd ** -0.5)).astype(BF16)
        return qs, lax.dot_general(qs, kk, _NT, preferred_element_type=F32)

    def shapes(q):
        bsz, length, width = q.shape
        tq = math.gcd(_pick(length, (256, 128)), ctx_len) if ctx_len else _pick(length, (256, 128))
        return bsz, length, width // heads, tq, ctx_len // tq

    def call_fwd(q, k, v):
        bsz, length, hd, tq, cb = shapes(q)

        def body(q_ref, k_ref, v_ref, o_ref, lse_ref):
            def run(nk):
                k_all, v_all = k_ref[0, :nk], v_ref[0, :nk]
                outs = []
                for j in range(kv_heads):
                    kk = k_all[:, j * hd:(j + 1) * hd]
                    v_ones = jnp.concatenate([v_all[:, j * hd:(j + 1) * hd], jnp.ones((nk, hd), BF16)], axis=1)
                    for h in range(j * grp, (j + 1) * grp):
                        _, s = scores(head(q_ref, h, hd), kk, hd)
                        m = jnp.max(s, axis=-1, keepdims=True)
                        acc = jnp.dot(jnp.exp2(s - m).astype(BF16), v_ones, preferred_element_type=F32)
                        l = acc[:, hd:hd + 1]
                        outs.append((acc[:, :hd] / l).astype(BF16))
                        lse_ref[0, h] = m + jnp.log2(l)
                o_ref[0] = jnp.concatenate(outs, axis=1)

            by_segment(pl.program_id(1), cb, length, run)

        q_spec = pl.BlockSpec((1, tq, heads * hd), lambda b, i: (b, i, 0))
        kv_spec = pl.BlockSpec((1, length, kv_heads * hd), lambda b, i: (b, 0, 0))
        return _pcall(
            body, name=name + "_fwd", grid=(bsz, length // tq), in_specs=[q_spec, kv_spec, kv_spec],
            out_specs=[q_spec, pl.BlockSpec((1, heads, tq, 1), lambda b, i: (b, 0, i, 0))],
            out_shape=[jax.ShapeDtypeStruct(q.shape, BF16), jax.ShapeDtypeStruct((bsz, heads, length, 1), F32)],
            compiler_params=_params(("parallel", "parallel")),
        )(q, k, v)

    def call_bwd(q, k, v, o, lse, do):
        bsz, length, hd, tq, cb = shapes(q)
        nq = length // tq

        def body(q_ref, k_ref, v_ref, o_ref, lse_ref, do_ref, dq_ref, dk_ref, dv_ref, dk_acc, dv_acc):
            qi = pl.program_id(1)

            @pl.when(qi == 0)
            def _():
                dk_acc[...] = jnp.zeros_like(dk_acc)
                dv_acc[...] = jnp.zeros_like(dv_acc)

            def run(nk):
                k_all, v_all = k_ref[0, :nk], v_ref[0, :nk]
                dqs, dks, dvs = [], [], []
                for j in range(kv_heads):
                    kk, vv = k_all[:, j * hd:(j + 1) * hd], v_all[:, j * hd:(j + 1) * hd]
                    dk_sum = dv_sum = None
                    for h in range(j * grp, (j + 1) * grp):
                        dd = head(do_ref, h, hd)
                        qs, s = scores(head(q_ref, h, hd), kk, hd)
                        p = jnp.exp2(s - lse_ref[0, h])
                        delta = jnp.sum(dd.astype(F32) * head(o_ref, h, hd).astype(F32), axis=-1, keepdims=True)
                        t = (p * (lax.dot_general(dd, vv, _NT, preferred_element_type=F32) - delta)).astype(BF16)
                        dqs.append((jnp.dot(t, kk, preferred_element_type=F32) * hd ** -0.5).astype(BF16))
                        dk_h = lax.dot_general(t, qs, _TN, preferred_element_type=F32)
                        dv_h = lax.dot_general(p.astype(BF16), dd, _TN, preferred_element_type=F32)
                        dk_sum = dk_h if dk_sum is None else dk_sum + dk_h
                        dv_sum = dv_h if dv_sum is None else dv_sum + dv_h
                    dks.append(dk_sum)
                    dvs.append(dv_sum)
                dq_ref[0] = jnp.concatenate(dqs, axis=1)
                dk_acc[:nk] += jnp.concatenate(dks, axis=1)
                dv_acc[:nk] += jnp.concatenate(dvs, axis=1)

            by_segment(qi, cb, length, run)

            @pl.when(qi == nq - 1)
            def _():
                dk_ref[0] = (dk_acc[...] * LN_2).astype(dk_ref.dtype)
                dv_ref[0] = dv_acc[...].astype(dv_ref.dtype)

        q_spec = pl.BlockSpec((1, tq, heads * hd), lambda b, i: (b, i, 0))
        kv_spec = pl.BlockSpec((1, length, kv_heads * hd), lambda b, i: (b, 0, 0))
        lse_spec = pl.BlockSpec((1, heads, tq, 1), lambda b, i: (b, 0, i, 0))
        return _pcall(
            body, name=name + "_bwd", grid=(bsz, nq), in_specs=[q_spec, kv_spec, kv_spec, q_spec, lse_spec, q_spec],
            out_specs=[q_spec, kv_spec, kv_spec],
            out_shape=[jax.ShapeDtypeStruct(q.shape, BF16), jax.ShapeDtypeStruct(k.shape, BF16), jax.ShapeDtypeStruct(v.shape, BF16)],
            scratch_shapes=[pltpu.VMEM(k.shape[1:], F32), pltpu.VMEM(v.shape[1:], F32)],
            compiler_params=_params(("parallel", "arbitrary")),
        )(q, k, v, o, lse, do)

    @jax.custom_vjp
    def op(q, k, v):
        return call_fwd(q, k, v)[0]

    def fwd(q, k, v):
        o, lse = call_fwd(q, k, v)
        return o, (q, k, v, o, lse)

    def bwd(res, do):
        q, k, v, o, lse = res
        dq, dk, dv = call_bwd(q, k, v, o, lse, do)
        return dq, dk.astype(k.dtype), dv.astype(v.dtype)

    op.defvjp(fwd, bwd)
    return op


def _attention_latent_rows(name, heads, nope, rope, dv, heads_forward, heads_backward):
    lanes = 128
    scale = (nope + rope) ** -0.5

    def shapes(q, kn):
        bsz, lq, _ = q.shape
        return bsz, lq, kn.shape[1], _pick(lq, (256, 128))

    def scores(q_tile, kk):
        qs = (q_tile.astype(F32) * (LOG2_E * scale)).astype(BF16)
        return qs, lax.dot_general(qs, kk, _NT, preferred_element_type=F32)

    def call_fwd(q, kn, kr, v):
        bsz, lq, lk, tq = shapes(q, kn)
        heads_per_step = heads_forward

        def body(q_ref, kn_ref, kr_ref, v_ref, o_ref, lse_ref):
            real = lax.broadcasted_iota(jnp.int32, (lk, lanes), 1) < dv
            outs = []
            for h in range(heads_per_step):
                kk = kn_ref[0, :, h * lanes:(h + 1) * lanes] + kr_ref[0]
                v_ones = jnp.where(real, v_ref[0, :, h * lanes:(h + 1) * lanes], jnp.ones((lk, lanes), BF16))
                _, s = scores(q_ref[0, :, h * lanes:(h + 1) * lanes], kk)
                m = jnp.max(s, axis=-1, keepdims=True)
                acc = jnp.dot(jnp.exp2(s - m).astype(BF16), v_ones, preferred_element_type=F32)
                l = acc[:, dv:dv + 1]
                outs.append((acc[:, :dv] / l).astype(BF16))
                lse_ref[0, h] = m + jnp.log2(l)
            o_ref[0] = jnp.concatenate(outs, axis=1)

        wide = heads_per_step * lanes
        return _pcall(
            body, name=name + "_fwd", grid=(bsz, heads // heads_per_step, lq // tq),
            in_specs=[pl.BlockSpec((1, tq, wide), lambda b, g, i: (b, i, g)), pl.BlockSpec((1, lk, wide), lambda b, g, i: (b, 0, g)),
                      pl.BlockSpec((1, lk, lanes), lambda b, g, i: (b, 0, 0)), pl.BlockSpec((1, lk, wide), lambda b, g, i: (b, 0, g))],
            out_specs=[pl.BlockSpec((1, tq, heads_per_step * dv), lambda b, g, i: (b, i, g)),
                       pl.BlockSpec((1, heads_per_step, tq, 1), lambda b, g, i: (b, g, i, 0))],
            out_shape=[jax.ShapeDtypeStruct((bsz, lq, heads * dv), BF16), jax.ShapeDtypeStruct((bsz, heads, lq, 1), F32)],
            compiler_params=_params(("parallel", "parallel", "parallel")),
        )(q, kn, kr, v)

    def call_bwd(q, kn, kr, v, o, lse, do):
        bsz, lq, lk, tq = shapes(q, kn)
        heads_per_step = heads_backward

        def body(q_ref, kn_ref, kr_ref, v_ref, o_ref, lse_ref, do_ref, dq_ref, dkn_ref, dkr_ref, dv_ref, dkn_acc, dkr_acc, dv_acc):
            g, qi = pl.program_id(1), pl.program_id(2)
            last_q = qi == lq // tq - 1

            @pl.when(qi == 0)
            def _():
                dkn_acc[...] = jnp.zeros_like(dkn_acc)
                dv_acc[...] = jnp.zeros_like(dv_acc)

            @pl.when((qi == 0) & (g == 0))
            def _():
                dkr_acc[...] = jnp.zeros_like(dkr_acc)

            dqs, dks, dvs = [], [], []
            for h in range(heads_per_step):
                kk = kn_ref[0, :, h * lanes:(h + 1) * lanes] + kr_ref[0]
                vv = v_ref[0, :, h * lanes:h * lanes + dv]
                dd = do_ref[0, :, h * dv:(h + 1) * dv]
                qs, s = scores(q_ref[0, :, h * lanes:(h + 1) * lanes], kk)
                p = jnp.exp2(s - lse_ref[0, h])
                delta = jnp.sum(dd.astype(F32) * o_ref[0, :, h * dv:(h + 1) * dv].astype(F32), axis=-1, keepdims=True)
                t = (p * (lax.dot_general(dd, vv, _NT, preferred_element_type=F32) - delta)).astype(BF16)
                dqs.append((jnp.dot(t, kk, preferred_element_type=F32) * scale).astype(BF16))
                dks.append(lax.dot_general(t, qs, _TN, preferred_element_type=F32) * LN_2)
                dv_h = lax.dot_general(p.astype(BF16), dd, _TN, preferred_element_type=F32)
                dvs.append(jnp.concatenate([dv_h, jnp.zeros((lk, lanes - dv), F32)], axis=1))
            dq_ref[0] = jnp.concatenate(dqs, axis=1)
            dkn_acc[...] += jnp.concatenate(dks, axis=1)
            dv_acc[...] += jnp.concatenate(dvs, axis=1)
            shared = dks[0]
            for d_h in dks[1:]:
                shared = shared + d_h
            dkr_acc[...] += shared

            @pl.when(last_q)
            def _():
                dkn_ref[0] = dkn_acc[...].astype(dkn_ref.dtype)
                dv_ref[0] = dv_acc[...].astype(dv_ref.dtype)

            @pl.when(last_q & (g == heads // heads_per_step - 1))
            def _():
                dkr_ref[0] = dkr_acc[...].astype(dkr_ref.dtype)

        wide = heads_per_step * lanes
        q_spec = pl.BlockSpec((1, tq, wide), lambda b, g, i: (b, i, g))
        k_spec = pl.BlockSpec((1, lk, wide), lambda b, g, i: (b, 0, g))
        kr_spec = pl.BlockSpec((1, lk, lanes), lambda b, g, i: (b, 0, 0))
        o_spec = pl.BlockSpec((1, tq, heads_per_step * dv), lambda b, g, i: (b, i, g))
        lse_spec = pl.BlockSpec((1, heads_per_step, tq, 1), lambda b, g, i: (b, g, i, 0))
        return _pcall(
            body, name=name + "_bwd", grid=(bsz, heads // heads_per_step, lq // tq),
            in_specs=[q_spec, k_spec, kr_spec, k_spec, o_spec, lse_spec, o_spec],
            out_specs=[q_spec, k_spec, kr_spec, k_spec],
            out_shape=[jax.ShapeDtypeStruct(q.shape, BF16), jax.ShapeDtypeStruct(kn.shape, BF16), jax.ShapeDtypeStruct(kr.shape, BF16),
                       jax.ShapeDtypeStruct(v.shape, BF16)],
            scratch_shapes=[pltpu.VMEM((lk, wide), F32), pltpu.VMEM((lk, lanes), F32), pltpu.VMEM((lk, wide), F32)],
            compiler_params=_params(("parallel", "arbitrary", "arbitrary")),
        )(q, kn, kr, v, o, lse, do)

    @jax.custom_vjp
    def op(q, kn, kr, v):
        return call_fwd(q, kn, kr, v)[0]

    def fwd(q, kn, kr, v):
        o, lse = call_fwd(q, kn, kr, v)
        return o, (q, kn, kr, v, o, lse)

    def bwd(res, do):
        q, kn, kr, v, o, lse = res
        dq, dkn, dkr, dv_ = call_bwd(q, kn, kr, v, o, lse, do)
        return dq, dkn.astype(kn.dtype), dkr.astype(kr.dtype), dv_.astype(v.dtype)

    op.defvjp(fwd, bwd)
    return op


def _conv_call(ypad, taps, name):
    bsz, lp, ch = ypad.shape
    length = lp - 2 * ROW_ALIGN
    tl = _pick(length, (256, 128))

    def body(y_ref, w_ref, o_ref):
        base = pl.multiple_of(pl.program_id(1) * tl, tl)
        for r0 in range(0, tl, CONV_ROWS):
            win = y_ref[0, pl.ds(base + r0, CONV_ROWS + 2 * ROW_ALIGN), :]
            acc = jnp.broadcast_to(w_ref[pl.ds(D_CONV, 1), :], (CONV_ROWS, ch))
            for shift in range(SUBLANES):
                shifted = win[shift:shift + CONV_ROWS + 2 * ROW_ALIGN - SUBLANES, :]
                for k in range(shift, D_CONV, SUBLANES):
                    acc = acc + shifted[k - shift:k - shift + CONV_ROWS, :] * w_ref[pl.ds(k, 1), :]
            o_ref[0, pl.ds(r0, CONV_ROWS), :] = acc

    return _pcall(
        body, name=name, grid=(bsz, length // tl),
        in_specs=[pl.BlockSpec((1, lp, ch), lambda b, l: (b, 0, 0)), pl.BlockSpec((D_CONV + 1, ch), lambda b, l: (0, 0))],
        out_specs=pl.BlockSpec((1, tl, ch), lambda b, l: (b, l, 0)),
        out_shape=jax.ShapeDtypeStruct((bsz, length, ch), F32),
        compiler_params=_params(("parallel", "parallel")),
    )(ypad, taps)


def _conv_dw_call(ypad, dout, name):
    bsz, lp, ch = ypad.shape
    length = lp - 2 * ROW_ALIGN
    tl = _pick(length, (256, 128))

    def body(y_ref, d_ref, o_ref):
        b, l = pl.program_id(0), pl.program_id(1)

        @pl.when((b == 0) & (l == 0))
        def _():
            o_ref[...] = jnp.zeros_like(o_ref)

        base = pl.multiple_of(l * tl, tl)
        win = y_ref[0, pl.ds(base, tl + 2 * ROW_ALIGN), :]
        dd = d_ref[0]
        for shift in range(SUBLANES):
            shifted = win[shift:shift + tl + 2 * ROW_ALIGN - SUBLANES, :]
            for k in range(shift, D_CONV, SUBLANES):
                o_ref[pl.ds(k, 1), :] += jnp.sum(shifted[k - shift:k - shift + tl, :] * dd, axis=0, keepdims=True)
        o_ref[pl.ds(D_CONV, 1), :] += jnp.sum(dd, axis=0, keepdims=True)

    return _pcall(
        body, name=name, grid=(bsz, length // tl),
        in_specs=[pl.BlockSpec((1, lp, ch), lambda b, l: (b, 0, 0)), pl.BlockSpec((1, tl, ch), lambda b, l: (b, l, 0))],
        out_specs=pl.BlockSpec((D_CONV + 1, ch), lambda b, l: (0, 0)),
        out_shape=jax.ShapeDtypeStruct((D_CONV + 1, ch), F32),
        compiler_params=_params(("arbitrary", "arbitrary")),
    )(ypad, dout)


def _pad_rows(y):
    return jnp.pad(y, ((0, 0), (CONV_PAD, 2 * ROW_ALIGN - CONV_PAD), (0, 0)))


@jax.custom_vjp
def _dwconv(y, taps):
    return _conv_call(_pad_rows(y), taps, "conv_fwd")


def _dwconv_fwd(y, taps):
    return _dwconv(y, taps), (y, taps)


def _dwconv_bwd(res, dout):
    y, taps = res
    flipped = jnp.concatenate([taps[:D_CONV][::-1], jnp.zeros_like(taps[D_CONV:])], axis=0)
    dy = _conv_call(_pad_rows(dout), flipped, "conv_dy")
    dtaps = _conv_dw_call(_pad_rows(y), dout, "conv_dw")
    return dy, dtaps


_dwconv.defvjp(_dwconv_fwd, _dwconv_bwd)


def _loss_head(x, target, g):
    bsz, length, d = x.shape
    tl = _pick(length, (256, 128))

    def f(xb, tb, gb):
        err = _rms(xb, gb) - tb
        return 0.5 * jnp.sum(jnp.sum(err * err, axis=-1, keepdims=True), axis=0, keepdims=True) / d

    def body(x_ref, t_ref, g_ref, loss_ref, dx_ref, dg_ref):
        val, vjp = jax.vjp(lambda xb, gb: f(xb, t_ref[0], gb), x_ref[0], g_ref[...])
        dx, dg = vjp(jnp.ones((1, 1), F32))
        dx_ref[0] = dx
        first = (pl.program_id(0) == 0) & (pl.program_id(1) == 0)

        @pl.when(first)
        def _():
            loss_ref[...] = val
            dg_ref[...] = dg

        @pl.when(jnp.logical_not(first))
        def _():
            loss_ref[...] += val
            dg_ref[...] += dg

    row = pl.BlockSpec((1, tl, d), lambda b, l: (b, l, 0))
    return _pcall(
        body, name="loss_head", grid=(bsz, length // tl),
        in_specs=[row, row, pl.BlockSpec((1, d), lambda b, l: (0, 0))],
        out_specs=[pl.BlockSpec((1, 1), lambda b, l: (0, 0)), row, pl.BlockSpec((1, d), lambda b, l: (0, 0))],
        out_shape=[jax.ShapeDtypeStruct((1, 1), F32), jax.ShapeDtypeStruct(x.shape, F32), jax.ShapeDtypeStruct((1, d), F32)],
        compiler_params=_params(("arbitrary", "arbitrary")),
    )(x, target, g)


def _adamw(w, g, m, v, name):
    shape = w.shape
    cols = shape[-1]
    rows = w.size // cols
    tr = _pick(rows, (512, 256, 128))
    w2, g2, m2, v2 = (t.reshape(rows, cols) for t in (w, g, m, v))

    def body(w_ref, g_ref, m_ref, v_ref, d_ref, nm_ref, nv_ref):
        gg = g_ref[...]
        nm = ADAM_B1 * m_ref[...] + (1.0 - ADAM_B1) * gg
        nv = ADAM_B2 * v_ref[...] + (1.0 - ADAM_B2) * jnp.square(gg)
        m_hat = nm / (1.0 - ADAM_B1 ** ADAM_STEP)
        v_hat = nv / (1.0 - ADAM_B2 ** ADAM_STEP)
        d_ref[...] = -ADAM_LR * (m_hat / (jnp.sqrt(v_hat) + ADAM_EPS) + ADAM_WD * w_ref[...])
        nm_ref[...] = nm
        nv_ref[...] = nv

    spec = pl.BlockSpec((tr, cols), lambda i: (i, 0))
    outs = _pcall(
        body, name=name, grid=(rows // tr,), in_specs=[spec] * 4, out_specs=[spec] * 3,
        out_shape=[jax.ShapeDtypeStruct((rows, cols), F32)] * 3,
        compiler_params=_params(("parallel",)),
    )(w2, g2, m2, v2)
    return tuple(o.reshape(shape) for o in outs)


def _mesh_pos():
    return lax.axis_index("x"), lax.axis_index("y"), lax.axis_index("c")


_RELATIONS = [(dx, dy, dc) for dx in (0, 1) for dy in (0, 1) for dc in (0, 1)][1:]


def _peer(pos, rel):
    return tuple(jnp.where(r == 1, 1 - p, p) if r else p for p, r in zip(pos, rel))


def _block_index(pos):
    return 4 * pos[0] + 2 * pos[1] + pos[2]


_HBM = pl.BlockSpec(memory_space=pltpu.HBM)


def _all_gather(xs, name):
    n = len(xs)

    def body(*refs):
        x_refs, out_refs, (send_sems, recv_sems, local_sems) = refs[:n], refs[n:2 * n], refs[2 * n:]
        x_, y_, c_ = _mesh_pos()
        me, sibling = (x_, y_, c_), (x_, y_, 1 - c_)
        chips = [(1 - x_, y_), (x_, 1 - y_), (1 - x_, 1 - y_)]

        def copy(t, k, block, to, own=False):
            slot = out_refs[t].at[_block_index(block)]
            return pltpu.make_async_remote_copy(
                src_ref=x_refs[t] if own else slot, dst_ref=slot, send_sem=send_sems.at[7 * t + k], recv_sem=recv_sems.at[7 * t + k],
                device_id=to, device_id_type=pl.DeviceIdType.MESH)

        mine = [pltpu.make_async_copy(x_refs[t], out_refs[t].at[_block_index(me)], local_sems.at[t]) for t in range(n)]
        first = [[copy(t, 0, me, sibling, own=True)] + [copy(t, 1 + j, me, (*chip, c_), own=True) for j, chip in enumerate(chips)]
                 for t in range(n)]
        passed = [[copy(t, 4 + j, (*chip, c_), sibling) for j, chip in enumerate(chips)] for t in range(n)]
        for t in range(n):
            mine[t].start()
            for cp in first[t]:
                cp.start()
        for t in range(n):
            for j, chip in enumerate(chips):
                copy(t, 1 + j, (*chip, c_), me).wait_recv()
                passed[t][j].start()
        for t in range(n):
            copy(t, 0, sibling, me).wait_recv()
            for j, chip in enumerate(chips):
                copy(t, 4 + j, (*chip, 1 - c_), me).wait_recv()
            for cp in first[t] + passed[t]:
                cp.wait_send()
            mine[t].wait()

    return _pcall(
        body, name=name, in_specs=[_HBM] * n, out_specs=[_HBM] * n,
        out_shape=[jax.ShapeDtypeStruct((N_DEV,) + x.shape, x.dtype) for x in xs],
        scratch_shapes=[pltpu.SemaphoreType.DMA((7 * n,)), pltpu.SemaphoreType.DMA((7 * n,)), pltpu.SemaphoreType.DMA((n,))],
    )(*xs)


_SEM = pl.BlockSpec(memory_space=pltpu.SEMAPHORE)
_EFFECT = pltpu.SideEffectType.DATAFLOW_SIDE_EFFECTING


def _push_start(srcs, after, name):
    n = len(srcs)
    lands = [lax.empty((N_DEV,) + s.shape[-2:], s.dtype) for s in srcs]

    def body(*refs):
        src_refs, land_refs = refs[:n], refs[n:2 * n]
        send_sems, recv_sems, token = refs[2 * n + 1:3 * n + 1], refs[3 * n + 1:4 * n + 1], refs[-1]
        me = _mesh_pos()
        for t in range(n):
            for rel in _RELATIONS:
                peer = _peer(me, rel)
                pltpu.make_async_remote_copy(
                    src_ref=src_refs[t].at[_block_index(peer)] if srcs[t].ndim == 3 else src_refs[t], dst_ref=land_refs[t].at[_block_index(me)],
                    send_sem=send_sems[t], recv_sem=recv_sems[t], device_id=peer, device_id_type=pl.DeviceIdType.MESH).start()
        token[...] = jnp.zeros_like(token)

    outs = _pcall(
        body, name=name,
        out_shape=[pltpu.SemaphoreType.DMA(())] * (2 * n) + [pltpu.HBM(s.shape, s.dtype) for s in srcs]
        + [pltpu.HBM(l.shape, l.dtype) for l in lands] + [jax.ShapeDtypeStruct((8, 128), F32)],
        in_specs=[_HBM] * (2 * n) + [pl.BlockSpec(memory_space=pl.ANY)],
        out_specs=[_SEM] * (2 * n) + [_HBM] * (2 * n) + [pl.BlockSpec(memory_space=pltpu.VMEM)],
        input_output_aliases={i: 2 * n + i for i in range(2 * n)}, compiler_params=pltpu.CompilerParams(has_side_effects=_EFFECT),
    )(*[pltpu.with_memory_space_constraint(t, pltpu.HBM) for t in list(srcs) + lands], after)
    return (outs[:n], outs[n:2 * n], outs[2 * n:3 * n], outs[3 * n:4 * n]), outs[-1]


def _push_wait(handle, after, owns, me, name):
    send_sems, recv_sems, src_thrus, land_thrus = handle
    n = len(land_thrus)

    def body(*refs):
        land_refs, sends, recvs = refs[n:2 * n], refs[2 * n:3 * n], refs[3 * n:4 * n]
        for t in range(n):
            seven = land_refs[t].at[pl.ds(0, N_DEV - 1)]
            all_seven = pltpu.make_async_remote_copy(src_ref=seven, dst_ref=seven, send_sem=sends[t], recv_sem=recvs[t],
                                                     device_id=_mesh_pos(), device_id_type=pl.DeviceIdType.MESH)
            all_seven.wait_send()
            all_seven.wait_recv()

    outs = _pcall(
        body, name=name,
        out_shape=[pltpu.HBM(t.shape, t.dtype) for t in list(src_thrus) + list(land_thrus)],
        in_specs=[_HBM] * (2 * n) + [_SEM] * (2 * n) + [pl.BlockSpec(memory_space=pl.ANY)], out_specs=[_HBM] * (2 * n),
        input_output_aliases={i: i for i in range(2 * n)}, compiler_params=pltpu.CompilerParams(has_side_effects=_EFFECT),
    )(*src_thrus, *land_thrus, *send_sems, *recv_sems, after)
    return [lax.dynamic_update_slice(land, own[None], (me, 0, 0)) for land, own in zip(outs[n:], owns)]


def _sum_blocks(p, name):
    n, rows, cols = p.shape
    tr = _pick(rows, (256, 128, 64, 32, 16, 8))

    def body(p_ref, o_ref):
        acc = p_ref[0].astype(F32)
        for s in range(1, n):
            acc = acc + p_ref[s].astype(F32)
        o_ref[...] = acc

    return _pcall(
        body, name=name, grid=(rows // tr,),
        in_specs=[pl.BlockSpec((n, tr, cols), lambda i: (0, i, 0))], out_specs=pl.BlockSpec((tr, cols), lambda i: (i, 0)),
        out_shape=jax.ShapeDtypeStruct((rows, cols), F32), compiler_params=_params(("parallel",)),
    )(p)


def _sum_rows(t, name):
    def body(t_ref, o_ref):
        o_ref[...] = jnp.sum(t_ref[...], axis=0, keepdims=True)

    return _pcall(body, name=name, out_shape=jax.ShapeDtypeStruct((1, t.shape[1]), F32))(t)


class _Packing:
    def __init__(self, sizes, align):
        self.offsets, self.sizes, self.align = {}, dict(sizes), align
        row = 0
        for name, size in sizes:
            self.offsets[name] = row
            row += -(-size // (align * PACK_COLS)) * align
        self.rows = row

    def pack(self, pieces):
        return self.pack_blocks({n: pieces[n].reshape(1, -1) for n in self.sizes})[0]

    def pack_blocks(self, pieces):
        out = []
        for n, size in self.sizes.items():
            padded = -(-size // (self.align * PACK_COLS)) * self.align * PACK_COLS
            out.append(jnp.pad(pieces[n], ((0, 0), (0, padded - size))).reshape(pieces[n].shape[0], -1, PACK_COLS))
        return jnp.concatenate(out, axis=1)

    def piece(self, packed, name, lead=()):
        start, size = self.offsets[name], self.sizes[name]
        nrow = -(-size // (self.align * PACK_COLS)) * self.align
        sl = packed[..., start:start + nrow, :]
        return sl.reshape(lead + (nrow * PACK_COLS,))[..., :size]


_PIECES = (("ev_w_in", "ev_w_in", 0, 1, "a"), ("w_out0", "w_out", 0, 0, "a"),
           ("mlp_w1_0", "mlp_w1", 0, 1, "b"), ("mlp_w2_0", "mlp_w2", 0, 0, "b"),
           ("od_w_in", "od_w_in", 0, 1, "c"), ("od_w_uq", "od_w_uq", 0, 1, "c"), ("od_w_ukv", "od_w_ukv", 0, 1, "c"),
           ("w_out1", "w_out", 1, 0, "c"), ("mlp_w1_1", "mlp_w1", 1, 1, "c"), ("mlp_w2_1", "mlp_w2", 1, 0, "c"))
_SMALL_SHARDED = (("od_q_norm_g", 1), ("od_conv_w", 2), ("od_conv_b", 1), ("od_ln_g", 1), ("od_ln_b", 1))
_REPLICATED = ("c_ctx", "norm1_g", "norm2_g", "ev_q_norm_g", "ev_k_norm_g", "ev_sgu_norm_g", "ev_sgu_w", "ev_sgu_b",
               "od_kv_norm_g", "final_g")


def _unshard(blocks, axis):
    moved = jnp.moveaxis(blocks, 0, axis)
    shape = moved.shape
    return moved.reshape(shape[:axis] + (shape[axis] * shape[axis + 1],) + shape[axis + 2:])


def _group_mean_matrix(width, group):
    idx = jnp.arange(width) // group
    return (idx[:, None] == idx[None, :]).astype(F32) / group


def _angles(length, d_rot):
    rows = length // GRID_W
    row = jnp.broadcast_to(jnp.arange(rows)[:, None], (rows, GRID_W)).reshape(-1).astype(F32)
    col = jnp.broadcast_to(jnp.arange(GRID_W)[None, :], (rows, GRID_W)).reshape(-1).astype(F32)
    d_axis = d_rot // 2
    inv = ROPE_THETA ** (-jnp.arange(0, d_axis, 2, dtype=F32) / d_axis)
    return jnp.concatenate([row[:, None] * inv, col[:, None] * inv], axis=-1)


def _rope_tables(length, d_rot, head_dim, heads, ctx_len, tail=0):
    ang = _angles(length, d_rot)
    cos = jnp.repeat(jnp.cos(ang), 2, axis=1)
    sin = jnp.repeat(jnp.sin(ang), 2, axis=1) * jnp.tile(jnp.array([-1.0, 1.0], F32), d_rot // 2)
    keep = head_dim - d_rot - tail
    cos = jnp.concatenate([jnp.ones((length, keep), F32), cos, jnp.ones((length, tail), F32)], axis=1)
    sin = jnp.concatenate([jnp.zeros((length, keep), F32), sin, jnp.zeros((length, tail), F32)], axis=1)
    cos, sin = jnp.tile(cos, (1, heads)), jnp.tile(sin, (1, heads))
    cos = jnp.concatenate([jnp.ones((ctx_len, cos.shape[1]), F32), cos], axis=0)
    sin = jnp.concatenate([jnp.zeros((ctx_len, sin.shape[1]), F32), sin], axis=0)
    return cos, sin


def _segment_params(mod, bsz):
    parts = jnp.split(mod, N_MOD, axis=-1)
    out = []
    for part in parts:
        lat = part[:bsz]
        ctx = jnp.broadcast_to(part[bsz:bsz + 1], lat.shape)
        out.append(jnp.stack([ctx, lat], axis=1)[:, :, None, :])
    return out


def _flat(t):
    return t.reshape(-1, t.shape[-1])


def _sequence_rowwise(ctx_len):
    tl = math.gcd(256, ctx_len)

    def make(name, f, out_specs, rows_per_block=tl, ctx_blocks=ctx_len // tl):
        return _rowwise(name, f, out_specs, rows_per_block, ctx_blocks)

    return make


def _mixer0(xall, modrows0, w, bsz, length, ctx_len):
    d = xall.shape[-1]
    total = ctx_len + length
    rowwise, flat = _sequence_rowwise(ctx_len), _flat
    sh1, sc1, g1, sh2, sc2, _ = _segment_params(modrows0, bsz)
    (h,) = rowwise("mod0", _f_modulate, [(d, BF16)])((xall,), (), (sh1, sc1), (w["norm1_g0"],), ())
    ev_q, ev_kv = A_Q_HEADS * A_HEAD_DIM, A_KV_HEADS * A_HEAD_DIM
    half = B_GROUPS * B_GROUP_DIM
    groups = tuple(jnp.split(w["ev_w_in"], [ev_q, ev_q + ev_kv, ev_q + 2 * ev_kv, ev_q + 2 * ev_kv + half], axis=0))
    qp, kp, vp, zu, zv = [t.reshape(bsz, total, -1) for t in _linear_multi("ev_in", (F32, F32, BF16, F32, F32))(flat(h), groups)]
    cos_q, sin_q = _rope_tables(length, A_HEAD_DIM, A_HEAD_DIM, A_Q_HEADS, ctx_len)
    cos_k, sin_k = cos_q[:, :ev_kv], sin_q[:, :ev_kv]
    (q,) = rowwise("ev_q", _f_headnorm_rope, [(ev_q, BF16)])(
        (qp,), (cos_q, sin_q), (), (jnp.tile(w["ev_q_norm_g"][0], A_Q_HEADS)[None],), (_group_mean_matrix(ev_q, A_HEAD_DIM),))
    (k,) = rowwise("ev_k", _f_headnorm_rope, [(ev_kv, BF16)])(
        (kp,), (cos_k, sin_k), (), (jnp.tile(w["ev_k_norm_g"][0], A_KV_HEADS)[None],), (_group_mean_matrix(ev_kv, A_HEAD_DIM),))
    o_att = _attention_rows("gqa", ctx_len, A_Q_HEADS, A_KV_HEADS)(q, k, vp)
    u, vn = rowwise("sgu_pre", _f_sgu_pre, [(half, F32), (half, BF16)])(
        (zu, zv), (), (), (w["ev_sgu_norm_g"][0].reshape(1, half),), (_group_mean_matrix(half, B_GROUP_DIM),))
    bias = jnp.repeat(w["ev_sgu_b"][0].T, B_GROUP_DIM, axis=1)
    (o_sgu,) = rowwise("sgu_mix", _f_sgu_mix, [(half, BF16)], rows_per_block=B_CHUNK, ctx_blocks=0)(
        (u, vn), (), (), tuple(w["ev_sgu_w"][0][g] for g in range(B_GROUPS)) + (bias,), ())
    y = _linear_sum("out0")((flat(o_att), flat(o_sgu)), tuple(jnp.split(w["w_out0"], 2, axis=0))).reshape(bsz, total, d)
    x1, h = rowwise("res_mod0a", _f_res_modulate, [(d, F32), (d, BF16)])((xall, y), (), (g1, sh2, sc2), (w["norm2_g0"],), ())
    return x1, h


def _mlp0(x1, h, modrows0, modrows1, w, bsz, length, ctx_len):
    d = x1.shape[-1]
    total = ctx_len + length
    g2 = _segment_params(modrows0, bsz)[5]
    sh1, sc1 = _segment_params(modrows1, bsz)[:2]
    y = _mlp("mlp0")(_flat(h), w["mlp_w1_0"], w["mlp_w2_0"]).reshape(bsz, total, d)
    return _sequence_rowwise(ctx_len)("res_mod0b", _f_res_modulate, [(d, F32), (d, BF16)])((x1, y), (), (g2, sh1, sc1), (w["norm1_g1"],), ())


def _layer1(x2, h, modrows1, w, bsz, length, ctx_len):
    d = x2.shape[-1]
    total = ctx_len + length
    half = B_GROUPS * B_GROUP_DIM
    rowwise, flat = _sequence_rowwise(ctx_len), _flat
    _, _, g1n, sh2n, sc2n, g2n = _segment_params(modrows1, bsz)
    g_cq, g_ckv, g_kr, g_za, g_zg = jnp.split(w["od_w_in"], [C_Q_RANK, C_Q_RANK + C_KV_RANK, C_Q_RANK + C_KV_RANK + C_ROPE,
                                                             C_Q_RANK + C_KV_RANK + C_ROPE + half], axis=0)
    lanes, c_qk = 128, C_NOPE + C_ROPE
    g_kr = jnp.pad(g_kr, ((C_NOPE, lanes - c_qk), (0, 0)))
    cq, ckv, kr, za, zg = [t.reshape(bsz, total, -1) for t in _linear_multi("od_in", (F32,) * 5)(flat(h), (g_cq, g_ckv, g_kr, g_za, g_zg))]
    lat = slice(ctx_len, total)
    lat_tl = math.gcd(256, length)
    (cqn,) = _rowwise("od_qn", _f_rms, [(C_Q_RANK, BF16)], lat_tl)((cq[:, lat],), (), (), (w["od_q_norm_g"],), ())
    w_uq = jnp.pad(w["od_w_uq"].reshape(C_HEADS, c_qk, C_Q_RANK), ((0, 0), (0, lanes - c_qk), (0, 0))).reshape(C_HEADS * lanes, C_Q_RANK)
    qf = _linear("od_uq")(flat(cqn), w_uq).reshape(bsz, length, C_HEADS * lanes)
    cos_q, sin_q = _rope_tables(length, C_ROPE, lanes, C_HEADS, 0, tail=lanes - c_qk)
    (q,) = _rowwise("od_qrope", _f_rope, [(C_HEADS * lanes, BF16)], lat_tl)((qf,), (cos_q, sin_q), (), (), ())
    (ckvn,) = rowwise("od_kvn", _f_rms, [(C_KV_RANK, BF16)])((ckv,), (), (), (w["od_kv_norm_g"],), ())
    per_head = w["od_w_ukv"].reshape(C_HEADS, C_NOPE + C_V, C_KV_RANK)
    w_kn = jnp.pad(per_head[:, :C_NOPE], ((0, 0), (0, lanes - C_NOPE), (0, 0))).reshape(C_HEADS * lanes, C_KV_RANK)
    w_v = jnp.pad(per_head[:, C_NOPE:], ((0, 0), (0, lanes - C_V), (0, 0))).reshape(C_HEADS * lanes, C_KV_RANK)
    kn, vv = [t.reshape(bsz, total, -1) for t in _linear_multi("od_ukv", (BF16, BF16))(flat(ckvn), (w_kn, w_v))]
    cos_r, sin_r = _rope_tables(length, C_ROPE, lanes, 1, ctx_len, tail=lanes - c_qk)
    (krr,) = rowwise("od_krope", _f_rope, [(lanes, BF16)])((kr,), (cos_r, sin_r), (), (), ())
    o_att = _attention_latent_rows("mla", C_HEADS, C_NOPE, C_ROPE, C_V, 4, 2)(q, kn, krr, vv)
    (glu,) = _rowwise("glu", _f_glu, [(half, F32)], lat_tl)((za[:, lat], zg[:, lat]), (), (), (), ())
    taps = jnp.concatenate([w["od_conv_w"][0], w["od_conv_b"]], axis=0)
    conv = _dwconv(glu, taps)
    (o_conv,) = _rowwise("ln_silu", _f_ln_silu, [(half, BF16)], lat_tl)((conv,), (), (), (w["od_ln_g"], w["od_ln_b"]), ())
    y = _linear_sum("out1")((flat(o_att), flat(o_conv)), tuple(jnp.split(w["w_out1"], 2, axis=0))).reshape(bsz, length, d)
    lat_param = lambda p: p[:, 1:]
    x3, h = _rowwise("res_mod1a", _f_res_modulate, [(d, F32), (d, BF16)], lat_tl)(
        (x2[:, lat], y), (), (lat_param(g1n), lat_param(sh2n), lat_param(sc2n)), (w["norm2_g1"],), ())
    y = _mlp("mlp1")(flat(h), w["mlp_w1_1"], w["mlp_w2_1"]).reshape(bsz, length, d)
    (x4,) = _rowwise("res1b", _f_res, [(d, F32)], lat_tl)((x3, y), (), (lat_param(g2n),), (), ())
    return x4


def kernel(x, c, ctx, c_ctx, ada_w, ada_b, norm1_g, norm2_g, w_out, mlp_w1, mlp_w2, ev_w_in, ev_q_norm_g, ev_k_norm_g, ev_sgu_norm_g, ev_sgu_w, ev_sgu_b, od_w_in, od_q_norm_g, od_kv_norm_g, od_w_uq, od_w_ukv, od_conv_w, od_conv_b, od_ln_g, od_ln_b, final_g, loss_target, m_c_ctx, m_ada_w, m_ada_b, m_norm1_g, m_norm2_g, m_w_out, m_mlp_w1, m_mlp_w2, m_ev_w_in, m_ev_q_norm_g, m_ev_k_norm_g, m_ev_sgu_norm_g, m_ev_sgu_w, m_ev_sgu_b, m_od_w_in, m_od_q_norm_g, m_od_kv_norm_g, m_od_w_uq, m_od_w_ukv, m_od_conv_w, m_od_conv_b, m_od_ln_g, m_od_ln_b, m_final_g, v_c_ctx, v_ada_w, v_ada_b, v_norm1_g, v_norm2_g, v_w_out, v_mlp_w1, v_mlp_w2, v_ev_w_in, v_ev_q_norm_g, v_ev_k_norm_g, v_ev_sgu_norm_g, v_ev_sgu_w, v_ev_sgu_b, v_od_w_in, v_od_q_norm_g, v_od_kv_norm_g, v_od_w_uq, v_od_w_ukv, v_od_conv_w, v_od_conv_b, v_od_ln_g, v_od_ln_b, v_final_g):
    names = ["c_ctx", "ada_w", "ada_b", "norm1_g", "norm2_g", "w_out", "mlp_w1", "mlp_w2", "ev_w_in", "ev_q_norm_g", "ev_k_norm_g",
             "ev_sgu_norm_g", "ev_sgu_w", "ev_sgu_b", "od_w_in", "od_q_norm_g", "od_kv_norm_g", "od_w_uq", "od_w_ukv", "od_conv_w",
             "od_conv_b", "od_ln_g", "od_ln_b", "final_g"]
    local = dict(zip(names, [c_ctx, ada_w, ada_b, norm1_g, norm2_g, w_out, mlp_w1, mlp_w2, ev_w_in, ev_q_norm_g, ev_k_norm_g, ev_sgu_norm_g, ev_sgu_w, ev_sgu_b, od_w_in, od_q_norm_g, od_kv_norm_g, od_w_uq, od_w_ukv, od_conv_w, od_conv_b, od_ln_g, od_ln_b, final_g]))
    mom1 = dict(zip(names, [m_c_ctx, m_ada_w, m_ada_b, m_norm1_g, m_norm2_g, m_w_out, m_mlp_w1, m_mlp_w2, m_ev_w_in, m_ev_q_norm_g, m_ev_k_norm_g, m_ev_sgu_norm_g, m_ev_sgu_w, m_ev_sgu_b, m_od_w_in, m_od_q_norm_g, m_od_kv_norm_g, m_od_w_uq, m_od_w_ukv, m_od_conv_w, m_od_conv_b, m_od_ln_g, m_od_ln_b, m_final_g]))
    mom2 = dict(zip(names, [v_c_ctx, v_ada_w, v_ada_b, v_norm1_g, v_norm2_g, v_w_out, v_mlp_w1, v_mlp_w2, v_ev_w_in, v_ev_q_norm_g, v_ev_k_norm_g, v_ev_sgu_norm_g, v_ev_sgu_w, v_ev_sgu_b, v_od_w_in, v_od_q_norm_g, v_od_kv_norm_g, v_od_w_uq, v_od_w_ukv, v_od_conv_w, v_od_conv_b, v_od_ln_g, v_od_ln_b, v_final_g]))
    bsz, length, d = x.shape
    ctx_len = ctx.shape[1]
    me = _block_index(_mesh_pos())

    shard = {p: local[wn][layer] for p, wn, layer, _, _ in _PIECES}
    by_columns = {p: axis == 1 for p, _, _, axis, _ in _PIECES}
    stages = {s: [p for p, _, _, _, st in _PIECES if st == s] for s in "abc"}
    block_rows = {p: shard[p].shape[1] if by_columns[p] else shard[p].shape[0] for p in shard}

    def pad_block_rows(t, p):
        extra = -block_rows[p] % ROW_ALIGN
        return jnp.pad(t, [(0, 0)] * (t.ndim - 2) + [(0, extra), (0, 0)]) if extra else t

    def travelling(p):
        t = shard[p].astype(BF16)
        return pad_block_rows(t.T if by_columns[p] else t, p)

    mine = {s: [travelling(p) for p in stages[s]] for s in "abc"}
    tiny_pack = _Packing([(n, local[n].size) for n, _ in _SMALL_SHARDED], 8)

    def unpack_weights(s, gathered):
        return {p: g[:, :block_rows[p]].reshape(N_DEV * block_rows[p], g.shape[2]) for p, g in zip(stages[s], gathered)}

    def pack_grads(s, g):
        return [pad_block_rows(g[p].reshape(N_DEV, block_rows[p], g[p].shape[1]), p) for p in stages[s]]

    cond_local = jnp.concatenate([c, c_ctx[None], jnp.zeros((COND_ROWS - bsz - 1, d), F32)], axis=0)
    cond, gathered_tiny, *gathered_a = _all_gather(
        [cond_local, tiny_pack.pack({n: local[n] for n, _ in _SMALL_SHARDED})] + mine["a"], "gather_inputs")
    cond = cond.reshape(N_DEV * COND_ROWS, d)
    silu_op = _rowwise("silu", _f_silu, [(d, F32)], N_DEV * COND_ROWS)
    silu_rows, silu_pullback = jax.vjp(lambda r: silu_op((r[None],), (), (), (), ())[0][0], cond)
    mod_cols = ada_w.shape[2]
    mod_part = jnp.concatenate([_mm(silu_rows, ada_w[i], "nn", (F32,), f"ada{i}_fwd")[0] for i in range(2)], axis=0)
    (mod_all,) = _all_gather([mod_part], "gather_mod")
    mod_all = mod_all.reshape(N_DEV, 2, N_DEV * COND_ROWS, mod_cols)
    modrows = []
    for i in range(2):
        whole = mod_all[:, i].transpose(1, 0, 2).reshape(N_DEV * COND_ROWS, N_DEV * mod_cols) + ada_b[i]
        modrows.append(lax.dynamic_slice_in_dim(whole, me * COND_ROWS, COND_ROWS, axis=0)[:bsz + 1])

    weights_a = unpack_weights("a", gathered_a)
    gather_b, token_b = _push_start(mine["b"], mod_all, "gather_weights_b_start")
    gather_c, token_c = _push_start(mine["c"], token_b, "gather_weights_c_start")
    full = {n: local[n] for n in _REPLICATED}
    for n, axis in _SMALL_SHARDED:
        full[n] = _unshard(tiny_pack.piece(gathered_tiny, n, (N_DEV,)).reshape((N_DEV,) + local[n].shape), axis)

    xall = jnp.concatenate([ctx, x], axis=1)
    modrows0, modrows1 = modrows[0] + token_c[0, 0], modrows[1]
    w_a = dict(weights_a, norm1_g0=norm1_g[0][None], norm2_g0=norm2_g[0][None],
               **{n: full[n] for n in ("ev_q_norm_g", "ev_k_norm_g", "ev_sgu_norm_g", "ev_sgu_w", "ev_sgu_b")})
    (x1, h0), pull_a = jax.vjp(lambda x_, m0, w: _mixer0(x_, m0, w, bsz, length, ctx_len), xall, modrows0, w_a)
    w_b = dict(unpack_weights("b", _push_wait(gather_b, x1, mine["b"], me, "gather_weights_b_wait")), norm1_g1=norm1_g[1][None])
    (x2, h1), pull_b = jax.vjp(lambda x_, h_, m0, m1, w: _mlp0(x_, h_, m0, m1, w, bsz, length, ctx_len), x1, h0, modrows0, modrows1, w_b)
    w_c = dict(unpack_weights("c", _push_wait(gather_c, x2, mine["c"], me, "gather_weights_c_wait")), norm2_g1=norm2_g[1][None],
               **{n: full[n] for n in ("od_q_norm_g", "od_kv_norm_g", "od_conv_w", "od_conv_b", "od_ln_g", "od_ln_b")})
    x4, pull_c = jax.vjp(lambda x_, h_, m1, w: _layer1(x_, h_, m1, w, bsz, length, ctx_len), x2, h1, modrows1, w_c)
    loss_part, dx4, dfinal = _loss_head(x4, loss_target, final_g[None])
    loss = lax.psum(loss_part[0, 0], MESH_AXES)

    dx2, dh1, dmod1_c, g_c = pull_c(dx4)
    grads_c = pack_grads("c", g_c)
    exchange_c, token = _push_start(grads_c, dx2, "exchange_grads_c_start")
    dx1, dh0, dmod0_b, dmod1_b, g_b = pull_b((dx2, dh1 + token[0, 0].astype(dh1.dtype)))
    grads_b = pack_grads("b", g_b)
    exchange_b, token = _push_start(grads_b, dx1, "exchange_grads_b_start")
    dxall, dmod0_a, g_a = pull_a((dx1, dh0 + token[0, 0].astype(dh0.dtype)))
    grad_x = dxall[:, ctx_len:]
    dmodrows = [dmod0_a + dmod0_b, dmod1_b + dmod1_c]
    grads = {n: g[n] for g in (g_a, g_c) for n in g if n in full}
    grads["norm1_g"] = jnp.concatenate([g_a["norm1_g0"], g_b["norm1_g1"]], axis=0)
    grads["norm2_g"] = jnp.concatenate([g_a["norm2_g0"], g_c["norm2_g1"]], axis=0)
    grads["final_g"] = dfinal[0]

    dmod_local = jnp.concatenate([jnp.pad(dm, ((0, COND_ROWS - bsz - 1), (0, 0))) for dm in dmodrows], axis=0)
    (dmod_all,) = _all_gather([dmod_local], "gather_dmod")
    dmod_all = dmod_all.reshape(N_DEV, 2, COND_ROWS, N_DEV * mod_cols)
    reduced = {}
    grad_ada_w, grad_ada_b, dmod_mine = [], [], []
    for i in range(2):
        dmod = dmod_all[:, i].reshape(N_DEV * COND_ROWS, N_DEV * mod_cols)
        grad_ada_b.append(_sum_rows(dmod, f"ada{i}_db")[0])
        dmod_mine.append(lax.dynamic_slice_in_dim(dmod, me * mod_cols, mod_cols, axis=1))
        grad_ada_w.append(_mm(silu_rows, dmod_mine[i], "tn", (F32,), f"ada{i}_dw")[0])
    reduced["ada_w"], reduced["ada_b"] = jnp.stack(grad_ada_w), jnp.stack(grad_ada_b)
    dsilu = _mm(jnp.concatenate(dmod_mine, axis=1), jnp.concatenate([ada_w[0], ada_w[1]], axis=1), "nt", (F32,), "ada_dx")[0]
    (dcond,) = silu_pullback(dsilu)
    grads["c_ctx"] = _sum_rows(dcond.reshape(N_DEV, COND_ROWS, d)[:, bsz], "c_ctx_rows")[0]

    small_names = list(_REPLICATED) + [n for n, _ in _SMALL_SHARDED]
    small_pack = _Packing([(n, full[n].size) for n in small_names], 8)
    (small_all,) = _all_gather([small_pack.pack({n: grads[n].astype(F32) for n in small_names})], "gather_small_grads")
    small_sum = _sum_blocks(small_all, "sum_small_grads")
    grads_a = pack_grads("a", g_a)
    exchange_a, token = _push_start(grads_a, small_sum, "exchange_grads_a_start")
    small_sum = small_sum + token[0, 0]
    for n in _REPLICATED:
        reduced[n] = small_pack.piece(small_sum, n).reshape(local[n].shape)
    for n, axis in _SMALL_SHARDED:
        whole = small_pack.piece(small_sum, n).reshape(full[n].shape)
        reduced[n] = lax.dynamic_slice_in_dim(whole, me * local[n].shape[axis], local[n].shape[axis], axis=axis)

    def own_blocks(blocks):
        return [lax.dynamic_index_in_dim(t, me, 0, keepdims=False) for t in blocks]

    piece_grad = {}

    def reduce_stage(s, handle, blocks, after):
        received = _push_wait(handle, after, own_blocks(blocks), me, f"exchange_grads_{s}_wait")
        for p, got in zip(stages[s], received):
            summed = _sum_blocks(got, "sum_grads_" + p)[:block_rows[p]]
            piece_grad[p] = summed.T if by_columns[p] else summed

    def stacked(n):
        return jnp.stack([piece_grad[p] for p, wn, _, _, _ in _PIECES if wn == n])

    reduce_stage("c", exchange_c, grads_c, small_sum)
    reduce_stage("b", exchange_b, grads_b, small_sum)
    reduced["mlp_w1"], reduced["mlp_w2"] = stacked("mlp_w1"), stacked("mlp_w2")
    for n in ("od_w_in", "od_w_uq", "od_w_ukv"):
        reduced[n] = piece_grad[n][None]

    delta, new_m, new_v = {}, {}, {}
    with_stage_a = ("ev_w_in", "w_out")
    for n in [n for n in names if n not in with_stage_a]:
        delta[n], new_m[n], new_v[n] = _adamw(local[n], reduced[n], mom1[n], mom2[n], "adamw_" + n)
    reduce_stage("a", exchange_a, grads_a, new_v["mlp_w2"])
    reduced["w_out"], reduced["ev_w_in"] = stacked("w_out"), piece_grad["ev_w_in"][None]
    for n in with_stage_a:
        delta[n], new_m[n], new_v[n] = _adamw(local[n], reduced[n], mom1[n], mom2[n], "adamw_" + n)
    return (loss, grad_x, *[reduced[n] for n in names], *[delta[n] for n in names], *[new_m[n] for n in names], *[new_v[n] for n in names])
```

```python
import functools
import math

import jax
import jax.numpy as jnp
from jax import lax
from jax.experimental import pallas as pl
from jax.experimental.pallas import tpu as pltpu

F32, BF16 = jnp.float32, jnp.bfloat16

EPS = 1e-6
GRID_W = 64
ROPE_THETA = 10000.0
A_HEAD_DIM, A_Q_HEADS, A_KV_HEADS = 64, 8, 2
B_GROUPS, B_GROUP_DIM, B_CHUNK = 8, 64, 128
C_HEADS, C_NOPE, C_ROPE, C_V, C_Q_RANK, C_KV_RANK = 8, 64, 32, 64, 256, 128
D_CONV = 31
CONV_PAD = D_CONV // 2
N_MOD = 6
N_DEV = 8
MESH_AXES = ("x", "y", "c")

ADAM_LR, ADAM_B1, ADAM_B2, ADAM_EPS, ADAM_WD, ADAM_STEP = 0.001, 0.9, 0.999, 1e-08, 0.01, 10

VMEM_LIMIT = 56 * 1024 * 1024
PACK_COLS = 1024
ROW_ALIGN = 16
SUBLANES = 8
CONV_ROWS = 32
COND_ROWS = 8


def _pcall(body, **kw):
    return pl.pallas_call(body, **kw)


def _params(sem=None):
    return pltpu.CompilerParams(dimension_semantics=sem, vmem_limit_bytes=VMEM_LIMIT)


def _pick(n, cands):
    for c in cands:
        if n % c == 0:
            return c
    return n


def _mm(a, b, mode, out_dtypes, name, epi=None, extras=()):
    if mode == "tn":
        kk, m = a.shape
    else:
        m, kk = a.shape
    n = b.shape[0] if mode == "nt" else b.shape[1]
    tm = _pick(m, (1152, 1024, 896, 768, 512, 256, 128))
    tn = _pick(n, (1024, 896, 768, 512, 256, 128))
    tk = kk if kk <= 1024 else _pick(kk, (2048, 1536, 1024, 896, 768, 512, 256, 128))
    nk = kk // tk
    ne, no = len(extras), len(out_dtypes)
    a_spec = pl.BlockSpec((tk, tm), lambda i, j, k: (k, i)) if mode == "tn" else pl.BlockSpec((tm, tk), lambda i, j, k: (i, k))
    b_spec = pl.BlockSpec((tn, tk), lambda i, j, k: (j, k)) if mode == "nt" else pl.BlockSpec((tk, tn), lambda i, j, k: (k, j))
    t_spec = pl.BlockSpec((tm, tn), lambda i, j, k: (i, j))
    dn = {"nn": ((1,), (0,)), "nt": ((1,), (1,)), "tn": ((0,), (0,))}[mode]

    def body(a_ref, b_ref, *rest):
        extra_refs, out_refs = rest[:ne], rest[ne:ne + no]

        def finish(acc):
            outs = (acc,) if epi is None else epi(acc, *[r[...] for r in extra_refs])
            for r, o in zip(out_refs, outs):
                r[...] = o.astype(r.dtype)

        part = lax.dot_general(a_ref[...].astype(BF16), b_ref[...].astype(BF16), (dn, ((), ())), preferred_element_type=F32)
        if nk == 1:
            finish(part)
        else:
            acc_ref = rest[-1]
            k = pl.program_id(2)

            @pl.when(k == 0)
            def _():
                acc_ref[...] = part

            @pl.when(k > 0)
            def _():
                acc_ref[...] += part

            @pl.when(k == nk - 1)
            def _():
                finish(acc_ref[...])

    outs = _pcall(
        body, name=name, grid=(m // tm, n // tn, nk),
        in_specs=[a_spec, b_spec] + [t_spec] * ne,
        out_specs=[t_spec] * no,
        out_shape=[jax.ShapeDtypeStruct((m, n), d) for d in out_dtypes],
        scratch_shapes=[pltpu.VMEM((tm, tn), F32)] if nk > 1 else [],
        compiler_params=_params(("parallel", "parallel", "arbitrary")),
    )(a, b, *extras)
    return outs


def _linear(name, out_dtype=F32):
    @jax.custom_vjp
    def op(x, wt):
        return _mm(x, wt, "nt", (out_dtype,), name + "_fwd")[0]

    def fwd(x, wt):
        return op(x, wt), (x, wt)

    def bwd(res, dy):
        x, wt = res
        return _mm(dy, wt, "nn", (x.dtype,), name + "_dx")[0], _mm(dy, x, "tn", (wt.dtype,), name + "_dw")[0]

    op.defvjp(fwd, bwd)
    return op


_NT, _NN, _TN = (((1,), (1,)), ((), ())), (((1,), (0,)), ((), ())), (((0,), (0,)), ((), ()))
_ROW_TILES = (1152, 1024, 768, 512, 256, 128)


def _whole(w):
    return pl.BlockSpec(w.shape, lambda i: (0, 0))


def _groups_apply(x, ws, out_dtypes, name):
    n, (m, kk) = len(ws), x.shape
    tm = _pick(m, _ROW_TILES)

    def body(x_ref, *refs):
        a = x_ref[...].astype(BF16)
        for w_ref, o_ref in zip(refs[:n], refs[n:]):
            o_ref[...] = lax.dot_general(a, w_ref[...], _NT, preferred_element_type=F32).astype(o_ref.dtype)

    return _pcall(
        body, name=name, grid=(m // tm,),
        in_specs=[pl.BlockSpec((tm, kk), lambda i: (i, 0))] + [_whole(w) for w in ws],
        out_specs=[pl.BlockSpec((tm, w.shape[0]), lambda i: (i, 0)) for w in ws],
        out_shape=[jax.ShapeDtypeStruct((m, w.shape[0]), dt) for w, dt in zip(ws, out_dtypes)],
        compiler_params=_params(("parallel",)),
    )(x, *ws)


def _groups_sum(xs, ws, out_dtype, name):
    n, m, kk = len(ws), xs[0].shape[0], ws[0].shape[1]
    tm = _pick(m, _ROW_TILES)

    def body(*refs):
        acc = None
        for x_ref, w_ref in zip(refs[:n], refs[n:2 * n]):
            part = lax.dot_general(x_ref[...].astype(BF16), w_ref[...], _NN, preferred_element_type=F32)
            acc = part if acc is None else acc + part
        refs[2 * n][...] = acc.astype(out_dtype)

    return _pcall(
        body, name=name, grid=(m // tm,),
        in_specs=[pl.BlockSpec((tm, w.shape[0]), lambda i: (i, 0)) for w in ws] + [_whole(w) for w in ws],
        out_specs=pl.BlockSpec((tm, kk), lambda i: (i, 0)), out_shape=jax.ShapeDtypeStruct((m, kk), out_dtype),
        compiler_params=_params(("parallel",)),
    )(*xs, *ws)


def _groups_outer(xs, y, ws, name):
    n, (m, kk) = len(ws), y.shape
    tk = _pick(m, (768, 512, 256, 128))
    steps = m // tk

    def body(y_ref, *refs):
        x_refs, o_refs, acc_refs = refs[:n], refs[n:2 * n], refs[2 * n:]
        k = pl.program_id(0)
        b = y_ref[...].astype(BF16)
        for x_ref, o_ref, acc_ref in zip(x_refs, o_refs, acc_refs):
            part = lax.dot_general(x_ref[...].astype(BF16), b, _TN, preferred_element_type=F32)

            @pl.when(k == 0)
            def _(acc_ref=acc_ref, part=part):
                acc_ref[...] = part

            @pl.when(k > 0)
            def _(acc_ref=acc_ref, part=part):
                acc_ref[...] += part

            @pl.when(k == steps - 1)
            def _(acc_ref=acc_ref, o_ref=o_ref):
                o_ref[...] = acc_ref[...].astype(o_ref.dtype)

    return _pcall(
        body, name=name, grid=(steps,),
        in_specs=[pl.BlockSpec((tk, kk), lambda k: (k, 0))] + [pl.BlockSpec((tk, w.shape[0]), lambda k: (k, 0)) for w in ws],
        out_specs=[_whole(w) for w in ws], out_shape=[jax.ShapeDtypeStruct(w.shape, w.dtype) for w in ws],
        scratch_shapes=[pltpu.VMEM(w.shape, F32) for w in ws], compiler_params=_params(("arbitrary",)),
    )(y, *xs)


def _linear_multi(name, out_dtypes):
    @jax.custom_vjp
    def op(x, wts):
        return tuple(_groups_apply(x, wts, out_dtypes, name + "_fwd"))

    def fwd(x, wts):
        return op(x, wts), (x, wts)

    def bwd(res, dys):
        x, wts = res
        return _groups_sum(dys, wts, x.dtype, name + "_dx"), tuple(_groups_outer(dys, x, wts, name + "_dw"))

    op.defvjp(fwd, bwd)
    return op


def _linear_sum(name):
    @jax.custom_vjp
    def op(xs, ws):
        return _groups_sum(xs, ws, F32, name + "_fwd")

    def fwd(xs, ws):
        return op(xs, ws), (xs, ws)

    def bwd(res, dy):
        xs, ws = res
        return tuple(_groups_apply(dy, ws, [x.dtype for x in xs], name + "_dx")), tuple(_groups_outer(xs, dy, ws, name + "_dw"))

    op.defvjp(fwd, bwd)
    return op


def _relu2_epi(acc):
    return jnp.square(jnp.maximum(acc, 0.0)), acc


def _relu2_bwd_epi(acc, a):
    return (acc * (2.0 * jnp.maximum(a.astype(F32), 0.0)),)


def _mlp(name):
    @jax.custom_vjp
    def op(h, w1t, w2):
        s, _ = _mm(h, w1t, "nt", (BF16, BF16), name + "_up", epi=_relu2_epi)
        return _mm(s, w2, "nn", (F32,), name + "_down")[0]

    def fwd(h, w1t, w2):
        s, a = _mm(h, w1t, "nt", (BF16, BF16), name + "_up", epi=_relu2_epi)
        return _mm(s, w2, "nn", (F32,), name + "_down")[0], (h, w1t, w2, s, a)

    def bwd(res, dy):
        h, w1t, w2, s, a = res
        da = _mm(dy, w2, "nt", (BF16,), name + "_ds", epi=_relu2_bwd_epi, extras=(a,))[0]
        dw2 = _mm(s, dy, "tn", (w2.dtype,), name + "_dw2")[0]
        dw1t = _mm(da, h, "tn", (w1t.dtype,), name + "_dw1")[0]
        dh = _mm(da, w1t, "nn", (h.dtype,), name + "_dh")[0]
        return dh, dw1t, dw2

    op.defvjp(fwd, bwd)
    return op


def _two_pass_dot(x, m):
    hi = x.astype(BF16)
    lo = (x - hi.astype(F32)).astype(BF16)
    mb = m.astype(BF16)
    return jnp.dot(hi, mb, preferred_element_type=F32) + jnp.dot(lo, mb, preferred_element_type=F32)


@jax.custom_vjp
def _sym_dot(x, m):
    return _two_pass_dot(x, m)


def _sym_dot_fwd(x, m):
    return _two_pass_dot(x, m), m


def _sym_dot_bwd(m, g):
    return _two_pass_dot(g, m), jnp.zeros_like(m)


_sym_dot.defvjp(_sym_dot_fwd, _sym_dot_bwd)


def _neighbour(x):
    lane = lax.broadcasted_iota(jnp.int32, x.shape, 1)
    return jnp.where(lane % 2 == 0, pltpu.roll(x, x.shape[1] - 1, 1), pltpu.roll(x, 1, 1))


@jax.custom_vjp
def _swap_pairs(x):
    return _neighbour(x)


_swap_pairs.defvjp(lambda x: (_neighbour(x), None), lambda _, g: (_neighbour(g),))


def _lane_group(x, k):
    return x[:, k * B_GROUP_DIM:(k + 1) * B_GROUP_DIM]


@jax.custom_vjp
def _group_mix(ws, v):
    return jnp.concatenate([jnp.dot(w.astype(BF16), _lane_group(v, k).astype(BF16), preferred_element_type=F32)
                            for k, w in enumerate(ws)], axis=1)


def _group_mix_fwd(ws, v):
    return _group_mix(ws, v), (ws, v)


def _group_mix_bwd(res, d):
    ws, v = res
    parts = [(_lane_group(d, k).astype(BF16), _lane_group(v, k).astype(BF16)) for k in range(len(ws))]
    dws = tuple(lax.dot_general(dk, vk, (((1,), (1,)), ((), ())), preferred_element_type=F32) for dk, vk in parts)
    dv = jnp.concatenate([lax.dot_general(w.astype(BF16), dk, (((0,), (0,)), ((), ())), preferred_element_type=F32)
                          for w, (dk, _) in zip(ws, parts)], axis=1)
    return dws, dv


_group_mix.defvjp(_group_mix_fwd, _group_mix_bwd)


def _rowwise(name, f, out_specs, tl, ctx_blocks=0):
    def seg(l, s):
        return jnp.where(l >= ctx_blocks, s - 1, 0) if s > 1 else 0

    def specs(rows, tabs, pers, glbs, consts):
        row_specs = [pl.BlockSpec((1, tl, r.shape[2]), lambda b, l: (b, l, 0)) for r in rows]
        tab_specs = [pl.BlockSpec((tl, t.shape[1]), lambda b, l: (l, 0)) for t in tabs]
        per_specs = [pl.BlockSpec((1, 1, 1, p.shape[3]), functools.partial(lambda b, l, s: (b, seg(l, s), 0, 0), s=p.shape[1])) for p in pers]
        glb_specs = [pl.BlockSpec(g.shape, functools.partial(lambda b, l, nd: (0,) * nd, nd=g.ndim)) for g in glbs]
        const_specs = [pl.BlockSpec(c.shape, functools.partial(lambda b, l, nd: (0,) * nd, nd=c.ndim)) for c in consts]
        return row_specs, tab_specs, per_specs, glb_specs, const_specs

    def load(refs_rows, refs_tabs, refs_pers, refs_glbs, refs_consts):
        return (tuple(r[0].astype(F32) for r in refs_rows), tuple(t[...] for t in refs_tabs),
                tuple(p[0, 0].astype(F32) for p in refs_pers), tuple(g[...].astype(F32) for g in refs_glbs),
                tuple(c[...] for c in refs_consts))

    def call_fwd(rows, tabs, pers, glbs, consts):
        bsz, length = rows[0].shape[:2]
        nr, nt, npp, ng, nc = len(rows), len(tabs), len(pers), len(glbs), len(consts)
        rs, ts, ps, gs, cs = specs(rows, tabs, pers, glbs, consts)

        def body(*refs):
            ins, outs = refs[:nr + nt + npp + ng + nc], refs[nr + nt + npp + ng + nc:]
            r, t, p, g, c = load(ins[:nr], ins[nr:nr + nt], ins[nr + nt:nr + nt + npp], ins[nr + nt + npp:nr + nt + npp + ng], ins[nr + nt + npp + ng:])
            for o_ref, o in zip(outs, f(r, t, p, g, c)):
                o_ref[0] = o.astype(o_ref.dtype)

        return _pcall(
            body, name=name + "_fwd", grid=(bsz, length // tl),
            in_specs=rs + ts + ps + gs + cs,
            out_specs=[pl.BlockSpec((1, tl, w), lambda b, l: (b, l, 0)) for w, _ in out_specs],
            out_shape=[jax.ShapeDtypeStruct((bsz, length, w), d) for w, d in out_specs],
            compiler_params=_params(("parallel", "parallel")),
        )(*rows, *tabs, *pers, *glbs, *consts)

    def call_bwd(rows, tabs, pers, glbs, consts, cts):
        bsz, length = rows[0].shape[:2]
        nr, nt, npp, ng, nc, no = len(rows), len(tabs), len(pers), len(glbs), len(consts), len(cts)
        rs, ts, ps, gs, cs = specs(rows, tabs, pers, glbs, consts)
        n_in = nr + nt + npp + ng + nc

        def body(*refs):
            ins, ct_refs, outs = refs[:n_in], refs[n_in:n_in + no], refs[n_in + no:]
            r, t, p, g, c = load(ins[:nr], ins[nr:nr + nt], ins[nr + nt:nr + nt + npp], ins[nr + nt + npp:nr + nt + npp + ng], ins[nr + nt + npp + ng:])
            _, vjp = jax.vjp(lambda r_, p_, g_: tuple(f(r_, t, p_, g_, c)), r, p, g)
            dr, dp, dg = vjp(tuple(ct[0].astype(F32) for ct in ct_refs))
            dr_refs, dp_refs, dg_refs = outs[:nr], outs[nr:nr + npp], outs[nr + npp:]
            for ref, d in zip(dr_refs, dr):
                ref[0] = d.astype(ref.dtype)
            b, l = pl.program_id(0), pl.program_id(1)
            first_of_segment = (l == 0) | (l == ctx_blocks)
            for ref, d in zip(dp_refs, dp):
                @pl.when(first_of_segment)
                def _(ref=ref, d=d):
                    ref[0, 0] = d

                @pl.when(jnp.logical_not(first_of_segment))
                def _(ref=ref, d=d):
                    ref[0, 0] += d
            first = (b == 0) & (l == 0)
            for ref, d in zip(dg_refs, dg):
                @pl.when(first)
                def _(ref=ref, d=d):
                    ref[...] = d

                @pl.when(jnp.logical_not(first))
                def _(ref=ref, d=d):
                    ref[...] += d

        ct_specs = [pl.BlockSpec((1, tl, w), lambda b, l: (b, l, 0)) for w, _ in out_specs]
        outs = _pcall(
            body, name=name + "_bwd", grid=(bsz, length // tl),
            in_specs=rs + ts + ps + gs + cs + ct_specs,
            out_specs=rs + ps + gs,
            out_shape=[jax.ShapeDtypeStruct(r.shape, r.dtype) for r in rows]
            + [jax.ShapeDtypeStruct(p.shape, F32) for p in pers] + [jax.ShapeDtypeStruct(g.shape, F32) for g in glbs],
            compiler_params=_params(("arbitrary", "arbitrary")),
        )(*rows, *tabs, *pers, *glbs, *consts, *cts)
        return tuple(outs[:nr]), tuple(outs[nr:nr + npp]), tuple(outs[nr + npp:])

    @jax.custom_vjp
    def op(rows, tabs, pers, glbs, consts):
        return tuple(call_fwd(rows, tabs, pers, glbs, consts))

    def fwd(rows, tabs, pers, glbs, consts):
        return op(rows, tabs, pers, glbs, consts), (rows, tabs, pers, glbs, consts)

    def bwd(res, cts):
        rows, tabs, pers, glbs, consts = res
        dr, dp, dg = call_bwd(rows, tabs, pers, glbs, consts, tuple(cts))
        dp = tuple(d.astype(p.dtype) for d, p in zip(dp, pers))
        dg = tuple(d.astype(g.dtype) for d, g in zip(dg, glbs))
        return dr, tuple(jnp.zeros_like(t) for t in tabs), dp, dg, tuple(jnp.zeros_like(c) for c in consts)

    op.defvjp(fwd, bwd)
    return op


def _rms(x, g):
    return x * lax.rsqrt(jnp.mean(x * x, axis=-1, keepdims=True) + EPS) * g


def _f_silu(r, t, p, g, c):
    return (jax.nn.silu(r[0]),)


def _f_modulate(r, t, p, g, c):
    shift, scale = p
    return (_rms(r[0], g[0]) * (1.0 + scale) + shift,)


def _f_res_modulate(r, t, p, g, c):
    x, y = r
    gate, shift, scale = p
    xn = x + gate * y
    return xn, _rms(xn, g[0]) * (1.0 + scale) + shift


def _f_res(r, t, p, g, c):
    return (r[0] + p[0] * r[1],)


def _f_headnorm_rope(r, t, p, g, c):
    x = r[0]
    cos, sin = t
    xn = x * lax.rsqrt(_sym_dot(x * x, c[0]) + EPS) * g[0]
    return (xn * cos + _swap_pairs(xn) * sin,)


def _f_rope(r, t, p, g, c):
    x = r[0]
    cos, sin = t
    return (x * cos + _swap_pairs(x) * sin,)


def _f_rms(r, t, p, g, c):
    return (_rms(r[0], g[0]),)


def _f_sgu_pre(r, t, p, g, c):
    u = jax.nn.gelu(r[0])
    v = jax.nn.gelu(r[1])
    vn = v * lax.rsqrt(_sym_dot(v * v, c[0]) + EPS) * g[0]
    return u, vn


def _f_sgu_mix(r, t, p, g, c):
    u, vn = r
    return (u * (g[B_GROUPS] + _group_mix(tuple(g[:B_GROUPS]), vn)),)


def _f_glu(r, t, p, g, c):
    return (r[0] * jax.nn.sigmoid(r[1]),)


def _f_ln_silu(r, t, p, g, c):
    x = r[0]
    mu = jnp.mean(x, axis=-1, keepdims=True)
    var = jnp.mean(jnp.square(x - mu), axis=-1, keepdims=True)
    return (jax.nn.silu((x - mu) * lax.rsqrt(var + EPS) * g[0] + g[1]),)


LOG2_E = 1.4426950408889634
LN_2 = 0.6931471805599453


def _attention_rows(name, ctx_len, heads, kv_heads):
    grp = heads // kv_heads

    def by_segment(qi, cb, lk, run):
        if cb > 0:
            @pl.when(qi < cb)
            def _():
                run(ctx_len)

            @pl.when(qi >= cb)
            def _():
                run(lk)
        else:
            run(lk)

    def head(ref, h, hd):
        return ref[0, :, h * hd:(h + 1) * hd]

    def scores(q_tile, kk, hd):
        qs = (q_tile.astype(F32) * (LOG2_E * hd ** -0.5)).astype(BF16)
        return qs, lax.dot_general(qs, kk, _NT, preferred_element_type=F32)

    def shapes(q):
        bsz, length, width = q.shape
        tq = math.gcd(_pick(length, (256, 128)), ctx_len) if ctx_len else _pick(length, (256, 128))
        return bsz, length, width // heads, tq, ctx_len // tq

    def call_fwd(q, k, v):
        bsz, length, hd, tq, cb = shapes(q)

        def body(q_ref, k_ref, v_ref, o_ref, lse_ref):
            def run(nk):
                k_all, v_all = k_ref[0, :nk], v_ref[0, :nk]
                outs = []
                for j in range(kv_heads):
                    kk = k_all[:, j * hd:(j + 1) * hd]
                    v_ones = jnp.concatenate([v_all[:, j * hd:(j + 1) * hd], jnp.ones((nk, hd), BF16)], axis=1)
                    for h in range(j * grp, (j + 1) * grp):
                        _, s = scores(head(q_ref, h, hd), kk, hd)
                        m = jnp.max(s, axis=-1, keepdims=True)
                        acc = jnp.dot(jnp.exp2(s - m).astype(BF16), v_ones, preferred_element_type=F32)
                        l = acc[:, hd:hd + 1]
                        outs.append((acc[:, :hd] / l).astype(BF16))
                        lse_ref[0, h] = m + jnp.log2(l)
                o_ref[0] = jnp.concatenate(outs, axis=1)

            by_segment(pl.program_id(1), cb, length, run)

        q_spec = pl.BlockSpec((1, tq, heads * hd), lambda b, i: (b, i, 0))
        kv_spec = pl.BlockSpec((1, length, kv_heads * hd), lambda b, i: (b, 0, 0))
        return _pcall(
            body, name=name + "_fwd", grid=(bsz, length // tq), in_specs=[q_spec, kv_spec, kv_spec],
            out_specs=[q_spec, pl.BlockSpec((1, heads, tq, 1), lambda b, i: (b, 0, i, 0))],
            out_shape=[jax.ShapeDtypeStruct(q.shape, BF16), jax.ShapeDtypeStruct((bsz, heads, length, 1), F32)],
            compiler_params=_params(("parallel", "parallel")),
        )(q, k, v)

    def call_bwd(q, k, v, o, lse, do):
        bsz, length, hd, tq, cb = shapes(q)
        nq = length // tq

        def body(q_ref, k_ref, v_ref, o_ref, lse_ref, do_ref, dq_ref, dk_ref, dv_ref, dk_acc, dv_acc):
            qi = pl.program_id(1)

            @pl.when(qi == 0)
            def _():
                dk_acc[...] = jnp.zeros_like(dk_acc)
                dv_acc[...] = jnp.zeros_like(dv_acc)

            def run(nk):
                k_all, v_all = k_ref[0, :nk], v_ref[0, :nk]
                dqs, dks, dvs = [], [], []
                for j in range(kv_heads):
                    kk, vv = k_all[:, j * hd:(j + 1) * hd], v_all[:, j * hd:(j + 1) * hd]
                    dk_sum = dv_sum = None
                    for h in range(j * grp, (j + 1) * grp):
                        dd = head(do_ref, h, hd)
                        qs, s = scores(head(q_ref, h, hd), kk, hd)
                        p = jnp.exp2(s - lse_ref[0, h])
                        delta = jnp.sum(dd.astype(F32) * head(o_ref, h, hd).astype(F32), axis=-1, keepdims=True)
                        t = (p * (lax.dot_general(dd, vv, _NT, preferred_element_type=F32) - delta)).astype(BF16)
                        dqs.append((jnp.dot(t, kk, preferred_element_type=F32) * hd ** -0.5).astype(BF16))
                        dk_h = lax.dot_general(t, qs, _TN, preferred_element_type=F32)
                        dv_h = lax.dot_general(p.astype(BF16), dd, _TN, preferred_element_type=F32)
                        dk_sum = dk_h if dk_sum is None else dk_sum + dk_h
                        dv_sum = dv_h if dv_sum is None else dv_sum + dv_h
                    dks.append(dk_sum)
                    dvs.append(dv_sum)
                dq_ref[0] = jnp.concatenate(dqs, axis=1)
                dk_acc[:nk] += jnp.concatenate(dks, axis=1)
                dv_acc[:nk] += jnp.concatenate(dvs, axis=1)

            by_segment(qi, cb, length, run)

            @pl.when(qi == nq - 1)
            def _():
                dk_ref[0] = (dk_acc[...] * LN_2).astype(dk_ref.dtype)
                dv_ref[0] = dv_acc[...].astype(dv_ref.dtype)

        q_spec = pl.BlockSpec((1, tq, heads * hd), lambda b, i: (b, i, 0))
        kv_spec = pl.BlockSpec((1, length, kv_heads * hd), lambda b, i: (b, 0, 0))
        lse_spec = pl.BlockSpec((1, heads, tq, 1), lambda b, i: (b, 0, i, 0))
        return _pcall(
            body, name=name + "_bwd", grid=(bsz, nq), in_specs=[q_spec, kv_spec, kv_spec, q_spec, lse_spec, q_spec],
            out_specs=[q_spec, kv_spec, kv_spec],
            out_shape=[jax.ShapeDtypeStruct(q.shape, BF16), jax.ShapeDtypeStruct(k.shape, BF16), jax.ShapeDtypeStruct(v.shape, BF16)],
            scratch_shapes=[pltpu.VMEM(k.shape[1:], F32), pltpu.VMEM(v.shape[1:], F32)],
            compiler_params=_params(("parallel", "arbitrary")),
        )(q, k, v, o, lse, do)

    @jax.custom_vjp
    def op(q, k, v):
        return call_fwd(q, k, v)[0]

    def fwd(q, k, v):
        o, lse = call_fwd(q, k, v)
        return o, (q, k, v, o, lse)

    def bwd(res, do):
        q, k, v, o, lse = res
        dq, dk, dv = call_bwd(q, k, v, o, lse, do)
        return dq, dk.astype(k.dtype), dv.astype(v.dtype)

    op.defvjp(fwd, bwd)
    return op


def _attention_latent_rows(name, heads, nope, rope, dv, heads_forward, heads_backward):
    lanes = 128
    scale = (nope + rope) ** -0.5

    def shapes(q, kn):
        bsz, lq, _ = q.shape
        return bsz, lq, kn.shape[1], _pick(lq, (256, 128))

    def scores(q_tile, kk):
        qs = (q_tile.astype(F32) * (LOG2_E * scale)).astype(BF16)
        return qs, lax.dot_general(qs, kk, _NT, preferred_element_type=F32)

    def call_fwd(q, kn, kr, v):
        bsz, lq, lk, tq = shapes(q, kn)
        heads_per_step = heads_forward

        def body(q_ref, kn_ref, kr_ref, v_ref, o_ref, lse_ref):
            real = lax.broadcasted_iota(jnp.int32, (lk, lanes), 1) < dv
            outs = []
            for h in range(heads_per_step):
                kk = kn_ref[0, :, h * lanes:(h + 1) * lanes] + kr_ref[0]
                v_ones = jnp.where(real, v_ref[0, :, h * lanes:(h + 1) * lanes], jnp.ones((lk, lanes), BF16))
                _, s = scores(q_ref[0, :, h * lanes:(h + 1) * lanes], kk)
                m = jnp.max(s, axis=-1, keepdims=True)
                acc = jnp.dot(jnp.exp2(s - m).astype(BF16), v_ones, preferred_element_type=F32)
                l = acc[:, dv:dv + 1]
                outs.append((acc[:, :dv] / l).astype(BF16))
                lse_ref[0, h] = m + jnp.log2(l)
            o_ref[0] = jnp.concatenate(outs, axis=1)

        wide = heads_per_step * lanes
        return _pcall(
            body, name=name + "_fwd", grid=(bsz, heads // heads_per_step, lq // tq),
            in_specs=[pl.BlockSpec((1, tq, wide), lambda b, g, i: (b, i, g)), pl.BlockSpec((1, lk, wide), lambda b, g, i: (b, 0, g)),
                      pl.BlockSpec((1, lk, lanes), lambda b, g, i: (b, 0, 0)), pl.BlockSpec((1, lk, wide), lambda b, g, i: (b, 0, g))],
            out_specs=[pl.BlockSpec((1, tq, heads_per_step * dv), lambda b, g, i: (b, i, g)),
                       pl.BlockSpec((1, heads_per_step, tq, 1), lambda b, g, i: (b, g, i, 0))],
            out_shape=[jax.ShapeDtypeStruct((bsz, lq, heads * dv), BF16), jax.ShapeDtypeStruct((bsz, heads, lq, 1), F32)],
            compiler_params=_params(("parallel", "parallel", "parallel")),
        )(q, kn, kr, v)

    def call_bwd(q, kn, kr, v, o, lse, do):
        bsz, lq, lk, tq = shapes(q, kn)
        heads_per_step = heads_backward

        def body(q_ref, kn_ref, kr_ref, v_ref, o_ref, lse_ref, do_ref, dq_ref, dkn_ref, dkr_ref, dv_ref, dkn_acc, dkr_acc, dv_acc):
            g, qi = pl.program_id(1), pl.program_id(2)
            last_q = qi == lq // tq - 1

            @pl.when(qi == 0)
            def _():
                dkn_acc[...] = jnp.zeros_like(dkn_acc)
                dv_acc[...] = jnp.zeros_like(dv_acc)

            @pl.when((qi == 0) & (g == 0))
            def _():
                dkr_acc[...] = jnp.zeros_like(dkr_acc)

            dqs, dks, dvs = [], [], []
            for h in range(heads_per_step):
                kk = kn_ref[0, :, h * lanes:(h + 1) * lanes] + kr_ref[0]
                vv = v_ref[0, :, h * lanes:h * lanes + dv]
                dd = do_ref[0, :, h * dv:(h + 1) * dv]
                qs, s = scores(q_ref[0, :, h * lanes:(h + 1) * lanes], kk)
                p = jnp.exp2(s - lse_ref[0, h])
                delta = jnp.sum(dd.astype(F32) * o_ref[0, :, h * dv:(h + 1) * dv].astype(F32), axis=-1, keepdims=True)
                t = (p * (lax.dot_general(dd, vv, _NT, preferred_element_type=F32) - delta)).astype(BF16)
                dqs.append((jnp.dot(t, kk, preferred_element_type=F32) * scale).astype(BF16))
                dks.append(lax.dot_general(t, qs, _TN, preferred_element_type=F32) * LN_2)
                dv_h = lax.dot_general(p.astype(BF16), dd, _TN, preferred_element_type=F32)
                dvs.append(jnp.concatenate([dv_h, jnp.zeros((lk, lanes - dv), F32)], axis=1))
            dq_ref[0] = jnp.concatenate(dqs, axis=1)
            dkn_acc[...] += jnp.concatenate(dks, axis=1)
            dv_acc[...] += jnp.concatenate(dvs, axis=1)
            shared = dks[0]
            for d_h in dks[1:]:
                shared = shared + d_h
            dkr_acc[...] += shared

            @pl.when(last_q)
            def _():
                dkn_ref[0] = dkn_acc[...].astype(dkn_ref.dtype)
                dv_ref[0] = dv_acc[...].astype(dv_ref.dtype)

            @pl.when(last_q & (g == heads // heads_per_step - 1))
            def _():
                dkr_ref[0] = dkr_acc[...].astype(dkr_ref.dtype)

        wide = heads_per_step * lanes
        q_spec = pl.BlockSpec((1, tq, wide), lambda b, g, i: (b, i, g))
        k_spec = pl.BlockSpec((1, lk, wide), lambda b, g, i: (b, 0, g))
        kr_spec = pl.BlockSpec((1, lk, lanes), lambda b, g, i: (b, 0, 0))
        o_spec = pl.BlockSpec((1, tq, heads_per_step * dv), lambda b, g, i: (b, i, g))
        lse_spec = pl.BlockSpec((1, heads_per_step, tq, 1), lambda b, g, i: (b, g, i, 0))
        return _pcall(
            body, name=name + "_bwd", grid=(bsz, heads // heads_per_step, lq // tq),
            in_specs=[q_spec, k_spec, kr_spec, k_spec, o_spec, lse_spec, o_spec],
            out_specs=[q_spec, k_spec, kr_spec, k_spec],
            out_shape=[jax.ShapeDtypeStruct(q.shape, BF16), jax.ShapeDtypeStruct(kn.shape, BF16), jax.ShapeDtypeStruct(kr.shape, BF16),
                       jax.ShapeDtypeStruct(v.shape, BF16)],
            scratch_shapes=[pltpu.VMEM((lk, wide), F32), pltpu.VMEM((lk, lanes), F32), pltpu.VMEM((lk, wide), F32)],
            compiler_params=_params(("parallel", "arbitrary", "arbitrary")),
        )(q, kn, kr, v, o, lse, do)

    @jax.custom_vjp
    def op(q, kn, kr, v):
        return call_fwd(q, kn, kr, v)[0]

    def fwd(q, kn, kr, v):
        o, lse = call_fwd(q, kn, kr, v)
        return o, (q, kn, kr, v, o, lse)

    def bwd(res, do):
        q, kn, kr, v, o, lse = res
        dq, dkn, dkr, dv_ = call_bwd(q, kn, kr, v, o, lse, do)
        return dq, dkn.astype(kn.dtype), dkr.astype(kr.dtype), dv_.astype(v.dtype)

    op.defvjp(fwd, bwd)
    return op


def _conv_call(ypad, taps, name):
    bsz, lp, ch = ypad.shape
    length = lp - 2 * ROW_ALIGN
    tl = _pick(length, (256, 128))

    def body(y_ref, w_ref, o_ref):
        base = pl.multiple_of(pl.program_id(1) * tl, tl)
        for r0 in range(0, tl, CONV_ROWS):
            win = y_ref[0, pl.ds(base + r0, CONV_ROWS + 2 * ROW_ALIGN), :]
            acc = jnp.broadcast_to(w_ref[pl.ds(D_CONV, 1), :], (CONV_ROWS, ch))
            for shift in range(SUBLANES):
                shifted = win[shift:shift + CONV_ROWS + 2 * ROW_ALIGN - SUBLANES, :]
                for k in range(shift, D_CONV, SUBLANES):
                    acc = acc + shifted[k - shift:k - shift + CONV_ROWS, :] * w_ref[pl.ds(k, 1), :]
            o_ref[0, pl.ds(r0, CONV_ROWS), :] = acc

    return _pcall(
        body, name=name, grid=(bsz, length // tl),
        in_specs=[pl.BlockSpec((1, lp, ch), lambda b, l: (b, 0, 0)), pl.BlockSpec((D_CONV + 1, ch), lambda b, l: (0, 0))],
        out_specs=pl.BlockSpec((1, tl, ch), lambda b, l: (b, l, 0)),
        out_shape=jax.ShapeDtypeStruct((bsz, length, ch), F32),
        compiler_params=_params(("parallel", "parallel")),
    )(ypad, taps)


def _conv_dw_call(ypad, dout, name):
    bsz, lp, ch = ypad.shape
    length = lp - 2 * ROW_ALIGN
    tl = _pick(length, (256, 128))

    def body(y_ref, d_ref, o_ref):
        b, l = pl.program_id(0), pl.program_id(1)

        @pl.when((b == 0) & (l == 0))
        def _():
            o_ref[...] = jnp.zeros_like(o_ref)

        base = pl.multiple_of(l * tl, tl)
        win = y_ref[0, pl.ds(base, tl + 2 * ROW_ALIGN), :]
        dd = d_ref[0]
        for shift in range(SUBLANES):
            shifted = win[shift:shift + tl + 2 * ROW_ALIGN - SUBLANES, :]
            for k in range(shift, D_CONV, SUBLANES):
                o_ref[pl.ds(k, 1), :] += jnp.sum(shifted[k - shift:k - shift + tl, :] * dd, axis=0, keepdims=True)
        o_ref[pl.ds(D_CONV, 1), :] += jnp.sum(dd, axis=0, keepdims=True)

    return _pcall(
        body, name=name, grid=(bsz, length // tl),
        in_specs=[pl.BlockSpec((1, lp, ch), lambda b, l: (b, 0, 0)), pl.BlockSpec((1, tl, ch), lambda b, l: (b, l, 0))],
        out_specs=pl.BlockSpec((D_CONV + 1, ch), lambda b, l: (0, 0)),
        out_shape=jax.ShapeDtypeStruct((D_CONV + 1, ch), F32),
        compiler_params=_params(("arbitrary", "arbitrary")),
    )(ypad, dout)


def _pad_rows(y):
    return jnp.pad(y, ((0, 0), (CONV_PAD, 2 * ROW_ALIGN - CONV_PAD), (0, 0)))


@jax.custom_vjp
def _dwconv(y, taps):
    return _conv_call(_pad_rows(y), taps, "conv_fwd")


def _dwconv_fwd(y, taps):
    return _dwconv(y, taps), (y, taps)


def _dwconv_bwd(res, dout):
    y, taps = res
    flipped = jnp.concatenate([taps[:D_CONV][::-1], jnp.zeros_like(taps[D_CONV:])], axis=0)
    dy = _conv_call(_pad_rows(dout), flipped, "conv_dy")
    dtaps = _conv_dw_call(_pad_rows(y), dout, "conv_dw")
    return dy, dtaps


_dwconv.defvjp(_dwconv_fwd, _dwconv_bwd)


def _loss_head(x, target, g):
    bsz, length, d = x.shape
    tl = _pick(length, (256, 128))

    def f(xb, tb, gb):
        err = _rms(xb, gb) - tb
        return 0.5 * jnp.sum(jnp.sum(err * err, axis=-1, keepdims=True), axis=0, keepdims=True) / d

    def body(x_ref, t_ref, g_ref, loss_ref, dx_ref, dg_ref):
        val, vjp = jax.vjp(lambda xb, gb: f(xb, t_ref[0], gb), x_ref[0], g_ref[...])
        dx, dg = vjp(jnp.ones((1, 1), F32))
        dx_ref[0] = dx
        first = (pl.program_id(0) == 0) & (pl.program_id(1) == 0)

        @pl.when(first)
        def _():
            loss_ref[...] = val
            dg_ref[...] = dg

        @pl.when(jnp.logical_not(first))
        def _():
            loss_ref[...] += val
            dg_ref[...] += dg

    row = pl.BlockSpec((1, tl, d), lambda b, l: (b, l, 0))
    return _pcall(
        body, name="loss_head", grid=(bsz, length // tl),
        in_specs=[row, row, pl.BlockSpec((1, d), lambda b, l: (0, 0))],
        out_specs=[pl.BlockSpec((1, 1), lambda b, l: (0, 0)), row, pl.BlockSpec((1, d), lambda b, l: (0, 0))],
        out_shape=[jax.ShapeDtypeStruct((1, 1), F32), jax.ShapeDtypeStruct(x.shape, F32), jax.ShapeDtypeStruct((1, d), F32)],
        compiler_params=_params(("arbitrary", "arbitrary")),
    )(x, target, g)


def _adamw(w, g, m, v, name):
    shape = w.shape
    cols = shape[-1]
    rows = w.size // cols
    tr = _pick(rows, (512, 256, 128))
    w2, g2, m2, v2 = (t.reshape(rows, cols) for t in (w, g, m, v))

    def body(w_ref, g_ref, m_ref, v_ref, d_ref, nm_ref, nv_ref):
        gg = g_ref[...]
        nm = ADAM_B1 * m_ref[...] + (1.0 - ADAM_B1) * gg
        nv = ADAM_B2 * v_ref[...] + (1.0 - ADAM_B2) * jnp.square(gg)
        m_hat = nm / (1.0 - ADAM_B1 ** ADAM_STEP)
        v_hat = nv / (1.0 - ADAM_B2 ** ADAM_STEP)
        d_ref[...] = -ADAM_LR * (m_hat / (jnp.sqrt(v_hat) + ADAM_EPS) + ADAM_WD * w_ref[...])
        nm_ref[...] = nm
        nv_ref[...] = nv

    spec = pl.BlockSpec((tr, cols), lambda i: (i, 0))
    outs = _pcall(
        body, name=name, grid=(rows // tr,), in_specs=[spec] * 4, out_specs=[spec] * 3,
        out_shape=[jax.ShapeDtypeStruct((rows, cols), F32)] * 3,
        compiler_params=_params(("parallel",)),
    )(w2, g2, m2, v2)
    return tuple(o.reshape(shape) for o in outs)


def _mesh_pos():
    return lax.axis_index("x"), lax.axis_index("y"), lax.axis_index("c")


_RELATIONS = [(dx, dy, dc) for dx in (0, 1) for dy in (0, 1) for dc in (0, 1)][1:]


def _peer(pos, rel):
    return tuple(jnp.where(r == 1, 1 - p, p) if r else p for p, r in zip(pos, rel))


def _block_index(pos):
    return 4 * pos[0] + 2 * pos[1] + pos[2]


_HBM = pl.BlockSpec(memory_space=pltpu.HBM)


def _all_gather(xs, name):
    n = len(xs)

    def body(*refs):
        x_refs, out_refs, (send_sems, recv_sems, local_sems) = refs[:n], refs[n:2 * n], refs[2 * n:]
        x_, y_, c_ = _mesh_pos()
        me, sibling = (x_, y_, c_), (x_, y_, 1 - c_)
        chips = [(1 - x_, y_), (x_, 1 - y_), (1 - x_, 1 - y_)]

        def copy(t, k, block, to, own=False):
            slot = out_refs[t].at[_block_index(block)]
            return pltpu.make_async_remote_copy(
                src_ref=x_refs[t] if own else slot, dst_ref=slot, send_sem=send_sems.at[7 * t + k], recv_sem=recv_sems.at[7 * t + k],
                device_id=to, device_id_type=pl.DeviceIdType.MESH)

        mine = [pltpu.make_async_copy(x_refs[t], out_refs[t].at[_block_index(me)], local_sems.at[t]) for t in range(n)]
        first = [[copy(t, 0, me, sibling, own=True)] + [copy(t, 1 + j, me, (*chip, c_), own=True) for j, chip in enumerate(chips)]
                 for t in range(n)]
        passed = [[copy(t, 4 + j, (*chip, c_), sibling) for j, chip in enumerate(chips)] for t in range(n)]
        for t in range(n):
            mine[t].start()
            for cp in first[t]:
                cp.start()
        for t in range(n):
            for j, chip in enumerate(chips):
                copy(t, 1 + j, (*chip, c_), me).wait_recv()
                passed[t][j].start()
        for t in range(n):
            copy(t, 0, sibling, me).wait_recv()
            for j, chip in enumerate(chips):
                copy(t, 4 + j, (*chip, 1 - c_), me).wait_recv()
            for cp in first[t] + passed[t]:
                cp.wait_send()
            mine[t].wait()

    return _pcall(
        body, name=name, in_specs=[_HBM] * n, out_specs=[_HBM] * n,
        out_shape=[jax.ShapeDtypeStruct((N_DEV,) + x.shape, x.dtype) for x in xs],
        scratch_shapes=[pltpu.SemaphoreType.DMA((7 * n,)), pltpu.SemaphoreType.DMA((7 * n,)), pltpu.SemaphoreType.DMA((n,))],
    )(*xs)


_SEM = pl.BlockSpec(memory_space=pltpu.SEMAPHORE)
_EFFECT = pltpu.SideEffectType.DATAFLOW_SIDE_EFFECTING


def _push_start(srcs, after, name):
    n = len(srcs)
    lands = [lax.empty((N_DEV,) + s.shape[-2:], s.dtype) for s in srcs]

    def body(*refs):
        src_refs, land_refs = refs[:n], refs[n:2 * n]
        send_sems, recv_sems, token = refs[2 * n + 1:3 * n + 1], refs[3 * n + 1:4 * n + 1], refs[-1]
        me = _mesh_pos()
        for t in range(n):
            for rel in _RELATIONS:
                peer = _peer(me, rel)
                pltpu.make_async_remote_copy(
                    src_ref=src_refs[t].at[_block_index(peer)] if srcs[t].ndim == 3 else src_refs[t], dst_ref=land_refs[t].at[_block_index(me)],
                    send_sem=send_sems[t], recv_sem=recv_sems[t], device_id=peer, device_id_type=pl.DeviceIdType.MESH).start()
        token[...] = jnp.zeros_like(token)

    outs = _pcall(
        body, name=name,
        out_shape=[pltpu.SemaphoreType.DMA(())] * (2 * n) + [pltpu.HBM(s.shape, s.dtype) for s in srcs]
        + [pltpu.HBM(l.shape, l.dtype) for l in lands] + [jax.ShapeDtypeStruct((8, 128), F32)],
        in_specs=[_HBM] * (2 * n) + [pl.BlockSpec(memory_space=pl.ANY)],
        out_specs=[_SEM] * (2 * n) + [_HBM] * (2 * n) + [pl.BlockSpec(memory_space=pltpu.VMEM)],
        input_output_aliases={i: 2 * n + i for i in range(2 * n)}, compiler_params=pltpu.CompilerParams(has_side_effects=_EFFECT),
    )(*[pltpu.with_memory_space_constraint(t, pltpu.HBM) for t in list(srcs) + lands], after)
    return (outs[:n], outs[n:2 * n], outs[2 * n:3 * n], outs[3 * n:4 * n]), outs[-1]


def _push_wait(handle, after, owns, me, name):
    send_sems, recv_sems, src_thrus, land_thrus = handle
    n = len(land_thrus)

    def body(*refs):
        land_refs, sends, recvs = refs[n:2 * n], refs[2 * n:3 * n], refs[3 * n:4 * n]
        for t in range(n):
            seven = land_refs[t].at[pl.ds(0, N_DEV - 1)]
            all_seven = pltpu.make_async_remote_copy(src_ref=seven, dst_ref=seven, send_sem=sends[t], recv_sem=recvs[t],
                                                     device_id=_mesh_pos(), device_id_type=pl.DeviceIdType.MESH)
            all_seven.wait_send()
            all_seven.wait_recv()

    outs = _pcall(
        body, name=name,
        out_shape=[pltpu.HBM(t.shape, t.dtype) for t in list(src_thrus) + list(land_thrus)],
        in_specs=[_HBM] * (2 * n) + [_SEM] * (2 * n) + [pl.BlockSpec(memory_space=pl.ANY)], out_specs=[_HBM] * (2 * n),
        input_output_aliases={i: i for i in range(2 * n)}, compiler_params=pltpu.CompilerParams(has_side_effects=_EFFECT),
    )(*src_thrus, *land_thrus, *send_sems, *recv_sems, after)
    return [lax.dynamic_update_slice(land, own[None], (me, 0, 0)) for land, own in zip(outs[n:], owns)]


def _sum_blocks(p, name):
    n, rows, cols = p.shape
    tr = _pick(rows, (256, 128, 64, 32, 16, 8))

    def body(p_ref, o_ref):
        acc = p_ref[0].astype(F32)
        for s in range(1, n):
            acc = acc + p_ref[s].astype(F32)
        o_ref[...] = acc

    return _pcall(
        body, name=name, grid=(rows // tr,),
        in_specs=[pl.BlockSpec((n, tr, cols), lambda i: (0, i, 0))], out_specs=pl.BlockSpec((tr, cols), lambda i: (i, 0)),
        out_shape=jax.ShapeDtypeStruct((rows, cols), F32), compiler_params=_params(("parallel",)),
    )(p)


def _sum_rows(t, name):
    def body(t_ref, o_ref):
        o_ref[...] = jnp.sum(t_ref[...], axis=0, keepdims=True)

    return _pcall(body, name=name, out_shape=jax.ShapeDtypeStruct((1, t.shape[1]), F32))(t)


class _Packing:
    def __init__(self, sizes, align):
        self.offsets, self.sizes, self.align = {}, dict(sizes), align
        row = 0
        for name, size in sizes:
            self.offsets[name] = row
            row += -(-size // (align * PACK_COLS)) * align
        self.rows = row

    def pack(self, pieces):
        return self.pack_blocks({n: pieces[n].reshape(1, -1) for n in self.sizes})[0]

    def pack_blocks(self, pieces):
        out = []
        for n, size in self.sizes.items():
            padded = -(-size // (self.align * PACK_COLS)) * self.align * PACK_COLS
            out.append(jnp.pad(pieces[n], ((0, 0), (0, padded - size))).reshape(pieces[n].shape[0], -1, PACK_COLS))
        return jnp.concatenate(out, axis=1)

    def piece(self, packed, name, lead=()):
        start, size = self.offsets[name], self.sizes[name]
        nrow = -(-size // (self.align * PACK_COLS)) * self.align
        sl = packed[..., start:start + nrow, :]
        return sl.reshape(lead + (nrow * PACK_COLS,))[..., :size]


_PIECES = (("ev_w_in", "ev_w_in", 0, 1, "a"), ("w_out0", "w_out", 0, 0, "a"),
           ("mlp_w1_0", "mlp_w1", 0, 1, "b"), ("mlp_w2_0", "mlp_w2", 0, 0, "b"),
           ("od_w_in", "od_w_in", 0, 1, "c"), ("od_w_uq", "od_w_uq", 0, 1, "c"), ("od_w_ukv", "od_w_ukv", 0, 1, "c"),
           ("w_out1", "w_out", 1, 0, "c"), ("mlp_w1_1", "mlp_w1", 1, 1, "c"), ("mlp_w2_1", "mlp_w2", 1, 0, "c"))
_SMALL_SHARDED = (("od_q_norm_g", 1), ("od_conv_w", 2), ("od_conv_b", 1), ("od_ln_g", 1), ("od_ln_b", 1))
_REPLICATED = ("c_ctx", "norm1_g", "norm2_g", "ev_q_norm_g", "ev_k_norm_g", "ev_sgu_norm_g", "ev_sgu_w", "ev_sgu_b",
               "od_kv_norm_g", "final_g")


def _unshard(blocks, axis):
    moved = jnp.moveaxis(blocks, 0, axis)
    shape = moved.shape
    return moved.reshape(shape[:axis] + (shape[axis] * shape[axis + 1],) + shape[axis + 2:])


def _group_mean_matrix(width, group):
    idx = jnp.arange(width) // group
    return (idx[:, None] == idx[None, :]).astype(F32) / group


def _angles(length, d_rot):
    rows = length // GRID_W
    row = jnp.broadcast_to(jnp.arange(rows)[:, None], (rows, GRID_W)).reshape(-1).astype(F32)
    col = jnp.broadcast_to(jnp.arange(GRID_W)[None, :], (rows, GRID_W)).reshape(-1).astype(F32)
    d_axis = d_rot // 2
    inv = ROPE_THETA ** (-jnp.arange(0, d_axis, 2, dtype=F32) / d_axis)
    return jnp.concatenate([row[:, None] * inv, col[:, None] * inv], axis=-1)


def _rope_tables(length, d_rot, head_dim, heads, ctx_len, tail=0):
    ang = _angles(length, d_rot)
    cos = jnp.repeat(jnp.cos(ang), 2, axis=1)
    sin = jnp.repeat(jnp.sin(ang), 2, axis=1) * jnp.tile(jnp.array([-1.0, 1.0], F32), d_rot // 2)
    keep = head_dim - d_rot - tail
    cos = jnp.concatenate([jnp.ones((length, keep), F32), cos, jnp.ones((length, tail), F32)], axis=1)
    sin = jnp.concatenate([jnp.zeros((length, keep), F32), sin, jnp.zeros((length, tail), F32)], axis=1)
    cos, sin = jnp.tile(cos, (1, heads)), jnp.tile(sin, (1, heads))
    cos = jnp.concatenate([jnp.ones((ctx_len, cos.shape[1]), F32), cos], axis=0)
    sin = jnp.concatenate([jnp.zeros((ctx_len, sin.shape[1]), F32), sin], axis=0)
    return cos, sin


def _segment_params(mod, bsz):
    parts = jnp.split(mod, N_MOD, axis=-1)
    out = []
    for part in parts:
        lat = part[:bsz]
        ctx = jnp.broadcast_to(part[bsz:bsz + 1], lat.shape)
        out.append(jnp.stack([ctx, lat], axis=1)[:, :, None, :])
    return out


def _flat(t):
    return t.reshape(-1, t.shape[-1])


def _sequence_rowwise(ctx_len):
    tl = math.gcd(256, ctx_len)

    def make(name, f, out_specs, rows_per_block=tl, ctx_blocks=ctx_len // tl):
        return _rowwise(name, f, out_specs, rows_per_block, ctx_blocks)

    return make


def _mixer0(xall, modrows0, w, bsz, length, ctx_len):
    d = xall.shape[-1]
    total = ctx_len + length
    rowwise, flat = _sequence_rowwise(ctx_len), _flat
    sh1, sc1, g1, sh2, sc2, _ = _segment_params(modrows0, bsz)
    (h,) = rowwise("mod0", _f_modulate, [(d, BF16)])((xall,), (), (sh1, sc1), (w["norm1_g0"],), ())
    ev_q, ev_kv = A_Q_HEADS * A_HEAD_DIM, A_KV_HEADS * A_HEAD_DIM
    half = B_GROUPS * B_GROUP_DIM
    groups = tuple(jnp.split(w["ev_w_in"], [ev_q, ev_q + ev_kv, ev_q + 2 * ev_kv, ev_q + 2 * ev_kv + half], axis=0))
    qp, kp, vp, zu, zv = [t.reshape(bsz, total, -1) for t in _linear_multi("ev_in", (F32, F32, BF16, F32, F32))(flat(h), groups)]
    cos_q, sin_q = _rope_tables(length, A_HEAD_DIM, A_HEAD_DIM, A_Q_HEADS, ctx_len)
    cos_k, sin_k = cos_q[:, :ev_kv], sin_q[:, :ev_kv]
    (q,) = rowwise("ev_q", _f_headnorm_rope, [(ev_q, BF16)])(
        (qp,), (cos_q, sin_q), (), (jnp.tile(w["ev_q_norm_g"][0], A_Q_HEADS)[None],), (_group_mean_matrix(ev_q, A_HEAD_DIM),))
    (k,) = rowwise("ev_k", _f_headnorm_rope, [(ev_kv, BF16)])(
        (kp,), (cos_k, sin_k), (), (jnp.tile(w["ev_k_norm_g"][0], A_KV_HEADS)[None],), (_group_mean_matrix(ev_kv, A_HEAD_DIM),))
    o_att = _attention_rows("gqa", ctx_len, A_Q_HEADS, A_KV_HEADS)(q, k, vp)
    u, vn = rowwise("sgu_pre", _f_sgu_pre, [(half, F32), (half, BF16)])(
        (zu, zv), (), (), (w["ev_sgu_norm_g"][0].reshape(1, half),), (_group_mean_matrix(half, B_GROUP_DIM),))
    bias = jnp.repeat(w["ev_sgu_b"][0].T, B_GROUP_DIM, axis=1)
    (o_sgu,) = rowwise("sgu_mix", _f_sgu_mix, [(half, BF16)], rows_per_block=B_CHUNK, ctx_blocks=0)(
        (u, vn), (), (), tuple(w["ev_sgu_w"][0][g] for g in range(B_GROUPS)) + (bias,), ())
    y = _linear_sum("out0")((flat(o_att), flat(o_sgu)), tuple(jnp.split(w["w_out0"], 2, axis=0))).reshape(bsz, total, d)
    x1, h = rowwise("res_mod0a", _f_res_modulate, [(d, F32), (d, BF16)])((xall, y), (), (g1, sh2, sc2), (w["norm2_g0"],), ())
    return x1, h


def _mlp0(x1, h, modrows0, modrows1, w, bsz, length, ctx_len):
    d = x1.shape[-1]
    total = ctx_len + length
    g2 = _segment_params(modrows0, bsz)[5]
    sh1, sc1 = _segment_params(modrows1, bsz)[:2]
    y = _mlp("mlp0")(_flat(h), w["mlp_w1_0"], w["mlp_w2_0"]).reshape(bsz, total, d)
    return _sequence_rowwise(ctx_len)("res_mod0b", _f_res_modulate, [(d, F32), (d, BF16)])((x1, y), (), (g2, sh1, sc1), (w["norm1_g1"],), ())


def _layer1(x2, h, modrows1, w, bsz, length, ctx_len):
    d = x2.shape[-1]
    total = ctx_len + length
    half = B_GROUPS * B_GROUP_DIM
    rowwise, flat = _sequence_rowwise(ctx_len), _flat
    _, _, g1n, sh2n, sc2n, g2n = _segment_params(modrows1, bsz)
    g_cq, g_ckv, g_kr, g_za, g_zg = jnp.split(w["od_w_in"], [C_Q_RANK, C_Q_RANK + C_KV_RANK, C_Q_RANK + C_KV_RANK + C_ROPE,
                                                             C_Q_RANK + C_KV_RANK + C_ROPE + half], axis=0)
    lanes, c_qk = 128, C_NOPE + C_ROPE
    g_kr = jnp.pad(g_kr, ((C_NOPE, lanes - c_qk), (0, 0)))
    cq, ckv, kr, za, zg = [t.reshape(bsz, total, -1) for t in _linear_multi("od_in", (F32,) * 5)(flat(h), (g_cq, g_ckv, g_kr, g_za, g_zg))]
    lat = slice(ctx_len, total)
    lat_tl = math.gcd(256, length)
    (cqn,) = _rowwise("od_qn", _f_rms, [(C_Q_RANK, BF16)], lat_tl)((cq[:, lat],), (), (), (w["od_q_norm_g"],), ())
    w_uq = jnp.pad(w["od_w_uq"].reshape(C_HEADS, c_qk, C_Q_RANK), ((0, 0), (0, lanes - c_qk), (0, 0))).reshape(C_HEADS * lanes, C_Q_RANK)
    qf = _linear("od_uq")(flat(cqn), w_uq).reshape(bsz, length, C_HEADS * lanes)
    cos_q, sin_q = _rope_tables(length, C_ROPE, lanes, C_HEADS, 0, tail=lanes - c_qk)
    (q,) = _rowwise("od_qrope", _f_rope, [(C_HEADS * lanes, BF16)], lat_tl)((qf,), (cos_q, sin_q), (), (), ())
    (ckvn,) = rowwise("od_kvn", _f_rms, [(C_KV_RANK, BF16)])((ckv,), (), (), (w["od_kv_norm_g"],), ())
    per_head = w["od_w_ukv"].reshape(C_HEADS, C_NOPE + C_V, C_KV_RANK)
    w_kn = jnp.pad(per_head[:, :C_NOPE], ((0, 0), (0, lanes - C_NOPE), (0, 0))).reshape(C_HEADS * lanes, C_KV_RANK)
    w_v = jnp.pad(per_head[:, C_NOPE:], ((0, 0), (0, lanes - C_V), (0, 0))).reshape(C_HEADS * lanes, C_KV_RANK)
    kn, vv = [t.reshape(bsz, total, -1) for t in _linear_multi("od_ukv", (BF16, BF16))(flat(ckvn), (w_kn, w_v))]
    cos_r, sin_r = _rope_tables(length, C_ROPE, lanes, 1, ctx_len, tail=lanes - c_qk)
    (krr,) = rowwise("od_krope", _f_rope, [(lanes, BF16)])((kr,), (cos_r, sin_r), (), (), ())
    o_att = _attention_latent_rows("mla", C_HEADS, C_NOPE, C_ROPE, C_V, 4, 2)(q, kn, krr, vv)
    (glu,) = _rowwise("glu", _f_glu, [(half, F32)], lat_tl)((za[:, lat], zg[:, lat]), (), (), (), ())
    taps = jnp.concatenate([w["od_conv_w"][0], w["od_conv_b"]], axis=0)
    conv = _dwconv(glu, taps)
    (o_conv,) = _rowwise("ln_silu", _f_ln_silu, [(half, BF16)], lat_tl)((conv,), (), (), (w["od_ln_g"], w["od_ln_b"]), ())
    y = _linear_sum("out1")((flat(o_att), flat(o_conv)), tuple(jnp.split(w["w_out1"], 2, axis=0))).reshape(bsz, length, d)
    lat_param = lambda p: p[:, 1:]
    x3, h = _rowwise("res_mod1a", _f_res_modulate, [(d, F32), (d, BF16)], lat_tl)(
        (x2[:, lat], y), (), (lat_param(g1n), lat_param(sh2n), lat_param(sc2n)), (w["norm2_g1"],), ())
    y = _mlp("mlp1")(flat(h), w["mlp_w1_1"], w["mlp_w2_1"]).reshape(bsz, length, d)
    (x4,) = _rowwise("res1b", _f_res, [(d, F32)], lat_tl)((x3, y), (), (lat_param(g2n),), (), ())
    return x4


def kernel(x, c, ctx, c_ctx, ada_w, ada_b, norm1_g, norm2_g, w_out, mlp_w1, mlp_w2, ev_w_in, ev_q_norm_g, ev_k_norm_g, ev_sgu_norm_g, ev_sgu_w, ev_sgu_b, od_w_in, od_q_norm_g, od_kv_norm_g, od_w_uq, od_w_ukv, od_conv_w, od_conv_b, od_ln_g, od_ln_b, final_g, loss_target, m_c_ctx, m_ada_w, m_ada_b, m_norm1_g, m_norm2_g, m_w_out, m_mlp_w1, m_mlp_w2, m_ev_w_in, m_ev_q_norm_g, m_ev_k_norm_g, m_ev_sgu_norm_g, m_ev_sgu_w, m_ev_sgu_b, m_od_w_in, m_od_q_norm_g, m_od_kv_norm_g, m_od_w_uq, m_od_w_ukv, m_od_conv_w, m_od_conv_b, m_od_ln_g, m_od_ln_b, m_final_g, v_c_ctx, v_ada_w, v_ada_b, v_norm1_g, v_norm2_g, v_w_out, v_mlp_w1, v_mlp_w2, v_ev_w_in, v_ev_q_norm_g, v_ev_k_norm_g, v_ev_sgu_norm_g, v_ev_sgu_w, v_ev_sgu_b, v_od_w_in, v_od_q_norm_g, v_od_kv_norm_g, v_od_w_uq, v_od_w_ukv, v_od_conv_w, v_od_conv_b, v_od_ln_g, v_od_ln_b, v_final_g):
    names = ["c_ctx", "ada_w", "ada_b", "norm1_g", "norm2_g", "w_out", "mlp_w1", "mlp_w2", "ev_w_in", "ev_q_norm_g", "ev_k_norm_g",
             "ev_sgu_norm_g", "ev_sgu_w", "ev_sgu_b", "od_w_in", "od_q_norm_g", "od_kv_norm_g", "od_w_uq", "od_w_ukv", "od_conv_w",
             "od_conv_b", "od_ln_g", "od_ln_b", "final_g"]
    local = dict(zip(names, [c_ctx, ada_w, ada_b, norm1_g, norm2_g, w_out, mlp_w1, mlp_w2, ev_w_in, ev_q_norm_g, ev_k_norm_g, ev_sgu_norm_g, ev_sgu_w, ev_sgu_b, od_w_in, od_q_norm_g, od_kv_norm_g, od_w_uq, od_w_ukv, od_conv_w, od_conv_b, od_ln_g, od_ln_b, final_g]))
    mom1 = dict(zip(names, [m_c_ctx, m_ada_w, m_ada_b, m_norm1_g, m_norm2_g, m_w_out, m_mlp_w1, m_mlp_w2, m_ev_w_in, m_ev_q_norm_g, m_ev_k_norm_g, m_ev_sgu_norm_g, m_ev_sgu_w, m_ev_sgu_b, m_od_w_in, m_od_q_norm_g, m_od_kv_norm_g, m_od_w_uq, m_od_w_ukv, m_od_conv_w, m_od_conv_b, m_od_ln_g, m_od_ln_b, m_final_g]))
    mom2 = dict(zip(names, [v_c_ctx, v_ada_w, v_ada_b, v_norm1_g, v_norm2_g, v_w_out, v_mlp_w1, v_mlp_w2, v_ev_w_in, v_ev_q_norm_g, v_ev_k_norm_g, v_ev_sgu_norm_g, v_ev_sgu_w, v_ev_sgu_b, v_od_w_in, v_od_q_norm_g, v_od_kv_norm_g, v_od_w_uq, v_od_w_ukv, v_od_conv_w, v_od_conv_b, v_od_ln_g, v_od_ln_b, v_final_g]))
    bsz, length, d = x.shape
    ctx_len = ctx.shape[1]
    me = _block_index(_mesh_pos())

    shard = {p: local[wn][layer] for p, wn, layer, _, _ in _PIECES}
    by_columns = {p: axis == 1 for p, _, _, axis, _ in _PIECES}
    stages = {s: [p for p, _, _, _, st in _PIECES if st == s] for s in "abc"}
    block_rows = {p: shard[p].shape[1] if by_columns[p] else shard[p].shape[0] for p in shard}

    def pad_block_rows(t, p):
        extra = -block_rows[p] % ROW_ALIGN
        return jnp.pad(t, [(0, 0)] * (t.ndim - 2) + [(0, extra), (0, 0)]) if extra else t

    def travelling(p):
        t = shard[p].astype(BF16)
        return pad_block_rows(t.T if by_columns[p] else t, p)

    mine = {s: [travelling(p) for p in stages[s]] for s in "abc"}
    tiny_pack = _Packing([(n, local[n].size) for n, _ in _SMALL_SHARDED], 8)

    def unpack_weights(s, gathered):
        return {p: g[:, :block_rows[p]].reshape(N_DEV * block_rows[p], g.shape[2]) for p, g in zip(stages[s], gathered)}

    def pack_grads(s, g):
        return [pad_block_rows(g[p].reshape(N_DEV, block_rows[p], g[p].shape[1]), p) for p in stages[s]]

    cond_local = jnp.concatenate([c, c_ctx[None], jnp.zeros((COND_ROWS - bsz - 1, d), F32)], axis=0)
    cond, gathered_tiny, *gathered_a = _all_gather(
        [cond_local, tiny_pack.pack({n: local[n] for n, _ in _SMALL_SHARDED})] + mine["a"], "gather_inputs")
    cond = cond.reshape(N_DEV * COND_ROWS, d)
    silu_op = _rowwise("silu", _f_silu, [(d, F32)], N_DEV * COND_ROWS)
    silu_rows, silu_pullback = jax.vjp(lambda r: silu_op((r[None],), (), (), (), ())[0][0], cond)
    mod_cols = ada_w.shape[2]
    mod_part = jnp.concatenate([_mm(silu_rows, ada_w[i], "nn", (F32,), f"ada{i}_fwd")[0] for i in range(2)], axis=0)
    (mod_all,) = _all_gather([mod_part], "gather_mod")
    mod_all = mod_all.reshape(N_DEV, 2, N_DEV * COND_ROWS, mod_cols)
    modrows = []
    for i in range(2):
        whole = mod_all[:, i].transpose(1, 0, 2).reshape(N_DEV * COND_ROWS, N_DEV * mod_cols) + ada_b[i]
        modrows.append(lax.dynamic_slice_in_dim(whole, me * COND_ROWS, COND_ROWS, axis=0)[:bsz + 1])

    weights_a = unpack_weights("a", gathered_a)
    gather_b, token_b = _push_start(mine["b"], mod_all, "gather_weights_b_start")
    gather_c, token_c = _push_start(mine["c"], token_b, "gather_weights_c_start")
    full = {n: local[n] for n in _REPLICATED}
    for n, axis in _SMALL_SHARDED:
        full[n] = _unshard(tiny_pack.piece(gathered_tiny, n, (N_DEV,)).reshape((N_DEV,) + local[n].shape), axis)

    xall = jnp.concatenate([ctx, x], axis=1)
    modrows0, modrows1 = modrows[0] + token_c[0, 0], modrows[1]
    w_a = dict(weights_a, norm1_g0=norm1_g[0][None], norm2_g0=norm2_g[0][None],
               **{n: full[n] for n in ("ev_q_norm_g", "ev_k_norm_g", "ev_sgu_norm_g", "ev_sgu_w", "ev_sgu_b")})
    (x1, h0), pull_a = jax.vjp(lambda x_, m0, w: _mixer0(x_, m0, w, bsz, length, ctx_len), xall, modrows0, w_a)
    w_b = dict(unpack_weights("b", _push_wait(gather_b, x1, mine["b"], me, "gather_weights_b_wait")), norm1_g1=norm1_g[1][None])
    (x2, h1), pull_b = jax.vjp(lambda x_, h_, m0, m1, w: _mlp0(x_, h_, m0, m1, w, bsz, length, ctx_len), x1, h0, modrows0, modrows1, w_b)
    w_c = dict(unpack_weights("c", _push_wait(gather_c, x2, mine["c"], me, "gather_weights_c_wait")), norm2_g1=norm2_g[1][None],
               **{n: full[n] for n in ("od_q_norm_g", "od_kv_norm_g", "od_conv_w", "od_conv_b", "od_ln_g", "od_ln_b")})
    x4, pull_c = jax.vjp(lambda x_, h_, m1, w: _layer1(x_, h_, m1, w, bsz, length, ctx_len), x2, h1, modrows1, w_c)
    loss_part, dx4, dfinal = _loss_head(x4, loss_target, final_g[None])
    loss = lax.psum(loss_part[0, 0], MESH_AXES)

    dx2, dh1, dmod1_c, g_c = pull_c(dx4)
    grads_c = pack_grads("c", g_c)
    exchange_c, token = _push_start(grads_c, jnp.reshape(loss, (1, 1)), "exchange_grads_c_start")
    dx1, dh0, dmod0_b, dmod1_b, g_b = pull_b((dx2, dh1 + token[0, 0].astype(dh1.dtype)))
    grads_b = pack_grads("b", g_b)
    exchange_b, token = _push_start(grads_b, dx1, "exchange_grads_b_start")
    dxall, dmod0_a, g_a = pull_a((dx1, dh0 + token[0, 0].astype(dh0.dtype)))
    grad_x = dxall[:, ctx_len:]
    dmodrows = [dmod0_a + dmod0_b, dmod1_b + dmod1_c]
    grads = {n: g[n] for g in (g_a, g_c) for n in g if n in full}
    grads["norm1_g"] = jnp.concatenate([g_a["norm1_g0"], g_b["norm1_g1"]], axis=0)
    grads["norm2_g"] = jnp.concatenate([g_a["norm2_g0"], g_c["norm2_g1"]], axis=0)
    grads["final_g"] = dfinal[0]

    dmod_local = jnp.concatenate([jnp.pad(dm, ((0, COND_ROWS - bsz - 1), (0, 0))) for dm in dmodrows], axis=0)
    (dmod_all,) = _all_gather([dmod_local], "gather_dmod")
    dmod_all = dmod_all.reshape(N_DEV, 2, COND_ROWS, N_DEV * mod_cols)
    reduced = {}
    grad_ada_w, grad_ada_b, dmod_mine = [], [], []
    for i in range(2):
        dmod = dmod_all[:, i].reshape(N_DEV * COND_ROWS, N_DEV * mod_cols)
        grad_ada_b.append(_sum_rows(dmod, f"ada{i}_db")[0])
        dmod_mine.append(lax.dynamic_slice_in_dim(dmod, me * mod_cols, mod_cols, axis=1))
        grad_ada_w.append(_mm(silu_rows, dmod_mine[i], "tn", (F32,), f"ada{i}_dw")[0])
    reduced["ada_w"], reduced["ada_b"] = jnp.stack(grad_ada_w), jnp.stack(grad_ada_b)
    dsilu = _mm(jnp.concatenate(dmod_mine, axis=1), jnp.concatenate([ada_w[0], ada_w[1]], axis=1), "nt", (F32,), "ada_dx")[0]
    (dcond,) = silu_pullback(dsilu)
    grads["c_ctx"] = _sum_rows(dcond.reshape(N_DEV, COND_ROWS, d)[:, bsz], "c_ctx_rows")[0]

    small_names = list(_REPLICATED) + [n for n, _ in _SMALL_SHARDED]
    small_pack = _Packing([(n, full[n].size) for n in small_names], 8)
    (small_all,) = _all_gather([small_pack.pack({n: grads[n].astype(F32) for n in small_names})], "gather_small_grads")
    small_sum = _sum_blocks(small_all, "sum_small_grads")
    grads_a = pack_grads("a", g_a)
    exchange_a, token = _push_start(grads_a, small_sum, "exchange_grads_a_start")
    small_sum = small_sum + token[0, 0]
    for n in _REPLICATED:
        reduced[n] = small_pack.piece(small_sum, n).reshape(local[n].shape)
    for n, axis in _SMALL_SHARDED:
        whole = small_pack.piece(small_sum, n).reshape(full[n].shape)
        reduced[n] = lax.dynamic_slice_in_dim(whole, me * local[n].shape[axis], local[n].shape[axis], axis=axis)

    def own_blocks(blocks):
        return [lax.dynamic_index_in_dim(t, me, 0, keepdims=False) for t in blocks]

    piece_grad = {}

    def reduce_stage(s, handle, blocks, after):
        received = _push_wait(handle, after, own_blocks(blocks), me, f"exchange_grads_{s}_wait")
        for p, got in zip(stages[s], received):
            summed = _sum_blocks(got, "sum_grads_" + p)[:block_rows[p]]
            piece_grad[p] = summed.T if by_columns[p] else summed

    def stacked(n):
        return jnp.stack([piece_grad[p] for p, wn, _, _, _ in _PIECES if wn == n])

    reduce_stage("c", exchange_c, grads_c, small_sum)
    reduce_stage("b", exchange_b, grads_b, small_sum)
    reduced["mlp_w1"], reduced["mlp_w2"] = stacked("mlp_w1"), stacked("mlp_w2")
    for n in ("od_w_in", "od_w_uq", "od_w_ukv"):
        reduced[n] = piece_grad[n][None]

    delta, new_m, new_v = {}, {}, {}
    with_stage_a = ("ev_w_in", "w_out")
    for n in [n for n in names if n not in with_stage_a]:
        delta[n], new_m[n], new_v[n] = _adamw(local[n], reduced[n], mom1[n], mom2[n], "adamw_" + n)
    reduce_stage("a", exchange_a, grads_a, new_v["mlp_w2"])
    reduced["w_out"], reduced["ev_w_in"] = stacked("w_out"), piece_grad["ev_w_in"][None]
    for n in with_stage_a:
        delta[n], new_m[n], new_v[n] = _adamw(local[n], reduced[n], mom1[n], mom2[n], "adamw_" + n)
    return (loss, grad_x, *[reduced[n] for n in names], *[delta[n] for n in names], *[new_m[n] for n in names], *[new_v[n] for n in names])
```

```python
import functools
import math

import jax
import jax.numpy as jnp
from jax import lax
from jax.experimental import pallas as pl
from jax.experimental.pallas import tpu as pltpu

F32, BF16 = jnp.float32, jnp.bfloat16

EPS = 1e-6
GRID_W = 64
ROPE_THETA = 10000.0
A_HEAD_DIM, A_Q_HEADS, A_KV_HEADS = 64, 8, 2
B_GROUPS, B_GROUP_DIM, B_CHUNK = 8, 64, 128
C_HEADS, C_NOPE, C_ROPE, C_V, C_Q_RANK, C_KV_RANK = 8, 64, 32, 64, 256, 128
D_CONV = 31
CONV_PAD = D_CONV // 2
N_MOD = 6
N_DEV = 8
MESH_AXES = ("x", "y", "c")

ADAM_LR, ADAM_B1, ADAM_B2, ADAM_EPS, ADAM_WD, ADAM_STEP = 0.001, 0.9, 0.999, 1e-08, 0.01, 10

VMEM_LIMIT = 56 * 1024 * 1024
PACK_COLS = 1024
ROW_ALIGN = 16
SUBLANES = 8
CONV_ROWS = 32
COND_ROWS = 8


def _pcall(body, **kw):
    return pl.pallas_call(body, **kw)


def _params(sem=None):
    return pltpu.CompilerParams(dimension_semantics=sem, vmem_limit_bytes=VMEM_LIMIT)


def _pick(n, cands):
    for c in cands:
        if n % c == 0:
            return c
    return n


def _mm(a, b, mode, out_dtypes, name, epi=None, extras=()):
    if mode == "tn":
        kk, m = a.shape
    else:
        m, kk = a.shape
    n = b.shape[0] if mode == "nt" else b.shape[1]
    tm = _pick(m, (1152, 1024, 896, 768, 512, 256, 128))
    tn = _pick(n, (1024, 896, 768, 512, 256, 128))
    tk = kk if kk <= 1024 else _pick(kk, (2048, 1536, 1024, 896, 768, 512, 256, 128))
    nk = kk // tk
    ne, no = len(extras), len(out_dtypes)
    a_spec = pl.BlockSpec((tk, tm), lambda i, j, k: (k, i)) if mode == "tn" else pl.BlockSpec((tm, tk), lambda i, j, k: (i, k))
    b_spec = pl.BlockSpec((tn, tk), lambda i, j, k: (j, k)) if mode == "nt" else pl.BlockSpec((tk, tn), lambda i, j, k: (k, j))
    t_spec = pl.BlockSpec((tm, tn), lambda i, j, k: (i, j))
    dn = {"nn": ((1,), (0,)), "nt": ((1,), (1,)), "tn": ((0,), (0,))}[mode]

    def body(a_ref, b_ref, *rest):
        extra_refs, out_refs = rest[:ne], rest[ne:ne + no]

        def finish(acc):
            outs = (acc,) if epi is None else epi(acc, *[r[...] for r in extra_refs])
            for r, o in zip(out_refs, outs):
                r[...] = o.astype(r.dtype)

        part = lax.dot_general(a_ref[...].astype(BF16), b_ref[...].astype(BF16), (dn, ((), ())), preferred_element_type=F32)
        if nk == 1:
            finish(part)
        else:
            acc_ref = rest[-1]
            k = pl.program_id(2)

            @pl.when(k == 0)
            def _():
                acc_ref[...] = part

            @pl.when(k > 0)
            def _():
                acc_ref[...] += part

            @pl.when(k == nk - 1)
            def _():
                finish(acc_ref[...])

    outs = _pcall(
        body, name=name, grid=(m // tm, n // tn, nk),
        in_specs=[a_spec, b_spec] + [t_spec] * ne,
        out_specs=[t_spec] * no,
        out_shape=[jax.ShapeDtypeStruct((m, n), d) for d in out_dtypes],
        scratch_shapes=[pltpu.VMEM((tm, tn), F32)] if nk > 1 else [],
        compiler_params=_params(("parallel", "parallel", "arbitrary")),
    )(a, b, *extras)
    return outs


def _linear(name, out_dtype=F32):
    @jax.custom_vjp
    def op(x, wt):
        return _mm(x, wt, "nt", (out_dtype,), name + "_fwd")[0]

    def fwd(x, wt):
        return op(x, wt), (x, wt)

    def bwd(res, dy):
        x, wt = res
        return _mm(dy, wt, "nn", (x.dtype,), name + "_dx")[0], _mm(dy, x, "tn", (wt.dtype,), name + "_dw")[0]

    op.defvjp(fwd, bwd)
    return op


_NT, _NN, _TN = (((1,), (1,)), ((), ())), (((1,), (0,)), ((), ())), (((0,), (0,)), ((), ()))
_ROW_TILES = (1152, 1024, 768, 512, 256, 128)


def _whole(w):
    return pl.BlockSpec(w.shape, lambda i: (0, 0))


def _groups_apply(x, ws, out_dtypes, name):
    n, (m, kk) = len(ws), x.shape
    tm = _pick(m, _ROW_TILES)

    def body(x_ref, *refs):
        a = x_ref[...].astype(BF16)
        for w_ref, o_ref in zip(refs[:n], refs[n:]):
            o_ref[...] = lax.dot_general(a, w_ref[...], _NT, preferred_element_type=F32).astype(o_ref.dtype)

    return _pcall(
        body, name=name, grid=(m // tm,),
        in_specs=[pl.BlockSpec((tm, kk), lambda i: (i, 0))] + [_whole(w) for w in ws],
        out_specs=[pl.BlockSpec((tm, w.shape[0]), lambda i: (i, 0)) for w in ws],
        out_shape=[jax.ShapeDtypeStruct((m, w.shape[0]), dt) for w, dt in zip(ws, out_dtypes)],
        compiler_params=_params(("parallel",)),
    )(x, *ws)


def _groups_sum(xs, ws, out_dtype, name):
    n, m, kk = len(ws), xs[0].shape[0], ws[0].shape[1]
    tm = _pick(m, _ROW_TILES)

    def body(*refs):
        acc = None
        for x_ref, w_ref in zip(refs[:n], refs[n:2 * n]):
            part = lax.dot_general(x_ref[...].astype(BF16), w_ref[...], _NN, preferred_element_type=F32)
            acc = part if acc is None else acc + part
        refs[2 * n][...] = acc.astype(out_dtype)

    return _pcall(
        body, name=name, grid=(m // tm,),
        in_specs=[pl.BlockSpec((tm, w.shape[0]), lambda i: (i, 0)) for w in ws] + [_whole(w) for w in ws],
        out_specs=pl.BlockSpec((tm, kk), lambda i: (i, 0)), out_shape=jax.ShapeDtypeStruct((m, kk), out_dtype),
        compiler_params=_params(("parallel",)),
    )(*xs, *ws)


def _groups_outer(xs, y, ws, name):
    n, (m, kk) = len(ws), y.shape
    tk = _pick(m, (768, 512, 256, 128))
    steps = m // tk

    def body(y_ref, *refs):
        x_refs, o_refs, acc_refs = refs[:n], refs[n:2 * n], refs[2 * n:]
        k = pl.program_id(0)
        b = y_ref[...].astype(BF16)
        for x_ref, o_ref, acc_ref in zip(x_refs, o_refs, acc_refs):
            part = lax.dot_general(x_ref[...].astype(BF16), b, _TN, preferred_element_type=F32)

            @pl.when(k == 0)
            def _(acc_ref=acc_ref, part=part):
                acc_ref[...] = part

            @pl.when(k > 0)
            def _(acc_ref=acc_ref, part=part):
                acc_ref[...] += part

            @pl.when(k == steps - 1)
            def _(acc_ref=acc_ref, o_ref=o_ref):
                o_ref[...] = acc_ref[...].astype(o_ref.dtype)

    return _pcall(
        body, name=name, grid=(steps,),
        in_specs=[pl.BlockSpec((tk, kk), lambda k: (k, 0))] + [pl.BlockSpec((tk, w.shape[0]), lambda k: (k, 0)) for w in ws],
        out_specs=[_whole(w) for w in ws], out_shape=[jax.ShapeDtypeStruct(w.shape, w.dtype) for w in ws],
        scratch_shapes=[pltpu.VMEM(w.shape, F32) for w in ws], compiler_params=_params(("arbitrary",)),
    )(y, *xs)


def _linear_multi(name, out_dtypes):
    @jax.custom_vjp
    def op(x, wts):
        return tuple(_groups_apply(x, wts, out_dtypes, name + "_fwd"))

    def fwd(x, wts):
        return op(x, wts), (x, wts)

    def bwd(res, dys):
        x, wts = res
        return _groups_sum(dys, wts, x.dtype, name + "_dx"), tuple(_groups_outer(dys, x, wts, name + "_dw"))

    op.defvjp(fwd, bwd)
    return op


def _linear_sum(name):
    @jax.custom_vjp
    def op(xs, ws):
        return _groups_sum(xs, ws, BF16, name + "_fwd")

    def fwd(xs, ws):
        return op(xs, ws), (xs, ws)

    def bwd(res, dy):
        xs, ws = res
        return tuple(_groups_apply(dy, ws, [x.dtype for x in xs], name + "_dx")), tuple(_groups_outer(xs, dy, ws, name + "_dw"))

    op.defvjp(fwd, bwd)
    return op


def _relu2_epi(acc):
    return jnp.square(jnp.maximum(acc, 0.0)), acc


def _relu2_bwd_epi(acc, a):
    return (acc * (2.0 * jnp.maximum(a.astype(F32), 0.0)),)


def _mlp(name):
    @jax.custom_vjp
    def op(h, w1t, w2):
        s, _ = _mm(h, w1t, "nt", (BF16, BF16), name + "_up", epi=_relu2_epi)
        return _mm(s, w2, "nn", (BF16,), name + "_down")[0]

    def fwd(h, w1t, w2):
        s, a = _mm(h, w1t, "nt", (BF16, BF16), name + "_up", epi=_relu2_epi)
        return _mm(s, w2, "nn", (BF16,), name + "_down")[0], (h, w1t, w2, s, a)

    def bwd(res, dy):
        h, w1t, w2, s, a = res
        da = _mm(dy, w2, "nt", (BF16,), name + "_ds", epi=_relu2_bwd_epi, extras=(a,))[0]
        dw2 = _mm(s, dy, "tn", (w2.dtype,), name + "_dw2")[0]
        dw1t = _mm(da, h, "tn", (w1t.dtype,), name + "_dw1")[0]
        dh = _mm(da, w1t, "nn", (h.dtype,), name + "_dh")[0]
        return dh, dw1t, dw2

    op.defvjp(fwd, bwd)
    return op


def _two_pass_dot(x, m):
    hi = x.astype(BF16)
    lo = (x - hi.astype(F32)).astype(BF16)
    mb = m.astype(BF16)
    return jnp.dot(hi, mb, preferred_element_type=F32) + jnp.dot(lo, mb, preferred_element_type=F32)


@jax.custom_vjp
def _sym_dot(x, m):
    return _two_pass_dot(x, m)


def _sym_dot_fwd(x, m):
    return _two_pass_dot(x, m), m


def _sym_dot_bwd(m, g):
    return _two_pass_dot(g, m), jnp.zeros_like(m)


_sym_dot.defvjp(_sym_dot_fwd, _sym_dot_bwd)


def _neighbour(x):
    lane = lax.broadcasted_iota(jnp.int32, x.shape, 1)
    return jnp.where(lane % 2 == 0, pltpu.roll(x, x.shape[1] - 1, 1), pltpu.roll(x, 1, 1))


@jax.custom_vjp
def _swap_pairs(x):
    return _neighbour(x)


_swap_pairs.defvjp(lambda x: (_neighbour(x), None), lambda _, g: (_neighbour(g),))


def _lane_group(x, k):
    return x[:, k * B_GROUP_DIM:(k + 1) * B_GROUP_DIM]


@jax.custom_vjp
def _group_mix(ws, v):
    return jnp.concatenate([jnp.dot(w.astype(BF16), _lane_group(v, k).astype(BF16), preferred_element_type=F32)
                            for k, w in enumerate(ws)], axis=1)


def _group_mix_fwd(ws, v):
    return _group_mix(ws, v), (ws, v)


def _group_mix_bwd(res, d):
    ws, v = res
    parts = [(_lane_group(d, k).astype(BF16), _lane_group(v, k).astype(BF16)) for k in range(len(ws))]
    dws = tuple(lax.dot_general(dk, vk, (((1,), (1,)), ((), ())), preferred_element_type=F32) for dk, vk in parts)
    dv = jnp.concatenate([lax.dot_general(w.astype(BF16), dk, (((0,), (0,)), ((), ())), preferred_element_type=F32)
                          for w, (dk, _) in zip(ws, parts)], axis=1)
    return dws, dv


_group_mix.defvjp(_group_mix_fwd, _group_mix_bwd)


def _rowwise(name, f, out_specs, tl, ctx_blocks=0):
    def seg(l, s):
        return jnp.where(l >= ctx_blocks, s - 1, 0) if s > 1 else 0

    def specs(rows, tabs, pers, glbs, consts):
        row_specs = [pl.BlockSpec((1, tl, r.shape[2]), lambda b, l: (b, l, 0)) for r in rows]
        tab_specs = [pl.BlockSpec((tl, t.shape[1]), lambda b, l: (l, 0)) for t in tabs]
        per_specs = [pl.BlockSpec((1, 1, 1, p.shape[3]), functools.partial(lambda b, l, s: (b, seg(l, s), 0, 0), s=p.shape[1])) for p in pers]
        glb_specs = [pl.BlockSpec(g.shape, functools.partial(lambda b, l, nd: (0,) * nd, nd=g.ndim)) for g in glbs]
        const_specs = [pl.BlockSpec(c.shape, functools.partial(lambda b, l, nd: (0,) * nd, nd=c.ndim)) for c in consts]
        return row_specs, tab_specs, per_specs, glb_specs, const_specs

    def load(refs_rows, refs_tabs, refs_pers, refs_glbs, refs_consts):
        return (tuple(r[0].astype(F32) for r in refs_rows), tuple(t[...] for t in refs_tabs),
                tuple(p[0, 0].astype(F32) for p in refs_pers), tuple(g[...].astype(F32) for g in refs_glbs),
                tuple(c[...] for c in refs_consts))

    def call_fwd(rows, tabs, pers, glbs, consts):
        bsz, length = rows[0].shape[:2]
        nr, nt, npp, ng, nc = len(rows), len(tabs), len(pers), len(glbs), len(consts)
        rs, ts, ps, gs, cs = specs(rows, tabs, pers, glbs, consts)

        def body(*refs):
            ins, outs = refs[:nr + nt + npp + ng + nc], refs[nr + nt + npp + ng + nc:]
            r, t, p, g, c = load(ins[:nr], ins[nr:nr + nt], ins[nr + nt:nr + nt + npp], ins[nr + nt + npp:nr + nt + npp + ng], ins[nr + nt + npp + ng:])
            for o_ref, o in zip(outs, f(r, t, p, g, c)):
                o_ref[0] = o.astype(o_ref.dtype)

        return _pcall(
            body, name=name + "_fwd", grid=(bsz, length // tl),
            in_specs=rs + ts + ps + gs + cs,
            out_specs=[pl.BlockSpec((1, tl, w), lambda b, l: (b, l, 0)) for w, _ in out_specs],
            out_shape=[jax.ShapeDtypeStruct((bsz, length, w), d) for w, d in out_specs],
            compiler_params=_params(("parallel", "parallel")),
        )(*rows, *tabs, *pers, *glbs, *consts)

    def call_bwd(rows, tabs, pers, glbs, consts, cts):
        bsz, length = rows[0].shape[:2]
        nr, nt, npp, ng, nc, no = len(rows), len(tabs), len(pers), len(glbs), len(consts), len(cts)
        rs, ts, ps, gs, cs = specs(rows, tabs, pers, glbs, consts)
        n_in = nr + nt + npp + ng + nc

        def body(*refs):
            ins, ct_refs, outs = refs[:n_in], refs[n_in:n_in + no], refs[n_in + no:]
            r, t, p, g, c = load(ins[:nr], ins[nr:nr + nt], ins[nr + nt:nr + nt + npp], ins[nr + nt + npp:nr + nt + npp + ng], ins[nr + nt + npp + ng:])
            _, vjp = jax.vjp(lambda r_, p_, g_: tuple(f(r_, t, p_, g_, c)), r, p, g)
            dr, dp, dg = vjp(tuple(ct[0].astype(F32) for ct in ct_refs))
            dr_refs, dp_refs, dg_refs = outs[:nr], outs[nr:nr + npp], outs[nr + npp:]
            for ref, d in zip(dr_refs, dr):
                ref[0] = d.astype(ref.dtype)
            b, l = pl.program_id(0), pl.program_id(1)
            first_of_segment = (l == 0) | (l == ctx_blocks)
            for ref, d in zip(dp_refs, dp):
                @pl.when(first_of_segment)
                def _(ref=ref, d=d):
                    ref[0, 0] = d

                @pl.when(jnp.logical_not(first_of_segment))
                def _(ref=ref, d=d):
                    ref[0, 0] += d
            first = (b == 0) & (l == 0)
            for ref, d in zip(dg_refs, dg):
                @pl.when(first)
                def _(ref=ref, d=d):
                    ref[...] = d

                @pl.when(jnp.logical_not(first))
                def _(ref=ref, d=d):
                    ref[...] += d

        ct_specs = [pl.BlockSpec((1, tl, w), lambda b, l: (b, l, 0)) for w, _ in out_specs]
        outs = _pcall(
            body, name=name + "_bwd", grid=(bsz, length // tl),
            in_specs=rs + ts + ps + gs + cs + ct_specs,
            out_specs=rs + ps + gs,
            out_shape=[jax.ShapeDtypeStruct(r.shape, r.dtype) for r in rows]
            + [jax.ShapeDtypeStruct(p.shape, F32) for p in pers] + [jax.ShapeDtypeStruct(g.shape, F32) for g in glbs],
            compiler_params=_params(("arbitrary", "arbitrary")),
        )(*rows, *tabs, *pers, *glbs, *consts, *cts)
        return tuple(outs[:nr]), tuple(outs[nr:nr + npp]), tuple(outs[nr + npp:])

    @jax.custom_vjp
    def op(rows, tabs, pers, glbs, consts):
        return tuple(call_fwd(rows, tabs, pers, glbs, consts))

    def fwd(rows, tabs, pers, glbs, consts):
        return op(rows, tabs, pers, glbs, consts), (rows, tabs, pers, glbs, consts)

    def bwd(res, cts):
        rows, tabs, pers, glbs, consts = res
        dr, dp, dg = call_bwd(rows, tabs, pers, glbs, consts, tuple(cts))
        dp = tuple(d.astype(p.dtype) for d, p in zip(dp, pers))
        dg = tuple(d.astype(g.dtype) for d, g in zip(dg, glbs))
        return dr, tuple(jnp.zeros_like(t) for t in tabs), dp, dg, tuple(jnp.zeros_like(c) for c in consts)

    op.defvjp(fwd, bwd)
    return op


def _rms(x, g):
    return x * lax.rsqrt(jnp.mean(x * x, axis=-1, keepdims=True) + EPS) * g


def _f_silu(r, t, p, g, c):
    return (jax.nn.silu(r[0]),)


def _f_modulate(r, t, p, g, c):
    shift, scale = p
    return (_rms(r[0], g[0]) * (1.0 + scale) + shift,)


def _f_res_modulate(r, t, p, g, c):
    x, y = r
    gate, shift, scale = p
    xn = x + gate * y
    return xn, _rms(xn, g[0]) * (1.0 + scale) + shift


def _f_res(r, t, p, g, c):
    return (r[0] + p[0] * r[1],)


def _f_headnorm_rope(r, t, p, g, c):
    x = r[0]
    cos, sin = t
    xn = x * lax.rsqrt(_sym_dot(x * x, c[0]) + EPS) * g[0]
    return (xn * cos + _swap_pairs(xn) * sin,)


def _f_rope(r, t, p, g, c):
    x = r[0]
    cos, sin = t
    return (x * cos + _swap_pairs(x) * sin,)


def _f_rms(r, t, p, g, c):
    return (_rms(r[0], g[0]),)


def _f_sgu_pre(r, t, p, g, c):
    u = jax.nn.gelu(r[0])
    v = jax.nn.gelu(r[1])
    vn = v * lax.rsqrt(_sym_dot(v * v, c[0]) + EPS) * g[0]
    return u, vn


def _f_sgu_mix(r, t, p, g, c):
    u, vn = r
    return (u * (g[B_GROUPS] + _group_mix(tuple(g[:B_GROUPS]), vn)),)


def _f_glu(r, t, p, g, c):
    return (r[0] * jax.nn.sigmoid(r[1]),)


def _f_ln_silu(r, t, p, g, c):
    x = r[0]
    mu = jnp.mean(x, axis=-1, keepdims=True)
    var = jnp.mean(jnp.square(x - mu), axis=-1, keepdims=True)
    return (jax.nn.silu((x - mu) * lax.rsqrt(var + EPS) * g[0] + g[1]),)


LOG2_E = 1.4426950408889634
LN_2 = 0.6931471805599453


def _attention_rows(name, ctx_len, heads, kv_heads):
    grp = heads // kv_heads

    def by_segment(qi, cb, lk, run):
        if cb > 0:
            @pl.when(qi < cb)
            def _():
                run(ctx_len)

            @pl.when(qi >= cb)
            def _():
                run(lk)
        else:
            run(lk)

    def head(ref, h, hd):
        return ref[0, :, h * hd:(h + 1) * hd]

    def scores(q_tile, kk, hd):
        qs = (q_tile.astype(F32) * (LOG2_E * hd ** -0.5)).astype(BF16)
        return qs, lax.dot_general(qs, kk, _NT, preferred_element_type=F32)

    def shapes(q):
        bsz, length, width = q.shape
        tq = math.gcd(_pick(length, (256, 128)), ctx_len) if ctx_len else _pick(length, (256, 128))
        return bsz, length, width // heads, tq, ctx_len // tq

    def call_fwd(q, k, v):
        bsz, length, hd, tq, cb = shapes(q)

        def body(q_ref, k_ref, v_ref, o_ref, lse_ref):
            def run(nk):
                k_all, v_all = k_ref[0, :nk], v_ref[0, :nk]
                outs = []
                for j in range(kv_heads):
                    kk = k_all[:, j * hd:(j + 1) * hd]
                    v_ones = jnp.concatenate([v_all[:, j * hd:(j + 1) * hd], jnp.ones((nk, hd), BF16)], axis=1)
                    for h in range(j * grp, (j + 1) * grp):
                        _, s = scores(head(q_ref, h, hd), kk, hd)
                        m = jnp.max(s, axis=-1, keepdims=True)
                        acc = jnp.dot(jnp.exp2(s - m).astype(BF16), v_ones, preferred_element_type=F32)
                        l = acc[:, hd:hd + 1]
                        outs.append((acc[:, :hd] / l).astype(BF16))
                        lse_ref[0, h] = m + jnp.log2(l)
                o_ref[0] = jnp.concatenate(outs, axis=1)

            by_segment(pl.program_id(1), cb, length, run)

        q_spec = pl.BlockSpec((1, tq, heads * hd), lambda b, i: (b, i, 0))
        kv_spec = pl.BlockSpec((1, length, kv_heads * hd), lambda b, i: (b, 0, 0))
        return _pcall(
            body, name=name + "_fwd", grid=(bsz, length // tq), in_specs=[q_spec, kv_spec, kv_spec],
            out_specs=[q_spec, pl.BlockSpec((1, heads, tq, 1), lambda b, i: (b, 0, i, 0))],
            out_shape=[jax.ShapeDtypeStruct(q.shape, BF16), jax.ShapeDtypeStruct((bsz, heads, length, 1), F32)],
            compiler_params=_params(("parallel", "parallel")),
        )(q, k, v)

    def call_bwd(q, k, v, o, lse, do):
        bsz, length, hd, tq, cb = shapes(q)
        nq = length // tq

        def body(q_ref, k_ref, v_ref, o_ref, lse_ref, do_ref, dq_ref, dk_ref, dv_ref, dk_acc, dv_acc):
            qi = pl.program_id(1)

            @pl.when(qi == 0)
            def _():
                dk_acc[...] = jnp.zeros_like(dk_acc)
                dv_acc[...] = jnp.zeros_like(dv_acc)

            def run(nk):
                k_all, v_all = k_ref[0, :nk], v_ref[0, :nk]
                dqs, dks, dvs = [], [], []
                for j in range(kv_heads):
                    kk, vv = k_all[:, j * hd:(j + 1) * hd], v_all[:, j * hd:(j + 1) * hd]
                    dk_sum = dv_sum = None
                    for h in range(j * grp, (j + 1) * grp):
                        dd = head(do_ref, h, hd)
                        qs, s = scores(head(q_ref, h, hd), kk, hd)
                        p = jnp.exp2(s - lse_ref[0, h])
                        delta = jnp.sum(dd.astype(F32) * head(o_ref, h, hd).astype(F32), axis=-1, keepdims=True)
                        t = (p * (lax.dot_general(dd, vv, _NT, preferred_element_type=F32) - delta)).astype(BF16)
                        dqs.append((jnp.dot(t, kk, preferred_element_type=F32) * hd ** -0.5).astype(BF16))
                        dk_h = lax.dot_general(t, qs, _TN, preferred_element_type=F32)
                        dv_h = lax.dot_general(p.astype(BF16), dd, _TN, preferred_element_type=F32)
                        dk_sum = dk_h if dk_sum is None else dk_sum + dk_h
                        dv_sum = dv_h if dv_sum is None else dv_sum + dv_h
                    dks.append(dk_sum)
                    dvs.append(dv_sum)
                dq_ref[0] = jnp.concatenate(dqs, axis=1)
                dk_acc[:nk] += jnp.concatenate(dks, axis=1)
                dv_acc[:nk] += jnp.concatenate(dvs, axis=1)

            by_segment(qi, cb, length, run)

            @pl.when(qi == nq - 1)
            def _():
                dk_ref[0] = (dk_acc[...] * LN_2).astype(dk_ref.dtype)
                dv_ref[0] = dv_acc[...].astype(dv_ref.dtype)

        q_spec = pl.BlockSpec((1, tq, heads * hd), lambda b, i: (b, i, 0))
        kv_spec = pl.BlockSpec((1, length, kv_heads * hd), lambda b, i: (b, 0, 0))
        lse_spec = pl.BlockSpec((1, heads, tq, 1), lambda b, i: (b, 0, i, 0))
        return _pcall(
            body, name=name + "_bwd", grid=(bsz, nq), in_specs=[q_spec, kv_spec, kv_spec, q_spec, lse_spec, q_spec],
            out_specs=[q_spec, kv_spec, kv_spec],
            out_shape=[jax.ShapeDtypeStruct(q.shape, BF16), jax.ShapeDtypeStruct(k.shape, BF16), jax.ShapeDtypeStruct(v.shape, BF16)],
            scratch_shapes=[pltpu.VMEM(k.shape[1:], F32), pltpu.VMEM(v.shape[1:], F32)],
            compiler_params=_params(("parallel", "arbitrary")),
        )(q, k, v, o, lse, do)

    @jax.custom_vjp
    def op(q, k, v):
        return call_fwd(q, k, v)[0]

    def fwd(q, k, v):
        o, lse = call_fwd(q, k, v)
        return o, (q, k, v, o, lse)

    def bwd(res, do):
        q, k, v, o, lse = res
        dq, dk, dv = call_bwd(q, k, v, o, lse, do)
        return dq, dk.astype(k.dtype), dv.astype(v.dtype)

    op.defvjp(fwd, bwd)
    return op


def _attention_latent_rows(name, heads, nope, rope, dv, heads_forward, heads_backward):
    lanes = 128
    scale = (nope + rope) ** -0.5

    def shapes(q, kn):
        bsz, lq, _ = q.shape
        return bsz, lq, kn.shape[1], _pick(lq, (256, 128))

    def scores(q_tile, kk):
        qs = (q_tile.astype(F32) * (LOG2_E * scale)).astype(BF16)
        return qs, lax.dot_general(qs, kk, _NT, preferred_element_type=F32)

    def call_fwd(q, kn, kr, v):
        bsz, lq, lk, tq = shapes(q, kn)
        heads_per_step = heads_forward

        def body(q_ref, kn_ref, kr_ref, v_ref, o_ref, lse_ref):
            real = lax.broadcasted_iota(jnp.int32, (lk, lanes), 1) < dv
            outs = []
            for h in range(heads_per_step):
                kk = kn_ref[0, :, h * lanes:(h + 1) * lanes] + kr_ref[0]
                v_ones = jnp.where(real, v_ref[0, :, h * lanes:(h + 1) * lanes], jnp.ones((lk, lanes), BF16))
                _, s = scores(q_ref[0, :, h * lanes:(h + 1) * lanes], kk)
                m = jnp.max(s, axis=-1, keepdims=True)
                acc = jnp.dot(jnp.exp2(s - m).astype(BF16), v_ones, preferred_element_type=F32)
                l = acc[:, dv:dv + 1]
                outs.append((acc[:, :dv] / l).astype(BF16))
                lse_ref[0, h] = m + jnp.log2(l)
            o_ref[0] = jnp.concatenate(outs, axis=1)

        wide = heads_per_step * lanes
        return _pcall(
            body, name=name + "_fwd", grid=(bsz, heads // heads_per_step, lq // tq),
            in_specs=[pl.BlockSpec((1, tq, wide), lambda b, g, i: (b, i, g)), pl.BlockSpec((1, lk, wide), lambda b, g, i: (b, 0, g)),
                      pl.BlockSpec((1, lk, lanes), lambda b, g, i: (b, 0, 0)), pl.BlockSpec((1, lk, wide), lambda b, g, i: (b, 0, g))],
            out_specs=[pl.BlockSpec((1, tq, heads_per_step * dv), lambda b, g, i: (b, i, g)),
                       pl.BlockSpec((1, heads_per_step, tq, 1), lambda b, g, i: (b, g, i, 0))],
            out_shape=[jax.ShapeDtypeStruct((bsz, lq, heads * dv), BF16), jax.ShapeDtypeStruct((bsz, heads, lq, 1), F32)],
            compiler_params=_params(("parallel", "parallel", "parallel")),
        )(q, kn, kr, v)

    def call_bwd(q, kn, kr, v, o, lse, do):
        bsz, lq, lk, tq = shapes(q, kn)
        heads_per_step = heads_backward

        def body(q_ref, kn_ref, kr_ref, v_ref, o_ref, lse_ref, do_ref, dq_ref, dkn_ref, dkr_ref, dv_ref, dkn_acc, dkr_acc, dv_acc):
            g, qi = pl.program_id(1), pl.program_id(2)
            last_q = qi == lq // tq - 1

            @pl.when(qi == 0)
            def _():
                dkn_acc[...] = jnp.zeros_like(dkn_acc)
                dv_acc[...] = jnp.zeros_like(dv_acc)

            @pl.when((qi == 0) & (g == 0))
            def _():
                dkr_acc[...] = jnp.zeros_like(dkr_acc)

            dqs, dks, dvs = [], [], []
            for h in range(heads_per_step):
                kk = kn_ref[0, :, h * lanes:(h + 1) * lanes] + kr_ref[0]
                vv = v_ref[0, :, h * lanes:h * lanes + dv]
                dd = do_ref[0, :, h * dv:(h + 1) * dv]
                qs, s = scores(q_ref[0, :, h * lanes:(h + 1) * lanes], kk)
                p = jnp.exp2(s - lse_ref[0, h])
                delta = jnp.sum(dd.astype(F32) * o_ref[0, :, h * dv:(h + 1) * dv].astype(F32), axis=-1, keepdims=True)
                t = (p * (lax.dot_general(dd, vv, _NT, preferred_element_type=F32) - delta)).astype(BF16)
                dqs.append((jnp.dot(t, kk, preferred_element_type=F32) * scale).astype(BF16))
                dks.append(lax.dot_general(t, qs, _TN, preferred_element_type=F32) * LN_2)
                dv_h = lax.dot_general(p.astype(BF16), dd, _TN, preferred_element_type=F32)
                dvs.append(jnp.concatenate([dv_h, jnp.zeros((lk, lanes - dv), F32)], axis=1))
            dq_ref[0] = jnp.concatenate(dqs, axis=1)
            dkn_acc[...] += jnp.concatenate(dks, axis=1)
            dv_acc[...] += jnp.concatenate(dvs, axis=1)
            shared = dks[0]
            for d_h in dks[1:]:
                shared = shared + d_h
            dkr_acc[...] += shared

            @pl.when(last_q)
            def _():
                dkn_ref[0] = dkn_acc[...].astype(dkn_ref.dtype)
                dv_ref[0] = dv_acc[...].astype(dv_ref.dtype)

            @pl.when(last_q & (g == heads // heads_per_step - 1))
            def _():
                dkr_ref[0] = dkr_acc[...].astype(dkr_ref.dtype)

        wide = heads_per_step * lanes
        q_spec = pl.BlockSpec((1, tq, wide), lambda b, g, i: (b, i, g))
        k_spec = pl.BlockSpec((1, lk, wide), lambda b, g, i: (b, 0, g))
        kr_spec = pl.BlockSpec((1, lk, lanes), lambda b, g, i: (b, 0, 0))
        o_spec = pl.BlockSpec((1, tq, heads_per_step * dv), lambda b, g, i: (b, i, g))
        lse_spec = pl.BlockSpec((1, heads_per_step, tq, 1), lambda b, g, i: (b, g, i, 0))
        return _pcall(
            body, name=name + "_bwd", grid=(bsz, heads // heads_per_step, lq // tq),
            in_specs=[q_spec, k_spec, kr_spec, k_spec, o_spec, lse_spec, o_spec],
            out_specs=[q_spec, k_spec, kr_spec, k_spec],
            out_shape=[jax.ShapeDtypeStruct(q.shape, BF16), jax.ShapeDtypeStruct(kn.shape, BF16), jax.ShapeDtypeStruct(kr.shape, BF16),
                       jax.ShapeDtypeStruct(v.shape, BF16)],
            scratch_shapes=[pltpu.VMEM((lk, wide), F32), pltpu.VMEM((lk, lanes), F32), pltpu.VMEM((lk, wide), F32)],
            compiler_params=_params(("parallel", "arbitrary", "arbitrary")),
        )(q, kn, kr, v, o, lse, do)

    @jax.custom_vjp
    def op(q, kn, kr, v):
        return call_fwd(q, kn, kr, v)[0]

    def fwd(q, kn, kr, v):
        o, lse = call_fwd(q, kn, kr, v)
        return o, (q, kn, kr, v, o, lse)

    def bwd(res, do):
        q, kn, kr, v, o, lse = res
        dq, dkn, dkr, dv_ = call_bwd(q, kn, kr, v, o, lse, do)
        return dq, dkn.astype(kn.dtype), dkr.astype(kr.dtype), dv_.astype(v.dtype)

    op.defvjp(fwd, bwd)
    return op


def _conv_call(ypad, taps, name):
    bsz, lp, ch = ypad.shape
    length = lp - 2 * ROW_ALIGN
    tl = _pick(length, (256, 128))

    def body(y_ref, w_ref, o_ref):
        base = pl.multiple_of(pl.program_id(1) * tl, tl)
        for r0 in range(0, tl, CONV_ROWS):
            win = y_ref[0, pl.ds(base + r0, CONV_ROWS + 2 * ROW_ALIGN), :]
            acc = jnp.broadcast_to(w_ref[pl.ds(D_CONV, 1), :], (CONV_ROWS, ch))
            for shift in range(SUBLANES):
                shifted = win[shift:shift + CONV_ROWS + 2 * ROW_ALIGN - SUBLANES, :]
                for k in range(shift, D_CONV, SUBLANES):
                    acc = acc + shifted[k - shift:k - shift + CONV_ROWS, :] * w_ref[pl.ds(k, 1), :]
            o_ref[0, pl.ds(r0, CONV_ROWS), :] = acc

    return _pcall(
        body, name=name, grid=(bsz, length // tl),
        in_specs=[pl.BlockSpec((1, lp, ch), lambda b, l: (b, 0, 0)), pl.BlockSpec((D_CONV + 1, ch), lambda b, l: (0, 0))],
        out_specs=pl.BlockSpec((1, tl, ch), lambda b, l: (b, l, 0)),
        out_shape=jax.ShapeDtypeStruct((bsz, length, ch), F32),
        compiler_params=_params(("parallel", "parallel")),
    )(ypad, taps)


def _conv_dw_call(ypad, dout, name):
    bsz, lp, ch = ypad.shape
    length = lp - 2 * ROW_ALIGN
    tl = _pick(length, (256, 128))

    def body(y_ref, d_ref, o_ref):
        b, l = pl.program_id(0), pl.program_id(1)

        @pl.when((b == 0) & (l == 0))
        def _():
            o_ref[...] = jnp.zeros_like(o_ref)

        base = pl.multiple_of(l * tl, tl)
        win = y_ref[0, pl.ds(base, tl + 2 * ROW_ALIGN), :]
        dd = d_ref[0]
        for shift in range(SUBLANES):
            shifted = win[shift:shift + tl + 2 * ROW_ALIGN - SUBLANES, :]
            for k in range(shift, D_CONV, SUBLANES):
                o_ref[pl.ds(k, 1), :] += jnp.sum(shifted[k - shift:k - shift + tl, :] * dd, axis=0, keepdims=True)
        o_ref[pl.ds(D_CONV, 1), :] += jnp.sum(dd, axis=0, keepdims=True)

    return _pcall(
        body, name=name, grid=(bsz, length // tl),
        in_specs=[pl.BlockSpec((1, lp, ch), lambda b, l: (b, 0, 0)), pl.BlockSpec((1, tl, ch), lambda b, l: (b, l, 0))],
        out_specs=pl.BlockSpec((D_CONV + 1, ch), lambda b, l: (0, 0)),
        out_shape=jax.ShapeDtypeStruct((D_CONV + 1, ch), F32),
        compiler_params=_params(("arbitrary", "arbitrary")),
    )(ypad, dout)


def _pad_rows(y):
    return jnp.pad(y, ((0, 0), (CONV_PAD, 2 * ROW_ALIGN - CONV_PAD), (0, 0)))


@jax.custom_vjp
def _dwconv(y, taps):
    return _conv_call(_pad_rows(y), taps, "conv_fwd")


def _dwconv_fwd(y, taps):
    return _dwconv(y, taps), (y, taps)


def _dwconv_bwd(res, dout):
    y, taps = res
    flipped = jnp.concatenate([taps[:D_CONV][::-1], jnp.zeros_like(taps[D_CONV:])], axis=0)
    dy = _conv_call(_pad_rows(dout), flipped, "conv_dy")
    dtaps = _conv_dw_call(_pad_rows(y), dout, "conv_dw")
    return dy, dtaps


_dwconv.defvjp(_dwconv_fwd, _dwconv_bwd)


def _loss_head(x, target, g):
    bsz, length, d = x.shape
    tl = _pick(length, (256, 128))

    def f(xb, tb, gb):
        err = _rms(xb, gb) - tb
        return 0.5 * jnp.sum(jnp.sum(err * err, axis=-1, keepdims=True), axis=0, keepdims=True) / d

    def body(x_ref, t_ref, g_ref, loss_ref, dx_ref, dg_ref):
        val, vjp = jax.vjp(lambda xb, gb: f(xb, t_ref[0], gb), x_ref[0], g_ref[...])
        dx, dg = vjp(jnp.ones((1, 1), F32))
        dx_ref[0] = dx
        first = (pl.program_id(0) == 0) & (pl.program_id(1) == 0)

        @pl.when(first)
        def _():
            loss_ref[...] = val
            dg_ref[...] = dg

        @pl.when(jnp.logical_not(first))
        def _():
            loss_ref[...] += val
            dg_ref[...] += dg

    row = pl.BlockSpec((1, tl, d), lambda b, l: (b, l, 0))
    return _pcall(
        body, name="loss_head", grid=(bsz, length // tl),
        in_specs=[row, row, pl.BlockSpec((1, d), lambda b, l: (0, 0))],
        out_specs=[pl.BlockSpec((1, 1), lambda b, l: (0, 0)), row, pl.BlockSpec((1, d), lambda b, l: (0, 0))],
        out_shape=[jax.ShapeDtypeStruct((1, 1), F32), jax.ShapeDtypeStruct(x.shape, F32), jax.ShapeDtypeStruct((1, d), F32)],
        compiler_params=_params(("arbitrary", "arbitrary")),
    )(x, target, g)


def _adamw(w, g, m, v, name):
    shape = w.shape
    cols = shape[-1]
    rows = w.size // cols
    tr = _pick(rows, (512, 256, 128))
    w2, g2, m2, v2 = (t.reshape(rows, cols) for t in (w, g, m, v))

    def body(w_ref, g_ref, m_ref, v_ref, d_ref, nm_ref, nv_ref):
        gg = g_ref[...]
        nm = ADAM_B1 * m_ref[...] + (1.0 - ADAM_B1) * gg
        nv = ADAM_B2 * v_ref[...] + (1.0 - ADAM_B2) * jnp.square(gg)
        m_hat = nm / (1.0 - ADAM_B1 ** ADAM_STEP)
        v_hat = nv / (1.0 - ADAM_B2 ** ADAM_STEP)
        d_ref[...] = -ADAM_LR * (m_hat / (jnp.sqrt(v_hat) + ADAM_EPS) + ADAM_WD * w_ref[...])
        nm_ref[...] = nm
        nv_ref[...] = nv

    spec = pl.BlockSpec((tr, cols), lambda i: (i, 0))
    outs = _pcall(
        body, name=name, grid=(rows // tr,), in_specs=[spec] * 4, out_specs=[spec] * 3,
        out_shape=[jax.ShapeDtypeStruct((rows, cols), F32)] * 3,
        compiler_params=_params(("parallel",)),
    )(w2, g2, m2, v2)
    return tuple(o.reshape(shape) for o in outs)


def _mesh_pos():
    return lax.axis_index("x"), lax.axis_index("y"), lax.axis_index("c")


_RELATIONS = [(dx, dy, dc) for dx in (0, 1) for dy in (0, 1) for dc in (0, 1)][1:]


def _peer(pos, rel):
    return tuple(jnp.where(r == 1, 1 - p, p) if r else p for p, r in zip(pos, rel))


def _block_index(pos):
    return 4 * pos[0] + 2 * pos[1] + pos[2]


_HBM = pl.BlockSpec(memory_space=pltpu.HBM)


def _all_gather(xs, name):
    n = len(xs)

    def body(*refs):
        x_refs, out_refs, (send_sems, recv_sems, local_sems) = refs[:n], refs[n:2 * n], refs[2 * n:]
        x_, y_, c_ = _mesh_pos()
        me, sibling = (x_, y_, c_), (x_, y_, 1 - c_)
        chips = [(1 - x_, y_), (x_, 1 - y_), (1 - x_, 1 - y_)]

        def copy(t, k, block, to, own=False):
            slot = out_refs[t].at[_block_index(block)]
            return pltpu.make_async_remote_copy(
                src_ref=x_refs[t] if own else slot, dst_ref=slot, send_sem=send_sems.at[7 * t + k], recv_sem=recv_sems.at[7 * t + k],
                device_id=to, device_id_type=pl.DeviceIdType.MESH)

        mine = [pltpu.make_async_copy(x_refs[t], out_refs[t].at[_block_index(me)], local_sems.at[t]) for t in range(n)]
        first = [[copy(t, 0, me, sibling, own=True)] + [copy(t, 1 + j, me, (*chip, c_), own=True) for j, chip in enumerate(chips)]
                 for t in range(n)]
        passed = [[copy(t, 4 + j, (*chip, c_), sibling) for j, chip in enumerate(chips)] for t in range(n)]
        for t in range(n):
            mine[t].start()
            for cp in first[t]:
                cp.start()
        for t in range(n):
            for j, chip in enumerate(chips):
                copy(t, 1 + j, (*chip, c_), me).wait_recv()
                passed[t][j].start()
        for t in range(n):
            copy(t, 0, sibling, me).wait_recv()
            for j, chip in enumerate(chips):
                copy(t, 4 + j, (*chip, 1 - c_), me).wait_recv()
            for cp in first[t] + passed[t]:
                cp.wait_send()
            mine[t].wait()

    return _pcall(
        body, name=name, in_specs=[_HBM] * n, out_specs=[_HBM] * n,
        out_shape=[jax.ShapeDtypeStruct((N_DEV,) + x.shape, x.dtype) for x in xs],
        scratch_shapes=[pltpu.SemaphoreType.DMA((7 * n,)), pltpu.SemaphoreType.DMA((7 * n,)), pltpu.SemaphoreType.DMA((n,))],
    )(*xs)


_SEM = pl.BlockSpec(memory_space=pltpu.SEMAPHORE)
_EFFECT = pltpu.SideEffectType.DATAFLOW_SIDE_EFFECTING


def _push_start(srcs, after, name):
    n = len(srcs)
    lands = [lax.empty((N_DEV,) + s.shape[-2:], s.dtype) for s in srcs]

    def body(*refs):
        src_refs, land_refs = refs[:n], refs[n:2 * n]
        send_sems, recv_sems, token = refs[2 * n + 1:3 * n + 1], refs[3 * n + 1:4 * n + 1], refs[-1]
        me = _mesh_pos()
        for t in range(n):
            for rel in _RELATIONS:
                peer = _peer(me, rel)
                pltpu.make_async_remote_copy(
                    src_ref=src_refs[t].at[_block_index(peer)] if srcs[t].ndim == 3 else src_refs[t], dst_ref=land_refs[t].at[_block_index(me)],
                    send_sem=send_sems[t], recv_sem=recv_sems[t], device_id=peer, device_id_type=pl.DeviceIdType.MESH).start()
        token[...] = jnp.zeros_like(token)

    outs = _pcall(
        body, name=name,
        out_shape=[pltpu.SemaphoreType.DMA(())] * (2 * n) + [pltpu.HBM(s.shape, s.dtype) for s in srcs]
        + [pltpu.HBM(l.shape, l.dtype) for l in lands] + [jax.ShapeDtypeStruct((8, 128), F32)],
        in_specs=[_HBM] * (2 * n) + [pl.BlockSpec(memory_space=pl.ANY)],
        out_specs=[_SEM] * (2 * n) + [_HBM] * (2 * n) + [pl.BlockSpec(memory_space=pltpu.VMEM)],
        input_output_aliases={i: 2 * n + i for i in range(2 * n)}, compiler_params=pltpu.CompilerParams(has_side_effects=_EFFECT),
    )(*[pltpu.with_memory_space_constraint(t, pltpu.HBM) for t in list(srcs) + lands], after)
    return (outs[:n], outs[n:2 * n], outs[2 * n:3 * n], outs[3 * n:4 * n]), outs[-1]


def _push_wait(handle, after, owns, me, name):
    send_sems, recv_sems, src_thrus, land_thrus = handle
    n = len(land_thrus)

    def body(*refs):
        land_refs, sends, recvs = refs[n:2 * n], refs[2 * n:3 * n], refs[3 * n:4 * n]
        for t in range(n):
            seven = land_refs[t].at[pl.ds(0, N_DEV - 1)]
            all_seven = pltpu.make_async_remote_copy(src_ref=seven, dst_ref=seven, send_sem=sends[t], recv_sem=recvs[t],
                                                     device_id=_mesh_pos(), device_id_type=pl.DeviceIdType.MESH)
            all_seven.wait_send()
            all_seven.wait_recv()

    outs = _pcall(
        body, name=name,
        out_shape=[pltpu.HBM(t.shape, t.dtype) for t in list(src_thrus) + list(land_thrus)],
        in_specs=[_HBM] * (2 * n) + [_SEM] * (2 * n) + [pl.BlockSpec(memory_space=pl.ANY)], out_specs=[_HBM] * (2 * n),
        input_output_aliases={i: i for i in range(2 * n)}, compiler_params=pltpu.CompilerParams(has_side_effects=_EFFECT),
    )(*src_thrus, *land_thrus, *send_sems, *recv_sems, after)
    return [lax.dynamic_update_slice(land, own[None], (me, 0, 0)) for land, own in zip(outs[n:], owns)]


def _sum_blocks(p, name):
    n, rows, cols = p.shape
    tr = _pick(rows, (256, 128, 64, 32, 16, 8))

    def body(p_ref, o_ref):
        acc = p_ref[0].astype(F32)
        for s in range(1, n):
            acc = acc + p_ref[s].astype(F32)
        o_ref[...] = acc

    return _pcall(
        body, name=name, grid=(rows // tr,),
        in_specs=[pl.BlockSpec((n, tr, cols), lambda i: (0, i, 0))], out_specs=pl.BlockSpec((tr, cols), lambda i: (i, 0)),
        out_shape=jax.ShapeDtypeStruct((rows, cols), F32), compiler_params=_params(("parallel",)),
    )(p)


def _sum_rows(t, name):
    def body(t_ref, o_ref):
        o_ref[...] = jnp.sum(t_ref[...], axis=0, keepdims=True)

    return _pcall(body, name=name, out_shape=jax.ShapeDtypeStruct((1, t.shape[1]), F32))(t)


class _Packing:
    def __init__(self, sizes, align):
        self.offsets, self.sizes, self.align = {}, dict(sizes), align
        row = 0
        for name, size in sizes:
            self.offsets[name] = row
            row += -(-size // (align * PACK_COLS)) * align
        self.rows = row

    def pack(self, pieces):
        return self.pack_blocks({n: pieces[n].reshape(1, -1) for n in self.sizes})[0]

    def pack_blocks(self, pieces):
        out = []
        for n, size in self.sizes.items():
            padded = -(-size // (self.align * PACK_COLS)) * self.align * PACK_COLS
            out.append(jnp.pad(pieces[n], ((0, 0), (0, padded - size))).reshape(pieces[n].shape[0], -1, PACK_COLS))
        return jnp.concatenate(out, axis=1)

    def piece(self, packed, name, lead=()):
        start, size = self.offsets[name], self.sizes[name]
        nrow = -(-size // (self.align * PACK_COLS)) * self.align
        sl = packed[..., start:start + nrow, :]
        return sl.reshape(lead + (nrow * PACK_COLS,))[..., :size]


_PIECES = (("ev_w_in", "ev_w_in", 0, 1, "a"), ("w_out0", "w_out", 0, 0, "a"),
           ("mlp_w1_0", "mlp_w1", 0, 1, "b"), ("mlp_w2_0", "mlp_w2", 0, 0, "b"),
           ("od_w_in", "od_w_in", 0, 1, "c"), ("od_w_uq", "od_w_uq", 0, 1, "c"), ("od_w_ukv", "od_w_ukv", 0, 1, "c"),
           ("w_out1", "w_out", 1, 0, "c"), ("mlp_w1_1", "mlp_w1", 1, 1, "c"), ("mlp_w2_1", "mlp_w2", 1, 0, "c"))
_SMALL_SHARDED = (("od_q_norm_g", 1), ("od_conv_w", 2), ("od_conv_b", 1), ("od_ln_g", 1), ("od_ln_b", 1))
_REPLICATED = ("c_ctx", "norm1_g", "norm2_g", "ev_q_norm_g", "ev_k_norm_g", "ev_sgu_norm_g", "ev_sgu_w", "ev_sgu_b",
               "od_kv_norm_g", "final_g")


def _unshard(blocks, axis):
    moved = jnp.moveaxis(blocks, 0, axis)
    shape = moved.shape
    return moved.reshape(shape[:axis] + (shape[axis] * shape[axis + 1],) + shape[axis + 2:])


def _group_mean_matrix(width, group):
    idx = jnp.arange(width) // group
    return (idx[:, None] == idx[None, :]).astype(F32) / group


def _angles(length, d_rot):
    rows = length // GRID_W
    row = jnp.broadcast_to(jnp.arange(rows)[:, None], (rows, GRID_W)).reshape(-1).astype(F32)
    col = jnp.broadcast_to(jnp.arange(GRID_W)[None, :], (rows, GRID_W)).reshape(-1).astype(F32)
    d_axis = d_rot // 2
    inv = ROPE_THETA ** (-jnp.arange(0, d_axis, 2, dtype=F32) / d_axis)
    return jnp.concatenate([row[:, None] * inv, col[:, None] * inv], axis=-1)


def _rope_tables(length, d_rot, head_dim, heads, ctx_len, tail=0):
    ang = _angles(length, d_rot)
    cos = jnp.repeat(jnp.cos(ang), 2, axis=1)
    sin = jnp.repeat(jnp.sin(ang), 2, axis=1) * jnp.tile(jnp.array([-1.0, 1.0], F32), d_rot // 2)
    keep = head_dim - d_rot - tail
    cos = jnp.concatenate([jnp.ones((length, keep), F32), cos, jnp.ones((length, tail), F32)], axis=1)
    sin = jnp.concatenate([jnp.zeros((length, keep), F32), sin, jnp.zeros((length, tail), F32)], axis=1)
    cos, sin = jnp.tile(cos, (1, heads)), jnp.tile(sin, (1, heads))
    cos = jnp.concatenate([jnp.ones((ctx_len, cos.shape[1]), F32), cos], axis=0)
    sin = jnp.concatenate([jnp.zeros((ctx_len, sin.shape[1]), F32), sin], axis=0)
    return cos, sin


def _segment_params(mod, bsz):
    parts = jnp.split(mod, N_MOD, axis=-1)
    out = []
    for part in parts:
        lat = part[:bsz]
        ctx = jnp.broadcast_to(part[bsz:bsz + 1], lat.shape)
        out.append(jnp.stack([ctx, lat], axis=1)[:, :, None, :])
    return out


def _flat(t):
    return t.reshape(-1, t.shape[-1])


def _sequence_rowwise(ctx_len):
    tl = math.gcd(256, ctx_len)

    def make(name, f, out_specs, rows_per_block=tl, ctx_blocks=ctx_len // tl):
        return _rowwise(name, f, out_specs, rows_per_block, ctx_blocks)

    return make


def _mixer0(xall, modrows0, w, bsz, length, ctx_len):
    d = xall.shape[-1]
    total = ctx_len + length
    rowwise, flat = _sequence_rowwise(ctx_len), _flat
    sh1, sc1, g1, sh2, sc2, _ = _segment_params(modrows0, bsz)
    (h,) = rowwise("mod0", _f_modulate, [(d, BF16)])((xall,), (), (sh1, sc1), (w["norm1_g0"],), ())
    ev_q, ev_kv = A_Q_HEADS * A_HEAD_DIM, A_KV_HEADS * A_HEAD_DIM
    half = B_GROUPS * B_GROUP_DIM
    groups = tuple(jnp.split(w["ev_w_in"], [ev_q, ev_q + ev_kv, ev_q + 2 * ev_kv, ev_q + 2 * ev_kv + half], axis=0))
    qp, kp, vp, zu, zv = [t.reshape(bsz, total, -1) for t in _linear_multi("ev_in", (F32, F32, BF16, F32, F32))(flat(h), groups)]
    cos_q, sin_q = _rope_tables(length, A_HEAD_DIM, A_HEAD_DIM, A_Q_HEADS, ctx_len)
    cos_k, sin_k = cos_q[:, :ev_kv], sin_q[:, :ev_kv]
    (q,) = rowwise("ev_q", _f_headnorm_rope, [(ev_q, BF16)])(
        (qp,), (cos_q, sin_q), (), (jnp.tile(w["ev_q_norm_g"][0], A_Q_HEADS)[None],), (_group_mean_matrix(ev_q, A_HEAD_DIM),))
    (k,) = rowwise("ev_k", _f_headnorm_rope, [(ev_kv, BF16)])(
        (kp,), (cos_k, sin_k), (), (jnp.tile(w["ev_k_norm_g"][0], A_KV_HEADS)[None],), (_group_mean_matrix(ev_kv, A_HEAD_DIM),))
    o_att = _attention_rows("gqa", ctx_len, A_Q_HEADS, A_KV_HEADS)(q, k, vp)
    u, vn = rowwise("sgu_pre", _f_sgu_pre, [(half, F32), (half, BF16)])(
        (zu, zv), (), (), (w["ev_sgu_norm_g"][0].reshape(1, half),), (_group_mean_matrix(half, B_GROUP_DIM),))
    bias = jnp.repeat(w["ev_sgu_b"][0].T, B_GROUP_DIM, axis=1)
    (o_sgu,) = rowwise("sgu_mix", _f_sgu_mix, [(half, BF16)], rows_per_block=B_CHUNK, ctx_blocks=0)(
        (u, vn), (), (), tuple(w["ev_sgu_w"][0][g] for g in range(B_GROUPS)) + (bias,), ())
    y = _linear_sum("out0")((flat(o_att), flat(o_sgu)), tuple(jnp.split(w["w_out0"], 2, axis=0))).reshape(bsz, total, d)
    x1, h = rowwise("res_mod0a", _f_res_modulate, [(d, F32), (d, BF16)])((xall, y), (), (g1, sh2, sc2), (w["norm2_g0"],), ())
    return x1, h


def _mlp0(x1, h, modrows0, modrows1, w, bsz, length, ctx_len):
    d = x1.shape[-1]
    total = ctx_len + length
    g2 = _segment_params(modrows0, bsz)[5]
    sh1, sc1 = _segment_params(modrows1, bsz)[:2]
    y = _mlp("mlp0")(_flat(h), w["mlp_w1_0"], w["mlp_w2_0"]).reshape(bsz, total, d)
    return _sequence_rowwise(ctx_len)("res_mod0b", _f_res_modulate, [(d, F32), (d, BF16)])((x1, y), (), (g2, sh1, sc1), (w["norm1_g1"],), ())


def _layer1(x2, h, modrows1, w, bsz, length, ctx_len):
    d = x2.shape[-1]
    total = ctx_len + length
    half = B_GROUPS * B_GROUP_DIM
    rowwise, flat = _sequence_rowwise(ctx_len), _flat
    _, _, g1n, sh2n, sc2n, g2n = _segment_params(modrows1, bsz)
    g_cq, g_ckv, g_kr, g_za, g_zg = jnp.split(w["od_w_in"], [C_Q_RANK, C_Q_RANK + C_KV_RANK, C_Q_RANK + C_KV_RANK + C_ROPE,
                                                             C_Q_RANK + C_KV_RANK + C_ROPE + half], axis=0)
    lanes, c_qk = 128, C_NOPE + C_ROPE
    g_kr = jnp.pad(g_kr, ((C_NOPE, lanes - c_qk), (0, 0)))
    cq, ckv, kr, za, zg = [t.reshape(bsz, total, -1) for t in _linear_multi("od_in", (F32,) * 5)(flat(h), (g_cq, g_ckv, g_kr, g_za, g_zg))]
    lat = slice(ctx_len, total)
    lat_tl = math.gcd(256, length)
    (cqn,) = _rowwise("od_qn", _f_rms, [(C_Q_RANK, BF16)], lat_tl)((cq[:, lat],), (), (), (w["od_q_norm_g"],), ())
    w_uq = jnp.pad(w["od_w_uq"].reshape(C_HEADS, c_qk, C_Q_RANK), ((0, 0), (0, lanes - c_qk), (0, 0))).reshape(C_HEADS * lanes, C_Q_RANK)
    qf = _linear("od_uq")(flat(cqn), w_uq).reshape(bsz, length, C_HEADS * lanes)
    cos_q, sin_q = _rope_tables(length, C_ROPE, lanes, C_HEADS, 0, tail=lanes - c_qk)
    (q,) = _rowwise("od_qrope", _f_rope, [(C_HEADS * lanes, BF16)], lat_tl)((qf,), (cos_q, sin_q), (), (), ())
    (ckvn,) = rowwise("od_kvn", _f_rms, [(C_KV_RANK, BF16)])((ckv,), (), (), (w["od_kv_norm_g"],), ())
    per_head = w["od_w_ukv"].reshape(C_HEADS, C_NOPE + C_V, C_KV_RANK)
    w_kn = jnp.pad(per_head[:, :C_NOPE], ((0, 0), (0, lanes - C_NOPE), (0, 0))).reshape(C_HEADS * lanes, C_KV_RANK)
    w_v = jnp.pad(per_head[:, C_NOPE:], ((0, 0), (0, lanes - C_V), (0, 0))).reshape(C_HEADS * lanes, C_KV_RANK)
    kn, vv = [t.reshape(bsz, total, -1) for t in _linear_multi("od_ukv", (BF16, BF16))(flat(ckvn), (w_kn, w_v))]
    cos_r, sin_r = _rope_tables(length, C_ROPE, lanes, 1, ctx_len, tail=lanes - c_qk)
    (krr,) = rowwise("od_krope", _f_rope, [(lanes, BF16)])((kr,), (cos_r, sin_r), (), (), ())
    o_att = _attention_latent_rows("mla", C_HEADS, C_NOPE, C_ROPE, C_V, 4, 2)(q, kn, krr, vv)
    (glu,) = _rowwise("glu", _f_glu, [(half, F32)], lat_tl)((za[:, lat], zg[:, lat]), (), (), (), ())
    taps = jnp.concatenate([w["od_conv_w"][0], w["od_conv_b"]], axis=0)
    conv = _dwconv(glu, taps)
    (o_conv,) = _rowwise("ln_silu", _f_ln_silu, [(half, BF16)], lat_tl)((conv,), (), (), (w["od_ln_g"], w["od_ln_b"]), ())
    y = _linear_sum("out1")((flat(o_att), flat(o_conv)), tuple(jnp.split(w["w_out1"], 2, axis=0))).reshape(bsz, length, d)
    lat_param = lambda p: p[:, 1:]
    x3, h = _rowwise("res_mod1a", _f_res_modulate, [(d, F32), (d, BF16)], lat_tl)(
        (x2[:, lat], y), (), (lat_param(g1n), lat_param(sh2n), lat_param(sc2n)), (w["norm2_g1"],), ())
    y = _mlp("mlp1")(flat(h), w["mlp_w1_1"], w["mlp_w2_1"]).reshape(bsz, length, d)
    (x4,) = _rowwise("res1b", _f_res, [(d, F32)], lat_tl)((x3, y), (), (lat_param(g2n),), (), ())
    return x4


def kernel(x, c, ctx, c_ctx, ada_w, ada_b, norm1_g, norm2_g, w_out, mlp_w1, mlp_w2, ev_w_in, ev_q_norm_g, ev_k_norm_g, ev_sgu_norm_g, ev_sgu_w, ev_sgu_b, od_w_in, od_q_norm_g, od_kv_norm_g, od_w_uq, od_w_ukv, od_conv_w, od_conv_b, od_ln_g, od_ln_b, final_g, loss_target, m_c_ctx, m_ada_w, m_ada_b, m_norm1_g, m_norm2_g, m_w_out, m_mlp_w1, m_mlp_w2, m_ev_w_in, m_ev_q_norm_g, m_ev_k_norm_g, m_ev_sgu_norm_g, m_ev_sgu_w, m_ev_sgu_b, m_od_w_in, m_od_q_norm_g, m_od_kv_norm_g, m_od_w_uq, m_od_w_ukv, m_od_conv_w, m_od_conv_b, m_od_ln_g, m_od_ln_b, m_final_g, v_c_ctx, v_ada_w, v_ada_b, v_norm1_g, v_norm2_g, v_w_out, v_mlp_w1, v_mlp_w2, v_ev_w_in, v_ev_q_norm_g, v_ev_k_norm_g, v_ev_sgu_norm_g, v_ev_sgu_w, v_ev_sgu_b, v_od_w_in, v_od_q_norm_g, v_od_kv_norm_g, v_od_w_uq, v_od_w_ukv, v_od_conv_w, v_od_conv_b, v_od_ln_g, v_od_ln_b, v_final_g):
    names = ["c_ctx", "ada_w", "ada_b", "norm1_g", "norm2_g", "w_out", "mlp_w1", "mlp_w2", "ev_w_in", "ev_q_norm_g", "ev_k_norm_g",
             "ev_sgu_norm_g", "ev_sgu_w", "ev_sgu_b", "od_w_in", "od_q_norm_g", "od_kv_norm_g", "od_w_uq", "od_w_ukv", "od_conv_w",
             "od_conv_b", "od_ln_g", "od_ln_b", "final_g"]
    local = dict(zip(names, [c_ctx, ada_w, ada_b, norm1_g, norm2_g, w_out, mlp_w1, mlp_w2, ev_w_in, ev_q_norm_g, ev_k_norm_g, ev_sgu_norm_g, ev_sgu_w, ev_sgu_b, od_w_in, od_q_norm_g, od_kv_norm_g, od_w_uq, od_w_ukv, od_conv_w, od_conv_b, od_ln_g, od_ln_b, final_g]))
    mom1 = dict(zip(names, [m_c_ctx, m_ada_w, m_ada_b, m_norm1_g, m_norm2_g, m_w_out, m_mlp_w1, m_mlp_w2, m_ev_w_in, m_ev_q_norm_g, m_ev_k_norm_g, m_ev_sgu_norm_g, m_ev_sgu_w, m_ev_sgu_b, m_od_w_in, m_od_q_norm_g, m_od_kv_norm_g, m_od_w_uq, m_od_w_ukv, m_od_conv_w, m_od_conv_b, m_od_ln_g, m_od_ln_b, m_final_g]))
    mom2 = dict(zip(names, [v_c_ctx, v_ada_w, v_ada_b, v_norm1_g, v_norm2_g, v_w_out, v_mlp_w1, v_mlp_w2, v_ev_w_in, v_ev_q_norm_g, v_ev_k_norm_g, v_ev_sgu_norm_g, v_ev_sgu_w, v_ev_sgu_b, v_od_w_in, v_od_q_norm_g, v_od_kv_norm_g, v_od_w_uq, v_od_w_ukv, v_od_conv_w, v_od_conv_b, v_od_ln_g, v_od_ln_b, v_final_g]))
    bsz, length, d = x.shape
    ctx_len = ctx.shape[1]
    me = _block_index(_mesh_pos())

    shard = {p: local[wn][layer] for p, wn, layer, _, _ in _PIECES}
    by_columns = {p: axis == 1 for p, _, _, axis, _ in _PIECES}
    stages = {s: [p for p, _, _, _, st in _PIECES if st == s] for s in "abc"}
    block_rows = {p: shard[p].shape[1] if by_columns[p] else shard[p].shape[0] for p in shard}

    def pad_block_rows(t, p):
        extra = -block_rows[p] % ROW_ALIGN
        return jnp.pad(t, [(0, 0)] * (t.ndim - 2) + [(0, extra), (0, 0)]) if extra else t

    def travelling(p):
        t = shard[p].astype(BF16)
        return pad_block_rows(t.T if by_columns[p] else t, p)

    mine = {s: [travelling(p) for p in stages[s]] for s in "abc"}
    tiny_pack = _Packing([(n, local[n].size) for n, _ in _SMALL_SHARDED], 8)

    def unpack_weights(s, gathered):
        return {p: g[:, :block_rows[p]].reshape(N_DEV * block_rows[p], g.shape[2]) for p, g in zip(stages[s], gathered)}

    def pack_grads(s, g):
        return [pad_block_rows(g[p].reshape(N_DEV, block_rows[p], g[p].shape[1]), p) for p in stages[s]]

    cond_local = jnp.concatenate([c, c_ctx[None], jnp.zeros((COND_ROWS - bsz - 1, d), F32)], axis=0)
    cond, gathered_tiny, *gathered_a = _all_gather(
        [cond_local, tiny_pack.pack({n: local[n] for n, _ in _SMALL_SHARDED})] + mine["a"], "gather_inputs")
    cond = cond.reshape(N_DEV * COND_ROWS, d)
    silu_op = _rowwise("silu", _f_silu, [(d, F32)], N_DEV * COND_ROWS)
    silu_rows, silu_pullback = jax.vjp(lambda r: silu_op((r[None],), (), (), (), ())[0][0], cond)
    mod_cols = ada_w.shape[2]
    mod_part = jnp.concatenate([_mm(silu_rows, ada_w[i], "nn", (F32,), f"ada{i}_fwd")[0] for i in range(2)], axis=0)
    (mod_all,) = _all_gather([mod_part], "gather_mod")
    mod_all = mod_all.reshape(N_DEV, 2, N_DEV * COND_ROWS, mod_cols)
    modrows = []
    for i in range(2):
        whole = mod_all[:, i].transpose(1, 0, 2).reshape(N_DEV * COND_ROWS, N_DEV * mod_cols) + ada_b[i]
        modrows.append(lax.dynamic_slice_in_dim(whole, me * COND_ROWS, COND_ROWS, axis=0)[:bsz + 1])

    weights_a = unpack_weights("a", gathered_a)
    gather_b, token_b = _push_start(mine["b"], mod_all, "gather_weights_b_start")
    gather_c, token_c = _push_start(mine["c"], token_b, "gather_weights_c_start")
    full = {n: local[n] for n in _REPLICATED}
    for n, axis in _SMALL_SHARDED:
        full[n] = _unshard(tiny_pack.piece(gathered_tiny, n, (N_DEV,)).reshape((N_DEV,) + local[n].shape), axis)

    xall = jnp.concatenate([ctx, x], axis=1)
    modrows0, modrows1 = modrows[0] + token_c[0, 0], modrows[1]
    w_a = dict(weights_a, norm1_g0=norm1_g[0][None], norm2_g0=norm2_g[0][None],
               **{n: full[n] for n in ("ev_q_norm_g", "ev_k_norm_g", "ev_sgu_norm_g", "ev_sgu_w", "ev_sgu_b")})
    (x1, h0), pull_a = jax.vjp(lambda x_, m0, w: _mixer0(x_, m0, w, bsz, length, ctx_len), xall, modrows0, w_a)
    w_b = dict(unpack_weights("b", _push_wait(gather_b, x1, mine["b"], me, "gather_weights_b_wait")), norm1_g1=norm1_g[1][None])
    (x2, h1), pull_b = jax.vjp(lambda x_, h_, m0, m1, w: _mlp0(x_, h_, m0, m1, w, bsz, length, ctx_len), x1, h0, modrows0, modrows1, w_b)
    w_c = dict(unpack_weights("c", _push_wait(gather_c, x2, mine["c"], me, "gather_weights_c_wait")), norm2_g1=norm2_g[1][None],
               **{n: full[n] for n in ("od_q_norm_g", "od_kv_norm_g", "od_conv_w", "od_conv_b", "od_ln_g", "od_ln_b")})
    x4, pull_c = jax.vjp(lambda x_, h_, m1, w: _layer1(x_, h_, m1, w, bsz, length, ctx_len), x2, h1, modrows1, w_c)
    loss_part, dx4, dfinal = _loss_head(x4, loss_target, final_g[None])
    loss = lax.psum(loss_part[0, 0], MESH_AXES)

    dx2, dh1, dmod1_c, g_c = pull_c(dx4)
    grads_c = pack_grads("c", g_c)
    exchange_c, token = _push_start(grads_c, dx2, "exchange_grads_c_start")
    dx1, dh0, dmod0_b, dmod1_b, g_b = pull_b((dx2, dh1 + token[0, 0].astype(dh1.dtype)))
    grads_b = pack_grads("b", g_b)
    exchange_b, token = _push_start(grads_b, dx1, "exchange_grads_b_start")
    dxall, dmod0_a, g_a = pull_a((dx1, dh0 + token[0, 0].astype(dh0.dtype)))
    grad_x = dxall[:, ctx_len:]
    dmodrows = [dmod0_a + dmod0_b, dmod1_b + dmod1_c]
    grads = {n: g[n] for g in (g_a, g_c) for n in g if n in full}
    grads["norm1_g"] = jnp.concatenate([g_a["norm1_g0"], g_b["norm1_g1"]], axis=0)
    grads["norm2_g"] = jnp.concatenate([g_a["norm2_g0"], g_c["norm2_g1"]], axis=0)
    grads["final_g"] = dfinal[0]

    dmod_local = jnp.concatenate([jnp.pad(dm, ((0, COND_ROWS - bsz - 1), (0, 0))) for dm in dmodrows], axis=0)
    (dmod_all,) = _all_gather([dmod_local], "gather_dmod")
    dmod_all = dmod_all.reshape(N_DEV, 2, COND_ROWS, N_DEV * mod_cols)
    grads_a = pack_grads("a", g_a)
    exchange_a, token = _push_start(grads_a, dmod_all, "exchange_grads_a_start")
    dmod_all = dmod_all + token[0, 0]
    reduced = {}
    grad_ada_w, grad_ada_b, dmod_mine = [], [], []
    for i in range(2):
        dmod = dmod_all[:, i].reshape(N_DEV * COND_ROWS, N_DEV * mod_cols)
        grad_ada_b.append(_sum_rows(dmod, f"ada{i}_db")[0])
        dmod_mine.append(lax.dynamic_slice_in_dim(dmod, me * mod_cols, mod_cols, axis=1))
        grad_ada_w.append(_mm(silu_rows, dmod_mine[i], "tn", (F32,), f"ada{i}_dw")[0])
    reduced["ada_w"], reduced["ada_b"] = jnp.stack(grad_ada_w), jnp.stack(grad_ada_b)
    dsilu = _mm(jnp.concatenate(dmod_mine, axis=1), jnp.concatenate([ada_w[0], ada_w[1]], axis=1), "nt", (F32,), "ada_dx")[0]
    (dcond,) = silu_pullback(dsilu)
    grads["c_ctx"] = _sum_rows(dcond.reshape(N_DEV, COND_ROWS, d)[:, bsz], "c_ctx_rows")[0]

    small_names = list(_REPLICATED) + [n for n, _ in _SMALL_SHARDED]
    small_pack = _Packing([(n, full[n].size) for n in small_names], 8)
    (small_all,) = _all_gather([small_pack.pack({n: grads[n].astype(F32) for n in small_names})], "gather_small_grads")
    small_sum = _sum_blocks(small_all, "sum_small_grads")
    for n in _REPLICATED:
        reduced[n] = small_pack.piece(small_sum, n).reshape(local[n].shape)
    for n, axis in _SMALL_SHARDED:
        whole = small_pack.piece(small_sum, n).reshape(full[n].shape)
        reduced[n] = lax.dynamic_slice_in_dim(whole, me * local[n].shape[axis], local[n].shape[axis], axis=axis)

    def own_blocks(blocks):
        return [lax.dynamic_index_in_dim(t, me, 0, keepdims=False) for t in blocks]

    received = {s: _push_wait(handle, small_sum, own_blocks(blocks), me, f"exchange_grads_{s}_wait")
                for s, handle, blocks in (("c", exchange_c, grads_c), ("b", exchange_b, grads_b), ("a", exchange_a, grads_a))}
    piece_grad = {}
    for s in "abc":
        for p, blocks in zip(stages[s], received[s]):
            summed = _sum_blocks(blocks, "sum_grads_" + p)[:block_rows[p]]
            piece_grad[p] = summed.T if by_columns[p] else summed
    for n in ("w_out", "mlp_w1", "mlp_w2"):
        reduced[n] = jnp.stack([piece_grad[p] for p, wn, _, _, _ in _PIECES if wn == n])
    for n in ("ev_w_in", "od_w_in", "od_w_uq", "od_w_ukv"):
        reduced[n] = piece_grad[n][None]

    delta, new_m, new_v = {}, {}, {}
    for n in names:
        delta[n], new_m[n], new_v[n] = _adamw(local[n], reduced[n], mom1[n], mom2[n], "adamw_" + n)
    return (loss, grad_x, *[reduced[n] for n in names], *[delta[n] for n in names], *[new_m[n] for n in names], *[new_v[n] for n in names])
```

```python
import functools
import math

import jax
import jax.numpy as jnp
from jax import lax
from jax.experimental import pallas as pl
from jax.experimental.pallas import tpu as pltpu

F32, BF16 = jnp.float32, jnp.bfloat16

EPS = 1e-6
GRID_W = 64
ROPE_THETA = 10000.0
A_HEAD_DIM, A_Q_HEADS, A_KV_HEADS = 64, 8, 2
B_GROUPS, B_GROUP_DIM, B_CHUNK = 8, 64, 128
C_HEADS, C_NOPE, C_ROPE, C_V, C_Q_RANK, C_KV_RANK = 8, 64, 32, 64, 256, 128
D_CONV = 31
CONV_PAD = D_CONV // 2
N_MOD = 6
N_DEV = 8
MESH_AXES = ("x", "y", "c")

ADAM_LR, ADAM_B1, ADAM_B2, ADAM_EPS, ADAM_WD, ADAM_STEP = 0.001, 0.9, 0.999, 1e-08, 0.01, 10

VMEM_LIMIT = 56 * 1024 * 1024
PACK_COLS = 1024
ROW_ALIGN = 16
SUBLANES = 8
CONV_ROWS = 32
COND_ROWS = 8


def _pcall(body, **kw):
    return pl.pallas_call(body, **kw)


def _params(sem=None):
    return pltpu.CompilerParams(dimension_semantics=sem, vmem_limit_bytes=VMEM_LIMIT)


def _pick(n, cands):
    for c in cands:
        if n % c == 0:
            return c
    return n


def _mm(a, b, mode, out_dtypes, name, epi=None, extras=()):
    if mode == "tn":
        kk, m = a.shape
    else:
        m, kk = a.shape
    n = b.shape[0] if mode == "nt" else b.shape[1]
    tm = _pick(m, (1152, 1024, 896, 768, 512, 256, 128))
    tn = _pick(n, (1024, 896, 768, 512, 256, 128))
    tk = kk if kk <= 1024 else _pick(kk, (2048, 1536, 1024, 896, 768, 512, 256, 128))
    nk = kk // tk
    ne, no = len(extras), len(out_dtypes)
    a_spec = pl.BlockSpec((tk, tm), lambda i, j, k: (k, i)) if mode == "tn" else pl.BlockSpec((tm, tk), lambda i, j, k: (i, k))
    b_spec = pl.BlockSpec((tn, tk), lambda i, j, k: (j, k)) if mode == "nt" else pl.BlockSpec((tk, tn), lambda i, j, k: (k, j))
    t_spec = pl.BlockSpec((tm, tn), lambda i, j, k: (i, j))
    dn = {"nn": ((1,), (0,)), "nt": ((1,), (1,)), "tn": ((0,), (0,))}[mode]

    def body(a_ref, b_ref, *rest):
        extra_refs, out_refs = rest[:ne], rest[ne:ne + no]

        def finish(acc):
            outs = (acc,) if epi is None else epi(acc, *[r[...] for r in extra_refs])
            for r, o in zip(out_refs, outs):
                r[...] = o.astype(r.dtype)

        part = lax.dot_general(a_ref[...].astype(BF16), b_ref[...].astype(BF16), (dn, ((), ())), preferred_element_type=F32)
        if nk == 1:
            finish(part)
        else:
            acc_ref = rest[-1]
            k = pl.program_id(2)

            @pl.when(k == 0)
            def _():
                acc_ref[...] = part

            @pl.when(k > 0)
            def _():
                acc_ref[...] += part

            @pl.when(k == nk - 1)
            def _():
                finish(acc_ref[...])

    outs = _pcall(
        body, name=name, grid=(m // tm, n // tn, nk),
        in_specs=[a_spec, b_spec] + [t_spec] * ne,
        out_specs=[t_spec] * no,
        out_shape=[jax.ShapeDtypeStruct((m, n), d) for d in out_dtypes],
        scratch_shapes=[pltpu.VMEM((tm, tn), F32)] if nk > 1 else [],
        compiler_params=_params(("parallel", "parallel", "arbitrary")),
    )(a, b, *extras)
    return outs


def _linear(name, out_dtype=F32):
    @jax.custom_vjp
    def op(x, wt):
        return _mm(x, wt, "nt", (out_dtype,), name + "_fwd")[0]

    def fwd(x, wt):
        return op(x, wt), (x, wt)

    def bwd(res, dy):
        x, wt = res
        return _mm(dy, wt, "nn", (x.dtype,), name + "_dx")[0], _mm(dy, x, "tn", (wt.dtype,), name + "_dw")[0]

    op.defvjp(fwd, bwd)
    return op


_NT, _NN, _TN = (((1,), (1,)), ((), ())), (((1,), (0,)), ((), ())), (((0,), (0,)), ((), ()))
_ROW_TILES = (1152, 1024, 768, 512, 256, 128)


def _whole(w):
    return pl.BlockSpec(w.shape, lambda i: (0, 0))


def _groups_apply(x, ws, out_dtypes, name):
    n, (m, kk) = len(ws), x.shape
    tm = _pick(m, _ROW_TILES)

    def body(x_ref, *refs):
        a = x_ref[...].astype(BF16)
        for w_ref, o_ref in zip(refs[:n], refs[n:]):
            o_ref[...] = lax.dot_general(a, w_ref[...], _NT, preferred_element_type=F32).astype(o_ref.dtype)

    return _pcall(
        body, name=name, grid=(m // tm,),
        in_specs=[pl.BlockSpec((tm, kk), lambda i: (i, 0))] + [_whole(w) for w in ws],
        out_specs=[pl.BlockSpec((tm, w.shape[0]), lambda i: (i, 0)) for w in ws],
        out_shape=[jax.ShapeDtypeStruct((m, w.shape[0]), dt) for w, dt in zip(ws, out_dtypes)],
        compiler_params=_params(("parallel",)),
    )(x, *ws)


def _groups_sum(xs, ws, out_dtype, name):
    n, m, kk = len(ws), xs[0].shape[0], ws[0].shape[1]
    tm = _pick(m, _ROW_TILES)

    def body(*refs):
        acc = None
        for x_ref, w_ref in zip(refs[:n], refs[n:2 * n]):
            part = lax.dot_general(x_ref[...].astype(BF16), w_ref[...], _NN, preferred_element_type=F32)
            acc = part if acc is None else acc + part
        refs[2 * n][...] = acc.astype(out_dtype)

    return _pcall(
        body, name=name, grid=(m // tm,),
        in_specs=[pl.BlockSpec((tm, w.shape[0]), lambda i: (i, 0)) for w in ws] + [_whole(w) for w in ws],
        out_specs=pl.BlockSpec((tm, kk), lambda i: (i, 0)), out_shape=jax.ShapeDtypeStruct((m, kk), out_dtype),
        compiler_params=_params(("parallel",)),
    )(*xs, *ws)


def _groups_outer(xs, y, ws, name):
    n, (m, kk) = len(ws), y.shape
    tk = _pick(m, (768, 512, 256, 128))
    steps = m // tk

    def body(y_ref, *refs):
        x_refs, o_refs, acc_refs = refs[:n], refs[n:2 * n], refs[2 * n:]
        k = pl.program_id(0)
        b = y_ref[...].astype(BF16)
        for x_ref, o_ref, acc_ref in zip(x_refs, o_refs, acc_refs):
            part = lax.dot_general(x_ref[...].astype(BF16), b, _TN, preferred_element_type=F32)

            @pl.when(k == 0)
            def _(acc_ref=acc_ref, part=part):
                acc_ref[...] = part

            @pl.when(k > 0)
            def _(acc_ref=acc_ref, part=part):
                acc_ref[...] += part

            @pl.when(k == steps - 1)
            def _(acc_ref=acc_ref, o_ref=o_ref):
                o_ref[...] = acc_ref[...].astype(o_ref.dtype)

    return _pcall(
        body, name=name, grid=(steps,),
        in_specs=[pl.BlockSpec((tk, kk), lambda k: (k, 0))] + [pl.BlockSpec((tk, w.shape[0]), lambda k: (k, 0)) for w in ws],
        out_specs=[_whole(w) for w in ws], out_shape=[jax.ShapeDtypeStruct(w.shape, w.dtype) for w in ws],
        scratch_shapes=[pltpu.VMEM(w.shape, F32) for w in ws], compiler_params=_params(("arbitrary",)),
    )(y, *xs)


def _linear_multi(name, out_dtypes):
    @jax.custom_vjp
    def op(x, wts):
        return tuple(_groups_apply(x, wts, out_dtypes, name + "_fwd"))

    def fwd(x, wts):
        return op(x, wts), (x, wts)

    def bwd(res, dys):
        x, wts = res
        return _groups_sum(dys, wts, x.dtype, name + "_dx"), tuple(_groups_outer(dys, x, wts, name + "_dw"))

    op.defvjp(fwd, bwd)
    return op


def _linear_sum(name):
    @jax.custom_vjp
    def op(xs, ws):
        return _groups_sum(xs, ws, BF16, name + "_fwd")

    def fwd(xs, ws):
        return op(xs, ws), (xs, ws)

    def bwd(res, dy):
        xs, ws = res
        return tuple(_groups_apply(dy, ws, [x.dtype for x in xs], name + "_dx")), tuple(_groups_outer(xs, dy, ws, name + "_dw"))

    op.defvjp(fwd, bwd)
    return op


def _relu2_epi(acc):
    return jnp.square(jnp.maximum(acc, 0.0)), acc


def _relu2_bwd_epi(acc, a):
    return (acc * (2.0 * jnp.maximum(a.astype(F32), 0.0)),)


def _mlp(name):
    @jax.custom_vjp
    def op(h, w1t, w2):
        s, _ = _mm(h, w1t, "nt", (BF16, BF16), name + "_up", epi=_relu2_epi)
        return _mm(s, w2, "nn", (BF16,), name + "_down")[0]

    def fwd(h, w1t, w2):
        s, a = _mm(h, w1t, "nt", (BF16, BF16), name + "_up", epi=_relu2_epi)
        return _mm(s, w2, "nn", (BF16,), name + "_down")[0], (h, w1t, w2, s, a)

    def bwd(res, dy):
        h, w1t, w2, s, a = res
        da = _mm(dy, w2, "nt", (BF16,), name + "_ds", epi=_relu2_bwd_epi, extras=(a,))[0]
        dw2 = _mm(s, dy, "tn", (w2.dtype,), name + "_dw2")[0]
        dw1t = _mm(da, h, "tn", (w1t.dtype,), name + "_dw1")[0]
        dh = _mm(da, w1t, "nn", (h.dtype,), name + "_dh")[0]
        return dh, dw1t, dw2

    op.defvjp(fwd, bwd)
    return op


def _two_pass_dot(x, m):
    hi = x.astype(BF16)
    lo = (x - hi.astype(F32)).astype(BF16)
    mb = m.astype(BF16)
    return jnp.dot(hi, mb, preferred_element_type=F32) + jnp.dot(lo, mb, preferred_element_type=F32)


@jax.custom_vjp
def _sym_dot(x, m):
    return _two_pass_dot(x, m)


def _sym_dot_fwd(x, m):
    return _two_pass_dot(x, m), m


def _sym_dot_bwd(m, g):
    return _two_pass_dot(g, m), jnp.zeros_like(m)


_sym_dot.defvjp(_sym_dot_fwd, _sym_dot_bwd)


def _neighbour(x):
    lane = lax.broadcasted_iota(jnp.int32, x.shape, 1)
    return jnp.where(lane % 2 == 0, pltpu.roll(x, x.shape[1] - 1, 1), pltpu.roll(x, 1, 1))


@jax.custom_vjp
def _swap_pairs(x):
    return _neighbour(x)


_swap_pairs.defvjp(lambda x: (_neighbour(x), None), lambda _, g: (_neighbour(g),))


def _lane_group(x, k):
    return x[:, k * B_GROUP_DIM:(k + 1) * B_GROUP_DIM]


@jax.custom_vjp
def _group_mix(ws, v):
    return jnp.concatenate([jnp.dot(w.astype(BF16), _lane_group(v, k).astype(BF16), preferred_element_type=F32)
                            for k, w in enumerate(ws)], axis=1)


def _group_mix_fwd(ws, v):
    return _group_mix(ws, v), (ws, v)


def _group_mix_bwd(res, d):
    ws, v = res
    parts = [(_lane_group(d, k).astype(BF16), _lane_group(v, k).astype(BF16)) for k in range(len(ws))]
    dws = tuple(lax.dot_general(dk, vk, (((1,), (1,)), ((), ())), preferred_element_type=F32) for dk, vk in parts)
    dv = jnp.concatenate([lax.dot_general(w.astype(BF16), dk, (((0,), (0,)), ((), ())), preferred_element_type=F32)
                          for w, (dk, _) in zip(ws, parts)], axis=1)
    return dws, dv


_group_mix.defvjp(_group_mix_fwd, _group_mix_bwd)


def _rowwise(name, f, out_specs, tl, ctx_blocks=0):
    def seg(l, s):
        return jnp.where(l >= ctx_blocks, s - 1, 0) if s > 1 else 0

    def specs(rows, tabs, pers, glbs, consts):
        row_specs = [pl.BlockSpec((1, tl, r.shape[2]), lambda b, l: (b, l, 0)) for r in rows]
        tab_specs = [pl.BlockSpec((tl, t.shape[1]), lambda b, l: (l, 0)) for t in tabs]
        per_specs = [pl.BlockSpec((1, 1, 1, p.shape[3]), functools.partial(lambda b, l, s: (b, seg(l, s), 0, 0), s=p.shape[1])) for p in pers]
        glb_specs = [pl.BlockSpec(g.shape, functools.partial(lambda b, l, nd: (0,) * nd, nd=g.ndim)) for g in glbs]
        const_specs = [pl.BlockSpec(c.shape, functools.partial(lambda b, l, nd: (0,) * nd, nd=c.ndim)) for c in consts]
        return row_specs, tab_specs, per_specs, glb_specs, const_specs

    def load(refs_rows, refs_tabs, refs_pers, refs_glbs, refs_consts):
        return (tuple(r[0].astype(F32) for r in refs_rows), tuple(t[...] for t in refs_tabs),
                tuple(p[0, 0].astype(F32) for p in refs_pers), tuple(g[...].astype(F32) for g in refs_glbs),
                tuple(c[...] for c in refs_consts))

    def call_fwd(rows, tabs, pers, glbs, consts):
        bsz, length = rows[0].shape[:2]
        nr, nt, npp, ng, nc = len(rows), len(tabs), len(pers), len(glbs), len(consts)
        rs, ts, ps, gs, cs = specs(rows, tabs, pers, glbs, consts)

        def body(*refs):
            ins, outs = refs[:nr + nt + npp + ng + nc], refs[nr + nt + npp + ng + nc:]
            r, t, p, g, c = load(ins[:nr], ins[nr:nr + nt], ins[nr + nt:nr + nt + npp], ins[nr + nt + npp:nr + nt + npp + ng], ins[nr + nt + npp + ng:])
            for o_ref, o in zip(outs, f(r, t, p, g, c)):
                o_ref[0] = o.astype(o_ref.dtype)

        return _pcall(
            body, name=name + "_fwd", grid=(bsz, length // tl),
            in_specs=rs + ts + ps + gs + cs,
            out_specs=[pl.BlockSpec((1, tl, w), lambda b, l: (b, l, 0)) for w, _ in out_specs],
            out_shape=[jax.ShapeDtypeStruct((bsz, length, w), d) for w, d in out_specs],
            compiler_params=_params(("parallel", "parallel")),
        )(*rows, *tabs, *pers, *glbs, *consts)

    def call_bwd(rows, tabs, pers, glbs, consts, cts):
        bsz, length = rows[0].shape[:2]
        nr, nt, npp, ng, nc, no = len(rows), len(tabs), len(pers), len(glbs), len(consts), len(cts)
        rs, ts, ps, gs, cs = specs(rows, tabs, pers, glbs, consts)
        n_in = nr + nt + npp + ng + nc

        def body(*refs):
            ins, ct_refs, outs = refs[:n_in], refs[n_in:n_in + no], refs[n_in + no:]
            r, t, p, g, c = load(ins[:nr], ins[nr:nr + nt], ins[nr + nt:nr + nt + npp], ins[nr + nt + npp:nr + nt + npp + ng], ins[nr + nt + npp + ng:])
            _, vjp = jax.vjp(lambda r_, p_, g_: tuple(f(r_, t, p_, g_, c)), r, p, g)
            dr, dp, dg = vjp(tuple(ct[0].astype(F32) for ct in ct_refs))
            dr_refs, dp_refs, dg_refs = outs[:nr], outs[nr:nr + npp], outs[nr + npp:]
            for ref, d in zip(dr_refs, dr):
                ref[0] = d.astype(ref.dtype)
            b, l = pl.program_id(0), pl.program_id(1)
            first_of_segment = (l == 0) | (l == ctx_blocks)
            for ref, d in zip(dp_refs, dp):
                @pl.when(first_of_segment)
                def _(ref=ref, d=d):
                    ref[0, 0] = d

                @pl.when(jnp.logical_not(first_of_segment))
                def _(ref=ref, d=d):
                    ref[0, 0] += d
            first = (b == 0) & (l == 0)
            for ref, d in zip(dg_refs, dg):
                @pl.when(first)
                def _(ref=ref, d=d):
                    ref[...] = d

                @pl.when(jnp.logical_not(first))
                def _(ref=ref, d=d):
                    ref[...] += d

        ct_specs = [pl.BlockSpec((1, tl, w), lambda b, l: (b, l, 0)) for w, _ in out_specs]
        outs = _pcall(
            body, name=name + "_bwd", grid=(bsz, length // tl),
            in_specs=rs + ts + ps + gs + cs + ct_specs,
            out_specs=rs + ps + gs,
            out_shape=[jax.ShapeDtypeStruct(r.shape, r.dtype) for r in rows]
            + [jax.ShapeDtypeStruct(p.shape, F32) for p in pers] + [jax.ShapeDtypeStruct(g.shape, F32) for g in glbs],
            compiler_params=_params(("arbitrary", "arbitrary")),
        )(*rows, *tabs, *pers, *glbs, *consts, *cts)
        return tuple(outs[:nr]), tuple(outs[nr:nr + npp]), tuple(outs[nr + npp:])

    @jax.custom_vjp
    def op(rows, tabs, pers, glbs, consts):
        return tuple(call_fwd(rows, tabs, pers, glbs, consts))

    def fwd(rows, tabs, pers, glbs, consts):
        return op(rows, tabs, pers, glbs, consts), (rows, tabs, pers, glbs, consts)

    def bwd(res, cts):
        rows, tabs, pers, glbs, consts = res
        dr, dp, dg = call_bwd(rows, tabs, pers, glbs, consts, tuple(cts))
        dp = tuple(d.astype(p.dtype) for d, p in zip(dp, pers))
        dg = tuple(d.astype(g.dtype) for d, g in zip(dg, glbs))
        return dr, tuple(jnp.zeros_like(t) for t in tabs), dp, dg, tuple(jnp.zeros_like(c) for c in consts)

    op.defvjp(fwd, bwd)
    return op


def _rms(x, g):
    return x * lax.rsqrt(jnp.mean(x * x, axis=-1, keepdims=True) + EPS) * g


def _f_silu(r, t, p, g, c):
    return (jax.nn.silu(r[0]),)


def _f_modulate(r, t, p, g, c):
    shift, scale = p
    return (_rms(r[0], g[0]) * (1.0 + scale) + shift,)


def _f_res_modulate(r, t, p, g, c):
    x, y = r
    gate, shift, scale = p
    xn = x + gate * y
    return xn, _rms(xn, g[0]) * (1.0 + scale) + shift


def _f_res(r, t, p, g, c):
    return (r[0] + p[0] * r[1],)


def _f_headnorm_rope(r, t, p, g, c):
    x = r[0]
    cos, sin = t
    xn = x * lax.rsqrt(_sym_dot(x * x, c[0]) + EPS) * g[0]
    return (xn * cos + _swap_pairs(xn) * sin,)


def _f_rope(r, t, p, g, c):
    x = r[0]
    cos, sin = t
    return (x * cos + _swap_pairs(x) * sin,)


def _f_rms(r, t, p, g, c):
    return (_rms(r[0], g[0]),)


def _f_sgu_pre(r, t, p, g, c):
    u = jax.nn.gelu(r[0])
    v = jax.nn.gelu(r[1])
    vn = v * lax.rsqrt(_sym_dot(v * v, c[0]) + EPS) * g[0]
    return u, vn


def _f_sgu_mix(r, t, p, g, c):
    u, vn = r
    return (u * (g[B_GROUPS] + _group_mix(tuple(g[:B_GROUPS]), vn)),)


def _f_glu(r, t, p, g, c):
    return (r[0] * jax.nn.sigmoid(r[1]),)


def _f_ln_silu(r, t, p, g, c):
    x = r[0]
    mu = jnp.mean(x, axis=-1, keepdims=True)
    var = jnp.mean(jnp.square(x - mu), axis=-1, keepdims=True)
    return (jax.nn.silu((x - mu) * lax.rsqrt(var + EPS) * g[0] + g[1]),)


LOG2_E = 1.4426950408889634
LN_2 = 0.6931471805599453


def _attention_rows(name, ctx_len, heads, kv_heads):
    grp = heads // kv_heads

    def by_segment(qi, cb, lk, run):
        if cb > 0:
            @pl.when(qi < cb)
            def _():
                run(ctx_len)

            @pl.when(qi >= cb)
            def _():
                run(lk)
        else:
            run(lk)

    def head(ref, h, hd):
        return ref[0, :, h * hd:(h + 1) * hd]

    def scores(q_tile, kk, hd):
        qs = (q_tile.astype(F32) * (LOG2_E * hd ** -0.5)).astype(BF16)
        return qs, lax.dot_general(qs, kk, _NT, preferred_element_type=F32)

    def shapes(q):
        bsz, length, width = q.shape
        tq = math.gcd(_pick(length, (256, 128)), ctx_len) if ctx_len else _pick(length, (256, 128))
        return bsz, length, width // heads, tq, ctx_len // tq

    def call_fwd(q, k, v):
        bsz, length, hd, tq, cb = shapes(q)

        def body(q_ref, k_ref, v_ref, o_ref, lse_ref):
            def run(nk):
                k_all, v_all = k_ref[0, :nk], v_ref[0, :nk]
                outs = []
                for j in range(kv_heads):
                    kk = k_all[:, j * hd:(j + 1) * hd]
                    v_ones = jnp.concatenate([v_all[:, j * hd:(j + 1) * hd], jnp.ones((nk, hd), BF16)], axis=1)
                    for h in range(j * grp, (j + 1) * grp):
                        _, s = scores(head(q_ref, h, hd), kk, hd)
                        m = jnp.max(s, axis=-1, keepdims=True)
                        acc = jnp.dot(jnp.exp2(s - m).astype(BF16), v_ones, preferred_element_type=F32)
                        l = acc[:, hd:hd + 1]
                        outs.append((acc[:, :hd] / l).astype(BF16))
                        lse_ref[0, h] = m + jnp.log2(l)
                o_ref[0] = jnp.concatenate(outs, axis=1)

            by_segment(pl.program_id(1), cb, length, run)

        q_spec = pl.BlockSpec((1, tq, heads * hd), lambda b, i: (b, i, 0))
        kv_spec = pl.BlockSpec((1, length, kv_heads * hd), lambda b, i: (b, 0, 0))
        return _pcall(
            body, name=name + "_fwd", grid=(bsz, length // tq), in_specs=[q_spec, kv_spec, kv_spec],
            out_specs=[q_spec, pl.BlockSpec((1, heads, tq, 1), lambda b, i: (b, 0, i, 0))],
            out_shape=[jax.ShapeDtypeStruct(q.shape, BF16), jax.ShapeDtypeStruct((bsz, heads, length, 1), F32)],
            compiler_params=_params(("parallel", "parallel")),
        )(q, k, v)

    def call_bwd(q, k, v, o, lse, do):
        bsz, length, hd, tq, cb = shapes(q)
        nq = length // tq

        def body(q_ref, k_ref, v_ref, o_ref, lse_ref, do_ref, dq_ref, dk_ref, dv_ref, dk_acc, dv_acc):
            qi = pl.program_id(1)

            @pl.when(qi == 0)
            def _():
                dk_acc[...] = jnp.zeros_like(dk_acc)
                dv_acc[...] = jnp.zeros_like(dv_acc)

            def run(nk):
                k_all, v_all = k_ref[0, :nk], v_ref[0, :nk]
                dqs, dks, dvs = [], [], []
                for j in range(kv_heads):
                    kk, vv = k_all[:, j * hd:(j + 1) * hd], v_all[:, j * hd:(j + 1) * hd]
                    dk_sum = dv_sum = None
                    for h in range(j * grp, (j + 1) * grp):
                        dd = head(do_ref, h, hd)
                        qs, s = scores(head(q_ref, h, hd), kk, hd)
                        p = jnp.exp2(s - lse_ref[0, h])
                        delta = jnp.sum(dd.astype(F32) * head(o_ref, h, hd).astype(F32), axis=-1, keepdims=True)
                        t = (p * (lax.dot_general(dd, vv, _NT, preferred_element_type=F32) - delta)).astype(BF16)
                        dqs.append((jnp.dot(t, kk, preferred_element_type=F32) * hd ** -0.5).astype(BF16))
                        dk_h = lax.dot_general(t, qs, _TN, preferred_element_type=F32)
                        dv_h = lax.dot_general(p.astype(BF16), dd, _TN, preferred_element_type=F32)
                        dk_sum = dk_h if dk_sum is None else dk_sum + dk_h
                        dv_sum = dv_h if dv_sum is None else dv_sum + dv_h
                    dks.append(dk_sum)
                    dvs.append(dv_sum)
                dq_ref[0] = jnp.concatenate(dqs, axis=1)
                dk_acc[:nk] += jnp.concatenate(dks, axis=1)
                dv_acc[:nk] += jnp.concatenate(dvs, axis=1)

            by_segment(qi, cb, length, run)

            @pl.when(qi == nq - 1)
            def _():
                dk_ref[0] = (dk_acc[...] * LN_2).astype(dk_ref.dtype)
                dv_ref[0] = dv_acc[...].astype(dv_ref.dtype)

        q_spec = pl.BlockSpec((1, tq, heads * hd), lambda b, i: (b, i, 0))
        kv_spec = pl.BlockSpec((1, length, kv_heads * hd), lambda b, i: (b, 0, 0))
        lse_spec = pl.BlockSpec((1, heads, tq, 1), lambda b, i: (b, 0, i, 0))
        return _pcall(
            body, name=name + "_bwd", grid=(bsz, nq), in_specs=[q_spec, kv_spec, kv_spec, q_spec, lse_spec, q_spec],
            out_specs=[q_spec, kv_spec, kv_spec],
            out_shape=[jax.ShapeDtypeStruct(q.shape, BF16), jax.ShapeDtypeStruct(k.shape, BF16), jax.ShapeDtypeStruct(v.shape, BF16)],
            scratch_shapes=[pltpu.VMEM(k.shape[1:], F32), pltpu.VMEM(v.shape[1:], F32)],
            compiler_params=_params(("parallel", "arbitrary")),
        )(q, k, v, o, lse, do)

    @jax.custom_vjp
    def op(q, k, v):
        return call_fwd(q, k, v)[0]

    def fwd(q, k, v):
        o, lse = call_fwd(q, k, v)
        return o, (q, k, v, o, lse)

    def bwd(res, do):
        q, k, v, o, lse = res
        dq, dk, dv = call_bwd(q, k, v, o, lse, do)
        return dq, dk.astype(k.dtype), dv.astype(v.dtype)

    op.defvjp(fwd, bwd)
    return op


def _attention_latent_rows(name, heads, nope, rope, dv, heads_forward, heads_backward):
    lanes = 128
    scale = (nope + rope) ** -0.5

    def shapes(q, kn):
        bsz, lq, _ = q.shape
        return bsz, lq, kn.shape[1], _pick(lq, (256, 128))

    def scores(q_tile, kk):
        qs = (q_tile.astype(F32) * (LOG2_E * scale)).astype(BF16)
        return qs, lax.dot_general(qs, kk, _NT, preferred_element_type=F32)

    def call_fwd(q, kn, kr, v):
        bsz, lq, lk, tq = shapes(q, kn)
        heads_per_step = heads_forward

        def body(q_ref, kn_ref, kr_ref, v_ref, o_ref, lse_ref):
            real = lax.broadcasted_iota(jnp.int32, (lk, lanes), 1) < dv
            outs = []
            for h in range(heads_per_step):
                kk = kn_ref[0, :, h * lanes:(h + 1) * lanes] + kr_ref[0]
                v_ones = jnp.where(real, v_ref[0, :, h * lanes:(h + 1) * lanes], jnp.ones((lk, lanes), BF16))
                _, s = scores(q_ref[0, :, h * lanes:(h + 1) * lanes], kk)
                m = jnp.max(s, axis=-1, keepdims=True)
                acc = jnp.dot(jnp.exp2(s - m).astype(BF16), v_ones, preferred_element_type=F32)
                l = acc[:, dv:dv + 1]
                outs.append((acc[:, :dv] / l).astype(BF16))
                lse_ref[0, h] = m + jnp.log2(l)
            o_ref[0] = jnp.concatenate(outs, axis=1)

        wide = heads_per_step * lanes
        return _pcall(
            body, name=name + "_fwd", grid=(bsz, heads // heads_per_step, lq // tq),
            in_specs=[pl.BlockSpec((1, tq, wide), lambda b, g, i: (b, i, g)), pl.BlockSpec((1, lk, wide), lambda b, g, i: (b, 0, g)),
                      pl.BlockSpec((1, lk, lanes), lambda b, g, i: (b, 0, 0)), pl.BlockSpec((1, lk, wide), lambda b, g, i: (b, 0, g))],
            out_specs=[pl.BlockSpec((1, tq, heads_per_step * dv), lambda b, g, i: (b, i, g)),
                       pl.BlockSpec((1, heads_per_step, tq, 1), lambda b, g, i: (b, g, i, 0))],
            out_shape=[jax.ShapeDtypeStruct((bsz, lq, heads * dv), BF16), jax.ShapeDtypeStruct((bsz, heads, lq, 1), F32)],
            compiler_params=_params(("parallel", "parallel", "parallel")),
        )(q, kn, kr, v)

    def call_bwd(q, kn, kr, v, o, lse, do):
        bsz, lq, lk, tq = shapes(q, kn)
        heads_per_step = heads_backward

        def body(q_ref, kn_ref, kr_ref, v_ref, o_ref, lse_ref, do_ref, dq_ref, dkn_ref, dkr_ref, dv_ref, dkn_acc, dkr_acc, dv_acc):
            g, qi = pl.program_id(1), pl.program_id(2)
            last_q = qi == lq // tq - 1

            @pl.when(qi == 0)
            def _():
                dkn_acc[...] = jnp.zeros_like(dkn_acc)
                dv_acc[...] = jnp.zeros_like(dv_acc)

            @pl.when((qi == 0) & (g == 0))
            def _():
                dkr_acc[...] = jnp.zeros_like(dkr_acc)

            dqs, dks, dvs = [], [], []
            for h in range(heads_per_step):
                kk = kn_ref[0, :, h * lanes:(h + 1) * lanes] + kr_ref[0]
                vv = v_ref[0, :, h * lanes:h * lanes + dv]
                dd = do_ref[0, :, h * dv:(h + 1) * dv]
                qs, s = scores(q_ref[0, :, h * lanes:(h + 1) * lanes], kk)
                p = jnp.exp2(s - lse_ref[0, h])
                delta = jnp.sum(dd.astype(F32) * o_ref[0, :, h * dv:(h + 1) * dv].astype(F32), axis=-1, keepdims=True)
                t = (p * (lax.dot_general(dd, vv, _NT, preferred_element_type=F32) - delta)).astype(BF16)
                dqs.append((jnp.dot(t, kk, preferred_element_type=F32) * scale).astype(BF16))
                dks.append(lax.dot_general(t, qs, _TN, preferred_element_type=F32) * LN_2)
                dv_h = lax.dot_general(p.astype(BF16), dd, _TN, preferred_element_type=F32)
                dvs.append(jnp.concatenate([dv_h, jnp.zeros((lk, lanes - dv), F32)], axis=1))
            dq_ref[0] = jnp.concatenate(dqs, axis=1)
            dkn_acc[...] += jnp.concatenate(dks, axis=1)
            dv_acc[...] += jnp.concatenate(dvs, axis=1)
            shared = dks[0]
            for d_h in dks[1:]:
                shared = shared + d_h
            dkr_acc[...] += shared

            @pl.when(last_q)
            def _():
                dkn_ref[0] = dkn_acc[...].astype(dkn_ref.dtype)
                dv_ref[0] = dv_acc[...].astype(dv_ref.dtype)

            @pl.when(last_q & (g == heads // heads_per_step - 1))
            def _():
                dkr_ref[0] = dkr_acc[...].astype(dkr_ref.dtype)

        wide = heads_per_step * lanes
        q_spec = pl.BlockSpec((1, tq, wide), lambda b, g, i: (b, i, g))
        k_spec = pl.BlockSpec((1, lk, wide), lambda b, g, i: (b, 0, g))
        kr_spec = pl.BlockSpec((1, lk, lanes), lambda b, g, i: (b, 0, 0))
        o_spec = pl.BlockSpec((1, tq, heads_per_step * dv), lambda b, g, i: (b, i, g))
        lse_spec = pl.BlockSpec((1, heads_per_step, tq, 1), lambda b, g, i: (b, g, i, 0))
        return _pcall(
            body, name=name + "_bwd", grid=(bsz, heads // heads_per_step, lq // tq),
            in_specs=[q_spec, k_spec, kr_spec, k_spec, o_spec, lse_spec, o_spec],
            out_specs=[q_spec, k_spec, kr_spec, k_spec],
            out_shape=[jax.ShapeDtypeStruct(q.shape, BF16), jax.ShapeDtypeStruct(kn.shape, BF16), jax.ShapeDtypeStruct(kr.shape, BF16),
                       jax.ShapeDtypeStruct(v.shape, BF16)],
            scratch_shapes=[pltpu.VMEM((lk, wide), F32), pltpu.VMEM((lk, lanes), F32), pltpu.VMEM((lk, wide), F32)],
            compiler_params=_params(("parallel", "arbitrary", "arbitrary")),
        )(q, kn, kr, v, o, lse, do)

    @jax.custom_vjp
    def op(q, kn, kr, v):
        return call_fwd(q, kn, kr, v)[0]

    def fwd(q, kn, kr, v):
        o, lse = call_fwd(q, kn, kr, v)
        return o, (q, kn, kr, v, o, lse)

    def bwd(res, do):
        q, kn, kr, v, o, lse = res
        dq, dkn, dkr, dv_ = call_bwd(q, kn, kr, v, o, lse, do)
        return dq, dkn.astype(kn.dtype), dkr.astype(kr.dtype), dv_.astype(v.dtype)

    op.defvjp(fwd, bwd)
    return op


def _conv_call(ypad, taps, name):
    bsz, lp, ch = ypad.shape
    length = lp - 2 * ROW_ALIGN
    tl = _pick(length, (256, 128))

    def body(y_ref, w_ref, o_ref):
        base = pl.multiple_of(pl.program_id(1) * tl, tl)
        for r0 in range(0, tl, CONV_ROWS):
            win = y_ref[0, pl.ds(base + r0, CONV_ROWS + 2 * ROW_ALIGN), :]
            acc = jnp.broadcast_to(w_ref[pl.ds(D_CONV, 1), :], (CONV_ROWS, ch))
            for shift in range(SUBLANES):
                shifted = win[shift:shift + CONV_ROWS + 2 * ROW_ALIGN - SUBLANES, :]
                for k in range(shift, D_CONV, SUBLANES):
                    acc = acc + shifted[k - shift:k - shift + CONV_ROWS, :] * w_ref[pl.ds(k, 1), :]
            o_ref[0, pl.ds(r0, CONV_ROWS), :] = acc

    return _pcall(
        body, name=name, grid=(bsz, length // tl),
        in_specs=[pl.BlockSpec((1, lp, ch), lambda b, l: (b, 0, 0)), pl.BlockSpec((D_CONV + 1, ch), lambda b, l: (0, 0))],
        out_specs=pl.BlockSpec((1, tl, ch), lambda b, l: (b, l, 0)),
        out_shape=jax.ShapeDtypeStruct((bsz, length, ch), F32),
        compiler_params=_params(("parallel", "parallel")),
    )(ypad, taps)


def _conv_dw_call(ypad, dout, name):
    bsz, lp, ch = ypad.shape
    length = lp - 2 * ROW_ALIGN
    tl = _pick(length, (256, 128))

    def body(y_ref, d_ref, o_ref):
        b, l = pl.program_id(0), pl.program_id(1)

        @pl.when((b == 0) & (l == 0))
        def _():
            o_ref[...] = jnp.zeros_like(o_ref)

        base = pl.multiple_of(l * tl, tl)
        win = y_ref[0, pl.ds(base, tl + 2 * ROW_ALIGN), :]
        dd = d_ref[0]
        for shift in range(SUBLANES):
            shifted = win[shift:shift + tl + 2 * ROW_ALIGN - SUBLANES, :]
            for k in range(shift, D_CONV, SUBLANES):
                o_ref[pl.ds(k, 1), :] += jnp.sum(shifted[k - shift:k - shift + tl, :] * dd, axis=0, keepdims=True)
        o_ref[pl.ds(D_CONV, 1), :] += jnp.sum(dd, axis=0, keepdims=True)

    return _pcall(
        body, name=name, grid=(bsz, length // tl),
        in_specs=[pl.BlockSpec((1, lp, ch), lambda b, l: (b, 0, 0)), pl.BlockSpec((1, tl, ch), lambda b, l: (b, l, 0))],
        out_specs=pl.BlockSpec((D_CONV + 1, ch), lambda b, l: (0, 0)),
        out_shape=jax.ShapeDtypeStruct((D_CONV + 1, ch), F32),
        compiler_params=_params(("arbitrary", "arbitrary")),
    )(ypad, dout)


def _pad_rows(y):
    return jnp.pad(y, ((0, 0), (CONV_PAD, 2 * ROW_ALIGN - CONV_PAD), (0, 0)))


@jax.custom_vjp
def _dwconv(y, taps):
    return _conv_call(_pad_rows(y), taps, "conv_fwd")


def _dwconv_fwd(y, taps):
    return _dwconv(y, taps), (y, taps)


def _dwconv_bwd(res, dout):
    y, taps = res
    flipped = jnp.concatenate([taps[:D_CONV][::-1], jnp.zeros_like(taps[D_CONV:])], axis=0)
    dy = _conv_call(_pad_rows(dout), flipped, "conv_dy")
    dtaps = _conv_dw_call(_pad_rows(y), dout, "conv_dw")
    return dy, dtaps


_dwconv.defvjp(_dwconv_fwd, _dwconv_bwd)


def _loss_head(x, target, g):
    bsz, length, d = x.shape
    tl = _pick(length, (256, 128))

    def f(xb, tb, gb):
        err = _rms(xb, gb) - tb
        return 0.5 * jnp.sum(jnp.sum(err * err, axis=-1, keepdims=True), axis=0, keepdims=True) / d

    def body(x_ref, t_ref, g_ref, loss_ref, dx_ref, dg_ref):
        val, vjp = jax.vjp(lambda xb, gb: f(xb, t_ref[0], gb), x_ref[0], g_ref[...])
        dx, dg = vjp(jnp.ones((1, 1), F32))
        dx_ref[0] = dx
        first = (pl.program_id(0) == 0) & (pl.program_id(1) == 0)

        @pl.when(first)
        def _():
            loss_ref[...] = val
            dg_ref[...] = dg

        @pl.when(jnp.logical_not(first))
        def _():
            loss_ref[...] += val
            dg_ref[...] += dg

    row = pl.BlockSpec((1, tl, d), lambda b, l: (b, l, 0))
    return _pcall(
        body, name="loss_head", grid=(bsz, length // tl),
        in_specs=[row, row, pl.BlockSpec((1, d), lambda b, l: (0, 0))],
        out_specs=[pl.BlockSpec((1, 1), lambda b, l: (0, 0)), row, pl.BlockSpec((1, d), lambda b, l: (0, 0))],
        out_shape=[jax.ShapeDtypeStruct((1, 1), F32), jax.ShapeDtypeStruct(x.shape, F32), jax.ShapeDtypeStruct((1, d), F32)],
        compiler_params=_params(("arbitrary", "arbitrary")),
    )(x, target, g)


def _adamw(w, g, m, v, name):
    shape = w.shape
    cols = shape[-1]
    rows = w.size // cols
    tr = _pick(rows, (512, 256, 128))
    w2, g2, m2, v2 = (t.reshape(rows, cols) for t in (w, g, m, v))

    def body(w_ref, g_ref, m_ref, v_ref, d_ref, nm_ref, nv_ref):
        gg = g_ref[...]
        nm = ADAM_B1 * m_ref[...] + (1.0 - ADAM_B1) * gg
        nv = ADAM_B2 * v_ref[...] + (1.0 - ADAM_B2) * jnp.square(gg)
        m_hat = nm / (1.0 - ADAM_B1 ** ADAM_STEP)
        v_hat = nv / (1.0 - ADAM_B2 ** ADAM_STEP)
        d_ref[...] = -ADAM_LR * (m_hat / (jnp.sqrt(v_hat) + ADAM_EPS) + ADAM_WD * w_ref[...])
        nm_ref[...] = nm
        nv_ref[...] = nv

    spec = pl.BlockSpec((tr, cols), lambda i: (i, 0))
    outs = _pcall(
        body, name=name, grid=(rows // tr,), in_specs=[spec] * 4, out_specs=[spec] * 3,
        out_shape=[jax.ShapeDtypeStruct((rows, cols), F32)] * 3,
        compiler_params=_params(("parallel",)),
    )(w2, g2, m2, v2)
    return tuple(o.reshape(shape) for o in outs)


def _mesh_pos():
    return lax.axis_index("x"), lax.axis_index("y"), lax.axis_index("c")


_RELATIONS = [(dx, dy, dc) for dx in (0, 1) for dy in (0, 1) for dc in (0, 1)][1:]


def _peer(pos, rel):
    return tuple(jnp.where(r == 1, 1 - p, p) if r else p for p, r in zip(pos, rel))


def _block_index(pos):
    return 4 * pos[0] + 2 * pos[1] + pos[2]


_HBM = pl.BlockSpec(memory_space=pltpu.HBM)


def _all_gather(xs, name):
    n = len(xs)

    def body(*refs):
        x_refs, out_refs, (send_sems, recv_sems, local_sems) = refs[:n], refs[n:2 * n], refs[2 * n:]
        x_, y_, c_ = _mesh_pos()
        me, sibling = (x_, y_, c_), (x_, y_, 1 - c_)
        chips = [(1 - x_, y_), (x_, 1 - y_), (1 - x_, 1 - y_)]

        def copy(t, k, block, to, own=False):
            slot = out_refs[t].at[_block_index(block)]
            return pltpu.make_async_remote_copy(
                src_ref=x_refs[t] if own else slot, dst_ref=slot, send_sem=send_sems.at[7 * t + k], recv_sem=recv_sems.at[7 * t + k],
                device_id=to, device_id_type=pl.DeviceIdType.MESH)

        mine = [pltpu.make_async_copy(x_refs[t], out_refs[t].at[_block_index(me)], local_sems.at[t]) for t in range(n)]
        first = [[copy(t, 0, me, sibling, own=True)] + [copy(t, 1 + j, me, (*chip, c_), own=True) for j, chip in enumerate(chips)]
                 for t in range(n)]
        passed = [[copy(t, 4 + j, (*chip, c_), sibling) for j, chip in enumerate(chips)] for t in range(n)]
        for t in range(n):
            mine[t].start()
            for cp in first[t]:
                cp.start()
        for t in range(n):
            for j, chip in enumerate(chips):
                copy(t, 1 + j, (*chip, c_), me).wait_recv()
                passed[t][j].start()
        for t in range(n):
            copy(t, 0, sibling, me).wait_recv()
            for j, chip in enumerate(chips):
                copy(t, 4 + j, (*chip, 1 - c_), me).wait_recv()
            for cp in first[t] + passed[t]:
                cp.wait_send()
            mine[t].wait()

    return _pcall(
        body, name=name, in_specs=[_HBM] * n, out_specs=[_HBM] * n,
        out_shape=[jax.ShapeDtypeStruct((N_DEV,) + x.shape, x.dtype) for x in xs],
        scratch_shapes=[pltpu.SemaphoreType.DMA((7 * n,)), pltpu.SemaphoreType.DMA((7 * n,)), pltpu.SemaphoreType.DMA((n,))],
    )(*xs)


_SEM = pl.BlockSpec(memory_space=pltpu.SEMAPHORE)
_EFFECT = pltpu.SideEffectType.DATAFLOW_SIDE_EFFECTING


def _push_start(srcs, after, name):
    n = len(srcs)
    lands = [lax.empty((N_DEV,) + s.shape[-2:], s.dtype) for s in srcs]

    def body(*refs):
        src_refs, land_refs = refs[:n], refs[n:2 * n]
        send_sems, recv_sems, token = refs[2 * n + 1:3 * n + 1], refs[3 * n + 1:4 * n + 1], refs[-1]
        me = _mesh_pos()
        for t in range(n):
            for rel in _RELATIONS:
                peer = _peer(me, rel)
                pltpu.make_async_remote_copy(
                    src_ref=src_refs[t].at[_block_index(peer)] if srcs[t].ndim == 3 else src_refs[t], dst_ref=land_refs[t].at[_block_index(me)],
                    send_sem=send_sems[t], recv_sem=recv_sems[t], device_id=peer, device_id_type=pl.DeviceIdType.MESH).start()
        token[...] = jnp.zeros_like(token)

    outs = _pcall(
        body, name=name,
        out_shape=[pltpu.SemaphoreType.DMA(())] * (2 * n) + [pltpu.HBM(s.shape, s.dtype) for s in srcs]
        + [pltpu.HBM(l.shape, l.dtype) for l in lands] + [jax.ShapeDtypeStruct((8, 128), F32)],
        in_specs=[_HBM] * (2 * n) + [pl.BlockSpec(memory_space=pl.ANY)],
        out_specs=[_SEM] * (2 * n) + [_HBM] * (2 * n) + [pl.BlockSpec(memory_space=pltpu.VMEM)],
        input_output_aliases={i: 2 * n + i for i in range(2 * n)}, compiler_params=pltpu.CompilerParams(has_side_effects=_EFFECT),
    )(*[pltpu.with_memory_space_constraint(t, pltpu.HBM) for t in list(srcs) + lands], after)
    return (outs[:n], outs[n:2 * n], outs[2 * n:3 * n], outs[3 * n:4 * n]), outs[-1]


def _push_wait(handle, after, owns, me, name):
    send_sems, recv_sems, src_thrus, land_thrus = handle
    n = len(land_thrus)

    def body(*refs):
        land_refs, sends, recvs = refs[n:2 * n], refs[2 * n:3 * n], refs[3 * n:4 * n]
        for t in range(n):
            seven = land_refs[t].at[pl.ds(0, N_DEV - 1)]
            all_seven = pltpu.make_async_remote_copy(src_ref=seven, dst_ref=seven, send_sem=sends[t], recv_sem=recvs[t],
                                                     device_id=_mesh_pos(), device_id_type=pl.DeviceIdType.MESH)
            all_seven.wait_send()
            all_seven.wait_recv()

    outs = _pcall(
        body, name=name,
        out_shape=[pltpu.HBM(t.shape, t.dtype) for t in list(src_thrus) + list(land_thrus)],
        in_specs=[_HBM] * (2 * n) + [_SEM] * (2 * n) + [pl.BlockSpec(memory_space=pl.ANY)], out_specs=[_HBM] * (2 * n),
        input_output_aliases={i: i for i in range(2 * n)}, compiler_params=pltpu.CompilerParams(has_side_effects=_EFFECT),
    )(*src_thrus, *land_thrus, *send_sems, *recv_sems, after)
    return [lax.dynamic_update_slice(land, own[None], (me, 0, 0)) for land, own in zip(outs[n:], owns)]


def _sum_blocks(p, name):
    n, rows, cols = p.shape
    tr = _pick(rows, (256, 128, 64, 32, 16, 8))

    def body(p_ref, o_ref):
        acc = p_ref[0].astype(F32)
        for s in range(1, n):
            acc = acc + p_ref[s].astype(F32)
        o_ref[...] = acc

    return _pcall(
        body, name=name, grid=(rows // tr,),
        in_specs=[pl.BlockSpec((n, tr, cols), lambda i: (0, i, 0))], out_specs=pl.BlockSpec((tr, cols), lambda i: (i, 0)),
        out_shape=jax.ShapeDtypeStruct((rows, cols), F32), compiler_params=_params(("parallel",)),
    )(p)


def _sum_rows(t, name):
    def body(t_ref, o_ref):
        o_ref[...] = jnp.sum(t_ref[...], axis=0, keepdims=True)

    return _pcall(body, name=name, out_shape=jax.ShapeDtypeStruct((1, t.shape[1]), F32))(t)


class _Packing:
    def __init__(self, sizes, align):
        self.offsets, self.sizes, self.align = {}, dict(sizes), align
        row = 0
        for name, size in sizes:
            self.offsets[name] = row
            row += -(-size // (align * PACK_COLS)) * align
        self.rows = row

    def pack(self, pieces):
        return self.pack_blocks({n: pieces[n].reshape(1, -1) for n in self.sizes})[0]

    def pack_blocks(self, pieces):
        out = []
        for n, size in self.sizes.items():
            padded = -(-size // (self.align * PACK_COLS)) * self.align * PACK_COLS
            out.append(jnp.pad(pieces[n], ((0, 0), (0, padded - size))).reshape(pieces[n].shape[0], -1, PACK_COLS))
        return jnp.concatenate(out, axis=1)

    def piece(self, packed, name, lead=()):
        start, size = self.offsets[name], self.sizes[name]
        nrow = -(-size // (self.align * PACK_COLS)) * self.align
        sl = packed[..., start:start + nrow, :]
        return sl.reshape(lead + (nrow * PACK_COLS,))[..., :size]


_PIECES = (("ev_w_in", "ev_w_in", 0, 1, "a"), ("w_out0", "w_out", 0, 0, "a"),
           ("mlp_w1_0", "mlp_w1", 0, 1, "b"), ("mlp_w2_0", "mlp_w2", 0, 0, "b"),
           ("od_w_in", "od_w_in", 0, 1, "c"), ("od_w_uq", "od_w_uq", 0, 1, "c"), ("od_w_ukv", "od_w_ukv", 0, 1, "c"),
           ("w_out1", "w_out", 1, 0, "c"), ("mlp_w1_1", "mlp_w1", 1, 1, "c"), ("mlp_w2_1", "mlp_w2", 1, 0, "c"))
_SMALL_SHARDED = (("od_q_norm_g", 1), ("od_conv_w", 2), ("od_conv_b", 1), ("od_ln_g", 1), ("od_ln_b", 1))
_REPLICATED = ("c_ctx", "norm1_g", "norm2_g", "ev_q_norm_g", "ev_k_norm_g", "ev_sgu_norm_g", "ev_sgu_w", "ev_sgu_b",
               "od_kv_norm_g", "final_g")


def _unshard(blocks, axis):
    moved = jnp.moveaxis(blocks, 0, axis)
    shape = moved.shape
    return moved.reshape(shape[:axis] + (shape[axis] * shape[axis + 1],) + shape[axis + 2:])


def _group_mean_matrix(width, group):
    idx = jnp.arange(width) // group
    return (idx[:, None] == idx[None, :]).astype(F32) / group


def _angles(length, d_rot):
    rows = length // GRID_W
    row = jnp.broadcast_to(jnp.arange(rows)[:, None], (rows, GRID_W)).reshape(-1).astype(F32)
    col = jnp.broadcast_to(jnp.arange(GRID_W)[None, :], (rows, GRID_W)).reshape(-1).astype(F32)
    d_axis = d_rot // 2
    inv = ROPE_THETA ** (-jnp.arange(0, d_axis, 2, dtype=F32) / d_axis)
    return jnp.concatenate([row[:, None] * inv, col[:, None] * inv], axis=-1)


def _rope_tables(length, d_rot, head_dim, heads, ctx_len, tail=0):
    ang = _angles(length, d_rot)
    cos = jnp.repeat(jnp.cos(ang), 2, axis=1)
    sin = jnp.repeat(jnp.sin(ang), 2, axis=1) * jnp.tile(jnp.array([-1.0, 1.0], F32), d_rot // 2)
    keep = head_dim - d_rot - tail
    cos = jnp.concatenate([jnp.ones((length, keep), F32), cos, jnp.ones((length, tail), F32)], axis=1)
    sin = jnp.concatenate([jnp.zeros((length, keep), F32), sin, jnp.zeros((length, tail), F32)], axis=1)
    cos, sin = jnp.tile(cos, (1, heads)), jnp.tile(sin, (1, heads))
    cos = jnp.concatenate([jnp.ones((ctx_len, cos.shape[1]), F32), cos], axis=0)
    sin = jnp.concatenate([jnp.zeros((ctx_len, sin.shape[1]), F32), sin], axis=0)
    return cos, sin


def _segment_params(mod, bsz):
    parts = jnp.split(mod, N_MOD, axis=-1)
    out = []
    for part in parts:
        lat = part[:bsz]
        ctx = jnp.broadcast_to(part[bsz:bsz + 1], lat.shape)
        out.append(jnp.stack([ctx, lat], axis=1)[:, :, None, :])
    return out


def _flat(t):
    return t.reshape(-1, t.shape[-1])


def _sequence_rowwise(ctx_len):
    tl = math.gcd(256, ctx_len)

    def make(name, f, out_specs, rows_per_block=tl, ctx_blocks=ctx_len // tl):
        return _rowwise(name, f, out_specs, rows_per_block, ctx_blocks)

    return make


def _mixer0(xall, modrows0, w, bsz, length, ctx_len):
    d = xall.shape[-1]
    total = ctx_len + length
    rowwise, flat = _sequence_rowwise(ctx_len), _flat
    sh1, sc1, g1, sh2, sc2, _ = _segment_params(modrows0, bsz)
    (h,) = rowwise("mod0", _f_modulate, [(d, BF16)])((xall,), (), (sh1, sc1), (w["norm1_g0"],), ())
    ev_q, ev_kv = A_Q_HEADS * A_HEAD_DIM, A_KV_HEADS * A_HEAD_DIM
    half = B_GROUPS * B_GROUP_DIM
    groups = tuple(jnp.split(w["ev_w_in"], [ev_q, ev_q + ev_kv, ev_q + 2 * ev_kv, ev_q + 2 * ev_kv + half], axis=0))
    qp, kp, vp, zu, zv = [t.reshape(bsz, total, -1) for t in _linear_multi("ev_in", (BF16,) * 5)(flat(h), groups)]
    cos_q, sin_q = _rope_tables(length, A_HEAD_DIM, A_HEAD_DIM, A_Q_HEADS, ctx_len)
    cos_k, sin_k = cos_q[:, :ev_kv], sin_q[:, :ev_kv]
    (q,) = rowwise("ev_q", _f_headnorm_rope, [(ev_q, BF16)])(
        (qp,), (cos_q, sin_q), (), (jnp.tile(w["ev_q_norm_g"][0], A_Q_HEADS)[None],), (_group_mean_matrix(ev_q, A_HEAD_DIM),))
    (k,) = rowwise("ev_k", _f_headnorm_rope, [(ev_kv, BF16)])(
        (kp,), (cos_k, sin_k), (), (jnp.tile(w["ev_k_norm_g"][0], A_KV_HEADS)[None],), (_group_mean_matrix(ev_kv, A_HEAD_DIM),))
    o_att = _attention_rows("gqa", ctx_len, A_Q_HEADS, A_KV_HEADS)(q, k, vp)
    u, vn = rowwise("sgu_pre", _f_sgu_pre, [(half, F32), (half, BF16)])(
        (zu, zv), (), (), (w["ev_sgu_norm_g"][0].reshape(1, half),), (_group_mean_matrix(half, B_GROUP_DIM),))
    bias = jnp.repeat(w["ev_sgu_b"][0].T, B_GROUP_DIM, axis=1)
    (o_sgu,) = rowwise("sgu_mix", _f_sgu_mix, [(half, BF16)], rows_per_block=B_CHUNK, ctx_blocks=0)(
        (u, vn), (), (), tuple(w["ev_sgu_w"][0][g] for g in range(B_GROUPS)) + (bias,), ())
    y = _linear_sum("out0")((flat(o_att), flat(o_sgu)), tuple(jnp.split(w["w_out0"], 2, axis=0))).reshape(bsz, total, d)
    x1, h = rowwise("res_mod0a", _f_res_modulate, [(d, F32), (d, BF16)])((xall, y), (), (g1, sh2, sc2), (w["norm2_g0"],), ())
    return x1, h


def _mlp0(x1, h, modrows0, modrows1, w, bsz, length, ctx_len):
    d = x1.shape[-1]
    total = ctx_len + length
    g2 = _segment_params(modrows0, bsz)[5]
    sh1, sc1 = _segment_params(modrows1, bsz)[:2]
    y = _mlp("mlp0")(_flat(h), w["mlp_w1_0"], w["mlp_w2_0"]).reshape(bsz, total, d)
    return _sequence_rowwise(ctx_len)("res_mod0b", _f_res_modulate, [(d, F32), (d, BF16)])((x1, y), (), (g2, sh1, sc1), (w["norm1_g1"],), ())


def _layer1(x2, h, modrows1, w, bsz, length, ctx_len):
    d = x2.shape[-1]
    total = ctx_len + length
    half = B_GROUPS * B_GROUP_DIM
    rowwise, flat = _sequence_rowwise(ctx_len), _flat
    _, _, g1n, sh2n, sc2n, g2n = _segment_params(modrows1, bsz)
    g_cq, g_ckv, g_kr, g_za, g_zg = jnp.split(w["od_w_in"], [C_Q_RANK, C_Q_RANK + C_KV_RANK, C_Q_RANK + C_KV_RANK + C_ROPE,
                                                             C_Q_RANK + C_KV_RANK + C_ROPE + half], axis=0)
    lanes, c_qk = 128, C_NOPE + C_ROPE
    g_kr = jnp.pad(g_kr, ((C_NOPE, lanes - c_qk), (0, 0)))
    cq, ckv, kr, za, zg = [t.reshape(bsz, total, -1) for t in _linear_multi("od_in", (BF16,) * 5)(flat(h), (g_cq, g_ckv, g_kr, g_za, g_zg))]
    lat = slice(ctx_len, total)
    lat_tl = math.gcd(256, length)
    (cqn,) = _rowwise("od_qn", _f_rms, [(C_Q_RANK, BF16)], lat_tl)((cq[:, lat],), (), (), (w["od_q_norm_g"],), ())
    w_uq = jnp.pad(w["od_w_uq"].reshape(C_HEADS, c_qk, C_Q_RANK), ((0, 0), (0, lanes - c_qk), (0, 0))).reshape(C_HEADS * lanes, C_Q_RANK)
    qf = _linear("od_uq", BF16)(flat(cqn), w_uq).reshape(bsz, length, C_HEADS * lanes)
    cos_q, sin_q = _rope_tables(length, C_ROPE, lanes, C_HEADS, 0, tail=lanes - c_qk)
    (q,) = _rowwise("od_qrope", _f_rope, [(C_HEADS * lanes, BF16)], lat_tl)((qf,), (cos_q, sin_q), (), (), ())
    (ckvn,) = rowwise("od_kvn", _f_rms, [(C_KV_RANK, BF16)])((ckv,), (), (), (w["od_kv_norm_g"],), ())
    per_head = w["od_w_ukv"].reshape(C_HEADS, C_NOPE + C_V, C_KV_RANK)
    w_kn = jnp.pad(per_head[:, :C_NOPE], ((0, 0), (0, lanes - C_NOPE), (0, 0))).reshape(C_HEADS * lanes, C_KV_RANK)
    w_v = jnp.pad(per_head[:, C_NOPE:], ((0, 0), (0, lanes - C_V), (0, 0))).reshape(C_HEADS * lanes, C_KV_RANK)
    kn, vv = [t.reshape(bsz, total, -1) for t in _linear_multi("od_ukv", (BF16, BF16))(flat(ckvn), (w_kn, w_v))]
    cos_r, sin_r = _rope_tables(length, C_ROPE, lanes, 1, ctx_len, tail=lanes - c_qk)
    (krr,) = rowwise("od_krope", _f_rope, [(lanes, BF16)])((kr,), (cos_r, sin_r), (), (), ())
    o_att = _attention_latent_rows("mla", C_HEADS, C_NOPE, C_ROPE, C_V, 4, 2)(q, kn, krr, vv)
    (glu,) = _rowwise("glu", _f_glu, [(half, F32)], lat_tl)((za[:, lat], zg[:, lat]), (), (), (), ())
    taps = jnp.concatenate([w["od_conv_w"][0], w["od_conv_b"]], axis=0)
    conv = _dwconv(glu, taps)
    (o_conv,) = _rowwise("ln_silu", _f_ln_silu, [(half, BF16)], lat_tl)((conv,), (), (), (w["od_ln_g"], w["od_ln_b"]), ())
    y = _linear_sum("out1")((flat(o_att), flat(o_conv)), tuple(jnp.split(w["w_out1"], 2, axis=0))).reshape(bsz, length, d)
    lat_param = lambda p: p[:, 1:]
    x3, h = _rowwise("res_mod1a", _f_res_modulate, [(d, F32), (d, BF16)], lat_tl)(
        (x2[:, lat], y), (), (lat_param(g1n), lat_param(sh2n), lat_param(sc2n)), (w["norm2_g1"],), ())
    y = _mlp("mlp1")(flat(h), w["mlp_w1_1"], w["mlp_w2_1"]).reshape(bsz, length, d)
    (x4,) = _rowwise("res1b", _f_res, [(d, F32)], lat_tl)((x3, y), (), (lat_param(g2n),), (), ())
    return x4


def kernel(x, c, ctx, c_ctx, ada_w, ada_b, norm1_g, norm2_g, w_out, mlp_w1, mlp_w2, ev_w_in, ev_q_norm_g, ev_k_norm_g, ev_sgu_norm_g, ev_sgu_w, ev_sgu_b, od_w_in, od_q_norm_g, od_kv_norm_g, od_w_uq, od_w_ukv, od_conv_w, od_conv_b, od_ln_g, od_ln_b, final_g, loss_target, m_c_ctx, m_ada_w, m_ada_b, m_norm1_g, m_norm2_g, m_w_out, m_mlp_w1, m_mlp_w2, m_ev_w_in, m_ev_q_norm_g, m_ev_k_norm_g, m_ev_sgu_norm_g, m_ev_sgu_w, m_ev_sgu_b, m_od_w_in, m_od_q_norm_g, m_od_kv_norm_g, m_od_w_uq, m_od_w_ukv, m_od_conv_w, m_od_conv_b, m_od_ln_g, m_od_ln_b, m_final_g, v_c_ctx, v_ada_w, v_ada_b, v_norm1_g, v_norm2_g, v_w_out, v_mlp_w1, v_mlp_w2, v_ev_w_in, v_ev_q_norm_g, v_ev_k_norm_g, v_ev_sgu_norm_g, v_ev_sgu_w, v_ev_sgu_b, v_od_w_in, v_od_q_norm_g, v_od_kv_norm_g, v_od_w_uq, v_od_w_ukv, v_od_conv_w, v_od_conv_b, v_od_ln_g, v_od_ln_b, v_final_g):
    names = ["c_ctx", "ada_w", "ada_b", "norm1_g", "norm2_g", "w_out", "mlp_w1", "mlp_w2", "ev_w_in", "ev_q_norm_g", "ev_k_norm_g",
             "ev_sgu_norm_g", "ev_sgu_w", "ev_sgu_b", "od_w_in", "od_q_norm_g", "od_kv_norm_g", "od_w_uq", "od_w_ukv", "od_conv_w",
             "od_conv_b", "od_ln_g", "od_ln_b", "final_g"]
    local = dict(zip(names, [c_ctx, ada_w, ada_b, norm1_g, norm2_g, w_out, mlp_w1, mlp_w2, ev_w_in, ev_q_norm_g, ev_k_norm_g, ev_sgu_norm_g, ev_sgu_w, ev_sgu_b, od_w_in, od_q_norm_g, od_kv_norm_g, od_w_uq, od_w_ukv, od_conv_w, od_conv_b, od_ln_g, od_ln_b, final_g]))
    mom1 = dict(zip(names, [m_c_ctx, m_ada_w, m_ada_b, m_norm1_g, m_norm2_g, m_w_out, m_mlp_w1, m_mlp_w2, m_ev_w_in, m_ev_q_norm_g, m_ev_k_norm_g, m_ev_sgu_norm_g, m_ev_sgu_w, m_ev_sgu_b, m_od_w_in, m_od_q_norm_g, m_od_kv_norm_g, m_od_w_uq, m_od_w_ukv, m_od_conv_w, m_od_conv_b, m_od_ln_g, m_od_ln_b, m_final_g]))
    mom2 = dict(zip(names, [v_c_ctx, v_ada_w, v_ada_b, v_norm1_g, v_norm2_g, v_w_out, v_mlp_w1, v_mlp_w2, v_ev_w_in, v_ev_q_norm_g, v_ev_k_norm_g, v_ev_sgu_norm_g, v_ev_sgu_w, v_ev_sgu_b, v_od_w_in, v_od_q_norm_g, v_od_kv_norm_g, v_od_w_uq, v_od_w_ukv, v_od_conv_w, v_od_conv_b, v_od_ln_g, v_od_ln_b, v_final_g]))
    bsz, length, d = x.shape
    ctx_len = ctx.shape[1]
    me = _block_index(_mesh_pos())

    shard = {p: local[wn][layer] for p, wn, layer, _, _ in _PIECES}
    by_columns = {p: axis == 1 for p, _, _, axis, _ in _PIECES}
    stages = {s: [p for p, _, _, _, st in _PIECES if st == s] for s in "abc"}
    block_rows = {p: shard[p].shape[1] if by_columns[p] else shard[p].shape[0] for p in shard}

    def pad_block_rows(t, p):
        extra = -block_rows[p] % ROW_ALIGN
        return jnp.pad(t, [(0, 0)] * (t.ndim - 2) + [(0, extra), (0, 0)]) if extra else t

    def travelling(p):
        t = shard[p].astype(BF16)
        return pad_block_rows(t.T if by_columns[p] else t, p)

    mine = {s: [travelling(p) for p in stages[s]] for s in "abc"}
    tiny_pack = _Packing([(n, local[n].size) for n, _ in _SMALL_SHARDED], 8)

    def unpack_weights(s, gathered):
        return {p: g[:, :block_rows[p]].reshape(N_DEV * block_rows[p], g.shape[2]) for p, g in zip(stages[s], gathered)}

    def pack_grads(s, g):
        return [pad_block_rows(g[p].reshape(N_DEV, block_rows[p], g[p].shape[1]), p) for p in stages[s]]

    cond_local = jnp.concatenate([c, c_ctx[None], jnp.zeros((COND_ROWS - bsz - 1, d), F32)], axis=0)
    cond, gathered_tiny, *gathered_a = _all_gather(
        [cond_local, tiny_pack.pack({n: local[n] for n, _ in _SMALL_SHARDED})] + mine["a"], "gather_inputs")
    cond = cond.reshape(N_DEV * COND_ROWS, d)
    silu_op = _rowwise("silu", _f_silu, [(d, F32)], N_DEV * COND_ROWS)
    silu_rows, silu_pullback = jax.vjp(lambda r: silu_op((r[None],), (), (), (), ())[0][0], cond)
    mod_cols = ada_w.shape[2]
    mod_part = jnp.concatenate([_mm(silu_rows, ada_w[i], "nn", (F32,), f"ada{i}_fwd")[0] for i in range(2)], axis=0)
    (mod_all,) = _all_gather([mod_part], "gather_mod")
    mod_all = mod_all.reshape(N_DEV, 2, N_DEV * COND_ROWS, mod_cols)
    modrows = []
    for i in range(2):
        whole = mod_all[:, i].transpose(1, 0, 2).reshape(N_DEV * COND_ROWS, N_DEV * mod_cols) + ada_b[i]
        modrows.append(lax.dynamic_slice_in_dim(whole, me * COND_ROWS, COND_ROWS, axis=0)[:bsz + 1])

    weights_a = unpack_weights("a", gathered_a)
    gather_b, token_b = _push_start(mine["b"], mod_all, "gather_weights_b_start")
    gather_c, token_c = _push_start(mine["c"], token_b, "gather_weights_c_start")
    full = {n: local[n] for n in _REPLICATED}
    for n, axis in _SMALL_SHARDED:
        full[n] = _unshard(tiny_pack.piece(gathered_tiny, n, (N_DEV,)).reshape((N_DEV,) + local[n].shape), axis)

    xall = jnp.concatenate([ctx, x], axis=1)
    modrows0, modrows1 = modrows[0] + token_c[0, 0], modrows[1]
    w_a = dict(weights_a, norm1_g0=norm1_g[0][None], norm2_g0=norm2_g[0][None],
               **{n: full[n] for n in ("ev_q_norm_g", "ev_k_norm_g", "ev_sgu_norm_g", "ev_sgu_w", "ev_sgu_b")})
    (x1, h0), pull_a = jax.vjp(lambda x_, m0, w: _mixer0(x_, m0, w, bsz, length, ctx_len), xall, modrows0, w_a)
    w_b = dict(unpack_weights("b", _push_wait(gather_b, x1, mine["b"], me, "gather_weights_b_wait")), norm1_g1=norm1_g[1][None])
    (x2, h1), pull_b = jax.vjp(lambda x_, h_, m0, m1, w: _mlp0(x_, h_, m0, m1, w, bsz, length, ctx_len), x1, h0, modrows0, modrows1, w_b)
    w_c = dict(unpack_weights("c", _push_wait(gather_c, x2, mine["c"], me, "gather_weights_c_wait")), norm2_g1=norm2_g[1][None],
               **{n: full[n] for n in ("od_q_norm_g", "od_kv_norm_g", "od_conv_w", "od_conv_b", "od_ln_g", "od_ln_b")})
    x4, pull_c = jax.vjp(lambda x_, h_, m1, w: _layer1(x_, h_, m1, w, bsz, length, ctx_len), x2, h1, modrows1, w_c)
    loss_part, dx4, dfinal = _loss_head(x4, loss_target, final_g[None])
    loss = lax.psum(loss_part[0, 0], MESH_AXES)

    dx2, dh1, dmod1_c, g_c = pull_c(dx4)
    grads_c = pack_grads("c", g_c)
    exchange_c, token = _push_start(grads_c, dx2, "exchange_grads_c_start")
    dx1, dh0, dmod0_b, dmod1_b, g_b = pull_b((dx2, dh1 + token[0, 0].astype(dh1.dtype)))
    grads_b = pack_grads("b", g_b)
    exchange_b, token = _push_start(grads_b, dx1, "exchange_grads_b_start")
    dxall, dmod0_a, g_a = pull_a((dx1, dh0 + token[0, 0].astype(dh0.dtype)))
    grad_x = dxall[:, ctx_len:]
    dmodrows = [dmod0_a + dmod0_b, dmod1_b + dmod1_c]
    grads = {n: g[n] for g in (g_a, g_c) for n in g if n in full}
    grads["norm1_g"] = jnp.concatenate([g_a["norm1_g0"], g_b["norm1_g1"]], axis=0)
    grads["norm2_g"] = jnp.concatenate([g_a["norm2_g0"], g_c["norm2_g1"]], axis=0)
    grads["final_g"] = dfinal[0]

    dmod_local = jnp.concatenate([jnp.pad(dm, ((0, COND_ROWS - bsz - 1), (0, 0))) for dm in dmodrows], axis=0)
    (dmod_all,) = _all_gather([dmod_local], "gather_dmod")
    dmod_all = dmod_all.reshape(N_DEV, 2, COND_ROWS, N_DEV * mod_cols)
    grads_a = pack_grads("a", g_a)
    exchange_a, token = _push_start(grads_a, dmod_all, "exchange_grads_a_start")
    dmod_all = dmod_all + token[0, 0]
    reduced = {}
    grad_ada_w, grad_ada_b, dmod_mine = [], [], []
    for i in range(2):
        dmod = dmod_all[:, i].reshape(N_DEV * COND_ROWS, N_DEV * mod_cols)
        grad_ada_b.append(_sum_rows(dmod, f"ada{i}_db")[0])
        dmod_mine.append(lax.dynamic_slice_in_dim(dmod, me * mod_cols, mod_cols, axis=1))
        grad_ada_w.append(_mm(silu_rows, dmod_mine[i], "tn", (F32,), f"ada{i}_dw")[0])
    reduced["ada_w"], reduced["ada_b"] = jnp.stack(grad_ada_w), jnp.stack(grad_ada_b)
    dsilu = _mm(jnp.concatenate(dmod_mine, axis=1), jnp.concatenate([ada_w[0], ada_w[1]], axis=1), "nt", (F32,), "ada_dx")[0]
    (dcond,) = silu_pullback(dsilu)
    grads["c_ctx"] = _sum_rows(dcond.reshape(N_DEV, COND_ROWS, d)[:, bsz], "c_ctx_rows")[0]

    small_names = list(_REPLICATED) + [n for n, _ in _SMALL_SHARDED]
    small_pack = _Packing([(n, full[n].size) for n in small_names], 8)
    (small_all,) = _all_gather([small_pack.pack({n: grads[n].astype(F32) for n in small_names})], "gather_small_grads")
    small_sum = _sum_blocks(small_all, "sum_small_grads")
    for n in _REPLICATED:
        reduced[n] = small_pack.piece(small_sum, n).reshape(local[n].shape)
    for n, axis in _SMALL_SHARDED:
        whole = small_pack.piece(small_sum, n).reshape(full[n].shape)
        reduced[n] = lax.dynamic_slice_in_dim(whole, me * local[n].shape[axis], local[n].shape[axis], axis=axis)

    def own_blocks(blocks):
        return [lax.dynamic_index_in_dim(t, me, 0, keepdims=False) for t in blocks]

    received = {s: _push_wait(handle, small_sum, own_blocks(blocks), me, f"exchange_grads_{s}_wait")
                for s, handle, blocks in (("c", exchange_c, grads_c), ("b", exchange_b, grads_b), ("a", exchange_a, grads_a))}
    piece_grad = {}
    for s in "abc":
        for p, blocks in zip(stages[s], received[s]):
            summed = _sum_blocks(blocks, "sum_grads_" + p)[:block_rows[p]]
            piece_grad[p] = summed.T if by_columns[p] else summed
    for n in ("w_out", "mlp_w1", "mlp_w2"):
        reduced[n] = jnp.stack([piece_grad[p] for p, wn, _, _, _ in _PIECES if wn == n])
    for n in ("ev_w_in", "od_w_in", "od_w_uq", "od_w_ukv"):
        reduced[n] = piece_grad[n][None]

    delta, new_m, new_v = {}, {}, {}
    for n in names:
        delta[n], new_m[n], new_v[n] = _adamw(local[n], reduced[n], mom1[n], mom2[n], "adamw_" + n)
    return (loss, grad_x, *[reduced[n] for n in names], *[delta[n] for n in names], *[new_m[n] for n in names], *[new_v[n] for n in names])
```
